```python
import math
import jax, jax.numpy as jnp
from jax import lax
import numpy as np

D_MODEL = 1024
BATCH = 16
SEQ = 2048
DEPTH = 4

N_MIXERS = 2
N_CONV_LAYERS = (DEPTH + 1) // 2
N_SSM_LAYERS = DEPTH // 2
CONV_WIDTH = 3
SSM_GROUP = 16
SSM_GROUPS = D_MODEL // SSM_GROUP
SSM_STATE = 64
SSM_CHUNK = 128
DT_MIN = 1e-3
DT_MAX = 1e-1
EIG_CLIP = -1e-4
D_FF = 2816
MEM_LEN = 256
XA_HEADS = 4
XA_HEAD_DIM = D_MODEL // XA_HEADS
NORM_EPS = 1e-6
N_NORMS = 5

kernel_name = "hybrid_conv_s5_macaron_decoder"


def rmsnorm(x, g):
    xf = x.astype(jnp.float32)
    y = xf * lax.rsqrt(jnp.mean(xf * xf, axis=-1, keepdims=True) + NORM_EPS)
    return (y * g.astype(jnp.float32)).astype(x.dtype)


def swiglu_ffn(h, w_up, w_down):
    gate, up = jnp.split(h @ w_up, 2, axis=-1)
    return (jax.nn.silu(gate) * up) @ w_down


def short_conv_mixer(h, w_in, conv_w, w_out):
    c_gate, b_gate, v = jnp.split(h @ w_in, 3, axis=-1)
    u = c_gate * v
    conv = lax.conv_general_dilated(
        u, conv_w[:, None, :],
        window_strides=(1,),
        padding=[(CONV_WIDTH - 1, 0)],
        dimension_numbers=("NWC", "WIO", "NWC"),
        feature_group_count=D_MODEL)
    return (b_gate * conv) @ w_out


def _complex_scan_op(e1, e2):
    a1r, a1i, b1r, b1i = e1
    a2r, a2i, b2r, b2i = e2
    return (a2r * a1r - a2i * a1i,
            a2r * a1i + a2i * a1r,
            a2r * b1r - a2i * b1i + b2r,
            a2r * b1i + a2i * b1r + b2i)


def s5_mixer(h, a_re, a_im, log_dt, b_re, b_im, c_re, c_im, d_skip, w_glu):
    bsz, seq, _ = h.shape
    f32 = jnp.float32
    lam_re = jnp.minimum(a_re.astype(f32), EIG_CLIP)
    lam_im = a_im.astype(f32)
    dt = jnp.exp(log_dt.astype(f32))[:, None]
    mag = jnp.exp(lam_re * dt)
    abar_re = mag * jnp.cos(lam_im * dt)
    abar_im = mag * jnp.sin(lam_im * dt)
    den = lam_re * lam_re + lam_im * lam_im
    num_re = abar_re - 1.0
    num_im = abar_im
    coef_re = (num_re * lam_re + num_im * lam_im) / den
    coef_im = (num_im * lam_re - num_re * lam_im) / den
    br = b_re.astype(f32)
    bi = b_im.astype(f32)
    bbar_re = coef_re[..., None] * br - coef_im[..., None] * bi
    bbar_im = coef_re[..., None] * bi + coef_im[..., None] * br
    cr = c_re.astype(f32)
    ci = c_im.astype(f32)

    n_chunks = seq // SSM_CHUNK
    u = h.astype(f32).reshape(bsz, n_chunks, SSM_CHUNK, SSM_GROUPS, SSM_GROUP)
    u = jnp.moveaxis(u, 1, 0)
    el_shape = (bsz, SSM_CHUNK, SSM_GROUPS, SSM_STATE)
    a_el_re = jnp.broadcast_to(abar_re, el_shape)
    a_el_im = jnp.broadcast_to(abar_im, el_shape)

    def chunk_step(carry, u_c):
        h_re, h_im = carry
        bu_re = jnp.einsum("btgh,gph->btgp", u_c, bbar_re)
        bu_im = jnp.einsum("btgh,gph->btgp", u_c, bbar_im)
        acc_re, acc_im, loc_re, loc_im = lax.associative_scan(
            _complex_scan_op, (a_el_re, a_el_im, bu_re, bu_im), axis=1)
        s_re = loc_re + acc_re * h_re[:, None] - acc_im * h_im[:, None]
        s_im = loc_im + acc_re * h_im[:, None] + acc_im * h_re[:, None]
        y_c = (jnp.einsum("btgp,ghp->btgh", s_re, cr)
               - jnp.einsum("btgp,ghp->btgh", s_im, ci))
        return (s_re[:, -1], s_im[:, -1]), y_c

    h0 = (jnp.zeros((bsz, SSM_GROUPS, SSM_STATE), f32),
          jnp.zeros((bsz, SSM_GROUPS, SSM_STATE), f32))
    _, y = lax.scan(chunk_step, h0, u)
    y = jnp.moveaxis(y, 0, 1).reshape(bsz, seq, D_MODEL)
    y = y + d_skip.astype(f32) * h.astype(f32)
    z = jax.nn.gelu(y).astype(h.dtype)
    val, gate = jnp.split(z @ w_glu, 2, axis=-1)
    return val * jax.nn.sigmoid(gate)


def memory_cross_attention(h, mem_n, w_q, w_kv, w_o):
    bsz, seq, _ = h.shape
    mlen = mem_n.shape[1]
    q = (h @ w_q).reshape(bsz, seq, XA_HEADS, XA_HEAD_DIM)
    k, v = jnp.split(mem_n @ w_kv, 2, axis=-1)
    k = k.reshape(bsz, mlen, XA_HEADS, XA_HEAD_DIM)
    v = v.reshape(bsz, mlen, XA_HEADS, XA_HEAD_DIM)
    s = jnp.einsum("bqhd,bkhd->bhqk", q, k).astype(jnp.float32) * (XA_HEAD_DIM ** -0.5)
    p = jax.nn.softmax(s, axis=-1).astype(v.dtype)
    o = jnp.einsum("bhqk,bkhd->bqhd", p, v).reshape(bsz, seq, D_MODEL)
    return o @ w_o


def _fwd_setup_inputs(seed: int = 0) -> dict:
    key = jax.random.key(seed)
    ks = jax.random.split(key, 24)
    D = D_MODEL
    nrm = jax.random.normal

    def w(k, shape, fan_in):
        return nrm(k, shape, jnp.float32) * (fan_in ** -0.5)

    x = nrm(ks[0], (BATCH, SEQ, D), jnp.float32)
    mem = nrm(ks[1], (BATCH, MEM_LEN, D), jnp.float32)
    norm_g = 1.0 + 0.02 * nrm(ks[2], (DEPTH, N_NORMS, D), jnp.float32)
    final_g = 1.0 + 0.02 * nrm(ks[3], (D,), jnp.float32)
    ffn1_up = w(ks[4], (DEPTH, D, 2 * D_FF), D)
    ffn1_down = w(ks[5], (DEPTH, D_FF, D), D_FF)
    ffn2_up = w(ks[6], (DEPTH, D, 2 * D_FF), D)
    ffn2_down = w(ks[7], (DEPTH, D_FF, D), D_FF)
    conv_w_in = w(ks[8], (N_CONV_LAYERS, D, 3 * D), D)
    conv_w = w(ks[9], (N_CONV_LAYERS, CONV_WIDTH, D), CONV_WIDTH)
    conv_w_out = w(ks[10], (N_CONV_LAYERS, D, D), D)
    ssm_shape = (N_SSM_LAYERS, SSM_GROUPS, SSM_STATE)
    ssm_a_re = -0.5 + 0.01 * nrm(ks[11], ssm_shape, jnp.float32)
    ssm_a_im = (math.pi * jnp.arange(SSM_STATE, dtype=jnp.float32)[None, None, :]
                + 0.01 * nrm(ks[12], ssm_shape, jnp.float32))
    ssm_log_dt = jax.random.uniform(ks[13], (N_SSM_LAYERS, SSM_GROUPS), jnp.float32,
                                    math.log(DT_MIN), math.log(DT_MAX))
    ssm_b_re = w(ks[14], (N_SSM_LAYERS, SSM_GROUPS, SSM_STATE, SSM_GROUP), 2 * SSM_GROUP)
    ssm_b_im = w(ks[15], (N_SSM_LAYERS, SSM_GROUPS, SSM_STATE, SSM_GROUP), 2 * SSM_GROUP)
    ssm_c_re = w(ks[16], (N_SSM_LAYERS, SSM_GROUPS, SSM_GROUP, SSM_STATE), SSM_STATE)
    ssm_c_im = w(ks[17], (N_SSM_LAYERS, SSM_GROUPS, SSM_GROUP, SSM_STATE), SSM_STATE)
    ssm_d = nrm(ks[18], (N_SSM_LAYERS, D), jnp.float32)
    ssm_w_glu = w(ks[19], (N_SSM_LAYERS, D, 2 * D), D)
    xa_w_q = w(ks[20], (DEPTH, D, D), D)
    xa_w_kv = w(ks[21], (DEPTH, D, 2 * D), D)
    xa_w_o = w(ks[22], (DEPTH, D, D), D)
    return {"x": x, "mem": mem, "norm_g": norm_g, "final_g": final_g,
            "ffn1_up": ffn1_up, "ffn1_down": ffn1_down,
            "ffn2_up": ffn2_up, "ffn2_down": ffn2_down,
            "conv_w_in": conv_w_in, "conv_w": conv_w, "conv_w_out": conv_w_out,
            "ssm_a_re": ssm_a_re, "ssm_a_im": ssm_a_im, "ssm_log_dt": ssm_log_dt,
            "ssm_b_re": ssm_b_re, "ssm_b_im": ssm_b_im,
            "ssm_c_re": ssm_c_re, "ssm_c_im": ssm_c_im,
            "ssm_d": ssm_d, "ssm_w_glu": ssm_w_glu,
            "xa_w_q": xa_w_q, "xa_w_kv": xa_w_kv, "xa_w_o": xa_w_o}


def _fwd_reference(x, mem, norm_g, final_g, ffn1_up, ffn1_down, ffn2_up, ffn2_down,
              conv_w_in, conv_w, conv_w_out,
              ssm_a_re, ssm_a_im, ssm_log_dt, ssm_b_re, ssm_b_im,
              ssm_c_re, ssm_c_im, ssm_d, ssm_w_glu,
              xa_w_q, xa_w_kv, xa_w_o):
    for i in range(DEPTH):
        g = norm_g[i]
        x = x + 0.5 * swiglu_ffn(rmsnorm(x, g[0]), ffn1_up[i], ffn1_down[i])
        h = rmsnorm(x, g[1])
        j = i // N_MIXERS
        if i % N_MIXERS == 0:
            x = x + short_conv_mixer(h, conv_w_in[j], conv_w[j], conv_w_out[j])
        else:
            x = x + s5_mixer(h, ssm_a_re[j], ssm_a_im[j], ssm_log_dt[j],
                             ssm_b_re[j], ssm_b_im[j], ssm_c_re[j], ssm_c_im[j],
                             ssm_d[j], ssm_w_glu[j])
        x = x + memory_cross_attention(rmsnorm(x, g[2]), rmsnorm(mem, g[3]),
                                       xa_w_q[i], xa_w_kv[i], xa_w_o[i])
        x = x + 0.5 * swiglu_ffn(rmsnorm(x, g[4]), ffn2_up[i], ffn2_down[i])
    return rmsnorm(x, final_g)


import jax as _jax
import jax.numpy as _jnp

TWIN_FORMAT = 'train_step'
FWD_PARAMS = ['x', 'mem', 'norm_g', 'final_g', 'ffn1_up', 'ffn1_down', 'ffn2_up', 'ffn2_down', 'conv_w_in', 'conv_w', 'conv_w_out', 'ssm_a_re', 'ssm_a_im', 'ssm_log_dt', 'ssm_b_re', 'ssm_b_im', 'ssm_c_re', 'ssm_c_im', 'ssm_d', 'ssm_w_glu', 'xa_w_q', 'xa_w_kv', 'xa_w_o']
TWIN_WEIGHTS = ['norm_g', 'final_g', 'ffn1_up', 'ffn1_down', 'ffn2_up', 'ffn2_down', 'conv_w_in', 'conv_w', 'conv_w_out', 'ssm_a_re', 'ssm_a_im', 'ssm_log_dt', 'ssm_b_re', 'ssm_b_im', 'ssm_c_re', 'ssm_c_im', 'ssm_d', 'ssm_w_glu', 'xa_w_q', 'xa_w_kv', 'xa_w_o']
TWIN_DIFF_INPUT = 'x'
TWIN_INPUTS = ['x', 'mem', 'norm_g', 'final_g', 'ffn1_up', 'ffn1_down', 'ffn2_up', 'ffn2_down', 'conv_w_in', 'conv_w', 'conv_w_out', 'ssm_a_re', 'ssm_a_im', 'ssm_log_dt', 'ssm_b_re', 'ssm_b_im', 'ssm_c_re', 'ssm_c_im', 'ssm_d', 'ssm_w_glu', 'xa_w_q', 'xa_w_kv', 'xa_w_o', 'loss_target', 'm_norm_g', 'm_final_g', 'm_ffn1_up', 'm_ffn1_down', 'm_ffn2_up', 'm_ffn2_down', 'm_conv_w_in', 'm_conv_w', 'm_conv_w_out', 'm_ssm_a_re', 'm_ssm_a_im', 'm_ssm_log_dt', 'm_ssm_b_re', 'm_ssm_b_im', 'm_ssm_c_re', 'm_ssm_c_im', 'm_ssm_d', 'm_ssm_w_glu', 'm_xa_w_q', 'm_xa_w_kv', 'm_xa_w_o', 'v_norm_g', 'v_final_g', 'v_ffn1_up', 'v_ffn1_down', 'v_ffn2_up', 'v_ffn2_down', 'v_conv_w_in', 'v_conv_w', 'v_conv_w_out', 'v_ssm_a_re', 'v_ssm_a_im', 'v_ssm_log_dt', 'v_ssm_b_re', 'v_ssm_b_im', 'v_ssm_c_re', 'v_ssm_c_im', 'v_ssm_d', 'v_ssm_w_glu', 'v_xa_w_q', 'v_xa_w_kv', 'v_xa_w_o']
TWIN_OUTPUTS = ['loss', 'grad_x', 'grad_norm_g', 'grad_final_g', 'grad_ffn1_up', 'grad_ffn1_down', 'grad_ffn2_up', 'grad_ffn2_down', 'grad_conv_w_in', 'grad_conv_w', 'grad_conv_w_out', 'grad_ssm_a_re', 'grad_ssm_a_im', 'grad_ssm_log_dt', 'grad_ssm_b_re', 'grad_ssm_b_im', 'grad_ssm_c_re', 'grad_ssm_c_im', 'grad_ssm_d', 'grad_ssm_w_glu', 'grad_xa_w_q', 'grad_xa_w_kv', 'grad_xa_w_o', 'delta_norm_g', 'delta_final_g', 'delta_ffn1_up', 'delta_ffn1_down', 'delta_ffn2_up', 'delta_ffn2_down', 'delta_conv_w_in', 'delta_conv_w', 'delta_conv_w_out', 'delta_ssm_a_re', 'delta_ssm_a_im', 'delta_ssm_log_dt', 'delta_ssm_b_re', 'delta_ssm_b_im', 'delta_ssm_c_re', 'delta_ssm_c_im', 'delta_ssm_d', 'delta_ssm_w_glu', 'delta_xa_w_q', 'delta_xa_w_kv', 'delta_xa_w_o', 'new_m_norm_g', 'new_m_final_g', 'new_m_ffn1_up', 'new_m_ffn1_down', 'new_m_ffn2_up', 'new_m_ffn2_down', 'new_m_conv_w_in', 'new_m_conv_w', 'new_m_conv_w_out', 'new_m_ssm_a_re', 'new_m_ssm_a_im', 'new_m_ssm_log_dt', 'new_m_ssm_b_re', 'new_m_ssm_b_im', 'new_m_ssm_c_re', 'new_m_ssm_c_im', 'new_m_ssm_d', 'new_m_ssm_w_glu', 'new_m_xa_w_q', 'new_m_xa_w_kv', 'new_m_xa_w_o', 'new_v_norm_g', 'new_v_final_g', 'new_v_ffn1_up', 'new_v_ffn1_down', 'new_v_ffn2_up', 'new_v_ffn2_down', 'new_v_conv_w_in', 'new_v_conv_w', 'new_v_conv_w_out', 'new_v_ssm_a_re', 'new_v_ssm_a_im', 'new_v_ssm_log_dt', 'new_v_ssm_b_re', 'new_v_ssm_b_im', 'new_v_ssm_c_re', 'new_v_ssm_c_im', 'new_v_ssm_d', 'new_v_ssm_w_glu', 'new_v_xa_w_q', 'new_v_xa_w_kv', 'new_v_xa_w_o']
TWIN_LEAF_KINDS = {'loss': 'loss', 'grad_x': 'grad_x', 'grad_norm_g': 'grad_w', 'grad_final_g': 'grad_w', 'grad_ffn1_up': 'grad_w', 'grad_ffn1_down': 'grad_w', 'grad_ffn2_up': 'grad_w', 'grad_ffn2_down': 'grad_w', 'grad_conv_w_in': 'grad_w', 'grad_conv_w': 'grad_w', 'grad_conv_w_out': 'grad_w', 'grad_ssm_a_re': 'grad_w', 'grad_ssm_a_im': 'grad_w', 'grad_ssm_log_dt': 'grad_w', 'grad_ssm_b_re': 'grad_w', 'grad_ssm_b_im': 'grad_w', 'grad_ssm_c_re': 'grad_w', 'grad_ssm_c_im': 'grad_w', 'grad_ssm_d': 'grad_w', 'grad_ssm_w_glu': 'grad_w', 'grad_xa_w_q': 'grad_w', 'grad_xa_w_kv': 'grad_w', 'grad_xa_w_o': 'grad_w', 'delta_norm_g': 'delta_w', 'delta_final_g': 'delta_w', 'delta_ffn1_up': 'delta_w', 'delta_ffn1_down': 'delta_w', 'delta_ffn2_up': 'delta_w', 'delta_ffn2_down': 'delta_w', 'delta_conv_w_in': 'delta_w', 'delta_conv_w': 'delta_w', 'delta_conv_w_out': 'delta_w', 'delta_ssm_a_re': 'delta_w', 'delta_ssm_a_im': 'delta_w', 'delta_ssm_log_dt': 'delta_w', 'delta_ssm_b_re': 'delta_w', 'delta_ssm_b_im': 'delta_w', 'delta_ssm_c_re': 'delta_w', 'delta_ssm_c_im': 'delta_w', 'delta_ssm_d': 'delta_w', 'delta_ssm_w_glu': 'delta_w', 'delta_xa_w_q': 'delta_w', 'delta_xa_w_kv': 'delta_w', 'delta_xa_w_o': 'delta_w', 'new_m_norm_g': 'new_m', 'new_m_final_g': 'new_m', 'new_m_ffn1_up': 'new_m', 'new_m_ffn1_down': 'new_m', 'new_m_ffn2_up': 'new_m', 'new_m_ffn2_down': 'new_m', 'new_m_conv_w_in': 'new_m', 'new_m_conv_w': 'new_m', 'new_m_conv_w_out': 'new_m', 'new_m_ssm_a_re': 'new_m', 'new_m_ssm_a_im': 'new_m', 'new_m_ssm_log_dt': 'new_m', 'new_m_ssm_b_re': 'new_m', 'new_m_ssm_b_im': 'new_m', 'new_m_ssm_c_re': 'new_m', 'new_m_ssm_c_im': 'new_m', 'new_m_ssm_d': 'new_m', 'new_m_ssm_w_glu': 'new_m', 'new_m_xa_w_q': 'new_m', 'new_m_xa_w_kv': 'new_m', 'new_m_xa_w_o': 'new_m', 'new_v_norm_g': 'new_v', 'new_v_final_g': 'new_v', 'new_v_ffn1_up': 'new_v', 'new_v_ffn1_down': 'new_v', 'new_v_ffn2_up': 'new_v', 'new_v_ffn2_down': 'new_v', 'new_v_conv_w_in': 'new_v', 'new_v_conv_w': 'new_v', 'new_v_conv_w_out': 'new_v', 'new_v_ssm_a_re': 'new_v', 'new_v_ssm_a_im': 'new_v', 'new_v_ssm_log_dt': 'new_v', 'new_v_ssm_b_re': 'new_v', 'new_v_ssm_b_im': 'new_v', 'new_v_ssm_c_re': 'new_v', 'new_v_ssm_c_im': 'new_v', 'new_v_ssm_d': 'new_v', 'new_v_ssm_w_glu': 'new_v', 'new_v_xa_w_q': 'new_v', 'new_v_xa_w_kv': 'new_v', 'new_v_xa_w_o': 'new_v'}


def _forward(args):
    return _fwd_reference(*[args[k] for k in FWD_PARAMS])


def _output_shape():
    out = _jax.eval_shape(lambda: _forward(_fwd_setup_inputs(0)))
    return out.shape, out.dtype

N_MICROBATCH = 1
ADAM_LR = 0.001
ADAM_B1 = 0.9
ADAM_B2 = 0.999
ADAM_EPS = 1e-08
ADAM_WD = 0.01
ADAM_STEP = 10
PER_EXAMPLE_BATCH_AXIS = {'x': 0, 'mem': 0, 'loss_target': 0}
SHARED_INPUTS = []
_WEIGHT_DTYPES = {'norm_g': _jnp.float32, 'final_g': _jnp.float32, 'ffn1_up': _jnp.float32, 'ffn1_down': _jnp.float32, 'ffn2_up': _jnp.float32, 'ffn2_down': _jnp.float32, 'conv_w_in': _jnp.float32, 'conv_w': _jnp.float32, 'conv_w_out': _jnp.float32, 'ssm_a_re': _jnp.float32, 'ssm_a_im': _jnp.float32, 'ssm_log_dt': _jnp.float32, 'ssm_b_re': _jnp.float32, 'ssm_b_im': _jnp.float32, 'ssm_c_re': _jnp.float32, 'ssm_c_im': _jnp.float32, 'ssm_d': _jnp.float32, 'ssm_w_glu': _jnp.float32, 'xa_w_q': _jnp.float32, 'xa_w_kv': _jnp.float32, 'xa_w_o': _jnp.float32}
MOMENT_SCALE = {'norm_g': 9.456215e-02, 'final_g': 3.195705e+01, 'ffn1_up': 3.726376e-02, 'ffn1_down': 6.095635e-02, 'ffn2_up': 2.349207e-02, 'ffn2_down': 3.830562e-02, 'conv_w_in': 1.404098e-01, 'conv_w': 1.391645e-01, 'conv_w_out': 1.398242e-01, 'ssm_a_re': 3.601450e-03, 'ssm_a_im': 4.033082e-03, 'ssm_log_dt': 2.025521e+00, 'ssm_b_re': 2.306105e-03, 'ssm_b_im': 2.313130e-03, 'ssm_c_re': 3.306282e-03, 'ssm_c_im': 3.293815e-03, 'ssm_d': 5.036752e-02, 'ssm_w_glu': 3.333942e-02, 'xa_w_q': 1.390277e-02, 'xa_w_kv': 1.412562e-02, 'xa_w_o': 1.421437e-02}


def _to_microbatches(a, axis):
    t = _jnp.moveaxis(a, axis, 0)
    t = t.reshape((N_MICROBATCH, t.shape[0] // N_MICROBATCH) + t.shape[1:])
    return _jnp.moveaxis(t, 1, axis + 1)


def setup_inputs(seed: int = 0) -> dict:
    inp = _fwd_setup_inputs(seed)
    key = _jax.random.fold_in(_jax.random.key(seed), 7919)
    shape, _ = _output_shape()
    out = dict(inp)
    out["loss_target"] = _jax.random.normal(_jax.random.fold_in(key, 0), shape, _jnp.float32)
    for i, name in enumerate(TWIN_WEIGHTS):
        w = inp[name].astype(_jnp.float32)
        if MOMENT_SCALE is None:
            s = _jnp.sqrt(_jnp.mean(_jnp.square(w)) + 1e-30)
        else:
            s = MOMENT_SCALE[name]
        km, kv = _jax.random.split(_jax.random.fold_in(key, i + 1))
        out[name] = w
        out["m_" + name] = s * _jax.random.normal(km, w.shape, _jnp.float32)
        out["v_" + name] = (s * s) * _jax.random.uniform(kv, w.shape, _jnp.float32, 0.5, 1.5)
    if N_MICROBATCH > 1:
        for name, axis in PER_EXAMPLE_BATCH_AXIS.items():
            out[name] = _to_microbatches(out[name], axis)
    return {'x': out['x'], 'mem': out['mem'], 'norm_g': out['norm_g'], 'final_g': out['final_g'], 'ffn1_up': out['ffn1_up'], 'ffn1_down': out['ffn1_down'], 'ffn2_up': out['ffn2_up'], 'ffn2_down': out['ffn2_down'], 'conv_w_in': out['conv_w_in'], 'conv_w': out['conv_w'], 'conv_w_out': out['conv_w_out'], 'ssm_a_re': out['ssm_a_re'], 'ssm_a_im': out['ssm_a_im'], 'ssm_log_dt': out['ssm_log_dt'], 'ssm_b_re': out['ssm_b_re'], 'ssm_b_im': out['ssm_b_im'], 'ssm_c_re': out['ssm_c_re'], 'ssm_c_im': out['ssm_c_im'], 'ssm_d': out['ssm_d'], 'ssm_w_glu': out['ssm_w_glu'], 'xa_w_q': out['xa_w_q'], 'xa_w_kv': out['xa_w_kv'], 'xa_w_o': out['xa_w_o'], 'loss_target': out['loss_target'], 'm_norm_g': out['m_norm_g'], 'm_final_g': out['m_final_g'], 'm_ffn1_up': out['m_ffn1_up'], 'm_ffn1_down': out['m_ffn1_down'], 'm_ffn2_up': out['m_ffn2_up'], 'm_ffn2_down': out['m_ffn2_down'], 'm_conv_w_in': out['m_conv_w_in'], 'm_conv_w': out['m_conv_w'], 'm_conv_w_out': out['m_conv_w_out'], 'm_ssm_a_re': out['m_ssm_a_re'], 'm_ssm_a_im': out['m_ssm_a_im'], 'm_ssm_log_dt': out['m_ssm_log_dt'], 'm_ssm_b_re': out['m_ssm_b_re'], 'm_ssm_b_im': out['m_ssm_b_im'], 'm_ssm_c_re': out['m_ssm_c_re'], 'm_ssm_c_im': out['m_ssm_c_im'], 'm_ssm_d': out['m_ssm_d'], 'm_ssm_w_glu': out['m_ssm_w_glu'], 'm_xa_w_q': out['m_xa_w_q'], 'm_xa_w_kv': out['m_xa_w_kv'], 'm_xa_w_o': out['m_xa_w_o'], 'v_norm_g': out['v_norm_g'], 'v_final_g': out['v_final_g'], 'v_ffn1_up': out['v_ffn1_up'], 'v_ffn1_down': out['v_ffn1_down'], 'v_ffn2_up': out['v_ffn2_up'], 'v_ffn2_down': out['v_ffn2_down'], 'v_conv_w_in': out['v_conv_w_in'], 'v_conv_w': out['v_conv_w'], 'v_conv_w_out': out['v_conv_w_out'], 'v_ssm_a_re': out['v_ssm_a_re'], 'v_ssm_a_im': out['v_ssm_a_im'], 'v_ssm_log_dt': out['v_ssm_log_dt'], 'v_ssm_b_re': out['v_ssm_b_re'], 'v_ssm_b_im': out['v_ssm_b_im'], 'v_ssm_c_re': out['v_ssm_c_re'], 'v_ssm_c_im': out['v_ssm_c_im'], 'v_ssm_d': out['v_ssm_d'], 'v_ssm_w_glu': out['v_ssm_w_glu'], 'v_xa_w_q': out['v_xa_w_q'], 'v_xa_w_kv': out['v_xa_w_kv'], 'v_xa_w_o': out['v_xa_w_o']}


def _loss(weights, diff, rest, loss_target):
    with _jax.named_scope("forward"):
        args = {**rest, TWIN_DIFF_INPUT: diff, **{k: w.astype(_WEIGHT_DTYPES[k]) for k, w in weights.items()}}
        y = _forward(args)
    with _jax.named_scope("loss_head"):
        err = _jnp.square(y.astype(_jnp.float32) - loss_target)
        return 0.5 * _jnp.sum(_jnp.mean(err, axis=-1)) if err.ndim else 0.5 * err


def _adamw(w, g, m, v):
    m = ADAM_B1 * m + (1.0 - ADAM_B1) * g
    v = ADAM_B2 * v + (1.0 - ADAM_B2) * _jnp.square(g)
    m_hat = m / (1.0 - ADAM_B1 ** ADAM_STEP)
    v_hat = v / (1.0 - ADAM_B2 ** ADAM_STEP)
    delta = -ADAM_LR * (m_hat / (_jnp.sqrt(v_hat) + ADAM_EPS) + ADAM_WD * w)
    return delta, m, v


def reference(x, mem, norm_g, final_g, ffn1_up, ffn1_down, ffn2_up, ffn2_down, conv_w_in, conv_w, conv_w_out, ssm_a_re, ssm_a_im, ssm_log_dt, ssm_b_re, ssm_b_im, ssm_c_re, ssm_c_im, ssm_d, ssm_w_glu, xa_w_q, xa_w_kv, xa_w_o, loss_target, m_norm_g, m_final_g, m_ffn1_up, m_ffn1_down, m_ffn2_up, m_ffn2_down, m_conv_w_in, m_conv_w, m_conv_w_out, m_ssm_a_re, m_ssm_a_im, m_ssm_log_dt, m_ssm_b_re, m_ssm_b_im, m_ssm_c_re, m_ssm_c_im, m_ssm_d, m_ssm_w_glu, m_xa_w_q, m_xa_w_kv, m_xa_w_o, v_norm_g, v_final_g, v_ffn1_up, v_ffn1_down, v_ffn2_up, v_ffn2_down, v_conv_w_in, v_conv_w, v_conv_w_out, v_ssm_a_re, v_ssm_a_im, v_ssm_log_dt, v_ssm_b_re, v_ssm_b_im, v_ssm_c_re, v_ssm_c_im, v_ssm_d, v_ssm_w_glu, v_xa_w_q, v_xa_w_kv, v_xa_w_o):
    given = dict(x=x, mem=mem, norm_g=norm_g, final_g=final_g, ffn1_up=ffn1_up, ffn1_down=ffn1_down, ffn2_up=ffn2_up, ffn2_down=ffn2_down, conv_w_in=conv_w_in, conv_w=conv_w, conv_w_out=conv_w_out, ssm_a_re=ssm_a_re, ssm_a_im=ssm_a_im, ssm_log_dt=ssm_log_dt, ssm_b_re=ssm_b_re, ssm_b_im=ssm_b_im, ssm_c_re=ssm_c_re, ssm_c_im=ssm_c_im, ssm_d=ssm_d, ssm_w_glu=ssm_w_glu, xa_w_q=xa_w_q, xa_w_kv=xa_w_kv, xa_w_o=xa_w_o, loss_target=loss_target, m_norm_g=m_norm_g, m_final_g=m_final_g, m_ffn1_up=m_ffn1_up, m_ffn1_down=m_ffn1_down, m_ffn2_up=m_ffn2_up, m_ffn2_down=m_ffn2_down, m_conv_w_in=m_conv_w_in, m_conv_w=m_conv_w, m_conv_w_out=m_conv_w_out, m_ssm_a_re=m_ssm_a_re, m_ssm_a_im=m_ssm_a_im, m_ssm_log_dt=m_ssm_log_dt, m_ssm_b_re=m_ssm_b_re, m_ssm_b_im=m_ssm_b_im, m_ssm_c_re=m_ssm_c_re, m_ssm_c_im=m_ssm_c_im, m_ssm_d=m_ssm_d, m_ssm_w_glu=m_ssm_w_glu, m_xa_w_q=m_xa_w_q, m_xa_w_kv=m_xa_w_kv, m_xa_w_o=m_xa_w_o, v_norm_g=v_norm_g, v_final_g=v_final_g, v_ffn1_up=v_ffn1_up, v_ffn1_down=v_ffn1_down, v_ffn2_up=v_ffn2_up, v_ffn2_down=v_ffn2_down, v_conv_w_in=v_conv_w_in, v_conv_w=v_conv_w, v_conv_w_out=v_conv_w_out, v_ssm_a_re=v_ssm_a_re, v_ssm_a_im=v_ssm_a_im, v_ssm_log_dt=v_ssm_log_dt, v_ssm_b_re=v_ssm_b_re, v_ssm_b_im=v_ssm_b_im, v_ssm_c_re=v_ssm_c_re, v_ssm_c_im=v_ssm_c_im, v_ssm_d=v_ssm_d, v_ssm_w_glu=v_ssm_w_glu, v_xa_w_q=v_xa_w_q, v_xa_w_kv=v_xa_w_kv, v_xa_w_o=v_xa_w_o)
    weights = {n: given[n] for n in TWIN_WEIGHTS}
    shared = {n: given[n] for n in SHARED_INPUTS}
    per_example = {n: given[n] for n in ['x', 'mem']}
    grad_fn = _jax.value_and_grad(_loss, argnums=(0, 1))

    def one_microbatch(ex, loss_target):
        ex = dict(ex)
        diff = ex.pop(TWIN_DIFF_INPUT)
        return grad_fn(weights, diff, {**shared, **ex}, loss_target)

    if N_MICROBATCH == 1:
        loss, (grad_w, grad_x) = one_microbatch(per_example, given["loss_target"])
    else:
        def body(carry, xs):
            loss_sum, grad_sum = carry
            l_k, (gw_k, gx_k) = one_microbatch(xs[0], xs[1])
            with _jax.named_scope("update"):
                return (loss_sum + l_k, _jax.tree.map(_jnp.add, grad_sum, gw_k)), gx_k

        init = (_jnp.zeros((), _jnp.float32), _jax.tree.map(_jnp.zeros_like, weights))
        (loss, grad_w), grad_x = _jax.lax.scan(body, init, (per_example, given["loss_target"]))
    with _jax.named_scope("update"):
        delta_w, new_m, new_v = {}, {}, {}
        for n in TWIN_WEIGHTS:
            delta_w[n], new_m[n], new_v[n] = _adamw(weights[n], grad_w[n], given["m_" + n], given["v_" + n])
    return (loss, grad_x, *[grad_w[n] for n in TWIN_WEIGHTS], *[delta_w[n] for n in TWIN_WEIGHTS],
            *[new_m[n] for n in TWIN_WEIGHTS], *[new_v[n] for n in TWIN_WEIGHTS])
```

```python
import math

import jax
import jax.numpy as jnp
from jax import lax
from jax.experimental import pallas as pl
from jax.experimental.pallas import tpu as pltpu

F32 = jnp.float32
BF16 = jnp.bfloat16
MESH = pl.DeviceIdType.MESH
N_DEV = 8

NORM_EPS = 1e-6
EIG_CLIP = -1e-4
CONV_WIDTH = 3
ADAM_LR = 0.001
ADAM_B1 = 0.9
ADAM_B2 = 0.999
ADAM_EPS = 1e-08
ADAM_WD = 0.01
ADAM_STEP = 10
GELU_C = math.sqrt(2.0 / math.pi)
GELU_A = 0.044715

V7X_LANES = 128
V7X_VMEM_LIMIT = 56 * 1024 * 1024
S5_CHANNELS = 128

HBM_SPEC = pl.BlockSpec(memory_space=pltpu.HBM)


def _params(n_grid):
    return pltpu.CompilerParams(dimension_semantics=("arbitrary",) * n_grid, vmem_limit_bytes=V7X_VMEM_LIMIT)


def _pick(n, pref, align):
    if n <= pref:
        return n
    t = (pref // align) * align
    while t >= align:
        if n % t == 0:
            return t
        t -= align
    raise ValueError(f"no tile for {n} (pref {pref}, align {align})")


def _mm(name, a, b, *, ta=False, tb=False, out_dtype=BF16, res=None, scale=None, tm_pref=512, tn_pref=512):
    if ta:
        k, m = a.shape
    else:
        m, k = a.shape
    if tb:
        n, k2 = b.shape
    else:
        k2, n = b.shape
    assert k == k2, (name, a.shape, b.shape)
    tm = _pick(m, tm_pref, V7X_LANES if ta else 16)
    tn = _pick(n, tn_pref, V7X_LANES)
    a_spec = pl.BlockSpec((k, tm), lambda i, j: (0, i)) if ta else pl.BlockSpec((tm, k), lambda i, j: (i, 0))
    b_spec = pl.BlockSpec((tn, k), lambda i, j: (j, 0)) if tb else pl.BlockSpec((k, tn), lambda i, j: (0, j))
    o_spec = pl.BlockSpec((tm, tn), lambda i, j: (i, j))
    dims = (((0 if ta else 1,), (1 if tb else 0,)), ((), ()))
    has_res = res is not None

    def body(*refs):
        a_ref, b_ref = refs[0], refs[1]
        o_ref = refs[-1]
        acc = lax.dot_general(a_ref[...].astype(BF16), b_ref[...].astype(BF16), dims, preferred_element_type=F32)
        if scale is not None:
            acc = acc * scale
        if has_res:
            acc = acc + refs[2][...].astype(F32)
        o_ref[...] = acc.astype(o_ref.dtype)

    ins = [a, b] + ([res] if has_res else [])
    specs = [a_spec, b_spec] + ([o_spec] if has_res else [])
    return pl.pallas_call(
        body, name=name, grid=(m // tm, n // tn), in_specs=specs, out_specs=o_spec,
        out_shape=jax.ShapeDtypeStruct((m, n), out_dtype), compiler_params=_params(2),
    )(*ins)


def _rowwise(name, fn, rows, row_ins, par_ins, row_outs, acc_outs=(), tm_pref=256):
    tm = _pick(rows, tm_pref, 16)
    in_specs, ins = [], []
    for r in row_ins:
        arr, cb, cw = r if isinstance(r, tuple) else (r, 0, r.shape[1])
        assert arr.shape[0] == rows, (name, arr.shape, rows)
        ins.append(arr)
        in_specs.append(pl.BlockSpec((tm, cw), lambda i, cb=cb: (i, cb)))
    for p in par_ins:
        ins.append(p)
        in_specs.append(pl.BlockSpec(p.shape, lambda i: (0, 0)))
    out_specs = [pl.BlockSpec((tm, c), lambda i: (i, 0)) for c, _ in row_outs]
    out_specs += [pl.BlockSpec((r, c), lambda i: (0, 0)) for r, c in acc_outs]
    out_shape = [jax.ShapeDtypeStruct((rows, c), dt) for c, dt in row_outs]
    out_shape += [jax.ShapeDtypeStruct((r, c), F32) for r, c in acc_outs]
    n_in, n_row = len(ins), len(row_outs)

    def body(*refs):
        vals = [r[...] for r in refs[:n_in]]
        outs = refs[n_in:]
        res = fn(*vals)
        if not isinstance(res, (tuple, list)):
            res = (res,)
        for o, v in zip(outs[:n_row], res[:n_row]):
            if isinstance(v, (tuple, list)):
                off = 0
                for piece in v:
                    w = piece.shape[1]
                    o[:, off:off + w] = piece.astype(o.dtype)
                    off += w
            else:
                o[...] = v.astype(o.dtype)
        if len(outs) > n_row:
            @pl.when(pl.program_id(0) == 0)
            def _():
                for o in outs[n_row:]:
                    o[...] = jnp.zeros_like(o)

            for o, v in zip(outs[n_row:], res[n_row:]):
                o[...] += v

    out = pl.pallas_call(
        body, name=name, grid=(rows // tm,), in_specs=in_specs, out_specs=out_specs, out_shape=out_shape,
        compiler_params=_params(1),
    )(*ins)
    return out


def _inv_rms(x):
    return lax.rsqrt(jnp.mean(x * x, axis=-1, keepdims=True) + NORM_EPS)


def _rms_fwd(name, x, g):
    def fn(x, g):
        return x * _inv_rms(x) * g

    return _rowwise(name, fn, x.shape[0], [x], [g], [(x.shape[1], BF16)], tm_pref=512)[0]


def _rms_bwd(name, x, g, dn, dres=None):
    d = x.shape[1]

    def fn(x, dn, *rest):
        g = rest[-1]
        dn = dn.astype(F32)
        r = _inv_rms(x)
        xh = x * r
        dg = jnp.sum(dn * xh, axis=0, keepdims=True)
        if dres is None:
            return (dg,)
        dxh = dn * g
        dx = r * (dxh - xh * jnp.mean(dxh * xh, axis=-1, keepdims=True)) + rest[0]
        return dx, dg

    row_ins = [x, dn] + ([] if dres is None else [dres])
    row_outs = [] if dres is None else [(d, F32)]
    out = _rowwise(name, fn, x.shape[0], row_ins, [g], row_outs, [(1, d)], tm_pref=256)
    return (None, out[0]) if dres is None else (out[0], out[1])


def _sigmoid(x):
    return 1.0 / (1.0 + jnp.exp(-x))


def _swiglu(name, gu, f):
    def fn(gt, up):
        gt = gt.astype(F32)
        return gt * _sigmoid(gt) * up.astype(F32)

    return _rowwise(name, fn, gu.shape[0], [(gu, 0, f), (gu, 1, f)], [], [(f, BF16)])[0]


def _swiglu_bwd(name, dact, gu, f):
    def fn(dact, gt, up):
        dact, gt, up = dact.astype(F32), gt.astype(F32), up.astype(F32)
        sg = _sigmoid(gt)
        dgt = dact * up * (sg * (1.0 + gt * (1.0 - sg)))
        dup = dact * (gt * sg)
        return ((dgt, dup),)

    return _rowwise(name, fn, gu.shape[0], [dact, (gu, 0, f), (gu, 1, f)], [], [(2 * f, BF16)])[0]


def _glu_res(name, vg, x, d):
    def fn(val, gate, x):
        return x + val.astype(F32) * _sigmoid(gate.astype(F32))

    return _rowwise(name, fn, x.shape[0], [(vg, 0, d), (vg, 1, d), x], [], [(d, F32)])[0]


def _glu_bwd(name, dres, vg, d):
    def fn(dres, val, gate):
        val, gate = val.astype(F32), gate.astype(F32)
        sg = _sigmoid(gate)
        return ((dres * sg, dres * val * sg * (1.0 - sg)),)

    return _rowwise(name, fn, dres.shape[0], [dres, (vg, 0, d), (vg, 1, d)], [], [(2 * d, BF16)])[0]


def _final_loss(name, x, g, tgt):
    d = x.shape[1]

    def fn(x, tgt, g):
        r = _inv_rms(x)
        xh = x * r
        err = xh * g - tgt
        dy = err * (1.0 / d)
        dxh = dy * g
        dx = r * (dxh - xh * jnp.mean(dxh * xh, axis=-1, keepdims=True))
        return dx, jnp.sum(err * err, axis=0, keepdims=True), jnp.sum(dy * xh, axis=0, keepdims=True)

    return _rowwise(name, fn, x.shape[0], [x, tgt], [g], [(d, F32)], [(1, d), (1, d)])


def _shift_down(u, k):
    rows = lax.broadcasted_iota(jnp.int32, u.shape, 0)
    return jnp.where(rows >= k, pltpu.roll(u, k, 0), 0.0)


def _shift_up(u, k):
    n = u.shape[0]
    rows = lax.broadcasted_iota(jnp.int32, u.shape, 0)
    return jnp.where(rows < n - k, pltpu.roll(u, n - k, 0), 0.0)


def _conv_specs(seq, cw, n_cb, swap):
    def at(off):
        if swap:
            return pl.BlockSpec((seq, cw), lambda j, b: (b, off * n_cb + j))
        return pl.BlockSpec((seq, cw), lambda b, j: (b, off * n_cb + j))

    return at


def _conv_fwd(name, cbv, w, n_seq, seq):
    d = w.shape[1]
    cw = _pick(d, 256, V7X_LANES)
    n_cb = d // cw
    at = _conv_specs(seq, cw, n_cb, swap=False)

    def body(c_ref, b_ref, v_ref, w_ref, z_ref):
        u = c_ref[...].astype(F32) * v_ref[...].astype(F32)
        cv = w_ref[0:1, :] * _shift_down(u, 2) + w_ref[1:2, :] * _shift_down(u, 1) + w_ref[2:3, :] * u
        z_ref[...] = (b_ref[...].astype(F32) * cv).astype(z_ref.dtype)

    return pl.pallas_call(
        body, name=name, grid=(n_seq, n_cb),
        in_specs=[at(0), at(1), at(2), pl.BlockSpec((CONV_WIDTH, cw), lambda b, j: (0, j))],
        out_specs=at(0), out_shape=jax.ShapeDtypeStruct((n_seq * seq, d), BF16), compiler_params=_params(2),
    )(cbv, cbv, cbv, w)


def _conv_bwd(name, dz, cbv, w, n_seq, seq):
    d = w.shape[1]
    cw = _pick(d, 256, V7X_LANES)
    n_cb = d // cw
    at = _conv_specs(seq, cw, n_cb, swap=True)

    def body(dz_ref, c_ref, b_ref, v_ref, w_ref, dc_ref, db_ref, dv_ref, dw_ref):
        c, b, v = c_ref[...].astype(F32), b_ref[...].astype(F32), v_ref[...].astype(F32)
        dz = dz_ref[...].astype(F32)
        w0, w1, w2 = w_ref[0:1, :], w_ref[1:2, :], w_ref[2:3, :]
        u = c * v
        u1, u2 = _shift_down(u, 1), _shift_down(u, 2)
        cv = w0 * u2 + w1 * u1 + w2 * u
        db_ref[...] = (dz * cv).astype(db_ref.dtype)
        dcv = dz * b
        du = w2 * dcv + w1 * _shift_up(dcv, 1) + w0 * _shift_up(dcv, 2)
        dc_ref[...] = (du * v).astype(dc_ref.dtype)
        dv_ref[...] = (du * c).astype(dv_ref.dtype)

        @pl.when(pl.program_id(1) == 0)
        def _():
            dw_ref[...] = jnp.zeros_like(dw_ref)

        dw_ref[0:1, :] += jnp.sum(dcv * u2, axis=0, keepdims=True)
        dw_ref[1:2, :] += jnp.sum(dcv * u1, axis=0, keepdims=True)
        dw_ref[2:3, :] += jnp.sum(dcv * u, axis=0, keepdims=True)

    act = jax.ShapeDtypeStruct((n_seq * seq, d), BF16)
    return pl.pallas_call(
        body, name=name, grid=(n_cb, n_seq),
        in_specs=[at(0), at(0), at(1), at(2), pl.BlockSpec((CONV_WIDTH, cw), lambda j, b: (0, j))],
        out_specs=[at(0), at(0), at(0), pl.BlockSpec((CONV_WIDTH, cw), lambda j, b: (0, j))],
        out_shape=[act, act, act, jax.ShapeDtypeStruct((CONV_WIDTH, d), F32)], compiler_params=_params(2),
    )(dz, cbv, cbv, cbv, w)


def _s5_discretize(a_re, a_im, log_dt, b_re, b_im):
    lam_re = jnp.minimum(a_re, EIG_CLIP)
    lam_im = a_im
    dt = jnp.exp(log_dt)[:, None]
    mag = jnp.exp(lam_re * dt)
    abar_re = mag * jnp.cos(lam_im * dt)
    abar_im = mag * jnp.sin(lam_im * dt)
    den = lam_re * lam_re + lam_im * lam_im
    num_re = abar_re - 1.0
    num_im = abar_im
    coef_re = (num_re * lam_re + num_im * lam_im) / den
    coef_im = (num_im * lam_re - num_re * lam_im) / den
    bbar_re = coef_re[..., None] * b_re - coef_im[..., None] * b_im
    bbar_im = coef_re[..., None] * b_im + coef_im[..., None] * b_re
    return abar_re, abar_im, bbar_re, bbar_im


def _block_diag_in(bbar, gb):
    g, p, h = bbar.shape
    t = jnp.transpose(bbar.reshape(g // gb, gb, p, h), (0, 1, 3, 2))
    return jnp.einsum("cghp,gk->cghkp", t, jnp.eye(gb, dtype=bbar.dtype)).reshape(g // gb, gb * h, gb * p)


def _block_diag_in_t(blk, gb, p, h):
    nb = blk.shape[0]
    t = jnp.einsum("cghkp,gk->cghp", blk.reshape(nb, gb, h, gb, p), jnp.eye(gb, dtype=blk.dtype))
    return jnp.transpose(t, (0, 1, 3, 2)).reshape(nb * gb, p, h)


def _block_diag_out(c, gb):
    g, h, p = c.shape
    t = jnp.transpose(c.reshape(g // gb, gb, h, p), (0, 1, 3, 2))
    return jnp.einsum("cgph,gk->cgpkh", t, jnp.eye(gb, dtype=c.dtype)).reshape(g // gb, gb * p, gb * h)


def _block_diag_out_t(blk, gb, p, h):
    nb = blk.shape[0]
    t = jnp.einsum("cgpkh,gk->cgph", blk.reshape(nb, gb, p, gb, h), jnp.eye(gb, dtype=blk.dtype))
    return jnp.transpose(t, (0, 1, 3, 2)).reshape(nb * gb, h, p)


def _gelu(y):
    return 0.5 * y * (1.0 + jnp.tanh(GELU_C * (y + GELU_A * y * y * y)))


def _gelu_grad(y):
    th = jnp.tanh(GELU_C * (y + GELU_A * y * y * y))
    return 0.5 * (1.0 + th) + 0.5 * y * (1.0 - th * th) * GELU_C * (1.0 + 3.0 * GELU_A * y * y)


def _dot(a, b, ca, cb):
    return lax.dot_general(a.astype(BF16), b.astype(BF16), (((ca,), (cb,)), ((), ())), preferred_element_type=F32)


def _s5_specs(seq, ch, sb):
    act = pl.BlockSpec((seq, ch), lambda j, b: (b, j))
    state = pl.BlockSpec((seq, sb), lambda j, b: (b, j))
    w_in = pl.BlockSpec((None, ch, sb), lambda j, b: (j, 0, 0))
    w_out = pl.BlockSpec((None, sb, ch), lambda j, b: (j, 0, 0))
    lane_s = pl.BlockSpec((1, sb), lambda j, b: (0, j))
    lane_c = pl.BlockSpec((1, ch), lambda j, b: (0, j))
    return act, state, w_in, w_out, lane_s, lane_c


def _s5_fwd(name, h, bin_re, bin_im, cout_re, cout_im, abar_re, abar_im, dskip, n_seq, seq):
    t, d = h.shape
    nb, ch, sb = bin_re.shape
    act, state, w_in, w_out, lane_s, lane_c = _s5_specs(seq, ch, sb)

    def body(h_ref, bre_ref, bim_ref, cre_ref, cim_ref, ar_ref, ai_ref, d_ref, sre_ref, sim_ref, y_ref, z_ref):
        u = h_ref[...]
        sre_ref[...] = _dot(u, bre_ref[...], 1, 0)
        sim_ref[...] = _dot(u, bim_ref[...], 1, 0)
        ar, ai = ar_ref[...], ai_ref[...]

        def step(i, carry):
            sr, si = carry
            row = pl.ds(i, 1)
            nr = ar * sr - ai * si + sre_ref[row, :]
            ni = ar * si + ai * sr + sim_ref[row, :]
            sre_ref[row, :] = nr
            sim_ref[row, :] = ni
            return nr, ni

        zero = jnp.zeros((1, sb), F32)
        lax.fori_loop(0, seq, step, (zero, zero), unroll=8)
        y = _dot(sre_ref[...], cre_ref[...], 1, 0) - _dot(sim_ref[...], cim_ref[...], 1, 0)
        y = y + d_ref[...] * u.astype(F32)
        y_ref[...] = y
        z_ref[...] = _gelu(y).astype(z_ref.dtype)

    return pl.pallas_call(
        body, name=name, grid=(nb, n_seq),
        in_specs=[act, w_in, w_in, w_out, w_out, lane_s, lane_s, lane_c],
        out_specs=[state, state, act, act],
        out_shape=[jax.ShapeDtypeStruct((t, nb * sb), F32), jax.ShapeDtypeStruct((t, nb * sb), F32),
                   jax.ShapeDtypeStruct((t, d), F32), jax.ShapeDtypeStruct((t, d), BF16)],
        compiler_params=_params(2),
    )(h, bin_re, bin_im, cout_re, cout_im, abar_re, abar_im, dskip)


def _s5_bwd(name, dz, ypre, h, s_re, s_im, bin_re, bin_im, cout_re, cout_im, abar_re, abar_im, dskip, n_seq, seq):
    t, d = h.shape
    nb, ch, sb = bin_re.shape
    act, state, w_in, w_out, lane_s, lane_c = _s5_specs(seq, ch, sb)

    def body(dz_ref, y_ref, h_ref, sre_ref, sim_ref, bre_ref, bim_ref, cre_ref, cim_ref, ar_ref, ai_ref, d_ref,
             dh_ref, dbre_ref, dbim_ref, dcre_ref, dcim_ref, dar_ref, dai_ref, dd_ref, gre, gim):
        first = pl.program_id(1) == 0
        u = h_ref[...].astype(F32)
        dy = dz_ref[...].astype(F32) * _gelu_grad(y_ref[...])
        gre[...] = _dot(dy, cre_ref[...], 1, 1)
        gim[...] = -_dot(dy, cim_ref[...], 1, 1)
        ar, ai = ar_ref[...], ai_ref[...]

        def step(i, carry):
            gr, gi = carry
            row = pl.ds(seq - 1 - i, 1)
            nr = gre[row, :] + ar * gr + ai * gi
            ni = gim[row, :] - ai * gr + ar * gi
            gre[row, :] = nr
            gim[row, :] = ni
            return nr, ni

        zero = jnp.zeros((1, sb), F32)
        lax.fori_loop(0, seq, step, (zero, zero), unroll=8)

        g_re, g_im = gre[...], gim[...]
        s_re, s_im = sre_ref[...], sim_ref[...]
        p_re, p_im = _shift_down(s_re, 1), _shift_down(s_im, 1)
        dar = jnp.sum(g_re * p_re + g_im * p_im, axis=0, keepdims=True)
        dai = jnp.sum(g_im * p_re - g_re * p_im, axis=0, keepdims=True)
        dbre = _dot(u, g_re, 0, 0)
        dbim = _dot(u, g_im, 0, 0)
        dcre = _dot(s_re, dy, 0, 0)
        dcim = -_dot(s_im, dy, 0, 0)
        ddd = jnp.sum(dy * u, axis=0, keepdims=True)
        dh_ref[...] = _dot(g_re, bre_ref[...], 1, 1) + _dot(g_im, bim_ref[...], 1, 1) + d_ref[...] * dy

        @pl.when(first)
        def _():
            dar_ref[...] = dar
            dai_ref[...] = dai
            dbre_ref[...] = dbre
            dbim_ref[...] = dbim
            dcre_ref[...] = dcre
            dcim_ref[...] = dcim
            dd_ref[...] = ddd

        @pl.when(jnp.logical_not(first))
        def _():
            dar_ref[...] += dar
            dai_ref[...] += dai
            dbre_ref[...] += dbre
            dbim_ref[...] += dbim
            dcre_ref[...] += dcre
            dcim_ref[...] += dcim
            dd_ref[...] += ddd

    return pl.pallas_call(
        body, name=name, grid=(nb, n_seq),
        in_specs=[act, act, act, state, state, w_in, w_in, w_out, w_out, lane_s, lane_s, lane_c],
        out_specs=[act, w_in, w_in, w_out, w_out, lane_s, lane_s, lane_c],
        out_shape=[jax.ShapeDtypeStruct((t, d), F32),
                   jax.ShapeDtypeStruct((nb, ch, sb), F32), jax.ShapeDtypeStruct((nb, ch, sb), F32),
                   jax.ShapeDtypeStruct((nb, sb, ch), F32), jax.ShapeDtypeStruct((nb, sb, ch), F32),
                   jax.ShapeDtypeStruct((1, nb * sb), F32), jax.ShapeDtypeStruct((1, nb * sb), F32),
                   jax.ShapeDtypeStruct((1, d), F32)],
        scratch_shapes=[pltpu.VMEM((seq, sb), F32), pltpu.VMEM((seq, sb), F32)],
        compiler_params=_params(2),
    )(dz, ypre, h, s_re, s_im, bin_re, bin_im, cout_re, cout_im, abar_re, abar_im, dskip)


def _softmax_rows(q, k, scale):
    s = _dot(q, k, 1, 1) * scale
    e = jnp.exp(s - jnp.max(s, axis=-1, keepdims=True))
    return e / jnp.sum(e, axis=-1, keepdims=True)


def _attn_fwd(name, q, kv, n_seq, seq, mlen, heads):
    t, d = q.shape
    hd = d // heads
    tq = _pick(seq, 512, 16)
    nq = seq // tq
    scale = hd ** -0.5
    q_spec = pl.BlockSpec((tq, hd), lambda b, h, i: (b * nq + i, h))

    def body(q_ref, k_ref, v_ref, o_ref):
        p = _softmax_rows(q_ref[...], k_ref[...], scale)
        o_ref[...] = _dot(p, v_ref[...], 1, 0).astype(o_ref.dtype)

    return pl.pallas_call(
        body, name=name, grid=(n_seq, heads, nq),
        in_specs=[q_spec, pl.BlockSpec((mlen, hd), lambda b, h, i: (b, h)),
                  pl.BlockSpec((mlen, hd), lambda b, h, i: (b, heads + h))],
        out_specs=q_spec, out_shape=jax.ShapeDtypeStruct((t, d), BF16), compiler_params=_params(3),
    )(q, kv, kv)


def _attn_bwd(name, q, kv, do, n_seq, seq, mlen, heads):
    t, d = q.shape
    hd = d // heads
    tq = _pick(seq, 512, 16)
    nq = seq // tq
    scale = hd ** -0.5
    q_spec = pl.BlockSpec((tq, hd), lambda b, h, i: (b * nq + i, h))
    k_spec = pl.BlockSpec((mlen, hd), lambda b, h, i: (b, h))

    def body(q_ref, k_ref, v_ref, do_ref, dq_ref, dk_ref, dv_ref):
        q, k, v, do = q_ref[...], k_ref[...], v_ref[...], do_ref[...]
        p = _softmax_rows(q, k, scale)
        dp = _dot(do, v, 1, 1)
        ds = p * (dp - jnp.sum(dp * p, axis=-1, keepdims=True)) * scale
        dq_ref[...] = _dot(ds, k, 1, 0).astype(dq_ref.dtype)

        @pl.when(pl.program_id(2) == 0)
        def _():
            dk_ref[...] = jnp.zeros_like(dk_ref)
            dv_ref[...] = jnp.zeros_like(dv_ref)

        dk_ref[...] += _dot(ds, q, 0, 0)
        dv_ref[...] += _dot(p, do, 0, 0)

    return pl.pallas_call(
        body, name=name, grid=(n_seq, heads, nq),
        in_specs=[q_spec, k_spec, pl.BlockSpec((mlen, hd), lambda b, h, i: (b, heads + h)), q_spec],
        out_specs=[q_spec, k_spec, k_spec],
        out_shape=[jax.ShapeDtypeStruct((t, d), BF16), jax.ShapeDtypeStruct((n_seq * mlen, d), F32),
                   jax.ShapeDtypeStruct((n_seq * mlen, d), F32)],
        compiler_params=_params(3),
    )(q, kv, kv, do)


def _adamw(name, parts, w, m, v):
    r, c = w.shape
    tr = _pick(r, max(16, (512 * 1024) // c // 16 * 16), 8)
    p_spec = pl.BlockSpec((N_DEV, tr, c), lambda i: (0, i, 0))
    spec = pl.BlockSpec((tr, c), lambda i: (i, 0))
    c1 = 1.0 - ADAM_B1 ** ADAM_STEP
    c2 = 1.0 - ADAM_B2 ** ADAM_STEP

    def body(p_ref, w_ref, m_ref, v_ref, g_ref, d_ref, nm_ref, nv_ref):
        g = p_ref[0].astype(F32)
        for k in range(1, N_DEV):
            g = g + p_ref[k].astype(F32)
        nm = ADAM_B1 * m_ref[...] + (1.0 - ADAM_B1) * g
        nv = ADAM_B2 * v_ref[...] + (1.0 - ADAM_B2) * (g * g)
        g_ref[...] = g
        nm_ref[...] = nm
        nv_ref[...] = nv
        d_ref[...] = -ADAM_LR * ((nm / c1) / (jnp.sqrt(nv / c2) + ADAM_EPS) + ADAM_WD * w_ref[...])

    out = jax.ShapeDtypeStruct((r, c), F32)
    return pl.pallas_call(
        body, name=name, grid=(r // tr,), in_specs=[p_spec, spec, spec, spec], out_specs=[spec] * 4,
        out_shape=[out] * 4, compiler_params=_params(1),
    )(parts, w, m, v)


def _place():
    x, y, c = lax.axis_index("x"), lax.axis_index("y"), lax.axis_index("c")
    return x, y, c


def _index(px, py, pc):
    return 4 * px + 2 * py + pc


def _all_gather(name, shards):
    n = len(shards)

    def body(*refs):
        in_refs, out_refs = refs[:n], refs[n:2 * n]
        send_sems, recv_sems, local_sems = refs[2 * n:]
        x, y, c = _place()
        me, sibling = (x, y, c), (x, y, 1 - c)
        chips = [(1 - x, y), (x, 1 - y), (1 - x, 1 - y)]

        def slot(k, block):
            return out_refs[k].at[_index(*block)]

        def copy(k, j, block, to, src=None):
            return pltpu.make_async_remote_copy(
                src_ref=slot(k, block) if src is None else src, dst_ref=slot(k, block),
                send_sem=send_sems.at[7 * k + j], recv_sem=recv_sems.at[7 * k + j], device_id=to, device_id_type=MESH)

        mine = [pltpu.make_async_copy(in_refs[k], slot(k, me), local_sems.at[k]) for k in range(n)]
        for cp in mine:
            cp.start()
        first = []
        for k in range(n):
            first.append(copy(k, 0, me, sibling, src=in_refs[k]))
            first += [copy(k, 1 + j, me, (*chip, c), src=in_refs[k]) for j, chip in enumerate(chips)]
        for cp in first:
            cp.start()
        passed = []
        for j, chip in enumerate(chips):
            for k in range(n):
                copy(k, 1 + j, (*chip, c), me).wait_recv()
                cp = copy(k, 4 + j, (*chip, c), sibling)
                cp.start()
                passed.append(cp)
        for k in range(n):
            copy(k, 0, sibling, me).wait_recv()
            for j, chip in enumerate(chips):
                copy(k, 4 + j, (*chip, 1 - c), me).wait_recv()
        for cp in first + passed:
            cp.wait_send()
        for cp in mine:
            cp.wait()

    return pl.pallas_call(
        body, name=name, in_specs=[HBM_SPEC] * n, out_specs=[HBM_SPEC] * n,
        out_shape=[jax.ShapeDtypeStruct((N_DEV,) + s.shape, s.dtype) for s in shards],
        scratch_shapes=[pltpu.SemaphoreType.DMA((7 * n,)), pltpu.SemaphoreType.DMA((7 * n,)),
                        pltpu.SemaphoreType.DMA((n,))],
    )(*shards)


def _all_to_all(name, parts):
    n = len(parts)

    def body(*refs):
        in_refs, out_refs = refs[:n], refs[n:2 * n]
        send_sems, recv_sems, local_sems = refs[2 * n:]
        x, y, c = _place()
        me = _index(x, y, c)
        peers = []
        for r in range(1, N_DEV):
            rx, ry, rc = (r >> 2) & 1, (r >> 1) & 1, r & 1
            peers.append((1 - x if rx else x, 1 - y if ry else y, 1 - c if rc else c))

        def copy(k, j):
            peer = peers[j]
            return pltpu.make_async_remote_copy(
                src_ref=in_refs[k].at[_index(*peer)], dst_ref=out_refs[k].at[me],
                send_sem=send_sems.at[7 * k + j], recv_sem=recv_sems.at[7 * k + j], device_id=peer, device_id_type=MESH)

        def landing(k, j):
            peer = peers[j]
            return pltpu.make_async_remote_copy(
                src_ref=in_refs[k].at[me], dst_ref=out_refs[k].at[_index(*peer)],
                send_sem=send_sems.at[7 * k + j], recv_sem=recv_sems.at[7 * k + j], device_id=peer, device_id_type=MESH)

        mine = [pltpu.make_async_copy(in_refs[k].at[me], out_refs[k].at[me], local_sems.at[k]) for k in range(n)]
        sends = [copy(k, j) for j in range(7) for k in range(n)]
        for cp in mine + sends:
            cp.start()
        for k in range(n):
            for j in range(7):
                landing(k, j).wait_recv()
        for cp in sends:
            cp.wait_send()
        for cp in mine:
            cp.wait()

    return pl.pallas_call(
        body, name=name, in_specs=[HBM_SPEC] * n, out_specs=[HBM_SPEC] * n,
        out_shape=[jax.ShapeDtypeStruct(p.shape, p.dtype) for p in parts],
        scratch_shapes=[pltpu.SemaphoreType.DMA((7 * n,)), pltpu.SemaphoreType.DMA((7 * n,)),
                        pltpu.SemaphoreType.DMA((n,))],
    )(*parts)


def _cols_whole(g, i):
    w = g[:, i]
    return jnp.transpose(w, (1, 0, 2)).reshape(w.shape[1], N_DEV * w.shape[2])


def _rows_whole(g, i):
    w = g[:, i]
    return w.reshape(N_DEV * w.shape[1], w.shape[2])


def _cols_parts(dws):
    out = []
    for dw in dws:
        k, n8 = dw.shape
        out.append(jnp.transpose(dw.reshape(k, N_DEV, n8 // N_DEV), (1, 0, 2)))
    return jnp.stack(out, axis=1)


def _rows_parts(dws):
    out = []
    for dw in dws:
        r8, c = dw.shape
        out.append(dw.reshape(N_DEV, r8 // N_DEV, c))
    return jnp.stack(out, axis=1)


def _pack_rows(arrays, lead=()):
    rows = []
    for a in arrays:
        flat = a.reshape(lead + (-1,)).astype(F32)
        pad = (-flat.shape[-1]) % V7X_LANES
        flat = jnp.pad(flat, [(0, 0)] * len(lead) + [(0, pad)])
        rows.append(flat.reshape(lead + (-1, V7X_LANES)))
    out = jnp.concatenate(rows, axis=len(lead))
    pad = (-out.shape[len(lead)]) % 8
    return jnp.pad(out, [(0, 0)] * len(lead) + [(0, pad), (0, 0)])


def _unpack_rows(packed, shapes):
    out, row = [], 0
    for s in shapes:
        size = math.prod(s)
        n_rows = -(-size // V7X_LANES)
        out.append(packed[row:row + n_rows].reshape(-1)[:size].reshape(s))
        row += n_rows
    return out


def _merge2d(a):
    return a.reshape(-1, a.shape[-1])


def _ffn_fwd(tag, x, g, w_up, w_down):
    f = w_down.shape[0]
    n = _rms_fwd(f"{tag}_norm", x, g)
    gu = _mm(f"{tag}_up", n, w_up)
    act = _swiglu(f"{tag}_swiglu", gu, f)
    out = _mm(f"{tag}_down", act, w_down, res=x, scale=0.5, out_dtype=F32)
    return out, (x, n, gu, act)


def _ffn_bwd(tag, dres, saved, g, w_up, w_down):
    x, n, gu, act = saved
    f = w_down.shape[0]
    dact = _mm(f"{tag}_down_dx", dres, w_down, tb=True, scale=0.5)
    d_down = _mm(f"{tag}_down_dw", act, dres, ta=True, scale=0.5, tm_pref=256)
    dgu = _swiglu_bwd(f"{tag}_swiglu_bwd", dact, gu, f)
    d_up = _mm(f"{tag}_up_dw", n, dgu, ta=True)
    dn = _mm(f"{tag}_up_dx", dgu, w_up, tb=True)
    dx, dg = _rms_bwd(f"{tag}_norm_bwd", x, g, dn, dres)
    return dx, dg, d_up, d_down


def _conv_mixer_fwd(tag, x, g, w_in, w_conv, w_out, n_seq, seq):
    h = _rms_fwd(f"{tag}_norm", x, g)
    cbv = _mm(f"{tag}_in", h, w_in)
    z = _conv_fwd(f"{tag}_conv", cbv, w_conv, n_seq, seq)
    out = _mm(f"{tag}_out", z, w_out, res=x, out_dtype=F32)
    return out, (x, h, cbv, z)


def _conv_mixer_bwd(tag, dres, saved, g, w_in, w_conv, w_out, n_seq, seq):
    x, h, cbv, z = saved
    dz = _mm(f"{tag}_out_dx", dres, w_out, tb=True)
    d_out = _mm(f"{tag}_out_dw", z, dres, ta=True, tm_pref=256)
    dc, db, dv, d_conv = _conv_bwd(f"{tag}_conv_bwd", dz, cbv, w_conv, n_seq, seq)
    dcbv = jnp.concatenate([dc, db, dv], axis=1)
    d_in = _mm(f"{tag}_in_dw", h, dcbv, ta=True)
    dh = _mm(f"{tag}_in_dx", dcbv, w_in, tb=True)
    dx, dg = _rms_bwd(f"{tag}_norm_bwd", x, g, dh, dres)
    return dx, dg, d_in, d_conv, d_out


def _s5_mixer_fwd(tag, x, g, ssm, dskip, w_glu, n_seq, seq):
    a_re, a_im, log_dt, b_re, b_im, c_re, c_im = ssm
    groups, p, hh = b_re.shape
    gb = S5_CHANNELS // hh
    disc, disc_vjp = jax.vjp(_s5_discretize, a_re, a_im, log_dt, b_re, b_im)
    abar_re, abar_im, bbar_re, bbar_im = disc
    mats = (_block_diag_in(bbar_re, gb).astype(BF16), _block_diag_in(bbar_im, gb).astype(BF16),
            _block_diag_out(c_re, gb).astype(BF16), _block_diag_out(c_im, gb).astype(BF16),
            abar_re.reshape(1, groups * p), abar_im.reshape(1, groups * p), dskip)
    d = x.shape[1]
    h = _rms_fwd(f"{tag}_norm", x, g)
    s_re, s_im, ypre, z = _s5_fwd(f"{tag}_scan", h, *mats, n_seq, seq)
    vg = _mm(f"{tag}_glu", z, w_glu)
    out = _glu_res(f"{tag}_glu_act", vg, x, d)
    return out, (x, h, s_re, s_im, ypre, z, vg, mats, disc_vjp, (groups, p, hh, gb))


def _s5_mixer_bwd(tag, dres, saved, g, w_glu, n_seq, seq):
    x, h, s_re, s_im, ypre, z, vg, mats, disc_vjp, (groups, p, hh, gb) = saved
    d = x.shape[1]
    dvg = _glu_bwd(f"{tag}_glu_act_bwd", dres, vg, d)
    d_glu = _mm(f"{tag}_glu_dw", z, dvg, ta=True)
    dz = _mm(f"{tag}_glu_dx", dvg, w_glu, tb=True)
    dh, dbin_re, dbin_im, dcout_re, dcout_im, dabar_re, dabar_im, d_skip = _s5_bwd(
        f"{tag}_scan_bwd", dz, ypre, h, s_re, s_im, *mats, n_seq, seq)
    d_are, d_aim, d_logdt, d_bre, d_bim = disc_vjp((
        dabar_re.reshape(groups, p), dabar_im.reshape(groups, p),
        _block_diag_in_t(dbin_re, gb, p, hh), _block_diag_in_t(dbin_im, gb, p, hh)))
    d_cre = _block_diag_out_t(dcout_re, gb, p, hh)
    d_cim = _block_diag_out_t(dcout_im, gb, p, hh)
    dx, dg = _rms_bwd(f"{tag}_norm_bwd", x, g, dh, dres)
    return dx, dg, (d_are, d_aim, d_logdt, d_bre, d_bim, d_cre, d_cim), d_skip, d_glu


def _xattn_fwd(tag, x, mem, g_q, g_mem, w_q, w_kv, w_o, n_seq, seq, mlen, heads):
    n = _rms_fwd(f"{tag}_norm", x, g_q)
    q = _mm(f"{tag}_q", n, w_q)
    mem_n = _rms_fwd(f"{tag}_mem_norm", mem, g_mem)
    kv = _mm(f"{tag}_kv", mem_n, w_kv)
    o = _attn_fwd(f"{tag}_attn", q, kv, n_seq, seq, mlen, heads)
    out = _mm(f"{tag}_o", o, w_o, res=x, out_dtype=F32)
    return out, (x, n, q, mem_n, kv, o)


def _xattn_bwd(tag, dres, saved, mem, g_q, g_mem, w_q, w_kv, w_o, n_seq, seq, mlen, heads):
    x, n, q, mem_n, kv, o = saved
    do = _mm(f"{tag}_o_dx", dres, w_o, tb=True)
    d_o = _mm(f"{tag}_o_dw", o, dres, ta=True, tm_pref=256)
    dq, dk, dv = _attn_bwd(f"{tag}_attn_bwd", q, kv, do, n_seq, seq, mlen, heads)
    dkv = jnp.concatenate([dk, dv], axis=1)
    d_q = _mm(f"{tag}_q_dw", n, dq, ta=True)
    dn = _mm(f"{tag}_q_dx", dq, w_q, tb=True)
    d_kv = _mm(f"{tag}_kv_dw", mem_n, dkv, ta=True)
    dmem_n = _mm(f"{tag}_kv_dx", dkv, w_kv, tb=True)
    _, dg_mem = _rms_bwd(f"{tag}_mem_norm_bwd", mem, g_mem, dmem_n)
    dx, dg_q = _rms_bwd(f"{tag}_norm_bwd", x, g_q, dn, dres)
    return dx, dg_q, dg_mem, d_q, d_kv, d_o


WEIGHT_NAMES = ("norm_g", "final_g", "ffn1_up", "ffn1_down", "ffn2_up", "ffn2_down", "conv_w_in", "conv_w",
                "conv_w_out", "ssm_a_re", "ssm_a_im", "ssm_log_dt", "ssm_b_re", "ssm_b_im", "ssm_c_re", "ssm_c_im",
                "ssm_d", "ssm_w_glu", "xa_w_q", "xa_w_kv", "xa_w_o")
MATRICES = ("ffn1_up", "ffn1_down", "ffn2_up", "ffn2_down", "conv_w_in", "conv_w_out", "ssm_w_glu", "xa_w_q",
            "xa_w_kv", "xa_w_o")
COLUMN_SHARDED = ("ffn1_up", "ffn2_up", "conv_w_in", "ssm_w_glu", "xa_w_kv")
SMALL_SHARDED = ("norm_g", "conv_w", "ssm_d")
REPLICATED = ("ssm_a_re", "ssm_a_im", "ssm_log_dt", "ssm_b_re", "ssm_b_im", "ssm_c_re", "ssm_c_im", "final_g")


def kernel(x, mem, norm_g, final_g, ffn1_up, ffn1_down, ffn2_up, ffn2_down, conv_w_in, conv_w, conv_w_out, ssm_a_re, ssm_a_im, ssm_log_dt, ssm_b_re, ssm_b_im, ssm_c_re, ssm_c_im, ssm_d, ssm_w_glu, xa_w_q, xa_w_kv, xa_w_o, loss_target, m_norm_g, m_final_g, m_ffn1_up, m_ffn1_down, m_ffn2_up, m_ffn2_down, m_conv_w_in, m_conv_w, m_conv_w_out, m_ssm_a_re, m_ssm_a_im, m_ssm_log_dt, m_ssm_b_re, m_ssm_b_im, m_ssm_c_re, m_ssm_c_im, m_ssm_d, m_ssm_w_glu, m_xa_w_q, m_xa_w_kv, m_xa_w_o, v_norm_g, v_final_g, v_ffn1_up, v_ffn1_down, v_ffn2_up, v_ffn2_down, v_conv_w_in, v_conv_w, v_conv_w_out, v_ssm_a_re, v_ssm_a_im, v_ssm_log_dt, v_ssm_b_re, v_ssm_b_im, v_ssm_c_re, v_ssm_c_im, v_ssm_d, v_ssm_w_glu, v_xa_w_q, v_xa_w_kv, v_xa_w_o):
    w = dict(norm_g=norm_g, final_g=final_g, ffn1_up=ffn1_up, ffn1_down=ffn1_down, ffn2_up=ffn2_up,
             ffn2_down=ffn2_down, conv_w_in=conv_w_in, conv_w=conv_w, conv_w_out=conv_w_out, ssm_a_re=ssm_a_re,
             ssm_a_im=ssm_a_im, ssm_log_dt=ssm_log_dt, ssm_b_re=ssm_b_re, ssm_b_im=ssm_b_im, ssm_c_re=ssm_c_re,
             ssm_c_im=ssm_c_im, ssm_d=ssm_d, ssm_w_glu=ssm_w_glu, xa_w_q=xa_w_q, xa_w_kv=xa_w_kv, xa_w_o=xa_w_o)
    mom = dict(norm_g=m_norm_g, final_g=m_final_g, ffn1_up=m_ffn1_up, ffn1_down=m_ffn1_down, ffn2_up=m_ffn2_up,
               ffn2_down=m_ffn2_down, conv_w_in=m_conv_w_in, conv_w=m_conv_w, conv_w_out=m_conv_w_out,
               ssm_a_re=m_ssm_a_re, ssm_a_im=m_ssm_a_im, ssm_log_dt=m_ssm_log_dt, ssm_b_re=m_ssm_b_re,
               ssm_b_im=m_ssm_b_im, ssm_c_re=m_ssm_c_re, ssm_c_im=m_ssm_c_im, ssm_d=m_ssm_d, ssm_w_glu=m_ssm_w_glu,
               xa_w_q=m_xa_w_q, xa_w_kv=m_xa_w_kv, xa_w_o=m_xa_w_o)
    var = dict(norm_g=v_norm_g, final_g=v_final_g, ffn1_up=v_ffn1_up, ffn1_down=v_ffn1_down, ffn2_up=v_ffn2_up,
               ffn2_down=v_ffn2_down, conv_w_in=v_conv_w_in, conv_w=v_conv_w, conv_w_out=v_conv_w_out,
               ssm_a_re=v_ssm_a_re, ssm_a_im=v_ssm_a_im, ssm_log_dt=v_ssm_log_dt, ssm_b_re=v_ssm_b_re,
               ssm_b_im=v_ssm_b_im, ssm_c_re=v_ssm_c_re, ssm_c_im=v_ssm_c_im, ssm_d=v_ssm_d, ssm_w_glu=v_ssm_w_glu,
               xa_w_q=v_xa_w_q, xa_w_kv=v_xa_w_kv, xa_w_o=v_xa_w_o)

    n_seq, seq, d = x.shape
    mlen = mem.shape[1]
    depth, n_norms = norm_g.shape[0], norm_g.shape[1]
    heads = 4
    tokens = n_seq * seq
    x2 = x.reshape(tokens, d)
    mem2 = mem.reshape(n_seq * mlen, d)
    tgt2 = loss_target.reshape(tokens, d)

    small_shapes = [w[k].shape for k in SMALL_SHARDED]
    small_rows = [_merge2d(w[k]) for k in SMALL_SHARDED]
    small_counts = [s.shape[0] for s in small_rows]
    small = jnp.concatenate(small_rows, axis=0)
    small = jnp.pad(small, [(0, (-small.shape[0]) % 8), (0, 0)])
    gathered = _all_gather("gather_weights", [w[k].astype(BF16) for k in MATRICES] + [small])
    gw = dict(zip(MATRICES, gathered[:-1]))
    small_all = gathered[-1]

    def small_whole(idx):
        start = sum(small_counts[:idx])
        part = small_all[:, start:start + small_counts[idx]]
        lead = small_shapes[idx][:-1]
        part = part.reshape((N_DEV,) + lead + (part.shape[-1],))
        part = jnp.moveaxis(part, 0, -2)
        return part.reshape(lead + (N_DEV * part.shape[-1],))

    norm_all = small_whole(0)
    conv_all = small_whole(1)
    dskip_all = small_whole(2)

    def whole(name, i):
        return _cols_whole(gw[name], i) if name in COLUMN_SHARDED else _rows_whole(gw[name], i)

    saved = []
    cur = x2
    for i in range(depth):
        g = [norm_all[i, k].reshape(1, d) for k in range(n_norms)]
        j = i // 2
        lw = dict(up1=whole("ffn1_up", i), down1=whole("ffn1_down", i), up2=whole("ffn2_up", i),
                  down2=whole("ffn2_down", i), q=whole("xa_w_q", i), kv=whole("xa_w_kv", i), o=whole("xa_w_o", i))
        cur, s_ffn1 = _ffn_fwd(f"l{i}_ffn1", cur, g[0], lw["up1"], lw["down1"])
        if i % 2 == 0:
            lw.update(w_in=whole("conv_w_in", j), w_out=whole("conv_w_out", j), w_conv=conv_all[j])
            cur, s_mix = _conv_mixer_fwd(f"l{i}_conv", cur, g[1], lw["w_in"], lw["w_conv"], lw["w_out"], n_seq, seq)
        else:
            lw.update(glu=whole("ssm_w_glu", j))
            ssm = tuple(w[k][j] for k in ("ssm_a_re", "ssm_a_im", "ssm_log_dt", "ssm_b_re", "ssm_b_im",
                                          "ssm_c_re", "ssm_c_im"))
            cur, s_mix = _s5_mixer_fwd(f"l{i}_s5", cur, g[1], ssm, dskip_all[j].reshape(1, d), lw["glu"], n_seq, seq)
        cur, s_xa = _xattn_fwd(f"l{i}_xa", cur, mem2, g[2], g[3], lw["q"], lw["kv"], lw["o"], n_seq, seq, mlen, heads)
        cur, s_ffn2 = _ffn_fwd(f"l{i}_ffn2", cur, g[4], lw["up2"], lw["down2"])
        saved.append((g, lw, s_ffn1, s_mix, s_xa, s_ffn2))

    dres, err2, d_final = _final_loss("loss_head", cur, final_g.reshape(1, d), tgt2)
    loss = lax.psum(0.5 * jnp.sum(err2) / d, ("x", "y", "c"))

    gm = {k: [None] * w[k].shape[0] for k in MATRICES}
    d_norm = [[None] * n_norms for _ in range(depth)]
    d_conv = [None] * conv_w.shape[0]
    d_skip = [None] * ssm_d.shape[0]
    d_ssm = [None] * ssm_a_re.shape[0]
    for i in reversed(range(depth)):
        g, lw, s_ffn1, s_mix, s_xa, s_ffn2 = saved[i]
        j = i // 2
        dres, d_norm[i][4], gm["ffn2_up"][i], gm["ffn2_down"][i] = _ffn_bwd(
            f"l{i}_ffn2", dres, s_ffn2, g[4], lw["up2"], lw["down2"])
        dres, d_norm[i][2], d_norm[i][3], gm["xa_w_q"][i], gm["xa_w_kv"][i], gm["xa_w_o"][i] = _xattn_bwd(
            f"l{i}_xa", dres, s_xa, mem2, g[2], g[3], lw["q"], lw["kv"], lw["o"], n_seq, seq, mlen, heads)
        if i % 2 == 0:
            dres, d_norm[i][1], gm["conv_w_in"][j], d_conv[j], gm["conv_w_out"][j] = _conv_mixer_bwd(
                f"l{i}_conv", dres, s_mix, g[1], lw["w_in"], lw["w_conv"], lw["w_out"], n_seq, seq)
        else:
            dres, d_norm[i][1], d_ssm[j], d_skip[j], gm["ssm_w_glu"][j] = _s5_mixer_bwd(
                f"l{i}_s5", dres, s_mix, g[1], lw["glu"], n_seq, seq)
        dres, d_norm[i][0], gm["ffn1_up"][i], gm["ffn1_down"][i] = _ffn_bwd(
            f"l{i}_ffn1", dres, s_ffn1, g[0], lw["up1"], lw["down1"])
    grad_x = dres.reshape(n_seq, seq, d)

    parts = [(_cols_parts if k in COLUMN_SHARDED else _rows_parts)(gm[k]) for k in MATRICES]
    d_norm_all = jnp.stack([jnp.concatenate(row, axis=0) for row in d_norm])
    d_conv_all = jnp.stack(d_conv)
    d_skip_all = jnp.concatenate(d_skip, axis=0)

    def small_parts(full):
        lead = full.shape[:-1]
        t = full.reshape(lead + (N_DEV, full.shape[-1] // N_DEV))
        t = jnp.moveaxis(t, -2, 0)
        return t.reshape(N_DEV, -1, t.shape[-1])

    small_g = jnp.concatenate([small_parts(a) for a in (d_norm_all, d_conv_all, d_skip_all)], axis=1)
    small_g = jnp.pad(small_g, [(0, 0), (0, (-small_g.shape[1]) % 8), (0, 0)])
    received = _all_to_all("exchange_grads", parts + [small_g])

    rep_grads = [jnp.stack([d_ssm[j][k] for j in range(len(d_ssm))]) for k in range(7)] + [d_final.reshape(-1)]
    rep_shapes = [w[k].shape for k in REPLICATED]
    rep_all = _all_gather("gather_replicated_grads", [_pack_rows(rep_grads)])[0]

    grads, deltas, new_m, new_v = {}, {}, {}, {}
    for k, rec in zip(MATRICES, received[:-1]):
        shape = w[k].shape
        out = _adamw(f"adamw_{k}", rec.reshape(N_DEV, -1, shape[-1]), _merge2d(w[k]), _merge2d(mom[k]),
                     _merge2d(var[k]))
        grads[k], deltas[k], new_m[k], new_v[k] = [o.reshape(shape) for o in out]

    def small_local(src):
        rows = jnp.concatenate([_merge2d(src[k]) for k in SMALL_SHARDED], axis=0)
        return jnp.pad(rows, [(0, (-rows.shape[0]) % 8), (0, 0)])

    out = _adamw("adamw_small", received[-1], small, small_local(mom), small_local(var))
    for res, o in zip((grads, deltas, new_m, new_v), out):
        start = 0
        for k, cnt, shape in zip(SMALL_SHARDED, small_counts, small_shapes):
            res[k] = o[start:start + cnt].reshape(shape)
            start += cnt

    out = _adamw("adamw_replicated", rep_all, _pack_rows([w[k] for k in REPLICATED]),
                 _pack_rows([mom[k] for k in REPLICATED]), _pack_rows([var[k] for k in REPLICATED]))
    for res, o in zip((grads, deltas, new_m, new_v), out):
        for k, a in zip(REPLICATED, _unpack_rows(o, rep_shapes)):
            res[k] = a

    return (loss, grad_x, *[grads[k] for k in WEIGHT_NAMES], *[deltas[k] for k in WEIGHT_NAMES],
            *[new_m[k] for k in WEIGHT_NAMES], *[new_v[k] for k in WEIGHT_NAMES])
```

```python
import math

import jax
import jax.numpy as jnp
from jax import lax
from jax.experimental import pallas as pl
from jax.experimental.pallas import tpu as pltpu

F32 = jnp.float32
BF16 = jnp.bfloat16
MESH = pl.DeviceIdType.MESH
N_DEV = 8

NORM_EPS = 1e-6
EIG_CLIP = -1e-4
CONV_WIDTH = 3
ADAM_LR = 0.001
ADAM_B1 = 0.9
ADAM_B2 = 0.999
ADAM_EPS = 1e-08
ADAM_WD = 0.01
ADAM_STEP = 10
GELU_C = math.sqrt(2.0 / math.pi)
GELU_A = 0.044715

V7X_LANES = 128
V7X_VMEM_LIMIT = 56 * 1024 * 1024
S5_CHANNELS = 128
PACK_TILE = 8 * V7X_LANES

HBM_SPEC = pl.BlockSpec(memory_space=pltpu.HBM)


def _params(n_grid):
    return pltpu.CompilerParams(dimension_semantics=("arbitrary",) * n_grid, vmem_limit_bytes=V7X_VMEM_LIMIT)


def _pick(n, pref, align):
    if n <= pref:
        return n
    t = (pref // align) * align
    while t >= align:
        if n % t == 0:
            return t
        t -= align
    raise ValueError(f"no tile for {n} (pref {pref}, align {align})")


MM_RHS_BLOCK_BYTES = 12 * 1024 * 1024
MM_LHS_BLOCK_BYTES = 6 * 1024 * 1024
MM_ACC_BYTES = 6 * 1024 * 1024
MM_ROWS = 512


def _mm_tiles(m, k, n, a_item, b_item, ta):
    tn = _pick(n, max(V7X_LANES, MM_RHS_BLOCK_BYTES // (k * b_item)), V7X_LANES)
    rows = min(MM_ROWS, MM_ACC_BYTES // (4 * tn), MM_LHS_BLOCK_BYTES // (k * a_item))
    align = V7X_LANES if ta else 16
    tm = _pick(m, max(align, rows), align)
    return tm, tn


def _mm(name, a, b, *, ta=False, tb=False, out_dtype=BF16, res=None, scale=None):
    if ta:
        k, m = a.shape
    else:
        m, k = a.shape
    if tb:
        n, k2 = b.shape
    else:
        k2, n = b.shape
    assert k == k2, (name, a.shape, b.shape)
    tm, tn = _mm_tiles(m, k, n, a.dtype.itemsize, b.dtype.itemsize, ta)
    a_spec = pl.BlockSpec((k, tm), lambda j, i: (0, i)) if ta else pl.BlockSpec((tm, k), lambda j, i: (i, 0))
    b_spec = pl.BlockSpec((tn, k), lambda j, i: (j, 0)) if tb else pl.BlockSpec((k, tn), lambda j, i: (0, j))
    o_spec = pl.BlockSpec((tm, tn), lambda j, i: (i, j))
    dims = (((0 if ta else 1,), (1 if tb else 0,)), ((), ()))
    has_res = res is not None

    def body(*refs):
        a_ref, b_ref = refs[0], refs[1]
        o_ref = refs[-1]
        acc = lax.dot_general(a_ref[...].astype(BF16), b_ref[...].astype(BF16), dims, preferred_element_type=F32)
        if scale is not None:
            acc = acc * scale
        if has_res:
            acc = acc + refs[2][...].astype(F32)
        o_ref[...] = acc.astype(o_ref.dtype)

    ins = [a, b] + ([res] if has_res else [])
    specs = [a_spec, b_spec] + ([o_spec] if has_res else [])
    return pl.pallas_call(
        body, name=name, grid=(n // tn, m // tm), in_specs=specs, out_specs=o_spec,
        out_shape=jax.ShapeDtypeStruct((m, n), out_dtype), compiler_params=_params(2),
    )(*ins)


def _rowwise(name, fn, rows, row_ins, par_ins, row_outs, acc_outs=(), tm_pref=256):
    tm = _pick(rows, tm_pref, 16)
    in_specs, ins = [], []
    for r in row_ins:
        arr, cb, cw = r if isinstance(r, tuple) else (r, 0, r.shape[1])
        assert arr.shape[0] == rows, (name, arr.shape, rows)
        ins.append(arr)
        in_specs.append(pl.BlockSpec((tm, cw), lambda i, cb=cb: (i, cb)))
    for p in par_ins:
        ins.append(p)
        in_specs.append(pl.BlockSpec(p.shape, lambda i: (0, 0)))
    out_specs = [pl.BlockSpec((tm, c), lambda i: (i, 0)) for c, _ in row_outs]
    out_specs += [pl.BlockSpec((r, c), lambda i: (0, 0)) for r, c in acc_outs]
    out_shape = [jax.ShapeDtypeStruct((rows, c), dt) for c, dt in row_outs]
    out_shape += [jax.ShapeDtypeStruct((r, c), F32) for r, c in acc_outs]
    n_in, n_row = len(ins), len(row_outs)

    def body(*refs):
        vals = [r[...] for r in refs[:n_in]]
        outs = refs[n_in:]
        res = fn(*vals)
        if not isinstance(res, (tuple, list)):
            res = (res,)
        for o, v in zip(outs[:n_row], res[:n_row]):
            if isinstance(v, (tuple, list)):
                off = 0
                for piece in v:
                    w = piece.shape[1]
                    o[:, off:off + w] = piece.astype(o.dtype)
                    off += w
            else:
                o[...] = v.astype(o.dtype)
        if len(outs) > n_row:
            @pl.when(pl.program_id(0) == 0)
            def _():
                for o in outs[n_row:]:
                    o[...] = jnp.zeros_like(o)

            for o, v in zip(outs[n_row:], res[n_row:]):
                o[...] += v

    out = pl.pallas_call(
        body, name=name, grid=(rows // tm,), in_specs=in_specs, out_specs=out_specs, out_shape=out_shape,
        compiler_params=_params(1),
    )(*ins)
    return out


def _inv_rms(x):
    return lax.rsqrt(jnp.mean(x * x, axis=-1, keepdims=True) + NORM_EPS)


def _rms_fwd(name, x, g):
    def fn(x, g):
        return x * _inv_rms(x) * g

    return _rowwise(name, fn, x.shape[0], [x], [g], [(x.shape[1], BF16)], tm_pref=512)[0]


def _rms_bwd(name, x, g, dn, dres=None):
    d = x.shape[1]

    def fn(x, dn, *rest):
        g = rest[-1]
        dn = dn.astype(F32)
        r = _inv_rms(x)
        xh = x * r
        dg = jnp.sum(dn * xh, axis=0, keepdims=True)
        if dres is None:
            return (dg,)
        dxh = dn * g
        dx = r * (dxh - xh * jnp.mean(dxh * xh, axis=-1, keepdims=True)) + rest[0]
        return dx, dx, dg

    row_ins = [x, dn] + ([] if dres is None else [dres])
    row_outs = [] if dres is None else [(d, F32), (d, BF16)]
    out = _rowwise(name, fn, x.shape[0], row_ins, [g], row_outs, [(1, d)], tm_pref=256)
    return (None, out[0]) if dres is None else ((out[0], out[1]), out[2])


def _sigmoid(x):
    return 1.0 / (1.0 + jnp.exp(-x))


def _swiglu(name, gu, f):
    def fn(gt, up):
        gt = gt.astype(F32)
        return gt * _sigmoid(gt) * up.astype(F32)

    return _rowwise(name, fn, gu.shape[0], [(gu, 0, f), (gu, 1, f)], [], [(f, BF16)])[0]


def _swiglu_bwd(name, dact, gu, f):
    def fn(dact, gt, up):
        dact, gt, up = dact.astype(F32), gt.astype(F32), up.astype(F32)
        sg = _sigmoid(gt)
        dgt = dact * up * (sg * (1.0 + gt * (1.0 - sg)))
        dup = dact * (gt * sg)
        return ((dgt, dup),)

    return _rowwise(name, fn, gu.shape[0], [dact, (gu, 0, f), (gu, 1, f)], [], [(2 * f, BF16)])[0]


def _glu_res(name, vg, x, d):
    def fn(val, gate, x):
        return x + val.astype(F32) * _sigmoid(gate.astype(F32))

    return _rowwise(name, fn, x.shape[0], [(vg, 0, d), (vg, 1, d), x], [], [(d, F32)])[0]


def _glu_bwd(name, dres, vg, d):
    def fn(dres, val, gate):
        val, gate = val.astype(F32), gate.astype(F32)
        sg = _sigmoid(gate)
        return ((dres * sg, dres * val * sg * (1.0 - sg)),)

    return _rowwise(name, fn, dres.shape[0], [dres, (vg, 0, d), (vg, 1, d)], [], [(2 * d, BF16)])[0]


def _final_loss(name, x, g, tgt):
    d = x.shape[1]

    def fn(x, tgt, g):
        r = _inv_rms(x)
        xh = x * r
        err = xh * g - tgt
        dy = err * (1.0 / d)
        dxh = dy * g
        dx = r * (dxh - xh * jnp.mean(dxh * xh, axis=-1, keepdims=True))
        return dx, dx, jnp.sum(err * err, axis=0, keepdims=True), jnp.sum(dy * xh, axis=0, keepdims=True)

    dx, dx16, err2, dg = _rowwise(name, fn, x.shape[0], [x, tgt], [g], [(d, F32), (d, BF16)], [(1, d), (1, d)])
    return (dx, dx16), err2, dg


def _shift_down(u, k):
    rows = lax.broadcasted_iota(jnp.int32, u.shape, 0)
    return jnp.where(rows >= k, pltpu.roll(u, k, 0), 0.0)


def _shift_up(u, k):
    n = u.shape[0]
    rows = lax.broadcasted_iota(jnp.int32, u.shape, 0)
    return jnp.where(rows < n - k, pltpu.roll(u, n - k, 0), 0.0)


def _conv_specs(seq, cw, n_cb, swap):
    def at(off):
        if swap:
            return pl.BlockSpec((seq, cw), lambda j, b: (b, off * n_cb + j))
        return pl.BlockSpec((seq, cw), lambda b, j: (b, off * n_cb + j))

    return at


def _conv_fwd(name, cbv, w, n_seq, seq):
    d = w.shape[1]
    cw = _pick(d, 256, V7X_LANES)
    n_cb = d // cw
    at = _conv_specs(seq, cw, n_cb, swap=False)

    def body(c_ref, b_ref, v_ref, w_ref, z_ref):
        u = c_ref[...].astype(F32) * v_ref[...].astype(F32)
        cv = w_ref[0:1, :] * _shift_down(u, 2) + w_ref[1:2, :] * _shift_down(u, 1) + w_ref[2:3, :] * u
        z_ref[...] = (b_ref[...].astype(F32) * cv).astype(z_ref.dtype)

    return pl.pallas_call(
        body, name=name, grid=(n_seq, n_cb),
        in_specs=[at(0), at(1), at(2), pl.BlockSpec((CONV_WIDTH, cw), lambda b, j: (0, j))],
        out_specs=at(0), out_shape=jax.ShapeDtypeStruct((n_seq * seq, d), BF16), compiler_params=_params(2),
    )(cbv, cbv, cbv, w)


def _conv_bwd(name, dz, cbv, w, n_seq, seq):
    d = w.shape[1]
    cw = _pick(d, 256, V7X_LANES)
    n_cb = d // cw
    at = _conv_specs(seq, cw, n_cb, swap=True)

    def body(dz_ref, c_ref, b_ref, v_ref, w_ref, dc_ref, db_ref, dv_ref, dw_ref):
        c, b, v = c_ref[...].astype(F32), b_ref[...].astype(F32), v_ref[...].astype(F32)
        dz = dz_ref[...].astype(F32)
        w0, w1, w2 = w_ref[0:1, :], w_ref[1:2, :], w_ref[2:3, :]
        u = c * v
        u1, u2 = _shift_down(u, 1), _shift_down(u, 2)
        cv = w0 * u2 + w1 * u1 + w2 * u
        db_ref[...] = (dz * cv).astype(db_ref.dtype)
        dcv = dz * b
        du = w2 * dcv + w1 * _shift_up(dcv, 1) + w0 * _shift_up(dcv, 2)
        dc_ref[...] = (du * v).astype(dc_ref.dtype)
        dv_ref[...] = (du * c).astype(dv_ref.dtype)

        @pl.when(pl.program_id(1) == 0)
        def _():
            dw_ref[...] = jnp.zeros_like(dw_ref)

        dw_ref[0:1, :] += jnp.sum(dcv * u2, axis=0, keepdims=True)
        dw_ref[1:2, :] += jnp.sum(dcv * u1, axis=0, keepdims=True)
        dw_ref[2:3, :] += jnp.sum(dcv * u, axis=0, keepdims=True)

    act = jax.ShapeDtypeStruct((n_seq * seq, d), BF16)
    return pl.pallas_call(
        body, name=name, grid=(n_cb, n_seq),
        in_specs=[at(0), at(0), at(1), at(2), pl.BlockSpec((CONV_WIDTH, cw), lambda j, b: (0, j))],
        out_specs=[at(0), at(0), at(0), pl.BlockSpec((CONV_WIDTH, cw), lambda j, b: (0, j))],
        out_shape=[act, act, act, jax.ShapeDtypeStruct((CONV_WIDTH, d), F32)], compiler_params=_params(2),
    )(dz, cbv, cbv, cbv, w)


def _s5_discretize(a_re, a_im, log_dt, b_re, b_im):
    lam_re = jnp.minimum(a_re, EIG_CLIP)
    lam_im = a_im
    dt = jnp.exp(log_dt)[:, None]
    mag = jnp.exp(lam_re * dt)
    abar_re = mag * jnp.cos(lam_im * dt)
    abar_im = mag * jnp.sin(lam_im * dt)
    den = lam_re * lam_re + lam_im * lam_im
    num_re = abar_re - 1.0
    num_im = abar_im
    coef_re = (num_re * lam_re + num_im * lam_im) / den
    coef_im = (num_im * lam_re - num_re * lam_im) / den
    bbar_re = coef_re[..., None] * b_re - coef_im[..., None] * b_im
    bbar_im = coef_re[..., None] * b_im + coef_im[..., None] * b_re
    return abar_re, abar_im, bbar_re, bbar_im


def _block_diag_in(bbar, gb):
    g, p, h = bbar.shape
    t = jnp.transpose(bbar.reshape(g // gb, gb, p, h), (0, 1, 3, 2))
    return jnp.einsum("cghp,gk->cghkp", t, jnp.eye(gb, dtype=bbar.dtype)).reshape(g // gb, gb * h, gb * p)


def _block_diag_in_t(blk, gb, p, h):
    nb = blk.shape[0]
    t = jnp.einsum("cghkp,gk->cghp", blk.reshape(nb, gb, h, gb, p), jnp.eye(gb, dtype=blk.dtype))
    return jnp.transpose(t, (0, 1, 3, 2)).reshape(nb * gb, p, h)


def _block_diag_out(c, gb):
    g, h, p = c.shape
    t = jnp.transpose(c.reshape(g // gb, gb, h, p), (0, 1, 3, 2))
    return jnp.einsum("cgph,gk->cgpkh", t, jnp.eye(gb, dtype=c.dtype)).reshape(g // gb, gb * p, gb * h)


def _block_diag_out_t(blk, gb, p, h):
    nb = blk.shape[0]
    t = jnp.einsum("cgpkh,gk->cgph", blk.reshape(nb, gb, p, gb, h), jnp.eye(gb, dtype=blk.dtype))
    return jnp.transpose(t, (0, 1, 3, 2)).reshape(nb * gb, h, p)


def _gelu(y):
    return 0.5 * y * (1.0 + jnp.tanh(GELU_C * (y + GELU_A * y * y * y)))


def _gelu_grad(y):
    th = jnp.tanh(GELU_C * (y + GELU_A * y * y * y))
    return 0.5 * (1.0 + th) + 0.5 * y * (1.0 - th * th) * GELU_C * (1.0 + 3.0 * GELU_A * y * y)


def _dot(a, b, ca, cb):
    return lax.dot_general(a.astype(BF16), b.astype(BF16), (((ca,), (cb,)), ((), ())), preferred_element_type=F32)


def _s5_specs(seq, ch, sb):
    act = pl.BlockSpec((seq, ch), lambda j, b: (b, j))
    state = pl.BlockSpec((seq, sb), lambda j, b: (b, j))
    w_in = pl.BlockSpec((None, ch, sb), lambda j, b: (j, 0, 0))
    w_out = pl.BlockSpec((None, sb, ch), lambda j, b: (j, 0, 0))
    lane_s = pl.BlockSpec((1, sb), lambda j, b: (0, j))
    lane_c = pl.BlockSpec((1, ch), lambda j, b: (0, j))
    return act, state, w_in, w_out, lane_s, lane_c


def _s5_fwd(name, h, bin_re, bin_im, cout_re, cout_im, abar_re, abar_im, dskip, n_seq, seq):
    t, d = h.shape
    nb, ch, sb = bin_re.shape
    act, state, w_in, w_out, lane_s, lane_c = _s5_specs(seq, ch, sb)

    def body(h_ref, bre_ref, bim_ref, cre_ref, cim_ref, ar_ref, ai_ref, d_ref, sre_ref, sim_ref, y_ref, z_ref):
        u = h_ref[...]
        sre_ref[...] = _dot(u, bre_ref[...], 1, 0)
        sim_ref[...] = _dot(u, bim_ref[...], 1, 0)
        ar, ai = ar_ref[...], ai_ref[...]

        def step(i, carry):
            sr, si = carry
            row = pl.ds(i, 1)
            nr = ar * sr - ai * si + sre_ref[row, :]
            ni = ar * si + ai * sr + sim_ref[row, :]
            sre_ref[row, :] = nr
            sim_ref[row, :] = ni
            return nr, ni

        zero = jnp.zeros((1, sb), F32)
        lax.fori_loop(0, seq, step, (zero, zero), unroll=8)
        y = _dot(sre_ref[...], cre_ref[...], 1, 0) - _dot(sim_ref[...], cim_ref[...], 1, 0)
        y = y + d_ref[...] * u.astype(F32)
        y_ref[...] = y
        z_ref[...] = _gelu(y).astype(z_ref.dtype)

    return pl.pallas_call(
        body, name=name, grid=(nb, n_seq),
        in_specs=[act, w_in, w_in, w_out, w_out, lane_s, lane_s, lane_c],
        out_specs=[state, state, act, act],
        out_shape=[jax.ShapeDtypeStruct((t, nb * sb), F32), jax.ShapeDtypeStruct((t, nb * sb), F32),
                   jax.ShapeDtypeStruct((t, d), F32), jax.ShapeDtypeStruct((t, d), BF16)],
        compiler_params=_params(2),
    )(h, bin_re, bin_im, cout_re, cout_im, abar_re, abar_im, dskip)


def _s5_bwd(name, dz, ypre, h, s_re, s_im, bin_re, bin_im, cout_re, cout_im, abar_re, abar_im, dskip, n_seq, seq):
    t, d = h.shape
    nb, ch, sb = bin_re.shape
    act, state, w_in, w_out, lane_s, lane_c = _s5_specs(seq, ch, sb)

    def body(dz_ref, y_ref, h_ref, sre_ref, sim_ref, bre_ref, bim_ref, cre_ref, cim_ref, ar_ref, ai_ref, d_ref,
             dh_ref, dbre_ref, dbim_ref, dcre_ref, dcim_ref, dar_ref, dai_ref, dd_ref, gre, gim):
        first = pl.program_id(1) == 0
        u = h_ref[...].astype(F32)
        dy = dz_ref[...].astype(F32) * _gelu_grad(y_ref[...])
        gre[...] = _dot(dy, cre_ref[...], 1, 1)
        gim[...] = -_dot(dy, cim_ref[...], 1, 1)
        ar, ai = ar_ref[...], ai_ref[...]

        def step(i, carry):
            gr, gi = carry
            row = pl.ds(seq - 1 - i, 1)
            nr = gre[row, :] + ar * gr + ai * gi
            ni = gim[row, :] - ai * gr + ar * gi
            gre[row, :] = nr
            gim[row, :] = ni
            return nr, ni

        zero = jnp.zeros((1, sb), F32)
        lax.fori_loop(0, seq, step, (zero, zero), unroll=8)

        g_re, g_im = gre[...], gim[...]
        s_re, s_im = sre_ref[...], sim_ref[...]
        p_re, p_im = _shift_down(s_re, 1), _shift_down(s_im, 1)
        dar = jnp.sum(g_re * p_re + g_im * p_im, axis=0, keepdims=True)
        dai = jnp.sum(g_im * p_re - g_re * p_im, axis=0, keepdims=True)
        dbre = _dot(u, g_re, 0, 0)
        dbim = _dot(u, g_im, 0, 0)
        dcre = _dot(s_re, dy, 0, 0)
        dcim = -_dot(s_im, dy, 0, 0)
        ddd = jnp.sum(dy * u, axis=0, keepdims=True)
        dh_ref[...] = _dot(g_re, bre_ref[...], 1, 1) + _dot(g_im, bim_ref[...], 1, 1) + d_ref[...] * dy

        @pl.when(first)
        def _():
            dar_ref[...] = dar
            dai_ref[...] = dai
            dbre_ref[...] = dbre
            dbim_ref[...] = dbim
            dcre_ref[...] = dcre
            dcim_ref[...] = dcim
            dd_ref[...] = ddd

        @pl.when(jnp.logical_not(first))
        def _():
            dar_ref[...] += dar
            dai_ref[...] += dai
            dbre_ref[...] += dbre
            dbim_ref[...] += dbim
            dcre_ref[...] += dcre
            dcim_ref[...] += dcim
            dd_ref[...] += ddd

    return pl.pallas_call(
        body, name=name, grid=(nb, n_seq),
        in_specs=[act, act, act, state, state, w_in, w_in, w_out, w_out, lane_s, lane_s, lane_c],
        out_specs=[act, w_in, w_in, w_out, w_out, lane_s, lane_s, lane_c],
        out_shape=[jax.ShapeDtypeStruct((t, d), F32),
                   jax.ShapeDtypeStruct((nb, ch, sb), F32), jax.ShapeDtypeStruct((nb, ch, sb), F32),
                   jax.ShapeDtypeStruct((nb, sb, ch), F32), jax.ShapeDtypeStruct((nb, sb, ch), F32),
                   jax.ShapeDtypeStruct((1, nb * sb), F32), jax.ShapeDtypeStruct((1, nb * sb), F32),
                   jax.ShapeDtypeStruct((1, d), F32)],
        scratch_shapes=[pltpu.VMEM((seq, sb), F32), pltpu.VMEM((seq, sb), F32)],
        compiler_params=_params(2),
    )(dz, ypre, h, s_re, s_im, bin_re, bin_im, cout_re, cout_im, abar_re, abar_im, dskip)


def _softmax_rows(q, k, scale):
    s = _dot(q, k, 1, 1) * scale
    e = jnp.exp(s - jnp.max(s, axis=-1, keepdims=True))
    return e / jnp.sum(e, axis=-1, keepdims=True)


def _attn_fwd(name, q, kv, n_seq, seq, mlen, heads):
    t, d = q.shape
    hd = d // heads
    tq = _pick(seq, 512, 16)
    nq = seq // tq
    scale = hd ** -0.5
    q_spec = pl.BlockSpec((tq, hd), lambda b, h, i: (b * nq + i, h))

    def body(q_ref, k_ref, v_ref, o_ref):
        p = _softmax_rows(q_ref[...], k_ref[...], scale)
        o_ref[...] = _dot(p, v_ref[...], 1, 0).astype(o_ref.dtype)

    return pl.pallas_call(
        body, name=name, grid=(n_seq, heads, nq),
        in_specs=[q_spec, pl.BlockSpec((mlen, hd), lambda b, h, i: (b, h)),
                  pl.BlockSpec((mlen, hd), lambda b, h, i: (b, heads + h))],
        out_specs=q_spec, out_shape=jax.ShapeDtypeStruct((t, d), BF16), compiler_params=_params(3),
    )(q, kv, kv)


def _attn_bwd(name, q, kv, do, n_seq, seq, mlen, heads):
    t, d = q.shape
    hd = d // heads
    tq = _pick(seq, 512, 16)
    nq = seq // tq
    scale = hd ** -0.5
    q_spec = pl.BlockSpec((tq, hd), lambda b, h, i: (b * nq + i, h))
    k_spec = pl.BlockSpec((mlen, hd), lambda b, h, i: (b, h))

    def body(q_ref, k_ref, v_ref, do_ref, dq_ref, dk_ref, dv_ref):
        q, k, v, do = q_ref[...], k_ref[...], v_ref[...], do_ref[...]
        p = _softmax_rows(q, k, scale)
        dp = _dot(do, v, 1, 1)
        ds = p * (dp - jnp.sum(dp * p, axis=-1, keepdims=True)) * scale
        dq_ref[...] = _dot(ds, k, 1, 0).astype(dq_ref.dtype)

        @pl.when(pl.program_id(2) == 0)
        def _():
            dk_ref[...] = jnp.zeros_like(dk_ref)
            dv_ref[...] = jnp.zeros_like(dv_ref)

        dk_ref[...] += _dot(ds, q, 0, 0)
        dv_ref[...] += _dot(p, do, 0, 0)

    return pl.pallas_call(
        body, name=name, grid=(n_seq, heads, nq),
        in_specs=[q_spec, k_spec, pl.BlockSpec((mlen, hd), lambda b, h, i: (b, heads + h)), q_spec],
        out_specs=[q_spec, k_spec, k_spec],
        out_shape=[jax.ShapeDtypeStruct((t, d), BF16), jax.ShapeDtypeStruct((n_seq * mlen, d), F32),
                   jax.ShapeDtypeStruct((n_seq * mlen, d), F32)],
        compiler_params=_params(3),
    )(q, kv, kv, do)


def _adamw(name, parts, w, m, v):
    r, c = w.shape
    tr = _pick(r, max(16, (512 * 1024) // c // 16 * 16), 8)
    p_spec = pl.BlockSpec((N_DEV, tr, c), lambda i: (0, i, 0))
    spec = pl.BlockSpec((tr, c), lambda i: (i, 0))
    c1 = 1.0 - ADAM_B1 ** ADAM_STEP
    c2 = 1.0 - ADAM_B2 ** ADAM_STEP

    def body(p_ref, w_ref, m_ref, v_ref, g_ref, d_ref, nm_ref, nv_ref):
        g = p_ref[0].astype(F32)
        for k in range(1, N_DEV):
            g = g + p_ref[k].astype(F32)
        nm = ADAM_B1 * m_ref[...] + (1.0 - ADAM_B1) * g
        nv = ADAM_B2 * v_ref[...] + (1.0 - ADAM_B2) * (g * g)
        g_ref[...] = g
        nm_ref[...] = nm
        nv_ref[...] = nv
        d_ref[...] = -ADAM_LR * ((nm / c1) / (jnp.sqrt(nv / c2) + ADAM_EPS) + ADAM_WD * w_ref[...])

    out = jax.ShapeDtypeStruct((r, c), F32)
    return pl.pallas_call(
        body, name=name, grid=(r // tr,), in_specs=[p_spec, spec, spec, spec], out_specs=[spec] * 4,
        out_shape=[out] * 4, compiler_params=_params(1),
    )(parts, w, m, v)


def _place():
    x, y, c = lax.axis_index("x"), lax.axis_index("y"), lax.axis_index("c")
    return x, y, c


def _index(px, py, pc):
    return 4 * px + 2 * py + pc


def _all_gather(name, shards):
    n = len(shards)

    def body(*refs):
        in_refs, out_refs = refs[:n], refs[n:2 * n]
        send_sems, recv_sems, local_sems = refs[2 * n:]
        x, y, c = _place()
        me, sibling = (x, y, c), (x, y, 1 - c)
        chips = [(1 - x, y), (x, 1 - y), (1 - x, 1 - y)]

        def slot(k, block):
            return out_refs[k].at[_index(*block)]

        def copy(k, j, block, to, src=None):
            return pltpu.make_async_remote_copy(
                src_ref=slot(k, block) if src is None else src, dst_ref=slot(k, block),
                send_sem=send_sems.at[7 * k + j], recv_sem=recv_sems.at[7 * k + j], device_id=to, device_id_type=MESH)

        mine = [pltpu.make_async_copy(in_refs[k], slot(k, me), local_sems.at[k]) for k in range(n)]
        for cp in mine:
            cp.start()
        first = []
        for k in range(n):
            first.append(copy(k, 0, me, sibling, src=in_refs[k]))
            first += [copy(k, 1 + j, me, (*chip, c), src=in_refs[k]) for j, chip in enumerate(chips)]
        for cp in first:
            cp.start()
        passed = []
        for j, chip in enumerate(chips):
            for k in range(n):
                copy(k, 1 + j, (*chip, c), me).wait_recv()
                cp = copy(k, 4 + j, (*chip, c), sibling)
                cp.start()
                passed.append(cp)
        for k in range(n):
            copy(k, 0, sibling, me).wait_recv()
            for j, chip in enumerate(chips):
                copy(k, 4 + j, (*chip, 1 - c), me).wait_recv()
        for cp in first + passed:
            cp.wait_send()
        for cp in mine:
            cp.wait()

    return pl.pallas_call(
        body, name=name, in_specs=[HBM_SPEC] * n, out_specs=[HBM_SPEC] * n,
        out_shape=[jax.ShapeDtypeStruct((N_DEV,) + s.shape, s.dtype) for s in shards],
        scratch_shapes=[pltpu.SemaphoreType.DMA((7 * n,)), pltpu.SemaphoreType.DMA((7 * n,)),
                        pltpu.SemaphoreType.DMA((n,))],
    )(*shards)


def _all_to_all(name, parts):
    n = len(parts)

    def body(*refs):
        in_refs, out_refs = refs[:n], refs[n:2 * n]
        send_sems, recv_sems, local_sems = refs[2 * n:]
        x, y, c = _place()
        me = _index(x, y, c)
        peers = []
        for r in range(1, N_DEV):
            rx, ry, rc = (r >> 2) & 1, (r >> 1) & 1, r & 1
            peers.append((1 - x if rx else x, 1 - y if ry else y, 1 - c if rc else c))

        def copy(k, j):
            peer = peers[j]
            return pltpu.make_async_remote_copy(
                src_ref=in_refs[k].at[_index(*peer)], dst_ref=out_refs[k].at[me],
                send_sem=send_sems.at[7 * k + j], recv_sem=recv_sems.at[7 * k + j], device_id=peer, device_id_type=MESH)

        def landing(k, j):
            peer = peers[j]
            return pltpu.make_async_remote_copy(
                src_ref=in_refs[k].at[me], dst_ref=out_refs[k].at[_index(*peer)],
                send_sem=send_sems.at[7 * k + j], recv_sem=recv_sems.at[7 * k + j], device_id=peer, device_id_type=MESH)

        mine = [pltpu.make_async_copy(in_refs[k].at[me], out_refs[k].at[me], local_sems.at[k]) for k in range(n)]
        sends = [copy(k, j) for j in range(7) for k in range(n)]
        for cp in mine + sends:
            cp.start()
        for k in range(n):
            for j in range(7):
                landing(k, j).wait_recv()
        for cp in sends:
            cp.wait_send()
        for cp in mine:
            cp.wait()

    return pl.pallas_call(
        body, name=name, in_specs=[HBM_SPEC] * n, out_specs=[HBM_SPEC] * n,
        out_shape=[jax.ShapeDtypeStruct(p.shape, p.dtype) for p in parts],
        scratch_shapes=[pltpu.SemaphoreType.DMA((7 * n,)), pltpu.SemaphoreType.DMA((7 * n,)),
                        pltpu.SemaphoreType.DMA((n,))],
    )(*parts)


def _cols_whole(g, i):
    w = g[:, i]
    return jnp.transpose(w, (1, 0, 2)).reshape(w.shape[1], N_DEV * w.shape[2])


def _rows_whole(g, i):
    w = g[:, i]
    return w.reshape(N_DEV * w.shape[1], w.shape[2])


def _cols_parts(dws):
    out = []
    for dw in dws:
        k, n8 = dw.shape
        out.append(jnp.transpose(dw.reshape(k, N_DEV, n8 // N_DEV), (1, 0, 2)))
    return jnp.stack(out, axis=1)


def _rows_parts(dws):
    out = []
    for dw in dws:
        r8, c = dw.shape
        out.append(dw.reshape(N_DEV, r8 // N_DEV, c))
    return jnp.stack(out, axis=1)


def _pack_rows(arrays):
    rows = []
    for a in arrays:
        flat = a.reshape(-1).astype(F32)
        flat = jnp.pad(flat, [(0, (-flat.shape[0]) % PACK_TILE)])
        rows.append(flat.reshape(-1, V7X_LANES))
    return jnp.concatenate(rows, axis=0)


def _unpack_rows(packed, shapes):
    out, row = [], 0
    for s in shapes:
        size = math.prod(s)
        n_rows = -(-size // PACK_TILE) * 8
        out.append(packed[row:row + n_rows].reshape(-1)[:size].reshape(s))
        row += n_rows
    return out


def _merge2d(a):
    return a.reshape(-1, a.shape[-1])


def _ffn_fwd(tag, x, g, w_up, w_down):
    f = w_down.shape[0]
    n = _rms_fwd(f"{tag}_norm", x, g)
    gu = _mm(f"{tag}_up", n, w_up)
    act = _swiglu(f"{tag}_swiglu", gu, f)
    out = _mm(f"{tag}_down", act, w_down, res=x, scale=0.5, out_dtype=F32)
    return out, (x, n, gu, act)


def _ffn_bwd(tag, dres, saved, g, w_up, w_down):
    x, n, gu, act = saved
    dres32, dres16 = dres
    f = w_down.shape[0]
    dact = _mm(f"{tag}_down_dx", dres16, w_down, tb=True, scale=0.5)
    d_down = _mm(f"{tag}_down_dw", act, dres16, ta=True, scale=0.5)
    dgu = _swiglu_bwd(f"{tag}_swiglu_bwd", dact, gu, f)
    d_up = _mm(f"{tag}_up_dw", n, dgu, ta=True)
    dn = _mm(f"{tag}_up_dx", dgu, w_up, tb=True)
    dx, dg = _rms_bwd(f"{tag}_norm_bwd", x, g, dn, dres32)
    return dx, dg, d_up, d_down


def _conv_mixer_fwd(tag, x, g, w_in, w_conv, w_out, n_seq, seq):
    h = _rms_fwd(f"{tag}_norm", x, g)
    cbv = _mm(f"{tag}_in", h, w_in)
    z = _conv_fwd(f"{tag}_conv", cbv, w_conv, n_seq, seq)
    out = _mm(f"{tag}_out", z, w_out, res=x, out_dtype=F32)
    return out, (x, h, cbv, z)


def _conv_mixer_bwd(tag, dres, saved, g, w_in, w_conv, w_out, n_seq, seq):
    x, h, cbv, z = saved
    dres32, dres16 = dres
    dz = _mm(f"{tag}_out_dx", dres16, w_out, tb=True)
    d_out = _mm(f"{tag}_out_dw", z, dres16, ta=True)
    dc, db, dv, d_conv = _conv_bwd(f"{tag}_conv_bwd", dz, cbv, w_conv, n_seq, seq)
    dcbv = jnp.concatenate([dc, db, dv], axis=1)
    d_in = _mm(f"{tag}_in_dw", h, dcbv, ta=True)
    dh = _mm(f"{tag}_in_dx", dcbv, w_in, tb=True)
    dx, dg = _rms_bwd(f"{tag}_norm_bwd", x, g, dh, dres32)
    return dx, dg, d_in, d_conv, d_out


def _s5_mixer_fwd(tag, x, g, ssm, dskip, w_glu, n_seq, seq):
    a_re, a_im, log_dt, b_re, b_im, c_re, c_im = ssm
    groups, p, hh = b_re.shape
    gb = S5_CHANNELS // hh
    disc, disc_vjp = jax.vjp(_s5_discretize, a_re, a_im, log_dt, b_re, b_im)
    abar_re, abar_im, bbar_re, bbar_im = disc
    mats = (_block_diag_in(bbar_re, gb).astype(BF16), _block_diag_in(bbar_im, gb).astype(BF16),
            _block_diag_out(c_re, gb).astype(BF16), _block_diag_out(c_im, gb).astype(BF16),
            abar_re.reshape(1, groups * p), abar_im.reshape(1, groups * p), dskip)
    d = x.shape[1]
    h = _rms_fwd(f"{tag}_norm", x, g)
    s_re, s_im, ypre, z = _s5_fwd(f"{tag}_scan", h, *mats, n_seq, seq)
    vg = _mm(f"{tag}_glu", z, w_glu)
    out = _glu_res(f"{tag}_glu_act", vg, x, d)
    return out, (x, h, s_re, s_im, ypre, z, vg, mats, disc_vjp, (groups, p, hh, gb))


def _s5_mixer_bwd(tag, dres, saved, g, w_glu, n_seq, seq):
    x, h, s_re, s_im, ypre, z, vg, mats, disc_vjp, (groups, p, hh, gb) = saved
    d = x.shape[1]
    dres32, _ = dres
    dvg = _glu_bwd(f"{tag}_glu_act_bwd", dres32, vg, d)
    d_glu = _mm(f"{tag}_glu_dw", z, dvg, ta=True)
    dz = _mm(f"{tag}_glu_dx", dvg, w_glu, tb=True)
    dh, dbin_re, dbin_im, dcout_re, dcout_im, dabar_re, dabar_im, d_skip = _s5_bwd(
        f"{tag}_scan_bwd", dz, ypre, h, s_re, s_im, *mats, n_seq, seq)
    d_are, d_aim, d_logdt, d_bre, d_bim = disc_vjp((
        dabar_re.reshape(groups, p), dabar_im.reshape(groups, p),
        _block_diag_in_t(dbin_re, gb, p, hh), _block_diag_in_t(dbin_im, gb, p, hh)))
    d_cre = _block_diag_out_t(dcout_re, gb, p, hh)
    d_cim = _block_diag_out_t(dcout_im, gb, p, hh)
    dx, dg = _rms_bwd(f"{tag}_norm_bwd", x, g, dh, dres32)
    return dx, dg, (d_are, d_aim, d_logdt, d_bre, d_bim, d_cre, d_cim), d_skip, d_glu


def _xattn_fwd(tag, x, mem, g_q, g_mem, w_q, w_kv, w_o, n_seq, seq, mlen, heads):
    n = _rms_fwd(f"{tag}_norm", x, g_q)
    q = _mm(f"{tag}_q", n, w_q)
    mem_n = _rms_fwd(f"{tag}_mem_norm", mem, g_mem)
    kv = _mm(f"{tag}_kv", mem_n, w_kv)
    o = _attn_fwd(f"{tag}_attn", q, kv, n_seq, seq, mlen, heads)
    out = _mm(f"{tag}_o", o, w_o, res=x, out_dtype=F32)
    return out, (x, n, q, mem_n, kv, o)


def _xattn_bwd(tag, dres, saved, mem, g_q, g_mem, w_q, w_kv, w_o, n_seq, seq, mlen, heads):
    x, n, q, mem_n, kv, o = saved
    dres32, dres16 = dres
    do = _mm(f"{tag}_o_dx", dres16, w_o, tb=True)
    d_o = _mm(f"{tag}_o_dw", o, dres16, ta=True)
    dq, dk, dv = _attn_bwd(f"{tag}_attn_bwd", q, kv, do, n_seq, seq, mlen, heads)
    dkv = jnp.concatenate([dk, dv], axis=1)
    d_q = _mm(f"{tag}_q_dw", n, dq, ta=True)
    dn = _mm(f"{tag}_q_dx", dq, w_q, tb=True)
    d_kv = _mm(f"{tag}_kv_dw", mem_n, dkv, ta=True)
    dmem_n = _mm(f"{tag}_kv_dx", dkv, w_kv, tb=True)
    _, dg_mem = _rms_bwd(f"{tag}_mem_norm_bwd", mem, g_mem, dmem_n)
    dx, dg_q = _rms_bwd(f"{tag}_norm_bwd", x, g_q, dn, dres32)
    return dx, dg_q, dg_mem, d_q, d_kv, d_o


WEIGHT_NAMES = ("norm_g", "final_g", "ffn1_up", "ffn1_down", "ffn2_up", "ffn2_down", "conv_w_in", "conv_w",
                "conv_w_out", "ssm_a_re", "ssm_a_im", "ssm_log_dt", "ssm_b_re", "ssm_b_im", "ssm_c_re", "ssm_c_im",
                "ssm_d", "ssm_w_glu", "xa_w_q", "xa_w_kv", "xa_w_o")
MATRICES = ("ffn1_up", "ffn1_down", "ffn2_up", "ffn2_down", "conv_w_in", "conv_w_out", "ssm_w_glu", "xa_w_q",
            "xa_w_kv", "xa_w_o")
COLUMN_SHARDED = ("ffn1_up", "ffn2_up", "conv_w_in", "ssm_w_glu", "xa_w_kv")
SMALL_SHARDED = ("norm_g", "conv_w", "ssm_d")
REPLICATED = ("ssm_a_re", "ssm_a_im", "ssm_log_dt", "ssm_b_re", "ssm_b_im", "ssm_c_re", "ssm_c_im", "final_g")


def kernel(x, mem, norm_g, final_g, ffn1_up, ffn1_down, ffn2_up, ffn2_down, conv_w_in, conv_w, conv_w_out, ssm_a_re, ssm_a_im, ssm_log_dt, ssm_b_re, ssm_b_im, ssm_c_re, ssm_c_im, ssm_d, ssm_w_glu, xa_w_q, xa_w_kv, xa_w_o, loss_target, m_norm_g, m_final_g, m_ffn1_up, m_ffn1_down, m_ffn2_up, m_ffn2_down, m_conv_w_in, m_conv_w, m_conv_w_out, m_ssm_a_re, m_ssm_a_im, m_ssm_log_dt, m_ssm_b_re, m_ssm_b_im, m_ssm_c_re, m_ssm_c_im, m_ssm_d, m_ssm_w_glu, m_xa_w_q, m_xa_w_kv, m_xa_w_o, v_norm_g, v_final_g, v_ffn1_up, v_ffn1_down, v_ffn2_up, v_ffn2_down, v_conv_w_in, v_conv_w, v_conv_w_out, v_ssm_a_re, v_ssm_a_im, v_ssm_log_dt, v_ssm_b_re, v_ssm_b_im, v_ssm_c_re, v_ssm_c_im, v_ssm_d, v_ssm_w_glu, v_xa_w_q, v_xa_w_kv, v_xa_w_o):
    w = dict(norm_g=norm_g, final_g=final_g, ffn1_up=ffn1_up, ffn1_down=ffn1_down, ffn2_up=ffn2_up,
             ffn2_down=ffn2_down, conv_w_in=conv_w_in, conv_w=conv_w, conv_w_out=conv_w_out, ssm_a_re=ssm_a_re,
             ssm_a_im=ssm_a_im, ssm_log_dt=ssm_log_dt, ssm_b_re=ssm_b_re, ssm_b_im=ssm_b_im, ssm_c_re=ssm_c_re,
             ssm_c_im=ssm_c_im, ssm_d=ssm_d, ssm_w_glu=ssm_w_glu, xa_w_q=xa_w_q, xa_w_kv=xa_w_kv, xa_w_o=xa_w_o)
    mom = dict(norm_g=m_norm_g, final_g=m_final_g, ffn1_up=m_ffn1_up, ffn1_down=m_ffn1_down, ffn2_up=m_ffn2_up,
               ffn2_down=m_ffn2_down, conv_w_in=m_conv_w_in, conv_w=m_conv_w, conv_w_out=m_conv_w_out,
               ssm_a_re=m_ssm_a_re, ssm_a_im=m_ssm_a_im, ssm_log_dt=m_ssm_log_dt, ssm_b_re=m_ssm_b_re,
               ssm_b_im=m_ssm_b_im, ssm_c_re=m_ssm_c_re, ssm_c_im=m_ssm_c_im, ssm_d=m_ssm_d, ssm_w_glu=m_ssm_w_glu,
               xa_w_q=m_xa_w_q, xa_w_kv=m_xa_w_kv, xa_w_o=m_xa_w_o)
    var = dict(norm_g=v_norm_g, final_g=v_final_g, ffn1_up=v_ffn1_up, ffn1_down=v_ffn1_down, ffn2_up=v_ffn2_up,
               ffn2_down=v_ffn2_down, conv_w_in=v_conv_w_in, conv_w=v_conv_w, conv_w_out=v_conv_w_out,
               ssm_a_re=v_ssm_a_re, ssm_a_im=v_ssm_a_im, ssm_log_dt=v_ssm_log_dt, ssm_b_re=v_ssm_b_re,
               ssm_b_im=v_ssm_b_im, ssm_c_re=v_ssm_c_re, ssm_c_im=v_ssm_c_im, ssm_d=v_ssm_d, ssm_w_glu=v_ssm_w_glu,
               xa_w_q=v_xa_w_q, xa_w_kv=v_xa_w_kv, xa_w_o=v_xa_w_o)

    n_seq, seq, d = x.shape
    mlen = mem.shape[1]
    depth, n_norms = norm_g.shape[0], norm_g.shape[1]
    heads = 4
    tokens = n_seq * seq
    x2 = x.reshape(tokens, d)
    mem2 = mem.reshape(n_seq * mlen, d)
    tgt2 = loss_target.reshape(tokens, d)

    small_shapes = [w[k].shape for k in SMALL_SHARDED]
    small_rows = [_merge2d(w[k]) for k in SMALL_SHARDED]
    small_counts = [s.shape[0] for s in small_rows]
    small = jnp.concatenate(small_rows, axis=0)
    small = jnp.pad(small, [(0, (-small.shape[0]) % 8), (0, 0)])
    gathered = _all_gather("gather_weights", [w[k].astype(BF16) for k in MATRICES] + [small])
    gw = dict(zip(MATRICES, gathered[:-1]))
    small_all = gathered[-1]

    def small_whole(idx):
        start = sum(small_counts[:idx])
        part = small_all[:, start:start + small_counts[idx]]
        lead = small_shapes[idx][:-1]
        part = part.reshape((N_DEV,) + lead + (part.shape[-1],))
        part = jnp.moveaxis(part, 0, -2)
        return part.reshape(lead + (N_DEV * part.shape[-1],))

    norm_all = small_whole(0)
    conv_all = small_whole(1)
    dskip_all = small_whole(2)

    def whole(name, i):
        return _cols_whole(gw[name], i) if name in COLUMN_SHARDED else _rows_whole(gw[name], i)

    saved = []
    cur = x2
    for i in range(depth):
        g = [norm_all[i, k].reshape(1, d) for k in range(n_norms)]
        j = i // 2
        lw = dict(up1=whole("ffn1_up", i), down1=whole("ffn1_down", i), up2=whole("ffn2_up", i),
                  down2=whole("ffn2_down", i), q=whole("xa_w_q", i), kv=whole("xa_w_kv", i), o=whole("xa_w_o", i))
        cur, s_ffn1 = _ffn_fwd(f"l{i}_ffn1", cur, g[0], lw["up1"], lw["down1"])
        if i % 2 == 0:
            lw.update(w_in=whole("conv_w_in", j), w_out=whole("conv_w_out", j), w_conv=conv_all[j])
            cur, s_mix = _conv_mixer_fwd(f"l{i}_conv", cur, g[1], lw["w_in"], lw["w_conv"], lw["w_out"], n_seq, seq)
        else:
            lw.update(glu=whole("ssm_w_glu", j))
            ssm = tuple(w[k][j] for k in ("ssm_a_re", "ssm_a_im", "ssm_log_dt", "ssm_b_re", "ssm_b_im",
                                          "ssm_c_re", "ssm_c_im"))
            cur, s_mix = _s5_mixer_fwd(f"l{i}_s5", cur, g[1], ssm, dskip_all[j].reshape(1, d), lw["glu"], n_seq, seq)
        cur, s_xa = _xattn_fwd(f"l{i}_xa", cur, mem2, g[2], g[3], lw["q"], lw["kv"], lw["o"], n_seq, seq, mlen, heads)
        cur, s_ffn2 = _ffn_fwd(f"l{i}_ffn2", cur, g[4], lw["up2"], lw["down2"])
        saved.append((g, lw, s_ffn1, s_mix, s_xa, s_ffn2))

    dres, err2, d_final = _final_loss("loss_head", cur, final_g.reshape(1, d), tgt2)
    loss = lax.psum(0.5 * jnp.sum(err2) / d, ("x", "y", "c"))

    gm = {k: [None] * w[k].shape[0] for k in MATRICES}
    d_norm = [[None] * n_norms for _ in range(depth)]
    d_conv = [None] * conv_w.shape[0]
    d_skip = [None] * ssm_d.shape[0]
    d_ssm = [None] * ssm_a_re.shape[0]
    for i in reversed(range(depth)):
        g, lw, s_ffn1, s_mix, s_xa, s_ffn2 = saved[i]
        j = i // 2
        dres, d_norm[i][4], gm["ffn2_up"][i], gm["ffn2_down"][i] = _ffn_bwd(
            f"l{i}_ffn2", dres, s_ffn2, g[4], lw["up2"], lw["down2"])
        dres, d_norm[i][2], d_norm[i][3], gm["xa_w_q"][i], gm["xa_w_kv"][i], gm["xa_w_o"][i] = _xattn_bwd(
            f"l{i}_xa", dres, s_xa, mem2, g[2], g[3], lw["q"], lw["kv"], lw["o"], n_seq, seq, mlen, heads)
        if i % 2 == 0:
            dres, d_norm[i][1], gm["conv_w_in"][j], d_conv[j], gm["conv_w_out"][j] = _conv_mixer_bwd(
                f"l{i}_conv", dres, s_mix, g[1], lw["w_in"], lw["w_conv"], lw["w_out"], n_seq, seq)
        else:
            dres, d_norm[i][1], d_ssm[j], d_skip[j], gm["ssm_w_glu"][j] = _s5_mixer_bwd(
                f"l{i}_s5", dres, s_mix, g[1], lw["glu"], n_seq, seq)
        dres, d_norm[i][0], gm["ffn1_up"][i], gm["ffn1_down"][i] = _ffn_bwd(
            f"l{i}_ffn1", dres, s_ffn1, g[0], lw["up1"], lw["down1"])
    grad_x = dres[0].reshape(n_seq, seq, d)

    parts = [(_cols_parts if k in COLUMN_SHARDED else _rows_parts)(gm[k]) for k in MATRICES]
    d_norm_all = jnp.stack([jnp.concatenate(row, axis=0) for row in d_norm])
    d_conv_all = jnp.stack(d_conv)
    d_skip_all = jnp.concatenate(d_skip, axis=0)

    def small_parts(full):
        lead = full.shape[:-1]
        t = full.reshape(lead + (N_DEV, full.shape[-1] // N_DEV))
        t = jnp.moveaxis(t, -2, 0)
        return t.reshape(N_DEV, -1, t.shape[-1])

    small_g = jnp.concatenate([small_parts(a) for a in (d_norm_all, d_conv_all, d_skip_all)], axis=1)
    small_g = jnp.pad(small_g, [(0, 0), (0, (-small_g.shape[1]) % 8), (0, 0)])
    received = _all_to_all("exchange_grads", parts + [small_g])

    rep_grads = [jnp.stack([d_ssm[j][k] for j in range(len(d_ssm))]) for k in range(7)] + [d_final.reshape(-1)]
    rep_shapes = [w[k].shape for k in REPLICATED]
    rep_all = _all_gather("gather_replicated_grads", [_pack_rows(rep_grads)])[0]

    grads, deltas, new_m, new_v = {}, {}, {}, {}
    for k, rec in zip(MATRICES, received[:-1]):
        shape = w[k].shape
        out = _adamw(f"adamw_{k}", rec.reshape(N_DEV, -1, shape[-1]), _merge2d(w[k]), _merge2d(mom[k]),
                     _merge2d(var[k]))
        grads[k], deltas[k], new_m[k], new_v[k] = [o.reshape(shape) for o in out]

    def small_local(src):
        rows = jnp.concatenate([_merge2d(src[k]) for k in SMALL_SHARDED], axis=0)
        return jnp.pad(rows, [(0, (-rows.shape[0]) % 8), (0, 0)])

    out = _adamw("adamw_small", received[-1], small, small_local(mom), small_local(var))
    for res, o in zip((grads, deltas, new_m, new_v), out):
        start = 0
        for k, cnt, shape in zip(SMALL_SHARDED, small_counts, small_shapes):
            res[k] = o[start:start + cnt].reshape(shape)
            start += cnt

    out = _adamw("adamw_replicated", rep_all, _pack_rows([w[k] for k in REPLICATED]),
                 _pack_rows([mom[k] for k in REPLICATED]), _pack_rows([var[k] for k in REPLICATED]))
    for res, o in zip((grads, deltas, new_m, new_v), out):
        for k, a in zip(REPLICATED, _unpack_rows(o, rep_shapes)):
            res[k] = a

    return (loss, grad_x, *[grads[k] for k in WEIGHT_NAMES], *[deltas[k] for k in WEIGHT_NAMES],
            *[new_m[k] for k in WEIGHT_NAMES], *[new_v[k] for k in WEIGHT_NAMES])
```

```python
import math

import jax
import jax.numpy as jnp
from jax import lax
from jax.experimental import pallas as pl
from jax.experimental.pallas import tpu as pltpu

F32 = jnp.float32
BF16 = jnp.bfloat16
MESH = pl.DeviceIdType.MESH
N_DEV = 8

NORM_EPS = 1e-6
EIG_CLIP = -1e-4
CONV_WIDTH = 3
ADAM_LR = 0.001
ADAM_B1 = 0.9
ADAM_B2 = 0.999
ADAM_EPS = 1e-08
ADAM_WD = 0.01
ADAM_STEP = 10
GELU_C = math.sqrt(2.0 / math.pi)
GELU_A = 0.044715

V7X_LANES = 128
V7X_VMEM_LIMIT = 56 * 1024 * 1024
S5_CHANNELS = 128
PACK_TILE = 8 * V7X_LANES

HBM_SPEC = pl.BlockSpec(memory_space=pltpu.HBM)
ANY_SPEC = pl.BlockSpec(memory_space=pl.ANY)
SEM_SPEC = pl.BlockSpec(memory_space=pltpu.SEMAPHORE)


def _params(n_grid):
    return pltpu.CompilerParams(dimension_semantics=("arbitrary",) * n_grid, vmem_limit_bytes=V7X_VMEM_LIMIT)


def _pick(n, pref, align):
    if n <= pref:
        return n
    t = (pref // align) * align
    while t >= align:
        if n % t == 0:
            return t
        t -= align
    raise ValueError(f"no tile for {n} (pref {pref}, align {align})")


MM_RHS_BLOCK_BYTES = 12 * 1024 * 1024
MM_LHS_BLOCK_BYTES = 6 * 1024 * 1024
MM_ACC_BYTES = 6 * 1024 * 1024
MM_ROWS = 512


def _mm_tiles(m, k, n, a_item, b_item, ta):
    tn = _pick(n, max(V7X_LANES, MM_RHS_BLOCK_BYTES // (k * b_item)), V7X_LANES)
    rows = min(MM_ROWS, MM_ACC_BYTES // (4 * tn), MM_LHS_BLOCK_BYTES // (k * a_item))
    align = V7X_LANES if ta else 16
    tm = _pick(m, max(align, rows), align)
    return tm, tn


def _mm(name, a, b, *, ta=False, tb=False, out_dtype=BF16, res=None, scale=None, deps=()):
    if ta:
        k, m = a.shape
    else:
        m, k = a.shape
    if tb:
        n, k2 = b.shape
    else:
        k2, n = b.shape
    assert k == k2, (name, a.shape, b.shape)
    tm, tn = _mm_tiles(m, k, n, a.dtype.itemsize, b.dtype.itemsize, ta)
    a_spec = pl.BlockSpec((k, tm), lambda j, i: (0, i)) if ta else pl.BlockSpec((tm, k), lambda j, i: (i, 0))
    b_spec = pl.BlockSpec((tn, k), lambda j, i: (j, 0)) if tb else pl.BlockSpec((k, tn), lambda j, i: (0, j))
    o_spec = pl.BlockSpec((tm, tn), lambda j, i: (i, j))
    dims = (((0 if ta else 1,), (1 if tb else 0,)), ((), ()))
    has_res = res is not None

    def body(*refs):
        a_ref, b_ref = refs[0], refs[1]
        o_ref = refs[-1]
        acc = lax.dot_general(a_ref[...].astype(BF16), b_ref[...].astype(BF16), dims, preferred_element_type=F32)
        if scale is not None:
            acc = acc * scale
        if has_res:
            acc = acc + refs[2][...].astype(F32)
        o_ref[...] = acc.astype(o_ref.dtype)

    ins = [a, b] + ([res] if has_res else []) + list(deps)
    specs = [a_spec, b_spec] + ([o_spec] if has_res else []) + [ANY_SPEC] * len(deps)
    return pl.pallas_call(
        body, name=name, grid=(n // tn, m // tm), in_specs=specs, out_specs=o_spec,
        out_shape=jax.ShapeDtypeStruct((m, n), out_dtype), compiler_params=_params(2),
    )(*ins)


def _rowwise(name, fn, rows, row_ins, par_ins, row_outs, acc_outs=(), tm_pref=256, deps=()):
    tm = _pick(rows, tm_pref, 16)
    in_specs, ins = [], []
    for r in row_ins:
        arr, cb, cw = r if isinstance(r, tuple) else (r, 0, r.shape[1])
        assert arr.shape[0] == rows, (name, arr.shape, rows)
        ins.append(arr)
        in_specs.append(pl.BlockSpec((tm, cw), lambda i, cb=cb: (i, cb)))
    for p in par_ins:
        ins.append(p)
        in_specs.append(pl.BlockSpec(p.shape, lambda i: (0, 0)))
    out_specs = [pl.BlockSpec((tm, c), lambda i: (i, 0)) for c, _ in row_outs]
    out_specs += [pl.BlockSpec((r, c), lambda i: (0, 0)) for r, c in acc_outs]
    out_shape = [jax.ShapeDtypeStruct((rows, c), dt) for c, dt in row_outs]
    out_shape += [jax.ShapeDtypeStruct((r, c), F32) for r, c in acc_outs]
    n_in, n_row = len(ins), len(row_outs)
    ins += list(deps)
    in_specs += [ANY_SPEC] * len(deps)

    def body(*refs):
        vals = [r[...] for r in refs[:n_in]]
        outs = refs[n_in + len(deps):]
        res = fn(*vals)
        if not isinstance(res, (tuple, list)):
            res = (res,)
        for o, v in zip(outs[:n_row], res[:n_row]):
            if isinstance(v, (tuple, list)):
                off = 0
                for piece in v:
                    w = piece.shape[1]
                    o[:, off:off + w] = piece.astype(o.dtype)
                    off += w
            else:
                o[...] = v.astype(o.dtype)
        if len(outs) > n_row:
            @pl.when(pl.program_id(0) == 0)
            def _():
                for o in outs[n_row:]:
                    o[...] = jnp.zeros_like(o)

            for o, v in zip(outs[n_row:], res[n_row:]):
                o[...] += v

    out = pl.pallas_call(
        body, name=name, grid=(rows // tm,), in_specs=in_specs, out_specs=out_specs, out_shape=out_shape,
        compiler_params=_params(1),
    )(*ins)
    return out


def _inv_rms(x):
    return lax.rsqrt(jnp.mean(x * x, axis=-1, keepdims=True) + NORM_EPS)


def _rms_fwd(name, x, g, deps=()):
    def fn(x, g):
        return x * _inv_rms(x) * g

    return _rowwise(name, fn, x.shape[0], [x], [g], [(x.shape[1], BF16)], tm_pref=512, deps=deps)[0]


def _rms_bwd(name, x, g, dn, dres=None):
    d = x.shape[1]

    def fn(x, dn, *rest):
        g = rest[-1]
        dn = dn.astype(F32)
        r = _inv_rms(x)
        xh = x * r
        dg = jnp.sum(dn * xh, axis=0, keepdims=True)
        if dres is None:
            return (dg,)
        dxh = dn * g
        dx = r * (dxh - xh * jnp.mean(dxh * xh, axis=-1, keepdims=True)) + rest[0]
        return dx, dx, dg

    row_ins = [x, dn] + ([] if dres is None else [dres])
    row_outs = [] if dres is None else [(d, F32), (d, BF16)]
    out = _rowwise(name, fn, x.shape[0], row_ins, [g], row_outs, [(1, d)], tm_pref=256)
    return (None, out[0]) if dres is None else ((out[0], out[1]), out[2])


def _sigmoid(x):
    return 1.0 / (1.0 + jnp.exp(-x))


def _swiglu(name, gu, f):
    def fn(gt, up):
        gt = gt.astype(F32)
        return gt * _sigmoid(gt) * up.astype(F32)

    return _rowwise(name, fn, gu.shape[0], [(gu, 0, f), (gu, 1, f)], [], [(f, BF16)])[0]


def _swiglu_bwd(name, dact, gu, f):
    def fn(dact, gt, up):
        dact, gt, up = dact.astype(F32), gt.astype(F32), up.astype(F32)
        sg = _sigmoid(gt)
        dgt = dact * up * (sg * (1.0 + gt * (1.0 - sg)))
        dup = dact * (gt * sg)
        return ((dgt, dup),)

    return _rowwise(name, fn, gu.shape[0], [dact, (gu, 0, f), (gu, 1, f)], [], [(2 * f, BF16)])[0]


def _glu_res(name, vg, x, d):
    def fn(val, gate, x):
        return x + val.astype(F32) * _sigmoid(gate.astype(F32))

    return _rowwise(name, fn, x.shape[0], [(vg, 0, d), (vg, 1, d), x], [], [(d, F32)])[0]


def _glu_bwd(name, dres, vg, d):
    def fn(dres, val, gate):
        val, gate = val.astype(F32), gate.astype(F32)
        sg = _sigmoid(gate)
        return ((dres * sg, dres * val * sg * (1.0 - sg)),)

    return _rowwise(name, fn, dres.shape[0], [dres, (vg, 0, d), (vg, 1, d)], [], [(2 * d, BF16)])[0]


def _final_loss(name, x, g, tgt):
    d = x.shape[1]

    def fn(x, tgt, g):
        r = _inv_rms(x)
        xh = x * r
        err = xh * g - tgt
        dy = err * (1.0 / d)
        dxh = dy * g
        dx = r * (dxh - xh * jnp.mean(dxh * xh, axis=-1, keepdims=True))
        return dx, dx, jnp.sum(err * err, axis=0, keepdims=True), jnp.sum(dy * xh, axis=0, keepdims=True)

    dx, dx16, err2, dg = _rowwise(name, fn, x.shape[0], [x, tgt], [g], [(d, F32), (d, BF16)], [(1, d), (1, d)])
    return (dx, dx16), err2, dg


def _shift_down(u, k):
    rows = lax.broadcasted_iota(jnp.int32, u.shape, 0)
    return jnp.where(rows >= k, pltpu.roll(u, k, 0), 0.0)


def _shift_up(u, k):
    n = u.shape[0]
    rows = lax.broadcasted_iota(jnp.int32, u.shape, 0)
    return jnp.where(rows < n - k, pltpu.roll(u, n - k, 0), 0.0)


def _conv_specs(seq, cw, n_cb, swap):
    def at(off):
        if swap:
            return pl.BlockSpec((seq, cw), lambda j, b: (b, off * n_cb + j))
        return pl.BlockSpec((seq, cw), lambda b, j: (b, off * n_cb + j))

    return at


def _conv_fwd(name, cbv, w, n_seq, seq):
    d = w.shape[1]
    cw = _pick(d, 256, V7X_LANES)
    n_cb = d // cw
    at = _conv_specs(seq, cw, n_cb, swap=False)

    def body(c_ref, b_ref, v_ref, w_ref, z_ref):
        u = c_ref[...].astype(F32) * v_ref[...].astype(F32)
        cv = w_ref[0:1, :] * _shift_down(u, 2) + w_ref[1:2, :] * _shift_down(u, 1) + w_ref[2:3, :] * u
        z_ref[...] = (b_ref[...].astype(F32) * cv).astype(z_ref.dtype)

    return pl.pallas_call(
        body, name=name, grid=(n_seq, n_cb),
        in_specs=[at(0), at(1), at(2), pl.BlockSpec((CONV_WIDTH, cw), lambda b, j: (0, j))],
        out_specs=at(0), out_shape=jax.ShapeDtypeStruct((n_seq * seq, d), BF16), compiler_params=_params(2),
    )(cbv, cbv, cbv, w)


def _conv_bwd(name, dz, cbv, w, n_seq, seq):
    d = w.shape[1]
    cw = _pick(d, 256, V7X_LANES)
    n_cb = d // cw
    at = _conv_specs(seq, cw, n_cb, swap=True)

    def body(dz_ref, c_ref, b_ref, v_ref, w_ref, dc_ref, db_ref, dv_ref, dw_ref):
        c, b, v = c_ref[...].astype(F32), b_ref[...].astype(F32), v_ref[...].astype(F32)
        dz = dz_ref[...].astype(F32)
        w0, w1, w2 = w_ref[0:1, :], w_ref[1:2, :], w_ref[2:3, :]
        u = c * v
        u1, u2 = _shift_down(u, 1), _shift_down(u, 2)
        cv = w0 * u2 + w1 * u1 + w2 * u
        db_ref[...] = (dz * cv).astype(db_ref.dtype)
        dcv = dz * b
        du = w2 * dcv + w1 * _shift_up(dcv, 1) + w0 * _shift_up(dcv, 2)
        dc_ref[...] = (du * v).astype(dc_ref.dtype)
        dv_ref[...] = (du * c).astype(dv_ref.dtype)

        @pl.when(pl.program_id(1) == 0)
        def _():
            dw_ref[...] = jnp.zeros_like(dw_ref)

        dw_ref[0:1, :] += jnp.sum(dcv * u2, axis=0, keepdims=True)
        dw_ref[1:2, :] += jnp.sum(dcv * u1, axis=0, keepdims=True)
        dw_ref[2:3, :] += jnp.sum(dcv * u, axis=0, keepdims=True)

    act = jax.ShapeDtypeStruct((n_seq * seq, d), BF16)
    return pl.pallas_call(
        body, name=name, grid=(n_cb, n_seq),
        in_specs=[at(0), at(0), at(1), at(2), pl.BlockSpec((CONV_WIDTH, cw), lambda j, b: (0, j))],
        out_specs=[at(0), at(0), at(0), pl.BlockSpec((CONV_WIDTH, cw), lambda j, b: (0, j))],
        out_shape=[act, act, act, jax.ShapeDtypeStruct((CONV_WIDTH, d), F32)], compiler_params=_params(2),
    )(dz, cbv, cbv, cbv, w)


def _s5_discretize(a_re, a_im, log_dt, b_re, b_im):
    lam_re = jnp.minimum(a_re, EIG_CLIP)
    lam_im = a_im
    dt = jnp.exp(log_dt)[:, None]
    mag = jnp.exp(lam_re * dt)
    abar_re = mag * jnp.cos(lam_im * dt)
    abar_im = mag * jnp.sin(lam_im * dt)
    den = lam_re * lam_re + lam_im * lam_im
    num_re = abar_re - 1.0
    num_im = abar_im
    coef_re = (num_re * lam_re + num_im * lam_im) / den
    coef_im = (num_im * lam_re - num_re * lam_im) / den
    bbar_re = coef_re[..., None] * b_re - coef_im[..., None] * b_im
    bbar_im = coef_re[..., None] * b_im + coef_im[..., None] * b_re
    return abar_re, abar_im, bbar_re, bbar_im


def _block_diag_in(bbar, gb):
    g, p, h = bbar.shape
    t = jnp.transpose(bbar.reshape(g // gb, gb, p, h), (0, 1, 3, 2))
    return jnp.einsum("cghp,gk->cghkp", t, jnp.eye(gb, dtype=bbar.dtype)).reshape(g // gb, gb * h, gb * p)


def _block_diag_in_t(blk, gb, p, h):
    nb = blk.shape[0]
    t = jnp.einsum("cghkp,gk->cghp", blk.reshape(nb, gb, h, gb, p), jnp.eye(gb, dtype=blk.dtype))
    return jnp.transpose(t, (0, 1, 3, 2)).reshape(nb * gb, p, h)


def _block_diag_out(c, gb):
    g, h, p = c.shape
    t = jnp.transpose(c.reshape(g // gb, gb, h, p), (0, 1, 3, 2))
    return jnp.einsum("cgph,gk->cgpkh", t, jnp.eye(gb, dtype=c.dtype)).reshape(g // gb, gb * p, gb * h)


def _block_diag_out_t(blk, gb, p, h):
    nb = blk.shape[0]
    t = jnp.einsum("cgpkh,gk->cgph", blk.reshape(nb, gb, p, gb, h), jnp.eye(gb, dtype=blk.dtype))
    return jnp.transpose(t, (0, 1, 3, 2)).reshape(nb * gb, h, p)


def _gelu(y):
    return 0.5 * y * (1.0 + jnp.tanh(GELU_C * (y + GELU_A * y * y * y)))


def _gelu_grad(y):
    th = jnp.tanh(GELU_C * (y + GELU_A * y * y * y))
    return 0.5 * (1.0 + th) + 0.5 * y * (1.0 - th * th) * GELU_C * (1.0 + 3.0 * GELU_A * y * y)


def _dot(a, b, ca, cb):
    return lax.dot_general(a.astype(BF16), b.astype(BF16), (((ca,), (cb,)), ((), ())), preferred_element_type=F32)


def _s5_specs(seq, ch, sb):
    act = pl.BlockSpec((seq, ch), lambda j, b: (b, j))
    state = pl.BlockSpec((seq, sb), lambda j, b: (b, j))
    w_in = pl.BlockSpec((None, ch, sb), lambda j, b: (j, 0, 0))
    w_out = pl.BlockSpec((None, sb, ch), lambda j, b: (j, 0, 0))
    lane_s = pl.BlockSpec((1, sb), lambda j, b: (0, j))
    lane_c = pl.BlockSpec((1, ch), lambda j, b: (0, j))
    return act, state, w_in, w_out, lane_s, lane_c


def _s5_fwd(name, h, bin_re, bin_im, cout_re, cout_im, abar_re, abar_im, dskip, n_seq, seq):
    t, d = h.shape
    nb, ch, sb = bin_re.shape
    act, state, w_in, w_out, lane_s, lane_c = _s5_specs(seq, ch, sb)

    def body(h_ref, bre_ref, bim_ref, cre_ref, cim_ref, ar_ref, ai_ref, d_ref, sre_ref, sim_ref, y_ref, z_ref):
        u = h_ref[...]
        sre_ref[...] = _dot(u, bre_ref[...], 1, 0)
        sim_ref[...] = _dot(u, bim_ref[...], 1, 0)
        ar, ai = ar_ref[...], ai_ref[...]

        def step(i, carry):
            sr, si = carry
            row = pl.ds(i, 1)
            nr = ar * sr - ai * si + sre_ref[row, :]
            ni = ar * si + ai * sr + sim_ref[row, :]
            sre_ref[row, :] = nr
            sim_ref[row, :] = ni
            return nr, ni

        zero = jnp.zeros((1, sb), F32)
        lax.fori_loop(0, seq, step, (zero, zero), unroll=8)
        y = _dot(sre_ref[...], cre_ref[...], 1, 0) - _dot(sim_ref[...], cim_ref[...], 1, 0)
        y = y + d_ref[...] * u.astype(F32)
        y_ref[...] = y
        z_ref[...] = _gelu(y).astype(z_ref.dtype)

    return pl.pallas_call(
        body, name=name, grid=(nb, n_seq),
        in_specs=[act, w_in, w_in, w_out, w_out, lane_s, lane_s, lane_c],
        out_specs=[state, state, act, act],
        out_shape=[jax.ShapeDtypeStruct((t, nb * sb), F32), jax.ShapeDtypeStruct((t, nb * sb), F32),
                   jax.ShapeDtypeStruct((t, d), F32), jax.ShapeDtypeStruct((t, d), BF16)],
        compiler_params=_params(2),
    )(h, bin_re, bin_im, cout_re, cout_im, abar_re, abar_im, dskip)


def _s5_bwd(name, dz, ypre, h, s_re, s_im, bin_re, bin_im, cout_re, cout_im, abar_re, abar_im, dskip, n_seq, seq):
    t, d = h.shape
    nb, ch, sb = bin_re.shape
    act, state, w_in, w_out, lane_s, lane_c = _s5_specs(seq, ch, sb)

    def body(dz_ref, y_ref, h_ref, sre_ref, sim_ref, bre_ref, bim_ref, cre_ref, cim_ref, ar_ref, ai_ref, d_ref,
             dh_ref, dbre_ref, dbim_ref, dcre_ref, dcim_ref, dar_ref, dai_ref, dd_ref, gre, gim):
        first = pl.program_id(1) == 0
        u = h_ref[...].astype(F32)
        dy = dz_ref[...].astype(F32) * _gelu_grad(y_ref[...])
        gre[...] = _dot(dy, cre_ref[...], 1, 1)
        gim[...] = -_dot(dy, cim_ref[...], 1, 1)
        ar, ai = ar_ref[...], ai_ref[...]

        def step(i, carry):
            gr, gi = carry
            row = pl.ds(seq - 1 - i, 1)
            nr = gre[row, :] + ar * gr + ai * gi
            ni = gim[row, :] - ai * gr + ar * gi
            gre[row, :] = nr
            gim[row, :] = ni
            return nr, ni

        zero = jnp.zeros((1, sb), F32)
        lax.fori_loop(0, seq, step, (zero, zero), unroll=8)

        g_re, g_im = gre[...], gim[...]
        s_re, s_im = sre_ref[...], sim_ref[...]
        p_re, p_im = _shift_down(s_re, 1), _shift_down(s_im, 1)
        dar = jnp.sum(g_re * p_re + g_im * p_im, axis=0, keepdims=True)
        dai = jnp.sum(g_im * p_re - g_re * p_im, axis=0, keepdims=True)
        dbre = _dot(u, g_re, 0, 0)
        dbim = _dot(u, g_im, 0, 0)
        dcre = _dot(s_re, dy, 0, 0)
        dcim = -_dot(s_im, dy, 0, 0)
        ddd = jnp.sum(dy * u, axis=0, keepdims=True)
        dh_ref[...] = _dot(g_re, bre_ref[...], 1, 1) + _dot(g_im, bim_ref[...], 1, 1) + d_ref[...] * dy

        @pl.when(first)
        def _():
            dar_ref[...] = dar
            dai_ref[...] = dai
            dbre_ref[...] = dbre
            dbim_ref[...] = dbim
            dcre_ref[...] = dcre
            dcim_ref[...] = dcim
            dd_ref[...] = ddd

        @pl.when(jnp.logical_not(first))
        def _():
            dar_ref[...] += dar
            dai_ref[...] += dai
            dbre_ref[...] += dbre
            dbim_ref[...] += dbim
            dcre_ref[...] += dcre
            dcim_ref[...] += dcim
            dd_ref[...] += ddd

    return pl.pallas_call(
        body, name=name, grid=(nb, n_seq),
        in_specs=[act, act, act, state, state, w_in, w_in, w_out, w_out, lane_s, lane_s, lane_c],
        out_specs=[act, w_in, w_in, w_out, w_out, lane_s, lane_s, lane_c],
        out_shape=[jax.ShapeDtypeStruct((t, d), F32),
                   jax.ShapeDtypeStruct((nb, ch, sb), F32), jax.ShapeDtypeStruct((nb, ch, sb), F32),
                   jax.ShapeDtypeStruct((nb, sb, ch), F32), jax.ShapeDtypeStruct((nb, sb, ch), F32),
                   jax.ShapeDtypeStruct((1, nb * sb), F32), jax.ShapeDtypeStruct((1, nb * sb), F32),
                   jax.ShapeDtypeStruct((1, d), F32)],
        scratch_shapes=[pltpu.VMEM((seq, sb), F32), pltpu.VMEM((seq, sb), F32)],
        compiler_params=_params(2),
    )(dz, ypre, h, s_re, s_im, bin_re, bin_im, cout_re, cout_im, abar_re, abar_im, dskip)


def _softmax_rows(q, k, scale):
    s = _dot(q, k, 1, 1) * scale
    e = jnp.exp(s - jnp.max(s, axis=-1, keepdims=True))
    return e / jnp.sum(e, axis=-1, keepdims=True)


def _attn_fwd(name, q, kv, n_seq, seq, mlen, heads):
    t, d = q.shape
    hd = d // heads
    tq = _pick(seq, 512, 16)
    nq = seq // tq
    scale = hd ** -0.5
    q_spec = pl.BlockSpec((tq, hd), lambda b, h, i: (b * nq + i, h))

    def body(q_ref, k_ref, v_ref, o_ref):
        p = _softmax_rows(q_ref[...], k_ref[...], scale)
        o_ref[...] = _dot(p, v_ref[...], 1, 0).astype(o_ref.dtype)

    return pl.pallas_call(
        body, name=name, grid=(n_seq, heads, nq),
        in_specs=[q_spec, pl.BlockSpec((mlen, hd), lambda b, h, i: (b, h)),
                  pl.BlockSpec((mlen, hd), lambda b, h, i: (b, heads + h))],
        out_specs=q_spec, out_shape=jax.ShapeDtypeStruct((t, d), BF16), compiler_params=_params(3),
    )(q, kv, kv)


def _attn_bwd(name, q, kv, do, n_seq, seq, mlen, heads):
    t, d = q.shape
    hd = d // heads
    tq = _pick(seq, 512, 16)
    nq = seq // tq
    scale = hd ** -0.5
    q_spec = pl.BlockSpec((tq, hd), lambda b, h, i: (b * nq + i, h))
    k_spec = pl.BlockSpec((mlen, hd), lambda b, h, i: (b, h))

    def body(q_ref, k_ref, v_ref, do_ref, dq_ref, dk_ref, dv_ref):
        q, k, v, do = q_ref[...], k_ref[...], v_ref[...], do_ref[...]
        p = _softmax_rows(q, k, scale)
        dp = _dot(do, v, 1, 1)
        ds = p * (dp - jnp.sum(dp * p, axis=-1, keepdims=True)) * scale
        dq_ref[...] = _dot(ds, k, 1, 0).astype(dq_ref.dtype)

        @pl.when(pl.program_id(2) == 0)
        def _():
            dk_ref[...] = jnp.zeros_like(dk_ref)
            dv_ref[...] = jnp.zeros_like(dv_ref)

        dk_ref[...] += _dot(ds, q, 0, 0)
        dv_ref[...] += _dot(p, do, 0, 0)

    return pl.pallas_call(
        body, name=name, grid=(n_seq, heads, nq),
        in_specs=[q_spec, k_spec, pl.BlockSpec((mlen, hd), lambda b, h, i: (b, heads + h)), q_spec],
        out_specs=[q_spec, k_spec, k_spec],
        out_shape=[jax.ShapeDtypeStruct((t, d), BF16), jax.ShapeDtypeStruct((n_seq * mlen, d), F32),
                   jax.ShapeDtypeStruct((n_seq * mlen, d), F32)],
        compiler_params=_params(3),
    )(q, kv, kv, do)


ADAMW_BLOCK_ELEMS = 128 * 1024


def _adamw(name, parts, w, m, v):
    n_layers = len(parts)
    _, r, c = parts[0].shape
    assert w.shape == (n_layers * r, c), (name, w.shape, parts[0].shape)
    tr = _pick(r, max(16, ADAMW_BLOCK_ELEMS // c // 16 * 16), 8)
    nt = r // tr
    spec = pl.BlockSpec((tr, c), lambda l, i: (l * nt + i, 0))
    c1 = 1.0 - ADAM_B1 ** ADAM_STEP
    c2 = 1.0 - ADAM_B2 ** ADAM_STEP

    def parts_spec(q):
        return pl.BlockSpec((N_DEV, tr, c), lambda l, i: (0, jnp.where(l == q, i, jnp.where(l > q, nt - 1, 0)), 0))

    def body(*refs):
        p_refs = refs[:n_layers]
        w_ref, m_ref, v_ref, g_ref, d_ref, nm_ref, nv_ref = refs[n_layers:]

        def update(p_ref):
            g = p_ref[0].astype(F32)
            for k in range(1, N_DEV):
                g = g + p_ref[k].astype(F32)
            nm = ADAM_B1 * m_ref[...] + (1.0 - ADAM_B1) * g
            nv = ADAM_B2 * v_ref[...] + (1.0 - ADAM_B2) * (g * g)
            g_ref[...] = g
            nm_ref[...] = nm
            nv_ref[...] = nv
            d_ref[...] = -ADAM_LR * ((nm / c1) / (jnp.sqrt(nv / c2) + ADAM_EPS) + ADAM_WD * w_ref[...])

        for q in range(n_layers):
            pl.when(pl.program_id(0) == q)(lambda q=q: update(p_refs[q]))

    out = jax.ShapeDtypeStruct(w.shape, F32)
    return pl.pallas_call(
        body, name=name, grid=(n_layers, nt), in_specs=[parts_spec(q) for q in range(n_layers)] + [spec] * 3,
        out_specs=[spec] * 4, out_shape=[out] * 4, compiler_params=_params(2),
    )(*parts, w, m, v)


def _place():
    x, y, c = lax.axis_index("x"), lax.axis_index("y"), lax.axis_index("c")
    return x, y, c


def _index(px, py, pc):
    return 4 * px + 2 * py + pc


def _all_gather(name, shards):
    n = len(shards)

    def body(*refs):
        in_refs, out_refs = refs[:n], refs[n:2 * n]
        send_sems, recv_sems, local_sems = refs[2 * n:]
        x, y, c = _place()
        me, sibling = (x, y, c), (x, y, 1 - c)
        chips = [(1 - x, y), (x, 1 - y), (1 - x, 1 - y)]

        def slot(k, block):
            return out_refs[k].at[_index(*block)]

        def copy(k, j, block, to, src=None):
            return pltpu.make_async_remote_copy(
                src_ref=slot(k, block) if src is None else src, dst_ref=slot(k, block),
                send_sem=send_sems.at[7 * k + j], recv_sem=recv_sems.at[7 * k + j], device_id=to, device_id_type=MESH)

        mine = [pltpu.make_async_copy(in_refs[k], slot(k, me), local_sems.at[k]) for k in range(n)]
        for cp in mine:
            cp.start()
        first = []
        for k in range(n):
            first.append(copy(k, 0, me, sibling, src=in_refs[k]))
            first += [copy(k, 1 + j, me, (*chip, c), src=in_refs[k]) for j, chip in enumerate(chips)]
        for cp in first:
            cp.start()
        passed = []
        for j, chip in enumerate(chips):
            for k in range(n):
                copy(k, 1 + j, (*chip, c), me).wait_recv()
                cp = copy(k, 4 + j, (*chip, c), sibling)
                cp.start()
                passed.append(cp)
        for k in range(n):
            copy(k, 0, sibling, me).wait_recv()
            for j, chip in enumerate(chips):
                copy(k, 4 + j, (*chip, 1 - c), me).wait_recv()
        for cp in first + passed:
            cp.wait_send()
        for cp in mine:
            cp.wait()

    return pl.pallas_call(
        body, name=name, in_specs=[HBM_SPEC] * n, out_specs=[HBM_SPEC] * n,
        out_shape=[jax.ShapeDtypeStruct((N_DEV,) + s.shape, s.dtype) for s in shards],
        scratch_shapes=[pltpu.SemaphoreType.DMA((7 * n,)), pltpu.SemaphoreType.DMA((7 * n,)),
                        pltpu.SemaphoreType.DMA((n,))],
    )(*shards)


def _all_to_all(name, parts):
    n = len(parts)

    def body(*refs):
        in_refs, out_refs = refs[:n], refs[n:2 * n]
        send_sems, recv_sems, local_sems = refs[2 * n:]
        x, y, c = _place()
        me = _index(x, y, c)
        peers = _xor_peers(x, y, c)

        def copy(k, j):
            peer = peers[j]
            return pltpu.make_async_remote_copy(
                src_ref=in_refs[k].at[_index(*peer)], dst_ref=out_refs[k].at[me],
                send_sem=send_sems.at[7 * k + j], recv_sem=recv_sems.at[7 * k + j], device_id=peer, device_id_type=MESH)

        def landing(k, j):
            peer = peers[j]
            return pltpu.make_async_remote_copy(
                src_ref=in_refs[k].at[me], dst_ref=out_refs[k].at[_index(*peer)],
                send_sem=send_sems.at[7 * k + j], recv_sem=recv_sems.at[7 * k + j], device_id=peer, device_id_type=MESH)

        mine = [pltpu.make_async_copy(in_refs[k].at[me], out_refs[k].at[me], local_sems.at[k]) for k in range(n)]
        sends = [copy(k, j) for j in range(7) for k in range(n)]
        for cp in mine + sends:
            cp.start()
        for k in range(n):
            for j in range(7):
                landing(k, j).wait_recv()
        for cp in sends:
            cp.wait_send()
        for cp in mine:
            cp.wait()

    return pl.pallas_call(
        body, name=name, in_specs=[HBM_SPEC] * n, out_specs=[HBM_SPEC] * n,
        out_shape=[jax.ShapeDtypeStruct(p.shape, p.dtype) for p in parts],
        scratch_shapes=[pltpu.SemaphoreType.DMA((7 * n,)), pltpu.SemaphoreType.DMA((7 * n,)),
                        pltpu.SemaphoreType.DMA((n,))],
    )(*parts)


def _xor_peers(x, y, c):
    peers = []
    for r in range(1, N_DEV):
        rx, ry, rc = (r >> 2) & 1, (r >> 1) & 1, r & 1
        peers.append((1 - x if rx else x, 1 - y if ry else y, 1 - c if rc else c))
    return peers


def _exchange_start(name, srcs, lands, src_by_peer, deps=()):
    n = len(srcs)

    def body(*refs):
        src_refs, land_refs = refs[:n], refs[n:2 * n]
        send_sems, recv_sems = refs[2 * n + len(deps)], refs[2 * n + len(deps) + 1]
        token = refs[-1]
        x, y, c = _place()
        me = _index(x, y, c)
        peers = _xor_peers(x, y, c)
        for k in range(n):
            for j, peer in enumerate(peers):
                src = src_refs[k].at[_index(*peer)] if src_by_peer else src_refs[k]
                pltpu.make_async_remote_copy(
                    src_ref=src, dst_ref=land_refs[k].at[me], send_sem=send_sems.at[7 * k + j],
                    recv_sem=recv_sems.at[7 * k + j], device_id=peer, device_id_type=MESH).start()
        token[...] = jnp.zeros_like(token)

    thru = [pltpu.HBM(a.shape, a.dtype) for a in list(srcs) + list(lands)]
    out = pl.pallas_call(
        body, name=name,
        out_shape=(pltpu.SemaphoreType.DMA((7 * n,)), pltpu.SemaphoreType.DMA((7 * n,)), *thru,
                   jax.ShapeDtypeStruct((8, V7X_LANES), F32)),
        in_specs=[HBM_SPEC] * (2 * n) + [ANY_SPEC] * len(deps),
        out_specs=(SEM_SPEC, SEM_SPEC, *([HBM_SPEC] * (2 * n)), pl.BlockSpec(memory_space=pltpu.VMEM)),
        input_output_aliases={k: 2 + k for k in range(2 * n)},
        compiler_params=pltpu.CompilerParams(has_side_effects=pltpu.SideEffectType.DATAFLOW_SIDE_EFFECTING),
    )(*[pltpu.with_memory_space_constraint(a, pltpu.HBM) for a in list(srcs) + list(lands)], *deps)
    return out[0], out[1], list(out[2:2 + n]), list(out[2 + n:2 + 2 * n]), out[-1]


def _exchange_wait(name, send_sems, recv_sems, srcs, lands, src_by_peer, after):
    n = len(srcs)

    def body(*refs):
        src_refs, land_refs = refs[:n], refs[n:2 * n]
        send_sems, recv_sems = refs[2 * n], refs[2 * n + 1]
        x, y, c = _place()
        peers = _xor_peers(x, y, c)
        for k in range(n):
            for j, peer in enumerate(peers):
                src = src_refs[k].at[_index(*peer)] if src_by_peer else src_refs[k]
                cp = pltpu.make_async_remote_copy(
                    src_ref=src, dst_ref=land_refs[k].at[_index(*peer)], send_sem=send_sems.at[7 * k + j],
                    recv_sem=recv_sems.at[7 * k + j], device_id=peer, device_id_type=MESH)
                cp.wait_send()
                cp.wait_recv()

    thru = [pltpu.HBM(a.shape, a.dtype) for a in list(srcs) + list(lands)]
    out = pl.pallas_call(
        body, name=name, out_shape=tuple(thru),
        in_specs=[HBM_SPEC] * (2 * n) + [SEM_SPEC, SEM_SPEC, ANY_SPEC], out_specs=tuple([HBM_SPEC] * (2 * n)),
        input_output_aliases={k: k for k in range(2 * n)},
        compiler_params=pltpu.CompilerParams(has_side_effects=pltpu.SideEffectType.DATAFLOW_SIDE_EFFECTING),
    )(*srcs, *lands, send_sems, recv_sems, after)
    return list(out[n:])


def _landing(shard, me):
    zone = lax.empty((N_DEV,) + shard.shape, shard.dtype)
    return lax.dynamic_update_slice(zone, shard[None], (me,) + (0,) * shard.ndim)


def _cols_whole(w):
    return jnp.transpose(w, (1, 0, 2)).reshape(w.shape[1], N_DEV * w.shape[2])


def _rows_whole(w):
    return w.reshape(N_DEV * w.shape[1], w.shape[2])


def _cols_parts(dw):
    k, n8 = dw.shape
    return jnp.transpose(dw.reshape(k, N_DEV, n8 // N_DEV), (1, 0, 2))


def _rows_parts(dw):
    r8, c = dw.shape
    return dw.reshape(N_DEV, r8 // N_DEV, c)


def _pack_rows(arrays):
    rows = []
    for a in arrays:
        flat = a.reshape(-1).astype(F32)
        flat = jnp.pad(flat, [(0, (-flat.shape[0]) % PACK_TILE)])
        rows.append(flat.reshape(-1, V7X_LANES))
    return jnp.concatenate(rows, axis=0)


def _unpack_rows(packed, shapes):
    out, row = [], 0
    for s in shapes:
        size = math.prod(s)
        n_rows = -(-size // PACK_TILE) * 8
        out.append(packed[row:row + n_rows].reshape(-1)[:size].reshape(s))
        row += n_rows
    return out


def _merge2d(a):
    return a.reshape(-1, a.shape[-1])


def _ffn_fwd(tag, x, g, w_up, w_down, deps=()):
    f = w_down.shape[0]
    n = _rms_fwd(f"{tag}_norm", x, g, deps)
    gu = _mm(f"{tag}_up", n, w_up)
    act = _swiglu(f"{tag}_swiglu", gu, f)
    out = _mm(f"{tag}_down", act, w_down, res=x, scale=0.5, out_dtype=F32)
    return out, (x, n, gu, act)


def _ffn_bwd(tag, dres, saved, g, w_up, w_down, deps=()):
    x, n, gu, act = saved
    dres32, dres16 = dres
    f = w_down.shape[0]
    dact = _mm(f"{tag}_down_dx", dres16, w_down, tb=True, scale=0.5, deps=deps)
    d_down = _mm(f"{tag}_down_dw", act, dres16, ta=True, scale=0.5)
    dgu = _swiglu_bwd(f"{tag}_swiglu_bwd", dact, gu, f)
    d_up = _mm(f"{tag}_up_dw", n, dgu, ta=True)
    dn = _mm(f"{tag}_up_dx", dgu, w_up, tb=True)
    dx, dg = _rms_bwd(f"{tag}_norm_bwd", x, g, dn, dres32)
    return dx, dg, d_up, d_down


def _conv_mixer_fwd(tag, x, g, w_in, w_conv, w_out, n_seq, seq):
    h = _rms_fwd(f"{tag}_norm", x, g)
    cbv = _mm(f"{tag}_in", h, w_in)
    z = _conv_fwd(f"{tag}_conv", cbv, w_conv, n_seq, seq)
    out = _mm(f"{tag}_out", z, w_out, res=x, out_dtype=F32)
    return out, (x, h, cbv, z)


def _conv_mixer_bwd(tag, dres, saved, g, w_in, w_conv, w_out, n_seq, seq):
    x, h, cbv, z = saved
    dres32, dres16 = dres
    dz = _mm(f"{tag}_out_dx", dres16, w_out, tb=True)
    d_out = _mm(f"{tag}_out_dw", z, dres16, ta=True)
    dc, db, dv, d_conv = _conv_bwd(f"{tag}_conv_bwd", dz, cbv, w_conv, n_seq, seq)
    dcbv = jnp.concatenate([dc, db, dv], axis=1)
    d_in = _mm(f"{tag}_in_dw", h, dcbv, ta=True)
    dh = _mm(f"{tag}_in_dx", dcbv, w_in, tb=True)
    dx, dg = _rms_bwd(f"{tag}_norm_bwd", x, g, dh, dres32)
    return dx, dg, d_in, d_conv, d_out


def _s5_mixer_fwd(tag, x, g, ssm, dskip, w_glu, n_seq, seq):
    a_re, a_im, log_dt, b_re, b_im, c_re, c_im = ssm
    groups, p, hh = b_re.shape
    gb = S5_CHANNELS // hh
    disc, disc_vjp = jax.vjp(_s5_discretize, a_re, a_im, log_dt, b_re, b_im)
    abar_re, abar_im, bbar_re, bbar_im = disc
    mats = (_block_diag_in(bbar_re, gb).astype(BF16), _block_diag_in(bbar_im, gb).astype(BF16),
            _block_diag_out(c_re, gb).astype(BF16), _block_diag_out(c_im, gb).astype(BF16),
            abar_re.reshape(1, groups * p), abar_im.reshape(1, groups * p), dskip)
    d = x.shape[1]
    h = _rms_fwd(f"{tag}_norm", x, g)
    s_re, s_im, ypre, z = _s5_fwd(f"{tag}_scan", h, *mats, n_seq, seq)
    vg = _mm(f"{tag}_glu", z, w_glu)
    out = _glu_res(f"{tag}_glu_act", vg, x, d)
    return out, (x, h, s_re, s_im, ypre, z, vg, mats, disc_vjp, (groups, p, hh, gb))


def _s5_mixer_bwd(tag, dres, saved, g, w_glu, n_seq, seq):
    x, h, s_re, s_im, ypre, z, vg, mats, disc_vjp, (groups, p, hh, gb) = saved
    d = x.shape[1]
    dres32, _ = dres
    dvg = _glu_bwd(f"{tag}_glu_act_bwd", dres32, vg, d)
    d_glu = _mm(f"{tag}_glu_dw", z, dvg, ta=True)
    dz = _mm(f"{tag}_glu_dx", dvg, w_glu, tb=True)
    dh, dbin_re, dbin_im, dcout_re, dcout_im, dabar_re, dabar_im, d_skip = _s5_bwd(
        f"{tag}_scan_bwd", dz, ypre, h, s_re, s_im, *mats, n_seq, seq)
    d_are, d_aim, d_logdt, d_bre, d_bim = disc_vjp((
        dabar_re.reshape(groups, p), dabar_im.reshape(groups, p),
        _block_diag_in_t(dbin_re, gb, p, hh), _block_diag_in_t(dbin_im, gb, p, hh)))
    d_cre = _block_diag_out_t(dcout_re, gb, p, hh)
    d_cim = _block_diag_out_t(dcout_im, gb, p, hh)
    dx, dg = _rms_bwd(f"{tag}_norm_bwd", x, g, dh, dres32)
    return dx, dg, (d_are, d_aim, d_logdt, d_bre, d_bim, d_cre, d_cim), d_skip, d_glu


def _xattn_fwd(tag, x, mem, g_q, g_mem, w_q, w_kv, w_o, n_seq, seq, mlen, heads):
    n = _rms_fwd(f"{tag}_norm", x, g_q)
    q = _mm(f"{tag}_q", n, w_q)
    mem_n = _rms_fwd(f"{tag}_mem_norm", mem, g_mem)
    kv = _mm(f"{tag}_kv", mem_n, w_kv)
    o = _attn_fwd(f"{tag}_attn", q, kv, n_seq, seq, mlen, heads)
    out = _mm(f"{tag}_o", o, w_o, res=x, out_dtype=F32)
    return out, (x, n, q, mem_n, kv, o)


def _xattn_bwd(tag, dres, saved, mem, g_q, g_mem, w_q, w_kv, w_o, n_seq, seq, mlen, heads):
    x, n, q, mem_n, kv, o = saved
    dres32, dres16 = dres
    do = _mm(f"{tag}_o_dx", dres16, w_o, tb=True)
    d_o = _mm(f"{tag}_o_dw", o, dres16, ta=True)
    dq, dk, dv = _attn_bwd(f"{tag}_attn_bwd", q, kv, do, n_seq, seq, mlen, heads)
    dkv = jnp.concatenate([dk, dv], axis=1)
    d_q = _mm(f"{tag}_q_dw", n, dq, ta=True)
    dn = _mm(f"{tag}_q_dx", dq, w_q, tb=True)
    d_kv = _mm(f"{tag}_kv_dw", mem_n, dkv, ta=True)
    dmem_n = _mm(f"{tag}_kv_dx", dkv, w_kv, tb=True)
    _, dg_mem = _rms_bwd(f"{tag}_mem_norm_bwd", mem, g_mem, dmem_n)
    dx, dg_q = _rms_bwd(f"{tag}_norm_bwd", x, g_q, dn, dres32)
    return dx, dg_q, dg_mem, d_q, d_kv, d_o


WEIGHT_NAMES = ("norm_g", "final_g", "ffn1_up", "ffn1_down", "ffn2_up", "ffn2_down", "conv_w_in", "conv_w",
                "conv_w_out", "ssm_a_re", "ssm_a_im", "ssm_log_dt", "ssm_b_re", "ssm_b_im", "ssm_c_re", "ssm_c_im",
                "ssm_d", "ssm_w_glu", "xa_w_q", "xa_w_kv", "xa_w_o")
MATRICES = ("ffn1_up", "ffn1_down", "ffn2_up", "ffn2_down", "conv_w_in", "conv_w_out", "ssm_w_glu", "xa_w_q",
            "xa_w_kv", "xa_w_o")
COLUMN_SHARDED = ("ffn1_up", "ffn2_up", "conv_w_in", "ssm_w_glu", "xa_w_kv")
SMALL_SHARDED = ("norm_g", "conv_w", "ssm_d")
REPLICATED = ("ssm_a_re", "ssm_a_im", "ssm_log_dt", "ssm_b_re", "ssm_b_im", "ssm_c_re", "ssm_c_im", "final_g")


def kernel(x, mem, norm_g, final_g, ffn1_up, ffn1_down, ffn2_up, ffn2_down, conv_w_in, conv_w, conv_w_out, ssm_a_re, ssm_a_im, ssm_log_dt, ssm_b_re, ssm_b_im, ssm_c_re, ssm_c_im, ssm_d, ssm_w_glu, xa_w_q, xa_w_kv, xa_w_o, loss_target, m_norm_g, m_final_g, m_ffn1_up, m_ffn1_down, m_ffn2_up, m_ffn2_down, m_conv_w_in, m_conv_w, m_conv_w_out, m_ssm_a_re, m_ssm_a_im, m_ssm_log_dt, m_ssm_b_re, m_ssm_b_im, m_ssm_c_re, m_ssm_c_im, m_ssm_d, m_ssm_w_glu, m_xa_w_q, m_xa_w_kv, m_xa_w_o, v_norm_g, v_final_g, v_ffn1_up, v_ffn1_down, v_ffn2_up, v_ffn2_down, v_conv_w_in, v_conv_w, v_conv_w_out, v_ssm_a_re, v_ssm_a_im, v_ssm_log_dt, v_ssm_b_re, v_ssm_b_im, v_ssm_c_re, v_ssm_c_im, v_ssm_d, v_ssm_w_glu, v_xa_w_q, v_xa_w_kv, v_xa_w_o):
    w = dict(norm_g=norm_g, final_g=final_g, ffn1_up=ffn1_up, ffn1_down=ffn1_down, ffn2_up=ffn2_up,
             ffn2_down=ffn2_down, conv_w_in=conv_w_in, conv_w=conv_w, conv_w_out=conv_w_out, ssm_a_re=ssm_a_re,
             ssm_a_im=ssm_a_im, ssm_log_dt=ssm_log_dt, ssm_b_re=ssm_b_re, ssm_b_im=ssm_b_im, ssm_c_re=ssm_c_re,
             ssm_c_im=ssm_c_im, ssm_d=ssm_d, ssm_w_glu=ssm_w_glu, xa_w_q=xa_w_q, xa_w_kv=xa_w_kv, xa_w_o=xa_w_o)
    mom = dict(norm_g=m_norm_g, final_g=m_final_g, ffn1_up=m_ffn1_up, ffn1_down=m_ffn1_down, ffn2_up=m_ffn2_up,
               ffn2_down=m_ffn2_down, conv_w_in=m_conv_w_in, conv_w=m_conv_w, conv_w_out=m_conv_w_out,
               ssm_a_re=m_ssm_a_re, ssm_a_im=m_ssm_a_im, ssm_log_dt=m_ssm_log_dt, ssm_b_re=m_ssm_b_re,
               ssm_b_im=m_ssm_b_im, ssm_c_re=m_ssm_c_re, ssm_c_im=m_ssm_c_im, ssm_d=m_ssm_d, ssm_w_glu=m_ssm_w_glu,
               xa_w_q=m_xa_w_q, xa_w_kv=m_xa_w_kv, xa_w_o=m_xa_w_o)
    var = dict(norm_g=v_norm_g, final_g=v_final_g, ffn1_up=v_ffn1_up, ffn1_down=v_ffn1_down, ffn2_up=v_ffn2_up,
               ffn2_down=v_ffn2_down, conv_w_in=v_conv_w_in, conv_w=v_conv_w, conv_w_out=v_conv_w_out,
               ssm_a_re=v_ssm_a_re, ssm_a_im=v_ssm_a_im, ssm_log_dt=v_ssm_log_dt, ssm_b_re=v_ssm_b_re,
               ssm_b_im=v_ssm_b_im, ssm_c_re=v_ssm_c_re, ssm_c_im=v_ssm_c_im, ssm_d=v_ssm_d, ssm_w_glu=v_ssm_w_glu,
               xa_w_q=v_xa_w_q, xa_w_kv=v_xa_w_kv, xa_w_o=v_xa_w_o)

    n_seq, seq, d = x.shape
    mlen = mem.shape[1]
    depth, n_norms = norm_g.shape[0], norm_g.shape[1]
    heads = 4
    tokens = n_seq * seq
    x2 = x.reshape(tokens, d)
    mem2 = mem.reshape(n_seq * mlen, d)
    tgt2 = loss_target.reshape(tokens, d)

    small_shapes = [w[k].shape for k in SMALL_SHARDED]
    small_rows = [_merge2d(w[k]) for k in SMALL_SHARDED]
    small_counts = [s.shape[0] for s in small_rows]
    small = jnp.concatenate(small_rows, axis=0)
    small = jnp.pad(small, [(0, (-small.shape[0]) % 8), (0, 0)])
    me = _index(*_place())

    def layer_weights(i):
        names = [(k, i) for k in ("ffn1_up", "ffn1_down", "ffn2_up", "ffn2_down", "xa_w_q", "xa_w_kv", "xa_w_o")]
        return names + ([("conv_w_in", i // 2), ("conv_w_out", i // 2)] if i % 2 == 0 else [("ssm_w_glu", i // 2)])

    shards = [[w[k][idx].astype(BF16) for k, idx in layer_weights(i)] for i in range(depth)]
    gathered = _all_gather("gather_layer0", shards[0] + [small])
    small_all = gathered[-1]
    blocks = [gathered[:-1]] + [None] * (depth - 1)
    in_flight = [None] * depth
    token = gathered[0]
    for i in range(1, depth):
        zones = [_landing(s, me) for s in shards[i]]
        *in_flight[i], token = _exchange_start(f"gather_start_l{i}", shards[i], zones, False, deps=[token])

    def small_whole(idx):
        start = sum(small_counts[:idx])
        part = small_all[:, start:start + small_counts[idx]]
        lead = small_shapes[idx][:-1]
        part = part.reshape((N_DEV,) + lead + (part.shape[-1],))
        part = jnp.moveaxis(part, 0, -2)
        return part.reshape(lead + (N_DEV * part.shape[-1],))

    norm_all = small_whole(0)
    conv_all = small_whole(1)
    dskip_all = small_whole(2)

    def whole(i):
        return {k: _cols_whole(blk) if k in COLUMN_SHARDED else _rows_whole(blk)
                for (k, _), blk in zip(layer_weights(i), blocks[i])}

    saved = []
    cur = x2
    for i in range(depth):
        g = [norm_all[i, k].reshape(1, d) for k in range(n_norms)]
        j = i // 2
        if i > 0:
            blocks[i] = _exchange_wait(f"gather_wait_l{i}", *in_flight[i], False, cur)
        lw = whole(i)
        cur, s_ffn1 = _ffn_fwd(f"l{i}_ffn1", cur, g[0], lw["ffn1_up"], lw["ffn1_down"], [token] if i == 0 else ())
        if i % 2 == 0:
            lw["conv_w"] = conv_all[j]
            cur, s_mix = _conv_mixer_fwd(f"l{i}_conv", cur, g[1], lw["conv_w_in"], lw["conv_w"], lw["conv_w_out"],
                                         n_seq, seq)
        else:
            ssm = tuple(w[k][j] for k in ("ssm_a_re", "ssm_a_im", "ssm_log_dt", "ssm_b_re", "ssm_b_im",
                                          "ssm_c_re", "ssm_c_im"))
            cur, s_mix = _s5_mixer_fwd(f"l{i}_s5", cur, g[1], ssm, dskip_all[j].reshape(1, d), lw["ssm_w_glu"],
                                       n_seq, seq)
        cur, s_xa = _xattn_fwd(f"l{i}_xa", cur, mem2, g[2], g[3], lw["xa_w_q"], lw["xa_w_kv"], lw["xa_w_o"],
                               n_seq, seq, mlen, heads)
        cur, s_ffn2 = _ffn_fwd(f"l{i}_ffn2", cur, g[4], lw["ffn2_up"], lw["ffn2_down"])
        saved.append((g, lw, s_ffn1, s_mix, s_xa, s_ffn2))

    dres, err2, d_final = _final_loss("loss_head", cur, final_g.reshape(1, d), tgt2)
    loss = lax.psum(0.5 * jnp.sum(err2) / d, ("x", "y", "c"))

    d_norm = [[None] * n_norms for _ in range(depth)]
    d_conv = [None] * conv_w.shape[0]
    d_skip = [None] * ssm_d.shape[0]
    d_ssm = [None] * ssm_a_re.shape[0]
    leaving = [None] * depth
    deps = ()
    for i in reversed(range(depth)):
        g, lw, s_ffn1, s_mix, s_xa, s_ffn2 = saved[i]
        j = i // 2
        gm = {}
        dres, d_norm[i][4], gm["ffn2_up"], gm["ffn2_down"] = _ffn_bwd(
            f"l{i}_ffn2", dres, s_ffn2, g[4], lw["ffn2_up"], lw["ffn2_down"], deps)
        dres, d_norm[i][2], d_norm[i][3], gm["xa_w_q"], gm["xa_w_kv"], gm["xa_w_o"] = _xattn_bwd(
            f"l{i}_xa", dres, s_xa, mem2, g[2], g[3], lw["xa_w_q"], lw["xa_w_kv"], lw["xa_w_o"], n_seq, seq, mlen,
            heads)
        if i % 2 == 0:
            dres, d_norm[i][1], gm["conv_w_in"], d_conv[j], gm["conv_w_out"] = _conv_mixer_bwd(
                f"l{i}_conv", dres, s_mix, g[1], lw["conv_w_in"], lw["conv_w"], lw["conv_w_out"], n_seq, seq)
        else:
            dres, d_norm[i][1], d_ssm[j], d_skip[j], gm["ssm_w_glu"] = _s5_mixer_bwd(
                f"l{i}_s5", dres, s_mix, g[1], lw["ssm_w_glu"], n_seq, seq)
        dres, d_norm[i][0], gm["ffn1_up"], gm["ffn1_down"] = _ffn_bwd(
            f"l{i}_ffn1", dres, s_ffn1, g[0], lw["ffn1_up"], lw["ffn1_down"])
        parts = [(_cols_parts if k in COLUMN_SHARDED else _rows_parts)(gm[k]) for k, _ in layer_weights(i)]
        zones = [_landing(lax.dynamic_index_in_dim(p, me, 0, keepdims=False), me) for p in parts]
        *leaving[i], token = _exchange_start(f"grads_start_l{i}", parts, zones, True)
        deps = [token]
    grad_x = dres[0].reshape(n_seq, seq, d)
    received = {k: [None] * w[k].shape[0] for k in MATRICES}
    for i in range(depth):
        arrived = _exchange_wait(f"grads_wait_l{i}", *leaving[i], True, dres[0])
        for (k, idx), blk in zip(layer_weights(i), arrived):
            received[k][idx] = blk

    d_norm_all = jnp.stack([jnp.concatenate(row, axis=0) for row in d_norm])
    d_conv_all = jnp.stack(d_conv)
    d_skip_all = jnp.concatenate(d_skip, axis=0)

    def small_parts(full):
        lead = full.shape[:-1]
        t = full.reshape(lead + (N_DEV, full.shape[-1] // N_DEV))
        t = jnp.moveaxis(t, -2, 0)
        return t.reshape(N_DEV, -1, t.shape[-1])

    small_g = jnp.concatenate([small_parts(a) for a in (d_norm_all, d_conv_all, d_skip_all)], axis=1)
    small_g = jnp.pad(small_g, [(0, 0), (0, (-small_g.shape[1]) % 8), (0, 0)])
    small_received = _all_to_all("exchange_small_grads", [small_g])[0]

    rep_grads = [jnp.stack([d_ssm[j][k] for j in range(len(d_ssm))]) for k in range(7)] + [d_final.reshape(-1)]
    rep_shapes = [w[k].shape for k in REPLICATED]
    rep_all = _all_gather("gather_replicated_grads", [_pack_rows(rep_grads)])[0]

    grads, deltas, new_m, new_v = {}, {}, {}, {}
    for k in MATRICES:
        shape = w[k].shape
        out = _adamw(f"adamw_{k}", received[k], _merge2d(w[k]), _merge2d(mom[k]), _merge2d(var[k]))
        grads[k], deltas[k], new_m[k], new_v[k] = [o.reshape(shape) for o in out]

    def small_local(src):
        rows = jnp.concatenate([_merge2d(src[k]) for k in SMALL_SHARDED], axis=0)
        return jnp.pad(rows, [(0, (-rows.shape[0]) % 8), (0, 0)])

    out = _adamw("adamw_small", [small_received], small, small_local(mom), small_local(var))
    for res, o in zip((grads, deltas, new_m, new_v), out):
        start = 0
        for k, cnt, shape in zip(SMALL_SHARDED, small_counts, small_shapes):
            res[k] = o[start:start + cnt].reshape(shape)
            start += cnt

    out = _adamw("adamw_replicated", [rep_all], _pack_rows([w[k] for k in REPLICATED]),
                 _pack_rows([mom[k] for k in REPLICATED]), _pack_rows([var[k] for k in REPLICATED]))
    for res, o in zip((grads, deltas, new_m, new_v), out):
        for k, a in zip(REPLICATED, _unpack_rows(o, rep_shapes)):
            res[k] = a

    return (loss, grad_x, *[grads[k] for k in WEIGHT_NAMES], *[deltas[k] for k in WEIGHT_NAMES],
            *[new_m[k] for k in WEIGHT_NAMES], *[new_v[k] for k in WEIGHT_NAMES])
```

```python
import math

import jax
import jax.numpy as jnp
from jax import lax
from jax.experimental import pallas as pl
from jax.experimental.pallas import tpu as pltpu

F32 = jnp.float32
BF16 = jnp.bfloat16
MESH = pl.DeviceIdType.MESH
N_DEV = 8

NORM_EPS = 1e-6
EIG_CLIP = -1e-4
CONV_WIDTH = 3
ADAM_LR = 0.001
ADAM_B1 = 0.9
ADAM_B2 = 0.999
ADAM_EPS = 1e-08
ADAM_WD = 0.01
ADAM_STEP = 10
GELU_C = math.sqrt(2.0 / math.pi)
GELU_A = 0.044715

V7X_LANES = 128
V7X_VMEM_LIMIT = 56 * 1024 * 1024
S5_CHANNELS = 128
PACK_TILE = 8 * V7X_LANES

HBM_SPEC = pl.BlockSpec(memory_space=pltpu.HBM)
ANY_SPEC = pl.BlockSpec(memory_space=pl.ANY)
SEM_SPEC = pl.BlockSpec(memory_space=pltpu.SEMAPHORE)


def _params(n_grid):
    return pltpu.CompilerParams(dimension_semantics=("arbitrary",) * n_grid, vmem_limit_bytes=V7X_VMEM_LIMIT)


def _pick(n, pref, align):
    if n <= pref:
        return n
    t = (pref // align) * align
    while t >= align:
        if n % t == 0:
            return t
        t -= align
    raise ValueError(f"no tile for {n} (pref {pref}, align {align})")


MM_RHS_BLOCK_BYTES = 12 * 1024 * 1024
MM_LHS_BLOCK_BYTES = 6 * 1024 * 1024
MM_ACC_BYTES = 6 * 1024 * 1024
MM_ROWS = 512


def _mm_tiles(m, k, n, a_item, b_item, ta):
    tn = _pick(n, max(V7X_LANES, MM_RHS_BLOCK_BYTES // (k * b_item)), V7X_LANES)
    rows = min(MM_ROWS, MM_ACC_BYTES // (4 * tn), MM_LHS_BLOCK_BYTES // (k * a_item))
    align = V7X_LANES if ta else 16
    tm = _pick(m, max(align, rows), align)
    return tm, tn


def _mm(name, a, b, *, ta=False, tb=False, out_dtype=BF16, res=None, scale=None, deps=()):
    if ta:
        k, m = a.shape
    else:
        m, k = a.shape
    if tb:
        n, k2 = b.shape
    else:
        k2, n = b.shape
    assert k == k2, (name, a.shape, b.shape)
    tm, tn = _mm_tiles(m, k, n, a.dtype.itemsize, b.dtype.itemsize, ta)
    a_spec = pl.BlockSpec((k, tm), lambda j, i: (0, i)) if ta else pl.BlockSpec((tm, k), lambda j, i: (i, 0))
    b_spec = pl.BlockSpec((tn, k), lambda j, i: (j, 0)) if tb else pl.BlockSpec((k, tn), lambda j, i: (0, j))
    o_spec = pl.BlockSpec((tm, tn), lambda j, i: (i, j))
    dims = (((0 if ta else 1,), (1 if tb else 0,)), ((), ()))
    has_res = res is not None

    def body(*refs):
        a_ref, b_ref = refs[0], refs[1]
        o_ref = refs[-1]
        acc = lax.dot_general(a_ref[...].astype(BF16), b_ref[...].astype(BF16), dims, preferred_element_type=F32)
        if scale is not None:
            acc = acc * scale
        if has_res:
            acc = acc + refs[2][...].astype(F32)
        o_ref[...] = acc.astype(o_ref.dtype)

    ins = [a, b] + ([res] if has_res else []) + list(deps)
    specs = [a_spec, b_spec] + ([o_spec] if has_res else []) + [ANY_SPEC] * len(deps)
    return pl.pallas_call(
        body, name=name, grid=(n // tn, m // tm), in_specs=specs, out_specs=o_spec,
        out_shape=jax.ShapeDtypeStruct((m, n), out_dtype), compiler_params=_params(2),
    )(*ins)


def _rowwise(name, fn, rows, row_ins, par_ins, row_outs, acc_outs=(), tm_pref=256, deps=()):
    tm = _pick(rows, tm_pref, 16)
    in_specs, ins = [], []
    for r in row_ins:
        arr, cb, cw = r if isinstance(r, tuple) else (r, 0, r.shape[1])
        assert arr.shape[0] == rows, (name, arr.shape, rows)
        ins.append(arr)
        in_specs.append(pl.BlockSpec((tm, cw), lambda i, cb=cb: (i, cb)))
    for p in par_ins:
        ins.append(p)
        in_specs.append(pl.BlockSpec(p.shape, lambda i: (0, 0)))
    out_specs = [pl.BlockSpec((tm, c), lambda i: (i, 0)) for c, _ in row_outs]
    out_specs += [pl.BlockSpec((r, c), lambda i: (0, 0)) for r, c in acc_outs]
    out_shape = [jax.ShapeDtypeStruct((rows, c), dt) for c, dt in row_outs]
    out_shape += [jax.ShapeDtypeStruct((r, c), F32) for r, c in acc_outs]
    n_in, n_row = len(ins), len(row_outs)
    ins += list(deps)
    in_specs += [ANY_SPEC] * len(deps)

    def body(*refs):
        vals = [r[...] for r in refs[:n_in]]
        outs = refs[n_in + len(deps):]
        res = fn(*vals)
        if not isinstance(res, (tuple, list)):
            res = (res,)
        for o, v in zip(outs[:n_row], res[:n_row]):
            if isinstance(v, (tuple, list)):
                off = 0
                for piece in v:
                    w = piece.shape[1]
                    o[:, off:off + w] = piece.astype(o.dtype)
                    off += w
            else:
                o[...] = v.astype(o.dtype)
        if len(outs) > n_row:
            @pl.when(pl.program_id(0) == 0)
            def _():
                for o in outs[n_row:]:
                    o[...] = jnp.zeros_like(o)

            for o, v in zip(outs[n_row:], res[n_row:]):
                o[...] += v

    out = pl.pallas_call(
        body, name=name, grid=(rows // tm,), in_specs=in_specs, out_specs=out_specs, out_shape=out_shape,
        compiler_params=_params(1),
    )(*ins)
    return out


def _inv_rms(x):
    return lax.rsqrt(jnp.mean(x * x, axis=-1, keepdims=True) + NORM_EPS)


def _rms_fwd(name, x, g, deps=()):
    def fn(x, g):
        return x * _inv_rms(x) * g

    return _rowwise(name, fn, x.shape[0], [x], [g], [(x.shape[1], BF16)], tm_pref=512, deps=deps)[0]


def _rms_bwd(name, x, g, dn, dres=None):
    d = x.shape[1]

    def fn(x, dn, *rest):
        g = rest[-1]
        dn = dn.astype(F32)
        r = _inv_rms(x)
        xh = x * r
        dg = jnp.sum(dn * xh, axis=0, keepdims=True)
        if dres is None:
            return (dg,)
        dxh = dn * g
        dx = r * (dxh - xh * jnp.mean(dxh * xh, axis=-1, keepdims=True)) + rest[0]
        return dx, dx, dg

    row_ins = [x, dn] + ([] if dres is None else [dres])
    row_outs = [] if dres is None else [(d, F32), (d, BF16)]
    out = _rowwise(name, fn, x.shape[0], row_ins, [g], row_outs, [(1, d)], tm_pref=256)
    return (None, out[0]) if dres is None else ((out[0], out[1]), out[2])


def _sigmoid(x):
    return 1.0 / (1.0 + jnp.exp(-x))


def _swiglu(name, gu, f):
    def fn(gt, up):
        gt = gt.astype(F32)
        return gt * _sigmoid(gt) * up.astype(F32)

    return _rowwise(name, fn, gu.shape[0], [(gu, 0, f), (gu, 1, f)], [], [(f, BF16)])[0]


def _swiglu_bwd(name, dact, gu, f):
    def fn(dact, gt, up):
        dact, gt, up = dact.astype(F32), gt.astype(F32), up.astype(F32)
        sg = _sigmoid(gt)
        dgt = dact * up * (sg * (1.0 + gt * (1.0 - sg)))
        dup = dact * (gt * sg)
        return ((dgt, dup),)

    return _rowwise(name, fn, gu.shape[0], [dact, (gu, 0, f), (gu, 1, f)], [], [(2 * f, BF16)])[0]


def _glu_res(name, vg, x, d):
    def fn(val, gate, x):
        return x + val.astype(F32) * _sigmoid(gate.astype(F32))

    return _rowwise(name, fn, x.shape[0], [(vg, 0, d), (vg, 1, d), x], [], [(d, F32)])[0]


def _glu_bwd(name, dres, vg, d):
    def fn(dres, val, gate):
        val, gate = val.astype(F32), gate.astype(F32)
        sg = _sigmoid(gate)
        return ((dres * sg, dres * val * sg * (1.0 - sg)),)

    return _rowwise(name, fn, dres.shape[0], [dres, (vg, 0, d), (vg, 1, d)], [], [(2 * d, BF16)])[0]


def _final_loss(name, x, g, tgt):
    d = x.shape[1]

    def fn(x, tgt, g):
        r = _inv_rms(x)
        xh = x * r
        err = xh * g - tgt
        dy = err * (1.0 / d)
        dxh = dy * g
        dx = r * (dxh - xh * jnp.mean(dxh * xh, axis=-1, keepdims=True))
        return dx, dx, jnp.sum(err * err, axis=0, keepdims=True), jnp.sum(dy * xh, axis=0, keepdims=True)

    dx, dx16, err2, dg = _rowwise(name, fn, x.shape[0], [x, tgt], [g], [(d, F32), (d, BF16)], [(1, d), (1, d)])
    return (dx, dx16), err2, dg


def _shift_down(u, k):
    rows = lax.broadcasted_iota(jnp.int32, u.shape, 0)
    return jnp.where(rows >= k, pltpu.roll(u, k, 0), 0.0)


def _shift_up(u, k):
    n = u.shape[0]
    rows = lax.broadcasted_iota(jnp.int32, u.shape, 0)
    return jnp.where(rows < n - k, pltpu.roll(u, n - k, 0), 0.0)


def _shift_rows3(u, k, up):
    n = u.shape[1]
    rows = lax.broadcasted_iota(jnp.int32, u.shape, 1)
    if up:
        return jnp.where(rows < n - k, pltpu.roll(u, n - k, 1), 0.0)
    return jnp.where(rows >= k, pltpu.roll(u, k, 1), 0.0)


def _conv_specs(seq, cw, n_cb, swap):
    def at(off):
        if swap:
            return pl.BlockSpec((seq, cw), lambda j, b: (b, off * n_cb + j))
        return pl.BlockSpec((seq, cw), lambda b, j: (b, off * n_cb + j))

    return at


def _conv_fwd(name, cbv, w, n_seq, seq):
    d = w.shape[1]
    cw = _pick(d, 256, V7X_LANES)
    n_cb = d // cw
    at = _conv_specs(seq, cw, n_cb, swap=False)

    def body(c_ref, b_ref, v_ref, w_ref, z_ref):
        u = c_ref[...].astype(F32) * v_ref[...].astype(F32)
        cv = w_ref[0:1, :] * _shift_down(u, 2) + w_ref[1:2, :] * _shift_down(u, 1) + w_ref[2:3, :] * u
        z_ref[...] = (b_ref[...].astype(F32) * cv).astype(z_ref.dtype)

    return pl.pallas_call(
        body, name=name, grid=(n_seq, n_cb),
        in_specs=[at(0), at(1), at(2), pl.BlockSpec((CONV_WIDTH, cw), lambda b, j: (0, j))],
        out_specs=at(0), out_shape=jax.ShapeDtypeStruct((n_seq * seq, d), BF16), compiler_params=_params(2),
    )(cbv, cbv, cbv, w)


def _conv_bwd(name, dz, cbv, w, n_seq, seq):
    d = w.shape[1]
    cw = _pick(d, 256, V7X_LANES)
    n_cb = d // cw
    at = _conv_specs(seq, cw, n_cb, swap=True)

    def body(dz_ref, c_ref, b_ref, v_ref, w_ref, dc_ref, db_ref, dv_ref, dw_ref):
        c, b, v = c_ref[...].astype(F32), b_ref[...].astype(F32), v_ref[...].astype(F32)
        dz = dz_ref[...].astype(F32)
        w0, w1, w2 = w_ref[0:1, :], w_ref[1:2, :], w_ref[2:3, :]
        u = c * v
        u1, u2 = _shift_down(u, 1), _shift_down(u, 2)
        cv = w0 * u2 + w1 * u1 + w2 * u
        db_ref[...] = (dz * cv).astype(db_ref.dtype)
        dcv = dz * b
        du = w2 * dcv + w1 * _shift_up(dcv, 1) + w0 * _shift_up(dcv, 2)
        dc_ref[...] = (du * v).astype(dc_ref.dtype)
        dv_ref[...] = (du * c).astype(dv_ref.dtype)

        @pl.when(pl.program_id(1) == 0)
        def _():
            dw_ref[...] = jnp.zeros_like(dw_ref)

        dw_ref[0:1, :] += jnp.sum(dcv * u2, axis=0, keepdims=True)
        dw_ref[1:2, :] += jnp.sum(dcv * u1, axis=0, keepdims=True)
        dw_ref[2:3, :] += jnp.sum(dcv * u, axis=0, keepdims=True)

    act = jax.ShapeDtypeStruct((n_seq * seq, d), BF16)
    return pl.pallas_call(
        body, name=name, grid=(n_cb, n_seq),
        in_specs=[at(0), at(0), at(1), at(2), pl.BlockSpec((CONV_WIDTH, cw), lambda j, b: (0, j))],
        out_specs=[at(0), at(0), at(0), pl.BlockSpec((CONV_WIDTH, cw), lambda j, b: (0, j))],
        out_shape=[act, act, act, jax.ShapeDtypeStruct((CONV_WIDTH, d), F32)], compiler_params=_params(2),
    )(dz, cbv, cbv, cbv, w)


def _s5_discretize(a_re, a_im, log_dt, b_re, b_im):
    lam_re = jnp.minimum(a_re, EIG_CLIP)
    lam_im = a_im
    dt = jnp.exp(log_dt)[:, None]
    mag = jnp.exp(lam_re * dt)
    abar_re = mag * jnp.cos(lam_im * dt)
    abar_im = mag * jnp.sin(lam_im * dt)
    den = lam_re * lam_re + lam_im * lam_im
    num_re = abar_re - 1.0
    num_im = abar_im
    coef_re = (num_re * lam_re + num_im * lam_im) / den
    coef_im = (num_im * lam_re - num_re * lam_im) / den
    bbar_re = coef_re[..., None] * b_re - coef_im[..., None] * b_im
    bbar_im = coef_re[..., None] * b_im + coef_im[..., None] * b_re
    return abar_re, abar_im, bbar_re, bbar_im


def _block_diag_in(bbar, gb):
    g, p, h = bbar.shape
    t = jnp.transpose(bbar.reshape(g // gb, gb, p, h), (0, 1, 3, 2))
    return jnp.einsum("cghp,gk->cghkp", t, jnp.eye(gb, dtype=bbar.dtype)).reshape(g // gb, gb * h, gb * p)


def _block_diag_in_t(blk, gb, p, h):
    nb = blk.shape[0]
    t = jnp.einsum("cghkp,gk->cghp", blk.reshape(nb, gb, h, gb, p), jnp.eye(gb, dtype=blk.dtype))
    return jnp.transpose(t, (0, 1, 3, 2)).reshape(nb * gb, p, h)


def _block_diag_out(c, gb):
    g, h, p = c.shape
    t = jnp.transpose(c.reshape(g // gb, gb, h, p), (0, 1, 3, 2))
    return jnp.einsum("cgph,gk->cgpkh", t, jnp.eye(gb, dtype=c.dtype)).reshape(g // gb, gb * p, gb * h)


def _block_diag_out_t(blk, gb, p, h):
    nb = blk.shape[0]
    t = jnp.einsum("cgpkh,gk->cgph", blk.reshape(nb, gb, p, gb, h), jnp.eye(gb, dtype=blk.dtype))
    return jnp.transpose(t, (0, 1, 3, 2)).reshape(nb * gb, h, p)


def _gelu(y):
    return 0.5 * y * (1.0 + jnp.tanh(GELU_C * (y + GELU_A * y * y * y)))


def _gelu_grad(y):
    th = jnp.tanh(GELU_C * (y + GELU_A * y * y * y))
    return 0.5 * (1.0 + th) + 0.5 * y * (1.0 - th * th) * GELU_C * (1.0 + 3.0 * GELU_A * y * y)


def _dot(a, b, ca, cb):
    return lax.dot_general(a.astype(BF16), b.astype(BF16), (((ca,), (cb,)), ((), ())), preferred_element_type=F32)


SCAN_SEGMENTS = 8


def _complex_mul(ar, ai, br, bi):
    return ar * br - ai * bi, ar * bi + ai * br


def _segmented_scan(re_ref, im_ref, ar, ai, seq, reverse):
    chunks, _, lanes = re_ref.shape
    seg = seq // SCAN_SEGMENTS
    assert seg * SCAN_SEGMENTS == seq and seg & (seg - 1) == 0, seq
    shape = (chunks, SCAN_SEGMENTS, lanes)
    arb, aib = jnp.broadcast_to(ar, shape), jnp.broadcast_to(ai, shape)

    def rows_at(i):
        return pl.ds(seg - 1 - i if reverse else i, SCAN_SEGMENTS, stride=seg)

    def local(i, carry):
        rows = rows_at(i)
        mr, mi = _complex_mul(arb, aib, *carry)
        nr, ni = mr + re_ref[:, rows, :], mi + im_ref[:, rows, :]
        re_ref[:, rows, :] = nr
        im_ref[:, rows, :] = ni
        return nr, ni

    zero = jnp.zeros(shape, F32)
    end_r, end_i = lax.fori_loop(0, seg, local, (zero, zero), unroll=4)

    pr, pi = arb, aib
    for _ in range(seg.bit_length() - 1):
        pr, pi = _complex_mul(pr, pi, pr, pi)
    step = 1
    while step < SCAN_SEGMENTS:
        mr, mi = _complex_mul(pr, pi, _shift_rows3(end_r, step, reverse), _shift_rows3(end_i, step, reverse))
        end_r, end_i = end_r + mr, end_i + mi
        pr, pi = _complex_mul(pr, pi, pr, pi)
        step *= 2
    in_r, in_i = _shift_rows3(end_r, 1, reverse), _shift_rows3(end_i, 1, reverse)

    def carry_in(i, power):
        rows = rows_at(i)
        mr, mi = _complex_mul(*power, in_r, in_i)
        re_ref[:, rows, :] += mr
        im_ref[:, rows, :] += mi
        return _complex_mul(*power, arb, aib)

    lax.fori_loop(0, seg, carry_in, (arb, aib), unroll=4)


def _to_chunks(ref, val):
    for k in range(ref.shape[0]):
        ref[k] = val[:, k * V7X_LANES:(k + 1) * V7X_LANES]


def _from_chunks(ref):
    return jnp.concatenate([ref[k] for k in range(ref.shape[0])], axis=1)


def _s5_specs(seq, ch, sb):
    chunks = sb // V7X_LANES
    act = pl.BlockSpec((seq, ch), lambda j, b: (b, j))
    state = pl.BlockSpec((chunks, seq, V7X_LANES), lambda j, b: (j, b, 0))
    w_in = pl.BlockSpec((None, ch, sb), lambda j, b: (j, 0, 0))
    w_out = pl.BlockSpec((None, sb, ch), lambda j, b: (j, 0, 0))
    lane_s = pl.BlockSpec((chunks, 1, V7X_LANES), lambda j, b: (j, 0, 0))
    lane_c = pl.BlockSpec((1, ch), lambda j, b: (0, j))
    return act, state, w_in, w_out, lane_s, lane_c


def _s5_fwd(name, h, bin_re, bin_im, cout_re, cout_im, abar_re, abar_im, dskip, n_seq, seq):
    t, d = h.shape
    nb, ch, sb = bin_re.shape
    chunks = sb // V7X_LANES
    act, state, w_in, w_out, lane_s, lane_c = _s5_specs(seq, ch, sb)

    def body(h_ref, bre_ref, bim_ref, cre_ref, cim_ref, ar_ref, ai_ref, d_ref, sre_ref, sim_ref, y_ref, z_ref):
        u = h_ref[...]
        _to_chunks(sre_ref, _dot(u, bre_ref[...], 1, 0))
        _to_chunks(sim_ref, _dot(u, bim_ref[...], 1, 0))
        _segmented_scan(sre_ref, sim_ref, ar_ref[...], ai_ref[...], seq, reverse=False)
        y = _dot(_from_chunks(sre_ref), cre_ref[...], 1, 0) - _dot(_from_chunks(sim_ref), cim_ref[...], 1, 0)
        y = y + d_ref[...] * u.astype(F32)
        y_ref[...] = y
        z_ref[...] = _gelu(y).astype(z_ref.dtype)

    s_shape = jax.ShapeDtypeStruct((nb * chunks, t, V7X_LANES), F32)
    return pl.pallas_call(
        body, name=name, grid=(nb, n_seq),
        in_specs=[act, w_in, w_in, w_out, w_out, lane_s, lane_s, lane_c],
        out_specs=[state, state, act, act],
        out_shape=[s_shape, s_shape, jax.ShapeDtypeStruct((t, d), F32), jax.ShapeDtypeStruct((t, d), BF16)],
        compiler_params=_params(2),
    )(h, bin_re, bin_im, cout_re, cout_im, abar_re, abar_im, dskip)


def _s5_bwd(name, dz, ypre, h, s_re, s_im, bin_re, bin_im, cout_re, cout_im, abar_re, abar_im, dskip, n_seq, seq):
    t, d = h.shape
    nb, ch, sb = bin_re.shape
    chunks = sb // V7X_LANES
    act, state, w_in, w_out, lane_s, lane_c = _s5_specs(seq, ch, sb)

    def body(dz_ref, y_ref, h_ref, sre_ref, sim_ref, bre_ref, bim_ref, cre_ref, cim_ref, ar_ref, ai_ref, d_ref,
             dh_ref, dbre_ref, dbim_ref, dcre_ref, dcim_ref, dar_ref, dai_ref, dd_ref, gre, gim):
        first = pl.program_id(1) == 0
        u = h_ref[...].astype(F32)
        dy = dz_ref[...].astype(F32) * _gelu_grad(y_ref[...])
        _to_chunks(gre, _dot(dy, cre_ref[...], 1, 1))
        _to_chunks(gim, -_dot(dy, cim_ref[...], 1, 1))
        _segmented_scan(gre, gim, ar_ref[...], -ai_ref[...], seq, reverse=True)

        g_re, g_im = gre[...], gim[...]
        p_re, p_im = _shift_rows3(sre_ref[...], 1, False), _shift_rows3(sim_ref[...], 1, False)
        dar = jnp.sum(g_re * p_re + g_im * p_im, axis=1, keepdims=True)
        dai = jnp.sum(g_im * p_re - g_re * p_im, axis=1, keepdims=True)
        g_re, g_im = _from_chunks(gre), _from_chunks(gim)
        dbre = _dot(u, g_re, 0, 0)
        dbim = _dot(u, g_im, 0, 0)
        dcre = _dot(_from_chunks(sre_ref), dy, 0, 0)
        dcim = -_dot(_from_chunks(sim_ref), dy, 0, 0)
        ddd = jnp.sum(dy * u, axis=0, keepdims=True)
        dh_ref[...] = _dot(g_re, bre_ref[...], 1, 1) + _dot(g_im, bim_ref[...], 1, 1) + d_ref[...] * dy

        @pl.when(first)
        def _():
            dar_ref[...] = dar
            dai_ref[...] = dai
            dbre_ref[...] = dbre
            dbim_ref[...] = dbim
            dcre_ref[...] = dcre
            dcim_ref[...] = dcim
            dd_ref[...] = ddd

        @pl.when(jnp.logical_not(first))
        def _():
            dar_ref[...] += dar
            dai_ref[...] += dai
            dbre_ref[...] += dbre
            dbim_ref[...] += dbim
            dcre_ref[...] += dcre
            dcim_ref[...] += dcim
            dd_ref[...] += ddd

    return pl.pallas_call(
        body, name=name, grid=(nb, n_seq),
        in_specs=[act, act, act, state, state, w_in, w_in, w_out, w_out, lane_s, lane_s, lane_c],
        out_specs=[act, w_in, w_in, w_out, w_out, lane_s, lane_s, lane_c],
        out_shape=[jax.ShapeDtypeStruct((t, d), F32),
                   jax.ShapeDtypeStruct((nb, ch, sb), F32), jax.ShapeDtypeStruct((nb, ch, sb), F32),
                   jax.ShapeDtypeStruct((nb, sb, ch), F32), jax.ShapeDtypeStruct((nb, sb, ch), F32),
                   jax.ShapeDtypeStruct((nb * chunks, 1, V7X_LANES), F32),
                   jax.ShapeDtypeStruct((nb * chunks, 1, V7X_LANES), F32), jax.ShapeDtypeStruct((1, d), F32)],
        scratch_shapes=[pltpu.VMEM((chunks, seq, V7X_LANES), F32), pltpu.VMEM((chunks, seq, V7X_LANES), F32)],
        compiler_params=_params(2),
    )(dz, ypre, h, s_re, s_im, bin_re, bin_im, cout_re, cout_im, abar_re, abar_im, dskip)


def _softmax_rows(q, k, scale):
    s = _dot(q, k, 1, 1) * scale
    e = jnp.exp(s - jnp.max(s, axis=-1, keepdims=True))
    return e / jnp.sum(e, axis=-1, keepdims=True)


def _attn_fwd(name, q, kv, n_seq, seq, mlen, heads):
    t, d = q.shape
    hd = d // heads
    tq = _pick(seq, 512, 16)
    nq = seq // tq
    scale = hd ** -0.5
    q_spec = pl.BlockSpec((tq, hd), lambda b, h, i: (b * nq + i, h))

    def body(q_ref, k_ref, v_ref, o_ref):
        p = _softmax_rows(q_ref[...], k_ref[...], scale)
        o_ref[...] = _dot(p, v_ref[...], 1, 0).astype(o_ref.dtype)

    return pl.pallas_call(
        body, name=name, grid=(n_seq, heads, nq),
        in_specs=[q_spec, pl.BlockSpec((mlen, hd), lambda b, h, i: (b, h)),
                  pl.BlockSpec((mlen, hd), lambda b, h, i: (b, heads + h))],
        out_specs=q_spec, out_shape=jax.ShapeDtypeStruct((t, d), BF16), compiler_params=_params(3),
    )(q, kv, kv)


def _attn_bwd(name, q, kv, do, n_seq, seq, mlen, heads):
    t, d = q.shape
    hd = d // heads
    tq = _pick(seq, 512, 16)
    nq = seq // tq
    scale = hd ** -0.5
    q_spec = pl.BlockSpec((tq, hd), lambda b, h, i: (b * nq + i, h))
    k_spec = pl.BlockSpec((mlen, hd), lambda b, h, i: (b, h))

    def body(q_ref, k_ref, v_ref, do_ref, dq_ref, dk_ref, dv_ref):
        q, k, v, do = q_ref[...], k_ref[...], v_ref[...], do_ref[...]
        p = _softmax_rows(q, k, scale)
        dp = _dot(do, v, 1, 1)
        ds = p * (dp - jnp.sum(dp * p, axis=-1, keepdims=True)) * scale
        dq_ref[...] = _dot(ds, k, 1, 0).astype(dq_ref.dtype)

        @pl.when(pl.program_id(2) == 0)
        def _():
            dk_ref[...] = jnp.zeros_like(dk_ref)
            dv_ref[...] = jnp.zeros_like(dv_ref)

        dk_ref[...] += _dot(ds, q, 0, 0)
        dv_ref[...] += _dot(p, do, 0, 0)

    return pl.pallas_call(
        body, name=name, grid=(n_seq, heads, nq),
        in_specs=[q_spec, k_spec, pl.BlockSpec((mlen, hd), lambda b, h, i: (b, heads + h)), q_spec],
        out_specs=[q_spec, k_spec, k_spec],
        out_shape=[jax.ShapeDtypeStruct((t, d), BF16), jax.ShapeDtypeStruct((n_seq * mlen, d), F32),
                   jax.ShapeDtypeStruct((n_seq * mlen, d), F32)],
        compiler_params=_params(3),
    )(q, kv, kv, do)


ADAMW_BLOCK_ELEMS = 128 * 1024


def _adamw(name, parts, w, m, v):
    n_layers = len(parts)
    _, r, c = parts[0].shape
    assert w.shape == (n_layers * r, c), (name, w.shape, parts[0].shape)
    tr = _pick(r, max(16, ADAMW_BLOCK_ELEMS // c // 16 * 16), 8)
    nt = r // tr
    spec = pl.BlockSpec((tr, c), lambda l, i: (l * nt + i, 0))
    c1 = 1.0 - ADAM_B1 ** ADAM_STEP
    c2 = 1.0 - ADAM_B2 ** ADAM_STEP

    def parts_spec(q):
        return pl.BlockSpec((N_DEV, tr, c), lambda l, i: (0, jnp.where(l == q, i, jnp.where(l > q, nt - 1, 0)), 0))

    def body(*refs):
        p_refs = refs[:n_layers]
        w_ref, m_ref, v_ref, g_ref, d_ref, nm_ref, nv_ref = refs[n_layers:]

        def update(p_ref):
            g = p_ref[0].astype(F32)
            for k in range(1, N_DEV):
                g = g + p_ref[k].astype(F32)
            nm = ADAM_B1 * m_ref[...] + (1.0 - ADAM_B1) * g
            nv = ADAM_B2 * v_ref[...] + (1.0 - ADAM_B2) * (g * g)
            g_ref[...] = g
            nm_ref[...] = nm
            nv_ref[...] = nv
            d_ref[...] = -ADAM_LR * ((nm / c1) / (jnp.sqrt(nv / c2) + ADAM_EPS) + ADAM_WD * w_ref[...])

        for q in range(n_layers):
            pl.when(pl.program_id(0) == q)(lambda q=q: update(p_refs[q]))

    out = jax.ShapeDtypeStruct(w.shape, F32)
    return pl.pallas_call(
        body, name=name, grid=(n_layers, nt), in_specs=[parts_spec(q) for q in range(n_layers)] + [spec] * 3,
        out_specs=[spec] * 4, out_shape=[out] * 4, compiler_params=_params(2),
    )(*parts, w, m, v)


def _place():
    x, y, c = lax.axis_index("x"), lax.axis_index("y"), lax.axis_index("c")
    return x, y, c


def _index(px, py, pc):
    return 4 * px + 2 * py + pc


def _all_gather(name, shards):
    n = len(shards)

    def body(*refs):
        in_refs, out_refs = refs[:n], refs[n:2 * n]
        send_sems, recv_sems, local_sems = refs[2 * n:]
        x, y, c = _place()
        me, sibling = (x, y, c), (x, y, 1 - c)
        chips = [(1 - x, y), (x, 1 - y), (1 - x, 1 - y)]

        def slot(k, block):
            return out_refs[k].at[_index(*block)]

        def copy(k, j, block, to, src=None):
            return pltpu.make_async_remote_copy(
                src_ref=slot(k, block) if src is None else src, dst_ref=slot(k, block),
                send_sem=send_sems.at[7 * k + j], recv_sem=recv_sems.at[7 * k + j], device_id=to, device_id_type=MESH)

        mine = [pltpu.make_async_copy(in_refs[k], slot(k, me), local_sems.at[k]) for k in range(n)]
        for cp in mine:
            cp.start()
        first = []
        for k in range(n):
            first.append(copy(k, 0, me, sibling, src=in_refs[k]))
            first += [copy(k, 1 + j, me, (*chip, c), src=in_refs[k]) for j, chip in enumerate(chips)]
        for cp in first:
            cp.start()
        passed = []
        for j, chip in enumerate(chips):
            for k in range(n):
                copy(k, 1 + j, (*chip, c), me).wait_recv()
                cp = copy(k, 4 + j, (*chip, c), sibling)
                cp.start()
                passed.append(cp)
        for k in range(n):
            copy(k, 0, sibling, me).wait_recv()
            for j, chip in enumerate(chips):
                copy(k, 4 + j, (*chip, 1 - c), me).wait_recv()
        for cp in first + passed:
            cp.wait_send()
        for cp in mine:
            cp.wait()

    return pl.pallas_call(
        body, name=name, in_specs=[HBM_SPEC] * n, out_specs=[HBM_SPEC] * n,
        out_shape=[jax.ShapeDtypeStruct((N_DEV,) + s.shape, s.dtype) for s in shards],
        scratch_shapes=[pltpu.SemaphoreType.DMA((7 * n,)), pltpu.SemaphoreType.DMA((7 * n,)),
                        pltpu.SemaphoreType.DMA((n,))],
    )(*shards)


def _all_to_all(name, parts):
    n = len(parts)

    def body(*refs):
        in_refs, out_refs = refs[:n], refs[n:2 * n]
        send_sems, recv_sems, local_sems = refs[2 * n:]
        x, y, c = _place()
        me = _index(x, y, c)
        peers = _xor_peers(x, y, c)

        def copy(k, j):
            peer = peers[j]
            return pltpu.make_async_remote_copy(
                src_ref=in_refs[k].at[_index(*peer)], dst_ref=out_refs[k].at[me],
                send_sem=send_sems.at[7 * k + j], recv_sem=recv_sems.at[7 * k + j], device_id=peer, device_id_type=MESH)

        def landing(k, j):
            peer = peers[j]
            return pltpu.make_async_remote_copy(
                src_ref=in_refs[k].at[me], dst_ref=out_refs[k].at[_index(*peer)],
                send_sem=send_sems.at[7 * k + j], recv_sem=recv_sems.at[7 * k + j], device_id=peer, device_id_type=MESH)

        mine = [pltpu.make_async_copy(in_refs[k].at[me], out_refs[k].at[me], local_sems.at[k]) for k in range(n)]
        sends = [copy(k, j) for j in range(7) for k in range(n)]
        for cp in mine + sends:
            cp.start()
        for k in range(n):
            for j in range(7):
                landing(k, j).wait_recv()
        for cp in sends:
            cp.wait_send()
        for cp in mine:
            cp.wait()

    return pl.pallas_call(
        body, name=name, in_specs=[HBM_SPEC] * n, out_specs=[HBM_SPEC] * n,
        out_shape=[jax.ShapeDtypeStruct(p.shape, p.dtype) for p in parts],
        scratch_shapes=[pltpu.SemaphoreType.DMA((7 * n,)), pltpu.SemaphoreType.DMA((7 * n,)),
                        pltpu.SemaphoreType.DMA((n,))],
    )(*parts)


def _xor_peers(x, y, c):
    peers = []
    for r in range(1, N_DEV):
        rx, ry, rc = (r >> 2) & 1, (r >> 1) & 1, r & 1
        peers.append((1 - x if rx else x, 1 - y if ry else y, 1 - c if rc else c))
    return peers


def _exchange_start(name, srcs, lands, src_by_peer, deps=()):
    n = len(srcs)

    def body(*refs):
        src_refs, land_refs = refs[:n], refs[n:2 * n]
        send_sems, recv_sems = refs[2 * n + len(deps)], refs[2 * n + len(deps) + 1]
        token = refs[-1]
        x, y, c = _place()
        me = _index(x, y, c)
        peers = _xor_peers(x, y, c)
        for k in range(n):
            for j, peer in enumerate(peers):
                src = src_refs[k].at[_index(*peer)] if src_by_peer else src_refs[k]
                pltpu.make_async_remote_copy(
                    src_ref=src, dst_ref=land_refs[k].at[me], send_sem=send_sems.at[7 * k + j],
                    recv_sem=recv_sems.at[7 * k + j], device_id=peer, device_id_type=MESH).start()
        token[...] = jnp.zeros_like(token)

    thru = [pltpu.HBM(a.shape, a.dtype) for a in list(srcs) + list(lands)]
    out = pl.pallas_call(
        body, name=name,
        out_shape=(pltpu.SemaphoreType.DMA((7 * n,)), pltpu.SemaphoreType.DMA((7 * n,)), *thru,
                   jax.ShapeDtypeStruct((8, V7X_LANES), F32)),
        in_specs=[HBM_SPEC] * (2 * n) + [ANY_SPEC] * len(deps),
        out_specs=(SEM_SPEC, SEM_SPEC, *([HBM_SPEC] * (2 * n)), pl.BlockSpec(memory_space=pltpu.VMEM)),
        input_output_aliases={k: 2 + k for k in range(2 * n)},
        compiler_params=pltpu.CompilerParams(has_side_effects=pltpu.SideEffectType.DATAFLOW_SIDE_EFFECTING),
    )(*[pltpu.with_memory_space_constraint(a, pltpu.HBM) for a in list(srcs) + list(lands)], *deps)
    return out[0], out[1], list(out[2:2 + n]), list(out[2 + n:2 + 2 * n]), out[-1]


def _exchange_wait(name, send_sems, recv_sems, srcs, lands, src_by_peer, after):
    n = len(srcs)

    def body(*refs):
        src_refs, land_refs = refs[:n], refs[n:2 * n]
        send_sems, recv_sems = refs[2 * n], refs[2 * n + 1]
        x, y, c = _place()
        peers = _xor_peers(x, y, c)
        for k in range(n):
            for j, peer in enumerate(peers):
                src = src_refs[k].at[_index(*peer)] if src_by_peer else src_refs[k]
                cp = pltpu.make_async_remote_copy(
                    src_ref=src, dst_ref=land_refs[k].at[_index(*peer)], send_sem=send_sems.at[7 * k + j],
                    recv_sem=recv_sems.at[7 * k + j], device_id=peer, device_id_type=MESH)
                cp.wait_send()
                cp.wait_recv()

    thru = [pltpu.HBM(a.shape, a.dtype) for a in list(srcs) + list(lands)]
    out = pl.pallas_call(
        body, name=name, out_shape=tuple(thru),
        in_specs=[HBM_SPEC] * (2 * n) + [SEM_SPEC, SEM_SPEC, ANY_SPEC], out_specs=tuple([HBM_SPEC] * (2 * n)),
        input_output_aliases={k: k for k in range(2 * n)},
        compiler_params=pltpu.CompilerParams(has_side_effects=pltpu.SideEffectType.DATAFLOW_SIDE_EFFECTING),
    )(*srcs, *lands, send_sems, recv_sems, after)
    return list(out[n:])


def _landing(shard, me):
    zone = lax.empty((N_DEV,) + shard.shape, shard.dtype)
    return lax.dynamic_update_slice(zone, shard[None], (me,) + (0,) * shard.ndim)


def _cols_whole(w):
    return jnp.transpose(w, (1, 0, 2)).reshape(w.shape[1], N_DEV * w.shape[2])


def _rows_whole(w):
    return w.reshape(N_DEV * w.shape[1], w.shape[2])


def _cols_parts(dw):
    k, n8 = dw.shape
    return jnp.transpose(dw.reshape(k, N_DEV, n8 // N_DEV), (1, 0, 2))


def _rows_parts(dw):
    r8, c = dw.shape
    return dw.reshape(N_DEV, r8 // N_DEV, c)


def _pack_rows(arrays):
    rows = []
    for a in arrays:
        flat = a.reshape(-1).astype(F32)
        flat = jnp.pad(flat, [(0, (-flat.shape[0]) % PACK_TILE)])
        rows.append(flat.reshape(-1, V7X_LANES))
    return jnp.concatenate(rows, axis=0)


def _unpack_rows(packed, shapes):
    out, row = [], 0
    for s in shapes:
        size = math.prod(s)
        n_rows = -(-size // PACK_TILE) * 8
        out.append(packed[row:row + n_rows].reshape(-1)[:size].reshape(s))
        row += n_rows
    return out


def _merge2d(a):
    return a.reshape(-1, a.shape[-1])


def _ffn_fwd(tag, x, g, w_up, w_down, deps=()):
    f = w_down.shape[0]
    n = _rms_fwd(f"{tag}_norm", x, g, deps)
    gu = _mm(f"{tag}_up", n, w_up)
    act = _swiglu(f"{tag}_swiglu", gu, f)
    out = _mm(f"{tag}_down", act, w_down, res=x, scale=0.5, out_dtype=F32)
    return out, (x, n, gu, act)


def _ffn_bwd(tag, dres, saved, g, w_up, w_down, deps=()):
    x, n, gu, act = saved
    dres32, dres16 = dres
    f = w_down.shape[0]
    dact = _mm(f"{tag}_down_dx", dres16, w_down, tb=True, scale=0.5, deps=deps)
    d_down = _mm(f"{tag}_down_dw", act, dres16, ta=True, scale=0.5)
    dgu = _swiglu_bwd(f"{tag}_swiglu_bwd", dact, gu, f)
    d_up = _mm(f"{tag}_up_dw", n, dgu, ta=True)
    dn = _mm(f"{tag}_up_dx", dgu, w_up, tb=True)
    dx, dg = _rms_bwd(f"{tag}_norm_bwd", x, g, dn, dres32)
    return dx, dg, d_up, d_down


def _conv_mixer_fwd(tag, x, g, w_in, w_conv, w_out, n_seq, seq):
    h = _rms_fwd(f"{tag}_norm", x, g)
    cbv = _mm(f"{tag}_in", h, w_in)
    z = _conv_fwd(f"{tag}_conv", cbv, w_conv, n_seq, seq)
    out = _mm(f"{tag}_out", z, w_out, res=x, out_dtype=F32)
    return out, (x, h, cbv, z)


def _conv_mixer_bwd(tag, dres, saved, g, w_in, w_conv, w_out, n_seq, seq):
    x, h, cbv, z = saved
    dres32, dres16 = dres
    dz = _mm(f"{tag}_out_dx", dres16, w_out, tb=True)
    d_out = _mm(f"{tag}_out_dw", z, dres16, ta=True)
    dc, db, dv, d_conv = _conv_bwd(f"{tag}_conv_bwd", dz, cbv, w_conv, n_seq, seq)
    dcbv = jnp.concatenate([dc, db, dv], axis=1)
    d_in = _mm(f"{tag}_in_dw", h, dcbv, ta=True)
    dh = _mm(f"{tag}_in_dx", dcbv, w_in, tb=True)
    dx, dg = _rms_bwd(f"{tag}_norm_bwd", x, g, dh, dres32)
    return dx, dg, d_in, d_conv, d_out


def _s5_mixer_fwd(tag, x, g, ssm, dskip, w_glu, n_seq, seq):
    a_re, a_im, log_dt, b_re, b_im, c_re, c_im = ssm
    groups, p, hh = b_re.shape
    gb = S5_CHANNELS // hh
    disc, disc_vjp = jax.vjp(_s5_discretize, a_re, a_im, log_dt, b_re, b_im)
    abar_re, abar_im, bbar_re, bbar_im = disc
    mats = (_block_diag_in(bbar_re, gb).astype(BF16), _block_diag_in(bbar_im, gb).astype(BF16),
            _block_diag_out(c_re, gb).astype(BF16), _block_diag_out(c_im, gb).astype(BF16),
            abar_re.reshape(-1, 1, V7X_LANES), abar_im.reshape(-1, 1, V7X_LANES), dskip)
    d = x.shape[1]
    h = _rms_fwd(f"{tag}_norm", x, g)
    s_re, s_im, ypre, z = _s5_fwd(f"{tag}_scan", h, *mats, n_seq, seq)
    vg = _mm(f"{tag}_glu", z, w_glu)
    out = _glu_res(f"{tag}_glu_act", vg, x, d)
    return out, (x, h, s_re, s_im, ypre, z, vg, mats, disc_vjp, (groups, p, hh, gb))


def _s5_mixer_bwd(tag, dres, saved, g, w_glu, n_seq, seq):
    x, h, s_re, s_im, ypre, z, vg, mats, disc_vjp, (groups, p, hh, gb) = saved
    d = x.shape[1]
    dres32, _ = dres
    dvg = _glu_bwd(f"{tag}_glu_act_bwd", dres32, vg, d)
    d_glu = _mm(f"{tag}_glu_dw", z, dvg, ta=True)
    dz = _mm(f"{tag}_glu_dx", dvg, w_glu, tb=True)
    dh, dbin_re, dbin_im, dcout_re, dcout_im, dabar_re, dabar_im, d_skip = _s5_bwd(
        f"{tag}_scan_bwd", dz, ypre, h, s_re, s_im, *mats, n_seq, seq)
    d_are, d_aim, d_logdt, d_bre, d_bim = disc_vjp((
        dabar_re.reshape(groups, p), dabar_im.reshape(groups, p),
        _block_diag_in_t(dbin_re, gb, p, hh), _block_diag_in_t(dbin_im, gb, p, hh)))
    d_cre = _block_diag_out_t(dcout_re, gb, p, hh)
    d_cim = _block_diag_out_t(dcout_im, gb, p, hh)
    dx, dg = _rms_bwd(f"{tag}_norm_bwd", x, g, dh, dres32)
    return dx, dg, (d_are, d_aim, d_logdt, d_bre, d_bim, d_cre, d_cim), d_skip, d_glu


def _xattn_fwd(tag, x, mem, g_q, g_mem, w_q, w_kv, w_o, n_seq, seq, mlen, heads):
    n = _rms_fwd(f"{tag}_norm", x, g_q)
    q = _mm(f"{tag}_q", n, w_q)
    mem_n = _rms_fwd(f"{tag}_mem_norm", mem, g_mem)
    kv = _mm(f"{tag}_kv", mem_n, w_kv)
    o = _attn_fwd(f"{tag}_attn", q, kv, n_seq, seq, mlen, heads)
    out = _mm(f"{tag}_o", o, w_o, res=x, out_dtype=F32)
    return out, (x, n, q, mem_n, kv, o)


def _xattn_bwd(tag, dres, saved, mem, g_q, g_mem, w_q, w_kv, w_o, n_seq, seq, mlen, heads):
    x, n, q, mem_n, kv, o = saved
    dres32, dres16 = dres
    do = _mm(f"{tag}_o_dx", dres16, w_o, tb=True)
    d_o = _mm(f"{tag}_o_dw", o, dres16, ta=True)
    dq, dk, dv = _attn_bwd(f"{tag}_attn_bwd", q, kv, do, n_seq, seq, mlen, heads)
    dkv = jnp.concatenate([dk, dv], axis=1)
    d_q = _mm(f"{tag}_q_dw", n, dq, ta=True)
    dn = _mm(f"{tag}_q_dx", dq, w_q, tb=True)
    d_kv = _mm(f"{tag}_kv_dw", mem_n, dkv, ta=True)
    dmem_n = _mm(f"{tag}_kv_dx", dkv, w_kv, tb=True)
    _, dg_mem = _rms_bwd(f"{tag}_mem_norm_bwd", mem, g_mem, dmem_n)
    dx, dg_q = _rms_bwd(f"{tag}_norm_bwd", x, g_q, dn, dres32)
    return dx, dg_q, dg_mem, d_q, d_kv, d_o


WEIGHT_NAMES = ("norm_g", "final_g", "ffn1_up", "ffn1_down", "ffn2_up", "ffn2_down", "conv_w_in", "conv_w",
                "conv_w_out", "ssm_a_re", "ssm_a_im", "ssm_log_dt", "ssm_b_re", "ssm_b_im", "ssm_c_re", "ssm_c_im",
                "ssm_d", "ssm_w_glu", "xa_w_q", "xa_w_kv", "xa_w_o")
MATRICES = ("ffn1_up", "ffn1_down", "ffn2_up", "ffn2_down", "conv_w_in", "conv_w_out", "ssm_w_glu", "xa_w_q",
            "xa_w_kv", "xa_w_o")
COLUMN_SHARDED = ("ffn1_up", "ffn2_up", "conv_w_in", "ssm_w_glu", "xa_w_kv")
SMALL_SHARDED = ("norm_g", "conv_w", "ssm_d")
REPLICATED = ("ssm_a_re", "ssm_a_im", "ssm_log_dt", "ssm_b_re", "ssm_b_im", "ssm_c_re", "ssm_c_im", "final_g")


def kernel(x, mem, norm_g, final_g, ffn1_up, ffn1_down, ffn2_up, ffn2_down, conv_w_in, conv_w, conv_w_out, ssm_a_re, ssm_a_im, ssm_log_dt, ssm_b_re, ssm_b_im, ssm_c_re, ssm_c_im, ssm_d, ssm_w_glu, xa_w_q, xa_w_kv, xa_w_o, loss_target, m_norm_g, m_final_g, m_ffn1_up, m_ffn1_down, m_ffn2_up, m_ffn2_down, m_conv_w_in, m_conv_w, m_conv_w_out, m_ssm_a_re, m_ssm_a_im, m_ssm_log_dt, m_ssm_b_re, m_ssm_b_im, m_ssm_c_re, m_ssm_c_im, m_ssm_d, m_ssm_w_glu, m_xa_w_q, m_xa_w_kv, m_xa_w_o, v_norm_g, v_final_g, v_ffn1_up, v_ffn1_down, v_ffn2_up, v_ffn2_down, v_conv_w_in, v_conv_w, v_conv_w_out, v_ssm_a_re, v_ssm_a_im, v_ssm_log_dt, v_ssm_b_re, v_ssm_b_im, v_ssm_c_re, v_ssm_c_im, v_ssm_d, v_ssm_w_glu, v_xa_w_q, v_xa_w_kv, v_xa_w_o):
    w = dict(norm_g=norm_g, final_g=final_g, ffn1_up=ffn1_up, ffn1_down=ffn1_down, ffn2_up=ffn2_up,
             ffn2_down=ffn2_down, conv_w_in=conv_w_in, conv_w=conv_w, conv_w_out=conv_w_out, ssm_a_re=ssm_a_re,
             ssm_a_im=ssm_a_im, ssm_log_dt=ssm_log_dt, ssm_b_re=ssm_b_re, ssm_b_im=ssm_b_im, ssm_c_re=ssm_c_re,
             ssm_c_im=ssm_c_im, ssm_d=ssm_d, ssm_w_glu=ssm_w_glu, xa_w_q=xa_w_q, xa_w_kv=xa_w_kv, xa_w_o=xa_w_o)
    mom = dict(norm_g=m_norm_g, final_g=m_final_g, ffn1_up=m_ffn1_up, ffn1_down=m_ffn1_down, ffn2_up=m_ffn2_up,
               ffn2_down=m_ffn2_down, conv_w_in=m_conv_w_in, conv_w=m_conv_w, conv_w_out=m_conv_w_out,
               ssm_a_re=m_ssm_a_re, ssm_a_im=m_ssm_a_im, ssm_log_dt=m_ssm_log_dt, ssm_b_re=m_ssm_b_re,
               ssm_b_im=m_ssm_b_im, ssm_c_re=m_ssm_c_re, ssm_c_im=m_ssm_c_im, ssm_d=m_ssm_d, ssm_w_glu=m_ssm_w_glu,
               xa_w_q=m_xa_w_q, xa_w_kv=m_xa_w_kv, xa_w_o=m_xa_w_o)
    var = dict(norm_g=v_norm_g, final_g=v_final_g, ffn1_up=v_ffn1_up, ffn1_down=v_ffn1_down, ffn2_up=v_ffn2_up,
               ffn2_down=v_ffn2_down, conv_w_in=v_conv_w_in, conv_w=v_conv_w, conv_w_out=v_conv_w_out,
               ssm_a_re=v_ssm_a_re, ssm_a_im=v_ssm_a_im, ssm_log_dt=v_ssm_log_dt, ssm_b_re=v_ssm_b_re,
               ssm_b_im=v_ssm_b_im, ssm_c_re=v_ssm_c_re, ssm_c_im=v_ssm_c_im, ssm_d=v_ssm_d, ssm_w_glu=v_ssm_w_glu,
               xa_w_q=v_xa_w_q, xa_w_kv=v_xa_w_kv, xa_w_o=v_xa_w_o)

    n_seq, seq, d = x.shape
    mlen = mem.shape[1]
    depth, n_norms = norm_g.shape[0], norm_g.shape[1]
    heads = 4
    tokens = n_seq * seq
    x2 = x.reshape(tokens, d)
    mem2 = mem.reshape(n_seq * mlen, d)
    tgt2 = loss_target.reshape(tokens, d)

    small_shapes = [w[k].shape for k in SMALL_SHARDED]
    small_rows = [_merge2d(w[k]) for k in SMALL_SHARDED]
    small_counts = [s.shape[0] for s in small_rows]
    small = jnp.concatenate(small_rows, axis=0)
    small = jnp.pad(small, [(0, (-small.shape[0]) % 8), (0, 0)])
    me = _index(*_place())

    def layer_weights(i):
        names = [(k, i) for k in ("ffn1_up", "ffn1_down", "ffn2_up", "ffn2_down", "xa_w_q", "xa_w_kv", "xa_w_o")]
        return names + ([("conv_w_in", i // 2), ("conv_w_out", i // 2)] if i % 2 == 0 else [("ssm_w_glu", i // 2)])

    shards = [[w[k][idx].astype(BF16) for k, idx in layer_weights(i)] for i in range(depth)]
    gathered = _all_gather("gather_layer0", shards[0] + [small])
    small_all = gathered[-1]
    blocks = [gathered[:-1]] + [None] * (depth - 1)
    in_flight = [None] * depth
    token = gathered[0]
    for i in range(1, depth):
        zones = [_landing(s, me) for s in shards[i]]
        *in_flight[i], token = _exchange_start(f"gather_start_l{i}", shards[i], zones, False, deps=[token])

    def small_whole(idx):
        start = sum(small_counts[:idx])
        part = small_all[:, start:start + small_counts[idx]]
        lead = small_shapes[idx][:-1]
        part = part.reshape((N_DEV,) + lead + (part.shape[-1],))
        part = jnp.moveaxis(part, 0, -2)
        return part.reshape(lead + (N_DEV * part.shape[-1],))

    norm_all = small_whole(0)
    conv_all = small_whole(1)
    dskip_all = small_whole(2)

    def whole(i):
        return {k: _cols_whole(blk) if k in COLUMN_SHARDED else _rows_whole(blk)
                for (k, _), blk in zip(layer_weights(i), blocks[i])}

    saved = []
    cur = x2
    for i in range(depth):
        g = [norm_all[i, k].reshape(1, d) for k in range(n_norms)]
        j = i // 2
        if i > 0:
            blocks[i] = _exchange_wait(f"gather_wait_l{i}", *in_flight[i], False, cur)
        lw = whole(i)
        cur, s_ffn1 = _ffn_fwd(f"l{i}_ffn1", cur, g[0], lw["ffn1_up"], lw["ffn1_down"], [token] if i == 0 else ())
        if i % 2 == 0:
            lw["conv_w"] = conv_all[j]
            cur, s_mix = _conv_mixer_fwd(f"l{i}_conv", cur, g[1], lw["conv_w_in"], lw["conv_w"], lw["conv_w_out"],
                                         n_seq, seq)
        else:
            ssm = tuple(w[k][j] for k in ("ssm_a_re", "ssm_a_im", "ssm_log_dt", "ssm_b_re", "ssm_b_im",
                                          "ssm_c_re", "ssm_c_im"))
            cur, s_mix = _s5_mixer_fwd(f"l{i}_s5", cur, g[1], ssm, dskip_all[j].reshape(1, d), lw["ssm_w_glu"],
                                       n_seq, seq)
        cur, s_xa = _xattn_fwd(f"l{i}_xa", cur, mem2, g[2], g[3], lw["xa_w_q"], lw["xa_w_kv"], lw["xa_w_o"],
                               n_seq, seq, mlen, heads)
        cur, s_ffn2 = _ffn_fwd(f"l{i}_ffn2", cur, g[4], lw["ffn2_up"], lw["ffn2_down"])
        saved.append((g, lw, s_ffn1, s_mix, s_xa, s_ffn2))

    dres, err2, d_final = _final_loss("loss_head", cur, final_g.reshape(1, d), tgt2)
    loss = lax.psum(0.5 * jnp.sum(err2) / d, ("x", "y", "c"))

    d_norm = [[None] * n_norms for _ in range(depth)]
    d_conv = [None] * conv_w.shape[0]
    d_skip = [None] * ssm_d.shape[0]
    d_ssm = [None] * ssm_a_re.shape[0]
    leaving = [None] * depth
    deps = ()
    for i in reversed(range(depth)):
        g, lw, s_ffn1, s_mix, s_xa, s_ffn2 = saved[i]
        j = i // 2
        gm = {}
        dres, d_norm[i][4], gm["ffn2_up"], gm["ffn2_down"] = _ffn_bwd(
            f"l{i}_ffn2", dres, s_ffn2, g[4], lw["ffn2_up"], lw["ffn2_down"], deps)
        dres, d_norm[i][2], d_norm[i][3], gm["xa_w_q"], gm["xa_w_kv"], gm["xa_w_o"] = _xattn_bwd(
            f"l{i}_xa", dres, s_xa, mem2, g[2], g[3], lw["xa_w_q"], lw["xa_w_kv"], lw["xa_w_o"], n_seq, seq, mlen,
            heads)
        if i % 2 == 0:
            dres, d_norm[i][1], gm["conv_w_in"], d_conv[j], gm["conv_w_out"] = _conv_mixer_bwd(
                f"l{i}_conv", dres, s_mix, g[1], lw["conv_w_in"], lw["conv_w"], lw["conv_w_out"], n_seq, seq)
        else:
            dres, d_norm[i][1], d_ssm[j], d_skip[j], gm["ssm_w_glu"] = _s5_mixer_bwd(
                f"l{i}_s5", dres, s_mix, g[1], lw["ssm_w_glu"], n_seq, seq)
        dres, d_norm[i][0], gm["ffn1_up"], gm["ffn1_down"] = _ffn_bwd(
            f"l{i}_ffn1", dres, s_ffn1, g[0], lw["ffn1_up"], lw["ffn1_down"])
        parts = [(_cols_parts if k in COLUMN_SHARDED else _rows_parts)(gm[k]) for k, _ in layer_weights(i)]
        zones = [_landing(lax.dynamic_index_in_dim(p, me, 0, keepdims=False), me) for p in parts]
        *leaving[i], token = _exchange_start(f"grads_start_l{i}", parts, zones, True)
        deps = [token]
    grad_x = dres[0].reshape(n_seq, seq, d)
    received = {k: [None] * w[k].shape[0] for k in MATRICES}
    for i in range(depth):
        arrived = _exchange_wait(f"grads_wait_l{i}", *leaving[i], True, dres[0])
        for (k, idx), blk in zip(layer_weights(i), arrived):
            received[k][idx] = blk

    d_norm_all = jnp.stack([jnp.concatenate(row, axis=0) for row in d_norm])
    d_conv_all = jnp.stack(d_conv)
    d_skip_all = jnp.concatenate(d_skip, axis=0)

    def small_parts(full):
        lead = full.shape[:-1]
        t = full.reshape(lead + (N_DEV, full.shape[-1] // N_DEV))
        t = jnp.moveaxis(t, -2, 0)
        return t.reshape(N_DEV, -1, t.shape[-1])

    small_g = jnp.concatenate([small_parts(a) for a in (d_norm_all, d_conv_all, d_skip_all)], axis=1)
    small_g = jnp.pad(small_g, [(0, 0), (0, (-small_g.shape[1]) % 8), (0, 0)])
    small_received = _all_to_all("exchange_small_grads", [small_g])[0]

    rep_grads = [jnp.stack([d_ssm[j][k] for j in range(len(d_ssm))]) for k in range(7)] + [d_final.reshape(-1)]
    rep_shapes = [w[k].shape for k in REPLICATED]
    rep_all = _all_gather("gather_replicated_grads", [_pack_rows(rep_grads)])[0]

    grads, deltas, new_m, new_v = {}, {}, {}, {}
    for k in MATRICES:
        shape = w[k].shape
        out = _adamw(f"adamw_{k}", received[k], _merge2d(w[k]), _merge2d(mom[k]), _merge2d(var[k]))
        grads[k], deltas[k], new_m[k], new_v[k] = [o.reshape(shape) for o in out]

    def small_local(src):
        rows = jnp.concatenate([_merge2d(src[k]) for k in SMALL_SHARDED], axis=0)
        return jnp.pad(rows, [(0, (-rows.shape[0]) % 8), (0, 0)])

    out = _adamw("adamw_small", [small_received], small, small_local(mom), small_local(var))
    for res, o in zip((grads, deltas, new_m, new_v), out):
        start = 0
        for k, cnt, shape in zip(SMALL_SHARDED, small_counts, small_shapes):
            res[k] = o[start:start + cnt].reshape(shape)
            start += cnt

    out = _adamw("adamw_replicated", [rep_all], _pack_rows([w[k] for k in REPLICATED]),
                 _pack_rows([mom[k] for k in REPLICATED]), _pack_rows([var[k] for k in REPLICATED]))
    for res, o in zip((grads, deltas, new_m, new_v), out):
        for k, a in zip(REPLICATED, _unpack_rows(o, rep_shapes)):
            res[k] = a

    return (loss, grad_x, *[grads[k] for k in WEIGHT_NAMES], *[deltas[k] for k in WEIGHT_NAMES],
            *[new_m[k] for k in WEIGHT_NAMES], *[new_v[k] for k in WEIGHT_NAMES])
```

```python
import math

import jax
import jax.numpy as jnp
from jax import lax
from jax.experimental import pallas as pl
from jax.experimental.pallas import tpu as pltpu

F32 = jnp.float32
BF16 = jnp.bfloat16
MESH = pl.DeviceIdType.MESH
N_DEV = 8

NORM_EPS = 1e-6
EIG_CLIP = -1e-4
CONV_WIDTH = 3
ADAM_LR = 0.001
ADAM_B1 = 0.9
ADAM_B2 = 0.999
ADAM_EPS = 1e-08
ADAM_WD = 0.01
ADAM_STEP = 10
GELU_C = math.sqrt(2.0 / math.pi)
GELU_A = 0.044715

V7X_LANES = 128
V7X_VMEM_LIMIT = 56 * 1024 * 1024
S5_CHANNELS = 128
PACK_TILE = 8 * V7X_LANES

HBM_SPEC = pl.BlockSpec(memory_space=pltpu.HBM)
ANY_SPEC = pl.BlockSpec(memory_space=pl.ANY)
SEM_SPEC = pl.BlockSpec(memory_space=pltpu.SEMAPHORE)


def _params(n_grid):
    return pltpu.CompilerParams(dimension_semantics=("arbitrary",) * n_grid, vmem_limit_bytes=V7X_VMEM_LIMIT)


def _pick(n, pref, align):
    if n <= pref:
        return n
    t = (pref // align) * align
    while t >= align:
        if n % t == 0:
            return t
        t -= align
    raise ValueError(f"no tile for {n} (pref {pref}, align {align})")


MM_RHS_BLOCK_BYTES = 12 * 1024 * 1024
MM_LHS_BLOCK_BYTES = 6 * 1024 * 1024
MM_ACC_BYTES = 6 * 1024 * 1024
MM_ROWS = 512


def _mm_tiles(m, k, n, a_item, b_item, ta):
    tn = _pick(n, max(V7X_LANES, MM_RHS_BLOCK_BYTES // (k * b_item)), V7X_LANES)
    rows = min(MM_ROWS, MM_ACC_BYTES // (4 * tn), MM_LHS_BLOCK_BYTES // (k * a_item))
    align = V7X_LANES if ta else 16
    tm = _pick(m, max(align, rows), align)
    return tm, tn


def _mm(name, a, b, *, ta=False, tb=False, out_dtype=BF16, res=None, scale=None, deps=()):
    if ta:
        k, m = a.shape
    else:
        m, k = a.shape
    if tb:
        n, k2 = b.shape
    else:
        k2, n = b.shape
    assert k == k2, (name, a.shape, b.shape)
    tm, tn = _mm_tiles(m, k, n, a.dtype.itemsize, b.dtype.itemsize, ta)
    a_spec = pl.BlockSpec((k, tm), lambda j, i: (0, i)) if ta else pl.BlockSpec((tm, k), lambda j, i: (i, 0))
    b_spec = pl.BlockSpec((tn, k), lambda j, i: (j, 0)) if tb else pl.BlockSpec((k, tn), lambda j, i: (0, j))
    o_spec = pl.BlockSpec((tm, tn), lambda j, i: (i, j))
    dims = (((0 if ta else 1,), (1 if tb else 0,)), ((), ()))
    has_res = res is not None

    def body(*refs):
        a_ref, b_ref = refs[0], refs[1]
        o_ref = refs[-1]
        acc = lax.dot_general(a_ref[...].astype(BF16), b_ref[...].astype(BF16), dims, preferred_element_type=F32)
        if scale is not None:
            acc = acc * scale
        if has_res:
            acc = acc + refs[2][...].astype(F32)
        o_ref[...] = acc.astype(o_ref.dtype)

    ins = [a, b] + ([res] if has_res else []) + list(deps)
    specs = [a_spec, b_spec] + ([o_spec] if has_res else []) + [ANY_SPEC] * len(deps)
    return pl.pallas_call(
        body, name=name, grid=(n // tn, m // tm), in_specs=specs, out_specs=o_spec,
        out_shape=jax.ShapeDtypeStruct((m, n), out_dtype), compiler_params=_params(2),
    )(*ins)


def _rowwise(name, fn, rows, row_ins, par_ins, row_outs, acc_outs=(), tm_pref=256, deps=()):
    tm = _pick(rows, tm_pref, 16)
    in_specs, ins = [], []
    for r in row_ins:
        arr, cb, cw = r if isinstance(r, tuple) else (r, 0, r.shape[1])
        assert arr.shape[0] == rows, (name, arr.shape, rows)
        ins.append(arr)
        in_specs.append(pl.BlockSpec((tm, cw), lambda i, cb=cb: (i, cb)))
    for p in par_ins:
        ins.append(p)
        in_specs.append(pl.BlockSpec(p.shape, lambda i: (0, 0)))
    out_specs = [pl.BlockSpec((tm, c), lambda i: (i, 0)) for c, _ in row_outs]
    out_specs += [pl.BlockSpec((r, c), lambda i: (0, 0)) for r, c in acc_outs]
    out_shape = [jax.ShapeDtypeStruct((rows, c), dt) for c, dt in row_outs]
    out_shape += [jax.ShapeDtypeStruct((r, c), F32) for r, c in acc_outs]
    n_in, n_row = len(ins), len(row_outs)
    ins += list(deps)
    in_specs += [ANY_SPEC] * len(deps)

    def body(*refs):
        vals = [r[...] for r in refs[:n_in]]
        outs = refs[n_in + len(deps):]
        res = fn(*vals)
        if not isinstance(res, (tuple, list)):
            res = (res,)
        for o, v in zip(outs[:n_row], res[:n_row]):
            if isinstance(v, (tuple, list)):
                off = 0
                for piece in v:
                    w = piece.shape[1]
                    o[:, off:off + w] = piece.astype(o.dtype)
                    off += w
            else:
                o[...] = v.astype(o.dtype)
        if len(outs) > n_row:
            @pl.when(pl.program_id(0) == 0)
            def _():
                for o in outs[n_row:]:
                    o[...] = jnp.zeros_like(o)

            for o, v in zip(outs[n_row:], res[n_row:]):
                o[...] += v

    out = pl.pallas_call(
        body, name=name, grid=(rows // tm,), in_specs=in_specs, out_specs=out_specs, out_shape=out_shape,
        compiler_params=_params(1),
    )(*ins)
    return out


def _inv_rms(x):
    return lax.rsqrt(jnp.mean(x * x, axis=-1, keepdims=True) + NORM_EPS)


def _rms_fwd(name, x, g, deps=()):
    def fn(x, g):
        return x * _inv_rms(x) * g

    return _rowwise(name, fn, x.shape[0], [x], [g], [(x.shape[1], BF16)], tm_pref=512, deps=deps)[0]


def _rms_bwd(name, x, g, dn, dres=None):
    d = x.shape[1]

    def fn(x, dn, *rest):
        g = rest[-1]
        dn = dn.astype(F32)
        r = _inv_rms(x)
        xh = x * r
        dg = jnp.sum(dn * xh, axis=0, keepdims=True)
        if dres is None:
            return (dg,)
        dxh = dn * g
        dx = r * (dxh - xh * jnp.mean(dxh * xh, axis=-1, keepdims=True)) + rest[0]
        return dx, dx, dg

    row_ins = [x, dn] + ([] if dres is None else [dres])
    row_outs = [] if dres is None else [(d, F32), (d, BF16)]
    out = _rowwise(name, fn, x.shape[0], row_ins, [g], row_outs, [(1, d)], tm_pref=256)
    return (None, out[0]) if dres is None else ((out[0], out[1]), out[2])


def _sigmoid(x):
    return 1.0 / (1.0 + jnp.exp(-x))


def _swiglu(name, gu, f):
    def fn(gt, up):
        gt = gt.astype(F32)
        return gt * _sigmoid(gt) * up.astype(F32)

    return _rowwise(name, fn, gu.shape[0], [(gu, 0, f), (gu, 1, f)], [], [(f, BF16)])[0]


def _swiglu_bwd(name, dact, gu, f):
    def fn(dact, gt, up):
        dact, gt, up = dact.astype(F32), gt.astype(F32), up.astype(F32)
        sg = _sigmoid(gt)
        dgt = dact * up * (sg * (1.0 + gt * (1.0 - sg)))
        dup = dact * (gt * sg)
        return ((dgt, dup),)

    return _rowwise(name, fn, gu.shape[0], [dact, (gu, 0, f), (gu, 1, f)], [], [(2 * f, BF16)])[0]


def _glu_res(name, vg, x, d):
    def fn(val, gate, x):
        return x + val.astype(F32) * _sigmoid(gate.astype(F32))

    return _rowwise(name, fn, x.shape[0], [(vg, 0, d), (vg, 1, d), x], [], [(d, F32)])[0]


def _glu_bwd(name, dres, vg, d):
    def fn(dres, val, gate):
        val, gate = val.astype(F32), gate.astype(F32)
        sg = _sigmoid(gate)
        return ((dres * sg, dres * val * sg * (1.0 - sg)),)

    return _rowwise(name, fn, dres.shape[0], [dres, (vg, 0, d), (vg, 1, d)], [], [(2 * d, BF16)])[0]


def _final_loss(name, x, g, tgt):
    d = x.shape[1]

    def fn(x, tgt, g):
        r = _inv_rms(x)
        xh = x * r
        err = xh * g - tgt
        dy = err * (1.0 / d)
        dxh = dy * g
        dx = r * (dxh - xh * jnp.mean(dxh * xh, axis=-1, keepdims=True))
        return dx, dx, jnp.sum(err * err, axis=0, keepdims=True), jnp.sum(dy * xh, axis=0, keepdims=True)

    dx, dx16, err2, dg = _rowwise(name, fn, x.shape[0], [x, tgt], [g], [(d, F32), (d, BF16)], [(1, d), (1, d)])
    return (dx, dx16), err2, dg


def _shift_down(u, k):
    rows = lax.broadcasted_iota(jnp.int32, u.shape, 0)
    return jnp.where(rows >= k, pltpu.roll(u, k, 0), 0.0)


def _shift_up(u, k):
    n = u.shape[0]
    rows = lax.broadcasted_iota(jnp.int32, u.shape, 0)
    return jnp.where(rows < n - k, pltpu.roll(u, n - k, 0), 0.0)


def _conv_specs(seq, cw, n_cb, swap):
    def at(off):
        if swap:
            return pl.BlockSpec((seq, cw), lambda j, b: (b, off * n_cb + j))
        return pl.BlockSpec((seq, cw), lambda b, j: (b, off * n_cb + j))

    return at


def _conv_fwd(name, cbv, w, n_seq, seq):
    d = w.shape[1]
    cw = _pick(d, 256, V7X_LANES)
    n_cb = d // cw
    at = _conv_specs(seq, cw, n_cb, swap=False)

    def body(c_ref, b_ref, v_ref, w_ref, z_ref):
        u = c_ref[...].astype(F32) * v_ref[...].astype(F32)
        cv = w_ref[0:1, :] * _shift_down(u, 2) + w_ref[1:2, :] * _shift_down(u, 1) + w_ref[2:3, :] * u
        z_ref[...] = (b_ref[...].astype(F32) * cv).astype(z_ref.dtype)

    return pl.pallas_call(
        body, name=name, grid=(n_seq, n_cb),
        in_specs=[at(0), at(1), at(2), pl.BlockSpec((CONV_WIDTH, cw), lambda b, j: (0, j))],
        out_specs=at(0), out_shape=jax.ShapeDtypeStruct((n_seq * seq, d), BF16), compiler_params=_params(2),
    )(cbv, cbv, cbv, w)


def _conv_bwd(name, dz, cbv, w, n_seq, seq):
    d = w.shape[1]
    cw = _pick(d, 256, V7X_LANES)
    n_cb = d // cw
    at = _conv_specs(seq, cw, n_cb, swap=True)

    def body(dz_ref, c_ref, b_ref, v_ref, w_ref, dc_ref, db_ref, dv_ref, dw_ref):
        c, b, v = c_ref[...].astype(F32), b_ref[...].astype(F32), v_ref[...].astype(F32)
        dz = dz_ref[...].astype(F32)
        w0, w1, w2 = w_ref[0:1, :], w_ref[1:2, :], w_ref[2:3, :]
        u = c * v
        u1, u2 = _shift_down(u, 1), _shift_down(u, 2)
        cv = w0 * u2 + w1 * u1 + w2 * u
        db_ref[...] = (dz * cv).astype(db_ref.dtype)
        dcv = dz * b
        du = w2 * dcv + w1 * _shift_up(dcv, 1) + w0 * _shift_up(dcv, 2)
        dc_ref[...] = (du * v).astype(dc_ref.dtype)
        dv_ref[...] = (du * c).astype(dv_ref.dtype)

        @pl.when(pl.program_id(1) == 0)
        def _():
            dw_ref[...] = jnp.zeros_like(dw_ref)

        dw_ref[0:1, :] += jnp.sum(dcv * u2, axis=0, keepdims=True)
        dw_ref[1:2, :] += jnp.sum(dcv * u1, axis=0, keepdims=True)
        dw_ref[2:3, :] += jnp.sum(dcv * u, axis=0, keepdims=True)

    act = jax.ShapeDtypeStruct((n_seq * seq, d), BF16)
    return pl.pallas_call(
        body, name=name, grid=(n_cb, n_seq),
        in_specs=[at(0), at(0), at(1), at(2), pl.BlockSpec((CONV_WIDTH, cw), lambda j, b: (0, j))],
        out_specs=[at(0), at(0), at(0), pl.BlockSpec((CONV_WIDTH, cw), lambda j, b: (0, j))],
        out_shape=[act, act, act, jax.ShapeDtypeStruct((CONV_WIDTH, d), F32)], compiler_params=_params(2),
    )(dz, cbv, cbv, cbv, w)


def _s5_discretize(a_re, a_im, log_dt, b_re, b_im):
    lam_re = jnp.minimum(a_re, EIG_CLIP)
    lam_im = a_im
    dt = jnp.exp(log_dt)[:, None]
    mag = jnp.exp(lam_re * dt)
    abar_re = mag * jnp.cos(lam_im * dt)
    abar_im = mag * jnp.sin(lam_im * dt)
    den = lam_re * lam_re + lam_im * lam_im
    num_re = abar_re - 1.0
    num_im = abar_im
    coef_re = (num_re * lam_re + num_im * lam_im) / den
    coef_im = (num_im * lam_re - num_re * lam_im) / den
    bbar_re = coef_re[..., None] * b_re - coef_im[..., None] * b_im
    bbar_im = coef_re[..., None] * b_im + coef_im[..., None] * b_re
    return abar_re, abar_im, bbar_re, bbar_im


def _block_diag_in(bbar, gb):
    g, p, h = bbar.shape
    t = jnp.transpose(bbar.reshape(g // gb, gb, p, h), (0, 1, 3, 2))
    return jnp.einsum("cghp,gk->cghkp", t, jnp.eye(gb, dtype=bbar.dtype)).reshape(g // gb, gb * h, gb * p)


def _block_diag_in_t(blk, gb, p, h):
    nb = blk.shape[0]
    t = jnp.einsum("cghkp,gk->cghp", blk.reshape(nb, gb, h, gb, p), jnp.eye(gb, dtype=blk.dtype))
    return jnp.transpose(t, (0, 1, 3, 2)).reshape(nb * gb, p, h)


def _block_diag_out(c, gb):
    g, h, p = c.shape
    t = jnp.transpose(c.reshape(g // gb, gb, h, p), (0, 1, 3, 2))
    return jnp.einsum("cgph,gk->cgpkh", t, jnp.eye(gb, dtype=c.dtype)).reshape(g // gb, gb * p, gb * h)


def _block_diag_out_t(blk, gb, p, h):
    nb = blk.shape[0]
    t = jnp.einsum("cgpkh,gk->cgph", blk.reshape(nb, gb, p, gb, h), jnp.eye(gb, dtype=blk.dtype))
    return jnp.transpose(t, (0, 1, 3, 2)).reshape(nb * gb, h, p)


def _gelu(y):
    return 0.5 * y * (1.0 + jnp.tanh(GELU_C * (y + GELU_A * y * y * y)))


def _gelu_grad(y):
    th = jnp.tanh(GELU_C * (y + GELU_A * y * y * y))
    return 0.5 * (1.0 + th) + 0.5 * y * (1.0 - th * th) * GELU_C * (1.0 + 3.0 * GELU_A * y * y)


def _dot(a, b, ca, cb):
    return lax.dot_general(a.astype(BF16), b.astype(BF16), (((ca,), (cb,)), ((), ())), preferred_element_type=F32)


def _s5_specs(seq, ch, sb):
    act = pl.BlockSpec((seq, ch), lambda j, b: (b, j))
    state = pl.BlockSpec((seq, sb), lambda j, b: (b, j))
    w_in = pl.BlockSpec((None, ch, sb), lambda j, b: (j, 0, 0))
    w_out = pl.BlockSpec((None, sb, ch), lambda j, b: (j, 0, 0))
    lane_s = pl.BlockSpec((1, sb), lambda j, b: (0, j))
    lane_c = pl.BlockSpec((1, ch), lambda j, b: (0, j))
    return act, state, w_in, w_out, lane_s, lane_c


def _s5_fwd(name, h, bin_re, bin_im, cout_re, cout_im, abar_re, abar_im, dskip, n_seq, seq):
    t, d = h.shape
    nb, ch, sb = bin_re.shape
    act, state, w_in, w_out, lane_s, lane_c = _s5_specs(seq, ch, sb)

    def body(h_ref, bre_ref, bim_ref, cre_ref, cim_ref, ar_ref, ai_ref, d_ref, sre_ref, sim_ref, y_ref, z_ref):
        u = h_ref[...]
        sre_ref[...] = _dot(u, bre_ref[...], 1, 0)
        sim_ref[...] = _dot(u, bim_ref[...], 1, 0)
        ar, ai = ar_ref[...], ai_ref[...]

        def step(i, carry):
            sr, si = carry
            row = pl.ds(i, 1)
            nr = ar * sr - ai * si + sre_ref[row, :]
            ni = ar * si + ai * sr + sim_ref[row, :]
            sre_ref[row, :] = nr
            sim_ref[row, :] = ni
            return nr, ni

        zero = jnp.zeros((1, sb), F32)
        lax.fori_loop(0, seq, step, (zero, zero), unroll=8)
        y = _dot(sre_ref[...], cre_ref[...], 1, 0) - _dot(sim_ref[...], cim_ref[...], 1, 0)
        y = y + d_ref[...] * u.astype(F32)
        y_ref[...] = y
        z_ref[...] = _gelu(y).astype(z_ref.dtype)

    return pl.pallas_call(
        body, name=name, grid=(nb, n_seq),
        in_specs=[act, w_in, w_in, w_out, w_out, lane_s, lane_s, lane_c],
        out_specs=[state, state, act, act],
        out_shape=[jax.ShapeDtypeStruct((t, nb * sb), F32), jax.ShapeDtypeStruct((t, nb * sb), F32),
                   jax.ShapeDtypeStruct((t, d), F32), jax.ShapeDtypeStruct((t, d), BF16)],
        compiler_params=_params(2),
    )(h, bin_re, bin_im, cout_re, cout_im, abar_re, abar_im, dskip)


def _s5_bwd(name, dz, ypre, h, s_re, s_im, bin_re, bin_im, cout_re, cout_im, abar_re, abar_im, dskip, n_seq, seq):
    t, d = h.shape
    nb, ch, sb = bin_re.shape
    act, state, w_in, w_out, lane_s, lane_c = _s5_specs(seq, ch, sb)

    def body(dz_ref, y_ref, h_ref, sre_ref, sim_ref, bre_ref, bim_ref, cre_ref, cim_ref, ar_ref, ai_ref, d_ref,
             dh_ref, dbre_ref, dbim_ref, dcre_ref, dcim_ref, dar_ref, dai_ref, dd_ref, gre, gim):
        first = pl.program_id(1) == 0
        u = h_ref[...].astype(F32)
        dy = dz_ref[...].astype(F32) * _gelu_grad(y_ref[...])
        gre[...] = _dot(dy, cre_ref[...], 1, 1)
        gim[...] = -_dot(dy, cim_ref[...], 1, 1)
        ar, ai = ar_ref[...], ai_ref[...]

        def step(i, carry):
            gr, gi = carry
            row = pl.ds(seq - 1 - i, 1)
            nr = gre[row, :] + ar * gr + ai * gi
            ni = gim[row, :] - ai * gr + ar * gi
            gre[row, :] = nr
            gim[row, :] = ni
            return nr, ni

        zero = jnp.zeros((1, sb), F32)
        lax.fori_loop(0, seq, step, (zero, zero), unroll=8)

        g_re, g_im = gre[...], gim[...]
        s_re, s_im = sre_ref[...], sim_ref[...]
        p_re, p_im = _shift_down(s_re, 1), _shift_down(s_im, 1)
        dar = jnp.sum(g_re * p_re + g_im * p_im, axis=0, keepdims=True)
        dai = jnp.sum(g_im * p_re - g_re * p_im, axis=0, keepdims=True)
        dbre = _dot(u, g_re, 0, 0)
        dbim = _dot(u, g_im, 0, 0)
        dcre = _dot(s_re, dy, 0, 0)
        dcim = -_dot(s_im, dy, 0, 0)
        ddd = jnp.sum(dy * u, axis=0, keepdims=True)
        dh_ref[...] = _dot(g_re, bre_ref[...], 1, 1) + _dot(g_im, bim_ref[...], 1, 1) + d_ref[...] * dy

        @pl.when(first)
        def _():
            dar_ref[...] = dar
            dai_ref[...] = dai
            dbre_ref[...] = dbre
            dbim_ref[...] = dbim
            dcre_ref[...] = dcre
            dcim_ref[...] = dcim
            dd_ref[...] = ddd

        @pl.when(jnp.logical_not(first))
        def _():
            dar_ref[...] += dar
            dai_ref[...] += dai
            dbre_ref[...] += dbre
            dbim_ref[...] += dbim
            dcre_ref[...] += dcre
            dcim_ref[...] += dcim
            dd_ref[...] += ddd

    return pl.pallas_call(
        body, name=name, grid=(nb, n_seq),
        in_specs=[act, act, act, state, state, w_in, w_in, w_out, w_out, lane_s, lane_s, lane_c],
        out_specs=[act, w_in, w_in, w_out, w_out, lane_s, lane_s, lane_c],
        out_shape=[jax.ShapeDtypeStruct((t, d), F32),
                   jax.ShapeDtypeStruct((nb, ch, sb), F32), jax.ShapeDtypeStruct((nb, ch, sb), F32),
                   jax.ShapeDtypeStruct((nb, sb, ch), F32), jax.ShapeDtypeStruct((nb, sb, ch), F32),
                   jax.ShapeDtypeStruct((1, nb * sb), F32), jax.ShapeDtypeStruct((1, nb * sb), F32),
                   jax.ShapeDtypeStruct((1, d), F32)],
        scratch_shapes=[pltpu.VMEM((seq, sb), F32), pltpu.VMEM((seq, sb), F32)],
        compiler_params=_params(2),
    )(dz, ypre, h, s_re, s_im, bin_re, bin_im, cout_re, cout_im, abar_re, abar_im, dskip)


def _softmax_rows(q, k, scale):
    s = _dot(q, k, 1, 1) * scale
    e = jnp.exp(s - jnp.max(s, axis=-1, keepdims=True))
    return e / jnp.sum(e, axis=-1, keepdims=True)


def _attn_fwd(name, q, kv, n_seq, seq, mlen, heads):
    t, d = q.shape
    hd = d // heads
    tq = _pick(seq, 512, 16)
    nq = seq // tq
    scale = hd ** -0.5
    q_spec = pl.BlockSpec((tq, hd), lambda b, h, i: (b * nq + i, h))

    def body(q_ref, k_ref, v_ref, o_ref):
        p = _softmax_rows(q_ref[...], k_ref[...], scale)
        o_ref[...] = _dot(p, v_ref[...], 1, 0).astype(o_ref.dtype)

    return pl.pallas_call(
        body, name=name, grid=(n_seq, heads, nq),
        in_specs=[q_spec, pl.BlockSpec((mlen, hd), lambda b, h, i: (b, h)),
                  pl.BlockSpec((mlen, hd), lambda b, h, i: (b, heads + h))],
        out_specs=q_spec, out_shape=jax.ShapeDtypeStruct((t, d), BF16), compiler_params=_params(3),
    )(q, kv, kv)


def _attn_bwd(name, q, kv, do, n_seq, seq, mlen, heads):
    t, d = q.shape
    hd = d // heads
    tq = _pick(seq, 512, 16)
    nq = seq // tq
    scale = hd ** -0.5
    q_spec = pl.BlockSpec((tq, hd), lambda b, h, i: (b * nq + i, h))
    k_spec = pl.BlockSpec((mlen, hd), lambda b, h, i: (b, h))

    def body(q_ref, k_ref, v_ref, do_ref, dq_ref, dk_ref, dv_ref):
        q, k, v, do = q_ref[...], k_ref[...], v_ref[...], do_ref[...]
        p = _softmax_rows(q, k, scale)
        dp = _dot(do, v, 1, 1)
        ds = p * (dp - jnp.sum(dp * p, axis=-1, keepdims=True)) * scale
        dq_ref[...] = _dot(ds, k, 1, 0).astype(dq_ref.dtype)

        @pl.when(pl.program_id(2) == 0)
        def _():
            dk_ref[...] = jnp.zeros_like(dk_ref)
            dv_ref[...] = jnp.zeros_like(dv_ref)

        dk_ref[...] += _dot(ds, q, 0, 0)
        dv_ref[...] += _dot(p, do, 0, 0)

    return pl.pallas_call(
        body, name=name, grid=(n_seq, heads, nq),
        in_specs=[q_spec, k_spec, pl.BlockSpec((mlen, hd), lambda b, h, i: (b, heads + h)), q_spec],
        out_specs=[q_spec, k_spec, k_spec],
        out_shape=[jax.ShapeDtypeStruct((t, d), BF16), jax.ShapeDtypeStruct((n_seq * mlen, d), F32),
                   jax.ShapeDtypeStruct((n_seq * mlen, d), F32)],
        compiler_params=_params(3),
    )(q, kv, kv, do)


ADAMW_BLOCK_ELEMS = 128 * 1024


def _adamw(name, parts, w, m, v):
    n_layers = len(parts)
    _, r, c = parts[0].shape
    assert w.shape == (n_layers * r, c), (name, w.shape, parts[0].shape)
    tr = _pick(r, max(16, ADAMW_BLOCK_ELEMS // c // 16 * 16), 8)
    nt = r // tr
    spec = pl.BlockSpec((tr, c), lambda l, i: (l * nt + i, 0))
    c1 = 1.0 - ADAM_B1 ** ADAM_STEP
    c2 = 1.0 - ADAM_B2 ** ADAM_STEP

    def parts_spec(q):
        return pl.BlockSpec((N_DEV, tr, c), lambda l, i: (0, jnp.where(l == q, i, jnp.where(l > q, nt - 1, 0)), 0))

    def body(*refs):
        p_refs = refs[:n_layers]
        w_ref, m_ref, v_ref, g_ref, d_ref, nm_ref, nv_ref = refs[n_layers:]

        def update(p_ref):
            g = p_ref[0].astype(F32)
            for k in range(1, N_DEV):
                g = g + p_ref[k].astype(F32)
            nm = ADAM_B1 * m_ref[...] + (1.0 - ADAM_B1) * g
            nv = ADAM_B2 * v_ref[...] + (1.0 - ADAM_B2) * (g * g)
            g_ref[...] = g
            nm_ref[...] = nm
            nv_ref[...] = nv
            d_ref[...] = -ADAM_LR * ((nm / c1) / (jnp.sqrt(nv / c2) + ADAM_EPS) + ADAM_WD * w_ref[...])

        for q in range(n_layers):
            pl.when(pl.program_id(0) == q)(lambda q=q: update(p_refs[q]))

    out = jax.ShapeDtypeStruct(w.shape, F32)
    return pl.pallas_call(
        body, name=name, grid=(n_layers, nt), in_specs=[parts_spec(q) for q in range(n_layers)] + [spec] * 3,
        out_specs=[spec] * 4, out_shape=[out] * 4, compiler_params=_params(2),
    )(*parts, w, m, v)


def _place():
    x, y, c = lax.axis_index("x"), lax.axis_index("y"), lax.axis_index("c")
    return x, y, c


def _index(px, py, pc):
    return 4 * px + 2 * py + pc


def _all_gather(name, shards):
    n = len(shards)

    def body(*refs):
        in_refs, out_refs = refs[:n], refs[n:2 * n]
        send_sems, recv_sems, local_sems = refs[2 * n:]
        x, y, c = _place()
        me, sibling = (x, y, c), (x, y, 1 - c)
        chips = [(1 - x, y), (x, 1 - y), (1 - x, 1 - y)]

        def slot(k, block):
            return out_refs[k].at[_index(*block)]

        def copy(k, j, block, to, src=None):
            return pltpu.make_async_remote_copy(
                src_ref=slot(k, block) if src is None else src, dst_ref=slot(k, block),
                send_sem=send_sems.at[7 * k + j], recv_sem=recv_sems.at[7 * k + j], device_id=to, device_id_type=MESH)

        mine = [pltpu.make_async_copy(in_refs[k], slot(k, me), local_sems.at[k]) for k in range(n)]
        for cp in mine:
            cp.start()
        first = []
        for k in range(n):
            first.append(copy(k, 0, me, sibling, src=in_refs[k]))
            first += [copy(k, 1 + j, me, (*chip, c), src=in_refs[k]) for j, chip in enumerate(chips)]
        for cp in first:
            cp.start()
        passed = []
        for j, chip in enumerate(chips):
            for k in range(n):
                copy(k, 1 + j, (*chip, c), me).wait_recv()
                cp = copy(k, 4 + j, (*chip, c), sibling)
                cp.start()
                passed.append(cp)
        for k in range(n):
            copy(k, 0, sibling, me).wait_recv()
            for j, chip in enumerate(chips):
                copy(k, 4 + j, (*chip, 1 - c), me).wait_recv()
        for cp in first + passed:
            cp.wait_send()
        for cp in mine:
            cp.wait()

    return pl.pallas_call(
        body, name=name, in_specs=[HBM_SPEC] * n, out_specs=[HBM_SPEC] * n,
        out_shape=[jax.ShapeDtypeStruct((N_DEV,) + s.shape, s.dtype) for s in shards],
        scratch_shapes=[pltpu.SemaphoreType.DMA((7 * n,)), pltpu.SemaphoreType.DMA((7 * n,)),
                        pltpu.SemaphoreType.DMA((n,))],
    )(*shards)


def _xor_peers(x, y, c):
    peers = []
    for r in range(1, N_DEV):
        rx, ry, rc = (r >> 2) & 1, (r >> 1) & 1, r & 1
        peers.append((1 - x if rx else x, 1 - y if ry else y, 1 - c if rc else c))
    return peers


def _exchange_start(name, srcs, lands, src_by_peer, deps=()):
    n = len(srcs)

    def body(*refs):
        src_refs, land_refs = refs[:n], refs[n:2 * n]
        send_sems, recv_sems = refs[2 * n + len(deps)], refs[2 * n + len(deps) + 1]
        token = refs[-1]
        x, y, c = _place()
        me = _index(x, y, c)
        peers = _xor_peers(x, y, c)
        for k in range(n):
            for j, peer in enumerate(peers):
                src = src_refs[k].at[_index(*peer)] if src_by_peer else src_refs[k]
                pltpu.make_async_remote_copy(
                    src_ref=src, dst_ref=land_refs[k].at[me], send_sem=send_sems.at[7 * k + j],
                    recv_sem=recv_sems.at[7 * k + j], device_id=peer, device_id_type=MESH).start()
        token[...] = jnp.zeros_like(token)

    thru = [pltpu.HBM(a.shape, a.dtype) for a in list(srcs) + list(lands)]
    out = pl.pallas_call(
        body, name=name,
        out_shape=(pltpu.SemaphoreType.DMA((7 * n,)), pltpu.SemaphoreType.DMA((7 * n,)), *thru,
                   jax.ShapeDtypeStruct((8, V7X_LANES), F32)),
        in_specs=[HBM_SPEC] * (2 * n) + [ANY_SPEC] * len(deps),
        out_specs=(SEM_SPEC, SEM_SPEC, *([HBM_SPEC] * (2 * n)), pl.BlockSpec(memory_space=pltpu.VMEM)),
        input_output_aliases={k: 2 + k for k in range(2 * n)},
        compiler_params=pltpu.CompilerParams(has_side_effects=pltpu.SideEffectType.DATAFLOW_SIDE_EFFECTING),
    )(*[pltpu.with_memory_space_constraint(a, pltpu.HBM) for a in list(srcs) + list(lands)], *deps)
    return out[0], out[1], list(out[2:2 + n]), list(out[2 + n:2 + 2 * n]), out[-1]


def _exchange_wait(name, send_sems, recv_sems, srcs, lands, src_by_peer, after):
    n = len(srcs)

    def body(*refs):
        src_refs, land_refs = refs[:n], refs[n:2 * n]
        send_sems, recv_sems = refs[2 * n], refs[2 * n + 1]
        x, y, c = _place()
        peers = _xor_peers(x, y, c)
        for k in range(n):
            for j, peer in enumerate(peers):
                src = src_refs[k].at[_index(*peer)] if src_by_peer else src_refs[k]
                cp = pltpu.make_async_remote_copy(
                    src_ref=src, dst_ref=land_refs[k].at[_index(*peer)], send_sem=send_sems.at[7 * k + j],
                    recv_sem=recv_sems.at[7 * k + j], device_id=peer, device_id_type=MESH)
                cp.wait_send()
                cp.wait_recv()

    thru = [pltpu.HBM(a.shape, a.dtype) for a in list(srcs) + list(lands)]
    out = pl.pallas_call(
        body, name=name, out_shape=tuple(thru),
        in_specs=[HBM_SPEC] * (2 * n) + [SEM_SPEC, SEM_SPEC, ANY_SPEC], out_specs=tuple([HBM_SPEC] * (2 * n)),
        input_output_aliases={k: k for k in range(2 * n)},
        compiler_params=pltpu.CompilerParams(has_side_effects=pltpu.SideEffectType.DATAFLOW_SIDE_EFFECTING),
    )(*srcs, *lands, send_sems, recv_sems, after)
    return list(out[n:])


def _landing(shard, me):
    zone = lax.empty((N_DEV,) + shard.shape, shard.dtype)
    return lax.dynamic_update_slice(zone, shard[None], (me,) + (0,) * shard.ndim)


def _cols_whole(w):
    return jnp.transpose(w, (1, 0, 2)).reshape(w.shape[1], N_DEV * w.shape[2])


def _rows_whole(w):
    return w.reshape(N_DEV * w.shape[1], w.shape[2])


def _cols_parts(dw):
    k, n8 = dw.shape
    return jnp.transpose(dw.reshape(k, N_DEV, n8 // N_DEV), (1, 0, 2))


def _rows_parts(dw):
    r8, c = dw.shape
    return dw.reshape(N_DEV, r8 // N_DEV, c)


def _pack_rows(arrays):
    rows = []
    for a in arrays:
        flat = a.reshape(-1).astype(F32)
        flat = jnp.pad(flat, [(0, (-flat.shape[0]) % PACK_TILE)])
        rows.append(flat.reshape(-1, V7X_LANES))
    return jnp.concatenate(rows, axis=0)


def _unpack_rows(packed, shapes):
    out, row = [], 0
    for s in shapes:
        size = math.prod(s)
        n_rows = -(-size // PACK_TILE) * 8
        out.append(packed[row:row + n_rows].reshape(-1)[:size].reshape(s))
        row += n_rows
    return out


def _merge2d(a):
    return a.reshape(-1, a.shape[-1])


def _ffn_fwd(tag, x, g, w_up, w_down, deps=()):
    f = w_down.shape[0]
    n = _rms_fwd(f"{tag}_norm", x, g, deps)
    gu = _mm(f"{tag}_up", n, w_up)
    act = _swiglu(f"{tag}_swiglu", gu, f)
    out = _mm(f"{tag}_down", act, w_down, res=x, scale=0.5, out_dtype=F32)
    return out, (x, n, gu, act)


def _ffn_bwd(tag, dres, saved, g, w_up, w_down, deps=()):
    x, n, gu, act = saved
    dres32, dres16 = dres
    f = w_down.shape[0]
    dact = _mm(f"{tag}_down_dx", dres16, w_down, tb=True, scale=0.5, deps=deps)
    d_down = _mm(f"{tag}_down_dw", act, dres16, ta=True, scale=0.5)
    dgu = _swiglu_bwd(f"{tag}_swiglu_bwd", dact, gu, f)
    d_up = _mm(f"{tag}_up_dw", n, dgu, ta=True)
    dn = _mm(f"{tag}_up_dx", dgu, w_up, tb=True)
    dx, dg = _rms_bwd(f"{tag}_norm_bwd", x, g, dn, dres32)
    return dx, dg, d_up, d_down


def _conv_mixer_fwd(tag, x, g, w_in, w_conv, w_out, n_seq, seq):
    h = _rms_fwd(f"{tag}_norm", x, g)
    cbv = _mm(f"{tag}_in", h, w_in)
    z = _conv_fwd(f"{tag}_conv", cbv, w_conv, n_seq, seq)
    out = _mm(f"{tag}_out", z, w_out, res=x, out_dtype=F32)
    return out, (x, h, cbv, z)


def _conv_mixer_bwd(tag, dres, saved, g, w_in, w_conv, w_out, n_seq, seq):
    x, h, cbv, z = saved
    dres32, dres16 = dres
    dz = _mm(f"{tag}_out_dx", dres16, w_out, tb=True)
    d_out = _mm(f"{tag}_out_dw", z, dres16, ta=True)
    dc, db, dv, d_conv = _conv_bwd(f"{tag}_conv_bwd", dz, cbv, w_conv, n_seq, seq)
    dcbv = jnp.concatenate([dc, db, dv], axis=1)
    d_in = _mm(f"{tag}_in_dw", h, dcbv, ta=True)
    dh = _mm(f"{tag}_in_dx", dcbv, w_in, tb=True)
    dx, dg = _rms_bwd(f"{tag}_norm_bwd", x, g, dh, dres32)
    return dx, dg, d_in, d_conv, d_out


def _s5_mixer_fwd(tag, x, g, ssm, dskip, w_glu, n_seq, seq):
    a_re, a_im, log_dt, b_re, b_im, c_re, c_im = ssm
    groups, p, hh = b_re.shape
    gb = S5_CHANNELS // hh
    disc, disc_vjp = jax.vjp(_s5_discretize, a_re, a_im, log_dt, b_re, b_im)
    abar_re, abar_im, bbar_re, bbar_im = disc
    mats = (_block_diag_in(bbar_re, gb).astype(BF16), _block_diag_in(bbar_im, gb).astype(BF16),
            _block_diag_out(c_re, gb).astype(BF16), _block_diag_out(c_im, gb).astype(BF16),
            abar_re.reshape(1, groups * p), abar_im.reshape(1, groups * p), dskip)
    d = x.shape[1]
    h = _rms_fwd(f"{tag}_norm", x, g)
    s_re, s_im, ypre, z = _s5_fwd(f"{tag}_scan", h, *mats, n_seq, seq)
    vg = _mm(f"{tag}_glu", z, w_glu)
    out = _glu_res(f"{tag}_glu_act", vg, x, d)
    return out, (x, h, s_re, s_im, ypre, z, vg, mats, disc_vjp, (groups, p, hh, gb))


def _s5_mixer_bwd(tag, dres, saved, g, w_glu, n_seq, seq):
    x, h, s_re, s_im, ypre, z, vg, mats, disc_vjp, (groups, p, hh, gb) = saved
    d = x.shape[1]
    dres32, _ = dres
    dvg = _glu_bwd(f"{tag}_glu_act_bwd", dres32, vg, d)
    d_glu = _mm(f"{tag}_glu_dw", z, dvg, ta=True)
    dz = _mm(f"{tag}_glu_dx", dvg, w_glu, tb=True)
    dh, dbin_re, dbin_im, dcout_re, dcout_im, dabar_re, dabar_im, d_skip = _s5_bwd(
        f"{tag}_scan_bwd", dz, ypre, h, s_re, s_im, *mats, n_seq, seq)
    d_are, d_aim, d_logdt, d_bre, d_bim = disc_vjp((
        dabar_re.reshape(groups, p), dabar_im.reshape(groups, p),
        _block_diag_in_t(dbin_re, gb, p, hh), _block_diag_in_t(dbin_im, gb, p, hh)))
    d_cre = _block_diag_out_t(dcout_re, gb, p, hh)
    d_cim = _block_diag_out_t(dcout_im, gb, p, hh)
    dx, dg = _rms_bwd(f"{tag}_norm_bwd", x, g, dh, dres32)
    return dx, dg, (d_are, d_aim, d_logdt, d_bre, d_bim, d_cre, d_cim), d_skip, d_glu


def _xattn_fwd(tag, x, mem, g_q, g_mem, w_q, w_kv, w_o, n_seq, seq, mlen, heads):
    n = _rms_fwd(f"{tag}_norm", x, g_q)
    q = _mm(f"{tag}_q", n, w_q)
    mem_n = _rms_fwd(f"{tag}_mem_norm", mem, g_mem)
    kv = _mm(f"{tag}_kv", mem_n, w_kv)
    o = _attn_fwd(f"{tag}_attn", q, kv, n_seq, seq, mlen, heads)
    out = _mm(f"{tag}_o", o, w_o, res=x, out_dtype=F32)
    return out, (x, n, q, mem_n, kv, o)


def _xattn_bwd(tag, dres, saved, mem, g_q, g_mem, w_q, w_kv, w_o, n_seq, seq, mlen, heads):
    x, n, q, mem_n, kv, o = saved
    dres32, dres16 = dres
    do = _mm(f"{tag}_o_dx", dres16, w_o, tb=True)
    d_o = _mm(f"{tag}_o_dw", o, dres16, ta=True)
    dq, dk, dv = _attn_bwd(f"{tag}_attn_bwd", q, kv, do, n_seq, seq, mlen, heads)
    dkv = jnp.concatenate([dk, dv], axis=1)
    d_q = _mm(f"{tag}_q_dw", n, dq, ta=True)
    dn = _mm(f"{tag}_q_dx", dq, w_q, tb=True)
    d_kv = _mm(f"{tag}_kv_dw", mem_n, dkv, ta=True)
    dmem_n = _mm(f"{tag}_kv_dx", dkv, w_kv, tb=True)
    _, dg_mem = _rms_bwd(f"{tag}_mem_norm_bwd", mem, g_mem, dmem_n)
    dx, dg_q = _rms_bwd(f"{tag}_norm_bwd", x, g_q, dn, dres32)
    return dx, dg_q, dg_mem, d_q, d_kv, d_o


WEIGHT_NAMES = ("norm_g", "final_g", "ffn1_up", "ffn1_down", "ffn2_up", "ffn2_down", "conv_w_in", "conv_w",
                "conv_w_out", "ssm_a_re", "ssm_a_im", "ssm_log_dt", "ssm_b_re", "ssm_b_im", "ssm_c_re", "ssm_c_im",
                "ssm_d", "ssm_w_glu", "xa_w_q", "xa_w_kv", "xa_w_o")
MATRICES = ("ffn1_up", "ffn1_down", "ffn2_up", "ffn2_down", "conv_w_in", "conv_w_out", "ssm_w_glu", "xa_w_q",
            "xa_w_kv", "xa_w_o")
COLUMN_SHARDED = ("ffn1_up", "ffn2_up", "conv_w_in", "ssm_w_glu", "xa_w_kv")
SMALL_SHARDED = ("norm_g", "conv_w", "ssm_d")
REPLICATED = ("ssm_a_re", "ssm_a_im", "ssm_log_dt", "ssm_b_re", "ssm_b_im", "ssm_c_re", "ssm_c_im", "final_g")


def kernel(x, mem, norm_g, final_g, ffn1_up, ffn1_down, ffn2_up, ffn2_down, conv_w_in, conv_w, conv_w_out, ssm_a_re, ssm_a_im, ssm_log_dt, ssm_b_re, ssm_b_im, ssm_c_re, ssm_c_im, ssm_d, ssm_w_glu, xa_w_q, xa_w_kv, xa_w_o, loss_target, m_norm_g, m_final_g, m_ffn1_up, m_ffn1_down, m_ffn2_up, m_ffn2_down, m_conv_w_in, m_conv_w, m_conv_w_out, m_ssm_a_re, m_ssm_a_im, m_ssm_log_dt, m_ssm_b_re, m_ssm_b_im, m_ssm_c_re, m_ssm_c_im, m_ssm_d, m_ssm_w_glu, m_xa_w_q, m_xa_w_kv, m_xa_w_o, v_norm_g, v_final_g, v_ffn1_up, v_ffn1_down, v_ffn2_up, v_ffn2_down, v_conv_w_in, v_conv_w, v_conv_w_out, v_ssm_a_re, v_ssm_a_im, v_ssm_log_dt, v_ssm_b_re, v_ssm_b_im, v_ssm_c_re, v_ssm_c_im, v_ssm_d, v_ssm_w_glu, v_xa_w_q, v_xa_w_kv, v_xa_w_o):
    w = dict(norm_g=norm_g, final_g=final_g, ffn1_up=ffn1_up, ffn1_down=ffn1_down, ffn2_up=ffn2_up,
             ffn2_down=ffn2_down, conv_w_in=conv_w_in, conv_w=conv_w, conv_w_out=conv_w_out, ssm_a_re=ssm_a_re,
             ssm_a_im=ssm_a_im, ssm_log_dt=ssm_log_dt, ssm_b_re=ssm_b_re, ssm_b_im=ssm_b_im, ssm_c_re=ssm_c_re,
             ssm_c_im=ssm_c_im, ssm_d=ssm_d, ssm_w_glu=ssm_w_glu, xa_w_q=xa_w_q, xa_w_kv=xa_w_kv, xa_w_o=xa_w_o)
    mom = dict(norm_g=m_norm_g, final_g=m_final_g, ffn1_up=m_ffn1_up, ffn1_down=m_ffn1_down, ffn2_up=m_ffn2_up,
               ffn2_down=m_ffn2_down, conv_w_in=m_conv_w_in, conv_w=m_conv_w, conv_w_out=m_conv_w_out,
               ssm_a_re=m_ssm_a_re, ssm_a_im=m_ssm_a_im, ssm_log_dt=m_ssm_log_dt, ssm_b_re=m_ssm_b_re,
               ssm_b_im=m_ssm_b_im, ssm_c_re=m_ssm_c_re, ssm_c_im=m_ssm_c_im, ssm_d=m_ssm_d, ssm_w_glu=m_ssm_w_glu,
               xa_w_q=m_xa_w_q, xa_w_kv=m_xa_w_kv, xa_w_o=m_xa_w_o)
    var = dict(norm_g=v_norm_g, final_g=v_final_g, ffn1_up=v_ffn1_up, ffn1_down=v_ffn1_down, ffn2_up=v_ffn2_up,
               ffn2_down=v_ffn2_down, conv_w_in=v_conv_w_in, conv_w=v_conv_w, conv_w_out=v_conv_w_out,
               ssm_a_re=v_ssm_a_re, ssm_a_im=v_ssm_a_im, ssm_log_dt=v_ssm_log_dt, ssm_b_re=v_ssm_b_re,
               ssm_b_im=v_ssm_b_im, ssm_c_re=v_ssm_c_re, ssm_c_im=v_ssm_c_im, ssm_d=v_ssm_d, ssm_w_glu=v_ssm_w_glu,
               xa_w_q=v_xa_w_q, xa_w_kv=v_xa_w_kv, xa_w_o=v_xa_w_o)

    n_seq, seq, d = x.shape
    mlen = mem.shape[1]
    depth, n_norms = norm_g.shape[0], norm_g.shape[1]
    heads = 4
    tokens = n_seq * seq
    x2 = x.reshape(tokens, d)
    mem2 = mem.reshape(n_seq * mlen, d)
    tgt2 = loss_target.reshape(tokens, d)

    small_shapes = [w[k].shape for k in SMALL_SHARDED]
    small_rows = [_merge2d(w[k]) for k in SMALL_SHARDED]
    small_counts = [s.shape[0] for s in small_rows]
    small = jnp.concatenate(small_rows, axis=0)
    small = jnp.pad(small, [(0, (-small.shape[0]) % 8), (0, 0)])
    me = _index(*_place())

    def layer_weights(i):
        names = [(k, i) for k in ("ffn1_up", "ffn1_down", "ffn2_up", "ffn2_down", "xa_w_q", "xa_w_kv", "xa_w_o")]
        return names + ([("conv_w_in", i // 2), ("conv_w_out", i // 2)] if i % 2 == 0 else [("ssm_w_glu", i // 2)])

    shards = [[w[k][idx].astype(BF16) for k, idx in layer_weights(i)] for i in range(depth)]
    gathered = _all_gather("gather_layer0", shards[0] + [small])
    small_all = gathered[-1]
    blocks = [gathered[:-1]] + [None] * (depth - 1)
    in_flight = [None] * depth
    token = gathered[0]
    for i in range(1, depth):
        zones = [_landing(s, me) for s in shards[i]]
        *in_flight[i], token = _exchange_start(f"gather_start_l{i}", shards[i], zones, False, deps=[token])

    def small_whole(idx):
        start = sum(small_counts[:idx])
        part = small_all[:, start:start + small_counts[idx]]
        lead = small_shapes[idx][:-1]
        part = part.reshape((N_DEV,) + lead + (part.shape[-1],))
        part = jnp.moveaxis(part, 0, -2)
        return part.reshape(lead + (N_DEV * part.shape[-1],))

    norm_all = small_whole(0)
    conv_all = small_whole(1)
    dskip_all = small_whole(2)

    def whole(i):
        return {k: _cols_whole(blk) if k in COLUMN_SHARDED else _rows_whole(blk)
                for (k, _), blk in zip(layer_weights(i), blocks[i])}

    saved = []
    cur = x2
    for i in range(depth):
        g = [norm_all[i, k].reshape(1, d) for k in range(n_norms)]
        j = i // 2
        if i > 0:
            blocks[i] = _exchange_wait(f"gather_wait_l{i}", *in_flight[i], False, cur)
        lw = whole(i)
        cur, s_ffn1 = _ffn_fwd(f"l{i}_ffn1", cur, g[0], lw["ffn1_up"], lw["ffn1_down"], [token] if i == 0 else ())
        if i % 2 == 0:
            lw["conv_w"] = conv_all[j]
            cur, s_mix = _conv_mixer_fwd(f"l{i}_conv", cur, g[1], lw["conv_w_in"], lw["conv_w"], lw["conv_w_out"],
                                         n_seq, seq)
        else:
            ssm = tuple(w[k][j] for k in ("ssm_a_re", "ssm_a_im", "ssm_log_dt", "ssm_b_re", "ssm_b_im",
                                          "ssm_c_re", "ssm_c_im"))
            cur, s_mix = _s5_mixer_fwd(f"l{i}_s5", cur, g[1], ssm, dskip_all[j].reshape(1, d), lw["ssm_w_glu"],
                                       n_seq, seq)
        cur, s_xa = _xattn_fwd(f"l{i}_xa", cur, mem2, g[2], g[3], lw["xa_w_q"], lw["xa_w_kv"], lw["xa_w_o"],
                               n_seq, seq, mlen, heads)
        cur, s_ffn2 = _ffn_fwd(f"l{i}_ffn2", cur, g[4], lw["ffn2_up"], lw["ffn2_down"])
        saved.append((g, lw, s_ffn1, s_mix, s_xa, s_ffn2))

    dres, err2, d_final = _final_loss("loss_head", cur, final_g.reshape(1, d), tgt2)
    loss = lax.psum(0.5 * jnp.sum(err2) / d, ("x", "y", "c"))

    d_norm = [[None] * n_norms for _ in range(depth)]
    d_conv = [None] * conv_w.shape[0]
    d_skip = [None] * ssm_d.shape[0]
    d_ssm = [None] * ssm_a_re.shape[0]
    leaving = [None] * depth
    deps = ()

    def leave(name, keys, gm, extra=()):
        parts = [(_cols_parts if k in COLUMN_SHARDED else _rows_parts)(gm[k]) for k in keys] + list(extra)
        zones = [_landing(lax.dynamic_index_in_dim(p, me, 0, keepdims=False), me) for p in parts]
        return _exchange_start(name, parts, zones, True)

    def small_parts(full):
        lead = full.shape[:-1]
        t = full.reshape(lead + (N_DEV, full.shape[-1] // N_DEV))
        t = jnp.moveaxis(t, -2, 0)
        return t.reshape(N_DEV, -1, t.shape[-1])

    for i in reversed(range(depth)):
        g, lw, s_ffn1, s_mix, s_xa, s_ffn2 = saved[i]
        j = i // 2
        gm = {}
        dres, d_norm[i][4], gm["ffn2_up"], gm["ffn2_down"] = _ffn_bwd(
            f"l{i}_ffn2", dres, s_ffn2, g[4], lw["ffn2_up"], lw["ffn2_down"], deps)
        dres, d_norm[i][2], d_norm[i][3], gm["xa_w_q"], gm["xa_w_kv"], gm["xa_w_o"] = _xattn_bwd(
            f"l{i}_xa", dres, s_xa, mem2, g[2], g[3], lw["xa_w_q"], lw["xa_w_kv"], lw["xa_w_o"], n_seq, seq, mlen,
            heads)
        if i % 2 == 0:
            dres, d_norm[i][1], gm["conv_w_in"], d_conv[j], gm["conv_w_out"] = _conv_mixer_bwd(
                f"l{i}_conv", dres, s_mix, g[1], lw["conv_w_in"], lw["conv_w"], lw["conv_w_out"], n_seq, seq)
        else:
            dres, d_norm[i][1], d_ssm[j], d_skip[j], gm["ssm_w_glu"] = _s5_mixer_bwd(
                f"l{i}_s5", dres, s_mix, g[1], lw["ssm_w_glu"], n_seq, seq)
        leaving[i] = [leave(f"grads_start_l{i}_upper", [k for k, _ in layer_weights(i)[2:]], gm)]
        dres, d_norm[i][0], gm["ffn1_up"], gm["ffn1_down"] = _ffn_bwd(
            f"l{i}_ffn1", dres, s_ffn1, g[0], lw["ffn1_up"], lw["ffn1_down"], [leaving[i][0][-1]])
        extra = []
        if i == 0:
            d_norm_all = jnp.stack([jnp.concatenate(row, axis=0) for row in d_norm])
            small_g = jnp.concatenate(
                [small_parts(a) for a in (d_norm_all, jnp.stack(d_conv), jnp.concatenate(d_skip, axis=0))], axis=1)
            extra = [jnp.pad(small_g, [(0, 0), (0, (-small_g.shape[1]) % 8), (0, 0)])]
        leaving[i].append(leave(f"grads_start_l{i}_lower", ["ffn1_up", "ffn1_down"], gm, extra))
        deps = [leaving[i][1][-1]]
        if i == min(1, depth - 1):
            rep_grads = [jnp.stack([d_ssm[j][k] for j in range(len(d_ssm))]) for k in range(7)]
            rep_packed = _pack_rows(rep_grads + [d_final.reshape(-1)])
            rep_leaving = _exchange_start("replicated_grads_start", [rep_packed], [_landing(rep_packed, me)], False)
            deps = deps + [rep_leaving[-1]]
    grad_x = dres[0].reshape(n_seq, seq, d)
    received = {k: [None] * w[k].shape[0] for k in MATRICES}
    for i in range(depth):
        upper = _exchange_wait(f"grads_wait_l{i}_upper", *leaving[i][0][:4], True, dres[0])
        lower = _exchange_wait(f"grads_wait_l{i}_lower", *leaving[i][1][:4], True, dres[0])
        for (k, idx), blk in zip(layer_weights(i), lower[:2] + upper):
            received[k][idx] = blk
        if i == 0:
            small_received = lower[2]
    rep_all = _exchange_wait("replicated_grads_wait", *rep_leaving[:4], False, dres[0])[0]
    rep_shapes = [w[k].shape for k in REPLICATED]

    grads, deltas, new_m, new_v = {}, {}, {}, {}
    for k in MATRICES:
        shape = w[k].shape
        out = _adamw(f"adamw_{k}", received[k], _merge2d(w[k]), _merge2d(mom[k]), _merge2d(var[k]))
        grads[k], deltas[k], new_m[k], new_v[k] = [o.reshape(shape) for o in out]

    def small_local(src):
        rows = jnp.concatenate([_merge2d(src[k]) for k in SMALL_SHARDED], axis=0)
        return jnp.pad(rows, [(0, (-rows.shape[0]) % 8), (0, 0)])

    out = _adamw("adamw_small", [small_received], small, small_local(mom), small_local(var))
    for res, o in zip((grads, deltas, new_m, new_v), out):
        start = 0
        for k, cnt, shape in zip(SMALL_SHARDED, small_counts, small_shapes):
            res[k] = o[start:start + cnt].reshape(shape)
            start += cnt

    out = _adamw("adamw_replicated", [rep_all], _pack_rows([w[k] for k in REPLICATED]),
                 _pack_rows([mom[k] for k in REPLICATED]), _pack_rows([var[k] for k in REPLICATED]))
    for res, o in zip((grads, deltas, new_m, new_v), out):
        for k, a in zip(REPLICATED, _unpack_rows(o, rep_shapes)):
            res[k] = a

    return (loss, grad_x, *[grads[k] for k in WEIGHT_NAMES], *[deltas[k] for k in WEIGHT_NAMES],
            *[new_m[k] for k in WEIGHT_NAMES], *[new_v[k] for k in WEIGHT_NAMES])
```

```python
import math

import jax
import jax.numpy as jnp
from jax import lax
from jax.experimental import pallas as pl
from jax.experimental.pallas import tpu as pltpu

F32 = jnp.float32
BF16 = jnp.bfloat16
MESH = pl.DeviceIdType.MESH
N_DEV = 8

NORM_EPS = 1e-6
EIG_CLIP = -1e-4
CONV_WIDTH = 3
ADAM_LR = 0.001
ADAM_B1 = 0.9
ADAM_B2 = 0.999
ADAM_EPS = 1e-08
ADAM_WD = 0.01
ADAM_STEP = 10
GELU_C = math.sqrt(2.0 / math.pi)
GELU_A = 0.044715

V7X_LANES = 128
V7X_VMEM_LIMIT = 56 * 1024 * 1024
S5_CHANNELS = 128
PACK_TILE = 8 * V7X_LANES

HBM_SPEC = pl.BlockSpec(memory_space=pltpu.HBM)
ANY_SPEC = pl.BlockSpec(memory_space=pl.ANY)
SEM_SPEC = pl.BlockSpec(memory_space=pltpu.SEMAPHORE)


def _params(n_grid):
    return pltpu.CompilerParams(dimension_semantics=("arbitrary",) * n_grid, vmem_limit_bytes=V7X_VMEM_LIMIT)


def _pick(n, pref, align):
    if n <= pref:
        return n
    t = (pref // align) * align
    while t >= align:
        if n % t == 0:
            return t
        t -= align
    raise ValueError(f"no tile for {n} (pref {pref}, align {align})")


MM_RHS_BLOCK_BYTES = 12 * 1024 * 1024
MM_LHS_BLOCK_BYTES = 6 * 1024 * 1024
MM_ACC_BYTES = 6 * 1024 * 1024
MM_ROWS = 512


MM_EPILOGUE_ROWS = 256


def _mm_tiles(m, k, n, a_item, b_item, ta, max_rows):
    tn = _pick(n, max(V7X_LANES, MM_RHS_BLOCK_BYTES // (k * b_item)), V7X_LANES)
    rows = min(max_rows, MM_ACC_BYTES // (4 * tn), MM_LHS_BLOCK_BYTES // (k * a_item))
    align = V7X_LANES if ta else 16
    tm = _pick(m, max(align, rows), align)
    return tm, tn


def _store_results(out_refs, n_row, results, first):
    if not isinstance(results, (tuple, list)):
        results = (results,)
    for o, v in zip(out_refs[:n_row], results[:n_row]):
        if isinstance(v, (tuple, list)):
            off = 0
            for piece in v:
                w = piece.shape[1]
                o[:, off:off + w] = piece.astype(o.dtype)
                off += w
        else:
            o[...] = v.astype(o.dtype)
    if len(out_refs) > n_row:
        @pl.when(first)
        def _():
            for o in out_refs[n_row:]:
                o[...] = jnp.zeros_like(o)

        for o, v in zip(out_refs[n_row:], results[n_row:]):
            o[...] += v


def _mm(name, a, b, *, ta=False, tb=False, out_dtype=BF16, res=None, scale=None, deps=(),
        epilogue=None, row_ins=(), par_ins=(), outs=(), acc_outs=()):
    if ta:
        k, m = a.shape
    else:
        m, k = a.shape
    if tb:
        n, k2 = b.shape
    else:
        k2, n = b.shape
    assert k == k2, (name, a.shape, b.shape)
    max_rows = MM_ROWS if epilogue is None else MM_EPILOGUE_ROWS
    tm, tn = _mm_tiles(m, k, n, a.dtype.itemsize, b.dtype.itemsize, ta, max_rows)
    a_spec = pl.BlockSpec((k, tm), lambda j, i: (0, i)) if ta else pl.BlockSpec((tm, k), lambda j, i: (i, 0))
    b_spec = pl.BlockSpec((tn, k), lambda j, i: (j, 0)) if tb else pl.BlockSpec((k, tn), lambda j, i: (0, j))
    o_spec = pl.BlockSpec((tm, tn), lambda j, i: (i, j))
    dims = (((0 if ta else 1,), (1 if tb else 0,)), ((), ()))
    has_res = res is not None
    ins = [a, b] + ([res] if has_res else [])
    specs = [a_spec, b_spec] + ([o_spec] if has_res else [])
    n_mm = len(ins)
    if epilogue is None:
        out_specs, out_shape = [o_spec], [jax.ShapeDtypeStruct((m, n), out_dtype)]
    else:
        assert tn == n, (name, tn, n)
        for r in row_ins:
            arr, cb, cw = r if isinstance(r, tuple) else (r, 0, r.shape[1])
            assert arr.shape[0] == m, (name, arr.shape, m)
            ins.append(arr)
            specs.append(pl.BlockSpec((tm, cw), lambda j, i, cb=cb: (i, cb)))
        for p in par_ins:
            ins.append(p)
            specs.append(pl.BlockSpec(p.shape, lambda j, i: (0, 0)))
        out_specs = [pl.BlockSpec((tm, c), lambda j, i: (i, 0)) for c, _ in outs]
        out_specs += [pl.BlockSpec((r, c), lambda j, i: (0, 0)) for r, c in acc_outs]
        out_shape = [jax.ShapeDtypeStruct((m, c), dt) for c, dt in outs]
        out_shape += [jax.ShapeDtypeStruct((r, c), F32) for r, c in acc_outs]
    n_in = len(ins)
    ins += list(deps)
    specs += [ANY_SPEC] * len(deps)

    def body(*refs):
        a_ref, b_ref = refs[0], refs[1]
        out_refs = refs[n_in + len(deps):]
        acc = lax.dot_general(a_ref[...].astype(BF16), b_ref[...].astype(BF16), dims, preferred_element_type=F32)
        if scale is not None:
            acc = acc * scale
        if has_res:
            acc = acc + refs[2][...].astype(F32)
        if epilogue is None:
            out_refs[0][...] = acc.astype(out_refs[0].dtype)
        else:
            extra = [r[...] for r in refs[n_mm:n_in]]
            _store_results(out_refs, len(outs), epilogue(acc, *extra), pl.program_id(1) == 0)

    out = pl.pallas_call(
        body, name=name, grid=(n // tn, m // tm), in_specs=specs, out_specs=out_specs, out_shape=out_shape,
        compiler_params=_params(2),
    )(*ins)
    return out[0] if epilogue is None else out


def _rowwise(name, fn, rows, row_ins, par_ins, row_outs, acc_outs=(), tm_pref=256, deps=()):
    tm = _pick(rows, tm_pref, 16)
    in_specs, ins = [], []
    for r in row_ins:
        arr, cb, cw = r if isinstance(r, tuple) else (r, 0, r.shape[1])
        assert arr.shape[0] == rows, (name, arr.shape, rows)
        ins.append(arr)
        in_specs.append(pl.BlockSpec((tm, cw), lambda i, cb=cb: (i, cb)))
    for p in par_ins:
        ins.append(p)
        in_specs.append(pl.BlockSpec(p.shape, lambda i: (0, 0)))
    out_specs = [pl.BlockSpec((tm, c), lambda i: (i, 0)) for c, _ in row_outs]
    out_specs += [pl.BlockSpec((r, c), lambda i: (0, 0)) for r, c in acc_outs]
    out_shape = [jax.ShapeDtypeStruct((rows, c), dt) for c, dt in row_outs]
    out_shape += [jax.ShapeDtypeStruct((r, c), F32) for r, c in acc_outs]
    n_in, n_row = len(ins), len(row_outs)
    ins += list(deps)
    in_specs += [ANY_SPEC] * len(deps)

    def body(*refs):
        vals = [r[...] for r in refs[:n_in]]
        _store_results(refs[n_in + len(deps):], n_row, fn(*vals), pl.program_id(0) == 0)

    out = pl.pallas_call(
        body, name=name, grid=(rows // tm,), in_specs=in_specs, out_specs=out_specs, out_shape=out_shape,
        compiler_params=_params(1),
    )(*ins)
    return out


def _inv_rms(x):
    return lax.rsqrt(jnp.mean(x * x, axis=-1, keepdims=True) + NORM_EPS)


def _rms_fwd(name, x, g, deps=()):
    def fn(x, g):
        return x * _inv_rms(x) * g

    return _rowwise(name, fn, x.shape[0], [x], [g], [(x.shape[1], BF16)], tm_pref=512, deps=deps)[0]


def _rms_bwd_rows(dn, x, dres, g):
    r = _inv_rms(x)
    xh = x * r
    dg = jnp.sum(dn * xh, axis=0, keepdims=True)
    dxh = dn * g
    dx = r * (dxh - xh * jnp.mean(dxh * xh, axis=-1, keepdims=True)) + dres
    return dx, dx, dg


def _rms_bwd(name, x, g, dn, dres=None):
    d = x.shape[1]
    if dres is None:
        def fn(x, dn, g):
            return (jnp.sum(dn.astype(F32) * (x * _inv_rms(x)), axis=0, keepdims=True),)

        return None, _rowwise(name, fn, x.shape[0], [x, dn], [g], [], [(1, d)])[0]

    def fn(x, dn, dres, g):
        return _rms_bwd_rows(dn.astype(F32), x, dres, g)

    out = _rowwise(name, fn, x.shape[0], [x, dn, dres], [g], [(d, F32), (d, BF16)], [(1, d)])
    return (out[0], out[1]), out[2]


def _mm_rms_bwd(name, dy, w, x, g, dres):
    d = x.shape[1]
    out = _mm(name, dy, w, tb=True, epilogue=_rms_bwd_rows, row_ins=[x, dres], par_ins=[g],
              outs=[(d, F32), (d, BF16)], acc_outs=[(1, d)])
    return (out[0], out[1]), out[2]


def _sigmoid(x):
    return 1.0 / (1.0 + jnp.exp(-x))


def _swiglu_rows(gu, f):
    gt = gu[:, :f]
    return gu, gt * _sigmoid(gt) * gu[:, f:]


def _swiglu_bwd_rows(dact, gt, up):
    gt, up = gt.astype(F32), up.astype(F32)
    sg = _sigmoid(gt)
    return ((dact * up * (sg * (1.0 + gt * (1.0 - sg))), dact * (gt * sg)),)


def _glu_res(name, vg, x, d):
    def fn(val, gate, x):
        return x + val.astype(F32) * _sigmoid(gate.astype(F32))

    return _rowwise(name, fn, x.shape[0], [(vg, 0, d), (vg, 1, d), x], [], [(d, F32)])[0]


def _glu_bwd(name, dres, vg, d):
    def fn(dres, val, gate):
        val, gate = val.astype(F32), gate.astype(F32)
        sg = _sigmoid(gate)
        return ((dres * sg, dres * val * sg * (1.0 - sg)),)

    return _rowwise(name, fn, dres.shape[0], [dres, (vg, 0, d), (vg, 1, d)], [], [(2 * d, BF16)])[0]


def _final_loss(name, x, g, tgt):
    d = x.shape[1]

    def fn(x, tgt, g):
        r = _inv_rms(x)
        xh = x * r
        err = xh * g - tgt
        dy = err * (1.0 / d)
        dxh = dy * g
        dx = r * (dxh - xh * jnp.mean(dxh * xh, axis=-1, keepdims=True))
        return dx, dx, jnp.sum(err * err, axis=0, keepdims=True), jnp.sum(dy * xh, axis=0, keepdims=True)

    dx, dx16, err2, dg = _rowwise(name, fn, x.shape[0], [x, tgt], [g], [(d, F32), (d, BF16)], [(1, d), (1, d)])
    return (dx, dx16), err2, dg


def _shift_down(u, k):
    rows = lax.broadcasted_iota(jnp.int32, u.shape, 0)
    return jnp.where(rows >= k, pltpu.roll(u, k, 0), 0.0)


def _shift_up(u, k):
    n = u.shape[0]
    rows = lax.broadcasted_iota(jnp.int32, u.shape, 0)
    return jnp.where(rows < n - k, pltpu.roll(u, n - k, 0), 0.0)


def _conv_specs(seq, cw, n_cb, swap):
    def at(off):
        if swap:
            return pl.BlockSpec((seq, cw), lambda j, b: (b, off * n_cb + j))
        return pl.BlockSpec((seq, cw), lambda b, j: (b, off * n_cb + j))

    return at


def _conv_fwd(name, cbv, w, n_seq, seq):
    d = w.shape[1]
    cw = _pick(d, 256, V7X_LANES)
    n_cb = d // cw
    at = _conv_specs(seq, cw, n_cb, swap=False)

    def body(c_ref, b_ref, v_ref, w_ref, z_ref):
        u = c_ref[...].astype(F32) * v_ref[...].astype(F32)
        cv = w_ref[0:1, :] * _shift_down(u, 2) + w_ref[1:2, :] * _shift_down(u, 1) + w_ref[2:3, :] * u
        z_ref[...] = (b_ref[...].astype(F32) * cv).astype(z_ref.dtype)

    return pl.pallas_call(
        body, name=name, grid=(n_seq, n_cb),
        in_specs=[at(0), at(1), at(2), pl.BlockSpec((CONV_WIDTH, cw), lambda b, j: (0, j))],
        out_specs=at(0), out_shape=jax.ShapeDtypeStruct((n_seq * seq, d), BF16), compiler_params=_params(2),
    )(cbv, cbv, cbv, w)


def _conv_bwd(name, dz, cbv, w, n_seq, seq):
    d = w.shape[1]
    cw = _pick(d, 256, V7X_LANES)
    n_cb = d // cw
    at = _conv_specs(seq, cw, n_cb, swap=True)

    def body(dz_ref, c_ref, b_ref, v_ref, w_ref, dc_ref, db_ref, dv_ref, dw_ref):
        c, b, v = c_ref[...].astype(F32), b_ref[...].astype(F32), v_ref[...].astype(F32)
        dz = dz_ref[...].astype(F32)
        w0, w1, w2 = w_ref[0:1, :], w_ref[1:2, :], w_ref[2:3, :]
        u = c * v
        u1, u2 = _shift_down(u, 1), _shift_down(u, 2)
        cv = w0 * u2 + w1 * u1 + w2 * u
        db_ref[...] = (dz * cv).astype(db_ref.dtype)
        dcv = dz * b
        du = w2 * dcv + w1 * _shift_up(dcv, 1) + w0 * _shift_up(dcv, 2)
        dc_ref[...] = (du * v).astype(dc_ref.dtype)
        dv_ref[...] = (du * c).astype(dv_ref.dtype)

        @pl.when(pl.program_id(1) == 0)
        def _():
            dw_ref[...] = jnp.zeros_like(dw_ref)

        dw_ref[0:1, :] += jnp.sum(dcv * u2, axis=0, keepdims=True)
        dw_ref[1:2, :] += jnp.sum(dcv * u1, axis=0, keepdims=True)
        dw_ref[2:3, :] += jnp.sum(dcv * u, axis=0, keepdims=True)

    act = jax.ShapeDtypeStruct((n_seq * seq, d), BF16)
    return pl.pallas_call(
        body, name=name, grid=(n_cb, n_seq),
        in_specs=[at(0), at(0), at(1), at(2), pl.BlockSpec((CONV_WIDTH, cw), lambda j, b: (0, j))],
        out_specs=[at(0), at(0), at(0), pl.BlockSpec((CONV_WIDTH, cw), lambda j, b: (0, j))],
        out_shape=[act, act, act, jax.ShapeDtypeStruct((CONV_WIDTH, d), F32)], compiler_params=_params(2),
    )(dz, cbv, cbv, cbv, w)


def _s5_discretize(a_re, a_im, log_dt, b_re, b_im):
    lam_re = jnp.minimum(a_re, EIG_CLIP)
    lam_im = a_im
    dt = jnp.exp(log_dt)[:, None]
    mag = jnp.exp(lam_re * dt)
    abar_re = mag * jnp.cos(lam_im * dt)
    abar_im = mag * jnp.sin(lam_im * dt)
    den = lam_re * lam_re + lam_im * lam_im
    num_re = abar_re - 1.0
    num_im = abar_im
    coef_re = (num_re * lam_re + num_im * lam_im) / den
    coef_im = (num_im * lam_re - num_re * lam_im) / den
    bbar_re = coef_re[..., None] * b_re - coef_im[..., None] * b_im
    bbar_im = coef_re[..., None] * b_im + coef_im[..., None] * b_re
    return abar_re, abar_im, bbar_re, bbar_im


def _block_diag_in(bbar, gb):
    g, p, h = bbar.shape
    t = jnp.transpose(bbar.reshape(g // gb, gb, p, h), (0, 1, 3, 2))
    return jnp.einsum("cghp,gk->cghkp", t, jnp.eye(gb, dtype=bbar.dtype)).reshape(g // gb, gb * h, gb * p)


def _block_diag_in_t(blk, gb, p, h):
    nb = blk.shape[0]
    t = jnp.einsum("cghkp,gk->cghp", blk.reshape(nb, gb, h, gb, p), jnp.eye(gb, dtype=blk.dtype))
    return jnp.transpose(t, (0, 1, 3, 2)).reshape(nb * gb, p, h)


def _block_diag_out(c, gb):
    g, h, p = c.shape
    t = jnp.transpose(c.reshape(g // gb, gb, h, p), (0, 1, 3, 2))
    return jnp.einsum("cgph,gk->cgpkh", t, jnp.eye(gb, dtype=c.dtype)).reshape(g // gb, gb * p, gb * h)


def _block_diag_out_t(blk, gb, p, h):
    nb = blk.shape[0]
    t = jnp.einsum("cgpkh,gk->cgph", blk.reshape(nb, gb, p, gb, h), jnp.eye(gb, dtype=blk.dtype))
    return jnp.transpose(t, (0, 1, 3, 2)).reshape(nb * gb, h, p)


def _gelu(y):
    return 0.5 * y * (1.0 + jnp.tanh(GELU_C * (y + GELU_A * y * y * y)))


def _gelu_grad(y):
    th = jnp.tanh(GELU_C * (y + GELU_A * y * y * y))
    return 0.5 * (1.0 + th) + 0.5 * y * (1.0 - th * th) * GELU_C * (1.0 + 3.0 * GELU_A * y * y)


def _dot(a, b, ca, cb):
    return lax.dot_general(a.astype(BF16), b.astype(BF16), (((ca,), (cb,)), ((), ())), preferred_element_type=F32)


def _s5_specs(seq, ch, sb):
    act = pl.BlockSpec((seq, ch), lambda j, b: (b, j))
    state = pl.BlockSpec((seq, sb), lambda j, b: (b, j))
    w_in = pl.BlockSpec((None, ch, sb), lambda j, b: (j, 0, 0))
    w_out = pl.BlockSpec((None, sb, ch), lambda j, b: (j, 0, 0))
    lane_s = pl.BlockSpec((1, sb), lambda j, b: (0, j))
    lane_c = pl.BlockSpec((1, ch), lambda j, b: (0, j))
    return act, state, w_in, w_out, lane_s, lane_c


def _s5_fwd(name, h, bin_re, bin_im, cout_re, cout_im, abar_re, abar_im, dskip, n_seq, seq):
    t, d = h.shape
    nb, ch, sb = bin_re.shape
    act, state, w_in, w_out, lane_s, lane_c = _s5_specs(seq, ch, sb)

    def body(h_ref, bre_ref, bim_ref, cre_ref, cim_ref, ar_ref, ai_ref, d_ref, sre_ref, sim_ref, y_ref, z_ref):
        u = h_ref[...]
        sre_ref[...] = _dot(u, bre_ref[...], 1, 0)
        sim_ref[...] = _dot(u, bim_ref[...], 1, 0)
        ar, ai = ar_ref[...], ai_ref[...]

        def step(i, carry):
            sr, si = carry
            row = pl.ds(i, 1)
            nr = ar * sr - ai * si + sre_ref[row, :]
            ni = ar * si + ai * sr + sim_ref[row, :]
            sre_ref[row, :] = nr
            sim_ref[row, :] = ni
            return nr, ni

        zero = jnp.zeros((1, sb), F32)
        lax.fori_loop(0, seq, step, (zero, zero), unroll=8)
        y = _dot(sre_ref[...], cre_ref[...], 1, 0) - _dot(sim_ref[...], cim_ref[...], 1, 0)
        y = y + d_ref[...] * u.astype(F32)
        y_ref[...] = y
        z_ref[...] = _gelu(y).astype(z_ref.dtype)

    return pl.pallas_call(
        body, name=name, grid=(nb, n_seq),
        in_specs=[act, w_in, w_in, w_out, w_out, lane_s, lane_s, lane_c],
        out_specs=[state, state, act, act],
        out_shape=[jax.ShapeDtypeStruct((t, nb * sb), F32), jax.ShapeDtypeStruct((t, nb * sb), F32),
                   jax.ShapeDtypeStruct((t, d), F32), jax.ShapeDtypeStruct((t, d), BF16)],
        compiler_params=_params(2),
    )(h, bin_re, bin_im, cout_re, cout_im, abar_re, abar_im, dskip)


def _s5_bwd(name, dz, ypre, h, s_re, s_im, bin_re, bin_im, cout_re, cout_im, abar_re, abar_im, dskip, n_seq, seq):
    t, d = h.shape
    nb, ch, sb = bin_re.shape
    act, state, w_in, w_out, lane_s, lane_c = _s5_specs(seq, ch, sb)

    def body(dz_ref, y_ref, h_ref, sre_ref, sim_ref, bre_ref, bim_ref, cre_ref, cim_ref, ar_ref, ai_ref, d_ref,
             dh_ref, dbre_ref, dbim_ref, dcre_ref, dcim_ref, dar_ref, dai_ref, dd_ref, gre, gim):
        first = pl.program_id(1) == 0
        u = h_ref[...].astype(F32)
        dy = dz_ref[...].astype(F32) * _gelu_grad(y_ref[...])
        gre[...] = _dot(dy, cre_ref[...], 1, 1)
        gim[...] = -_dot(dy, cim_ref[...], 1, 1)
        ar, ai = ar_ref[...], ai_ref[...]

        def step(i, carry):
            gr, gi = carry
            row = pl.ds(seq - 1 - i, 1)
            nr = gre[row, :] + ar * gr + ai * gi
            ni = gim[row, :] - ai * gr + ar * gi
            gre[row, :] = nr
            gim[row, :] = ni
            return nr, ni

        zero = jnp.zeros((1, sb), F32)
        lax.fori_loop(0, seq, step, (zero, zero), unroll=8)

        g_re, g_im = gre[...], gim[...]
        s_re, s_im = sre_ref[...], sim_ref[...]
        p_re, p_im = _shift_down(s_re, 1), _shift_down(s_im, 1)
        dar = jnp.sum(g_re * p_re + g_im * p_im, axis=0, keepdims=True)
        dai = jnp.sum(g_im * p_re - g_re * p_im, axis=0, keepdims=True)
        dbre = _dot(u, g_re, 0, 0)
        dbim = _dot(u, g_im, 0, 0)
        dcre = _dot(s_re, dy, 0, 0)
        dcim = -_dot(s_im, dy, 0, 0)
        ddd = jnp.sum(dy * u, axis=0, keepdims=True)
        dh_ref[...] = _dot(g_re, bre_ref[...], 1, 1) + _dot(g_im, bim_ref[...], 1, 1) + d_ref[...] * dy

        @pl.when(first)
        def _():
            dar_ref[...] = dar
            dai_ref[...] = dai
            dbre_ref[...] = dbre
            dbim_ref[...] = dbim
            dcre_ref[...] = dcre
            dcim_ref[...] = dcim
            dd_ref[...] = ddd

        @pl.when(jnp.logical_not(first))
        def _():
            dar_ref[...] += dar
            dai_ref[...] += dai
            dbre_ref[...] += dbre
            dbim_ref[...] += dbim
            dcre_ref[...] += dcre
            dcim_ref[...] += dcim
            dd_ref[...] += ddd

    return pl.pallas_call(
        body, name=name, grid=(nb, n_seq),
        in_specs=[act, act, act, state, state, w_in, w_in, w_out, w_out, lane_s, lane_s, lane_c],
        out_specs=[act, w_in, w_in, w_out, w_out, lane_s, lane_s, lane_c],
        out_shape=[jax.ShapeDtypeStruct((t, d), F32),
                   jax.ShapeDtypeStruct((nb, ch, sb), F32), jax.ShapeDtypeStruct((nb, ch, sb), F32),
                   jax.ShapeDtypeStruct((nb, sb, ch), F32), jax.ShapeDtypeStruct((nb, sb, ch), F32),
                   jax.ShapeDtypeStruct((1, nb * sb), F32), jax.ShapeDtypeStruct((1, nb * sb), F32),
                   jax.ShapeDtypeStruct((1, d), F32)],
        scratch_shapes=[pltpu.VMEM((seq, sb), F32), pltpu.VMEM((seq, sb), F32)],
        compiler_params=_params(2),
    )(dz, ypre, h, s_re, s_im, bin_re, bin_im, cout_re, cout_im, abar_re, abar_im, dskip)


def _softmax_rows(q, k, scale):
    s = _dot(q, k, 1, 1) * scale
    e = jnp.exp(s - jnp.max(s, axis=-1, keepdims=True))
    return e / jnp.sum(e, axis=-1, keepdims=True)


def _attn_fwd(name, q, kv, n_seq, seq, mlen, heads):
    t, d = q.shape
    hd = d // heads
    tq = _pick(seq, 512, 16)
    nq = seq // tq
    scale = hd ** -0.5
    q_spec = pl.BlockSpec((tq, hd), lambda b, h, i: (b * nq + i, h))

    def body(q_ref, k_ref, v_ref, o_ref):
        p = _softmax_rows(q_ref[...], k_ref[...], scale)
        o_ref[...] = _dot(p, v_ref[...], 1, 0).astype(o_ref.dtype)

    return pl.pallas_call(
        body, name=name, grid=(n_seq, heads, nq),
        in_specs=[q_spec, pl.BlockSpec((mlen, hd), lambda b, h, i: (b, h)),
                  pl.BlockSpec((mlen, hd), lambda b, h, i: (b, heads + h))],
        out_specs=q_spec, out_shape=jax.ShapeDtypeStruct((t, d), BF16), compiler_params=_params(3),
    )(q, kv, kv)


def _attn_bwd(name, q, kv, do, n_seq, seq, mlen, heads):
    t, d = q.shape
    hd = d // heads
    tq = _pick(seq, 512, 16)
    nq = seq // tq
    scale = hd ** -0.5
    q_spec = pl.BlockSpec((tq, hd), lambda b, h, i: (b * nq + i, h))
    k_spec = pl.BlockSpec((mlen, hd), lambda b, h, i: (b, h))

    def body(q_ref, k_ref, v_ref, do_ref, dq_ref, dk_ref, dv_ref):
        q, k, v, do = q_ref[...], k_ref[...], v_ref[...], do_ref[...]
        p = _softmax_rows(q, k, scale)
        dp = _dot(do, v, 1, 1)
        ds = p * (dp - jnp.sum(dp * p, axis=-1, keepdims=True)) * scale
        dq_ref[...] = _dot(ds, k, 1, 0).astype(dq_ref.dtype)

        @pl.when(pl.program_id(2) == 0)
        def _():
            dk_ref[...] = jnp.zeros_like(dk_ref)
            dv_ref[...] = jnp.zeros_like(dv_ref)

        dk_ref[...] += _dot(ds, q, 0, 0)
        dv_ref[...] += _dot(p, do, 0, 0)

    return pl.pallas_call(
        body, name=name, grid=(n_seq, heads, nq),
        in_specs=[q_spec, k_spec, pl.BlockSpec((mlen, hd), lambda b, h, i: (b, heads + h)), q_spec],
        out_specs=[q_spec, k_spec, k_spec],
        out_shape=[jax.ShapeDtypeStruct((t, d), BF16), jax.ShapeDtypeStruct((n_seq * mlen, d), F32),
                   jax.ShapeDtypeStruct((n_seq * mlen, d), F32)],
        compiler_params=_params(3),
    )(q, kv, kv, do)


ADAMW_BLOCK_ELEMS = 128 * 1024


def _adamw(name, parts, w, m, v):
    n_layers = len(parts)
    _, r, c = parts[0].shape
    assert w.shape == (n_layers * r, c), (name, w.shape, parts[0].shape)
    tr = _pick(r, max(16, ADAMW_BLOCK_ELEMS // c // 16 * 16), 8)
    nt = r // tr
    spec = pl.BlockSpec((tr, c), lambda l, i: (l * nt + i, 0))
    c1 = 1.0 - ADAM_B1 ** ADAM_STEP
    c2 = 1.0 - ADAM_B2 ** ADAM_STEP

    def parts_spec(q):
        return pl.BlockSpec((N_DEV, tr, c), lambda l, i: (0, jnp.where(l == q, i, jnp.where(l > q, nt - 1, 0)), 0))

    def body(*refs):
        p_refs = refs[:n_layers]
        w_ref, m_ref, v_ref, g_ref, d_ref, nm_ref, nv_ref = refs[n_layers:]

        def update(p_ref):
            g = p_ref[0].astype(F32)
            for k in range(1, N_DEV):
                g = g + p_ref[k].astype(F32)
            nm = ADAM_B1 * m_ref[...] + (1.0 - ADAM_B1) * g
            nv = ADAM_B2 * v_ref[...] + (1.0 - ADAM_B2) * (g * g)
            g_ref[...] = g
            nm_ref[...] = nm
            nv_ref[...] = nv
            d_ref[...] = -ADAM_LR * ((nm / c1) / (jnp.sqrt(nv / c2) + ADAM_EPS) + ADAM_WD * w_ref[...])

        for q in range(n_layers):
            pl.when(pl.program_id(0) == q)(lambda q=q: update(p_refs[q]))

    out = jax.ShapeDtypeStruct(w.shape, F32)
    return pl.pallas_call(
        body, name=name, grid=(n_layers, nt), in_specs=[parts_spec(q) for q in range(n_layers)] + [spec] * 3,
        out_specs=[spec] * 4, out_shape=[out] * 4, compiler_params=_params(2),
    )(*parts, w, m, v)


def _place():
    x, y, c = lax.axis_index("x"), lax.axis_index("y"), lax.axis_index("c")
    return x, y, c


def _index(px, py, pc):
    return 4 * px + 2 * py + pc


def _all_gather(name, shards):
    n = len(shards)

    def body(*refs):
        in_refs, out_refs = refs[:n], refs[n:2 * n]
        send_sems, recv_sems, local_sems = refs[2 * n:]
        x, y, c = _place()
        me, sibling = (x, y, c), (x, y, 1 - c)
        chips = [(1 - x, y), (x, 1 - y), (1 - x, 1 - y)]

        def slot(k, block):
            return out_refs[k].at[_index(*block)]

        def copy(k, j, block, to, src=None):
            return pltpu.make_async_remote_copy(
                src_ref=slot(k, block) if src is None else src, dst_ref=slot(k, block),
                send_sem=send_sems.at[7 * k + j], recv_sem=recv_sems.at[7 * k + j], device_id=to, device_id_type=MESH)

        mine = [pltpu.make_async_copy(in_refs[k], slot(k, me), local_sems.at[k]) for k in range(n)]
        for cp in mine:
            cp.start()
        first = []
        for k in range(n):
            first.append(copy(k, 0, me, sibling, src=in_refs[k]))
            first += [copy(k, 1 + j, me, (*chip, c), src=in_refs[k]) for j, chip in enumerate(chips)]
        for cp in first:
            cp.start()
        passed = []
        for j, chip in enumerate(chips):
            for k in range(n):
                copy(k, 1 + j, (*chip, c), me).wait_recv()
                cp = copy(k, 4 + j, (*chip, c), sibling)
                cp.start()
                passed.append(cp)
        for k in range(n):
            copy(k, 0, sibling, me).wait_recv()
            for j, chip in enumerate(chips):
                copy(k, 4 + j, (*chip, 1 - c), me).wait_recv()
        for cp in first + passed:
            cp.wait_send()
        for cp in mine:
            cp.wait()

    return pl.pallas_call(
        body, name=name, in_specs=[HBM_SPEC] * n, out_specs=[HBM_SPEC] * n,
        out_shape=[jax.ShapeDtypeStruct((N_DEV,) + s.shape, s.dtype) for s in shards],
        scratch_shapes=[pltpu.SemaphoreType.DMA((7 * n,)), pltpu.SemaphoreType.DMA((7 * n,)),
                        pltpu.SemaphoreType.DMA((n,))],
    )(*shards)


def _xor_peers(x, y, c):
    peers = []
    for r in range(1, N_DEV):
        rx, ry, rc = (r >> 2) & 1, (r >> 1) & 1, r & 1
        peers.append((1 - x if rx else x, 1 - y if ry else y, 1 - c if rc else c))
    return peers


def _exchange_start(name, srcs, lands, src_by_peer, deps=()):
    n = len(srcs)

    def body(*refs):
        src_refs, land_refs = refs[:n], refs[n:2 * n]
        send_sems, recv_sems = refs[2 * n + len(deps)], refs[2 * n + len(deps) + 1]
        token = refs[-1]
        x, y, c = _place()
        me = _index(x, y, c)
        peers = _xor_peers(x, y, c)
        for k in range(n):
            for j, peer in enumerate(peers):
                src = src_refs[k].at[_index(*peer)] if src_by_peer else src_refs[k]
                pltpu.make_async_remote_copy(
                    src_ref=src, dst_ref=land_refs[k].at[me], send_sem=send_sems.at[7 * k + j],
                    recv_sem=recv_sems.at[7 * k + j], device_id=peer, device_id_type=MESH).start()
        token[...] = jnp.zeros_like(token)

    thru = [pltpu.HBM(a.shape, a.dtype) for a in list(srcs) + list(lands)]
    out = pl.pallas_call(
        body, name=name,
        out_shape=(pltpu.SemaphoreType.DMA((7 * n,)), pltpu.SemaphoreType.DMA((7 * n,)), *thru,
                   jax.ShapeDtypeStruct((8, V7X_LANES), F32)),
        in_specs=[HBM_SPEC] * (2 * n) + [ANY_SPEC] * len(deps),
        out_specs=(SEM_SPEC, SEM_SPEC, *([HBM_SPEC] * (2 * n)), pl.BlockSpec(memory_space=pltpu.VMEM)),
        input_output_aliases={k: 2 + k for k in range(2 * n)},
        compiler_params=pltpu.CompilerParams(has_side_effects=pltpu.SideEffectType.DATAFLOW_SIDE_EFFECTING),
    )(*[pltpu.with_memory_space_constraint(a, pltpu.HBM) for a in list(srcs) + list(lands)], *deps)
    return out[0], out[1], list(out[2:2 + n]), list(out[2 + n:2 + 2 * n]), out[-1]


def _exchange_wait(name, send_sems, recv_sems, srcs, lands, src_by_peer, after):
    n = len(srcs)

    def body(*refs):
        src_refs, land_refs = refs[:n], refs[n:2 * n]
        send_sems, recv_sems = refs[2 * n], refs[2 * n + 1]
        x, y, c = _place()
        peers = _xor_peers(x, y, c)
        for k in range(n):
            for j, peer in enumerate(peers):
                src = src_refs[k].at[_index(*peer)] if src_by_peer else src_refs[k]
                cp = pltpu.make_async_remote_copy(
                    src_ref=src, dst_ref=land_refs[k].at[_index(*peer)], send_sem=send_sems.at[7 * k + j],
                    recv_sem=recv_sems.at[7 * k + j], device_id=peer, device_id_type=MESH)
                cp.wait_send()
                cp.wait_recv()

    thru = [pltpu.HBM(a.shape, a.dtype) for a in list(srcs) + list(lands)]
    out = pl.pallas_call(
        body, name=name, out_shape=tuple(thru),
        in_specs=[HBM_SPEC] * (2 * n) + [SEM_SPEC, SEM_SPEC, ANY_SPEC], out_specs=tuple([HBM_SPEC] * (2 * n)),
        input_output_aliases={k: k for k in range(2 * n)},
        compiler_params=pltpu.CompilerParams(has_side_effects=pltpu.SideEffectType.DATAFLOW_SIDE_EFFECTING),
    )(*srcs, *lands, send_sems, recv_sems, after)
    return list(out[n:])


def _landing(shard, me):
    zone = lax.empty((N_DEV,) + shard.shape, shard.dtype)
    return lax.dynamic_update_slice(zone, shard[None], (me,) + (0,) * shard.ndim)


def _cols_whole(w):
    return jnp.transpose(w, (1, 0, 2)).reshape(w.shape[1], N_DEV * w.shape[2])


def _rows_whole(w):
    return w.reshape(N_DEV * w.shape[1], w.shape[2])


def _cols_parts(dw):
    k, n8 = dw.shape
    return jnp.transpose(dw.reshape(k, N_DEV, n8 // N_DEV), (1, 0, 2))


def _rows_parts(dw):
    r8, c = dw.shape
    return dw.reshape(N_DEV, r8 // N_DEV, c)


def _pack_rows(arrays):
    rows = []
    for a in arrays:
        flat = a.reshape(-1).astype(F32)
        flat = jnp.pad(flat, [(0, (-flat.shape[0]) % PACK_TILE)])
        rows.append(flat.reshape(-1, V7X_LANES))
    return jnp.concatenate(rows, axis=0)


def _unpack_rows(packed, shapes):
    out, row = [], 0
    for s in shapes:
        size = math.prod(s)
        n_rows = -(-size // PACK_TILE) * 8
        out.append(packed[row:row + n_rows].reshape(-1)[:size].reshape(s))
        row += n_rows
    return out


def _merge2d(a):
    return a.reshape(-1, a.shape[-1])


def _ffn_fwd(tag, x, g, w_up, w_down, deps=()):
    f = w_down.shape[0]
    n = _rms_fwd(f"{tag}_norm", x, g, deps)
    gu, act = _mm(f"{tag}_up", n, w_up, epilogue=lambda acc: _swiglu_rows(acc, f), outs=[(2 * f, BF16), (f, BF16)])
    out = _mm(f"{tag}_down", act, w_down, res=x, scale=0.5, out_dtype=F32)
    return out, (x, n, gu, act)


def _ffn_bwd(tag, dres, saved, g, w_up, w_down, deps=()):
    x, n, gu, act = saved
    dres32, dres16 = dres
    f = w_down.shape[0]
    dgu = _mm(f"{tag}_down_dx", dres16, w_down, tb=True, scale=0.5, deps=deps, epilogue=_swiglu_bwd_rows,
              row_ins=[(gu, 0, f), (gu, 1, f)], outs=[(2 * f, BF16)])[0]
    d_down = _mm(f"{tag}_down_dw", act, dres16, ta=True, scale=0.5)
    d_up = _mm(f"{tag}_up_dw", n, dgu, ta=True)
    dx, dg = _mm_rms_bwd(f"{tag}_up_dx", dgu, w_up, x, g, dres32)
    return dx, dg, d_up, d_down


def _conv_mixer_fwd(tag, x, g, w_in, w_conv, w_out, n_seq, seq):
    h = _rms_fwd(f"{tag}_norm", x, g)
    cbv = _mm(f"{tag}_in", h, w_in)
    z = _conv_fwd(f"{tag}_conv", cbv, w_conv, n_seq, seq)
    out = _mm(f"{tag}_out", z, w_out, res=x, out_dtype=F32)
    return out, (x, h, cbv, z)


def _conv_mixer_bwd(tag, dres, saved, g, w_in, w_conv, w_out, n_seq, seq):
    x, h, cbv, z = saved
    dres32, dres16 = dres
    dz = _mm(f"{tag}_out_dx", dres16, w_out, tb=True)
    d_out = _mm(f"{tag}_out_dw", z, dres16, ta=True)
    dc, db, dv, d_conv = _conv_bwd(f"{tag}_conv_bwd", dz, cbv, w_conv, n_seq, seq)
    dcbv = jnp.concatenate([dc, db, dv], axis=1)
    d_in = _mm(f"{tag}_in_dw", h, dcbv, ta=True)
    dx, dg = _mm_rms_bwd(f"{tag}_in_dx", dcbv, w_in, x, g, dres32)
    return dx, dg, d_in, d_conv, d_out


def _s5_mixer_fwd(tag, x, g, ssm, dskip, w_glu, n_seq, seq):
    a_re, a_im, log_dt, b_re, b_im, c_re, c_im = ssm
    groups, p, hh = b_re.shape
    gb = S5_CHANNELS // hh
    disc, disc_vjp = jax.vjp(_s5_discretize, a_re, a_im, log_dt, b_re, b_im)
    abar_re, abar_im, bbar_re, bbar_im = disc
    mats = (_block_diag_in(bbar_re, gb).astype(BF16), _block_diag_in(bbar_im, gb).astype(BF16),
            _block_diag_out(c_re, gb).astype(BF16), _block_diag_out(c_im, gb).astype(BF16),
            abar_re.reshape(1, groups * p), abar_im.reshape(1, groups * p), dskip)
    d = x.shape[1]
    h = _rms_fwd(f"{tag}_norm", x, g)
    s_re, s_im, ypre, z = _s5_fwd(f"{tag}_scan", h, *mats, n_seq, seq)
    vg = _mm(f"{tag}_glu", z, w_glu)
    out = _glu_res(f"{tag}_glu_act", vg, x, d)
    return out, (x, h, s_re, s_im, ypre, z, vg, mats, disc_vjp, (groups, p, hh, gb))


def _s5_mixer_bwd(tag, dres, saved, g, w_glu, n_seq, seq):
    x, h, s_re, s_im, ypre, z, vg, mats, disc_vjp, (groups, p, hh, gb) = saved
    d = x.shape[1]
    dres32, _ = dres
    dvg = _glu_bwd(f"{tag}_glu_act_bwd", dres32, vg, d)
    d_glu = _mm(f"{tag}_glu_dw", z, dvg, ta=True)
    dz = _mm(f"{tag}_glu_dx", dvg, w_glu, tb=True)
    dh, dbin_re, dbin_im, dcout_re, dcout_im, dabar_re, dabar_im, d_skip = _s5_bwd(
        f"{tag}_scan_bwd", dz, ypre, h, s_re, s_im, *mats, n_seq, seq)
    d_are, d_aim, d_logdt, d_bre, d_bim = disc_vjp((
        dabar_re.reshape(groups, p), dabar_im.reshape(groups, p),
        _block_diag_in_t(dbin_re, gb, p, hh), _block_diag_in_t(dbin_im, gb, p, hh)))
    d_cre = _block_diag_out_t(dcout_re, gb, p, hh)
    d_cim = _block_diag_out_t(dcout_im, gb, p, hh)
    dx, dg = _rms_bwd(f"{tag}_norm_bwd", x, g, dh, dres32)
    return dx, dg, (d_are, d_aim, d_logdt, d_bre, d_bim, d_cre, d_cim), d_skip, d_glu


def _xattn_fwd(tag, x, mem, g_q, g_mem, w_q, w_kv, w_o, n_seq, seq, mlen, heads):
    n = _rms_fwd(f"{tag}_norm", x, g_q)
    q = _mm(f"{tag}_q", n, w_q)
    mem_n = _rms_fwd(f"{tag}_mem_norm", mem, g_mem)
    kv = _mm(f"{tag}_kv", mem_n, w_kv)
    o = _attn_fwd(f"{tag}_attn", q, kv, n_seq, seq, mlen, heads)
    out = _mm(f"{tag}_o", o, w_o, res=x, out_dtype=F32)
    return out, (x, n, q, mem_n, kv, o)


def _xattn_bwd(tag, dres, saved, mem, g_q, g_mem, w_q, w_kv, w_o, n_seq, seq, mlen, heads):
    x, n, q, mem_n, kv, o = saved
    dres32, dres16 = dres
    do = _mm(f"{tag}_o_dx", dres16, w_o, tb=True)
    d_o = _mm(f"{tag}_o_dw", o, dres16, ta=True)
    dq, dk, dv = _attn_bwd(f"{tag}_attn_bwd", q, kv, do, n_seq, seq, mlen, heads)
    dkv = jnp.concatenate([dk, dv], axis=1)
    d_q = _mm(f"{tag}_q_dw", n, dq, ta=True)
    d_kv = _mm(f"{tag}_kv_dw", mem_n, dkv, ta=True)
    dmem_n = _mm(f"{tag}_kv_dx", dkv, w_kv, tb=True)
    _, dg_mem = _rms_bwd(f"{tag}_mem_norm_bwd", mem, g_mem, dmem_n)
    dx, dg_q = _mm_rms_bwd(f"{tag}_q_dx", dq, w_q, x, g_q, dres32)
    return dx, dg_q, dg_mem, d_q, d_kv, d_o


WEIGHT_NAMES = ("norm_g", "final_g", "ffn1_up", "ffn1_down", "ffn2_up", "ffn2_down", "conv_w_in", "conv_w",
                "conv_w_out", "ssm_a_re", "ssm_a_im", "ssm_log_dt", "ssm_b_re", "ssm_b_im", "ssm_c_re", "ssm_c_im",
                "ssm_d", "ssm_w_glu", "xa_w_q", "xa_w_kv", "xa_w_o")
MATRICES = ("ffn1_up", "ffn1_down", "ffn2_up", "ffn2_down", "conv_w_in", "conv_w_out", "ssm_w_glu", "xa_w_q",
            "xa_w_kv", "xa_w_o")
COLUMN_SHARDED = ("ffn1_up", "ffn2_up", "conv_w_in", "ssm_w_glu", "xa_w_kv")
SMALL_SHARDED = ("norm_g", "conv_w", "ssm_d")
REPLICATED = ("ssm_a_re", "ssm_a_im", "ssm_log_dt", "ssm_b_re", "ssm_b_im", "ssm_c_re", "ssm_c_im", "final_g")


def kernel(x, mem, norm_g, final_g, ffn1_up, ffn1_down, ffn2_up, ffn2_down, conv_w_in, conv_w, conv_w_out, ssm_a_re, ssm_a_im, ssm_log_dt, ssm_b_re, ssm_b_im, ssm_c_re, ssm_c_im, ssm_d, ssm_w_glu, xa_w_q, xa_w_kv, xa_w_o, loss_target, m_norm_g, m_final_g, m_ffn1_up, m_ffn1_down, m_ffn2_up, m_ffn2_down, m_conv_w_in, m_conv_w, m_conv_w_out, m_ssm_a_re, m_ssm_a_im, m_ssm_log_dt, m_ssm_b_re, m_ssm_b_im, m_ssm_c_re, m_ssm_c_im, m_ssm_d, m_ssm_w_glu, m_xa_w_q, m_xa_w_kv, m_xa_w_o, v_norm_g, v_final_g, v_ffn1_up, v_ffn1_down, v_ffn2_up, v_ffn2_down, v_conv_w_in, v_conv_w, v_conv_w_out, v_ssm_a_re, v_ssm_a_im, v_ssm_log_dt, v_ssm_b_re, v_ssm_b_im, v_ssm_c_re, v_ssm_c_im, v_ssm_d, v_ssm_w_glu, v_xa_w_q, v_xa_w_kv, v_xa_w_o):
    w = dict(norm_g=norm_g, final_g=final_g, ffn1_up=ffn1_up, ffn1_down=ffn1_down, ffn2_up=ffn2_up,
             ffn2_down=ffn2_down, conv_w_in=conv_w_in, conv_w=conv_w, conv_w_out=conv_w_out, ssm_a_re=ssm_a_re,
             ssm_a_im=ssm_a_im, ssm_log_dt=ssm_log_dt, ssm_b_re=ssm_b_re, ssm_b_im=ssm_b_im, ssm_c_re=ssm_c_re,
             ssm_c_im=ssm_c_im, ssm_d=ssm_d, ssm_w_glu=ssm_w_glu, xa_w_q=xa_w_q, xa_w_kv=xa_w_kv, xa_w_o=xa_w_o)
    mom = dict(norm_g=m_norm_g, final_g=m_final_g, ffn1_up=m_ffn1_up, ffn1_down=m_ffn1_down, ffn2_up=m_ffn2_up,
               ffn2_down=m_ffn2_down, conv_w_in=m_conv_w_in, conv_w=m_conv_w, conv_w_out=m_conv_w_out,
               ssm_a_re=m_ssm_a_re, ssm_a_im=m_ssm_a_im, ssm_log_dt=m_ssm_log_dt, ssm_b_re=m_ssm_b_re,
               ssm_b_im=m_ssm_b_im, ssm_c_re=m_ssm_c_re, ssm_c_im=m_ssm_c_im, ssm_d=m_ssm_d, ssm_w_glu=m_ssm_w_glu,
               xa_w_q=m_xa_w_q, xa_w_kv=m_xa_w_kv, xa_w_o=m_xa_w_o)
    var = dict(norm_g=v_norm_g, final_g=v_final_g, ffn1_up=v_ffn1_up, ffn1_down=v_ffn1_down, ffn2_up=v_ffn2_up,
               ffn2_down=v_ffn2_down, conv_w_in=v_conv_w_in, conv_w=v_conv_w, conv_w_out=v_conv_w_out,
               ssm_a_re=v_ssm_a_re, ssm_a_im=v_ssm_a_im, ssm_log_dt=v_ssm_log_dt, ssm_b_re=v_ssm_b_re,
               ssm_b_im=v_ssm_b_im, ssm_c_re=v_ssm_c_re, ssm_c_im=v_ssm_c_im, ssm_d=v_ssm_d, ssm_w_glu=v_ssm_w_glu,
               xa_w_q=v_xa_w_q, xa_w_kv=v_xa_w_kv, xa_w_o=v_xa_w_o)

    n_seq, seq, d = x.shape
    mlen = mem.shape[1]
    depth, n_norms = norm_g.shape[0], norm_g.shape[1]
    heads = 4
    tokens = n_seq * seq
    x2 = x.reshape(tokens, d)
    mem2 = mem.reshape(n_seq * mlen, d)
    tgt2 = loss_target.reshape(tokens, d)

    small_shapes = [w[k].shape for k in SMALL_SHARDED]
    small_rows = [_merge2d(w[k]) for k in SMALL_SHARDED]
    small_counts = [s.shape[0] for s in small_rows]
    small = jnp.concatenate(small_rows, axis=0)
    small = jnp.pad(small, [(0, (-small.shape[0]) % 8), (0, 0)])
    me = _index(*_place())

    def layer_weights(i):
        names = [(k, i) for k in ("ffn1_up", "ffn1_down", "ffn2_up", "ffn2_down", "xa_w_q", "xa_w_kv", "xa_w_o")]
        return names + ([("conv_w_in", i // 2), ("conv_w_out", i // 2)] if i % 2 == 0 else [("ssm_w_glu", i // 2)])

    shards = [[w[k][idx].astype(BF16) for k, idx in layer_weights(i)] for i in range(depth)]
    gathered = _all_gather("gather_layer0", shards[0] + [small])
    small_all = gathered[-1]
    blocks = [gathered[:-1]] + [None] * (depth - 1)
    in_flight = [None] * depth
    token = gathered[0]
    for i in range(1, depth):
        zones = [_landing(s, me) for s in shards[i]]
        *in_flight[i], token = _exchange_start(f"gather_start_l{i}", shards[i], zones, False, deps=[token])

    def small_whole(idx):
        start = sum(small_counts[:idx])
        part = small_all[:, start:start + small_counts[idx]]
        lead = small_shapes[idx][:-1]
        part = part.reshape((N_DEV,) + lead + (part.shape[-1],))
        part = jnp.moveaxis(part, 0, -2)
        return part.reshape(lead + (N_DEV * part.shape[-1],))

    norm_all = small_whole(0)
    conv_all = small_whole(1)
    dskip_all = small_whole(2)

    def whole(i):
        return {k: _cols_whole(blk) if k in COLUMN_SHARDED else _rows_whole(blk)
                for (k, _), blk in zip(layer_weights(i), blocks[i])}

    saved = []
    cur = x2
    for i in range(depth):
        g = [norm_all[i, k].reshape(1, d) for k in range(n_norms)]
        j = i // 2
        if i > 0:
            blocks[i] = _exchange_wait(f"gather_wait_l{i}", *in_flight[i], False, cur)
        lw = whole(i)
        cur, s_ffn1 = _ffn_fwd(f"l{i}_ffn1", cur, g[0], lw["ffn1_up"], lw["ffn1_down"], [token] if i == 0 else ())
        if i % 2 == 0:
            lw["conv_w"] = conv_all[j]
            cur, s_mix = _conv_mixer_fwd(f"l{i}_conv", cur, g[1], lw["conv_w_in"], lw["conv_w"], lw["conv_w_out"],
                                         n_seq, seq)
        else:
            ssm = tuple(w[k][j] for k in ("ssm_a_re", "ssm_a_im", "ssm_log_dt", "ssm_b_re", "ssm_b_im",
                                          "ssm_c_re", "ssm_c_im"))
            cur, s_mix = _s5_mixer_fwd(f"l{i}_s5", cur, g[1], ssm, dskip_all[j].reshape(1, d), lw["ssm_w_glu"],
                                       n_seq, seq)
        cur, s_xa = _xattn_fwd(f"l{i}_xa", cur, mem2, g[2], g[3], lw["xa_w_q"], lw["xa_w_kv"], lw["xa_w_o"],
                               n_seq, seq, mlen, heads)
        cur, s_ffn2 = _ffn_fwd(f"l{i}_ffn2", cur, g[4], lw["ffn2_up"], lw["ffn2_down"])
        saved.append((g, lw, s_ffn1, s_mix, s_xa, s_ffn2))

    dres, err2, d_final = _final_loss("loss_head", cur, final_g.reshape(1, d), tgt2)
    loss = lax.psum(0.5 * jnp.sum(err2) / d, ("x", "y", "c"))

    d_norm = [[None] * n_norms for _ in range(depth)]
    d_conv = [None] * conv_w.shape[0]
    d_skip = [None] * ssm_d.shape[0]
    d_ssm = [None] * ssm_a_re.shape[0]
    leaving = [None] * depth
    deps = ()

    def leave(name, keys, gm, extra=()):
        parts = [(_cols_parts if k in COLUMN_SHARDED else _rows_parts)(gm[k]) for k in keys] + list(extra)
        zones = [_landing(lax.dynamic_index_in_dim(p, me, 0, keepdims=False), me) for p in parts]
        return _exchange_start(name, parts, zones, True)

    def small_parts(full):
        lead = full.shape[:-1]
        t = full.reshape(lead + (N_DEV, full.shape[-1] // N_DEV))
        t = jnp.moveaxis(t, -2, 0)
        return t.reshape(N_DEV, -1, t.shape[-1])

    for i in reversed(range(depth)):
        g, lw, s_ffn1, s_mix, s_xa, s_ffn2 = saved[i]
        j = i // 2
        gm = {}
        dres, d_norm[i][4], gm["ffn2_up"], gm["ffn2_down"] = _ffn_bwd(
            f"l{i}_ffn2", dres, s_ffn2, g[4], lw["ffn2_up"], lw["ffn2_down"], deps)
        dres, d_norm[i][2], d_norm[i][3], gm["xa_w_q"], gm["xa_w_kv"], gm["xa_w_o"] = _xattn_bwd(
            f"l{i}_xa", dres, s_xa, mem2, g[2], g[3], lw["xa_w_q"], lw["xa_w_kv"], lw["xa_w_o"], n_seq, seq, mlen,
            heads)
        if i % 2 == 0:
            dres, d_norm[i][1], gm["conv_w_in"], d_conv[j], gm["conv_w_out"] = _conv_mixer_bwd(
                f"l{i}_conv", dres, s_mix, g[1], lw["conv_w_in"], lw["conv_w"], lw["conv_w_out"], n_seq, seq)
        else:
            dres, d_norm[i][1], d_ssm[j], d_skip[j], gm["ssm_w_glu"] = _s5_mixer_bwd(
                f"l{i}_s5", dres, s_mix, g[1], lw["ssm_w_glu"], n_seq, seq)
        leaving[i] = [leave(f"grads_start_l{i}_upper", [k for k, _ in layer_weights(i)[2:]], gm)]
        dres, d_norm[i][0], gm["ffn1_up"], gm["ffn1_down"] = _ffn_bwd(
            f"l{i}_ffn1", dres, s_ffn1, g[0], lw["ffn1_up"], lw["ffn1_down"], [leaving[i][0][-1]])
        extra = []
        if i == 0:
            d_norm_all = jnp.stack([jnp.concatenate(row, axis=0) for row in d_norm])
            small_g = jnp.concatenate(
                [small_parts(a) for a in (d_norm_all, jnp.stack(d_conv), jnp.concatenate(d_skip, axis=0))], axis=1)
            extra = [jnp.pad(small_g, [(0, 0), (0, (-small_g.shape[1]) % 8), (0, 0)])]
        leaving[i].append(leave(f"grads_start_l{i}_lower", ["ffn1_up", "ffn1_down"], gm, extra))
        deps = [leaving[i][1][-1]]
        if i == min(1, depth - 1):
            rep_grads = [jnp.stack([d_ssm[j][k] for j in range(len(d_ssm))]) for k in range(7)]
            rep_packed = _pack_rows(rep_grads + [d_final.reshape(-1)])
            rep_leaving = _exchange_start("replicated_grads_start", [rep_packed], [_landing(rep_packed, me)], False)
            deps = deps + [rep_leaving[-1]]
    grad_x = dres[0].reshape(n_seq, seq, d)
    received = {k: [None] * w[k].shape[0] for k in MATRICES}
    for i in range(depth):
        upper = _exchange_wait(f"grads_wait_l{i}_upper", *leaving[i][0][:4], True, dres[0])
        lower = _exchange_wait(f"grads_wait_l{i}_lower", *leaving[i][1][:4], True, dres[0])
        for (k, idx), blk in zip(layer_weights(i), lower[:2] + upper):
            received[k][idx] = blk
        if i == 0:
            small_received = lower[2]
    rep_all = _exchange_wait("replicated_grads_wait", *rep_leaving[:4], False, dres[0])[0]
    rep_shapes = [w[k].shape for k in REPLICATED]

    grads, deltas, new_m, new_v = {}, {}, {}, {}
    for k in MATRICES:
        shape = w[k].shape
        out = _adamw(f"adamw_{k}", received[k], _merge2d(w[k]), _merge2d(mom[k]), _merge2d(var[k]))
        grads[k], deltas[k], new_m[k], new_v[k] = [o.reshape(shape) for o in out]

    def small_local(src):
        rows = jnp.concatenate([_merge2d(src[k]) for k in SMALL_SHARDED], axis=0)
        return jnp.pad(rows, [(0, (-rows.shape[0]) % 8), (0, 0)])

    out = _adamw("adamw_small", [small_received], small, small_local(mom), small_local(var))
    for res, o in zip((grads, deltas, new_m, new_v), out):
        start = 0
        for k, cnt, shape in zip(SMALL_SHARDED, small_counts, small_shapes):
            res[k] = o[start:start + cnt].reshape(shape)
            start += cnt

    out = _adamw("adamw_replicated", [rep_all], _pack_rows([w[k] for k in REPLICATED]),
                 _pack_rows([mom[k] for k in REPLICATED]), _pack_rows([var[k] for k in REPLICATED]))
    for res, o in zip((grads, deltas, new_m, new_v), out):
        for k, a in zip(REPLICATED, _unpack_rows(o, rep_shapes)):
            res[k] = a

    return (loss, grad_x, *[grads[k] for k in WEIGHT_NAMES], *[deltas[k] for k in WEIGHT_NAMES],
            *[new_m[k] for k in WEIGHT_NAMES], *[new_v[k] for k in WEIGHT_NAMES])
```

```python
import math

import jax
import jax.numpy as jnp
from jax import lax
from jax.experimental import pallas as pl
from jax.experimental.pallas import tpu as pltpu

F32 = jnp.float32
BF16 = jnp.bfloat16
MESH = pl.DeviceIdType.MESH
N_DEV = 8

NORM_EPS = 1e-6
EIG_CLIP = -1e-4
CONV_WIDTH = 3
ADAM_LR = 0.001
ADAM_B1 = 0.9
ADAM_B2 = 0.999
ADAM_EPS = 1e-08
ADAM_WD = 0.01
ADAM_STEP = 10
GELU_C = math.sqrt(2.0 / math.pi)
GELU_A = 0.044715

V7X_LANES = 128
V7X_VMEM_LIMIT = 56 * 1024 * 1024
S5_CHANNELS = 128
PACK_TILE = 8 * V7X_LANES

HBM_SPEC = pl.BlockSpec(memory_space=pltpu.HBM)
ANY_SPEC = pl.BlockSpec(memory_space=pl.ANY)
SEM_SPEC = pl.BlockSpec(memory_space=pltpu.SEMAPHORE)


def _params(n_grid):
    return pltpu.CompilerParams(dimension_semantics=("arbitrary",) * n_grid, vmem_limit_bytes=V7X_VMEM_LIMIT)


def _pick(n, pref, align):
    if n <= pref:
        return n
    t = (pref // align) * align
    while t >= align:
        if n % t == 0:
            return t
        t -= align
    raise ValueError(f"no tile for {n} (pref {pref}, align {align})")


MM_RHS_BLOCK_BYTES = 12 * 1024 * 1024
MM_LHS_BLOCK_BYTES = 6 * 1024 * 1024
MM_ACC_BYTES = 6 * 1024 * 1024
MM_ROWS = 512


MM_EPILOGUE_ROWS = 256


def _mm_tiles(m, k, n, a_item, b_item, ta, max_rows):
    tn = _pick(n, max(V7X_LANES, MM_RHS_BLOCK_BYTES // (k * b_item)), V7X_LANES)
    rows = min(max_rows, MM_ACC_BYTES // (4 * tn), MM_LHS_BLOCK_BYTES // (k * a_item))
    align = V7X_LANES if ta else 16
    tm = _pick(m, max(align, rows), align)
    return tm, tn


def _store_results(out_refs, n_row, results, first):
    if not isinstance(results, (tuple, list)):
        results = (results,)
    for o, v in zip(out_refs[:n_row], results[:n_row]):
        if isinstance(v, (tuple, list)):
            off = 0
            for piece in v:
                w = piece.shape[1]
                o[:, off:off + w] = piece.astype(o.dtype)
                off += w
        else:
            o[...] = v.astype(o.dtype)
    if len(out_refs) > n_row:
        @pl.when(first)
        def _():
            for o in out_refs[n_row:]:
                o[...] = jnp.zeros_like(o)

        for o, v in zip(out_refs[n_row:], results[n_row:]):
            o[...] += v


def _mm(name, a, b, *, ta=False, tb=False, out_dtype=BF16, res=None, scale=None, deps=(),
        epilogue=None, row_ins=(), par_ins=(), outs=(), acc_outs=(), prologue=None, prologue_pars=()):
    if ta:
        k, m = a.shape
    else:
        m, k = a.shape
    if tb:
        n, k2 = b.shape
    else:
        k2, n = b.shape
    assert k == k2, (name, a.shape, b.shape)
    max_rows = MM_ROWS if epilogue is None else MM_EPILOGUE_ROWS
    tm, tn = _mm_tiles(m, k, n, a.dtype.itemsize, b.dtype.itemsize, ta, max_rows)
    a_spec = pl.BlockSpec((k, tm), lambda j, i: (0, i)) if ta else pl.BlockSpec((tm, k), lambda j, i: (i, 0))
    b_spec = pl.BlockSpec((tn, k), lambda j, i: (j, 0)) if tb else pl.BlockSpec((k, tn), lambda j, i: (0, j))
    o_spec = pl.BlockSpec((tm, tn), lambda j, i: (i, j))
    dims = (((0 if ta else 1,), (1 if tb else 0,)), ((), ()))
    has_res = res is not None
    ins = [a, b] + ([res] if has_res else [])
    specs = [a_spec, b_spec] + ([o_spec] if has_res else [])
    n_mm = len(ins)
    if epilogue is None:
        out_specs, out_shape = [o_spec], [jax.ShapeDtypeStruct((m, n), out_dtype)]
    else:
        assert tn == n, (name, tn, n)
        for r in row_ins:
            arr, cb, cw = r if isinstance(r, tuple) else (r, 0, r.shape[1])
            assert arr.shape[0] == m, (name, arr.shape, m)
            ins.append(arr)
            specs.append(pl.BlockSpec((tm, cw), lambda j, i, cb=cb: (i, cb)))
        for p in par_ins:
            ins.append(p)
            specs.append(pl.BlockSpec(p.shape, lambda j, i: (0, 0)))
        out_specs = [pl.BlockSpec((tm, c), lambda j, i: (i, 0)) for c, _ in outs]
        out_specs += [pl.BlockSpec((r, c), lambda j, i: (0, 0)) for r, c in acc_outs]
        out_shape = [jax.ShapeDtypeStruct((m, c), dt) for c, dt in outs]
        out_shape += [jax.ShapeDtypeStruct((r, c), F32) for r, c in acc_outs]
    n_in = len(ins)
    if prologue is not None:
        assert tn == n and not ta, (name, tn, n, ta)
        for p in prologue_pars:
            ins.append(p)
            specs.append(pl.BlockSpec(p.shape, lambda j, i: (0, 0)))
        out_specs = out_specs + [a_spec]
        out_shape = out_shape + [jax.ShapeDtypeStruct((m, k), BF16)]
    n_pro = len(ins)
    ins += list(deps)
    specs += [ANY_SPEC] * len(deps)

    def body(*refs):
        a_ref, b_ref = refs[0], refs[1]
        out_refs = refs[n_pro + len(deps):]
        if prologue is None:
            lhs = a_ref[...].astype(BF16)
        else:
            lhs = prologue(a_ref[...], *[r[...] for r in refs[n_in:n_pro]]).astype(BF16)
            out_refs[-1][...] = lhs
            out_refs = out_refs[:-1]
        acc = lax.dot_general(lhs, b_ref[...].astype(BF16), dims, preferred_element_type=F32)
        if scale is not None:
            acc = acc * scale
        if has_res:
            acc = acc + refs[2][...].astype(F32)
        if epilogue is None:
            out_refs[0][...] = acc.astype(out_refs[0].dtype)
        else:
            extra = [r[...] for r in refs[n_mm:n_in]]
            _store_results(out_refs, len(outs), epilogue(acc, *extra), pl.program_id(1) == 0)

    out = pl.pallas_call(
        body, name=name, grid=(n // tn, m // tm), in_specs=specs, out_specs=out_specs, out_shape=out_shape,
        compiler_params=_params(2),
    )(*ins)
    return out[0] if epilogue is None and prologue is None else out


def _rowwise(name, fn, rows, row_ins, par_ins, row_outs, acc_outs=(), tm_pref=256, deps=()):
    tm = _pick(rows, tm_pref, 16)
    in_specs, ins = [], []
    for r in row_ins:
        arr, cb, cw = r if isinstance(r, tuple) else (r, 0, r.shape[1])
        assert arr.shape[0] == rows, (name, arr.shape, rows)
        ins.append(arr)
        in_specs.append(pl.BlockSpec((tm, cw), lambda i, cb=cb: (i, cb)))
    for p in par_ins:
        ins.append(p)
        in_specs.append(pl.BlockSpec(p.shape, lambda i: (0, 0)))
    out_specs = [pl.BlockSpec((tm, c), lambda i: (i, 0)) for c, _ in row_outs]
    out_specs += [pl.BlockSpec((r, c), lambda i: (0, 0)) for r, c in acc_outs]
    out_shape = [jax.ShapeDtypeStruct((rows, c), dt) for c, dt in row_outs]
    out_shape += [jax.ShapeDtypeStruct((r, c), F32) for r, c in acc_outs]
    n_in, n_row = len(ins), len(row_outs)
    ins += list(deps)
    in_specs += [ANY_SPEC] * len(deps)

    def body(*refs):
        vals = [r[...] for r in refs[:n_in]]
        _store_results(refs[n_in + len(deps):], n_row, fn(*vals), pl.program_id(0) == 0)

    out = pl.pallas_call(
        body, name=name, grid=(rows // tm,), in_specs=in_specs, out_specs=out_specs, out_shape=out_shape,
        compiler_params=_params(1),
    )(*ins)
    return out


def _inv_rms(x):
    return lax.rsqrt(jnp.mean(x * x, axis=-1, keepdims=True) + NORM_EPS)


def _rms_rows(x, g):
    return x * _inv_rms(x) * g


def _rms_fwd(name, x, g):
    return _rowwise(name, _rms_rows, x.shape[0], [x], [g], [(x.shape[1], BF16)], tm_pref=512)[0]


def _rms_bwd_rows(dn, x, dres, g):
    r = _inv_rms(x)
    xh = x * r
    dg = jnp.sum(dn * xh, axis=0, keepdims=True)
    dxh = dn * g
    dx = r * (dxh - xh * jnp.mean(dxh * xh, axis=-1, keepdims=True)) + dres
    return dx, dx, dg


def _rms_bwd(name, x, g, dn, dres=None):
    d = x.shape[1]
    if dres is None:
        def fn(x, dn, g):
            return (jnp.sum(dn.astype(F32) * (x * _inv_rms(x)), axis=0, keepdims=True),)

        return None, _rowwise(name, fn, x.shape[0], [x, dn], [g], [], [(1, d)])[0]

    def fn(x, dn, dres, g):
        return _rms_bwd_rows(dn.astype(F32), x, dres, g)

    out = _rowwise(name, fn, x.shape[0], [x, dn, dres], [g], [(d, F32), (d, BF16)], [(1, d)])
    return (out[0], out[1]), out[2]


def _mm_rms_bwd(name, dy, w, x, g, dres):
    d = x.shape[1]
    out = _mm(name, dy, w, tb=True, epilogue=_rms_bwd_rows, row_ins=[x, dres], par_ins=[g],
              outs=[(d, F32), (d, BF16)], acc_outs=[(1, d)])
    return (out[0], out[1]), out[2]


def _sigmoid(x):
    return 1.0 / (1.0 + jnp.exp(-x))


def _swiglu_rows(gu, f):
    gt = gu[:, :f]
    return gu, gt * _sigmoid(gt) * gu[:, f:]


def _swiglu_bwd_rows(dact, gt, up):
    gt, up = gt.astype(F32), up.astype(F32)
    sg = _sigmoid(gt)
    return ((dact * up * (sg * (1.0 + gt * (1.0 - sg))), dact * (gt * sg)),)


def _glu_res(name, vg, x, d):
    def fn(val, gate, x):
        return x + val.astype(F32) * _sigmoid(gate.astype(F32))

    return _rowwise(name, fn, x.shape[0], [(vg, 0, d), (vg, 1, d), x], [], [(d, F32)])[0]


def _glu_bwd(name, dres, vg, d):
    def fn(dres, val, gate):
        val, gate = val.astype(F32), gate.astype(F32)
        sg = _sigmoid(gate)
        return ((dres * sg, dres * val * sg * (1.0 - sg)),)

    return _rowwise(name, fn, dres.shape[0], [dres, (vg, 0, d), (vg, 1, d)], [], [(2 * d, BF16)])[0]


def _final_loss(name, x, g, tgt):
    d = x.shape[1]

    def fn(x, tgt, g):
        r = _inv_rms(x)
        xh = x * r
        err = xh * g - tgt
        dy = err * (1.0 / d)
        dxh = dy * g
        dx = r * (dxh - xh * jnp.mean(dxh * xh, axis=-1, keepdims=True))
        return dx, dx, jnp.sum(err * err, axis=0, keepdims=True), jnp.sum(dy * xh, axis=0, keepdims=True)

    dx, dx16, err2, dg = _rowwise(name, fn, x.shape[0], [x, tgt], [g], [(d, F32), (d, BF16)], [(1, d), (1, d)])
    return (dx, dx16), err2, dg


def _shift_down(u, k):
    rows = lax.broadcasted_iota(jnp.int32, u.shape, 0)
    return jnp.where(rows >= k, pltpu.roll(u, k, 0), 0.0)


def _shift_up(u, k):
    n = u.shape[0]
    rows = lax.broadcasted_iota(jnp.int32, u.shape, 0)
    return jnp.where(rows < n - k, pltpu.roll(u, n - k, 0), 0.0)


def _conv_specs(seq, cw, n_cb, swap):
    def at(off):
        if swap:
            return pl.BlockSpec((seq, cw), lambda j, b: (b, off * n_cb + j))
        return pl.BlockSpec((seq, cw), lambda b, j: (b, off * n_cb + j))

    return at


def _conv_fwd(name, cbv, w, n_seq, seq):
    d = w.shape[1]
    cw = _pick(d, 256, V7X_LANES)
    n_cb = d // cw
    at = _conv_specs(seq, cw, n_cb, swap=False)

    def body(c_ref, b_ref, v_ref, w_ref, z_ref):
        u = c_ref[...].astype(F32) * v_ref[...].astype(F32)
        cv = w_ref[0:1, :] * _shift_down(u, 2) + w_ref[1:2, :] * _shift_down(u, 1) + w_ref[2:3, :] * u
        z_ref[...] = (b_ref[...].astype(F32) * cv).astype(z_ref.dtype)

    return pl.pallas_call(
        body, name=name, grid=(n_seq, n_cb),
        in_specs=[at(0), at(1), at(2), pl.BlockSpec((CONV_WIDTH, cw), lambda b, j: (0, j))],
        out_specs=at(0), out_shape=jax.ShapeDtypeStruct((n_seq * seq, d), BF16), compiler_params=_params(2),
    )(cbv, cbv, cbv, w)


def _conv_bwd(name, dz, cbv, w, n_seq, seq):
    d = w.shape[1]
    cw = _pick(d, 256, V7X_LANES)
    n_cb = d // cw
    at = _conv_specs(seq, cw, n_cb, swap=True)

    def body(dz_ref, c_ref, b_ref, v_ref, w_ref, dc_ref, db_ref, dv_ref, dw_ref):
        c, b, v = c_ref[...].astype(F32), b_ref[...].astype(F32), v_ref[...].astype(F32)
        dz = dz_ref[...].astype(F32)
        w0, w1, w2 = w_ref[0:1, :], w_ref[1:2, :], w_ref[2:3, :]
        u = c * v
        u1, u2 = _shift_down(u, 1), _shift_down(u, 2)
        cv = w0 * u2 + w1 * u1 + w2 * u
        db_ref[...] = (dz * cv).astype(db_ref.dtype)
        dcv = dz * b
        du = w2 * dcv + w1 * _shift_up(dcv, 1) + w0 * _shift_up(dcv, 2)
        dc_ref[...] = (du * v).astype(dc_ref.dtype)
        dv_ref[...] = (du * c).astype(dv_ref.dtype)

        @pl.when(pl.program_id(1) == 0)
        def _():
            dw_ref[...] = jnp.zeros_like(dw_ref)

        dw_ref[0:1, :] += jnp.sum(dcv * u2, axis=0, keepdims=True)
        dw_ref[1:2, :] += jnp.sum(dcv * u1, axis=0, keepdims=True)
        dw_ref[2:3, :] += jnp.sum(dcv * u, axis=0, keepdims=True)

    act = jax.ShapeDtypeStruct((n_seq * seq, d), BF16)
    return pl.pallas_call(
        body, name=name, grid=(n_cb, n_seq),
        in_specs=[at(0), at(0), at(1), at(2), pl.BlockSpec((CONV_WIDTH, cw), lambda j, b: (0, j))],
        out_specs=[at(0), at(0), at(0), pl.BlockSpec((CONV_WIDTH, cw), lambda j, b: (0, j))],
        out_shape=[act, act, act, jax.ShapeDtypeStruct((CONV_WIDTH, d), F32)], compiler_params=_params(2),
    )(dz, cbv, cbv, cbv, w)


def _s5_discretize(a_re, a_im, log_dt, b_re, b_im):
    lam_re = jnp.minimum(a_re, EIG_CLIP)
    lam_im = a_im
    dt = jnp.exp(log_dt)[:, None]
    mag = jnp.exp(lam_re * dt)
    abar_re = mag * jnp.cos(lam_im * dt)
    abar_im = mag * jnp.sin(lam_im * dt)
    den = lam_re * lam_re + lam_im * lam_im
    num_re = abar_re - 1.0
    num_im = abar_im
    coef_re = (num_re * lam_re + num_im * lam_im) / den
    coef_im = (num_im * lam_re - num_re * lam_im) / den
    bbar_re = coef_re[..., None] * b_re - coef_im[..., None] * b_im
    bbar_im = coef_re[..., None] * b_im + coef_im[..., None] * b_re
    return abar_re, abar_im, bbar_re, bbar_im


def _block_diag_in(bbar, gb):
    g, p, h = bbar.shape
    t = jnp.transpose(bbar.reshape(g // gb, gb, p, h), (0, 1, 3, 2))
    return jnp.einsum("cghp,gk->cghkp", t, jnp.eye(gb, dtype=bbar.dtype)).reshape(g // gb, gb * h, gb * p)


def _block_diag_in_t(blk, gb, p, h):
    nb = blk.shape[0]
    t = jnp.einsum("cghkp,gk->cghp", blk.reshape(nb, gb, h, gb, p), jnp.eye(gb, dtype=blk.dtype))
    return jnp.transpose(t, (0, 1, 3, 2)).reshape(nb * gb, p, h)


def _block_diag_out(c, gb):
    g, h, p = c.shape
    t = jnp.transpose(c.reshape(g // gb, gb, h, p), (0, 1, 3, 2))
    return jnp.einsum("cgph,gk->cgpkh", t, jnp.eye(gb, dtype=c.dtype)).reshape(g // gb, gb * p, gb * h)


def _block_diag_out_t(blk, gb, p, h):
    nb = blk.shape[0]
    t = jnp.einsum("cgpkh,gk->cgph", blk.reshape(nb, gb, p, gb, h), jnp.eye(gb, dtype=blk.dtype))
    return jnp.transpose(t, (0, 1, 3, 2)).reshape(nb * gb, h, p)


def _gelu(y):
    return 0.5 * y * (1.0 + jnp.tanh(GELU_C * (y + GELU_A * y * y * y)))


def _gelu_grad(y):
    th = jnp.tanh(GELU_C * (y + GELU_A * y * y * y))
    return 0.5 * (1.0 + th) + 0.5 * y * (1.0 - th * th) * GELU_C * (1.0 + 3.0 * GELU_A * y * y)


def _dot(a, b, ca, cb):
    return lax.dot_general(a.astype(BF16), b.astype(BF16), (((ca,), (cb,)), ((), ())), preferred_element_type=F32)


def _s5_specs(seq, ch, sb):
    act = pl.BlockSpec((seq, ch), lambda j, b: (b, j))
    state = pl.BlockSpec((seq, sb), lambda j, b: (b, j))
    w_in = pl.BlockSpec((None, ch, sb), lambda j, b: (j, 0, 0))
    w_out = pl.BlockSpec((None, sb, ch), lambda j, b: (j, 0, 0))
    lane_s = pl.BlockSpec((1, sb), lambda j, b: (0, j))
    lane_c = pl.BlockSpec((1, ch), lambda j, b: (0, j))
    return act, state, w_in, w_out, lane_s, lane_c


def _s5_fwd(name, h, bin_re, bin_im, cout_re, cout_im, abar_re, abar_im, dskip, n_seq, seq):
    t, d = h.shape
    nb, ch, sb = bin_re.shape
    act, state, w_in, w_out, lane_s, lane_c = _s5_specs(seq, ch, sb)

    def body(h_ref, bre_ref, bim_ref, cre_ref, cim_ref, ar_ref, ai_ref, d_ref, sre_ref, sim_ref, y_ref, z_ref):
        u = h_ref[...]
        sre_ref[...] = _dot(u, bre_ref[...], 1, 0)
        sim_ref[...] = _dot(u, bim_ref[...], 1, 0)
        ar, ai = ar_ref[...], ai_ref[...]

        def step(i, carry):
            sr, si = carry
            row = pl.ds(i, 1)
            nr = ar * sr - ai * si + sre_ref[row, :]
            ni = ar * si + ai * sr + sim_ref[row, :]
            sre_ref[row, :] = nr
            sim_ref[row, :] = ni
            return nr, ni

        zero = jnp.zeros((1, sb), F32)
        lax.fori_loop(0, seq, step, (zero, zero), unroll=8)
        y = _dot(sre_ref[...], cre_ref[...], 1, 0) - _dot(sim_ref[...], cim_ref[...], 1, 0)
        y = y + d_ref[...] * u.astype(F32)
        y_ref[...] = y
        z_ref[...] = _gelu(y).astype(z_ref.dtype)

    return pl.pallas_call(
        body, name=name, grid=(nb, n_seq),
        in_specs=[act, w_in, w_in, w_out, w_out, lane_s, lane_s, lane_c],
        out_specs=[state, state, act, act],
        out_shape=[jax.ShapeDtypeStruct((t, nb * sb), F32), jax.ShapeDtypeStruct((t, nb * sb), F32),
                   jax.ShapeDtypeStruct((t, d), F32), jax.ShapeDtypeStruct((t, d), BF16)],
        compiler_params=_params(2),
    )(h, bin_re, bin_im, cout_re, cout_im, abar_re, abar_im, dskip)


def _s5_bwd(name, dz, ypre, h, s_re, s_im, bin_re, bin_im, cout_re, cout_im, abar_re, abar_im, dskip, n_seq, seq):
    t, d = h.shape
    nb, ch, sb = bin_re.shape
    act, state, w_in, w_out, lane_s, lane_c = _s5_specs(seq, ch, sb)

    def body(dz_ref, y_ref, h_ref, sre_ref, sim_ref, bre_ref, bim_ref, cre_ref, cim_ref, ar_ref, ai_ref, d_ref,
             dh_ref, dbre_ref, dbim_ref, dcre_ref, dcim_ref, dar_ref, dai_ref, dd_ref, gre, gim):
        first = pl.program_id(1) == 0
        u = h_ref[...].astype(F32)
        dy = dz_ref[...].astype(F32) * _gelu_grad(y_ref[...])
        gre[...] = _dot(dy, cre_ref[...], 1, 1)
        gim[...] = -_dot(dy, cim_ref[...], 1, 1)
        ar, ai = ar_ref[...], ai_ref[...]

        def step(i, carry):
            gr, gi = carry
            row = pl.ds(seq - 1 - i, 1)
            nr = gre[row, :] + ar * gr + ai * gi
            ni = gim[row, :] - ai * gr + ar * gi
            gre[row, :] = nr
            gim[row, :] = ni
            return nr, ni

        zero = jnp.zeros((1, sb), F32)
        lax.fori_loop(0, seq, step, (zero, zero), unroll=8)

        g_re, g_im = gre[...], gim[...]
        s_re, s_im = sre_ref[...], sim_ref[...]
        p_re, p_im = _shift_down(s_re, 1), _shift_down(s_im, 1)
        dar = jnp.sum(g_re * p_re + g_im * p_im, axis=0, keepdims=True)
        dai = jnp.sum(g_im * p_re - g_re * p_im, axis=0, keepdims=True)
        dbre = _dot(u, g_re, 0, 0)
        dbim = _dot(u, g_im, 0, 0)
        dcre = _dot(s_re, dy, 0, 0)
        dcim = -_dot(s_im, dy, 0, 0)
        ddd = jnp.sum(dy * u, axis=0, keepdims=True)
        dh_ref[...] = _dot(g_re, bre_ref[...], 1, 1) + _dot(g_im, bim_ref[...], 1, 1) + d_ref[...] * dy

        @pl.when(first)
        def _():
            dar_ref[...] = dar
            dai_ref[...] = dai
            dbre_ref[...] = dbre
            dbim_ref[...] = dbim
            dcre_ref[...] = dcre
            dcim_ref[...] = dcim
            dd_ref[...] = ddd

        @pl.when(jnp.logical_not(first))
        def _():
            dar_ref[...] += dar
            dai_ref[...] += dai
            dbre_ref[...] += dbre
            dbim_ref[...] += dbim
            dcre_ref[...] += dcre
            dcim_ref[...] += dcim
            dd_ref[...] += ddd

    return pl.pallas_call(
        body, name=name, grid=(nb, n_seq),
        in_specs=[act, act, act, state, state, w_in, w_in, w_out, w_out, lane_s, lane_s, lane_c],
        out_specs=[act, w_in, w_in, w_out, w_out, lane_s, lane_s, lane_c],
        out_shape=[jax.ShapeDtypeStruct((t, d), F32),
                   jax.ShapeDtypeStruct((nb, ch, sb), F32), jax.ShapeDtypeStruct((nb, ch, sb), F32),
                   jax.ShapeDtypeStruct((nb, sb, ch), F32), jax.ShapeDtypeStruct((nb, sb, ch), F32),
                   jax.ShapeDtypeStruct((1, nb * sb), F32), jax.ShapeDtypeStruct((1, nb * sb), F32),
                   jax.ShapeDtypeStruct((1, d), F32)],
        scratch_shapes=[pltpu.VMEM((seq, sb), F32), pltpu.VMEM((seq, sb), F32)],
        compiler_params=_params(2),
    )(dz, ypre, h, s_re, s_im, bin_re, bin_im, cout_re, cout_im, abar_re, abar_im, dskip)


ATTN_QUERY_ROWS = 1024


def _softmax_rows(q, k, scale):
    s = _dot(q, k, 1, 1) * scale
    e = jnp.exp(s - jnp.max(s, axis=-1, keepdims=True))
    return e / jnp.sum(e, axis=-1, keepdims=True)


def _attn_fwd(name, q, kv, n_seq, seq, mlen, heads):
    t, d = q.shape
    hd = d // heads
    tq = _pick(seq, ATTN_QUERY_ROWS, 16)
    nq = seq // tq
    scale = hd ** -0.5
    q_spec = pl.BlockSpec((tq, hd), lambda b, h, i: (b * nq + i, h))

    def body(q_ref, k_ref, v_ref, o_ref):
        p = _softmax_rows(q_ref[...], k_ref[...], scale)
        o_ref[...] = _dot(p, v_ref[...], 1, 0).astype(o_ref.dtype)

    return pl.pallas_call(
        body, name=name, grid=(n_seq, heads, nq),
        in_specs=[q_spec, pl.BlockSpec((mlen, hd), lambda b, h, i: (b, h)),
                  pl.BlockSpec((mlen, hd), lambda b, h, i: (b, heads + h))],
        out_specs=q_spec, out_shape=jax.ShapeDtypeStruct((t, d), BF16), compiler_params=_params(3),
    )(q, kv, kv)


def _attn_bwd(name, q, kv, do, n_seq, seq, mlen, heads):
    t, d = q.shape
    hd = d // heads
    tq = _pick(seq, ATTN_QUERY_ROWS, 16)
    nq = seq // tq
    scale = hd ** -0.5
    q_spec = pl.BlockSpec((tq, hd), lambda b, h, i: (b * nq + i, h))
    k_spec = pl.BlockSpec((mlen, hd), lambda b, h, i: (b, h))

    def body(q_ref, k_ref, v_ref, do_ref, dq_ref, dk_ref, dv_ref):
        q, k, v, do = q_ref[...], k_ref[...], v_ref[...], do_ref[...]
        p = _softmax_rows(q, k, scale)
        dp = _dot(do, v, 1, 1)
        ds = p * (dp - jnp.sum(dp * p, axis=-1, keepdims=True)) * scale
        dq_ref[...] = _dot(ds, k, 1, 0).astype(dq_ref.dtype)

        @pl.when(pl.program_id(2) == 0)
        def _():
            dk_ref[...] = jnp.zeros_like(dk_ref)
            dv_ref[...] = jnp.zeros_like(dv_ref)

        dk_ref[...] += _dot(ds, q, 0, 0)
        dv_ref[...] += _dot(p, do, 0, 0)

    return pl.pallas_call(
        body, name=name, grid=(n_seq, heads, nq),
        in_specs=[q_spec, k_spec, pl.BlockSpec((mlen, hd), lambda b, h, i: (b, heads + h)), q_spec],
        out_specs=[q_spec, k_spec, k_spec],
        out_shape=[jax.ShapeDtypeStruct((t, d), BF16), jax.ShapeDtypeStruct((n_seq * mlen, d), F32),
                   jax.ShapeDtypeStruct((n_seq * mlen, d), F32)],
        compiler_params=_params(3),
    )(q, kv, kv, do)


ADAMW_BLOCK_ELEMS = 128 * 1024


def _adamw(name, parts, w, m, v, first_layer=0, earlier=None):
    n_layers = len(parts)
    _, r, c = parts[0].shape
    assert w.shape[0] % r == 0 and w.shape[1] == c and first_layer + n_layers <= w.shape[0] // r, (name, w.shape)
    tr = _pick(r, max(16, ADAMW_BLOCK_ELEMS // c // 16 * 16), 8)
    nt = r // tr
    spec = pl.BlockSpec((tr, c), lambda l, i: ((first_layer + l) * nt + i, 0))
    c1 = 1.0 - ADAM_B1 ** ADAM_STEP
    c2 = 1.0 - ADAM_B2 ** ADAM_STEP

    def parts_spec(q):
        return pl.BlockSpec((N_DEV, tr, c), lambda l, i: (0, jnp.where(l == q, i, jnp.where(l > q, nt - 1, 0)), 0))

    earlier = list(earlier or ())

    def body(*refs):
        p_refs = refs[:n_layers]
        w_ref, m_ref, v_ref = refs[n_layers:n_layers + 3]
        g_ref, d_ref, nm_ref, nv_ref = refs[n_layers + 3 + len(earlier):]

        def update(p_ref):
            g = p_ref[0].astype(F32)
            for k in range(1, N_DEV):
                g = g + p_ref[k].astype(F32)
            nm = ADAM_B1 * m_ref[...] + (1.0 - ADAM_B1) * g
            nv = ADAM_B2 * v_ref[...] + (1.0 - ADAM_B2) * (g * g)
            g_ref[...] = g
            nm_ref[...] = nm
            nv_ref[...] = nv
            d_ref[...] = -ADAM_LR * ((nm / c1) / (jnp.sqrt(nv / c2) + ADAM_EPS) + ADAM_WD * w_ref[...])

        for q in range(n_layers):
            pl.when(pl.program_id(0) == q)(lambda q=q: update(p_refs[q]))

    out = jax.ShapeDtypeStruct(w.shape, F32)
    return pl.pallas_call(
        body, name=name, grid=(n_layers, nt),
        in_specs=[parts_spec(q) for q in range(n_layers)] + [spec] * 3 + [ANY_SPEC] * len(earlier),
        out_specs=[spec] * 4, out_shape=[out] * 4, compiler_params=_params(2),
        input_output_aliases={n_layers + 3 + q: q for q in range(len(earlier))},
    )(*parts, w, m, v, *earlier)


def _place():
    x, y, c = lax.axis_index("x"), lax.axis_index("y"), lax.axis_index("c")
    return x, y, c


def _index(px, py, pc):
    return 4 * px + 2 * py + pc


def _all_gather(name, shards):
    n = len(shards)

    def body(*refs):
        in_refs, out_refs = refs[:n], refs[n:2 * n]
        send_sems, recv_sems, local_sems = refs[2 * n:]
        x, y, c = _place()
        me, sibling = (x, y, c), (x, y, 1 - c)
        chips = [(1 - x, y), (x, 1 - y), (1 - x, 1 - y)]

        def slot(k, block):
            return out_refs[k].at[_index(*block)]

        def copy(k, j, block, to, src=None):
            return pltpu.make_async_remote_copy(
                src_ref=slot(k, block) if src is None else src, dst_ref=slot(k, block),
                send_sem=send_sems.at[7 * k + j], recv_sem=recv_sems.at[7 * k + j], device_id=to, device_id_type=MESH)

        mine = [pltpu.make_async_copy(in_refs[k], slot(k, me), local_sems.at[k]) for k in range(n)]
        for cp in mine:
            cp.start()
        first = []
        for k in range(n):
            first.append(copy(k, 0, me, sibling, src=in_refs[k]))
            first += [copy(k, 1 + j, me, (*chip, c), src=in_refs[k]) for j, chip in enumerate(chips)]
        for cp in first:
            cp.start()
        passed = []
        for j, chip in enumerate(chips):
            for k in range(n):
                copy(k, 1 + j, (*chip, c), me).wait_recv()
                cp = copy(k, 4 + j, (*chip, c), sibling)
                cp.start()
                passed.append(cp)
        for k in range(n):
            copy(k, 0, sibling, me).wait_recv()
            for j, chip in enumerate(chips):
                copy(k, 4 + j, (*chip, 1 - c), me).wait_recv()
        for cp in first + passed:
            cp.wait_send()
        for cp in mine:
            cp.wait()

    return pl.pallas_call(
        body, name=name, in_specs=[HBM_SPEC] * n, out_specs=[HBM_SPEC] * n,
        out_shape=[jax.ShapeDtypeStruct((N_DEV,) + s.shape, s.dtype) for s in shards],
        scratch_shapes=[pltpu.SemaphoreType.DMA((7 * n,)), pltpu.SemaphoreType.DMA((7 * n,)),
                        pltpu.SemaphoreType.DMA((n,))],
    )(*shards)


def _xor_peers(x, y, c):
    peers = []
    for r in range(1, N_DEV):
        rx, ry, rc = (r >> 2) & 1, (r >> 1) & 1, r & 1
        peers.append((1 - x if rx else x, 1 - y if ry else y, 1 - c if rc else c))
    return peers


def _exchange_start(name, srcs, lands, src_by_peer, deps=()):
    n = len(srcs)

    def body(*refs):
        src_refs, land_refs = refs[:n], refs[n:2 * n]
        send_sems, recv_sems = refs[2 * n + len(deps)], refs[2 * n + len(deps) + 1]
        token = refs[-1]
        x, y, c = _place()
        me = _index(x, y, c)
        peers = _xor_peers(x, y, c)
        for k in range(n):
            for j, peer in enumerate(peers):
                src = src_refs[k].at[_index(*peer)] if src_by_peer else src_refs[k]
                pltpu.make_async_remote_copy(
                    src_ref=src, dst_ref=land_refs[k].at[me], send_sem=send_sems.at[7 * k + j],
                    recv_sem=recv_sems.at[7 * k + j], device_id=peer, device_id_type=MESH).start()
        token[...] = jnp.zeros_like(token)

    thru = [pltpu.HBM(a.shape, a.dtype) for a in list(srcs) + list(lands)]
    out = pl.pallas_call(
        body, name=name,
        out_shape=(pltpu.SemaphoreType.DMA((7 * n,)), pltpu.SemaphoreType.DMA((7 * n,)), *thru,
                   jax.ShapeDtypeStruct((8, V7X_LANES), F32)),
        in_specs=[HBM_SPEC] * (2 * n) + [ANY_SPEC] * len(deps),
        out_specs=(SEM_SPEC, SEM_SPEC, *([HBM_SPEC] * (2 * n)), pl.BlockSpec(memory_space=pltpu.VMEM)),
        input_output_aliases={k: 2 + k for k in range(2 * n)},
        compiler_params=pltpu.CompilerParams(has_side_effects=pltpu.SideEffectType.DATAFLOW_SIDE_EFFECTING),
    )(*[pltpu.with_memory_space_constraint(a, pltpu.HBM) for a in list(srcs) + list(lands)], *deps)
    return out[0], out[1], list(out[2:2 + n]), list(out[2 + n:2 + 2 * n]), out[-1]


def _exchange_wait(name, send_sems, recv_sems, srcs, lands, src_by_peer, after):
    n = len(srcs)

    def body(*refs):
        src_refs, land_refs = refs[:n], refs[n:2 * n]
        send_sems, recv_sems = refs[2 * n], refs[2 * n + 1]
        x, y, c = _place()
        peers = _xor_peers(x, y, c)
        for k in range(n):
            for j, peer in enumerate(peers):
                src = src_refs[k].at[_index(*peer)] if src_by_peer else src_refs[k]
                cp = pltpu.make_async_remote_copy(
                    src_ref=src, dst_ref=land_refs[k].at[_index(*peer)], send_sem=send_sems.at[7 * k + j],
                    recv_sem=recv_sems.at[7 * k + j], device_id=peer, device_id_type=MESH)
                cp.wait_send()
                cp.wait_recv()

    thru = [pltpu.HBM(a.shape, a.dtype) for a in list(srcs) + list(lands)]
    out = pl.pallas_call(
        body, name=name, out_shape=tuple(thru),
        in_specs=[HBM_SPEC] * (2 * n) + [SEM_SPEC, SEM_SPEC, ANY_SPEC], out_specs=tuple([HBM_SPEC] * (2 * n)),
        input_output_aliases={k: k for k in range(2 * n)},
        compiler_params=pltpu.CompilerParams(has_side_effects=pltpu.SideEffectType.DATAFLOW_SIDE_EFFECTING),
    )(*srcs, *lands, send_sems, recv_sems, after)
    return list(out[n:])


def _landing(shard, me):
    zone = lax.empty((N_DEV,) + shard.shape, shard.dtype)
    return lax.dynamic_update_slice(zone, shard[None], (me,) + (0,) * shard.ndim)


def _cols_whole(w):
    return jnp.transpose(w, (1, 0, 2)).reshape(w.shape[1], N_DEV * w.shape[2])


def _rows_whole(w):
    return w.reshape(N_DEV * w.shape[1], w.shape[2])


def _cols_parts(dw):
    k, n8 = dw.shape
    return jnp.transpose(dw.reshape(k, N_DEV, n8 // N_DEV), (1, 0, 2))


def _rows_parts(dw):
    r8, c = dw.shape
    return dw.reshape(N_DEV, r8 // N_DEV, c)


def _pack_rows(arrays):
    rows = []
    for a in arrays:
        flat = a.reshape(-1).astype(F32)
        flat = jnp.pad(flat, [(0, (-flat.shape[0]) % PACK_TILE)])
        rows.append(flat.reshape(-1, V7X_LANES))
    return jnp.concatenate(rows, axis=0)


def _unpack_rows(packed, shapes):
    out, row = [], 0
    for s in shapes:
        size = math.prod(s)
        n_rows = -(-size // PACK_TILE) * 8
        out.append(packed[row:row + n_rows].reshape(-1)[:size].reshape(s))
        row += n_rows
    return out


def _merge2d(a):
    return a.reshape(-1, a.shape[-1])


def _ffn_fwd(tag, x, g, w_up, w_down, deps=()):
    f = w_down.shape[0]
    gu, act, n = _mm(f"{tag}_up", x, w_up, prologue=_rms_rows, prologue_pars=[g], deps=deps,
                     epilogue=lambda acc: _swiglu_rows(acc, f), outs=[(2 * f, BF16), (f, BF16)])
    out = _mm(f"{tag}_down", act, w_down, res=x, scale=0.5, out_dtype=F32)
    return out, (x, n, gu, act)


def _ffn_bwd(tag, dres, saved, g, w_up, w_down, deps=()):
    x, n, gu, act = saved
    dres32, dres16 = dres
    f = w_down.shape[0]
    dgu = _mm(f"{tag}_down_dx", dres16, w_down, tb=True, scale=0.5, deps=deps, epilogue=_swiglu_bwd_rows,
              row_ins=[(gu, 0, f), (gu, 1, f)], outs=[(2 * f, BF16)])[0]
    d_down = _mm(f"{tag}_down_dw", act, dres16, ta=True, scale=0.5)
    d_up = _mm(f"{tag}_up_dw", n, dgu, ta=True)
    dx, dg = _mm_rms_bwd(f"{tag}_up_dx", dgu, w_up, x, g, dres32)
    return dx, dg, d_up, d_down


def _conv_mixer_fwd(tag, x, g, w_in, w_conv, w_out, n_seq, seq):
    cbv, h = _mm(f"{tag}_in", x, w_in, prologue=_rms_rows, prologue_pars=[g])
    z = _conv_fwd(f"{tag}_conv", cbv, w_conv, n_seq, seq)
    out = _mm(f"{tag}_out", z, w_out, res=x, out_dtype=F32)
    return out, (x, h, cbv, z)


def _conv_mixer_bwd(tag, dres, saved, g, w_in, w_conv, w_out, n_seq, seq):
    x, h, cbv, z = saved
    dres32, dres16 = dres
    dz = _mm(f"{tag}_out_dx", dres16, w_out, tb=True)
    d_out = _mm(f"{tag}_out_dw", z, dres16, ta=True)
    dc, db, dv, d_conv = _conv_bwd(f"{tag}_conv_bwd", dz, cbv, w_conv, n_seq, seq)
    dcbv = jnp.concatenate([dc, db, dv], axis=1)
    d_in = _mm(f"{tag}_in_dw", h, dcbv, ta=True)
    dx, dg = _mm_rms_bwd(f"{tag}_in_dx", dcbv, w_in, x, g, dres32)
    return dx, dg, d_in, d_conv, d_out


def _s5_mixer_fwd(tag, x, g, ssm, dskip, w_glu, n_seq, seq):
    a_re, a_im, log_dt, b_re, b_im, c_re, c_im = ssm
    groups, p, hh = b_re.shape
    gb = S5_CHANNELS // hh
    disc, disc_vjp = jax.vjp(_s5_discretize, a_re, a_im, log_dt, b_re, b_im)
    abar_re, abar_im, bbar_re, bbar_im = disc
    mats = (_block_diag_in(bbar_re, gb).astype(BF16), _block_diag_in(bbar_im, gb).astype(BF16),
            _block_diag_out(c_re, gb).astype(BF16), _block_diag_out(c_im, gb).astype(BF16),
            abar_re.reshape(1, groups * p), abar_im.reshape(1, groups * p), dskip)
    d = x.shape[1]
    h = _rms_fwd(f"{tag}_norm", x, g)
    s_re, s_im, ypre, z = _s5_fwd(f"{tag}_scan", h, *mats, n_seq, seq)
    vg = _mm(f"{tag}_glu", z, w_glu)
    out = _glu_res(f"{tag}_glu_act", vg, x, d)
    return out, (x, h, s_re, s_im, ypre, z, vg, mats, disc_vjp, (groups, p, hh, gb))


def _s5_mixer_bwd(tag, dres, saved, g, w_glu, n_seq, seq):
    x, h, s_re, s_im, ypre, z, vg, mats, disc_vjp, (groups, p, hh, gb) = saved
    d = x.shape[1]
    dres32, _ = dres
    dvg = _glu_bwd(f"{tag}_glu_act_bwd", dres32, vg, d)
    d_glu = _mm(f"{tag}_glu_dw", z, dvg, ta=True)
    dz = _mm(f"{tag}_glu_dx", dvg, w_glu, tb=True)
    dh, dbin_re, dbin_im, dcout_re, dcout_im, dabar_re, dabar_im, d_skip = _s5_bwd(
        f"{tag}_scan_bwd", dz, ypre, h, s_re, s_im, *mats, n_seq, seq)
    d_are, d_aim, d_logdt, d_bre, d_bim = disc_vjp((
        dabar_re.reshape(groups, p), dabar_im.reshape(groups, p),
        _block_diag_in_t(dbin_re, gb, p, hh), _block_diag_in_t(dbin_im, gb, p, hh)))
    d_cre = _block_diag_out_t(dcout_re, gb, p, hh)
    d_cim = _block_diag_out_t(dcout_im, gb, p, hh)
    dx, dg = _rms_bwd(f"{tag}_norm_bwd", x, g, dh, dres32)
    return dx, dg, (d_are, d_aim, d_logdt, d_bre, d_bim, d_cre, d_cim), d_skip, d_glu


def _xattn_fwd(tag, x, mem, g_q, g_mem, w_q, w_kv, w_o, n_seq, seq, mlen, heads):
    q, n = _mm(f"{tag}_q", x, w_q, prologue=_rms_rows, prologue_pars=[g_q])
    mem_n = _rms_fwd(f"{tag}_mem_norm", mem, g_mem)
    kv = _mm(f"{tag}_kv", mem_n, w_kv)
    o = _attn_fwd(f"{tag}_attn", q, kv, n_seq, seq, mlen, heads)
    out = _mm(f"{tag}_o", o, w_o, res=x, out_dtype=F32)
    return out, (x, n, q, mem_n, kv, o)


def _xattn_bwd(tag, dres, saved, mem, g_q, g_mem, w_q, w_kv, w_o, n_seq, seq, mlen, heads):
    x, n, q, mem_n, kv, o = saved
    dres32, dres16 = dres
    do = _mm(f"{tag}_o_dx", dres16, w_o, tb=True)
    d_o = _mm(f"{tag}_o_dw", o, dres16, ta=True)
    dq, dk, dv = _attn_bwd(f"{tag}_attn_bwd", q, kv, do, n_seq, seq, mlen, heads)
    dkv = jnp.concatenate([dk, dv], axis=1)
    d_q = _mm(f"{tag}_q_dw", n, dq, ta=True)
    d_kv = _mm(f"{tag}_kv_dw", mem_n, dkv, ta=True)
    dmem_n = _mm(f"{tag}_kv_dx", dkv, w_kv, tb=True)
    _, dg_mem = _rms_bwd(f"{tag}_mem_norm_bwd", mem, g_mem, dmem_n)
    dx, dg_q = _mm_rms_bwd(f"{tag}_q_dx", dq, w_q, x, g_q, dres32)
    return dx, dg_q, dg_mem, d_q, d_kv, d_o


WEIGHT_NAMES = ("norm_g", "final_g", "ffn1_up", "ffn1_down", "ffn2_up", "ffn2_down", "conv_w_in", "conv_w",
                "conv_w_out", "ssm_a_re", "ssm_a_im", "ssm_log_dt", "ssm_b_re", "ssm_b_im", "ssm_c_re", "ssm_c_im",
                "ssm_d", "ssm_w_glu", "xa_w_q", "xa_w_kv", "xa_w_o")
MATRICES = ("ffn1_up", "ffn1_down", "ffn2_up", "ffn2_down", "conv_w_in", "conv_w_out", "ssm_w_glu", "xa_w_q",
            "xa_w_kv", "xa_w_o")
COLUMN_SHARDED = ("ffn1_up", "ffn2_up", "conv_w_in", "ssm_w_glu", "xa_w_kv")
SMALL_SHARDED = ("norm_g", "conv_w", "ssm_d")
REPLICATED = ("ssm_a_re", "ssm_a_im", "ssm_log_dt", "ssm_b_re", "ssm_b_im", "ssm_c_re", "ssm_c_im", "final_g")


def kernel(x, mem, norm_g, final_g, ffn1_up, ffn1_down, ffn2_up, ffn2_down, conv_w_in, conv_w, conv_w_out, ssm_a_re, ssm_a_im, ssm_log_dt, ssm_b_re, ssm_b_im, ssm_c_re, ssm_c_im, ssm_d, ssm_w_glu, xa_w_q, xa_w_kv, xa_w_o, loss_target, m_norm_g, m_final_g, m_ffn1_up, m_ffn1_down, m_ffn2_up, m_ffn2_down, m_conv_w_in, m_conv_w, m_conv_w_out, m_ssm_a_re, m_ssm_a_im, m_ssm_log_dt, m_ssm_b_re, m_ssm_b_im, m_ssm_c_re, m_ssm_c_im, m_ssm_d, m_ssm_w_glu, m_xa_w_q, m_xa_w_kv, m_xa_w_o, v_norm_g, v_final_g, v_ffn1_up, v_ffn1_down, v_ffn2_up, v_ffn2_down, v_conv_w_in, v_conv_w, v_conv_w_out, v_ssm_a_re, v_ssm_a_im, v_ssm_log_dt, v_ssm_b_re, v_ssm_b_im, v_ssm_c_re, v_ssm_c_im, v_ssm_d, v_ssm_w_glu, v_xa_w_q, v_xa_w_kv, v_xa_w_o):
    w = dict(norm_g=norm_g, final_g=final_g, ffn1_up=ffn1_up, ffn1_down=ffn1_down, ffn2_up=ffn2_up,
             ffn2_down=ffn2_down, conv_w_in=conv_w_in, conv_w=conv_w, conv_w_out=conv_w_out, ssm_a_re=ssm_a_re,
             ssm_a_im=ssm_a_im, ssm_log_dt=ssm_log_dt, ssm_b_re=ssm_b_re, ssm_b_im=ssm_b_im, ssm_c_re=ssm_c_re,
             ssm_c_im=ssm_c_im, ssm_d=ssm_d, ssm_w_glu=ssm_w_glu, xa_w_q=xa_w_q, xa_w_kv=xa_w_kv, xa_w_o=xa_w_o)
    mom = dict(norm_g=m_norm_g, final_g=m_final_g, ffn1_up=m_ffn1_up, ffn1_down=m_ffn1_down, ffn2_up=m_ffn2_up,
               ffn2_down=m_ffn2_down, conv_w_in=m_conv_w_in, conv_w=m_conv_w, conv_w_out=m_conv_w_out,
               ssm_a_re=m_ssm_a_re, ssm_a_im=m_ssm_a_im, ssm_log_dt=m_ssm_log_dt, ssm_b_re=m_ssm_b_re,
               ssm_b_im=m_ssm_b_im, ssm_c_re=m_ssm_c_re, ssm_c_im=m_ssm_c_im, ssm_d=m_ssm_d, ssm_w_glu=m_ssm_w_glu,
               xa_w_q=m_xa_w_q, xa_w_kv=m_xa_w_kv, xa_w_o=m_xa_w_o)
    var = dict(norm_g=v_norm_g, final_g=v_final_g, ffn1_up=v_ffn1_up, ffn1_down=v_ffn1_down, ffn2_up=v_ffn2_up,
               ffn2_down=v_ffn2_down, conv_w_in=v_conv_w_in, conv_w=v_conv_w, conv_w_out=v_conv_w_out,
               ssm_a_re=v_ssm_a_re, ssm_a_im=v_ssm_a_im, ssm_log_dt=v_ssm_log_dt, ssm_b_re=v_ssm_b_re,
               ssm_b_im=v_ssm_b_im, ssm_c_re=v_ssm_c_re, ssm_c_im=v_ssm_c_im, ssm_d=v_ssm_d, ssm_w_glu=v_ssm_w_glu,
               xa_w_q=v_xa_w_q, xa_w_kv=v_xa_w_kv, xa_w_o=v_xa_w_o)

    n_seq, seq, d = x.shape
    mlen = mem.shape[1]
    depth, n_norms = norm_g.shape[0], norm_g.shape[1]
    heads = 4
    tokens = n_seq * seq
    x2 = x.reshape(tokens, d)
    mem2 = mem.reshape(n_seq * mlen, d)
    tgt2 = loss_target.reshape(tokens, d)

    small_shapes = [w[k].shape for k in SMALL_SHARDED]
    small_rows = [_merge2d(w[k]) for k in SMALL_SHARDED]
    small_counts = [s.shape[0] for s in small_rows]
    small = jnp.concatenate(small_rows, axis=0)
    small = jnp.pad(small, [(0, (-small.shape[0]) % 8), (0, 0)])
    me = _index(*_place())

    def layer_weights(i):
        names = [(k, i) for k in ("ffn1_up", "ffn1_down", "ffn2_up", "ffn2_down", "xa_w_q", "xa_w_kv", "xa_w_o")]
        return names + ([("conv_w_in", i // 2), ("conv_w_out", i // 2)] if i % 2 == 0 else [("ssm_w_glu", i // 2)])

    shards = [[w[k][idx].astype(BF16) for k, idx in layer_weights(i)] for i in range(depth)]
    gathered = _all_gather("gather_layer0", shards[0] + [small])
    small_all = gathered[-1]
    blocks = [gathered[:-1]] + [None] * (depth - 1)
    in_flight = [None] * depth
    token = gathered[0]
    for i in range(1, depth):
        zones = [_landing(s, me) for s in shards[i]]
        *in_flight[i], token = _exchange_start(f"gather_start_l{i}", shards[i], zones, False, deps=[token])

    def small_whole(idx):
        start = sum(small_counts[:idx])
        part = small_all[:, start:start + small_counts[idx]]
        lead = small_shapes[idx][:-1]
        part = part.reshape((N_DEV,) + lead + (part.shape[-1],))
        part = jnp.moveaxis(part, 0, -2)
        return part.reshape(lead + (N_DEV * part.shape[-1],))

    norm_all = small_whole(0)
    conv_all = small_whole(1)
    dskip_all = small_whole(2)

    def whole(i):
        return {k: _cols_whole(blk) if k in COLUMN_SHARDED else _rows_whole(blk)
                for (k, _), blk in zip(layer_weights(i), blocks[i])}

    saved = []
    cur = x2
    for i in range(depth):
        g = [norm_all[i, k].reshape(1, d) for k in range(n_norms)]
        j = i // 2
        if i > 0:
            blocks[i] = _exchange_wait(f"gather_wait_l{i}", *in_flight[i], False, cur)
        lw = whole(i)
        cur, s_ffn1 = _ffn_fwd(f"l{i}_ffn1", cur, g[0], lw["ffn1_up"], lw["ffn1_down"], [token] if i == 0 else ())
        if i % 2 == 0:
            lw["conv_w"] = conv_all[j]
            cur, s_mix = _conv_mixer_fwd(f"l{i}_conv", cur, g[1], lw["conv_w_in"], lw["conv_w"], lw["conv_w_out"],
                                         n_seq, seq)
        else:
            ssm = tuple(w[k][j] for k in ("ssm_a_re", "ssm_a_im", "ssm_log_dt", "ssm_b_re", "ssm_b_im",
                                          "ssm_c_re", "ssm_c_im"))
            cur, s_mix = _s5_mixer_fwd(f"l{i}_s5", cur, g[1], ssm, dskip_all[j].reshape(1, d), lw["ssm_w_glu"],
                                       n_seq, seq)
        cur, s_xa = _xattn_fwd(f"l{i}_xa", cur, mem2, g[2], g[3], lw["xa_w_q"], lw["xa_w_kv"], lw["xa_w_o"],
                               n_seq, seq, mlen, heads)
        cur, s_ffn2 = _ffn_fwd(f"l{i}_ffn2", cur, g[4], lw["ffn2_up"], lw["ffn2_down"])
        saved.append((g, lw, s_ffn1, s_mix, s_xa, s_ffn2))

    dres, err2, d_final = _final_loss("loss_head", cur, final_g.reshape(1, d), tgt2)
    loss = lax.psum(0.5 * jnp.sum(err2) / d, ("x", "y", "c"))

    d_norm = [[None] * n_norms for _ in range(depth)]
    d_conv = [None] * conv_w.shape[0]
    d_skip = [None] * ssm_d.shape[0]
    d_ssm = [None] * ssm_a_re.shape[0]
    leaving = [None] * depth
    deps = ()

    def leave(name, keys, gm, extra=()):
        parts = [(_cols_parts if k in COLUMN_SHARDED else _rows_parts)(gm[k]) for k in keys] + list(extra)
        zones = [_landing(lax.dynamic_index_in_dim(p, me, 0, keepdims=False), me) for p in parts]
        return _exchange_start(name, parts, zones, True)

    def small_parts(full):
        lead = full.shape[:-1]
        t = full.reshape(lead + (N_DEV, full.shape[-1] // N_DEV))
        t = jnp.moveaxis(t, -2, 0)
        return t.reshape(N_DEV, -1, t.shape[-1])

    for i in reversed(range(depth)):
        g, lw, s_ffn1, s_mix, s_xa, s_ffn2 = saved[i]
        j = i // 2
        gm = {}
        dres, d_norm[i][4], gm["ffn2_up"], gm["ffn2_down"] = _ffn_bwd(
            f"l{i}_ffn2", dres, s_ffn2, g[4], lw["ffn2_up"], lw["ffn2_down"], deps)
        dres, d_norm[i][2], d_norm[i][3], gm["xa_w_q"], gm["xa_w_kv"], gm["xa_w_o"] = _xattn_bwd(
            f"l{i}_xa", dres, s_xa, mem2, g[2], g[3], lw["xa_w_q"], lw["xa_w_kv"], lw["xa_w_o"], n_seq, seq, mlen,
            heads)
        if i % 2 == 0:
            dres, d_norm[i][1], gm["conv_w_in"], d_conv[j], gm["conv_w_out"] = _conv_mixer_bwd(
                f"l{i}_conv", dres, s_mix, g[1], lw["conv_w_in"], lw["conv_w"], lw["conv_w_out"], n_seq, seq)
        else:
            dres, d_norm[i][1], d_ssm[j], d_skip[j], gm["ssm_w_glu"] = _s5_mixer_bwd(
                f"l{i}_s5", dres, s_mix, g[1], lw["ssm_w_glu"], n_seq, seq)
        leaving[i] = [leave(f"grads_start_l{i}_upper", [k for k, _ in layer_weights(i)[2:]], gm)]
        dres, d_norm[i][0], gm["ffn1_up"], gm["ffn1_down"] = _ffn_bwd(
            f"l{i}_ffn1", dres, s_ffn1, g[0], lw["ffn1_up"], lw["ffn1_down"], [leaving[i][0][-1]])
        extra = []
        if i == 0:
            d_norm_all = jnp.stack([jnp.concatenate(row, axis=0) for row in d_norm])
            small_g = jnp.concatenate(
                [small_parts(a) for a in (d_norm_all, jnp.stack(d_conv), jnp.concatenate(d_skip, axis=0))], axis=1)
            extra = [jnp.pad(small_g, [(0, 0), (0, (-small_g.shape[1]) % 8), (0, 0)])]
        leaving[i].append(leave(f"grads_start_l{i}_lower", ["ffn1_up", "ffn1_down"], gm, extra))
        deps = [leaving[i][1][-1]]
        if i == min(1, depth - 1):
            rep_grads = [jnp.stack([d_ssm[j][k] for j in range(len(d_ssm))]) for k in range(7)]
            rep_packed = _pack_rows(rep_grads + [d_final.reshape(-1)])
            rep_leaving = _exchange_start("replicated_grads_start", [rep_packed], [_landing(rep_packed, me)], False)
            deps = deps + [rep_leaving[-1]]
    grad_x = dres[0].reshape(n_seq, seq, d)
    received = {k: [None] * w[k].shape[0] for k in MATRICES}

    def arrive(i, after):
        upper = _exchange_wait(f"grads_wait_l{i}_upper", *leaving[i][0][:4], True, after)
        lower = _exchange_wait(f"grads_wait_l{i}_lower", *leaving[i][1][:4], True, after)
        for (k, idx), blk in zip(layer_weights(i), lower[:2] + upper):
            received[k][idx] = blk
        return lower[2:]

    for i in range(1, depth):
        arrive(i, dres[0])
    rep_all = _exchange_wait("replicated_grads_wait", *rep_leaving[:4], False, dres[0])[0]
    rep_shapes = [w[k].shape for k in REPLICATED]
    flat = {k: (_merge2d(w[k]), _merge2d(mom[k]), _merge2d(var[k])) for k in MATRICES}
    late = {k: received[k][0] is None for k in MATRICES}
    early = {}
    for k in MATRICES:
        first = 1 if late[k] else 0
        if first < len(received[k]):
            early[k] = _adamw(f"adamw_{k}_upper", received[k][first:], *flat[k], first_layer=first)
    small_received, = arrive(0, list(early.values())[-1][0] if early else dres[0])
    grads, deltas, new_m, new_v = {}, {}, {}, {}
    for k in MATRICES:
        out = early.get(k)
        if late[k]:
            out = _adamw(f"adamw_{k}_l0", received[k][:1], *flat[k], earlier=out)
        grads[k], deltas[k], new_m[k], new_v[k] = [o.reshape(w[k].shape) for o in out]

    def small_local(src):
        rows = jnp.concatenate([_merge2d(src[k]) for k in SMALL_SHARDED], axis=0)
        return jnp.pad(rows, [(0, (-rows.shape[0]) % 8), (0, 0)])

    out = _adamw("adamw_small", [small_received], small, small_local(mom), small_local(var))
    for res, o in zip((grads, deltas, new_m, new_v), out):
        start = 0
        for k, cnt, shape in zip(SMALL_SHARDED, small_counts, small_shapes):
            res[k] = o[start:start + cnt].reshape(shape)
            start += cnt

    out = _adamw("adamw_replicated", [rep_all], _pack_rows([w[k] for k in REPLICATED]),
                 _pack_rows([mom[k] for k in REPLICATED]), _pack_rows([var[k] for k in REPLICATED]))
    for res, o in zip((grads, deltas, new_m, new_v), out):
        for k, a in zip(REPLICATED, _unpack_rows(o, rep_shapes)):
            res[k] = a

    return (loss, grad_x, *[grads[k] for k in WEIGHT_NAMES], *[deltas[k] for k in WEIGHT_NAMES],
            *[new_m[k] for k in WEIGHT_NAMES], *[new_v[k] for k in WEIGHT_NAMES])
```

```python
import math

import jax
import jax.numpy as jnp
from jax import lax
from jax.experimental import pallas as pl
from jax.experimental.pallas import tpu as pltpu

F32 = jnp.float32
BF16 = jnp.bfloat16
MESH = pl.DeviceIdType.MESH
N_DEV = 8

NORM_EPS = 1e-6
EIG_CLIP = -1e-4
CONV_WIDTH = 3
ADAM_LR = 0.001
ADAM_B1 = 0.9
ADAM_B2 = 0.999
ADAM_EPS = 1e-08
ADAM_WD = 0.01
ADAM_STEP = 10
GELU_C = math.sqrt(2.0 / math.pi)
GELU_A = 0.044715

V7X_LANES = 128
V7X_VMEM_LIMIT = 56 * 1024 * 1024
S5_CHANNELS = 128
PACK_TILE = 8 * V7X_LANES

HBM_SPEC = pl.BlockSpec(memory_space=pltpu.HBM)
ANY_SPEC = pl.BlockSpec(memory_space=pl.ANY)
SEM_SPEC = pl.BlockSpec(memory_space=pltpu.SEMAPHORE)


def _params(n_grid):
    return pltpu.CompilerParams(dimension_semantics=("arbitrary",) * n_grid, vmem_limit_bytes=V7X_VMEM_LIMIT)


def _pick(n, pref, align):
    if n <= pref:
        return n
    t = (pref // align) * align
    while t >= align:
        if n % t == 0:
            return t
        t -= align
    raise ValueError(f"no tile for {n} (pref {pref}, align {align})")


MM_RHS_BLOCK_BYTES = 12 * 1024 * 1024
MM_LHS_BLOCK_BYTES = 6 * 1024 * 1024
MM_ACC_BYTES = 6 * 1024 * 1024
MM_ROWS = 512


MM_EPILOGUE_ROWS = 256


def _mm_tiles(m, k, n, a_item, b_item, ta, max_rows):
    tn = _pick(n, max(V7X_LANES, MM_RHS_BLOCK_BYTES // (k * b_item)), V7X_LANES)
    rows = min(max_rows, MM_ACC_BYTES // (4 * tn), MM_LHS_BLOCK_BYTES // (k * a_item))
    align = V7X_LANES if ta else 16
    tm = _pick(m, max(align, rows), align)
    return tm, tn


def _store_results(out_refs, n_row, results, first):
    if not isinstance(results, (tuple, list)):
        results = (results,)
    for o, v in zip(out_refs[:n_row], results[:n_row]):
        if isinstance(v, (tuple, list)):
            off = 0
            for piece in v:
                w = piece.shape[1]
                o[:, off:off + w] = piece.astype(o.dtype)
                off += w
        else:
            o[...] = v.astype(o.dtype)
    if len(out_refs) > n_row:
        @pl.when(first)
        def _():
            for o in out_refs[n_row:]:
                o[...] = jnp.zeros_like(o)

        for o, v in zip(out_refs[n_row:], results[n_row:]):
            o[...] += v


def _mm(name, a, b, *, ta=False, tb=False, out_dtype=BF16, res=None, scale=None, deps=(),
        epilogue=None, row_ins=(), par_ins=(), outs=(), acc_outs=(), prologue=None, prologue_pars=()):
    if ta:
        k, m = a.shape
    else:
        m, k = a.shape
    if tb:
        n, k2 = b.shape
    else:
        k2, n = b.shape
    assert k == k2, (name, a.shape, b.shape)
    max_rows = MM_ROWS if epilogue is None else MM_EPILOGUE_ROWS
    tm, tn = _mm_tiles(m, k, n, a.dtype.itemsize, b.dtype.itemsize, ta, max_rows)
    a_spec = pl.BlockSpec((k, tm), lambda j, i: (0, i)) if ta else pl.BlockSpec((tm, k), lambda j, i: (i, 0))
    b_spec = pl.BlockSpec((tn, k), lambda j, i: (j, 0)) if tb else pl.BlockSpec((k, tn), lambda j, i: (0, j))
    o_spec = pl.BlockSpec((tm, tn), lambda j, i: (i, j))
    dims = (((0 if ta else 1,), (1 if tb else 0,)), ((), ()))
    has_res = res is not None
    ins = [a, b] + ([res] if has_res else [])
    specs = [a_spec, b_spec] + ([o_spec] if has_res else [])
    n_mm = len(ins)
    if epilogue is None:
        out_specs, out_shape = [o_spec], [jax.ShapeDtypeStruct((m, n), out_dtype)]
    else:
        assert tn == n, (name, tn, n)
        for r in row_ins:
            arr, cb, cw = r if isinstance(r, tuple) else (r, 0, r.shape[1])
            assert arr.shape[0] == m, (name, arr.shape, m)
            ins.append(arr)
            specs.append(pl.BlockSpec((tm, cw), lambda j, i, cb=cb: (i, cb)))
        for p in par_ins:
            ins.append(p)
            specs.append(pl.BlockSpec(p.shape, lambda j, i: (0, 0)))
        out_specs = [pl.BlockSpec((tm, c), lambda j, i: (i, 0)) for c, _ in outs]
        out_specs += [pl.BlockSpec((r, c), lambda j, i: (0, 0)) for r, c in acc_outs]
        out_shape = [jax.ShapeDtypeStruct((m, c), dt) for c, dt in outs]
        out_shape += [jax.ShapeDtypeStruct((r, c), F32) for r, c in acc_outs]
    n_in = len(ins)
    if prologue is not None:
        assert tn == n and not ta, (name, tn, n, ta)
        for p in prologue_pars:
            ins.append(p)
            specs.append(pl.BlockSpec(p.shape, lambda j, i: (0, 0)))
        out_specs = out_specs + [a_spec]
        out_shape = out_shape + [jax.ShapeDtypeStruct((m, k), BF16)]
    n_pro = len(ins)
    ins += list(deps)
    specs += [ANY_SPEC] * len(deps)

    def body(*refs):
        a_ref, b_ref = refs[0], refs[1]
        out_refs = refs[n_pro + len(deps):]
        if prologue is None:
            lhs = a_ref[...].astype(BF16)
        else:
            lhs = prologue(a_ref[...], *[r[...] for r in refs[n_in:n_pro]]).astype(BF16)
            out_refs[-1][...] = lhs
            out_refs = out_refs[:-1]
        acc = lax.dot_general(lhs, b_ref[...].astype(BF16), dims, preferred_element_type=F32)
        if scale is not None:
            acc = acc * scale
        if has_res:
            acc = acc + refs[2][...].astype(F32)
        if epilogue is None:
            out_refs[0][...] = acc.astype(out_refs[0].dtype)
        else:
            extra = [r[...] for r in refs[n_mm:n_in]]
            _store_results(out_refs, len(outs), epilogue(acc, *extra), pl.program_id(1) == 0)

    out = pl.pallas_call(
        body, name=name, grid=(n // tn, m // tm), in_specs=specs, out_specs=out_specs, out_shape=out_shape,
        compiler_params=_params(2),
    )(*ins)
    return out[0] if epilogue is None and prologue is None else out


def _rowwise(name, fn, rows, row_ins, par_ins, row_outs, acc_outs=(), tm_pref=256, deps=()):
    tm = _pick(rows, tm_pref, 16)
    in_specs, ins = [], []
    for r in row_ins:
        arr, cb, cw = r if isinstance(r, tuple) else (r, 0, r.shape[1])
        assert arr.shape[0] == rows, (name, arr.shape, rows)
        ins.append(arr)
        in_specs.append(pl.BlockSpec((tm, cw), lambda i, cb=cb: (i, cb)))
    for p in par_ins:
        ins.append(p)
        in_specs.append(pl.BlockSpec(p.shape, lambda i: (0, 0)))
    out_specs = [pl.BlockSpec((tm, c), lambda i: (i, 0)) for c, _ in row_outs]
    out_specs += [pl.BlockSpec((r, c), lambda i: (0, 0)) for r, c in acc_outs]
    out_shape = [jax.ShapeDtypeStruct((rows, c), dt) for c, dt in row_outs]
    out_shape += [jax.ShapeDtypeStruct((r, c), F32) for r, c in acc_outs]
    n_in, n_row = len(ins), len(row_outs)
    ins += list(deps)
    in_specs += [ANY_SPEC] * len(deps)

    def body(*refs):
        vals = [r[...] for r in refs[:n_in]]
        _store_results(refs[n_in + len(deps):], n_row, fn(*vals), pl.program_id(0) == 0)

    out = pl.pallas_call(
        body, name=name, grid=(rows // tm,), in_specs=in_specs, out_specs=out_specs, out_shape=out_shape,
        compiler_params=_params(1),
    )(*ins)
    return out


def _inv_rms(x):
    return lax.rsqrt(jnp.mean(x * x, axis=-1, keepdims=True) + NORM_EPS)


def _rms_rows(x, g):
    return x * _inv_rms(x) * g


def _rms_fwd(name, x, g):
    return _rowwise(name, _rms_rows, x.shape[0], [x], [g], [(x.shape[1], BF16)], tm_pref=512)[0]


def _rms_bwd_rows(dn, x, dres, g):
    r = _inv_rms(x)
    xh = x * r
    dg = jnp.sum(dn * xh, axis=0, keepdims=True)
    dxh = dn * g
    dx = r * (dxh - xh * jnp.mean(dxh * xh, axis=-1, keepdims=True)) + dres
    return dx, dx, dg


def _rms_bwd(name, x, g, dn, dres=None):
    d = x.shape[1]
    if dres is None:
        def fn(x, dn, g):
            return (jnp.sum(dn.astype(F32) * (x * _inv_rms(x)), axis=0, keepdims=True),)

        return None, _rowwise(name, fn, x.shape[0], [x, dn], [g], [], [(1, d)])[0]

    def fn(x, dn, dres, g):
        return _rms_bwd_rows(dn.astype(F32), x, dres, g)

    out = _rowwise(name, fn, x.shape[0], [x, dn, dres], [g], [(d, F32), (d, BF16)], [(1, d)])
    return (out[0], out[1]), out[2]


def _mm_rms_bwd(name, dy, w, x, g, dres):
    d = x.shape[1]
    out = _mm(name, dy, w, tb=True, epilogue=_rms_bwd_rows, row_ins=[x, dres], par_ins=[g],
              outs=[(d, F32), (d, BF16)], acc_outs=[(1, d)])
    return (out[0], out[1]), out[2]


def _sigmoid(x):
    return 1.0 / (1.0 + jnp.exp(-x))


def _swiglu_rows(gu, f):
    gt = gu[:, :f]
    return gu, gt * _sigmoid(gt) * gu[:, f:]


def _swiglu_bwd_rows(dact, gt, up):
    gt, up = gt.astype(F32), up.astype(F32)
    sg = _sigmoid(gt)
    return ((dact * up * (sg * (1.0 + gt * (1.0 - sg))), dact * (gt * sg)),)


def _glu_res(name, vg, x, d):
    def fn(val, gate, x):
        return x + val.astype(F32) * _sigmoid(gate.astype(F32))

    return _rowwise(name, fn, x.shape[0], [(vg, 0, d), (vg, 1, d), x], [], [(d, F32)])[0]


def _glu_bwd(name, dres, vg, d):
    def fn(dres, val, gate):
        val, gate = val.astype(F32), gate.astype(F32)
        sg = _sigmoid(gate)
        return ((dres * sg, dres * val * sg * (1.0 - sg)),)

    return _rowwise(name, fn, dres.shape[0], [dres, (vg, 0, d), (vg, 1, d)], [], [(2 * d, BF16)])[0]


def _final_loss(name, x, g, tgt):
    d = x.shape[1]

    def fn(x, tgt, g):
        r = _inv_rms(x)
        xh = x * r
        err = xh * g - tgt
        dy = err * (1.0 / d)
        dxh = dy * g
        dx = r * (dxh - xh * jnp.mean(dxh * xh, axis=-1, keepdims=True))
        return dx, dx, jnp.sum(err * err, axis=0, keepdims=True), jnp.sum(dy * xh, axis=0, keepdims=True)

    dx, dx16, err2, dg = _rowwise(name, fn, x.shape[0], [x, tgt], [g], [(d, F32), (d, BF16)], [(1, d), (1, d)])
    return (dx, dx16), err2, dg


def _shift_down(u, k):
    rows = lax.broadcasted_iota(jnp.int32, u.shape, 0)
    return jnp.where(rows >= k, pltpu.roll(u, k, 0), 0.0)


def _shift_up(u, k):
    n = u.shape[0]
    rows = lax.broadcasted_iota(jnp.int32, u.shape, 0)
    return jnp.where(rows < n - k, pltpu.roll(u, n - k, 0), 0.0)


def _conv_specs(seq, cw, n_cb, swap):
    def at(off):
        if swap:
            return pl.BlockSpec((seq, cw), lambda j, b: (b, off * n_cb + j))
        return pl.BlockSpec((seq, cw), lambda b, j: (b, off * n_cb + j))

    return at


def _conv_fwd(name, cbv, w, n_seq, seq):
    d = w.shape[1]
    cw = _pick(d, 256, V7X_LANES)
    n_cb = d // cw
    at = _conv_specs(seq, cw, n_cb, swap=False)

    def body(c_ref, b_ref, v_ref, w_ref, z_ref):
        u = c_ref[...].astype(F32) * v_ref[...].astype(F32)
        cv = w_ref[0:1, :] * _shift_down(u, 2) + w_ref[1:2, :] * _shift_down(u, 1) + w_ref[2:3, :] * u
        z_ref[...] = (b_ref[...].astype(F32) * cv).astype(z_ref.dtype)

    return pl.pallas_call(
        body, name=name, grid=(n_seq, n_cb),
        in_specs=[at(0), at(1), at(2), pl.BlockSpec((CONV_WIDTH, cw), lambda b, j: (0, j))],
        out_specs=at(0), out_shape=jax.ShapeDtypeStruct((n_seq * seq, d), BF16), compiler_params=_params(2),
    )(cbv, cbv, cbv, w)


def _conv_bwd(name, dz, cbv, w, n_seq, seq):
    d = w.shape[1]
    cw = _pick(d, 256, V7X_LANES)
    n_cb = d // cw
    at = _conv_specs(seq, cw, n_cb, swap=True)

    def body(dz_ref, c_ref, b_ref, v_ref, w_ref, dc_ref, db_ref, dv_ref, dw_ref):
        c, b, v = c_ref[...].astype(F32), b_ref[...].astype(F32), v_ref[...].astype(F32)
        dz = dz_ref[...].astype(F32)
        w0, w1, w2 = w_ref[0:1, :], w_ref[1:2, :], w_ref[2:3, :]
        u = c * v
        u1, u2 = _shift_down(u, 1), _shift_down(u, 2)
        cv = w0 * u2 + w1 * u1 + w2 * u
        db_ref[...] = (dz * cv).astype(db_ref.dtype)
        dcv = dz * b
        du = w2 * dcv + w1 * _shift_up(dcv, 1) + w0 * _shift_up(dcv, 2)
        dc_ref[...] = (du * v).astype(dc_ref.dtype)
        dv_ref[...] = (du * c).astype(dv_ref.dtype)

        @pl.when(pl.program_id(1) == 0)
        def _():
            dw_ref[...] = jnp.zeros_like(dw_ref)

        dw_ref[0:1, :] += jnp.sum(dcv * u2, axis=0, keepdims=True)
        dw_ref[1:2, :] += jnp.sum(dcv * u1, axis=0, keepdims=True)
        dw_ref[2:3, :] += jnp.sum(dcv * u, axis=0, keepdims=True)

    act = jax.ShapeDtypeStruct((n_seq * seq, d), BF16)
    return pl.pallas_call(
        body, name=name, grid=(n_cb, n_seq),
        in_specs=[at(0), at(0), at(1), at(2), pl.BlockSpec((CONV_WIDTH, cw), lambda j, b: (0, j))],
        out_specs=[at(0), at(0), at(0), pl.BlockSpec((CONV_WIDTH, cw), lambda j, b: (0, j))],
        out_shape=[act, act, act, jax.ShapeDtypeStruct((CONV_WIDTH, d), F32)], compiler_params=_params(2),
    )(dz, cbv, cbv, cbv, w)


def _s5_discretize(a_re, a_im, log_dt, b_re, b_im):
    lam_re = jnp.minimum(a_re, EIG_CLIP)
    lam_im = a_im
    dt = jnp.exp(log_dt)[:, None]
    mag = jnp.exp(lam_re * dt)
    abar_re = mag * jnp.cos(lam_im * dt)
    abar_im = mag * jnp.sin(lam_im * dt)
    den = lam_re * lam_re + lam_im * lam_im
    num_re = abar_re - 1.0
    num_im = abar_im
    coef_re = (num_re * lam_re + num_im * lam_im) / den
    coef_im = (num_im * lam_re - num_re * lam_im) / den
    bbar_re = coef_re[..., None] * b_re - coef_im[..., None] * b_im
    bbar_im = coef_re[..., None] * b_im + coef_im[..., None] * b_re
    return abar_re, abar_im, bbar_re, bbar_im


def _block_diag_in(bbar, gb):
    g, p, h = bbar.shape
    t = jnp.transpose(bbar.reshape(g // gb, gb, p, h), (0, 1, 3, 2))
    return jnp.einsum("cghp,gk->cghkp", t, jnp.eye(gb, dtype=bbar.dtype)).reshape(g // gb, gb * h, gb * p)


def _block_diag_in_t(blk, gb, p, h):
    nb = blk.shape[0]
    t = jnp.einsum("cghkp,gk->cghp", blk.reshape(nb, gb, h, gb, p), jnp.eye(gb, dtype=blk.dtype))
    return jnp.transpose(t, (0, 1, 3, 2)).reshape(nb * gb, p, h)


def _block_diag_out(c, gb):
    g, h, p = c.shape
    t = jnp.transpose(c.reshape(g // gb, gb, h, p), (0, 1, 3, 2))
    return jnp.einsum("cgph,gk->cgpkh", t, jnp.eye(gb, dtype=c.dtype)).reshape(g // gb, gb * p, gb * h)


def _block_diag_out_t(blk, gb, p, h):
    nb = blk.shape[0]
    t = jnp.einsum("cgpkh,gk->cgph", blk.reshape(nb, gb, p, gb, h), jnp.eye(gb, dtype=blk.dtype))
    return jnp.transpose(t, (0, 1, 3, 2)).reshape(nb * gb, h, p)


def _gelu(y):
    return 0.5 * y * (1.0 + jnp.tanh(GELU_C * (y + GELU_A * y * y * y)))


def _gelu_grad(y):
    th = jnp.tanh(GELU_C * (y + GELU_A * y * y * y))
    return 0.5 * (1.0 + th) + 0.5 * y * (1.0 - th * th) * GELU_C * (1.0 + 3.0 * GELU_A * y * y)


def _dot(a, b, ca, cb):
    return lax.dot_general(a.astype(BF16), b.astype(BF16), (((ca,), (cb,)), ((), ())), preferred_element_type=F32)


def _s5_specs(seq, ch, sb):
    act = pl.BlockSpec((seq, ch), lambda j, b: (b, j))
    state = pl.BlockSpec((seq, sb), lambda j, b: (b, j))
    w_in = pl.BlockSpec((None, ch, sb), lambda j, b: (j, 0, 0))
    w_out = pl.BlockSpec((None, sb, ch), lambda j, b: (j, 0, 0))
    lane_s = pl.BlockSpec((1, sb), lambda j, b: (0, j))
    lane_c = pl.BlockSpec((1, ch), lambda j, b: (0, j))
    return act, state, w_in, w_out, lane_s, lane_c


def _s5_fwd(name, h, bin_re, bin_im, cout_re, cout_im, abar_re, abar_im, dskip, n_seq, seq):
    t, d = h.shape
    nb, ch, sb = bin_re.shape
    act, state, w_in, w_out, lane_s, lane_c = _s5_specs(seq, ch, sb)

    def body(h_ref, bre_ref, bim_ref, cre_ref, cim_ref, ar_ref, ai_ref, d_ref, sre_ref, sim_ref, y_ref, z_ref):
        u = h_ref[...]
        sre_ref[...] = _dot(u, bre_ref[...], 1, 0)
        sim_ref[...] = _dot(u, bim_ref[...], 1, 0)
        ar, ai = ar_ref[...], ai_ref[...]

        def step(i, carry):
            sr, si = carry
            row = pl.ds(i, 1)
            nr = ar * sr - ai * si + sre_ref[row, :]
            ni = ar * si + ai * sr + sim_ref[row, :]
            sre_ref[row, :] = nr
            sim_ref[row, :] = ni
            return nr, ni

        zero = jnp.zeros((1, sb), F32)
        lax.fori_loop(0, seq, step, (zero, zero), unroll=8)
        y = _dot(sre_ref[...], cre_ref[...], 1, 0) - _dot(sim_ref[...], cim_ref[...], 1, 0)
        y = y + d_ref[...] * u.astype(F32)
        y_ref[...] = y
        z_ref[...] = _gelu(y).astype(z_ref.dtype)

    return pl.pallas_call(
        body, name=name, grid=(nb, n_seq),
        in_specs=[act, w_in, w_in, w_out, w_out, lane_s, lane_s, lane_c],
        out_specs=[state, state, act, act],
        out_shape=[jax.ShapeDtypeStruct((t, nb * sb), F32), jax.ShapeDtypeStruct((t, nb * sb), F32),
                   jax.ShapeDtypeStruct((t, d), F32), jax.ShapeDtypeStruct((t, d), BF16)],
        compiler_params=_params(2),
    )(h, bin_re, bin_im, cout_re, cout_im, abar_re, abar_im, dskip)


def _s5_bwd(name, dz, ypre, h, s_re, s_im, bin_re, bin_im, cout_re, cout_im, abar_re, abar_im, dskip, n_seq, seq):
    t, d = h.shape
    nb, ch, sb = bin_re.shape
    act, state, w_in, w_out, lane_s, lane_c = _s5_specs(seq, ch, sb)

    def body(dz_ref, y_ref, h_ref, sre_ref, sim_ref, bre_ref, bim_ref, cre_ref, cim_ref, ar_ref, ai_ref, d_ref,
             dh_ref, dbre_ref, dbim_ref, dcre_ref, dcim_ref, dar_ref, dai_ref, dd_ref, gre, gim):
        first = pl.program_id(1) == 0
        u = h_ref[...].astype(F32)
        dy = dz_ref[...].astype(F32) * _gelu_grad(y_ref[...])
        gre[...] = _dot(dy, cre_ref[...], 1, 1)
        gim[...] = -_dot(dy, cim_ref[...], 1, 1)
        ar, ai = ar_ref[...], ai_ref[...]

        def step(i, carry):
            gr, gi = carry
            row = pl.ds(seq - 1 - i, 1)
            nr = gre[row, :] + ar * gr + ai * gi
            ni = gim[row, :] - ai * gr + ar * gi
            gre[row, :] = nr
            gim[row, :] = ni
            return nr, ni

        zero = jnp.zeros((1, sb), F32)
        lax.fori_loop(0, seq, step, (zero, zero), unroll=8)

        g_re, g_im = gre[...], gim[...]
        s_re, s_im = sre_ref[...], sim_ref[...]
        p_re, p_im = _shift_down(s_re, 1), _shift_down(s_im, 1)
        dar = jnp.sum(g_re * p_re + g_im * p_im, axis=0, keepdims=True)
        dai = jnp.sum(g_im * p_re - g_re * p_im, axis=0, keepdims=True)
        dbre = _dot(u, g_re, 0, 0)
        dbim = _dot(u, g_im, 0, 0)
        dcre = _dot(s_re, dy, 0, 0)
        dcim = -_dot(s_im, dy, 0, 0)
        ddd = jnp.sum(dy * u, axis=0, keepdims=True)
        dh_ref[...] = _dot(g_re, bre_ref[...], 1, 1) + _dot(g_im, bim_ref[...], 1, 1) + d_ref[...] * dy

        @pl.when(first)
        def _():
            dar_ref[...] = dar
            dai_ref[...] = dai
            dbre_ref[...] = dbre
            dbim_ref[...] = dbim
            dcre_ref[...] = dcre
            dcim_ref[...] = dcim
            dd_ref[...] = ddd

        @pl.when(jnp.logical_not(first))
        def _():
            dar_ref[...] += dar
            dai_ref[...] += dai
            dbre_ref[...] += dbre
            dbim_ref[...] += dbim
            dcre_ref[...] += dcre
            dcim_ref[...] += dcim
            dd_ref[...] += ddd

    return pl.pallas_call(
        body, name=name, grid=(nb, n_seq),
        in_specs=[act, act, act, state, state, w_in, w_in, w_out, w_out, lane_s, lane_s, lane_c],
        out_specs=[act, w_in, w_in, w_out, w_out, lane_s, lane_s, lane_c],
        out_shape=[jax.ShapeDtypeStruct((t, d), F32),
                   jax.ShapeDtypeStruct((nb, ch, sb), F32), jax.ShapeDtypeStruct((nb, ch, sb), F32),
                   jax.ShapeDtypeStruct((nb, sb, ch), F32), jax.ShapeDtypeStruct((nb, sb, ch), F32),
                   jax.ShapeDtypeStruct((1, nb * sb), F32), jax.ShapeDtypeStruct((1, nb * sb), F32),
                   jax.ShapeDtypeStruct((1, d), F32)],
        scratch_shapes=[pltpu.VMEM((seq, sb), F32), pltpu.VMEM((seq, sb), F32)],
        compiler_params=_params(2),
    )(dz, ypre, h, s_re, s_im, bin_re, bin_im, cout_re, cout_im, abar_re, abar_im, dskip)


ATTN_QUERY_ROWS = 1024


def _softmax_rows(q, k, scale):
    s = _dot(q, k, 1, 1) * scale
    e = jnp.exp(s - jnp.max(s, axis=-1, keepdims=True))
    return e / jnp.sum(e, axis=-1, keepdims=True)


def _attn_fwd(name, q, kv, n_seq, seq, mlen, heads):
    t, d = q.shape
    hd = d // heads
    tq = _pick(seq, ATTN_QUERY_ROWS, 16)
    nq = seq // tq
    scale = hd ** -0.5
    q_spec = pl.BlockSpec((tq, hd), lambda b, h, i: (b * nq + i, h))

    def body(q_ref, k_ref, v_ref, o_ref):
        p = _softmax_rows(q_ref[...], k_ref[...], scale)
        o_ref[...] = _dot(p, v_ref[...], 1, 0).astype(o_ref.dtype)

    return pl.pallas_call(
        body, name=name, grid=(n_seq, heads, nq),
        in_specs=[q_spec, pl.BlockSpec((mlen, hd), lambda b, h, i: (b, h)),
                  pl.BlockSpec((mlen, hd), lambda b, h, i: (b, heads + h))],
        out_specs=q_spec, out_shape=jax.ShapeDtypeStruct((t, d), BF16), compiler_params=_params(3),
    )(q, kv, kv)


def _attn_bwd(name, q, kv, do, n_seq, seq, mlen, heads):
    t, d = q.shape
    hd = d // heads
    tq = _pick(seq, ATTN_QUERY_ROWS, 16)
    nq = seq // tq
    scale = hd ** -0.5
    q_spec = pl.BlockSpec((tq, hd), lambda b, h, i: (b * nq + i, h))
    k_spec = pl.BlockSpec((mlen, hd), lambda b, h, i: (b, h))

    def body(q_ref, k_ref, v_ref, do_ref, dq_ref, dk_ref, dv_ref):
        q, k, v, do = q_ref[...], k_ref[...], v_ref[...], do_ref[...]
        p = _softmax_rows(q, k, scale)
        dp = _dot(do, v, 1, 1)
        ds = p * (dp - jnp.sum(dp * p, axis=-1, keepdims=True)) * scale
        dq_ref[...] = _dot(ds, k, 1, 0).astype(dq_ref.dtype)

        @pl.when(pl.program_id(2) == 0)
        def _():
            dk_ref[...] = jnp.zeros_like(dk_ref)
            dv_ref[...] = jnp.zeros_like(dv_ref)

        dk_ref[...] += _dot(ds, q, 0, 0)
        dv_ref[...] += _dot(p, do, 0, 0)

    return pl.pallas_call(
        body, name=name, grid=(n_seq, heads, nq),
        in_specs=[q_spec, k_spec, pl.BlockSpec((mlen, hd), lambda b, h, i: (b, heads + h)), q_spec],
        out_specs=[q_spec, k_spec, k_spec],
        out_shape=[jax.ShapeDtypeStruct((t, d), BF16), jax.ShapeDtypeStruct((n_seq * mlen, d), F32),
                   jax.ShapeDtypeStruct((n_seq * mlen, d), F32)],
        compiler_params=_params(3),
    )(q, kv, kv, do)


ADAMW_BLOCK_ELEMS = 128 * 1024


def _adamw(name, parts, w, m, v, first_layer=0, earlier=None):
    n_layers = len(parts)
    _, r, c = parts[0].shape
    assert w.shape[0] % r == 0 and w.shape[1] == c and first_layer + n_layers <= w.shape[0] // r, (name, w.shape)
    tr = _pick(r, max(16, ADAMW_BLOCK_ELEMS // c // 16 * 16), 8)
    nt = r // tr
    spec = pl.BlockSpec((tr, c), lambda l, i: ((first_layer + l) * nt + i, 0))
    c1 = 1.0 - ADAM_B1 ** ADAM_STEP
    c2 = 1.0 - ADAM_B2 ** ADAM_STEP

    def parts_spec(q):
        return pl.BlockSpec((N_DEV, tr, c), lambda l, i: (0, jnp.where(l == q, i, jnp.where(l > q, nt - 1, 0)), 0))

    earlier = list(earlier or ())

    def body(*refs):
        p_refs = refs[:n_layers]
        w_ref, m_ref, v_ref = refs[n_layers:n_layers + 3]
        g_ref, d_ref, nm_ref, nv_ref = refs[n_layers + 3 + len(earlier):]

        def update(p_ref):
            g = p_ref[0].astype(F32)
            for k in range(1, N_DEV):
                g = g + p_ref[k].astype(F32)
            nm = ADAM_B1 * m_ref[...] + (1.0 - ADAM_B1) * g
            nv = ADAM_B2 * v_ref[...] + (1.0 - ADAM_B2) * (g * g)
            g_ref[...] = g
            nm_ref[...] = nm
            nv_ref[...] = nv
            d_ref[...] = -ADAM_LR * ((nm / c1) / (jnp.sqrt(nv / c2) + ADAM_EPS) + ADAM_WD * w_ref[...])

        for q in range(n_layers):
            pl.when(pl.program_id(0) == q)(lambda q=q: update(p_refs[q]))

    out = jax.ShapeDtypeStruct(w.shape, F32)
    return pl.pallas_call(
        body, name=name, grid=(n_layers, nt),
        in_specs=[parts_spec(q) for q in range(n_layers)] + [spec] * 3 + [ANY_SPEC] * len(earlier),
        out_specs=[spec] * 4, out_shape=[out] * 4, compiler_params=_params(2),
        input_output_aliases={n_layers + 3 + q: q for q in range(len(earlier))},
    )(*parts, w, m, v, *earlier)


def _place():
    x, y, c = lax.axis_index("x"), lax.axis_index("y"), lax.axis_index("c")
    return x, y, c


def _index(px, py, pc):
    return 4 * px + 2 * py + pc


def _all_gather(name, shards):
    n = len(shards)

    def body(*refs):
        in_refs, out_refs = refs[:n], refs[n:2 * n]
        send_sems, recv_sems, local_sems = refs[2 * n:]
        x, y, c = _place()
        me, sibling = (x, y, c), (x, y, 1 - c)
        chips = [(1 - x, y), (x, 1 - y), (1 - x, 1 - y)]

        def slot(k, block):
            return out_refs[k].at[_index(*block)]

        def copy(k, j, block, to, src=None):
            return pltpu.make_async_remote_copy(
                src_ref=slot(k, block) if src is None else src, dst_ref=slot(k, block),
                send_sem=send_sems.at[7 * k + j], recv_sem=recv_sems.at[7 * k + j], device_id=to, device_id_type=MESH)

        mine = [pltpu.make_async_copy(in_refs[k], slot(k, me), local_sems.at[k]) for k in range(n)]
        for cp in mine:
            cp.start()
        first = []
        for k in range(n):
            first.append(copy(k, 0, me, sibling, src=in_refs[k]))
            first += [copy(k, 1 + j, me, (*chip, c), src=in_refs[k]) for j, chip in enumerate(chips)]
        for cp in first:
            cp.start()
        passed = []
        for j, chip in enumerate(chips):
            for k in range(n):
                copy(k, 1 + j, (*chip, c), me).wait_recv()
                cp = copy(k, 4 + j, (*chip, c), sibling)
                cp.start()
                passed.append(cp)
        for k in range(n):
            copy(k, 0, sibling, me).wait_recv()
            for j, chip in enumerate(chips):
                copy(k, 4 + j, (*chip, 1 - c), me).wait_recv()
        for cp in first + passed:
            cp.wait_send()
        for cp in mine:
            cp.wait()

    return pl.pallas_call(
        body, name=name, in_specs=[HBM_SPEC] * n, out_specs=[HBM_SPEC] * n,
        out_shape=[jax.ShapeDtypeStruct((N_DEV,) + s.shape, s.dtype) for s in shards],
        scratch_shapes=[pltpu.SemaphoreType.DMA((7 * n,)), pltpu.SemaphoreType.DMA((7 * n,)),
                        pltpu.SemaphoreType.DMA((n,))],
    )(*shards)


WHOLE, BLOCK, COLUMNS = "whole", "block", "columns"


def _slot(ref, index, mode):
    if mode == WHOLE:
        return ref
    if mode == BLOCK:
        return ref.at[index]
    width = ref.shape[1] // N_DEV
    return ref.at[:, pl.ds(pl.multiple_of(index * width, width), width)]


DIRECT = tuple(range(1, N_DEV))
CHIPS = (1, 4, 2, 6)
FORWARD = "forward"


def _copies(plan, x, y, c):
    def xor(r, flip_core=False):
        rx, ry, rc = (r >> 2) & 1, (r >> 1) & 1, (r & 1) ^ int(flip_core)
        return (1 - x if rx else x, 1 - y if ry else y, 1 - c if rc else c)

    me = _index(x, y, c)
    if plan == FORWARD:
        return [(xor(1), _index(*xor(r)), _index(*xor(r)), _index(*xor(r, True))) for r in (4, 2, 6)]
    return [(xor(r), _index(*xor(r)), me, _index(*xor(r))) for r in plan]


def _exchange_start(name, srcs, lands, src_modes, land_modes, deps=(), plan=DIRECT):
    n, n_src = len(lands), len(srcs)
    n_copies = 3 if plan == FORWARD else len(plan)

    def body(*refs):
        land_refs = refs[n_src:n_src + n]
        src_refs = refs[:n_src] if n_src else land_refs
        send_sems, recv_sems = refs[n_src + n + len(deps)], refs[n_src + n + len(deps) + 1]
        token = refs[-1]
        x, y, c = _place()
        for k in range(n):
            for j, (peer, src_index, dst_index, _) in enumerate(_copies(plan, x, y, c)):
                pltpu.make_async_remote_copy(
                    src_ref=_slot(src_refs[k], src_index, src_modes[k]),
                    dst_ref=_slot(land_refs[k], dst_index, land_modes[k]), send_sem=send_sems.at[n_copies * k + j],
                    recv_sem=recv_sems.at[n_copies * k + j], device_id=peer, device_id_type=MESH).start()
        token[...] = jnp.zeros_like(token)

    arrays = list(srcs) + list(lands)
    thru = [pltpu.HBM(a.shape, a.dtype) for a in arrays]
    out = pl.pallas_call(
        body, name=name,
        out_shape=(pltpu.SemaphoreType.DMA((n_copies * n,)), pltpu.SemaphoreType.DMA((n_copies * n,)), *thru,
                   jax.ShapeDtypeStruct((8, V7X_LANES), F32)),
        in_specs=[HBM_SPEC] * len(arrays) + [ANY_SPEC] * len(deps),
        out_specs=(SEM_SPEC, SEM_SPEC, *([HBM_SPEC] * len(arrays)), pl.BlockSpec(memory_space=pltpu.VMEM)),
        input_output_aliases={k: 2 + k for k in range(len(arrays))},
        compiler_params=pltpu.CompilerParams(has_side_effects=pltpu.SideEffectType.DATAFLOW_SIDE_EFFECTING),
    )(*[pltpu.with_memory_space_constraint(a, pltpu.HBM) for a in arrays], *deps)
    return out[0], out[1], list(out[2:2 + n_src]), list(out[2 + n_src:2 + n_src + n]), out[-1]


def _exchange_wait(name, send_sems, recv_sems, srcs, lands, src_modes, land_modes, after, plan=DIRECT):
    n, n_src = len(lands), len(srcs)
    n_copies = 3 if plan == FORWARD else len(plan)

    def body(*refs):
        land_refs = refs[n_src:n_src + n]
        src_refs = refs[:n_src] if n_src else land_refs
        send_sems, recv_sems = refs[n_src + n], refs[n_src + n + 1]
        x, y, c = _place()
        for k in range(n):
            for j, (peer, src_index, _, arrival_index) in enumerate(_copies(plan, x, y, c)):
                cp = pltpu.make_async_remote_copy(
                    src_ref=_slot(src_refs[k], src_index, src_modes[k]),
                    dst_ref=_slot(land_refs[k], arrival_index, land_modes[k]), send_sem=send_sems.at[n_copies * k + j],
                    recv_sem=recv_sems.at[n_copies * k + j], device_id=peer, device_id_type=MESH)
                cp.wait_send()
                cp.wait_recv()

    arrays = list(srcs) + list(lands)
    thru = [pltpu.HBM(a.shape, a.dtype) for a in arrays]
    out = pl.pallas_call(
        body, name=name, out_shape=tuple(thru),
        in_specs=[HBM_SPEC] * len(arrays) + [SEM_SPEC, SEM_SPEC, ANY_SPEC], out_specs=tuple([HBM_SPEC] * len(arrays)),
        input_output_aliases={k: k for k in range(len(arrays))},
        compiler_params=pltpu.CompilerParams(has_side_effects=pltpu.SideEffectType.DATAFLOW_SIDE_EFFECTING),
    )(*arrays, send_sems, recv_sems, after)
    return list(out[n_src:])


def _landing(shard, me, mode=BLOCK):
    if mode == COLUMNS:
        k, n = shard.shape
        return lax.dynamic_update_slice(lax.empty((k, N_DEV * n), shard.dtype), shard, (0, me * n))
    zone = lax.empty((N_DEV,) + shard.shape, shard.dtype)
    return lax.dynamic_update_slice(zone, shard[None], (me,) + (0,) * shard.ndim)


def _cols_whole(w):
    return jnp.transpose(w, (1, 0, 2)).reshape(w.shape[1], N_DEV * w.shape[2])


def _rows_whole(w):
    return w.reshape(N_DEV * w.shape[1], w.shape[2])


def _cols_parts(dw):
    k, n8 = dw.shape
    return jnp.transpose(dw.reshape(k, N_DEV, n8 // N_DEV), (1, 0, 2))


def _rows_parts(dw):
    r8, c = dw.shape
    return dw.reshape(N_DEV, r8 // N_DEV, c)


def _pack_rows(arrays):
    rows = []
    for a in arrays:
        flat = a.reshape(-1).astype(F32)
        flat = jnp.pad(flat, [(0, (-flat.shape[0]) % PACK_TILE)])
        rows.append(flat.reshape(-1, V7X_LANES))
    return jnp.concatenate(rows, axis=0)


def _unpack_rows(packed, shapes):
    out, row = [], 0
    for s in shapes:
        size = math.prod(s)
        n_rows = -(-size // PACK_TILE) * 8
        out.append(packed[row:row + n_rows].reshape(-1)[:size].reshape(s))
        row += n_rows
    return out


def _merge2d(a):
    return a.reshape(-1, a.shape[-1])


def _ffn_fwd(tag, x, g, w_up, w_down, deps=()):
    f = w_down.shape[0]
    gu, act, n = _mm(f"{tag}_up", x, w_up, prologue=_rms_rows, prologue_pars=[g], deps=deps,
                     epilogue=lambda acc: _swiglu_rows(acc, f), outs=[(2 * f, BF16), (f, BF16)])
    out = _mm(f"{tag}_down", act, w_down, res=x, scale=0.5, out_dtype=F32)
    return out, (x, n, gu, act)


def _ffn_bwd(tag, dres, saved, g, w_up, w_down, deps=()):
    x, n, gu, act = saved
    dres32, dres16 = dres
    f = w_down.shape[0]
    dgu = _mm(f"{tag}_down_dx", dres16, w_down, tb=True, scale=0.5, deps=deps, epilogue=_swiglu_bwd_rows,
              row_ins=[(gu, 0, f), (gu, 1, f)], outs=[(2 * f, BF16)])[0]
    d_down = _mm(f"{tag}_down_dw", act, dres16, ta=True, scale=0.5)
    d_up = _mm(f"{tag}_up_dw", n, dgu, ta=True)
    dx, dg = _mm_rms_bwd(f"{tag}_up_dx", dgu, w_up, x, g, dres32)
    return dx, dg, d_up, d_down


def _conv_mixer_fwd(tag, x, g, w_in, w_conv, w_out, n_seq, seq):
    cbv, h = _mm(f"{tag}_in", x, w_in, prologue=_rms_rows, prologue_pars=[g])
    z = _conv_fwd(f"{tag}_conv", cbv, w_conv, n_seq, seq)
    out = _mm(f"{tag}_out", z, w_out, res=x, out_dtype=F32)
    return out, (x, h, cbv, z)


def _conv_mixer_bwd(tag, dres, saved, g, w_in, w_conv, w_out, n_seq, seq):
    x, h, cbv, z = saved
    dres32, dres16 = dres
    dz = _mm(f"{tag}_out_dx", dres16, w_out, tb=True)
    d_out = _mm(f"{tag}_out_dw", z, dres16, ta=True)
    dc, db, dv, d_conv = _conv_bwd(f"{tag}_conv_bwd", dz, cbv, w_conv, n_seq, seq)
    dcbv = jnp.concatenate([dc, db, dv], axis=1)
    d_in = _mm(f"{tag}_in_dw", h, dcbv, ta=True)
    dx, dg = _mm_rms_bwd(f"{tag}_in_dx", dcbv, w_in, x, g, dres32)
    return dx, dg, d_in, d_conv, d_out


def _s5_mixer_fwd(tag, x, g, ssm, dskip, w_glu, n_seq, seq):
    a_re, a_im, log_dt, b_re, b_im, c_re, c_im = ssm
    groups, p, hh = b_re.shape
    gb = S5_CHANNELS // hh
    disc, disc_vjp = jax.vjp(_s5_discretize, a_re, a_im, log_dt, b_re, b_im)
    abar_re, abar_im, bbar_re, bbar_im = disc
    mats = (_block_diag_in(bbar_re, gb).astype(BF16), _block_diag_in(bbar_im, gb).astype(BF16),
            _block_diag_out(c_re, gb).astype(BF16), _block_diag_out(c_im, gb).astype(BF16),
            abar_re.reshape(1, groups * p), abar_im.reshape(1, groups * p), dskip)
    d = x.shape[1]
    h = _rms_fwd(f"{tag}_norm", x, g)
    s_re, s_im, ypre, z = _s5_fwd(f"{tag}_scan", h, *mats, n_seq, seq)
    vg = _mm(f"{tag}_glu", z, w_glu)
    out = _glu_res(f"{tag}_glu_act", vg, x, d)
    return out, (x, h, s_re, s_im, ypre, z, vg, mats, disc_vjp, (groups, p, hh, gb))


def _s5_mixer_bwd(tag, dres, saved, g, w_glu, n_seq, seq):
    x, h, s_re, s_im, ypre, z, vg, mats, disc_vjp, (groups, p, hh, gb) = saved
    d = x.shape[1]
    dres32, _ = dres
    dvg = _glu_bwd(f"{tag}_glu_act_bwd", dres32, vg, d)
    d_glu = _mm(f"{tag}_glu_dw", z, dvg, ta=True)
    dz = _mm(f"{tag}_glu_dx", dvg, w_glu, tb=True)
    dh, dbin_re, dbin_im, dcout_re, dcout_im, dabar_re, dabar_im, d_skip = _s5_bwd(
        f"{tag}_scan_bwd", dz, ypre, h, s_re, s_im, *mats, n_seq, seq)
    d_are, d_aim, d_logdt, d_bre, d_bim = disc_vjp((
        dabar_re.reshape(groups, p), dabar_im.reshape(groups, p),
        _block_diag_in_t(dbin_re, gb, p, hh), _block_diag_in_t(dbin_im, gb, p, hh)))
    d_cre = _block_diag_out_t(dcout_re, gb, p, hh)
    d_cim = _block_diag_out_t(dcout_im, gb, p, hh)
    dx, dg = _rms_bwd(f"{tag}_norm_bwd", x, g, dh, dres32)
    return dx, dg, (d_are, d_aim, d_logdt, d_bre, d_bim, d_cre, d_cim), d_skip, d_glu


def _xattn_fwd(tag, x, mem, g_q, g_mem, w_q, w_kv, w_o, n_seq, seq, mlen, heads):
    q, n = _mm(f"{tag}_q", x, w_q, prologue=_rms_rows, prologue_pars=[g_q])
    mem_n = _rms_fwd(f"{tag}_mem_norm", mem, g_mem)
    kv = _mm(f"{tag}_kv", mem_n, w_kv)
    o = _attn_fwd(f"{tag}_attn", q, kv, n_seq, seq, mlen, heads)
    out = _mm(f"{tag}_o", o, w_o, res=x, out_dtype=F32)
    return out, (x, n, q, mem_n, kv, o)


def _xattn_bwd(tag, dres, saved, mem, g_q, g_mem, w_q, w_kv, w_o, n_seq, seq, mlen, heads):
    x, n, q, mem_n, kv, o = saved
    dres32, dres16 = dres
    do = _mm(f"{tag}_o_dx", dres16, w_o, tb=True)
    d_o = _mm(f"{tag}_o_dw", o, dres16, ta=True)
    dq, dk, dv = _attn_bwd(f"{tag}_attn_bwd", q, kv, do, n_seq, seq, mlen, heads)
    dkv = jnp.concatenate([dk, dv], axis=1)
    d_q = _mm(f"{tag}_q_dw", n, dq, ta=True)
    d_kv = _mm(f"{tag}_kv_dw", mem_n, dkv, ta=True)
    dmem_n = _mm(f"{tag}_kv_dx", dkv, w_kv, tb=True)
    _, dg_mem = _rms_bwd(f"{tag}_mem_norm_bwd", mem, g_mem, dmem_n)
    dx, dg_q = _mm_rms_bwd(f"{tag}_q_dx", dq, w_q, x, g_q, dres32)
    return dx, dg_q, dg_mem, d_q, d_kv, d_o


WEIGHT_NAMES = ("norm_g", "final_g", "ffn1_up", "ffn1_down", "ffn2_up", "ffn2_down", "conv_w_in", "conv_w",
                "conv_w_out", "ssm_a_re", "ssm_a_im", "ssm_log_dt", "ssm_b_re", "ssm_b_im", "ssm_c_re", "ssm_c_im",
                "ssm_d", "ssm_w_glu", "xa_w_q", "xa_w_kv", "xa_w_o")
MATRICES = ("ffn1_up", "ffn1_down", "ffn2_up", "ffn2_down", "conv_w_in", "conv_w_out", "ssm_w_glu", "xa_w_q",
            "xa_w_kv", "xa_w_o")
COLUMN_SHARDED = ("ffn1_up", "ffn2_up", "conv_w_in", "ssm_w_glu", "xa_w_kv")
SMALL_SHARDED = ("norm_g", "conv_w", "ssm_d")
REPLICATED = ("ssm_a_re", "ssm_a_im", "ssm_log_dt", "ssm_b_re", "ssm_b_im", "ssm_c_re", "ssm_c_im", "final_g")


def kernel(x, mem, norm_g, final_g, ffn1_up, ffn1_down, ffn2_up, ffn2_down, conv_w_in, conv_w, conv_w_out, ssm_a_re, ssm_a_im, ssm_log_dt, ssm_b_re, ssm_b_im, ssm_c_re, ssm_c_im, ssm_d, ssm_w_glu, xa_w_q, xa_w_kv, xa_w_o, loss_target, m_norm_g, m_final_g, m_ffn1_up, m_ffn1_down, m_ffn2_up, m_ffn2_down, m_conv_w_in, m_conv_w, m_conv_w_out, m_ssm_a_re, m_ssm_a_im, m_ssm_log_dt, m_ssm_b_re, m_ssm_b_im, m_ssm_c_re, m_ssm_c_im, m_ssm_d, m_ssm_w_glu, m_xa_w_q, m_xa_w_kv, m_xa_w_o, v_norm_g, v_final_g, v_ffn1_up, v_ffn1_down, v_ffn2_up, v_ffn2_down, v_conv_w_in, v_conv_w, v_conv_w_out, v_ssm_a_re, v_ssm_a_im, v_ssm_log_dt, v_ssm_b_re, v_ssm_b_im, v_ssm_c_re, v_ssm_c_im, v_ssm_d, v_ssm_w_glu, v_xa_w_q, v_xa_w_kv, v_xa_w_o):
    w = dict(norm_g=norm_g, final_g=final_g, ffn1_up=ffn1_up, ffn1_down=ffn1_down, ffn2_up=ffn2_up,
             ffn2_down=ffn2_down, conv_w_in=conv_w_in, conv_w=conv_w, conv_w_out=conv_w_out, ssm_a_re=ssm_a_re,
             ssm_a_im=ssm_a_im, ssm_log_dt=ssm_log_dt, ssm_b_re=ssm_b_re, ssm_b_im=ssm_b_im, ssm_c_re=ssm_c_re,
             ssm_c_im=ssm_c_im, ssm_d=ssm_d, ssm_w_glu=ssm_w_glu, xa_w_q=xa_w_q, xa_w_kv=xa_w_kv, xa_w_o=xa_w_o)
    mom = dict(norm_g=m_norm_g, final_g=m_final_g, ffn1_up=m_ffn1_up, ffn1_down=m_ffn1_down, ffn2_up=m_ffn2_up,
               ffn2_down=m_ffn2_down, conv_w_in=m_conv_w_in, conv_w=m_conv_w, conv_w_out=m_conv_w_out,
               ssm_a_re=m_ssm_a_re, ssm_a_im=m_ssm_a_im, ssm_log_dt=m_ssm_log_dt, ssm_b_re=m_ssm_b_re,
               ssm_b_im=m_ssm_b_im, ssm_c_re=m_ssm_c_re, ssm_c_im=m_ssm_c_im, ssm_d=m_ssm_d, ssm_w_glu=m_ssm_w_glu,
               xa_w_q=m_xa_w_q, xa_w_kv=m_xa_w_kv, xa_w_o=m_xa_w_o)
    var = dict(norm_g=v_norm_g, final_g=v_final_g, ffn1_up=v_ffn1_up, ffn1_down=v_ffn1_down, ffn2_up=v_ffn2_up,
               ffn2_down=v_ffn2_down, conv_w_in=v_conv_w_in, conv_w=v_conv_w, conv_w_out=v_conv_w_out,
               ssm_a_re=v_ssm_a_re, ssm_a_im=v_ssm_a_im, ssm_log_dt=v_ssm_log_dt, ssm_b_re=v_ssm_b_re,
               ssm_b_im=v_ssm_b_im, ssm_c_re=v_ssm_c_re, ssm_c_im=v_ssm_c_im, ssm_d=v_ssm_d, ssm_w_glu=v_ssm_w_glu,
               xa_w_q=v_xa_w_q, xa_w_kv=v_xa_w_kv, xa_w_o=v_xa_w_o)

    n_seq, seq, d = x.shape
    mlen = mem.shape[1]
    depth, n_norms = norm_g.shape[0], norm_g.shape[1]
    heads = 4
    tokens = n_seq * seq
    x2 = x.reshape(tokens, d)
    mem2 = mem.reshape(n_seq * mlen, d)
    tgt2 = loss_target.reshape(tokens, d)

    small_shapes = [w[k].shape for k in SMALL_SHARDED]
    small_rows = [_merge2d(w[k]) for k in SMALL_SHARDED]
    small_counts = [s.shape[0] for s in small_rows]
    small = jnp.concatenate(small_rows, axis=0)
    small = jnp.pad(small, [(0, (-small.shape[0]) % 8), (0, 0)])
    me = _index(*_place())

    def layer_weights(i):
        names = [(k, i) for k in ("ffn1_up", "ffn1_down", "ffn2_up", "ffn2_down", "xa_w_q", "xa_w_kv", "xa_w_o")]
        return names + ([("conv_w_in", i // 2), ("conv_w_out", i // 2)] if i % 2 == 0 else [("ssm_w_glu", i // 2)])

    shards = [[w[k][idx].astype(BF16) for k, idx in layer_weights(i)] for i in range(depth)]
    gathered = _all_gather("gather_layer0", shards[0] + [small])
    small_all = gathered[-1]
    blocks = [gathered[:-1]] + [None] * (depth - 1)
    in_flight = [None] * depth
    token = gathered[0]
    in_place = {k for k in COLUMN_SHARDED if w[k].shape[-1] % V7X_LANES == 0}

    def modes(i):
        return [COLUMNS if k in in_place else BLOCK for k, _ in layer_weights(i)]

    for i in range(1, depth):
        zones = [_landing(s, me, mode) for s, mode in zip(shards[i], modes(i))]
        whole_src = [WHOLE] * len(zones)
        *in_flight[i], token = _exchange_start(f"gather_start_l{i}", shards[i], zones, whole_src, modes(i),
                                               deps=[token], plan=CHIPS)
    passing = [None] * depth

    def small_whole(idx):
        start = sum(small_counts[:idx])
        part = small_all[:, start:start + small_counts[idx]]
        lead = small_shapes[idx][:-1]
        part = part.reshape((N_DEV,) + lead + (part.shape[-1],))
        part = jnp.moveaxis(part, 0, -2)
        return part.reshape(lead + (N_DEV * part.shape[-1],))

    norm_all = small_whole(0)
    conv_all = small_whole(1)
    dskip_all = small_whole(2)

    def whole(i):
        return {k: _rows_whole(blk) if k not in COLUMN_SHARDED else blk if i > 0 and k in in_place else _cols_whole(blk)
                for (k, _), blk in zip(layer_weights(i), blocks[i])}

    saved = []
    cur = x2
    for i in range(depth):
        g = [norm_all[i, k].reshape(1, d) for k in range(n_norms)]
        j = i // 2
        if i > 0:
            blocks[i] = _exchange_wait(f"forward_wait_l{i}", *passing[i], modes(i), modes(i), cur, plan=FORWARD)
        lw = whole(i)
        cur, s_ffn1 = _ffn_fwd(f"l{i}_ffn1", cur, g[0], lw["ffn1_up"], lw["ffn1_down"], [token] if i == 0 else ())
        if i % 2 == 0:
            lw["conv_w"] = conv_all[j]
            cur, s_mix = _conv_mixer_fwd(f"l{i}_conv", cur, g[1], lw["conv_w_in"], lw["conv_w"], lw["conv_w_out"],
                                         n_seq, seq)
        else:
            ssm = tuple(w[k][j] for k in ("ssm_a_re", "ssm_a_im", "ssm_log_dt", "ssm_b_re", "ssm_b_im",
                                          "ssm_c_re", "ssm_c_im"))
            cur, s_mix = _s5_mixer_fwd(f"l{i}_s5", cur, g[1], ssm, dskip_all[j].reshape(1, d), lw["ssm_w_glu"],
                                       n_seq, seq)
        cur, s_xa = _xattn_fwd(f"l{i}_xa", cur, mem2, g[2], g[3], lw["xa_w_q"], lw["xa_w_kv"], lw["xa_w_o"],
                               n_seq, seq, mlen, heads)
        deps = ()
        if i + 1 < depth:
            nxt = modes(i + 1)
            landed = _exchange_wait(f"gather_wait_l{i + 1}", *in_flight[i + 1], [WHOLE] * len(nxt), nxt, cur,
                                    plan=CHIPS)
            *passing[i + 1], forward_token = _exchange_start(f"forward_start_l{i + 1}", [], landed, nxt, nxt,
                                                             plan=FORWARD)
            deps = [forward_token]
        cur, s_ffn2 = _ffn_fwd(f"l{i}_ffn2", cur, g[4], lw["ffn2_up"], lw["ffn2_down"], deps)
        saved.append((g, lw, s_ffn1, s_mix, s_xa, s_ffn2))

    dres, err2, d_final = _final_loss("loss_head", cur, final_g.reshape(1, d), tgt2)
    loss = lax.psum(0.5 * jnp.sum(err2) / d, ("x", "y", "c"))

    d_norm = [[None] * n_norms for _ in range(depth)]
    d_conv = [None] * conv_w.shape[0]
    d_skip = [None] * ssm_d.shape[0]
    d_ssm = [None] * ssm_a_re.shape[0]
    leaving = [None] * depth
    deps = ()

    def leave(name, keys, gm, extra=()):
        srcs, src_modes, zones = [], [], []
        for k in keys:
            if k in in_place:
                rows, n = gm[k].shape[0], gm[k].shape[1] // N_DEV
                srcs.append(gm[k])
                src_modes.append(COLUMNS)
                zones.append(_landing(lax.dynamic_slice(gm[k], (0, me * n), (rows, n)), me))
            else:
                srcs.append(_cols_parts(gm[k]) if k in COLUMN_SHARDED else _rows_parts(gm[k]))
                src_modes.append(BLOCK)
                zones.append(_landing(lax.dynamic_index_in_dim(srcs[-1], me, 0, keepdims=False), me))
        for p in extra:
            srcs.append(p)
            src_modes.append(BLOCK)
            zones.append(_landing(lax.dynamic_index_in_dim(p, me, 0, keepdims=False), me))
        land_modes = [BLOCK] * len(srcs)
        *handles, token = _exchange_start(name, srcs, zones, src_modes, land_modes)
        return (*handles, src_modes, land_modes), token

    def small_parts(full):
        lead = full.shape[:-1]
        t = full.reshape(lead + (N_DEV, full.shape[-1] // N_DEV))
        t = jnp.moveaxis(t, -2, 0)
        return t.reshape(N_DEV, -1, t.shape[-1])

    for i in reversed(range(depth)):
        g, lw, s_ffn1, s_mix, s_xa, s_ffn2 = saved[i]
        j = i // 2
        gm = {}
        dres, d_norm[i][4], gm["ffn2_up"], gm["ffn2_down"] = _ffn_bwd(
            f"l{i}_ffn2", dres, s_ffn2, g[4], lw["ffn2_up"], lw["ffn2_down"], deps)
        dres, d_norm[i][2], d_norm[i][3], gm["xa_w_q"], gm["xa_w_kv"], gm["xa_w_o"] = _xattn_bwd(
            f"l{i}_xa", dres, s_xa, mem2, g[2], g[3], lw["xa_w_q"], lw["xa_w_kv"], lw["xa_w_o"], n_seq, seq, mlen,
            heads)
        if i % 2 == 0:
            dres, d_norm[i][1], gm["conv_w_in"], d_conv[j], gm["conv_w_out"] = _conv_mixer_bwd(
                f"l{i}_conv", dres, s_mix, g[1], lw["conv_w_in"], lw["conv_w"], lw["conv_w_out"], n_seq, seq)
        else:
            dres, d_norm[i][1], d_ssm[j], d_skip[j], gm["ssm_w_glu"] = _s5_mixer_bwd(
                f"l{i}_s5", dres, s_mix, g[1], lw["ssm_w_glu"], n_seq, seq)
        upper, token = leave(f"grads_start_l{i}_upper", [k for k, _ in layer_weights(i)[2:]], gm)
        dres, d_norm[i][0], gm["ffn1_up"], gm["ffn1_down"] = _ffn_bwd(
            f"l{i}_ffn1", dres, s_ffn1, g[0], lw["ffn1_up"], lw["ffn1_down"], [token])
        extra = []
        if i == 0:
            d_norm_all = jnp.stack([jnp.concatenate(row, axis=0) for row in d_norm])
            small_g = jnp.concatenate(
                [small_parts(a) for a in (d_norm_all, jnp.stack(d_conv), jnp.concatenate(d_skip, axis=0))], axis=1)
            extra = [jnp.pad(small_g, [(0, 0), (0, (-small_g.shape[1]) % 8), (0, 0)])]
        lower, token = leave(f"grads_start_l{i}_lower", ["ffn1_up", "ffn1_down"], gm, extra)
        leaving[i] = (upper, lower)
        deps = [token]
        if i == min(1, depth - 1):
            rep_grads = [jnp.stack([d_ssm[j][k] for j in range(len(d_ssm))]) for k in range(7)]
            rep_packed = _pack_rows(rep_grads + [d_final.reshape(-1)])
            *rep_leaving, token = _exchange_start("replicated_grads_start", [rep_packed], [_landing(rep_packed, me)],
                                                  [WHOLE], [BLOCK])
            deps = deps + [token]
    grad_x = dres[0].reshape(n_seq, seq, d)
    received = {k: [None] * w[k].shape[0] for k in MATRICES}

    def arrive(i, after):
        upper = _exchange_wait(f"grads_wait_l{i}_upper", *leaving[i][0], after)
        lower = _exchange_wait(f"grads_wait_l{i}_lower", *leaving[i][1], after)
        for (k, idx), blk in zip(layer_weights(i), lower[:2] + upper):
            received[k][idx] = blk
        return lower[2:]

    for i in range(1, depth):
        arrive(i, dres[0])
    rep_all = _exchange_wait("replicated_grads_wait", *rep_leaving, [WHOLE], [BLOCK], dres[0])[0]
    rep_shapes = [w[k].shape for k in REPLICATED]
    flat = {k: (_merge2d(w[k]), _merge2d(mom[k]), _merge2d(var[k])) for k in MATRICES}
    late = {k: received[k][0] is None for k in MATRICES}
    early = {}
    for k in MATRICES:
        first = 1 if late[k] else 0
        if first < len(received[k]):
            early[k] = _adamw(f"adamw_{k}_upper", received[k][first:], *flat[k], first_layer=first)
    small_received, = arrive(0, list(early.values())[-1][0] if early else dres[0])
    grads, deltas, new_m, new_v = {}, {}, {}, {}
    for k in MATRICES:
        out = early.get(k)
        if late[k]:
            out = _adamw(f"adamw_{k}_l0", received[k][:1], *flat[k], earlier=out)
        grads[k], deltas[k], new_m[k], new_v[k] = [o.reshape(w[k].shape) for o in out]

    def small_local(src):
        rows = jnp.concatenate([_merge2d(src[k]) for k in SMALL_SHARDED], axis=0)
        return jnp.pad(rows, [(0, (-rows.shape[0]) % 8), (0, 0)])

    out = _adamw("adamw_small", [small_received], small, small_local(mom), small_local(var))
    for res, o in zip((grads, deltas, new_m, new_v), out):
        start = 0
        for k, cnt, shape in zip(SMALL_SHARDED, small_counts, small_shapes):
            res[k] = o[start:start + cnt].reshape(shape)
            start += cnt

    out = _adamw("adamw_replicated", [rep_all], _pack_rows([w[k] for k in REPLICATED]),
                 _pack_rows([mom[k] for k in REPLICATED]), _pack_rows([var[k] for k in REPLICATED]))
    for res, o in zip((grads, deltas, new_m, new_v), out):
        for k, a in zip(REPLICATED, _unpack_rows(o, rep_shapes)):
            res[k] = a

    return (loss, grad_x, *[grads[k] for k in WEIGHT_NAMES], *[deltas[k] for k in WEIGHT_NAMES],
            *[new_m[k] for k in WEIGHT_NAMES], *[new_v[k] for k in WEIGHT_NAMES])
```

```python
import math

import jax
import jax.numpy as jnp
from jax import lax
from jax.experimental import pallas as pl
from jax.experimental.pallas import tpu as pltpu

F32 = jnp.float32
BF16 = jnp.bfloat16
MESH = pl.DeviceIdType.MESH
N_DEV = 8

NORM_EPS = 1e-6
EIG_CLIP = -1e-4
CONV_WIDTH = 3
ADAM_LR = 0.001
ADAM_B1 = 0.9
ADAM_B2 = 0.999
ADAM_EPS = 1e-08
ADAM_WD = 0.01
ADAM_STEP = 10
GELU_C = math.sqrt(2.0 / math.pi)
GELU_A = 0.044715

V7X_LANES = 128
V7X_VMEM_LIMIT = 56 * 1024 * 1024
S5_CHANNELS = 128
PACK_TILE = 8 * V7X_LANES

HBM_SPEC = pl.BlockSpec(memory_space=pltpu.HBM)
ANY_SPEC = pl.BlockSpec(memory_space=pl.ANY)
SEM_SPEC = pl.BlockSpec(memory_space=pltpu.SEMAPHORE)


def _params(n_grid):
    return pltpu.CompilerParams(dimension_semantics=("arbitrary",) * n_grid, vmem_limit_bytes=V7X_VMEM_LIMIT)


def _pick(n, pref, align):
    if n <= pref:
        return n
    t = (pref // align) * align
    while t >= align:
        if n % t == 0:
            return t
        t -= align
    raise ValueError(f"no tile for {n} (pref {pref}, align {align})")


MM_RHS_BLOCK_BYTES = 12 * 1024 * 1024
MM_LHS_BLOCK_BYTES = 6 * 1024 * 1024
MM_ACC_BYTES = 8 * 1024 * 1024
MM_ROWS = 1024


MM_EPILOGUE_ROWS = 256
MM_EPILOGUE_CHUNKS = 2


def _mm_tiles(m, k, n, a_item, b_item, ta, max_rows):
    tn = _pick(n, max(V7X_LANES, MM_RHS_BLOCK_BYTES // (k * b_item)), V7X_LANES)
    rows = min(max_rows, MM_ACC_BYTES // (4 * tn), MM_LHS_BLOCK_BYTES // (k * a_item))
    align = V7X_LANES if ta else 16
    tm = _pick(m, max(align, rows), align)
    return tm, tn


def _store_results(out_refs, n_row, results, first):
    if not isinstance(results, (tuple, list)):
        results = (results,)
    for o, v in zip(out_refs[:n_row], results[:n_row]):
        if isinstance(v, (tuple, list)):
            off = 0
            for piece in v:
                w = piece.shape[1]
                o[:, off:off + w] = piece.astype(o.dtype)
                off += w
        else:
            o[...] = v.astype(o.dtype)
    if len(out_refs) > n_row:
        @pl.when(first)
        def _():
            for o in out_refs[n_row:]:
                o[...] = jnp.zeros_like(o)

        for o, v in zip(out_refs[n_row:], results[n_row:]):
            o[...] += v


def _mm(name, a, b, *, ta=False, tb=False, out_dtype=BF16, res=None, scale=None, deps=(),
        epilogue=None, row_ins=(), par_ins=(), outs=(), acc_outs=(), prologue=None, prologue_pars=(), sections=0):
    if ta:
        k, m = a.shape
    else:
        m, k = a.shape
    if tb:
        n, k2 = b.shape
    else:
        k2, n = b.shape
    assert k == k2, (name, a.shape, b.shape)
    max_rows = MM_ROWS if epilogue is None else MM_EPILOGUE_ROWS
    tm, tn = _mm_tiles(m, k, n, a.dtype.itemsize, b.dtype.itemsize, ta, max_rows)
    a_spec = pl.BlockSpec((k, tm), lambda j, i: (0, i)) if ta else pl.BlockSpec((tm, k), lambda j, i: (i, 0))
    b_spec = pl.BlockSpec((tn, k), lambda j, i: (j, 0)) if tb else pl.BlockSpec((k, tn), lambda j, i: (0, j))
    o_spec = pl.BlockSpec((tm, tn), lambda j, i: (i, j))
    dims = (((0 if ta else 1,), (1 if tb else 0,)), ((), ()))
    has_res = res is not None
    ins = [a, b] + ([res] if has_res else [])
    specs = [a_spec, b_spec] + ([o_spec] if has_res else [])
    n_mm = len(ins)
    if epilogue is None:
        out_specs, out_shape = [o_spec], [jax.ShapeDtypeStruct((m, n), out_dtype)]
    else:
        assert tn == n, (name, tn, n)
        for r in row_ins:
            arr, cb, cw = r if isinstance(r, tuple) else (r, 0, r.shape[1])
            assert arr.shape[0] == m, (name, arr.shape, m)
            ins.append(arr)
            specs.append(pl.BlockSpec((tm, cw), lambda j, i, cb=cb: (i, cb)))
        for p in par_ins:
            ins.append(p)
            specs.append(pl.BlockSpec(p.shape, lambda j, i: (0, 0)))
        out_specs = [pl.BlockSpec((tm, c), lambda j, i: (i, 0)) for c, _ in outs]
        out_specs += [pl.BlockSpec((r, c), lambda j, i: (0, 0)) for r, c in acc_outs]
        out_shape = [jax.ShapeDtypeStruct((m, c), dt) for c, dt in outs]
        out_shape += [jax.ShapeDtypeStruct((r, c), F32) for r, c in acc_outs]
    n_in = len(ins)
    if prologue is not None:
        assert tn == n and not ta, (name, tn, n, ta)
        for p in prologue_pars:
            ins.append(p)
            specs.append(pl.BlockSpec(p.shape, lambda j, i: (0, 0)))
        out_specs = out_specs + [a_spec]
        out_shape = out_shape + [jax.ShapeDtypeStruct((m, k), BF16)]
    n_pro = len(ins)
    ins += list(deps)
    specs += [ANY_SPEC] * len(deps)
    if sections:
        assert epilogue is not None and not has_res and not acc_outs and not par_ins and n % sections == 0, name
        width = n // sections
        n_chunks = MM_EPILOGUE_CHUNKS if width % (MM_EPILOGUE_CHUNKS * V7X_LANES) == 0 else 1
        chunk = width // n_chunks

    def body(*refs):
        a_ref, b_ref = refs[0], refs[1]
        out_refs = refs[n_pro + len(deps):]
        if prologue is None:
            lhs = a_ref[...].astype(BF16)
        else:
            lhs = prologue(a_ref[...], *[r[...] for r in refs[n_in:n_pro]]).astype(BF16)
            out_refs[-1][...] = lhs
            out_refs = out_refs[:-1]
        if sections:
            for ch in range(n_chunks):
                at = [s * width + ch * chunk for s in range(sections)]
                accs = []
                for c0 in at:
                    rhs = b_ref[c0:c0 + chunk, :] if tb else b_ref[:, c0:c0 + chunk]
                    acc = lax.dot_general(lhs, rhs.astype(BF16), dims, preferred_element_type=F32)
                    accs.append(acc if scale is None else acc * scale)
                extra = [r[:, ch * chunk:(ch + 1) * chunk] for r in refs[n_mm:n_in]]
                for o, pieces in zip(out_refs, epilogue(*accs, *extra)):
                    for s, piece in enumerate(pieces):
                        o[:, s * width + ch * chunk:s * width + (ch + 1) * chunk] = piece.astype(o.dtype)
            return
        acc = lax.dot_general(lhs, b_ref[...].astype(BF16), dims, preferred_element_type=F32)
        if scale is not None:
            acc = acc * scale
        if has_res:
            acc = acc + refs[2][...].astype(F32)
        if epilogue is None:
            out_refs[0][...] = acc.astype(out_refs[0].dtype)
        else:
            extra = [r[...] for r in refs[n_mm:n_in]]
            _store_results(out_refs, len(outs), epilogue(acc, *extra), pl.program_id(1) == 0)

    out = pl.pallas_call(
        body, name=name, grid=(n // tn, m // tm), in_specs=specs, out_specs=out_specs, out_shape=out_shape,
        compiler_params=_params(2),
    )(*ins)
    return out[0] if epilogue is None and prologue is None else out


def _rowwise(name, fn, rows, row_ins, par_ins, row_outs, acc_outs=(), tm_pref=256):
    tm = _pick(rows, tm_pref, 16)
    in_specs, ins = [], []
    for r in row_ins:
        arr, cb, cw = r if isinstance(r, tuple) else (r, 0, r.shape[1])
        assert arr.shape[0] == rows, (name, arr.shape, rows)
        ins.append(arr)
        in_specs.append(pl.BlockSpec((tm, cw), lambda i, cb=cb: (i, cb)))
    for p in par_ins:
        ins.append(p)
        in_specs.append(pl.BlockSpec(p.shape, lambda i: (0, 0)))
    out_specs = [pl.BlockSpec((tm, c), lambda i: (i, 0)) for c, _ in row_outs]
    out_specs += [pl.BlockSpec((r, c), lambda i: (0, 0)) for r, c in acc_outs]
    out_shape = [jax.ShapeDtypeStruct((rows, c), dt) for c, dt in row_outs]
    out_shape += [jax.ShapeDtypeStruct((r, c), F32) for r, c in acc_outs]
    n_in, n_row = len(ins), len(row_outs)

    def body(*refs):
        vals = [r[...] for r in refs[:n_in]]
        _store_results(refs[n_in:], n_row, fn(*vals), pl.program_id(0) == 0)

    return pl.pallas_call(
        body, name=name, grid=(rows // tm,), in_specs=in_specs, out_specs=out_specs, out_shape=out_shape,
        compiler_params=_params(1),
    )(*ins)


def _inv_rms(x):
    return lax.rsqrt(jnp.mean(x * x, axis=-1, keepdims=True) + NORM_EPS)


def _rms_rows(x, g):
    return x * _inv_rms(x) * g


def _rms_fwd(name, x, g):
    return _rowwise(name, _rms_rows, x.shape[0], [x], [g], [(x.shape[1], BF16)], tm_pref=512)[0]


def _rms_bwd_rows(dn, x, dres, g):
    r = _inv_rms(x)
    xh = x * r
    dg = jnp.sum(dn * xh, axis=0, keepdims=True)
    dxh = dn * g
    dx = r * (dxh - xh * jnp.mean(dxh * xh, axis=-1, keepdims=True)) + dres
    return dx, dx, dg


def _rms_bwd(name, x, g, dn, dres=None):
    d = x.shape[1]
    if dres is None:
        def fn(x, dn, g):
            return (jnp.sum(dn.astype(F32) * (x * _inv_rms(x)), axis=0, keepdims=True),)

        return None, _rowwise(name, fn, x.shape[0], [x, dn], [g], [], [(1, d)])[0]

    def fn(x, dn, dres, g):
        return _rms_bwd_rows(dn.astype(F32), x, dres, g)

    out = _rowwise(name, fn, x.shape[0], [x, dn, dres], [g], [(d, F32), (d, BF16)], [(1, d)])
    return (out[0], out[1]), out[2]


def _mm_rms_bwd(name, dy, w, x, g, dres):
    d = x.shape[1]
    out = _mm(name, dy, w, tb=True, epilogue=_rms_bwd_rows, row_ins=[x, dres], par_ins=[g],
              outs=[(d, F32), (d, BF16)], acc_outs=[(1, d)])
    return (out[0], out[1]), out[2]


def _sigmoid(x):
    return 1.0 / (1.0 + jnp.exp(-x))


def _swiglu_tiles(gt, up):
    return (gt, up), (gt * _sigmoid(gt) * up,)


def _swiglu_bwd_tiles(dact, gt, up):
    gt, up = gt.astype(F32), up.astype(F32)
    sg = _sigmoid(gt)
    return ((dact * up * (sg * (1.0 + gt * (1.0 - sg))), dact * (gt * sg)),)


def _glu_res(name, vg, x, d):
    def fn(val, gate, x):
        return x + val.astype(F32) * _sigmoid(gate.astype(F32))

    return _rowwise(name, fn, x.shape[0], [(vg, 0, d), (vg, 1, d), x], [], [(d, F32)])[0]


def _glu_bwd(name, dres, vg, d):
    def fn(dres, val, gate):
        val, gate = val.astype(F32), gate.astype(F32)
        sg = _sigmoid(gate)
        return ((dres * sg, dres * val * sg * (1.0 - sg)),)

    return _rowwise(name, fn, dres.shape[0], [dres, (vg, 0, d), (vg, 1, d)], [], [(2 * d, BF16)])[0]


def _final_loss(name, x, g, tgt):
    d = x.shape[1]

    def fn(x, tgt, g):
        r = _inv_rms(x)
        xh = x * r
        err = xh * g - tgt
        dy = err * (1.0 / d)
        dxh = dy * g
        dx = r * (dxh - xh * jnp.mean(dxh * xh, axis=-1, keepdims=True))
        return dx, dx, jnp.sum(err * err, axis=0, keepdims=True), jnp.sum(dy * xh, axis=0, keepdims=True)

    dx, dx16, err2, dg = _rowwise(name, fn, x.shape[0], [x, tgt], [g], [(d, F32), (d, BF16)], [(1, d), (1, d)])
    return (dx, dx16), err2, dg


def _shift_down(u, k):
    rows = lax.broadcasted_iota(jnp.int32, u.shape, 0)
    return jnp.where(rows >= k, pltpu.roll(u, k, 0), 0.0)


def _shift_up(u, k):
    n = u.shape[0]
    rows = lax.broadcasted_iota(jnp.int32, u.shape, 0)
    return jnp.where(rows < n - k, pltpu.roll(u, n - k, 0), 0.0)


def _conv_specs(seq, cw, n_cb, swap):
    def at(off):
        if swap:
            return pl.BlockSpec((seq, cw), lambda j, b: (b, off * n_cb + j))
        return pl.BlockSpec((seq, cw), lambda b, j: (b, off * n_cb + j))

    return at


def _conv_fwd(name, cbv, w, n_seq, seq):
    d = w.shape[1]
    cw = _pick(d, 256, V7X_LANES)
    n_cb = d // cw
    at = _conv_specs(seq, cw, n_cb, swap=False)

    def body(c_ref, b_ref, v_ref, w_ref, z_ref):
        u = c_ref[...].astype(F32) * v_ref[...].astype(F32)
        cv = w_ref[0:1, :] * _shift_down(u, 2) + w_ref[1:2, :] * _shift_down(u, 1) + w_ref[2:3, :] * u
        z_ref[...] = (b_ref[...].astype(F32) * cv).astype(z_ref.dtype)

    return pl.pallas_call(
        body, name=name, grid=(n_seq, n_cb),
        in_specs=[at(0), at(1), at(2), pl.BlockSpec((CONV_WIDTH, cw), lambda b, j: (0, j))],
        out_specs=at(0), out_shape=jax.ShapeDtypeStruct((n_seq * seq, d), BF16), compiler_params=_params(2),
    )(cbv, cbv, cbv, w)


def _conv_bwd(name, dz, cbv, w, n_seq, seq):
    d = w.shape[1]
    cw = _pick(d, 256, V7X_LANES)
    n_cb = d // cw
    at = _conv_specs(seq, cw, n_cb, swap=True)

    def body(dz_ref, c_ref, b_ref, v_ref, w_ref, dc_ref, db_ref, dv_ref, dw_ref):
        c, b, v = c_ref[...].astype(F32), b_ref[...].astype(F32), v_ref[...].astype(F32)
        dz = dz_ref[...].astype(F32)
        w0, w1, w2 = w_ref[0:1, :], w_ref[1:2, :], w_ref[2:3, :]
        u = c * v
        u1, u2 = _shift_down(u, 1), _shift_down(u, 2)
        cv = w0 * u2 + w1 * u1 + w2 * u
        db_ref[...] = (dz * cv).astype(db_ref.dtype)
        dcv = dz * b
        du = w2 * dcv + w1 * _shift_up(dcv, 1) + w0 * _shift_up(dcv, 2)
        dc_ref[...] = (du * v).astype(dc_ref.dtype)
        dv_ref[...] = (du * c).astype(dv_ref.dtype)

        @pl.when(pl.program_id(1) == 0)
        def _():
            dw_ref[...] = jnp.zeros_like(dw_ref)

        dw_ref[0:1, :] += jnp.sum(dcv * u2, axis=0, keepdims=True)
        dw_ref[1:2, :] += jnp.sum(dcv * u1, axis=0, keepdims=True)
        dw_ref[2:3, :] += jnp.sum(dcv * u, axis=0, keepdims=True)

    act = jax.ShapeDtypeStruct((n_seq * seq, d), BF16)
    return pl.pallas_call(
        body, name=name, grid=(n_cb, n_seq),
        in_specs=[at(0), at(0), at(1), at(2), pl.BlockSpec((CONV_WIDTH, cw), lambda j, b: (0, j))],
        out_specs=[at(0), at(0), at(0), pl.BlockSpec((CONV_WIDTH, cw), lambda j, b: (0, j))],
        out_shape=[act, act, act, jax.ShapeDtypeStruct((CONV_WIDTH, d), F32)], compiler_params=_params(2),
    )(dz, cbv, cbv, cbv, w)


def _s5_discretize(a_re, a_im, log_dt, b_re, b_im):
    lam_re = jnp.minimum(a_re, EIG_CLIP)
    lam_im = a_im
    dt = jnp.exp(log_dt)[:, None]
    mag = jnp.exp(lam_re * dt)
    abar_re = mag * jnp.cos(lam_im * dt)
    abar_im = mag * jnp.sin(lam_im * dt)
    den = lam_re * lam_re + lam_im * lam_im
    num_re = abar_re - 1.0
    num_im = abar_im
    coef_re = (num_re * lam_re + num_im * lam_im) / den
    coef_im = (num_im * lam_re - num_re * lam_im) / den
    bbar_re = coef_re[..., None] * b_re - coef_im[..., None] * b_im
    bbar_im = coef_re[..., None] * b_im + coef_im[..., None] * b_re
    return abar_re, abar_im, bbar_re, bbar_im


def _block_diag_in(bbar, gb):
    g, p, h = bbar.shape
    t = jnp.transpose(bbar.reshape(g // gb, gb, p, h), (0, 1, 3, 2))
    return jnp.einsum("cghp,gk->cghkp", t, jnp.eye(gb, dtype=bbar.dtype)).reshape(g // gb, gb * h, gb * p)


def _block_diag_in_t(blk, gb, p, h):
    nb = blk.shape[0]
    t = jnp.einsum("cghkp,gk->cghp", blk.reshape(nb, gb, h, gb, p), jnp.eye(gb, dtype=blk.dtype))
    return jnp.transpose(t, (0, 1, 3, 2)).reshape(nb * gb, p, h)


def _block_diag_out(c, gb):
    g, h, p = c.shape
    t = jnp.transpose(c.reshape(g // gb, gb, h, p), (0, 1, 3, 2))
    return jnp.einsum("cgph,gk->cgpkh", t, jnp.eye(gb, dtype=c.dtype)).reshape(g // gb, gb * p, gb * h)


def _block_diag_out_t(blk, gb, p, h):
    nb = blk.shape[0]
    t = jnp.einsum("cgpkh,gk->cgph", blk.reshape(nb, gb, p, gb, h), jnp.eye(gb, dtype=blk.dtype))
    return jnp.transpose(t, (0, 1, 3, 2)).reshape(nb * gb, h, p)


def _gelu(y):
    return 0.5 * y * (1.0 + jnp.tanh(GELU_C * (y + GELU_A * y * y * y)))


def _gelu_grad(y):
    th = jnp.tanh(GELU_C * (y + GELU_A * y * y * y))
    return 0.5 * (1.0 + th) + 0.5 * y * (1.0 - th * th) * GELU_C * (1.0 + 3.0 * GELU_A * y * y)


def _dot(a, b, ca, cb):
    return lax.dot_general(a.astype(BF16), b.astype(BF16), (((ca,), (cb,)), ((), ())), preferred_element_type=F32)


def _s5_specs(seq, ch, sb):
    act = pl.BlockSpec((seq, ch), lambda j, b: (b, j))
    state = pl.BlockSpec((seq, sb), lambda j, b: (b, j))
    w_in = pl.BlockSpec((None, ch, sb), lambda j, b: (j, 0, 0))
    w_out = pl.BlockSpec((None, sb, ch), lambda j, b: (j, 0, 0))
    lane_s = pl.BlockSpec((1, sb), lambda j, b: (0, j))
    lane_c = pl.BlockSpec((1, ch), lambda j, b: (0, j))
    return act, state, w_in, w_out, lane_s, lane_c


def _s5_fwd(name, h, bin_re, bin_im, cout_re, cout_im, abar_re, abar_im, dskip, n_seq, seq):
    t, d = h.shape
    nb, ch, sb = bin_re.shape
    act, state, w_in, w_out, lane_s, lane_c = _s5_specs(seq, ch, sb)

    def body(h_ref, bre_ref, bim_ref, cre_ref, cim_ref, ar_ref, ai_ref, d_ref, sre_ref, sim_ref, y_ref, z_ref,
             sre, sim):
        u = h_ref[...]
        sre[...] = _dot(u, bre_ref[...], 1, 0)
        sim[...] = _dot(u, bim_ref[...], 1, 0)
        ar, ai = ar_ref[...], ai_ref[...]

        def step(i, carry):
            sr, si = carry
            row = pl.ds(i, 1)
            nr = ar * sr - ai * si + sre[row, :]
            ni = ar * si + ai * sr + sim[row, :]
            sre[row, :] = nr
            sim[row, :] = ni
            return nr, ni

        zero = jnp.zeros((1, sb), F32)
        lax.fori_loop(0, seq, step, (zero, zero), unroll=8)
        s_re, s_im = sre[...].astype(BF16), sim[...].astype(BF16)
        sre_ref[...] = s_re
        sim_ref[...] = s_im
        y = _dot(s_re, cre_ref[...], 1, 0) - _dot(s_im, cim_ref[...], 1, 0)
        y = y + d_ref[...] * u.astype(F32)
        y_ref[...] = y
        z_ref[...] = _gelu(y).astype(z_ref.dtype)

    return pl.pallas_call(
        body, name=name, grid=(nb, n_seq),
        in_specs=[act, w_in, w_in, w_out, w_out, lane_s, lane_s, lane_c],
        out_specs=[state, state, act, act],
        out_shape=[jax.ShapeDtypeStruct((t, nb * sb), BF16), jax.ShapeDtypeStruct((t, nb * sb), BF16),
                   jax.ShapeDtypeStruct((t, d), F32), jax.ShapeDtypeStruct((t, d), BF16)],
        scratch_shapes=[pltpu.VMEM((seq, sb), F32), pltpu.VMEM((seq, sb), F32)],
        compiler_params=_params(2),
    )(h, bin_re, bin_im, cout_re, cout_im, abar_re, abar_im, dskip)


def _s5_bwd(name, dz, ypre, h, s_re, s_im, bin_re, bin_im, cout_re, cout_im, abar_re, abar_im, dskip, n_seq, seq):
    t, d = h.shape
    nb, ch, sb = bin_re.shape
    act, state, w_in, w_out, lane_s, lane_c = _s5_specs(seq, ch, sb)

    def body(dz_ref, y_ref, h_ref, sre_ref, sim_ref, bre_ref, bim_ref, cre_ref, cim_ref, ar_ref, ai_ref, d_ref,
             dh_ref, dbre_ref, dbim_ref, dcre_ref, dcim_ref, dar_ref, dai_ref, dd_ref, gre, gim):
        first = pl.program_id(1) == 0
        u = h_ref[...].astype(F32)
        dy = dz_ref[...].astype(F32) * _gelu_grad(y_ref[...])
        gre[...] = _dot(dy, cre_ref[...], 1, 1)
        gim[...] = -_dot(dy, cim_ref[...], 1, 1)
        ar, ai = ar_ref[...], ai_ref[...]

        def step(i, carry):
            gr, gi = carry
            row = pl.ds(seq - 1 - i, 1)
            nr = gre[row, :] + ar * gr + ai * gi
            ni = gim[row, :] - ai * gr + ar * gi
            gre[row, :] = nr
            gim[row, :] = ni
            return nr, ni

        zero = jnp.zeros((1, sb), F32)
        lax.fori_loop(0, seq, step, (zero, zero), unroll=8)

        g_re, g_im = gre[...], gim[...]
        s_re, s_im = sre_ref[...].astype(F32), sim_ref[...].astype(F32)
        p_re, p_im = _shift_down(s_re, 1), _shift_down(s_im, 1)
        dar = jnp.sum(g_re * p_re + g_im * p_im, axis=0, keepdims=True)
        dai = jnp.sum(g_im * p_re - g_re * p_im, axis=0, keepdims=True)
        dbre = _dot(u, g_re, 0, 0)
        dbim = _dot(u, g_im, 0, 0)
        dcre = _dot(s_re, dy, 0, 0)
        dcim = -_dot(s_im, dy, 0, 0)
        ddd = jnp.sum(dy * u, axis=0, keepdims=True)
        dh_ref[...] = _dot(g_re, bre_ref[...], 1, 1) + _dot(g_im, bim_ref[...], 1, 1) + d_ref[...] * dy

        @pl.when(first)
        def _():
            dar_ref[...] = dar
            dai_ref[...] = dai
            dbre_ref[...] = dbre
            dbim_ref[...] = dbim
            dcre_ref[...] = dcre
            dcim_ref[...] = dcim
            dd_ref[...] = ddd

        @pl.when(jnp.logical_not(first))
        def _():
            dar_ref[...] += dar
            dai_ref[...] += dai
            dbre_ref[...] += dbre
            dbim_ref[...] += dbim
            dcre_ref[...] += dcre
            dcim_ref[...] += dcim
            dd_ref[...] += ddd

    return pl.pallas_call(
        body, name=name, grid=(nb, n_seq),
        in_specs=[act, act, act, state, state, w_in, w_in, w_out, w_out, lane_s, lane_s, lane_c],
        out_specs=[act, w_in, w_in, w_out, w_out, lane_s, lane_s, lane_c],
        out_shape=[jax.ShapeDtypeStruct((t, d), F32),
                   jax.ShapeDtypeStruct((nb, ch, sb), F32), jax.ShapeDtypeStruct((nb, ch, sb), F32),
                   jax.ShapeDtypeStruct((nb, sb, ch), F32), jax.ShapeDtypeStruct((nb, sb, ch), F32),
                   jax.ShapeDtypeStruct((1, nb * sb), F32), jax.ShapeDtypeStruct((1, nb * sb), F32),
                   jax.ShapeDtypeStruct((1, d), F32)],
        scratch_shapes=[pltpu.VMEM((seq, sb), F32), pltpu.VMEM((seq, sb), F32)],
        compiler_params=_params(2),
    )(dz, ypre, h, s_re, s_im, bin_re, bin_im, cout_re, cout_im, abar_re, abar_im, dskip)


ATTN_QUERY_ROWS = 1024


def _softmax_rows(q, k, scale):
    s = _dot(q, k, 1, 1) * scale
    e = jnp.exp(s - jnp.max(s, axis=-1, keepdims=True))
    return e / jnp.sum(e, axis=-1, keepdims=True)


def _attn_fwd(name, q, kv, n_seq, seq, mlen, heads):
    t, d = q.shape
    hd = d // heads
    tq = _pick(seq, ATTN_QUERY_ROWS, 16)
    nq = seq // tq
    scale = hd ** -0.5
    q_spec = pl.BlockSpec((tq, hd), lambda b, h, i: (b * nq + i, h))

    def body(q_ref, k_ref, v_ref, o_ref):
        p = _softmax_rows(q_ref[...], k_ref[...], scale)
        o_ref[...] = _dot(p, v_ref[...], 1, 0).astype(o_ref.dtype)

    return pl.pallas_call(
        body, name=name, grid=(n_seq, heads, nq),
        in_specs=[q_spec, pl.BlockSpec((mlen, hd), lambda b, h, i: (b, h)),
                  pl.BlockSpec((mlen, hd), lambda b, h, i: (b, heads + h))],
        out_specs=q_spec, out_shape=jax.ShapeDtypeStruct((t, d), BF16), compiler_params=_params(3),
    )(q, kv, kv)


def _attn_bwd(name, q, kv, do, n_seq, seq, mlen, heads):
    t, d = q.shape
    hd = d // heads
    tq = _pick(seq, ATTN_QUERY_ROWS, 16)
    nq = seq // tq
    scale = hd ** -0.5
    q_spec = pl.BlockSpec((tq, hd), lambda b, h, i: (b * nq + i, h))
    k_spec = pl.BlockSpec((mlen, hd), lambda b, h, i: (b, h))

    def body(q_ref, k_ref, v_ref, do_ref, dq_ref, dk_ref, dv_ref):
        q, k, v, do = q_ref[...], k_ref[...], v_ref[...], do_ref[...]
        p = _softmax_rows(q, k, scale)
        dp = _dot(do, v, 1, 1)
        ds = p * (dp - jnp.sum(dp * p, axis=-1, keepdims=True)) * scale
        dq_ref[...] = _dot(ds, k, 1, 0).astype(dq_ref.dtype)

        @pl.when(pl.program_id(2) == 0)
        def _():
            dk_ref[...] = jnp.zeros_like(dk_ref)
            dv_ref[...] = jnp.zeros_like(dv_ref)

        dk_ref[...] += _dot(ds, q, 0, 0)
        dv_ref[...] += _dot(p, do, 0, 0)

    return pl.pallas_call(
        body, name=name, grid=(n_seq, heads, nq),
        in_specs=[q_spec, k_spec, pl.BlockSpec((mlen, hd), lambda b, h, i: (b, heads + h)), q_spec],
        out_specs=[q_spec, k_spec, k_spec],
        out_shape=[jax.ShapeDtypeStruct((t, d), BF16), jax.ShapeDtypeStruct((n_seq * mlen, d), F32),
                   jax.ShapeDtypeStruct((n_seq * mlen, d), F32)],
        compiler_params=_params(3),
    )(q, kv, kv, do)


ADAMW_BLOCK_ELEMS = 128 * 1024


def _adamw(name, parts, w, m, v, first_layer=0, earlier=None):
    n_layers = len(parts)
    _, r, c = parts[0].shape
    assert w.shape[0] % r == 0 and w.shape[1] == c and first_layer + n_layers <= w.shape[0] // r, (name, w.shape)
    tr = _pick(r, max(16, ADAMW_BLOCK_ELEMS // c // 16 * 16), 8)
    nt = r // tr
    spec = pl.BlockSpec((tr, c), lambda l, i: ((first_layer + l) * nt + i, 0))
    c1 = 1.0 - ADAM_B1 ** ADAM_STEP
    c2 = 1.0 - ADAM_B2 ** ADAM_STEP

    def parts_spec(q):
        return pl.BlockSpec((N_DEV, tr, c), lambda l, i: (0, jnp.where(l == q, i, jnp.where(l > q, nt - 1, 0)), 0))

    earlier = list(earlier or ())

    def body(*refs):
        p_refs = refs[:n_layers]
        w_ref, m_ref, v_ref = refs[n_layers:n_layers + 3]
        g_ref, d_ref, nm_ref, nv_ref = refs[n_layers + 3 + len(earlier):]

        def update(p_ref):
            g = p_ref[0].astype(F32)
            for k in range(1, N_DEV):
                g = g + p_ref[k].astype(F32)
            nm = ADAM_B1 * m_ref[...] + (1.0 - ADAM_B1) * g
            nv = ADAM_B2 * v_ref[...] + (1.0 - ADAM_B2) * (g * g)
            g_ref[...] = g
            nm_ref[...] = nm
            nv_ref[...] = nv
            d_ref[...] = -ADAM_LR * ((nm / c1) / (jnp.sqrt(nv / c2) + ADAM_EPS) + ADAM_WD * w_ref[...])

        for q in range(n_layers):
            pl.when(pl.program_id(0) == q)(lambda q=q: update(p_refs[q]))

    out = jax.ShapeDtypeStruct(w.shape, F32)
    return pl.pallas_call(
        body, name=name, grid=(n_layers, nt),
        in_specs=[parts_spec(q) for q in range(n_layers)] + [spec] * 3 + [ANY_SPEC] * len(earlier),
        out_specs=[spec] * 4, out_shape=[out] * 4, compiler_params=_params(2),
        input_output_aliases={n_layers + 3 + q: q for q in range(len(earlier))},
    )(*parts, w, m, v, *earlier)


def _place():
    x, y, c = lax.axis_index("x"), lax.axis_index("y"), lax.axis_index("c")
    return x, y, c


def _index(px, py, pc):
    return 4 * px + 2 * py + pc


def _all_gather(name, shards):
    n = len(shards)

    def body(*refs):
        in_refs, out_refs = refs[:n], refs[n:2 * n]
        send_sems, recv_sems, local_sems = refs[2 * n:]
        x, y, c = _place()
        me, sibling = (x, y, c), (x, y, 1 - c)
        chips = [(1 - x, y), (x, 1 - y), (1 - x, 1 - y)]

        def slot(k, block):
            return out_refs[k].at[_index(*block)]

        def copy(k, j, block, to, src=None):
            return pltpu.make_async_remote_copy(
                src_ref=slot(k, block) if src is None else src, dst_ref=slot(k, block),
                send_sem=send_sems.at[7 * k + j], recv_sem=recv_sems.at[7 * k + j], device_id=to, device_id_type=MESH)

        mine = [pltpu.make_async_copy(in_refs[k], slot(k, me), local_sems.at[k]) for k in range(n)]
        for cp in mine:
            cp.start()
        first = []
        for k in range(n):
            first.append(copy(k, 0, me, sibling, src=in_refs[k]))
            first += [copy(k, 1 + j, me, (*chip, c), src=in_refs[k]) for j, chip in enumerate(chips)]
        for cp in first:
            cp.start()
        passed = []
        for j, chip in enumerate(chips):
            for k in range(n):
                copy(k, 1 + j, (*chip, c), me).wait_recv()
                cp = copy(k, 4 + j, (*chip, c), sibling)
                cp.start()
                passed.append(cp)
        for k in range(n):
            copy(k, 0, sibling, me).wait_recv()
            for j, chip in enumerate(chips):
                copy(k, 4 + j, (*chip, 1 - c), me).wait_recv()
        for cp in first + passed:
            cp.wait_send()
        for cp in mine:
            cp.wait()

    return pl.pallas_call(
        body, name=name, in_specs=[HBM_SPEC] * n, out_specs=[HBM_SPEC] * n,
        out_shape=[jax.ShapeDtypeStruct((N_DEV,) + s.shape, s.dtype) for s in shards],
        scratch_shapes=[pltpu.SemaphoreType.DMA((7 * n,)), pltpu.SemaphoreType.DMA((7 * n,)),
                        pltpu.SemaphoreType.DMA((n,))],
    )(*shards)


WHOLE, BLOCK, COLUMNS = "whole", "block", "columns"


def _slot(ref, index, mode):
    if mode == WHOLE:
        return ref
    if mode == BLOCK:
        return ref.at[index]
    width = ref.shape[1] // N_DEV
    return ref.at[:, pl.ds(pl.multiple_of(index * width, width), width)]


DIRECT = tuple(range(1, N_DEV))
CHIPS = (1, 4, 2, 6)
FORWARD = "forward"


def _copies(plan, x, y, c):
    def xor(r, flip_core=False):
        rx, ry, rc = (r >> 2) & 1, (r >> 1) & 1, (r & 1) ^ int(flip_core)
        return (1 - x if rx else x, 1 - y if ry else y, 1 - c if rc else c)

    me = _index(x, y, c)
    if plan == FORWARD:
        return [(xor(1), _index(*xor(r)), _index(*xor(r)), _index(*xor(r, True))) for r in (4, 2, 6)]
    return [(xor(r), _index(*xor(r)), me, _index(*xor(r))) for r in plan]


def _exchange_start(name, srcs, lands, src_modes, land_modes, deps=(), plan=DIRECT):
    n, n_src = len(lands), len(srcs)
    n_copies = 3 if plan == FORWARD else len(plan)

    def body(*refs):
        land_refs = refs[n_src:n_src + n]
        src_refs = refs[:n_src] if n_src else land_refs
        send_sems, recv_sems = refs[n_src + n + len(deps)], refs[n_src + n + len(deps) + 1]
        token = refs[-1]
        x, y, c = _place()
        for k in range(n):
            for j, (peer, src_index, dst_index, _) in enumerate(_copies(plan, x, y, c)):
                pltpu.make_async_remote_copy(
                    src_ref=_slot(src_refs[k], src_index, src_modes[k]),
                    dst_ref=_slot(land_refs[k], dst_index, land_modes[k]), send_sem=send_sems.at[n_copies * k + j],
                    recv_sem=recv_sems.at[n_copies * k + j], device_id=peer, device_id_type=MESH).start()
        token[...] = jnp.zeros_like(token)

    arrays = list(srcs) + list(lands)
    thru = [pltpu.HBM(a.shape, a.dtype) for a in arrays]
    out = pl.pallas_call(
        body, name=name,
        out_shape=(pltpu.SemaphoreType.DMA((n_copies * n,)), pltpu.SemaphoreType.DMA((n_copies * n,)), *thru,
                   jax.ShapeDtypeStruct((8, V7X_LANES), F32)),
        in_specs=[HBM_SPEC] * len(arrays) + [ANY_SPEC] * len(deps),
        out_specs=(SEM_SPEC, SEM_SPEC, *([HBM_SPEC] * len(arrays)), pl.BlockSpec(memory_space=pltpu.VMEM)),
        input_output_aliases={k: 2 + k for k in range(len(arrays))},
        compiler_params=pltpu.CompilerParams(has_side_effects=pltpu.SideEffectType.DATAFLOW_SIDE_EFFECTING),
    )(*[pltpu.with_memory_space_constraint(a, pltpu.HBM) for a in arrays], *deps)
    return out[0], out[1], list(out[2:2 + n_src]), list(out[2 + n_src:2 + n_src + n]), out[-1]


def _exchange_wait(name, send_sems, recv_sems, srcs, lands, src_modes, land_modes, after, plan=DIRECT):
    n, n_src = len(lands), len(srcs)
    n_copies = 3 if plan == FORWARD else len(plan)

    def body(*refs):
        land_refs = refs[n_src:n_src + n]
        src_refs = refs[:n_src] if n_src else land_refs
        send_sems, recv_sems = refs[n_src + n], refs[n_src + n + 1]
        x, y, c = _place()
        for k in range(n):
            for j, (peer, src_index, _, arrival_index) in enumerate(_copies(plan, x, y, c)):
                cp = pltpu.make_async_remote_copy(
                    src_ref=_slot(src_refs[k], src_index, src_modes[k]),
                    dst_ref=_slot(land_refs[k], arrival_index, land_modes[k]), send_sem=send_sems.at[n_copies * k + j],
                    recv_sem=recv_sems.at[n_copies * k + j], device_id=peer, device_id_type=MESH)
                cp.wait_send()
                cp.wait_recv()

    arrays = list(srcs) + list(lands)
    thru = [pltpu.HBM(a.shape, a.dtype) for a in arrays]
    out = pl.pallas_call(
        body, name=name, out_shape=tuple(thru),
        in_specs=[HBM_SPEC] * len(arrays) + [SEM_SPEC, SEM_SPEC, ANY_SPEC], out_specs=tuple([HBM_SPEC] * len(arrays)),
        input_output_aliases={k: k for k in range(len(arrays))},
        compiler_params=pltpu.CompilerParams(has_side_effects=pltpu.SideEffectType.DATAFLOW_SIDE_EFFECTING),
    )(*arrays, send_sems, recv_sems, after)
    return list(out[n_src:])


def _landing(shard, me, mode=BLOCK):
    if mode == COLUMNS:
        k, n = shard.shape
        return lax.dynamic_update_slice(lax.empty((k, N_DEV * n), shard.dtype), shard, (0, me * n))
    zone = lax.empty((N_DEV,) + shard.shape, shard.dtype)
    return lax.dynamic_update_slice(zone, shard[None], (me,) + (0,) * shard.ndim)


def _cols_whole(w):
    return jnp.transpose(w, (1, 0, 2)).reshape(w.shape[1], N_DEV * w.shape[2])


def _rows_whole(w):
    return w.reshape(N_DEV * w.shape[1], w.shape[2])


def _cols_parts(dw):
    k, n8 = dw.shape
    return jnp.transpose(dw.reshape(k, N_DEV, n8 // N_DEV), (1, 0, 2))


def _rows_parts(dw):
    r8, c = dw.shape
    return dw.reshape(N_DEV, r8 // N_DEV, c)


def _pack_rows(arrays):
    rows = []
    for a in arrays:
        flat = a.reshape(-1).astype(F32)
        flat = jnp.pad(flat, [(0, (-flat.shape[0]) % PACK_TILE)])
        rows.append(flat.reshape(-1, V7X_LANES))
    return jnp.concatenate(rows, axis=0)


def _unpack_rows(packed, shapes):
    out, row = [], 0
    for s in shapes:
        size = math.prod(s)
        n_rows = -(-size // PACK_TILE) * 8
        out.append(packed[row:row + n_rows].reshape(-1)[:size].reshape(s))
        row += n_rows
    return out


def _merge2d(a):
    return a.reshape(-1, a.shape[-1])


def _ffn_fwd(tag, x, g, w_up, w_down, deps=()):
    f = w_down.shape[0]
    gu, act, n = _mm(f"{tag}_up", x, w_up, prologue=_rms_rows, prologue_pars=[g], deps=deps,
                     epilogue=_swiglu_tiles, sections=2, outs=[(2 * f, BF16), (f, BF16)])
    out = _mm(f"{tag}_down", act, w_down, res=x, scale=0.5, out_dtype=F32)
    return out, (x, n, gu, act)


def _ffn_bwd(tag, dres, saved, g, w_up, w_down, deps=()):
    x, n, gu, act = saved
    dres32, dres16 = dres
    f = w_down.shape[0]
    dgu = _mm(f"{tag}_down_dx", dres16, w_down, tb=True, scale=0.5, deps=deps, epilogue=_swiglu_bwd_tiles,
              sections=1, row_ins=[(gu, 0, f), (gu, 1, f)], outs=[(2 * f, BF16)])[0]
    d_down = _mm(f"{tag}_down_dw", act, dres16, ta=True, scale=0.5)
    d_up = _mm(f"{tag}_up_dw", n, dgu, ta=True)
    dx, dg = _mm_rms_bwd(f"{tag}_up_dx", dgu, w_up, x, g, dres32)
    return dx, dg, d_up, d_down


def _conv_mixer_fwd(tag, x, g, w_in, w_conv, w_out, n_seq, seq):
    cbv, h = _mm(f"{tag}_in", x, w_in, prologue=_rms_rows, prologue_pars=[g])
    z = _conv_fwd(f"{tag}_conv", cbv, w_conv, n_seq, seq)
    out = _mm(f"{tag}_out", z, w_out, res=x, out_dtype=F32)
    return out, (x, h, cbv, z)


def _conv_mixer_bwd(tag, dres, saved, g, w_in, w_conv, w_out, n_seq, seq):
    x, h, cbv, z = saved
    dres32, dres16 = dres
    dz = _mm(f"{tag}_out_dx", dres16, w_out, tb=True)
    d_out = _mm(f"{tag}_out_dw", z, dres16, ta=True)
    dc, db, dv, d_conv = _conv_bwd(f"{tag}_conv_bwd", dz, cbv, w_conv, n_seq, seq)
    dcbv = jnp.concatenate([dc, db, dv], axis=1)
    d_in = _mm(f"{tag}_in_dw", h, dcbv, ta=True)
    dx, dg = _mm_rms_bwd(f"{tag}_in_dx", dcbv, w_in, x, g, dres32)
    return dx, dg, d_in, d_conv, d_out


def _s5_mixer_fwd(tag, x, g, ssm, dskip, w_glu, n_seq, seq):
    a_re, a_im, log_dt, b_re, b_im, c_re, c_im = ssm
    groups, p, hh = b_re.shape
    gb = S5_CHANNELS // hh
    disc, disc_vjp = jax.vjp(_s5_discretize, a_re, a_im, log_dt, b_re, b_im)
    abar_re, abar_im, bbar_re, bbar_im = disc
    mats = (_block_diag_in(bbar_re, gb).astype(BF16), _block_diag_in(bbar_im, gb).astype(BF16),
            _block_diag_out(c_re, gb).astype(BF16), _block_diag_out(c_im, gb).astype(BF16),
            abar_re.reshape(1, groups * p), abar_im.reshape(1, groups * p), dskip)
    d = x.shape[1]
    h = _rms_fwd(f"{tag}_norm", x, g)
    s_re, s_im, ypre, z = _s5_fwd(f"{tag}_scan", h, *mats, n_seq, seq)
    vg = _mm(f"{tag}_glu", z, w_glu)
    out = _glu_res(f"{tag}_glu_act", vg, x, d)
    return out, (x, h, s_re, s_im, ypre, z, vg, mats, disc_vjp, (groups, p, hh, gb))


def _s5_mixer_bwd(tag, dres, saved, g, w_glu, n_seq, seq):
    x, h, s_re, s_im, ypre, z, vg, mats, disc_vjp, (groups, p, hh, gb) = saved
    d = x.shape[1]
    dres32, _ = dres
    dvg = _glu_bwd(f"{tag}_glu_act_bwd", dres32, vg, d)
    d_glu = _mm(f"{tag}_glu_dw", z, dvg, ta=True)
    dz = _mm(f"{tag}_glu_dx", dvg, w_glu, tb=True)
    dh, dbin_re, dbin_im, dcout_re, dcout_im, dabar_re, dabar_im, d_skip = _s5_bwd(
        f"{tag}_scan_bwd", dz, ypre, h, s_re, s_im, *mats, n_seq, seq)
    d_are, d_aim, d_logdt, d_bre, d_bim = disc_vjp((
        dabar_re.reshape(groups, p), dabar_im.reshape(groups, p),
        _block_diag_in_t(dbin_re, gb, p, hh), _block_diag_in_t(dbin_im, gb, p, hh)))
    d_cre = _block_diag_out_t(dcout_re, gb, p, hh)
    d_cim = _block_diag_out_t(dcout_im, gb, p, hh)
    dx, dg = _rms_bwd(f"{tag}_norm_bwd", x, g, dh, dres32)
    return dx, dg, (d_are, d_aim, d_logdt, d_bre, d_bim, d_cre, d_cim), d_skip, d_glu


def _xattn_fwd(tag, x, mem, g_q, g_mem, w_q, w_kv, w_o, n_seq, seq, mlen, heads):
    q, n = _mm(f"{tag}_q", x, w_q, prologue=_rms_rows, prologue_pars=[g_q])
    mem_n = _rms_fwd(f"{tag}_mem_norm", mem, g_mem)
    kv = _mm(f"{tag}_kv", mem_n, w_kv)
    o = _attn_fwd(f"{tag}_attn", q, kv, n_seq, seq, mlen, heads)
    out = _mm(f"{tag}_o", o, w_o, res=x, out_dtype=F32)
    return out, (x, n, q, mem_n, kv, o)


def _xattn_bwd(tag, dres, saved, mem, g_q, g_mem, w_q, w_kv, w_o, n_seq, seq, mlen, heads):
    x, n, q, mem_n, kv, o = saved
    dres32, dres16 = dres
    do = _mm(f"{tag}_o_dx", dres16, w_o, tb=True)
    d_o = _mm(f"{tag}_o_dw", o, dres16, ta=True)
    dq, dk, dv = _attn_bwd(f"{tag}_attn_bwd", q, kv, do, n_seq, seq, mlen, heads)
    dkv = jnp.concatenate([dk, dv], axis=1)
    d_q = _mm(f"{tag}_q_dw", n, dq, ta=True)
    d_kv = _mm(f"{tag}_kv_dw", mem_n, dkv, ta=True)
    dmem_n = _mm(f"{tag}_kv_dx", dkv, w_kv, tb=True)
    _, dg_mem = _rms_bwd(f"{tag}_mem_norm_bwd", mem, g_mem, dmem_n)
    dx, dg_q = _mm_rms_bwd(f"{tag}_q_dx", dq, w_q, x, g_q, dres32)
    return dx, dg_q, dg_mem, d_q, d_kv, d_o


WEIGHT_NAMES = ("norm_g", "final_g", "ffn1_up", "ffn1_down", "ffn2_up", "ffn2_down", "conv_w_in", "conv_w",
                "conv_w_out", "ssm_a_re", "ssm_a_im", "ssm_log_dt", "ssm_b_re", "ssm_b_im", "ssm_c_re", "ssm_c_im",
                "ssm_d", "ssm_w_glu", "xa_w_q", "xa_w_kv", "xa_w_o")
MATRICES = ("ffn1_up", "ffn1_down", "ffn2_up", "ffn2_down", "conv_w_in", "conv_w_out", "ssm_w_glu", "xa_w_q",
            "xa_w_kv", "xa_w_o")
COLUMN_SHARDED = ("ffn1_up", "ffn2_up", "conv_w_in", "ssm_w_glu", "xa_w_kv")
SMALL_SHARDED = ("norm_g", "conv_w", "ssm_d")
REPLICATED = ("ssm_a_re", "ssm_a_im", "ssm_log_dt", "ssm_b_re", "ssm_b_im", "ssm_c_re", "ssm_c_im", "final_g")


def kernel(x, mem, norm_g, final_g, ffn1_up, ffn1_down, ffn2_up, ffn2_down, conv_w_in, conv_w, conv_w_out, ssm_a_re, ssm_a_im, ssm_log_dt, ssm_b_re, ssm_b_im, ssm_c_re, ssm_c_im, ssm_d, ssm_w_glu, xa_w_q, xa_w_kv, xa_w_o, loss_target, m_norm_g, m_final_g, m_ffn1_up, m_ffn1_down, m_ffn2_up, m_ffn2_down, m_conv_w_in, m_conv_w, m_conv_w_out, m_ssm_a_re, m_ssm_a_im, m_ssm_log_dt, m_ssm_b_re, m_ssm_b_im, m_ssm_c_re, m_ssm_c_im, m_ssm_d, m_ssm_w_glu, m_xa_w_q, m_xa_w_kv, m_xa_w_o, v_norm_g, v_final_g, v_ffn1_up, v_ffn1_down, v_ffn2_up, v_ffn2_down, v_conv_w_in, v_conv_w, v_conv_w_out, v_ssm_a_re, v_ssm_a_im, v_ssm_log_dt, v_ssm_b_re, v_ssm_b_im, v_ssm_c_re, v_ssm_c_im, v_ssm_d, v_ssm_w_glu, v_xa_w_q, v_xa_w_kv, v_xa_w_o):
    w = dict(norm_g=norm_g, final_g=final_g, ffn1_up=ffn1_up, ffn1_down=ffn1_down, ffn2_up=ffn2_up,
             ffn2_down=ffn2_down, conv_w_in=conv_w_in, conv_w=conv_w, conv_w_out=conv_w_out, ssm_a_re=ssm_a_re,
             ssm_a_im=ssm_a_im, ssm_log_dt=ssm_log_dt, ssm_b_re=ssm_b_re, ssm_b_im=ssm_b_im, ssm_c_re=ssm_c_re,
             ssm_c_im=ssm_c_im, ssm_d=ssm_d, ssm_w_glu=ssm_w_glu, xa_w_q=xa_w_q, xa_w_kv=xa_w_kv, xa_w_o=xa_w_o)
    mom = dict(norm_g=m_norm_g, final_g=m_final_g, ffn1_up=m_ffn1_up, ffn1_down=m_ffn1_down, ffn2_up=m_ffn2_up,
               ffn2_down=m_ffn2_down, conv_w_in=m_conv_w_in, conv_w=m_conv_w, conv_w_out=m_conv_w_out,
               ssm_a_re=m_ssm_a_re, ssm_a_im=m_ssm_a_im, ssm_log_dt=m_ssm_log_dt, ssm_b_re=m_ssm_b_re,
               ssm_b_im=m_ssm_b_im, ssm_c_re=m_ssm_c_re, ssm_c_im=m_ssm_c_im, ssm_d=m_ssm_d, ssm_w_glu=m_ssm_w_glu,
               xa_w_q=m_xa_w_q, xa_w_kv=m_xa_w_kv, xa_w_o=m_xa_w_o)
    var = dict(norm_g=v_norm_g, final_g=v_final_g, ffn1_up=v_ffn1_up, ffn1_down=v_ffn1_down, ffn2_up=v_ffn2_up,
               ffn2_down=v_ffn2_down, conv_w_in=v_conv_w_in, conv_w=v_conv_w, conv_w_out=v_conv_w_out,
               ssm_a_re=v_ssm_a_re, ssm_a_im=v_ssm_a_im, ssm_log_dt=v_ssm_log_dt, ssm_b_re=v_ssm_b_re,
               ssm_b_im=v_ssm_b_im, ssm_c_re=v_ssm_c_re, ssm_c_im=v_ssm_c_im, ssm_d=v_ssm_d, ssm_w_glu=v_ssm_w_glu,
               xa_w_q=v_xa_w_q, xa_w_kv=v_xa_w_kv, xa_w_o=v_xa_w_o)

    n_seq, seq, d = x.shape
    mlen = mem.shape[1]
    depth, n_norms = norm_g.shape[0], norm_g.shape[1]
    heads = 4
    tokens = n_seq * seq
    x2 = x.reshape(tokens, d)
    mem2 = mem.reshape(n_seq * mlen, d)
    tgt2 = loss_target.reshape(tokens, d)

    small_shapes = [w[k].shape for k in SMALL_SHARDED]
    small_rows = [_merge2d(w[k]) for k in SMALL_SHARDED]
    small_counts = [s.shape[0] for s in small_rows]
    small = jnp.concatenate(small_rows, axis=0)
    small = jnp.pad(small, [(0, (-small.shape[0]) % 8), (0, 0)])
    me = _index(*_place())

    def layer_weights(i):
        names = [(k, i) for k in ("ffn1_up", "ffn1_down", "ffn2_up", "ffn2_down", "xa_w_q", "xa_w_kv", "xa_w_o")]
        return names + ([("conv_w_in", i // 2), ("conv_w_out", i // 2)] if i % 2 == 0 else [("ssm_w_glu", i // 2)])

    shards = [[w[k][idx].astype(BF16) for k, idx in layer_weights(i)] for i in range(depth)]
    gathered = _all_gather("gather_layer0", shards[0] + [small])
    small_all = gathered[-1]
    blocks = [gathered[:-1]] + [None] * (depth - 1)
    in_flight = [None] * depth
    token = gathered[0]
    in_place = {k for k in COLUMN_SHARDED if w[k].shape[-1] % V7X_LANES == 0}

    def modes(i):
        return [COLUMNS if k in in_place else BLOCK for k, _ in layer_weights(i)]

    for i in range(1, depth):
        zones = [_landing(s, me, mode) for s, mode in zip(shards[i], modes(i))]
        whole_src = [WHOLE] * len(zones)
        *in_flight[i], token = _exchange_start(f"gather_start_l{i}", shards[i], zones, whole_src, modes(i),
                                               deps=[token], plan=CHIPS)
    passing = [None] * depth

    def small_whole(idx):
        start = sum(small_counts[:idx])
        part = small_all[:, start:start + small_counts[idx]]
        lead = small_shapes[idx][:-1]
        part = part.reshape((N_DEV,) + lead + (part.shape[-1],))
        part = jnp.moveaxis(part, 0, -2)
        return part.reshape(lead + (N_DEV * part.shape[-1],))

    norm_all = small_whole(0)
    conv_all = small_whole(1)
    dskip_all = small_whole(2)

    def whole(i):
        return {k: _rows_whole(blk) if k not in COLUMN_SHARDED else blk if i > 0 and k in in_place else _cols_whole(blk)
                for (k, _), blk in zip(layer_weights(i), blocks[i])}

    saved = []
    cur = x2
    for i in range(depth):
        g = [norm_all[i, k].reshape(1, d) for k in range(n_norms)]
        j = i // 2
        if i > 0:
            blocks[i] = _exchange_wait(f"forward_wait_l{i}", *passing[i], modes(i), modes(i), cur, plan=FORWARD)
        lw = whole(i)
        cur, s_ffn1 = _ffn_fwd(f"l{i}_ffn1", cur, g[0], lw["ffn1_up"], lw["ffn1_down"], [token] if i == 0 else ())
        if i % 2 == 0:
            lw["conv_w"] = conv_all[j]
            cur, s_mix = _conv_mixer_fwd(f"l{i}_conv", cur, g[1], lw["conv_w_in"], lw["conv_w"], lw["conv_w_out"],
                                         n_seq, seq)
        else:
            ssm = tuple(w[k][j] for k in ("ssm_a_re", "ssm_a_im", "ssm_log_dt", "ssm_b_re", "ssm_b_im",
                                          "ssm_c_re", "ssm_c_im"))
            cur, s_mix = _s5_mixer_fwd(f"l{i}_s5", cur, g[1], ssm, dskip_all[j].reshape(1, d), lw["ssm_w_glu"],
                                       n_seq, seq)
        cur, s_xa = _xattn_fwd(f"l{i}_xa", cur, mem2, g[2], g[3], lw["xa_w_q"], lw["xa_w_kv"], lw["xa_w_o"],
                               n_seq, seq, mlen, heads)
        deps = ()
        if i + 1 < depth:
            nxt = modes(i + 1)
            landed = _exchange_wait(f"gather_wait_l{i + 1}", *in_flight[i + 1], [WHOLE] * len(nxt), nxt, cur,
                                    plan=CHIPS)
            *passing[i + 1], forward_token = _exchange_start(f"forward_start_l{i + 1}", [], landed, nxt, nxt,
                                                             plan=FORWARD)
            deps = [forward_token]
        cur, s_ffn2 = _ffn_fwd(f"l{i}_ffn2", cur, g[4], lw["ffn2_up"], lw["ffn2_down"], deps)
        saved.append((g, lw, s_ffn1, s_mix, s_xa, s_ffn2))

    dres, err2, d_final = _final_loss("loss_head", cur, final_g.reshape(1, d), tgt2)
    loss = lax.psum(0.5 * jnp.sum(err2) / d, ("x", "y", "c"))

    d_norm = [[None] * n_norms for _ in range(depth)]
    d_conv = [None] * conv_w.shape[0]
    d_skip = [None] * ssm_d.shape[0]
    d_ssm = [None] * ssm_a_re.shape[0]
    leaving = [None] * depth
    deps = ()

    def leave(name, keys, gm, extra=()):
        srcs, src_modes, zones = [], [], []
        for k in keys:
            if k in in_place:
                rows, n = gm[k].shape[0], gm[k].shape[1] // N_DEV
                srcs.append(gm[k])
                src_modes.append(COLUMNS)
                zones.append(_landing(lax.dynamic_slice(gm[k], (0, me * n), (rows, n)), me))
            else:
                srcs.append(_cols_parts(gm[k]) if k in COLUMN_SHARDED else _rows_parts(gm[k]))
                src_modes.append(BLOCK)
                zones.append(_landing(lax.dynamic_index_in_dim(srcs[-1], me, 0, keepdims=False), me))
        for p in extra:
            srcs.append(p)
            src_modes.append(BLOCK)
            zones.append(_landing(lax.dynamic_index_in_dim(p, me, 0, keepdims=False), me))
        land_modes = [BLOCK] * len(srcs)
        *handles, token = _exchange_start(name, srcs, zones, src_modes, land_modes)
        return (*handles, src_modes, land_modes), token

    def small_parts(full):
        lead = full.shape[:-1]
        t = full.reshape(lead + (N_DEV, full.shape[-1] // N_DEV))
        t = jnp.moveaxis(t, -2, 0)
        return t.reshape(N_DEV, -1, t.shape[-1])

    for i in reversed(range(depth)):
        g, lw, s_ffn1, s_mix, s_xa, s_ffn2 = saved[i]
        j = i // 2
        gm = {}
        dres, d_norm[i][4], gm["ffn2_up"], gm["ffn2_down"] = _ffn_bwd(
            f"l{i}_ffn2", dres, s_ffn2, g[4], lw["ffn2_up"], lw["ffn2_down"], deps)
        dres, d_norm[i][2], d_norm[i][3], gm["xa_w_q"], gm["xa_w_kv"], gm["xa_w_o"] = _xattn_bwd(
            f"l{i}_xa", dres, s_xa, mem2, g[2], g[3], lw["xa_w_q"], lw["xa_w_kv"], lw["xa_w_o"], n_seq, seq, mlen,
            heads)
        if i % 2 == 0:
            dres, d_norm[i][1], gm["conv_w_in"], d_conv[j], gm["conv_w_out"] = _conv_mixer_bwd(
                f"l{i}_conv", dres, s_mix, g[1], lw["conv_w_in"], lw["conv_w"], lw["conv_w_out"], n_seq, seq)
        else:
            dres, d_norm[i][1], d_ssm[j], d_skip[j], gm["ssm_w_glu"] = _s5_mixer_bwd(
                f"l{i}_s5", dres, s_mix, g[1], lw["ssm_w_glu"], n_seq, seq)
        upper, token = leave(f"grads_start_l{i}_upper", [k for k, _ in layer_weights(i)[2:]], gm)
        dres, d_norm[i][0], gm["ffn1_up"], gm["ffn1_down"] = _ffn_bwd(
            f"l{i}_ffn1", dres, s_ffn1, g[0], lw["ffn1_up"], lw["ffn1_down"], [token])
        extra = []
        if i == 0:
            d_norm_all = jnp.stack([jnp.concatenate(row, axis=0) for row in d_norm])
            small_g = jnp.concatenate(
                [small_parts(a) for a in (d_norm_all, jnp.stack(d_conv), jnp.concatenate(d_skip, axis=0))], axis=1)
            extra = [jnp.pad(small_g, [(0, 0), (0, (-small_g.shape[1]) % 8), (0, 0)])]
        lower, token = leave(f"grads_start_l{i}_lower", ["ffn1_up", "ffn1_down"], gm, extra)
        leaving[i] = (upper, lower)
        deps = [token]
        if i == min(1, depth - 1):
            rep_grads = [jnp.stack([d_ssm[j][k] for j in range(len(d_ssm))]) for k in range(7)]
            rep_packed = _pack_rows(rep_grads + [d_final.reshape(-1)])
            *rep_leaving, token = _exchange_start("replicated_grads_start", [rep_packed], [_landing(rep_packed, me)],
                                                  [WHOLE], [BLOCK])
            deps = deps + [token]
    grad_x = dres[0].reshape(n_seq, seq, d)
    received = {k: [None] * w[k].shape[0] for k in MATRICES}

    def arrive(i, after):
        upper = _exchange_wait(f"grads_wait_l{i}_upper", *leaving[i][0], after)
        lower = _exchange_wait(f"grads_wait_l{i}_lower", *leaving[i][1], after)
        for (k, idx), blk in zip(layer_weights(i), lower[:2] + upper):
            received[k][idx] = blk
        return lower[2:]

    for i in range(1, depth):
        arrive(i, dres[0])
    rep_all = _exchange_wait("replicated_grads_wait", *rep_leaving, [WHOLE], [BLOCK], dres[0])[0]
    rep_shapes = [w[k].shape for k in REPLICATED]
    flat = {k: (_merge2d(w[k]), _merge2d(mom[k]), _merge2d(var[k])) for k in MATRICES}
    late = {k: received[k][0] is None for k in MATRICES}
    early = {}
    for k in MATRICES:
        first = 1 if late[k] else 0
        if first < len(received[k]):
            early[k] = _adamw(f"adamw_{k}_upper", received[k][first:], *flat[k], first_layer=first)
    small_received, = arrive(0, list(early.values())[-1][0] if early else dres[0])
    grads, deltas, new_m, new_v = {}, {}, {}, {}
    for k in MATRICES:
        out = early.get(k)
        if late[k]:
            out = _adamw(f"adamw_{k}_l0", received[k][:1], *flat[k], earlier=out)
        grads[k], deltas[k], new_m[k], new_v[k] = [o.reshape(w[k].shape) for o in out]

    def small_local(src):
        rows = jnp.concatenate([_merge2d(src[k]) for k in SMALL_SHARDED], axis=0)
        return jnp.pad(rows, [(0, (-rows.shape[0]) % 8), (0, 0)])

    out = _adamw("adamw_small", [small_received], small, small_local(mom), small_local(var))
    for res, o in zip((grads, deltas, new_m, new_v), out):
        start = 0
        for k, cnt, shape in zip(SMALL_SHARDED, small_counts, small_shapes):
            res[k] = o[start:start + cnt].reshape(shape)
            start += cnt

    out = _adamw("adamw_replicated", [rep_all], _pack_rows([w[k] for k in REPLICATED]),
                 _pack_rows([mom[k] for k in REPLICATED]), _pack_rows([var[k] for k in REPLICATED]))
    for res, o in zip((grads, deltas, new_m, new_v), out):
        for k, a in zip(REPLICATED, _unpack_rows(o, rep_shapes)):
            res[k] = a

    return (loss, grad_x, *[grads[k] for k in WEIGHT_NAMES], *[deltas[k] for k in WEIGHT_NAMES],
            *[new_m[k] for k in WEIGHT_NAMES], *[new_v[k] for k in WEIGHT_NAMES])
```

```python
import math

import jax
import jax.numpy as jnp
from jax import lax
from jax.experimental import pallas as pl
from jax.experimental.pallas import tpu as pltpu

F32 = jnp.float32
BF16 = jnp.bfloat16
MESH = pl.DeviceIdType.MESH
N_DEV = 8

NORM_EPS = 1e-6
EIG_CLIP = -1e-4
CONV_WIDTH = 3
ADAM_LR = 0.001
ADAM_B1 = 0.9
ADAM_B2 = 0.999
ADAM_EPS = 1e-08
ADAM_WD = 0.01
ADAM_STEP = 10
GELU_C = math.sqrt(2.0 / math.pi)
GELU_A = 0.044715

V7X_LANES = 128
V7X_VMEM_LIMIT = 56 * 1024 * 1024
S5_CHANNELS = 128
PACK_TILE = 8 * V7X_LANES

HBM_SPEC = pl.BlockSpec(memory_space=pltpu.HBM)
ANY_SPEC = pl.BlockSpec(memory_space=pl.ANY)
SEM_SPEC = pl.BlockSpec(memory_space=pltpu.SEMAPHORE)


def _params(n_grid):
    return pltpu.CompilerParams(dimension_semantics=("arbitrary",) * n_grid, vmem_limit_bytes=V7X_VMEM_LIMIT)


def _pick(n, pref, align):
    if n <= pref:
        return n
    t = (pref // align) * align
    while t >= align:
        if n % t == 0:
            return t
        t -= align
    raise ValueError(f"no tile for {n} (pref {pref}, align {align})")


MM_RHS_BLOCK_BYTES = 12 * 1024 * 1024
MM_LHS_BLOCK_BYTES = 6 * 1024 * 1024
MM_ACC_BYTES = 6 * 1024 * 1024
MM_ROWS = 512


MM_EPILOGUE_ROWS = 256


def _mm_tiles(m, k, n, a_item, b_item, ta, max_rows):
    tn = _pick(n, max(V7X_LANES, MM_RHS_BLOCK_BYTES // (k * b_item)), V7X_LANES)
    rows = min(max_rows, MM_ACC_BYTES // (4 * tn), MM_LHS_BLOCK_BYTES // (k * a_item))
    align = V7X_LANES if ta else 16
    tm = _pick(m, max(align, rows), align)
    return tm, tn


def _store_results(out_refs, n_row, results, first):
    if not isinstance(results, (tuple, list)):
        results = (results,)
    for o, v in zip(out_refs[:n_row], results[:n_row]):
        if isinstance(v, (tuple, list)):
            off = 0
            for piece in v:
                w = piece.shape[1]
                o[:, off:off + w] = piece.astype(o.dtype)
                off += w
        else:
            o[...] = v.astype(o.dtype)
    if len(out_refs) > n_row:
        @pl.when(first)
        def _():
            for o in out_refs[n_row:]:
                o[...] = jnp.zeros_like(o)

        for o, v in zip(out_refs[n_row:], results[n_row:]):
            o[...] += v


def _mm(name, a, b, *, ta=False, tb=False, out_dtype=BF16, res=None, scale=None, deps=(),
        epilogue=None, row_ins=(), par_ins=(), outs=(), acc_outs=(), prologue=None, prologue_pars=()):
    if ta:
        k, m = a.shape
    else:
        m, k = a.shape
    if tb:
        n, k2 = b.shape
    else:
        k2, n = b.shape
    assert k == k2, (name, a.shape, b.shape)
    max_rows = MM_ROWS if epilogue is None else MM_EPILOGUE_ROWS
    tm, tn = _mm_tiles(m, k, n, a.dtype.itemsize, b.dtype.itemsize, ta, max_rows)
    a_spec = pl.BlockSpec((k, tm), lambda j, i: (0, i)) if ta else pl.BlockSpec((tm, k), lambda j, i: (i, 0))
    b_spec = pl.BlockSpec((tn, k), lambda j, i: (j, 0)) if tb else pl.BlockSpec((k, tn), lambda j, i: (0, j))
    o_spec = pl.BlockSpec((tm, tn), lambda j, i: (i, j))
    dims = (((0 if ta else 1,), (1 if tb else 0,)), ((), ()))
    has_res = res is not None
    ins = [a, b] + ([res] if has_res else [])
    specs = [a_spec, b_spec] + ([o_spec] if has_res else [])
    n_mm = len(ins)
    if epilogue is None:
        out_specs, out_shape = [o_spec], [jax.ShapeDtypeStruct((m, n), out_dtype)]
    else:
        assert tn == n, (name, tn, n)
        for r in row_ins:
            arr, cb, cw = r if isinstance(r, tuple) else (r, 0, r.shape[1])
            assert arr.shape[0] == m, (name, arr.shape, m)
            ins.append(arr)
            specs.append(pl.BlockSpec((tm, cw), lambda j, i, cb=cb: (i, cb)))
        for p in par_ins:
            ins.append(p)
            specs.append(pl.BlockSpec(p.shape, lambda j, i: (0, 0)))
        out_specs = [pl.BlockSpec((tm, c), lambda j, i: (i, 0)) for c, _ in outs]
        out_specs += [pl.BlockSpec((r, c), lambda j, i: (0, 0)) for r, c in acc_outs]
        out_shape = [jax.ShapeDtypeStruct((m, c), dt) for c, dt in outs]
        out_shape += [jax.ShapeDtypeStruct((r, c), F32) for r, c in acc_outs]
    n_in = len(ins)
    if prologue is not None:
        assert tn == n and not ta, (name, tn, n, ta)
        for p in prologue_pars:
            ins.append(p)
            specs.append(pl.BlockSpec(p.shape, lambda j, i: (0, 0)))
        out_specs = out_specs + [a_spec]
        out_shape = out_shape + [jax.ShapeDtypeStruct((m, k), BF16)]
    n_pro = len(ins)
    ins += list(deps)
    specs += [ANY_SPEC] * len(deps)

    def body(*refs):
        a_ref, b_ref = refs[0], refs[1]
        out_refs = refs[n_pro + len(deps):]
        if prologue is None:
            lhs = a_ref[...].astype(BF16)
        else:
            lhs = prologue(a_ref[...], *[r[...] for r in refs[n_in:n_pro]]).astype(BF16)
            out_refs[-1][...] = lhs
            out_refs = out_refs[:-1]
        acc = lax.dot_general(lhs, b_ref[...].astype(BF16), dims, preferred_element_type=F32)
        if scale is not None:
            acc = acc * scale
        if has_res:
            acc = acc + refs[2][...].astype(F32)
        if epilogue is None:
            out_refs[0][...] = acc.astype(out_refs[0].dtype)
        else:
            extra = [r[...] for r in refs[n_mm:n_in]]
            _store_results(out_refs, len(outs), epilogue(acc, *extra), pl.program_id(1) == 0)

    out = pl.pallas_call(
        body, name=name, grid=(n // tn, m // tm), in_specs=specs, out_specs=out_specs, out_shape=out_shape,
        compiler_params=_params(2),
    )(*ins)
    return out[0] if epilogue is None and prologue is None else out


def _rowwise(name, fn, rows, row_ins, par_ins, row_outs, acc_outs=(), tm_pref=256):
    tm = _pick(rows, tm_pref, 16)
    in_specs, ins = [], []
    for r in row_ins:
        arr, cb, cw = r if isinstance(r, tuple) else (r, 0, r.shape[1])
        assert arr.shape[0] == rows, (name, arr.shape, rows)
        ins.append(arr)
        in_specs.append(pl.BlockSpec((tm, cw), lambda i, cb=cb: (i, cb)))
    for p in par_ins:
        ins.append(p)
        in_specs.append(pl.BlockSpec(p.shape, lambda i: (0, 0)))
    out_specs = [pl.BlockSpec((tm, c), lambda i: (i, 0)) for c, _ in row_outs]
    out_specs += [pl.BlockSpec((r, c), lambda i: (0, 0)) for r, c in acc_outs]
    out_shape = [jax.ShapeDtypeStruct((rows, c), dt) for c, dt in row_outs]
    out_shape += [jax.ShapeDtypeStruct((r, c), F32) for r, c in acc_outs]
    n_in, n_row = len(ins), len(row_outs)

    def body(*refs):
        vals = [r[...] for r in refs[:n_in]]
        _store_results(refs[n_in:], n_row, fn(*vals), pl.program_id(0) == 0)

    return pl.pallas_call(
        body, name=name, grid=(rows // tm,), in_specs=in_specs, out_specs=out_specs, out_shape=out_shape,
        compiler_params=_params(1),
    )(*ins)


def _inv_rms(x):
    return lax.rsqrt(jnp.mean(x * x, axis=-1, keepdims=True) + NORM_EPS)


def _rms_rows(x, g):
    return x * _inv_rms(x) * g


def _rms_fwd(name, x, g):
    return _rowwise(name, _rms_rows, x.shape[0], [x], [g], [(x.shape[1], BF16)], tm_pref=512)[0]


def _rms_bwd_rows(dn, x, dres, g):
    r = _inv_rms(x)
    xh = x * r
    dg = jnp.sum(dn * xh, axis=0, keepdims=True)
    dxh = dn * g
    dx = r * (dxh - xh * jnp.mean(dxh * xh, axis=-1, keepdims=True)) + dres
    return dx, dx, dg


def _rms_bwd(name, x, g, dn, dres=None):
    d = x.shape[1]
    if dres is None:
        def fn(x, dn, g):
            return (jnp.sum(dn.astype(F32) * (x * _inv_rms(x)), axis=0, keepdims=True),)

        return None, _rowwise(name, fn, x.shape[0], [x, dn], [g], [], [(1, d)])[0]

    def fn(x, dn, dres, g):
        return _rms_bwd_rows(dn.astype(F32), x, dres, g)

    out = _rowwise(name, fn, x.shape[0], [x, dn, dres], [g], [(d, F32), (d, BF16)], [(1, d)])
    return (out[0], out[1]), out[2]


def _mm_rms_bwd(name, dy, w, x, g, dres, tb=True):
    d = x.shape[1]
    out = _mm(name, dy, w, tb=tb, epilogue=_rms_bwd_rows, row_ins=[x, dres], par_ins=[g],
              outs=[(d, F32), (d, BF16)], acc_outs=[(1, d)])
    return (out[0], out[1]), out[2]


def _sigmoid(x):
    return 1.0 / (1.0 + jnp.exp(-x))


def _swiglu_rows(gu, f):
    gt = gu[:, :f]
    return gu, gt * _sigmoid(gt) * gu[:, f:]


def _swiglu_bwd_rows(dact, gt, up):
    gt, up = gt.astype(F32), up.astype(F32)
    sg = _sigmoid(gt)
    return ((dact * up * (sg * (1.0 + gt * (1.0 - sg))), dact * (gt * sg)),)


def _glu_res(name, vg, x, d):
    def fn(val, gate, x):
        return x + val.astype(F32) * _sigmoid(gate.astype(F32))

    return _rowwise(name, fn, x.shape[0], [(vg, 0, d), (vg, 1, d), x], [], [(d, F32)])[0]


def _glu_bwd(name, dres, vg, d):
    def fn(dres, val, gate):
        val, gate = val.astype(F32), gate.astype(F32)
        sg = _sigmoid(gate)
        return ((dres * sg, dres * val * sg * (1.0 - sg)),)

    return _rowwise(name, fn, dres.shape[0], [dres, (vg, 0, d), (vg, 1, d)], [], [(2 * d, BF16)])[0]


def _final_loss(name, x, g, tgt):
    d = x.shape[1]

    def fn(x, tgt, g):
        r = _inv_rms(x)
        xh = x * r
        err = xh * g - tgt
        dy = err * (1.0 / d)
        dxh = dy * g
        dx = r * (dxh - xh * jnp.mean(dxh * xh, axis=-1, keepdims=True))
        return dx, dx, jnp.sum(err * err, axis=0, keepdims=True), jnp.sum(dy * xh, axis=0, keepdims=True)

    dx, dx16, err2, dg = _rowwise(name, fn, x.shape[0], [x, tgt], [g], [(d, F32), (d, BF16)], [(1, d), (1, d)])
    return (dx, dx16), err2, dg


def _shift_down(u, k):
    rows = lax.broadcasted_iota(jnp.int32, u.shape, 0)
    return jnp.where(rows >= k, pltpu.roll(u, k, 0), 0.0)


def _shift_up(u, k):
    n = u.shape[0]
    rows = lax.broadcasted_iota(jnp.int32, u.shape, 0)
    return jnp.where(rows < n - k, pltpu.roll(u, n - k, 0), 0.0)


def _conv_specs(seq, cw, n_cb, swap):
    def at(off):
        if swap:
            return pl.BlockSpec((seq, cw), lambda j, b: (b, off * n_cb + j))
        return pl.BlockSpec((seq, cw), lambda b, j: (b, off * n_cb + j))

    return at


def _conv_fwd(name, cbv, w, n_seq, seq):
    d = w.shape[1]
    cw = _pick(d, 256, V7X_LANES)
    n_cb = d // cw
    at = _conv_specs(seq, cw, n_cb, swap=False)

    def body(c_ref, b_ref, v_ref, w_ref, z_ref):
        u = c_ref[...].astype(F32) * v_ref[...].astype(F32)
        cv = w_ref[0:1, :] * _shift_down(u, 2) + w_ref[1:2, :] * _shift_down(u, 1) + w_ref[2:3, :] * u
        z_ref[...] = (b_ref[...].astype(F32) * cv).astype(z_ref.dtype)

    return pl.pallas_call(
        body, name=name, grid=(n_seq, n_cb),
        in_specs=[at(0), at(1), at(2), pl.BlockSpec((CONV_WIDTH, cw), lambda b, j: (0, j))],
        out_specs=at(0), out_shape=jax.ShapeDtypeStruct((n_seq * seq, d), BF16), compiler_params=_params(2),
    )(cbv, cbv, cbv, w)


def _conv_bwd(name, dz, cbv, w, n_seq, seq):
    d = w.shape[1]
    cw = _pick(d, 256, V7X_LANES)
    n_cb = d // cw
    at = _conv_specs(seq, cw, n_cb, swap=True)

    def body(dz_ref, c_ref, b_ref, v_ref, w_ref, dc_ref, db_ref, dv_ref, dw_ref):
        c, b, v = c_ref[...].astype(F32), b_ref[...].astype(F32), v_ref[...].astype(F32)
        dz = dz_ref[...].astype(F32)
        w0, w1, w2 = w_ref[0:1, :], w_ref[1:2, :], w_ref[2:3, :]
        u = c * v
        u1, u2 = _shift_down(u, 1), _shift_down(u, 2)
        cv = w0 * u2 + w1 * u1 + w2 * u
        db_ref[...] = (dz * cv).astype(db_ref.dtype)
        dcv = dz * b
        du = w2 * dcv + w1 * _shift_up(dcv, 1) + w0 * _shift_up(dcv, 2)
        dc_ref[...] = (du * v).astype(dc_ref.dtype)
        dv_ref[...] = (du * c).astype(dv_ref.dtype)

        @pl.when(pl.program_id(1) == 0)
        def _():
            dw_ref[...] = jnp.zeros_like(dw_ref)

        dw_ref[0:1, :] += jnp.sum(dcv * u2, axis=0, keepdims=True)
        dw_ref[1:2, :] += jnp.sum(dcv * u1, axis=0, keepdims=True)
        dw_ref[2:3, :] += jnp.sum(dcv * u, axis=0, keepdims=True)

    act = jax.ShapeDtypeStruct((n_seq * seq, d), BF16)
    return pl.pallas_call(
        body, name=name, grid=(n_cb, n_seq),
        in_specs=[at(0), at(0), at(1), at(2), pl.BlockSpec((CONV_WIDTH, cw), lambda j, b: (0, j))],
        out_specs=[at(0), at(0), at(0), pl.BlockSpec((CONV_WIDTH, cw), lambda j, b: (0, j))],
        out_shape=[act, act, act, jax.ShapeDtypeStruct((CONV_WIDTH, d), F32)], compiler_params=_params(2),
    )(dz, cbv, cbv, cbv, w)


def _s5_discretize(a_re, a_im, log_dt, b_re, b_im):
    lam_re = jnp.minimum(a_re, EIG_CLIP)
    lam_im = a_im
    dt = jnp.exp(log_dt)[:, None]
    mag = jnp.exp(lam_re * dt)
    abar_re = mag * jnp.cos(lam_im * dt)
    abar_im = mag * jnp.sin(lam_im * dt)
    den = lam_re * lam_re + lam_im * lam_im
    num_re = abar_re - 1.0
    num_im = abar_im
    coef_re = (num_re * lam_re + num_im * lam_im) / den
    coef_im = (num_im * lam_re - num_re * lam_im) / den
    bbar_re = coef_re[..., None] * b_re - coef_im[..., None] * b_im
    bbar_im = coef_re[..., None] * b_im + coef_im[..., None] * b_re
    return abar_re, abar_im, bbar_re, bbar_im


def _block_diag_in(bbar, gb):
    g, p, h = bbar.shape
    t = jnp.transpose(bbar.reshape(g // gb, gb, p, h), (0, 1, 3, 2))
    return jnp.einsum("cghp,gk->cghkp", t, jnp.eye(gb, dtype=bbar.dtype)).reshape(g // gb, gb * h, gb * p)


def _block_diag_in_t(blk, gb, p, h):
    nb = blk.shape[0]
    t = jnp.einsum("cghkp,gk->cghp", blk.reshape(nb, gb, h, gb, p), jnp.eye(gb, dtype=blk.dtype))
    return jnp.transpose(t, (0, 1, 3, 2)).reshape(nb * gb, p, h)


def _block_diag_out(c, gb):
    g, h, p = c.shape
    t = jnp.transpose(c.reshape(g // gb, gb, h, p), (0, 1, 3, 2))
    return jnp.einsum("cgph,gk->cgpkh", t, jnp.eye(gb, dtype=c.dtype)).reshape(g // gb, gb * p, gb * h)


def _block_diag_out_t(blk, gb, p, h):
    nb = blk.shape[0]
    t = jnp.einsum("cgpkh,gk->cgph", blk.reshape(nb, gb, p, gb, h), jnp.eye(gb, dtype=blk.dtype))
    return jnp.transpose(t, (0, 1, 3, 2)).reshape(nb * gb, h, p)


def _gelu(y):
    return 0.5 * y * (1.0 + jnp.tanh(GELU_C * (y + GELU_A * y * y * y)))


def _gelu_grad(y):
    th = jnp.tanh(GELU_C * (y + GELU_A * y * y * y))
    return 0.5 * (1.0 + th) + 0.5 * y * (1.0 - th * th) * GELU_C * (1.0 + 3.0 * GELU_A * y * y)


def _dot(a, b, ca, cb):
    return lax.dot_general(a.astype(BF16), b.astype(BF16), (((ca,), (cb,)), ((), ())), preferred_element_type=F32)


def _s5_specs(seq, ch, sb):
    act = pl.BlockSpec((seq, ch), lambda j, b: (b, j))
    state = pl.BlockSpec((seq, sb), lambda j, b: (b, j))
    w_in = pl.BlockSpec((None, ch, sb), lambda j, b: (j, 0, 0))
    w_out = pl.BlockSpec((None, sb, ch), lambda j, b: (j, 0, 0))
    lane_s = pl.BlockSpec((1, sb), lambda j, b: (0, j))
    lane_c = pl.BlockSpec((1, ch), lambda j, b: (0, j))
    return act, state, w_in, w_out, lane_s, lane_c


def _s5_fwd(name, h, bin_re, bin_im, cout_re, cout_im, abar_re, abar_im, dskip, n_seq, seq):
    t, d = h.shape
    nb, ch, sb = bin_re.shape
    act, state, w_in, w_out, lane_s, lane_c = _s5_specs(seq, ch, sb)

    def body(h_ref, bre_ref, bim_ref, cre_ref, cim_ref, ar_ref, ai_ref, d_ref, sre_ref, sim_ref, y_ref, z_ref):
        u = h_ref[...]
        sre_ref[...] = _dot(u, bre_ref[...], 1, 0)
        sim_ref[...] = _dot(u, bim_ref[...], 1, 0)
        ar, ai = ar_ref[...], ai_ref[...]

        def step(i, carry):
            sr, si = carry
            row = pl.ds(i, 1)
            nr = ar * sr - ai * si + sre_ref[row, :]
            ni = ar * si + ai * sr + sim_ref[row, :]
            sre_ref[row, :] = nr
            sim_ref[row, :] = ni
            return nr, ni

        zero = jnp.zeros((1, sb), F32)
        lax.fori_loop(0, seq, step, (zero, zero), unroll=8)
        y = _dot(sre_ref[...], cre_ref[...], 1, 0) - _dot(sim_ref[...], cim_ref[...], 1, 0)
        y = y + d_ref[...] * u.astype(F32)
        y_ref[...] = y
        z_ref[...] = _gelu(y).astype(z_ref.dtype)

    return pl.pallas_call(
        body, name=name, grid=(nb, n_seq),
        in_specs=[act, w_in, w_in, w_out, w_out, lane_s, lane_s, lane_c],
        out_specs=[state, state, act, act],
        out_shape=[jax.ShapeDtypeStruct((t, nb * sb), F32), jax.ShapeDtypeStruct((t, nb * sb), F32),
                   jax.ShapeDtypeStruct((t, d), F32), jax.ShapeDtypeStruct((t, d), BF16)],
        compiler_params=_params(2),
    )(h, bin_re, bin_im, cout_re, cout_im, abar_re, abar_im, dskip)


def _s5_bwd(name, dz, ypre, h, s_re, s_im, bin_re, bin_im, cout_re, cout_im, abar_re, abar_im, dskip, n_seq, seq):
    t, d = h.shape
    nb, ch, sb = bin_re.shape
    act, state, w_in, w_out, lane_s, lane_c = _s5_specs(seq, ch, sb)

    def body(dz_ref, y_ref, h_ref, sre_ref, sim_ref, bre_ref, bim_ref, cre_ref, cim_ref, ar_ref, ai_ref, d_ref,
             dh_ref, dbre_ref, dbim_ref, dcre_ref, dcim_ref, dar_ref, dai_ref, dd_ref, gre, gim):
        first = pl.program_id(1) == 0
        u = h_ref[...].astype(F32)
        dy = dz_ref[...].astype(F32) * _gelu_grad(y_ref[...])
        gre[...] = _dot(dy, cre_ref[...], 1, 1)
        gim[...] = -_dot(dy, cim_ref[...], 1, 1)
        ar, ai = ar_ref[...], ai_ref[...]

        def step(i, carry):
            gr, gi = carry
            row = pl.ds(seq - 1 - i, 1)
            nr = gre[row, :] + ar * gr + ai * gi
            ni = gim[row, :] - ai * gr + ar * gi
            gre[row, :] = nr
            gim[row, :] = ni
            return nr, ni

        zero = jnp.zeros((1, sb), F32)
        lax.fori_loop(0, seq, step, (zero, zero), unroll=8)

        g_re, g_im = gre[...], gim[...]
        s_re, s_im = sre_ref[...], sim_ref[...]
        p_re, p_im = _shift_down(s_re, 1), _shift_down(s_im, 1)
        dar = jnp.sum(g_re * p_re + g_im * p_im, axis=0, keepdims=True)
        dai = jnp.sum(g_im * p_re - g_re * p_im, axis=0, keepdims=True)
        dbre = _dot(u, g_re, 0, 0)
        dbim = _dot(u, g_im, 0, 0)
        dcre = _dot(s_re, dy, 0, 0)
        dcim = -_dot(s_im, dy, 0, 0)
        ddd = jnp.sum(dy * u, axis=0, keepdims=True)
        dh_ref[...] = _dot(g_re, bre_ref[...], 1, 1) + _dot(g_im, bim_ref[...], 1, 1) + d_ref[...] * dy

        @pl.when(first)
        def _():
            dar_ref[...] = dar
            dai_ref[...] = dai
            dbre_ref[...] = dbre
            dbim_ref[...] = dbim
            dcre_ref[...] = dcre
            dcim_ref[...] = dcim
            dd_ref[...] = ddd

        @pl.when(jnp.logical_not(first))
        def _():
            dar_ref[...] += dar
            dai_ref[...] += dai
            dbre_ref[...] += dbre
            dbim_ref[...] += dbim
            dcre_ref[...] += dcre
            dcim_ref[...] += dcim
            dd_ref[...] += ddd

    return pl.pallas_call(
        body, name=name, grid=(nb, n_seq),
        in_specs=[act, act, act, state, state, w_in, w_in, w_out, w_out, lane_s, lane_s, lane_c],
        out_specs=[act, w_in, w_in, w_out, w_out, lane_s, lane_s, lane_c],
        out_shape=[jax.ShapeDtypeStruct((t, d), F32),
                   jax.ShapeDtypeStruct((nb, ch, sb), F32), jax.ShapeDtypeStruct((nb, ch, sb), F32),
                   jax.ShapeDtypeStruct((nb, sb, ch), F32), jax.ShapeDtypeStruct((nb, sb, ch), F32),
                   jax.ShapeDtypeStruct((1, nb * sb), F32), jax.ShapeDtypeStruct((1, nb * sb), F32),
                   jax.ShapeDtypeStruct((1, d), F32)],
        scratch_shapes=[pltpu.VMEM((seq, sb), F32), pltpu.VMEM((seq, sb), F32)],
        compiler_params=_params(2),
    )(dz, ypre, h, s_re, s_im, bin_re, bin_im, cout_re, cout_im, abar_re, abar_im, dskip)


ATTN_QUERY_ROWS = 1024


def _softmax_rows(q, k, scale):
    s = _dot(q, k, 1, 1) * scale
    e = jnp.exp(s - jnp.max(s, axis=-1, keepdims=True))
    return e / jnp.sum(e, axis=-1, keepdims=True)


def _attn_fwd(name, q, kv, n_seq, seq, mlen, heads):
    t, d = q.shape
    hd = d // heads
    tq = _pick(seq, ATTN_QUERY_ROWS, 16)
    nq = seq // tq
    scale = hd ** -0.5
    q_spec = pl.BlockSpec((tq, hd), lambda b, h, i: (b * nq + i, h))

    def body(q_ref, k_ref, v_ref, o_ref):
        p = _softmax_rows(q_ref[...], k_ref[...], scale)
        o_ref[...] = _dot(p, v_ref[...], 1, 0).astype(o_ref.dtype)

    return pl.pallas_call(
        body, name=name, grid=(n_seq, heads, nq),
        in_specs=[q_spec, pl.BlockSpec((mlen, hd), lambda b, h, i: (b, h)),
                  pl.BlockSpec((mlen, hd), lambda b, h, i: (b, heads + h))],
        out_specs=q_spec, out_shape=jax.ShapeDtypeStruct((t, d), BF16), compiler_params=_params(3),
    )(q, kv, kv)


def _attn_bwd(name, q, kv, do, n_seq, seq, mlen, heads):
    t, d = q.shape
    hd = d // heads
    tq = _pick(seq, ATTN_QUERY_ROWS, 16)
    nq = seq // tq
    scale = hd ** -0.5
    q_spec = pl.BlockSpec((tq, hd), lambda b, h, i: (b * nq + i, h))
    k_spec = pl.BlockSpec((mlen, hd), lambda b, h, i: (b, h))

    def body(q_ref, k_ref, v_ref, do_ref, dq_ref, dk_ref, dv_ref):
        q, k, v, do = q_ref[...], k_ref[...], v_ref[...], do_ref[...]
        p = _softmax_rows(q, k, scale)
        dp = _dot(do, v, 1, 1)
        ds = p * (dp - jnp.sum(dp * p, axis=-1, keepdims=True)) * scale
        dq_ref[...] = _dot(ds, k, 1, 0).astype(dq_ref.dtype)

        @pl.when(pl.program_id(2) == 0)
        def _():
            dk_ref[...] = jnp.zeros_like(dk_ref)
            dv_ref[...] = jnp.zeros_like(dv_ref)

        dk_ref[...] += _dot(ds, q, 0, 0)
        dv_ref[...] += _dot(p, do, 0, 0)

    return pl.pallas_call(
        body, name=name, grid=(n_seq, heads, nq),
        in_specs=[q_spec, k_spec, pl.BlockSpec((mlen, hd), lambda b, h, i: (b, heads + h)), q_spec],
        out_specs=[q_spec, k_spec, k_spec],
        out_shape=[jax.ShapeDtypeStruct((t, d), BF16), jax.ShapeDtypeStruct((n_seq * mlen, d), F32),
                   jax.ShapeDtypeStruct((n_seq * mlen, d), F32)],
        compiler_params=_params(3),
    )(q, kv, kv, do)


ADAMW_BLOCK_ELEMS = 128 * 1024


def _adamw(name, parts, w, m, v, first_layer=0, earlier=None, transposed=False):
    n_layers = len(parts)
    r, c = parts[0].shape[1:][::-1] if transposed else parts[0].shape[1:]
    assert w.shape[0] % r == 0 and w.shape[1] == c and first_layer + n_layers <= w.shape[0] // r, (name, w.shape)
    tr = _pick(r, max(V7X_LANES, ADAMW_BLOCK_ELEMS // c // V7X_LANES * V7X_LANES), V7X_LANES if transposed else 8)
    nt = r // tr
    spec = pl.BlockSpec((tr, c), lambda l, i: ((first_layer + l) * nt + i, 0))
    c1 = 1.0 - ADAM_B1 ** ADAM_STEP
    c2 = 1.0 - ADAM_B2 ** ADAM_STEP

    def parts_spec(q):
        def at(l, i):
            return jnp.where(l == q, i, jnp.where(l > q, nt - 1, 0))

        if transposed:
            return pl.BlockSpec((N_DEV, c, tr), lambda l, i: (0, 0, at(l, i)))
        return pl.BlockSpec((N_DEV, tr, c), lambda l, i: (0, at(l, i), 0))

    earlier = list(earlier or ())

    def body(*refs):
        p_refs = refs[:n_layers]
        w_ref, m_ref, v_ref = refs[n_layers:n_layers + 3]
        g_ref, d_ref, nm_ref, nv_ref = refs[n_layers + 3 + len(earlier):]

        def update(p_ref):
            g = p_ref[0].astype(F32)
            for k in range(1, N_DEV):
                g = g + p_ref[k].astype(F32)
            if transposed:
                g = g.T
            nm = ADAM_B1 * m_ref[...] + (1.0 - ADAM_B1) * g
            nv = ADAM_B2 * v_ref[...] + (1.0 - ADAM_B2) * (g * g)
            g_ref[...] = g
            nm_ref[...] = nm
            nv_ref[...] = nv
            d_ref[...] = -ADAM_LR * ((nm / c1) / (jnp.sqrt(nv / c2) + ADAM_EPS) + ADAM_WD * w_ref[...])

        for q in range(n_layers):
            pl.when(pl.program_id(0) == q)(lambda q=q: update(p_refs[q]))

    out = jax.ShapeDtypeStruct(w.shape, F32)
    return pl.pallas_call(
        body, name=name, grid=(n_layers, nt),
        in_specs=[parts_spec(q) for q in range(n_layers)] + [spec] * 3 + [ANY_SPEC] * len(earlier),
        out_specs=[spec] * 4, out_shape=[out] * 4, compiler_params=_params(2),
        input_output_aliases={n_layers + 3 + q: q for q in range(len(earlier))},
    )(*parts, w, m, v, *earlier)


def _place():
    x, y, c = lax.axis_index("x"), lax.axis_index("y"), lax.axis_index("c")
    return x, y, c


def _index(px, py, pc):
    return 4 * px + 2 * py + pc


def _all_gather(name, shards):
    n = len(shards)

    def body(*refs):
        in_refs, out_refs = refs[:n], refs[n:2 * n]
        send_sems, recv_sems, local_sems = refs[2 * n:]
        x, y, c = _place()
        me, sibling = (x, y, c), (x, y, 1 - c)
        chips = [(1 - x, y), (x, 1 - y), (1 - x, 1 - y)]

        def slot(k, block):
            return out_refs[k].at[_index(*block)]

        def copy(k, j, block, to, src=None):
            return pltpu.make_async_remote_copy(
                src_ref=slot(k, block) if src is None else src, dst_ref=slot(k, block),
                send_sem=send_sems.at[7 * k + j], recv_sem=recv_sems.at[7 * k + j], device_id=to, device_id_type=MESH)

        mine = [pltpu.make_async_copy(in_refs[k], slot(k, me), local_sems.at[k]) for k in range(n)]
        for cp in mine:
            cp.start()
        first = []
        for k in range(n):
            first.append(copy(k, 0, me, sibling, src=in_refs[k]))
            first += [copy(k, 1 + j, me, (*chip, c), src=in_refs[k]) for j, chip in enumerate(chips)]
        for cp in first:
            cp.start()
        passed = []
        for j, chip in enumerate(chips):
            for k in range(n):
                copy(k, 1 + j, (*chip, c), me).wait_recv()
                cp = copy(k, 4 + j, (*chip, c), sibling)
                cp.start()
                passed.append(cp)
        for k in range(n):
            copy(k, 0, sibling, me).wait_recv()
            for j, chip in enumerate(chips):
                copy(k, 4 + j, (*chip, 1 - c), me).wait_recv()
        for cp in first + passed:
            cp.wait_send()
        for cp in mine:
            cp.wait()

    return pl.pallas_call(
        body, name=name, in_specs=[HBM_SPEC] * n, out_specs=[HBM_SPEC] * n,
        out_shape=[jax.ShapeDtypeStruct((N_DEV,) + s.shape, s.dtype) for s in shards],
        scratch_shapes=[pltpu.SemaphoreType.DMA((7 * n,)), pltpu.SemaphoreType.DMA((7 * n,)),
                        pltpu.SemaphoreType.DMA((n,))],
    )(*shards)


WHOLE, BLOCK, COLUMNS = "whole", "block", "columns"


def _slot(ref, index, mode):
    if mode == WHOLE:
        return ref
    if mode == BLOCK:
        return ref.at[index]
    width = ref.shape[1] // N_DEV
    return ref.at[:, pl.ds(pl.multiple_of(index * width, width), width)]


DIRECT = tuple(range(1, N_DEV))
CHIPS = (1, 4, 2, 6)
FORWARD = "forward"


def _copies(plan, x, y, c):
    def xor(r, flip_core=False):
        rx, ry, rc = (r >> 2) & 1, (r >> 1) & 1, (r & 1) ^ int(flip_core)
        return (1 - x if rx else x, 1 - y if ry else y, 1 - c if rc else c)

    me = _index(x, y, c)
    if plan == FORWARD:
        return [(xor(1), _index(*xor(r)), _index(*xor(r)), _index(*xor(r, True))) for r in (4, 2, 6)]
    return [(xor(r), _index(*xor(r)), me, _index(*xor(r))) for r in plan]


def _exchange_start(name, srcs, lands, src_modes, land_modes, deps=(), plan=DIRECT):
    n, n_src = len(lands), len(srcs)
    n_copies = 3 if plan == FORWARD else len(plan)

    def body(*refs):
        land_refs = refs[n_src:n_src + n]
        src_refs = refs[:n_src] if n_src else land_refs
        send_sems, recv_sems = refs[n_src + n + len(deps)], refs[n_src + n + len(deps) + 1]
        token = refs[-1]
        x, y, c = _place()
        for k in range(n):
            for j, (peer, src_index, dst_index, _) in enumerate(_copies(plan, x, y, c)):
                pltpu.make_async_remote_copy(
                    src_ref=_slot(src_refs[k], src_index, src_modes[k]),
                    dst_ref=_slot(land_refs[k], dst_index, land_modes[k]), send_sem=send_sems.at[n_copies * k + j],
                    recv_sem=recv_sems.at[n_copies * k + j], device_id=peer, device_id_type=MESH).start()
        token[...] = jnp.zeros_like(token)

    arrays = list(srcs) + list(lands)
    thru = [pltpu.HBM(a.shape, a.dtype) for a in arrays]
    out = pl.pallas_call(
        body, name=name,
        out_shape=(pltpu.SemaphoreType.DMA((n_copies * n,)), pltpu.SemaphoreType.DMA((n_copies * n,)), *thru,
                   jax.ShapeDtypeStruct((8, V7X_LANES), F32)),
        in_specs=[HBM_SPEC] * len(arrays) + [ANY_SPEC] * len(deps),
        out_specs=(SEM_SPEC, SEM_SPEC, *([HBM_SPEC] * len(arrays)), pl.BlockSpec(memory_space=pltpu.VMEM)),
        input_output_aliases={k: 2 + k for k in range(len(arrays))},
        compiler_params=pltpu.CompilerParams(has_side_effects=pltpu.SideEffectType.DATAFLOW_SIDE_EFFECTING),
    )(*[pltpu.with_memory_space_constraint(a, pltpu.HBM) for a in arrays], *deps)
    return out[0], out[1], list(out[2:2 + n_src]), list(out[2 + n_src:2 + n_src + n]), out[-1]


def _exchange_wait(name, send_sems, recv_sems, srcs, lands, src_modes, land_modes, after, plan=DIRECT):
    n, n_src = len(lands), len(srcs)
    n_copies = 3 if plan == FORWARD else len(plan)

    def body(*refs):
        land_refs = refs[n_src:n_src + n]
        src_refs = refs[:n_src] if n_src else land_refs
        send_sems, recv_sems = refs[n_src + n], refs[n_src + n + 1]
        x, y, c = _place()
        for k in range(n):
            for j, (peer, src_index, _, arrival_index) in enumerate(_copies(plan, x, y, c)):
                cp = pltpu.make_async_remote_copy(
                    src_ref=_slot(src_refs[k], src_index, src_modes[k]),
                    dst_ref=_slot(land_refs[k], arrival_index, land_modes[k]), send_sem=send_sems.at[n_copies * k + j],
                    recv_sem=recv_sems.at[n_copies * k + j], device_id=peer, device_id_type=MESH)
                cp.wait_send()
                cp.wait_recv()

    arrays = list(srcs) + list(lands)
    thru = [pltpu.HBM(a.shape, a.dtype) for a in arrays]
    out = pl.pallas_call(
        body, name=name, out_shape=tuple(thru),
        in_specs=[HBM_SPEC] * len(arrays) + [SEM_SPEC, SEM_SPEC, ANY_SPEC], out_specs=tuple([HBM_SPEC] * len(arrays)),
        input_output_aliases={k: k for k in range(len(arrays))},
        compiler_params=pltpu.CompilerParams(has_side_effects=pltpu.SideEffectType.DATAFLOW_SIDE_EFFECTING),
    )(*arrays, send_sems, recv_sems, after)
    return list(out[n_src:])


def _landing(shard, me, mode=BLOCK):
    if mode == COLUMNS:
        k, n = shard.shape
        return lax.dynamic_update_slice(lax.empty((k, N_DEV * n), shard.dtype), shard, (0, me * n))
    zone = lax.empty((N_DEV,) + shard.shape, shard.dtype)
    return lax.dynamic_update_slice(zone, shard[None], (me,) + (0,) * shard.ndim)


def _cols_whole(w):
    return jnp.transpose(w, (1, 0, 2)).reshape(w.shape[1], N_DEV * w.shape[2])


def _rows_whole(w):
    return w.reshape(N_DEV * w.shape[1], w.shape[2])


def _cols_parts(dw):
    k, n8 = dw.shape
    return jnp.transpose(dw.reshape(k, N_DEV, n8 // N_DEV), (1, 0, 2))


def _rows_parts(dw):
    r8, c = dw.shape
    return dw.reshape(N_DEV, r8 // N_DEV, c)


def _pack_rows(arrays):
    rows = []
    for a in arrays:
        flat = a.reshape(-1).astype(F32)
        flat = jnp.pad(flat, [(0, (-flat.shape[0]) % PACK_TILE)])
        rows.append(flat.reshape(-1, V7X_LANES))
    return jnp.concatenate(rows, axis=0)


def _unpack_rows(packed, shapes):
    out, row = [], 0
    for s in shapes:
        size = math.prod(s)
        n_rows = -(-size // PACK_TILE) * 8
        out.append(packed[row:row + n_rows].reshape(-1)[:size].reshape(s))
        row += n_rows
    return out


def _merge2d(a):
    return a.reshape(-1, a.shape[-1])


def _ffn_fwd(tag, x, g, w_up_t, w_down, deps=()):
    f = w_down.shape[0]
    gu, act, n = _mm(f"{tag}_up", x, w_up_t, tb=True, prologue=_rms_rows, prologue_pars=[g], deps=deps,
                     epilogue=lambda acc: _swiglu_rows(acc, f), outs=[(2 * f, BF16), (f, BF16)])
    out = _mm(f"{tag}_down", act, w_down, res=x, scale=0.5, out_dtype=F32)
    return out, (x, n, gu, act)


def _ffn_bwd(tag, dres, saved, g, w_up_t, w_down, deps=()):
    x, n, gu, act = saved
    dres32, dres16 = dres
    f = w_down.shape[0]
    dgu = _mm(f"{tag}_down_dx", dres16, w_down, tb=True, scale=0.5, deps=deps, epilogue=_swiglu_bwd_rows,
              row_ins=[(gu, 0, f), (gu, 1, f)], outs=[(2 * f, BF16)])[0]
    d_down = _mm(f"{tag}_down_dw", act, dres16, ta=True, scale=0.5)
    d_up_t = _mm(f"{tag}_up_dw", dgu, n, ta=True)
    dx, dg = _mm_rms_bwd(f"{tag}_up_dx", dgu, w_up_t, x, g, dres32, tb=False)
    return dx, dg, d_up_t, d_down


def _conv_mixer_fwd(tag, x, g, w_in, w_conv, w_out, n_seq, seq):
    cbv, h = _mm(f"{tag}_in", x, w_in, prologue=_rms_rows, prologue_pars=[g])
    z = _conv_fwd(f"{tag}_conv", cbv, w_conv, n_seq, seq)
    out = _mm(f"{tag}_out", z, w_out, res=x, out_dtype=F32)
    return out, (x, h, cbv, z)


def _conv_mixer_bwd(tag, dres, saved, g, w_in, w_conv, w_out, n_seq, seq):
    x, h, cbv, z = saved
    dres32, dres16 = dres
    dz = _mm(f"{tag}_out_dx", dres16, w_out, tb=True)
    d_out = _mm(f"{tag}_out_dw", z, dres16, ta=True)
    dc, db, dv, d_conv = _conv_bwd(f"{tag}_conv_bwd", dz, cbv, w_conv, n_seq, seq)
    dcbv = jnp.concatenate([dc, db, dv], axis=1)
    d_in = _mm(f"{tag}_in_dw", h, dcbv, ta=True)
    dx, dg = _mm_rms_bwd(f"{tag}_in_dx", dcbv, w_in, x, g, dres32)
    return dx, dg, d_in, d_conv, d_out


def _s5_mixer_fwd(tag, x, g, ssm, dskip, w_glu, n_seq, seq):
    a_re, a_im, log_dt, b_re, b_im, c_re, c_im = ssm
    groups, p, hh = b_re.shape
    gb = S5_CHANNELS // hh
    disc, disc_vjp = jax.vjp(_s5_discretize, a_re, a_im, log_dt, b_re, b_im)
    abar_re, abar_im, bbar_re, bbar_im = disc
    mats = (_block_diag_in(bbar_re, gb).astype(BF16), _block_diag_in(bbar_im, gb).astype(BF16),
            _block_diag_out(c_re, gb).astype(BF16), _block_diag_out(c_im, gb).astype(BF16),
            abar_re.reshape(1, groups * p), abar_im.reshape(1, groups * p), dskip)
    d = x.shape[1]
    h = _rms_fwd(f"{tag}_norm", x, g)
    s_re, s_im, ypre, z = _s5_fwd(f"{tag}_scan", h, *mats, n_seq, seq)
    vg = _mm(f"{tag}_glu", z, w_glu)
    out = _glu_res(f"{tag}_glu_act", vg, x, d)
    return out, (x, h, s_re, s_im, ypre, z, vg, mats, disc_vjp, (groups, p, hh, gb))


def _s5_mixer_bwd(tag, dres, saved, g, w_glu, n_seq, seq):
    x, h, s_re, s_im, ypre, z, vg, mats, disc_vjp, (groups, p, hh, gb) = saved
    d = x.shape[1]
    dres32, _ = dres
    dvg = _glu_bwd(f"{tag}_glu_act_bwd", dres32, vg, d)
    d_glu = _mm(f"{tag}_glu_dw", z, dvg, ta=True)
    dz = _mm(f"{tag}_glu_dx", dvg, w_glu, tb=True)
    dh, dbin_re, dbin_im, dcout_re, dcout_im, dabar_re, dabar_im, d_skip = _s5_bwd(
        f"{tag}_scan_bwd", dz, ypre, h, s_re, s_im, *mats, n_seq, seq)
    d_are, d_aim, d_logdt, d_bre, d_bim = disc_vjp((
        dabar_re.reshape(groups, p), dabar_im.reshape(groups, p),
        _block_diag_in_t(dbin_re, gb, p, hh), _block_diag_in_t(dbin_im, gb, p, hh)))
    d_cre = _block_diag_out_t(dcout_re, gb, p, hh)
    d_cim = _block_diag_out_t(dcout_im, gb, p, hh)
    dx, dg = _rms_bwd(f"{tag}_norm_bwd", x, g, dh, dres32)
    return dx, dg, (d_are, d_aim, d_logdt, d_bre, d_bim, d_cre, d_cim), d_skip, d_glu


def _xattn_fwd(tag, x, mem, g_q, g_mem, w_q, w_kv, w_o, n_seq, seq, mlen, heads):
    q, n = _mm(f"{tag}_q", x, w_q, prologue=_rms_rows, prologue_pars=[g_q])
    mem_n = _rms_fwd(f"{tag}_mem_norm", mem, g_mem)
    kv = _mm(f"{tag}_kv", mem_n, w_kv)
    o = _attn_fwd(f"{tag}_attn", q, kv, n_seq, seq, mlen, heads)
    out = _mm(f"{tag}_o", o, w_o, res=x, out_dtype=F32)
    return out, (x, n, q, mem_n, kv, o)


def _xattn_bwd(tag, dres, saved, mem, g_q, g_mem, w_q, w_kv, w_o, n_seq, seq, mlen, heads):
    x, n, q, mem_n, kv, o = saved
    dres32, dres16 = dres
    do = _mm(f"{tag}_o_dx", dres16, w_o, tb=True)
    d_o = _mm(f"{tag}_o_dw", o, dres16, ta=True)
    dq, dk, dv = _attn_bwd(f"{tag}_attn_bwd", q, kv, do, n_seq, seq, mlen, heads)
    dkv = jnp.concatenate([dk, dv], axis=1)
    d_q = _mm(f"{tag}_q_dw", n, dq, ta=True)
    d_kv = _mm(f"{tag}_kv_dw", mem_n, dkv, ta=True)
    dmem_n = _mm(f"{tag}_kv_dx", dkv, w_kv, tb=True)
    _, dg_mem = _rms_bwd(f"{tag}_mem_norm_bwd", mem, g_mem, dmem_n)
    dx, dg_q = _mm_rms_bwd(f"{tag}_q_dx", dq, w_q, x, g_q, dres32)
    return dx, dg_q, dg_mem, d_q, d_kv, d_o


WEIGHT_NAMES = ("norm_g", "final_g", "ffn1_up", "ffn1_down", "ffn2_up", "ffn2_down", "conv_w_in", "conv_w",
                "conv_w_out", "ssm_a_re", "ssm_a_im", "ssm_log_dt", "ssm_b_re", "ssm_b_im", "ssm_c_re", "ssm_c_im",
                "ssm_d", "ssm_w_glu", "xa_w_q", "xa_w_kv", "xa_w_o")
MATRICES = ("ffn1_up", "ffn1_down", "ffn2_up", "ffn2_down", "conv_w_in", "conv_w_out", "ssm_w_glu", "xa_w_q",
            "xa_w_kv", "xa_w_o")
COLUMN_SHARDED = ("conv_w_in", "ssm_w_glu", "xa_w_kv")
TRANSPOSED = ("ffn1_up", "ffn2_up")
SMALL_SHARDED = ("norm_g", "conv_w", "ssm_d")
REPLICATED = ("ssm_a_re", "ssm_a_im", "ssm_log_dt", "ssm_b_re", "ssm_b_im", "ssm_c_re", "ssm_c_im", "final_g")


def kernel(x, mem, norm_g, final_g, ffn1_up, ffn1_down, ffn2_up, ffn2_down, conv_w_in, conv_w, conv_w_out, ssm_a_re, ssm_a_im, ssm_log_dt, ssm_b_re, ssm_b_im, ssm_c_re, ssm_c_im, ssm_d, ssm_w_glu, xa_w_q, xa_w_kv, xa_w_o, loss_target, m_norm_g, m_final_g, m_ffn1_up, m_ffn1_down, m_ffn2_up, m_ffn2_down, m_conv_w_in, m_conv_w, m_conv_w_out, m_ssm_a_re, m_ssm_a_im, m_ssm_log_dt, m_ssm_b_re, m_ssm_b_im, m_ssm_c_re, m_ssm_c_im, m_ssm_d, m_ssm_w_glu, m_xa_w_q, m_xa_w_kv, m_xa_w_o, v_norm_g, v_final_g, v_ffn1_up, v_ffn1_down, v_ffn2_up, v_ffn2_down, v_conv_w_in, v_conv_w, v_conv_w_out, v_ssm_a_re, v_ssm_a_im, v_ssm_log_dt, v_ssm_b_re, v_ssm_b_im, v_ssm_c_re, v_ssm_c_im, v_ssm_d, v_ssm_w_glu, v_xa_w_q, v_xa_w_kv, v_xa_w_o):
    w = dict(norm_g=norm_g, final_g=final_g, ffn1_up=ffn1_up, ffn1_down=ffn1_down, ffn2_up=ffn2_up,
             ffn2_down=ffn2_down, conv_w_in=conv_w_in, conv_w=conv_w, conv_w_out=conv_w_out, ssm_a_re=ssm_a_re,
             ssm_a_im=ssm_a_im, ssm_log_dt=ssm_log_dt, ssm_b_re=ssm_b_re, ssm_b_im=ssm_b_im, ssm_c_re=ssm_c_re,
             ssm_c_im=ssm_c_im, ssm_d=ssm_d, ssm_w_glu=ssm_w_glu, xa_w_q=xa_w_q, xa_w_kv=xa_w_kv, xa_w_o=xa_w_o)
    mom = dict(norm_g=m_norm_g, final_g=m_final_g, ffn1_up=m_ffn1_up, ffn1_down=m_ffn1_down, ffn2_up=m_ffn2_up,
               ffn2_down=m_ffn2_down, conv_w_in=m_conv_w_in, conv_w=m_conv_w, conv_w_out=m_conv_w_out,
               ssm_a_re=m_ssm_a_re, ssm_a_im=m_ssm_a_im, ssm_log_dt=m_ssm_log_dt, ssm_b_re=m_ssm_b_re,
               ssm_b_im=m_ssm_b_im, ssm_c_re=m_ssm_c_re, ssm_c_im=m_ssm_c_im, ssm_d=m_ssm_d, ssm_w_glu=m_ssm_w_glu,
               xa_w_q=m_xa_w_q, xa_w_kv=m_xa_w_kv, xa_w_o=m_xa_w_o)
    var = dict(norm_g=v_norm_g, final_g=v_final_g, ffn1_up=v_ffn1_up, ffn1_down=v_ffn1_down, ffn2_up=v_ffn2_up,
               ffn2_down=v_ffn2_down, conv_w_in=v_conv_w_in, conv_w=v_conv_w, conv_w_out=v_conv_w_out,
               ssm_a_re=v_ssm_a_re, ssm_a_im=v_ssm_a_im, ssm_log_dt=v_ssm_log_dt, ssm_b_re=v_ssm_b_re,
               ssm_b_im=v_ssm_b_im, ssm_c_re=v_ssm_c_re, ssm_c_im=v_ssm_c_im, ssm_d=v_ssm_d, ssm_w_glu=v_ssm_w_glu,
               xa_w_q=v_xa_w_q, xa_w_kv=v_xa_w_kv, xa_w_o=v_xa_w_o)

    n_seq, seq, d = x.shape
    mlen = mem.shape[1]
    depth, n_norms = norm_g.shape[0], norm_g.shape[1]
    heads = 4
    tokens = n_seq * seq
    x2 = x.reshape(tokens, d)
    mem2 = mem.reshape(n_seq * mlen, d)
    tgt2 = loss_target.reshape(tokens, d)

    small_shapes = [w[k].shape for k in SMALL_SHARDED]
    small_rows = [_merge2d(w[k]) for k in SMALL_SHARDED]
    small_counts = [s.shape[0] for s in small_rows]
    small = jnp.concatenate(small_rows, axis=0)
    small = jnp.pad(small, [(0, (-small.shape[0]) % 8), (0, 0)])
    me = _index(*_place())

    def layer_weights(i):
        names = [(k, i) for k in ("ffn1_up", "ffn1_down", "ffn2_up", "ffn2_down", "xa_w_q", "xa_w_kv", "xa_w_o")]
        return names + ([("conv_w_in", i // 2), ("conv_w_out", i // 2)] if i % 2 == 0 else [("ssm_w_glu", i // 2)])

    shards = [[(w[k][idx].T if k in TRANSPOSED else w[k][idx]).astype(BF16) for k, idx in layer_weights(i)]
              for i in range(depth)]
    gathered = _all_gather("gather_layer0", shards[0] + [small])
    small_all = gathered[-1]
    blocks = [gathered[:-1]] + [None] * (depth - 1)
    in_flight = [None] * depth
    token = gathered[0]
    in_place = {k for k in COLUMN_SHARDED if w[k].shape[-1] % V7X_LANES == 0}

    def modes(i):
        return [COLUMNS if k in in_place else BLOCK for k, _ in layer_weights(i)]

    for i in range(1, depth):
        zones = [_landing(s, me, mode) for s, mode in zip(shards[i], modes(i))]
        whole_src = [WHOLE] * len(zones)
        *in_flight[i], token = _exchange_start(f"gather_start_l{i}", shards[i], zones, whole_src, modes(i),
                                               deps=[token], plan=CHIPS)
    passing = [None] * depth

    def small_whole(idx):
        start = sum(small_counts[:idx])
        part = small_all[:, start:start + small_counts[idx]]
        lead = small_shapes[idx][:-1]
        part = part.reshape((N_DEV,) + lead + (part.shape[-1],))
        part = jnp.moveaxis(part, 0, -2)
        return part.reshape(lead + (N_DEV * part.shape[-1],))

    norm_all = small_whole(0)
    conv_all = small_whole(1)
    dskip_all = small_whole(2)

    def whole(i):
        return {k: _rows_whole(blk) if k not in COLUMN_SHARDED else blk if i > 0 and k in in_place else _cols_whole(blk)
                for (k, _), blk in zip(layer_weights(i), blocks[i])}

    saved = []
    cur = x2
    for i in range(depth):
        g = [norm_all[i, k].reshape(1, d) for k in range(n_norms)]
        j = i // 2
        if i > 0:
            blocks[i] = _exchange_wait(f"forward_wait_l{i}", *passing[i], modes(i), modes(i), cur, plan=FORWARD)
        lw = whole(i)
        cur, s_ffn1 = _ffn_fwd(f"l{i}_ffn1", cur, g[0], lw["ffn1_up"], lw["ffn1_down"], [token] if i == 0 else ())
        if i % 2 == 0:
            lw["conv_w"] = conv_all[j]
            cur, s_mix = _conv_mixer_fwd(f"l{i}_conv", cur, g[1], lw["conv_w_in"], lw["conv_w"], lw["conv_w_out"],
                                         n_seq, seq)
        else:
            ssm = tuple(w[k][j] for k in ("ssm_a_re", "ssm_a_im", "ssm_log_dt", "ssm_b_re", "ssm_b_im",
                                          "ssm_c_re", "ssm_c_im"))
            cur, s_mix = _s5_mixer_fwd(f"l{i}_s5", cur, g[1], ssm, dskip_all[j].reshape(1, d), lw["ssm_w_glu"],
                                       n_seq, seq)
        cur, s_xa = _xattn_fwd(f"l{i}_xa", cur, mem2, g[2], g[3], lw["xa_w_q"], lw["xa_w_kv"], lw["xa_w_o"],
                               n_seq, seq, mlen, heads)
        deps = ()
        if i + 1 < depth:
            nxt = modes(i + 1)
            landed = _exchange_wait(f"gather_wait_l{i + 1}", *in_flight[i + 1], [WHOLE] * len(nxt), nxt, cur,
                                    plan=CHIPS)
            *passing[i + 1], forward_token = _exchange_start(f"forward_start_l{i + 1}", [], landed, nxt, nxt,
                                                             plan=FORWARD)
            deps = [forward_token]
        cur, s_ffn2 = _ffn_fwd(f"l{i}_ffn2", cur, g[4], lw["ffn2_up"], lw["ffn2_down"], deps)
        saved.append((g, lw, s_ffn1, s_mix, s_xa, s_ffn2))

    dres, err2, d_final = _final_loss("loss_head", cur, final_g.reshape(1, d), tgt2)
    loss = lax.psum(0.5 * jnp.sum(err2) / d, ("x", "y", "c"))

    d_norm = [[None] * n_norms for _ in range(depth)]
    d_conv = [None] * conv_w.shape[0]
    d_skip = [None] * ssm_d.shape[0]
    d_ssm = [None] * ssm_a_re.shape[0]
    leaving = [None] * depth
    deps = ()

    def leave(name, keys, gm, extra=()):
        srcs, src_modes, zones = [], [], []
        for k in keys:
            if k in in_place:
                rows, n = gm[k].shape[0], gm[k].shape[1] // N_DEV
                srcs.append(gm[k])
                src_modes.append(COLUMNS)
                zones.append(_landing(lax.dynamic_slice(gm[k], (0, me * n), (rows, n)), me))
            else:
                srcs.append(_cols_parts(gm[k]) if k in COLUMN_SHARDED else _rows_parts(gm[k]))
                src_modes.append(BLOCK)
                zones.append(_landing(lax.dynamic_index_in_dim(srcs[-1], me, 0, keepdims=False), me))
        for p in extra:
            srcs.append(p)
            src_modes.append(BLOCK)
            zones.append(_landing(lax.dynamic_index_in_dim(p, me, 0, keepdims=False), me))
        land_modes = [BLOCK] * len(srcs)
        *handles, token = _exchange_start(name, srcs, zones, src_modes, land_modes)
        return (*handles, src_modes, land_modes), token

    def small_parts(full):
        lead = full.shape[:-1]
        t = full.reshape(lead + (N_DEV, full.shape[-1] // N_DEV))
        t = jnp.moveaxis(t, -2, 0)
        return t.reshape(N_DEV, -1, t.shape[-1])

    for i in reversed(range(depth)):
        g, lw, s_ffn1, s_mix, s_xa, s_ffn2 = saved[i]
        j = i // 2
        gm = {}
        dres, d_norm[i][4], gm["ffn2_up"], gm["ffn2_down"] = _ffn_bwd(
            f"l{i}_ffn2", dres, s_ffn2, g[4], lw["ffn2_up"], lw["ffn2_down"], deps)
        dres, d_norm[i][2], d_norm[i][3], gm["xa_w_q"], gm["xa_w_kv"], gm["xa_w_o"] = _xattn_bwd(
            f"l{i}_xa", dres, s_xa, mem2, g[2], g[3], lw["xa_w_q"], lw["xa_w_kv"], lw["xa_w_o"], n_seq, seq, mlen,
            heads)
        if i % 2 == 0:
            dres, d_norm[i][1], gm["conv_w_in"], d_conv[j], gm["conv_w_out"] = _conv_mixer_bwd(
                f"l{i}_conv", dres, s_mix, g[1], lw["conv_w_in"], lw["conv_w"], lw["conv_w_out"], n_seq, seq)
        else:
            dres, d_norm[i][1], d_ssm[j], d_skip[j], gm["ssm_w_glu"] = _s5_mixer_bwd(
                f"l{i}_s5", dres, s_mix, g[1], lw["ssm_w_glu"], n_seq, seq)
        upper, token = leave(f"grads_start_l{i}_upper", [k for k, _ in layer_weights(i)[2:]], gm)
        dres, d_norm[i][0], gm["ffn1_up"], gm["ffn1_down"] = _ffn_bwd(
            f"l{i}_ffn1", dres, s_ffn1, g[0], lw["ffn1_up"], lw["ffn1_down"], [token])
        extra = []
        if i == 0:
            d_norm_all = jnp.stack([jnp.concatenate(row, axis=0) for row in d_norm])
            small_g = jnp.concatenate(
                [small_parts(a) for a in (d_norm_all, jnp.stack(d_conv), jnp.concatenate(d_skip, axis=0))], axis=1)
            extra = [jnp.pad(small_g, [(0, 0), (0, (-small_g.shape[1]) % 8), (0, 0)])]
        lower, token = leave(f"grads_start_l{i}_lower", ["ffn1_up", "ffn1_down"], gm, extra)
        leaving[i] = (upper, lower)
        deps = [token]
        if i == min(1, depth - 1):
            rep_grads = [jnp.stack([d_ssm[j][k] for j in range(len(d_ssm))]) for k in range(7)]
            rep_packed = _pack_rows(rep_grads + [d_final.reshape(-1)])
            *rep_leaving, token = _exchange_start("replicated_grads_start", [rep_packed], [_landing(rep_packed, me)],
                                                  [WHOLE], [BLOCK])
            deps = deps + [token]
    grad_x = dres[0].reshape(n_seq, seq, d)
    received = {k: [None] * w[k].shape[0] for k in MATRICES}

    def arrive(i, after):
        upper = _exchange_wait(f"grads_wait_l{i}_upper", *leaving[i][0], after)
        lower = _exchange_wait(f"grads_wait_l{i}_lower", *leaving[i][1], after)
        for (k, idx), blk in zip(layer_weights(i), lower[:2] + upper):
            received[k][idx] = blk
        return lower[2:]

    for i in range(1, depth):
        arrive(i, dres[0])
    rep_all = _exchange_wait("replicated_grads_wait", *rep_leaving, [WHOLE], [BLOCK], dres[0])[0]
    rep_shapes = [w[k].shape for k in REPLICATED]
    flat = {k: (_merge2d(w[k]), _merge2d(mom[k]), _merge2d(var[k])) for k in MATRICES}
    late = {k: received[k][0] is None for k in MATRICES}
    early = {}
    for k in MATRICES:
        first = 1 if late[k] else 0
        if first < len(received[k]):
            early[k] = _adamw(f"adamw_{k}_upper", received[k][first:], *flat[k], first_layer=first,
                              transposed=k in TRANSPOSED)
    small_received, = arrive(0, list(early.values())[-1][0] if early else dres[0])
    grads, deltas, new_m, new_v = {}, {}, {}, {}
    for k in MATRICES:
        out = early.get(k)
        if late[k]:
            out = _adamw(f"adamw_{k}_l0", received[k][:1], *flat[k], earlier=out, transposed=k in TRANSPOSED)
        grads[k], deltas[k], new_m[k], new_v[k] = [o.reshape(w[k].shape) for o in out]

    def small_local(src):
        rows = jnp.concatenate([_merge2d(src[k]) for k in SMALL_SHARDED], axis=0)
        return jnp.pad(rows, [(0, (-rows.shape[0]) % 8), (0, 0)])

    out = _adamw("adamw_small", [small_received], small, small_local(mom), small_local(var))
    for res, o in zip((grads, deltas, new_m, new_v), out):
        start = 0
        for k, cnt, shape in zip(SMALL_SHARDED, small_counts, small_shapes):
            res[k] = o[start:start + cnt].reshape(shape)
            start += cnt

    out = _adamw("adamw_replicated", [rep_all], _pack_rows([w[k] for k in REPLICATED]),
                 _pack_rows([mom[k] for k in REPLICATED]), _pack_rows([var[k] for k in REPLICATED]))
    for res, o in zip((grads, deltas, new_m, new_v), out):
        for k, a in zip(REPLICATED, _unpack_rows(o, rep_shapes)):
            res[k] = a

    return (loss, grad_x, *[grads[k] for k in WEIGHT_NAMES], *[deltas[k] for k in WEIGHT_NAMES],
            *[new_m[k] for k in WEIGHT_NAMES], *[new_v[k] for k in WEIGHT_NAMES])
```

```python
import math

import jax
import jax.numpy as jnp
from jax import lax
from jax.experimental import pallas as pl
from jax.experimental.pallas import tpu as pltpu

F32 = jnp.float32
BF16 = jnp.bfloat16
MESH = pl.DeviceIdType.MESH
N_DEV = 8

NORM_EPS = 1e-6
EIG_CLIP = -1e-4
CONV_WIDTH = 3
ADAM_LR = 0.001
ADAM_B1 = 0.9
ADAM_B2 = 0.999
ADAM_EPS = 1e-08
ADAM_WD = 0.01
ADAM_STEP = 10
GELU_C = math.sqrt(2.0 / math.pi)
GELU_A = 0.044715

V7X_LANES = 128
V7X_VMEM_LIMIT = 56 * 1024 * 1024
S5_CHANNELS = 128
PACK_TILE = 8 * V7X_LANES

HBM_SPEC = pl.BlockSpec(memory_space=pltpu.HBM)
ANY_SPEC = pl.BlockSpec(memory_space=pl.ANY)
SEM_SPEC = pl.BlockSpec(memory_space=pltpu.SEMAPHORE)


def _params(n_grid):
    return pltpu.CompilerParams(dimension_semantics=("arbitrary",) * n_grid, vmem_limit_bytes=V7X_VMEM_LIMIT)


def _pick(n, pref, align):
    if n <= pref:
        return n
    t = (pref // align) * align
    while t >= align:
        if n % t == 0:
            return t
        t -= align
    raise ValueError(f"no tile for {n} (pref {pref}, align {align})")


MM_RHS_BLOCK_BYTES = 12 * 1024 * 1024
MM_LHS_BLOCK_BYTES = 6 * 1024 * 1024
MM_ACC_BYTES = 6 * 1024 * 1024
MM_ROWS = 512


MM_EPILOGUE_ROWS = 256


def _mm_tiles(m, k, n, a_item, b_item, ta, max_rows):
    tn = _pick(n, max(V7X_LANES, MM_RHS_BLOCK_BYTES // (k * b_item)), V7X_LANES)
    rows = min(max_rows, MM_ACC_BYTES // (4 * tn), MM_LHS_BLOCK_BYTES // (k * a_item))
    align = V7X_LANES if ta else 16
    tm = _pick(m, max(align, rows), align)
    return tm, tn


def _store_results(out_refs, n_row, results, first):
    if not isinstance(results, (tuple, list)):
        results = (results,)
    for o, v in zip(out_refs[:n_row], results[:n_row]):
        if isinstance(v, (tuple, list)):
            off = 0
            for piece in v:
                w = piece.shape[1]
                o[:, off:off + w] = piece.astype(o.dtype)
                off += w
        else:
            o[...] = v.astype(o.dtype)
    if len(out_refs) > n_row:
        @pl.when(first)
        def _():
            for o in out_refs[n_row:]:
                o[...] = jnp.zeros_like(o)

        for o, v in zip(out_refs[n_row:], results[n_row:]):
            o[...] += v


def _mm(name, a, b, *, ta=False, tb=False, out_dtype=BF16, res=None, scale=None, deps=(),
        epilogue=None, row_ins=(), par_ins=(), outs=(), acc_outs=(), prologue=None, prologue_pars=()):
    if ta:
        k, m = a.shape
    else:
        m, k = a.shape
    if tb:
        n, k2 = b.shape
    else:
        k2, n = b.shape
    assert k == k2, (name, a.shape, b.shape)
    max_rows = MM_ROWS if epilogue is None else MM_EPILOGUE_ROWS
    tm, tn = _mm_tiles(m, k, n, a.dtype.itemsize, b.dtype.itemsize, ta, max_rows)
    a_spec = pl.BlockSpec((k, tm), lambda j, i: (0, i)) if ta else pl.BlockSpec((tm, k), lambda j, i: (i, 0))
    b_spec = pl.BlockSpec((tn, k), lambda j, i: (j, 0)) if tb else pl.BlockSpec((k, tn), lambda j, i: (0, j))
    o_spec = pl.BlockSpec((tm, tn), lambda j, i: (i, j))
    dims = (((0 if ta else 1,), (1 if tb else 0,)), ((), ()))
    has_res = res is not None
    ins = [a, b] + ([res] if has_res else [])
    specs = [a_spec, b_spec] + ([o_spec] if has_res else [])
    n_mm = len(ins)
    if epilogue is None:
        out_specs, out_shape = [o_spec], [jax.ShapeDtypeStruct((m, n), out_dtype)]
    else:
        assert tn == n, (name, tn, n)
        for r in row_ins:
            arr, cb, cw = r if isinstance(r, tuple) else (r, 0, r.shape[1])
            assert arr.shape[0] == m, (name, arr.shape, m)
            ins.append(arr)
            specs.append(pl.BlockSpec((tm, cw), lambda j, i, cb=cb: (i, cb)))
        for p in par_ins:
            ins.append(p)
            specs.append(pl.BlockSpec(p.shape, lambda j, i: (0, 0)))
        out_specs = [pl.BlockSpec((tm, c), lambda j, i: (i, 0)) for c, _ in outs]
        out_specs += [pl.BlockSpec((r, c), lambda j, i: (0, 0)) for r, c in acc_outs]
        out_shape = [jax.ShapeDtypeStruct((m, c), dt) for c, dt in outs]
        out_shape += [jax.ShapeDtypeStruct((r, c), F32) for r, c in acc_outs]
    n_in = len(ins)
    if prologue is not None:
        assert tn == n and not ta, (name, tn, n, ta)
        for p in prologue_pars:
            ins.append(p)
            specs.append(pl.BlockSpec(p.shape, lambda j, i: (0, 0)))
        out_specs = out_specs + [a_spec]
        out_shape = out_shape + [jax.ShapeDtypeStruct((m, k), BF16)]
    n_pro = len(ins)
    ins += list(deps)
    specs += [ANY_SPEC] * len(deps)

    def body(*refs):
        a_ref, b_ref = refs[0], refs[1]
        out_refs = refs[n_pro + len(deps):]
        if prologue is None:
            lhs = a_ref[...].astype(BF16)
        else:
            lhs = prologue(a_ref[...], *[r[...] for r in refs[n_in:n_pro]]).astype(BF16)
            out_refs[-1][...] = lhs
            out_refs = out_refs[:-1]
        acc = lax.dot_general(lhs, b_ref[...].astype(BF16), dims, preferred_element_type=F32)
        if scale is not None:
            acc = acc * scale
        if has_res:
            acc = acc + refs[2][...].astype(F32)
        if epilogue is None:
            out_refs[0][...] = acc.astype(out_refs[0].dtype)
        else:
            extra = [r[...] for r in refs[n_mm:n_in]]
            _store_results(out_refs, len(outs), epilogue(acc, *extra), pl.program_id(1) == 0)

    out = pl.pallas_call(
        body, name=name, grid=(n // tn, m // tm), in_specs=specs, out_specs=out_specs, out_shape=out_shape,
        compiler_params=_params(2),
    )(*ins)
    return out[0] if epilogue is None and prologue is None else out


def _rowwise(name, fn, rows, row_ins, par_ins, row_outs, acc_outs=(), tm_pref=256):
    tm = _pick(rows, tm_pref, 16)
    in_specs, ins = [], []
    for r in row_ins:
        arr, cb, cw = r if isinstance(r, tuple) else (r, 0, r.shape[1])
        assert arr.shape[0] == rows, (name, arr.shape, rows)
        ins.append(arr)
        in_specs.append(pl.BlockSpec((tm, cw), lambda i, cb=cb: (i, cb)))
    for p in par_ins:
        ins.append(p)
        in_specs.append(pl.BlockSpec(p.shape, lambda i: (0, 0)))
    out_specs = [pl.BlockSpec((tm, c), lambda i: (i, 0)) for c, _ in row_outs]
    out_specs += [pl.BlockSpec((r, c), lambda i: (0, 0)) for r, c in acc_outs]
    out_shape = [jax.ShapeDtypeStruct((rows, c), dt) for c, dt in row_outs]
    out_shape += [jax.ShapeDtypeStruct((r, c), F32) for r, c in acc_outs]
    n_in, n_row = len(ins), len(row_outs)

    def body(*refs):
        vals = [r[...] for r in refs[:n_in]]
        _store_results(refs[n_in:], n_row, fn(*vals), pl.program_id(0) == 0)

    return pl.pallas_call(
        body, name=name, grid=(rows // tm,), in_specs=in_specs, out_specs=out_specs, out_shape=out_shape,
        compiler_params=_params(1),
    )(*ins)


def _inv_rms(x):
    return lax.rsqrt(jnp.mean(x * x, axis=-1, keepdims=True) + NORM_EPS)


def _rms_rows(x, g):
    return x * _inv_rms(x) * g


def _rms_fwd(name, x, g):
    return _rowwise(name, _rms_rows, x.shape[0], [x], [g], [(x.shape[1], BF16)], tm_pref=512)[0]


def _rms_bwd_rows(dn, x, dres, g):
    r = _inv_rms(x)
    xh = x * r
    dg = jnp.sum(dn * xh, axis=0, keepdims=True)
    dxh = dn * g
    dx = r * (dxh - xh * jnp.mean(dxh * xh, axis=-1, keepdims=True)) + dres
    return dx, dx, dg


def _rms_bwd(name, x, g, dn, dres=None):
    d = x.shape[1]
    if dres is None:
        def fn(x, dn, g):
            return (jnp.sum(dn.astype(F32) * (x * _inv_rms(x)), axis=0, keepdims=True),)

        return None, _rowwise(name, fn, x.shape[0], [x, dn], [g], [], [(1, d)])[0]

    def fn(x, dn, dres, g):
        return _rms_bwd_rows(dn.astype(F32), x, dres, g)

    out = _rowwise(name, fn, x.shape[0], [x, dn, dres], [g], [(d, F32), (d, BF16)], [(1, d)])
    return (out[0], out[1]), out[2]


def _mm_rms_bwd(name, dy, w, x, g, dres, tb=True):
    d = x.shape[1]
    out = _mm(name, dy, w, tb=tb, epilogue=_rms_bwd_rows, row_ins=[x, dres], par_ins=[g],
              outs=[(d, F32), (d, BF16)], acc_outs=[(1, d)])
    return (out[0], out[1]), out[2]


def _sigmoid(x):
    return 0.5 + 0.5 * jnp.tanh(0.5 * x)


def _swiglu_rows(gu, f):
    gt = gu[:, :f]
    return gu, gt * _sigmoid(gt) * gu[:, f:]


def _swiglu_bwd_rows(dact, gt, up):
    gt, up = gt.astype(F32), up.astype(F32)
    sg = _sigmoid(gt)
    return ((dact * up * (sg * (1.0 + gt * (1.0 - sg))), dact * (gt * sg)),)


def _glu_res(name, vg, x, d):
    def fn(val, gate, x):
        return x + val.astype(F32) * _sigmoid(gate.astype(F32))

    return _rowwise(name, fn, x.shape[0], [(vg, 0, d), (vg, 1, d), x], [], [(d, F32)])[0]


def _glu_bwd(name, dres, vg, d):
    def fn(dres, val, gate):
        val, gate = val.astype(F32), gate.astype(F32)
        sg = _sigmoid(gate)
        return ((dres * sg, dres * val * sg * (1.0 - sg)),)

    return _rowwise(name, fn, dres.shape[0], [dres, (vg, 0, d), (vg, 1, d)], [], [(2 * d, BF16)])[0]


def _final_loss(name, x, g, tgt):
    d = x.shape[1]

    def fn(x, tgt, g):
        r = _inv_rms(x)
        xh = x * r
        err = xh * g - tgt
        dy = err * (1.0 / d)
        dxh = dy * g
        dx = r * (dxh - xh * jnp.mean(dxh * xh, axis=-1, keepdims=True))
        return dx, dx, jnp.sum(err * err, axis=0, keepdims=True), jnp.sum(dy * xh, axis=0, keepdims=True)

    dx, dx16, err2, dg = _rowwise(name, fn, x.shape[0], [x, tgt], [g], [(d, F32), (d, BF16)], [(1, d), (1, d)])
    return (dx, dx16), err2, dg


def _shift_down(u, k):
    rows = lax.broadcasted_iota(jnp.int32, u.shape, 0)
    return jnp.where(rows >= k, pltpu.roll(u, k, 0), 0.0)


def _shift_up(u, k):
    n = u.shape[0]
    rows = lax.broadcasted_iota(jnp.int32, u.shape, 0)
    return jnp.where(rows < n - k, pltpu.roll(u, n - k, 0), 0.0)


def _conv_specs(seq, cw, n_cb, swap):
    def at(off):
        if swap:
            return pl.BlockSpec((seq, cw), lambda j, b: (b, off * n_cb + j))
        return pl.BlockSpec((seq, cw), lambda b, j: (b, off * n_cb + j))

    return at


def _conv_fwd(name, cbv, w, n_seq, seq):
    d = w.shape[1]
    cw = _pick(d, 256, V7X_LANES)
    n_cb = d // cw
    at = _conv_specs(seq, cw, n_cb, swap=False)

    def body(c_ref, b_ref, v_ref, w_ref, z_ref):
        u = c_ref[...].astype(F32) * v_ref[...].astype(F32)
        cv = w_ref[0:1, :] * _shift_down(u, 2) + w_ref[1:2, :] * _shift_down(u, 1) + w_ref[2:3, :] * u
        z_ref[...] = (b_ref[...].astype(F32) * cv).astype(z_ref.dtype)

    return pl.pallas_call(
        body, name=name, grid=(n_seq, n_cb),
        in_specs=[at(0), at(1), at(2), pl.BlockSpec((CONV_WIDTH, cw), lambda b, j: (0, j))],
        out_specs=at(0), out_shape=jax.ShapeDtypeStruct((n_seq * seq, d), BF16), compiler_params=_params(2),
    )(cbv, cbv, cbv, w)


def _conv_bwd(name, dz, cbv, w, n_seq, seq):
    d = w.shape[1]
    cw = _pick(d, 256, V7X_LANES)
    n_cb = d // cw
    at = _conv_specs(seq, cw, n_cb, swap=True)

    def body(dz_ref, c_ref, b_ref, v_ref, w_ref, dc_ref, db_ref, dv_ref, dw_ref):
        c, b, v = c_ref[...].astype(F32), b_ref[...].astype(F32), v_ref[...].astype(F32)
        dz = dz_ref[...].astype(F32)
        w0, w1, w2 = w_ref[0:1, :], w_ref[1:2, :], w_ref[2:3, :]
        u = c * v
        u1, u2 = _shift_down(u, 1), _shift_down(u, 2)
        cv = w0 * u2 + w1 * u1 + w2 * u
        db_ref[...] = (dz * cv).astype(db_ref.dtype)
        dcv = dz * b
        du = w2 * dcv + w1 * _shift_up(dcv, 1) + w0 * _shift_up(dcv, 2)
        dc_ref[...] = (du * v).astype(dc_ref.dtype)
        dv_ref[...] = (du * c).astype(dv_ref.dtype)

        @pl.when(pl.program_id(1) == 0)
        def _():
            dw_ref[...] = jnp.zeros_like(dw_ref)

        dw_ref[0:1, :] += jnp.sum(dcv * u2, axis=0, keepdims=True)
        dw_ref[1:2, :] += jnp.sum(dcv * u1, axis=0, keepdims=True)
        dw_ref[2:3, :] += jnp.sum(dcv * u, axis=0, keepdims=True)

    act = jax.ShapeDtypeStruct((n_seq * seq, d), BF16)
    return pl.pallas_call(
        body, name=name, grid=(n_cb, n_seq),
        in_specs=[at(0), at(0), at(1), at(2), pl.BlockSpec((CONV_WIDTH, cw), lambda j, b: (0, j))],
        out_specs=[at(0), at(0), at(0), pl.BlockSpec((CONV_WIDTH, cw), lambda j, b: (0, j))],
        out_shape=[act, act, act, jax.ShapeDtypeStruct((CONV_WIDTH, d), F32)], compiler_params=_params(2),
    )(dz, cbv, cbv, cbv, w)


def _s5_discretize(a_re, a_im, log_dt, b_re, b_im):
    lam_re = jnp.minimum(a_re, EIG_CLIP)
    lam_im = a_im
    dt = jnp.exp(log_dt)[:, None]
    mag = jnp.exp(lam_re * dt)
    abar_re = mag * jnp.cos(lam_im * dt)
    abar_im = mag * jnp.sin(lam_im * dt)
    den = lam_re * lam_re + lam_im * lam_im
    num_re = abar_re - 1.0
    num_im = abar_im
    coef_re = (num_re * lam_re + num_im * lam_im) / den
    coef_im = (num_im * lam_re - num_re * lam_im) / den
    bbar_re = coef_re[..., None] * b_re - coef_im[..., None] * b_im
    bbar_im = coef_re[..., None] * b_im + coef_im[..., None] * b_re
    return abar_re, abar_im, bbar_re, bbar_im


def _block_diag_in(bbar, gb):
    g, p, h = bbar.shape
    t = jnp.transpose(bbar.reshape(g // gb, gb, p, h), (0, 1, 3, 2))
    return jnp.einsum("cghp,gk->cghkp", t, jnp.eye(gb, dtype=bbar.dtype)).reshape(g // gb, gb * h, gb * p)


def _block_diag_in_t(blk, gb, p, h):
    nb = blk.shape[0]
    t = jnp.einsum("cghkp,gk->cghp", blk.reshape(nb, gb, h, gb, p), jnp.eye(gb, dtype=blk.dtype))
    return jnp.transpose(t, (0, 1, 3, 2)).reshape(nb * gb, p, h)


def _block_diag_out(c, gb):
    g, h, p = c.shape
    t = jnp.transpose(c.reshape(g // gb, gb, h, p), (0, 1, 3, 2))
    return jnp.einsum("cgph,gk->cgpkh", t, jnp.eye(gb, dtype=c.dtype)).reshape(g // gb, gb * p, gb * h)


def _block_diag_out_t(blk, gb, p, h):
    nb = blk.shape[0]
    t = jnp.einsum("cgpkh,gk->cgph", blk.reshape(nb, gb, p, gb, h), jnp.eye(gb, dtype=blk.dtype))
    return jnp.transpose(t, (0, 1, 3, 2)).reshape(nb * gb, h, p)


def _gelu(y):
    return 0.5 * y * (1.0 + jnp.tanh(GELU_C * (y + GELU_A * y * y * y)))


def _gelu_grad(y):
    th = jnp.tanh(GELU_C * (y + GELU_A * y * y * y))
    return 0.5 * (1.0 + th) + 0.5 * y * (1.0 - th * th) * GELU_C * (1.0 + 3.0 * GELU_A * y * y)


def _dot(a, b, ca, cb):
    return lax.dot_general(a.astype(BF16), b.astype(BF16), (((ca,), (cb,)), ((), ())), preferred_element_type=F32)


def _s5_specs(seq, ch, sb):
    act = pl.BlockSpec((seq, ch), lambda j, b: (b, j))
    state = pl.BlockSpec((seq, sb), lambda j, b: (b, j))
    w_in = pl.BlockSpec((None, ch, sb), lambda j, b: (j, 0, 0))
    w_out = pl.BlockSpec((None, sb, ch), lambda j, b: (j, 0, 0))
    lane_s = pl.BlockSpec((1, sb), lambda j, b: (0, j))
    lane_c = pl.BlockSpec((1, ch), lambda j, b: (0, j))
    return act, state, w_in, w_out, lane_s, lane_c


def _s5_fwd(name, h, bin_re, bin_im, cout_re, cout_im, abar_re, abar_im, dskip, n_seq, seq):
    t, d = h.shape
    nb, ch, sb = bin_re.shape
    act, state, w_in, w_out, lane_s, lane_c = _s5_specs(seq, ch, sb)

    def body(h_ref, bre_ref, bim_ref, cre_ref, cim_ref, ar_ref, ai_ref, d_ref, sre_ref, sim_ref, y_ref, z_ref):
        u = h_ref[...]
        sre_ref[...] = _dot(u, bre_ref[...], 1, 0)
        sim_ref[...] = _dot(u, bim_ref[...], 1, 0)
        ar, ai = ar_ref[...], ai_ref[...]

        def step(i, carry):
            sr, si = carry
            row = pl.ds(i, 1)
            nr = ar * sr - ai * si + sre_ref[row, :]
            ni = ar * si + ai * sr + sim_ref[row, :]
            sre_ref[row, :] = nr
            sim_ref[row, :] = ni
            return nr, ni

        zero = jnp.zeros((1, sb), F32)
        lax.fori_loop(0, seq, step, (zero, zero), unroll=8)
        y = _dot(sre_ref[...], cre_ref[...], 1, 0) - _dot(sim_ref[...], cim_ref[...], 1, 0)
        y = y + d_ref[...] * u.astype(F32)
        y_ref[...] = y
        z_ref[...] = _gelu(y).astype(z_ref.dtype)

    return pl.pallas_call(
        body, name=name, grid=(nb, n_seq),
        in_specs=[act, w_in, w_in, w_out, w_out, lane_s, lane_s, lane_c],
        out_specs=[state, state, act, act],
        out_shape=[jax.ShapeDtypeStruct((t, nb * sb), F32), jax.ShapeDtypeStruct((t, nb * sb), F32),
                   jax.ShapeDtypeStruct((t, d), F32), jax.ShapeDtypeStruct((t, d), BF16)],
        compiler_params=_params(2),
    )(h, bin_re, bin_im, cout_re, cout_im, abar_re, abar_im, dskip)


def _s5_bwd(name, dz, ypre, h, s_re, s_im, bin_re, bin_im, cout_re, cout_im, abar_re, abar_im, dskip, n_seq, seq):
    t, d = h.shape
    nb, ch, sb = bin_re.shape
    act, state, w_in, w_out, lane_s, lane_c = _s5_specs(seq, ch, sb)

    def body(dz_ref, y_ref, h_ref, sre_ref, sim_ref, bre_ref, bim_ref, cre_ref, cim_ref, ar_ref, ai_ref, d_ref,
             dh_ref, dbre_ref, dbim_ref, dcre_ref, dcim_ref, dar_ref, dai_ref, dd_ref, gre, gim):
        first = pl.program_id(1) == 0
        u = h_ref[...].astype(F32)
        dy = dz_ref[...].astype(F32) * _gelu_grad(y_ref[...])
        gre[...] = _dot(dy, cre_ref[...], 1, 1)
        gim[...] = -_dot(dy, cim_ref[...], 1, 1)
        ar, ai = ar_ref[...], ai_ref[...]

        def step(i, carry):
            gr, gi = carry
            row = pl.ds(seq - 1 - i, 1)
            nr = gre[row, :] + ar * gr + ai * gi
            ni = gim[row, :] - ai * gr + ar * gi
            gre[row, :] = nr
            gim[row, :] = ni
            return nr, ni

        zero = jnp.zeros((1, sb), F32)
        lax.fori_loop(0, seq, step, (zero, zero), unroll=8)

        g_re, g_im = gre[...], gim[...]
        s_re, s_im = sre_ref[...], sim_ref[...]
        p_re, p_im = _shift_down(s_re, 1), _shift_down(s_im, 1)
        dar = jnp.sum(g_re * p_re + g_im * p_im, axis=0, keepdims=True)
        dai = jnp.sum(g_im * p_re - g_re * p_im, axis=0, keepdims=True)
        dbre = _dot(u, g_re, 0, 0)
        dbim = _dot(u, g_im, 0, 0)
        dcre = _dot(s_re, dy, 0, 0)
        dcim = -_dot(s_im, dy, 0, 0)
        ddd = jnp.sum(dy * u, axis=0, keepdims=True)
        dh_ref[...] = _dot(g_re, bre_ref[...], 1, 1) + _dot(g_im, bim_ref[...], 1, 1) + d_ref[...] * dy

        @pl.when(first)
        def _():
            dar_ref[...] = dar
            dai_ref[...] = dai
            dbre_ref[...] = dbre
            dbim_ref[...] = dbim
            dcre_ref[...] = dcre
            dcim_ref[...] = dcim
            dd_ref[...] = ddd

        @pl.when(jnp.logical_not(first))
        def _():
            dar_ref[...] += dar
            dai_ref[...] += dai
            dbre_ref[...] += dbre
            dbim_ref[...] += dbim
            dcre_ref[...] += dcre
            dcim_ref[...] += dcim
            dd_ref[...] += ddd

    return pl.pallas_call(
        body, name=name, grid=(nb, n_seq),
        in_specs=[act, act, act, state, state, w_in, w_in, w_out, w_out, lane_s, lane_s, lane_c],
        out_specs=[act, w_in, w_in, w_out, w_out, lane_s, lane_s, lane_c],
        out_shape=[jax.ShapeDtypeStruct((t, d), F32),
                   jax.ShapeDtypeStruct((nb, ch, sb), F32), jax.ShapeDtypeStruct((nb, ch, sb), F32),
                   jax.ShapeDtypeStruct((nb, sb, ch), F32), jax.ShapeDtypeStruct((nb, sb, ch), F32),
                   jax.ShapeDtypeStruct((1, nb * sb), F32), jax.ShapeDtypeStruct((1, nb * sb), F32),
                   jax.ShapeDtypeStruct((1, d), F32)],
        scratch_shapes=[pltpu.VMEM((seq, sb), F32), pltpu.VMEM((seq, sb), F32)],
        compiler_params=_params(2),
    )(dz, ypre, h, s_re, s_im, bin_re, bin_im, cout_re, cout_im, abar_re, abar_im, dskip)


ATTN_QUERY_ROWS = 1024


def _softmax_rows(q, k, scale):
    s = _dot(q, k, 1, 1) * scale
    e = jnp.exp(s - jnp.max(s, axis=-1, keepdims=True))
    return e * (1.0 / jnp.sum(e, axis=-1, keepdims=True))


def _attn_fwd(name, q, kv, n_seq, seq, mlen, heads):
    t, d = q.shape
    hd = d // heads
    tq = _pick(seq, ATTN_QUERY_ROWS, 16)
    nq = seq // tq
    scale = hd ** -0.5
    q_spec = pl.BlockSpec((tq, hd), lambda b, h, i: (b * nq + i, h))

    def body(q_ref, k_ref, v_ref, o_ref):
        p = _softmax_rows(q_ref[...], k_ref[...], scale)
        o_ref[...] = _dot(p, v_ref[...], 1, 0).astype(o_ref.dtype)

    return pl.pallas_call(
        body, name=name, grid=(n_seq, heads, nq),
        in_specs=[q_spec, pl.BlockSpec((mlen, hd), lambda b, h, i: (b, h)),
                  pl.BlockSpec((mlen, hd), lambda b, h, i: (b, heads + h))],
        out_specs=q_spec, out_shape=jax.ShapeDtypeStruct((t, d), BF16), compiler_params=_params(3),
    )(q, kv, kv)


def _attn_bwd(name, q, kv, do, n_seq, seq, mlen, heads):
    t, d = q.shape
    hd = d // heads
    tq = _pick(seq, ATTN_QUERY_ROWS, 16)
    nq = seq // tq
    scale = hd ** -0.5
    q_spec = pl.BlockSpec((tq, hd), lambda b, h, i: (b * nq + i, h))
    k_spec = pl.BlockSpec((mlen, hd), lambda b, h, i: (b, h))

    def body(q_ref, k_ref, v_ref, do_ref, dq_ref, dk_ref, dv_ref):
        q, k, v, do = q_ref[...], k_ref[...], v_ref[...], do_ref[...]
        p = _softmax_rows(q, k, scale)
        dp = _dot(do, v, 1, 1)
        ds = p * (dp - jnp.sum(dp * p, axis=-1, keepdims=True)) * scale
        dq_ref[...] = _dot(ds, k, 1, 0).astype(dq_ref.dtype)

        @pl.when(pl.program_id(2) == 0)
        def _():
            dk_ref[...] = jnp.zeros_like(dk_ref)
            dv_ref[...] = jnp.zeros_like(dv_ref)

        dk_ref[...] += _dot(ds, q, 0, 0)
        dv_ref[...] += _dot(p, do, 0, 0)

    return pl.pallas_call(
        body, name=name, grid=(n_seq, heads, nq),
        in_specs=[q_spec, k_spec, pl.BlockSpec((mlen, hd), lambda b, h, i: (b, heads + h)), q_spec],
        out_specs=[q_spec, k_spec, k_spec],
        out_shape=[jax.ShapeDtypeStruct((t, d), BF16), jax.ShapeDtypeStruct((n_seq * mlen, d), F32),
                   jax.ShapeDtypeStruct((n_seq * mlen, d), F32)],
        compiler_params=_params(3),
    )(q, kv, kv, do)


ADAMW_BLOCK_ELEMS = 128 * 1024


def _adamw(name, parts, w, m, v, first_layer=0, earlier=None, transposed=False):
    n_layers = len(parts)
    r, c = parts[0].shape[1:][::-1] if transposed else parts[0].shape[1:]
    assert w.shape[0] % r == 0 and w.shape[1] == c and first_layer + n_layers <= w.shape[0] // r, (name, w.shape)
    tr = _pick(r, max(V7X_LANES, ADAMW_BLOCK_ELEMS // c // V7X_LANES * V7X_LANES), V7X_LANES if transposed else 8)
    nt = r // tr
    spec = pl.BlockSpec((tr, c), lambda l, i: ((first_layer + l) * nt + i, 0))
    c1 = 1.0 - ADAM_B1 ** ADAM_STEP
    c2 = 1.0 - ADAM_B2 ** ADAM_STEP

    def parts_spec(q):
        def at(l, i):
            return jnp.where(l == q, i, jnp.where(l > q, nt - 1, 0))

        if transposed:
            return pl.BlockSpec((N_DEV, c, tr), lambda l, i: (0, 0, at(l, i)))
        return pl.BlockSpec((N_DEV, tr, c), lambda l, i: (0, at(l, i), 0))

    earlier = list(earlier or ())

    def body(*refs):
        p_refs = refs[:n_layers]
        w_ref, m_ref, v_ref = refs[n_layers:n_layers + 3]
        g_ref, d_ref, nm_ref, nv_ref = refs[n_layers + 3 + len(earlier):]

        def update(p_ref):
            g = p_ref[0].astype(F32)
            for k in range(1, N_DEV):
                g = g + p_ref[k].astype(F32)
            if transposed:
                g = g.T
            nm = ADAM_B1 * m_ref[...] + (1.0 - ADAM_B1) * g
            nv = ADAM_B2 * v_ref[...] + (1.0 - ADAM_B2) * (g * g)
            g_ref[...] = g
            nm_ref[...] = nm
            nv_ref[...] = nv
            d_ref[...] = -ADAM_LR * ((nm / c1) / (jnp.sqrt(nv / c2) + ADAM_EPS) + ADAM_WD * w_ref[...])

        for q in range(n_layers):
            pl.when(pl.program_id(0) == q)(lambda q=q: update(p_refs[q]))

    out = jax.ShapeDtypeStruct(w.shape, F32)
    return pl.pallas_call(
        body, name=name, grid=(n_layers, nt),
        in_specs=[parts_spec(q) for q in range(n_layers)] + [spec] * 3 + [ANY_SPEC] * len(earlier),
        out_specs=[spec] * 4, out_shape=[out] * 4, compiler_params=_params(2),
        input_output_aliases={n_layers + 3 + q: q for q in range(len(earlier))},
    )(*parts, w, m, v, *earlier)


def _place():
    x, y, c = lax.axis_index("x"), lax.axis_index("y"), lax.axis_index("c")
    return x, y, c


def _index(px, py, pc):
    return 4 * px + 2 * py + pc


def _all_gather(name, shards):
    n = len(shards)

    def body(*refs):
        in_refs, out_refs = refs[:n], refs[n:2 * n]
        send_sems, recv_sems, local_sems = refs[2 * n:]
        x, y, c = _place()
        me, sibling = (x, y, c), (x, y, 1 - c)
        chips = [(1 - x, y), (x, 1 - y), (1 - x, 1 - y)]

        def slot(k, block):
            return out_refs[k].at[_index(*block)]

        def copy(k, j, block, to, src=None):
            return pltpu.make_async_remote_copy(
                src_ref=slot(k, block) if src is None else src, dst_ref=slot(k, block),
                send_sem=send_sems.at[7 * k + j], recv_sem=recv_sems.at[7 * k + j], device_id=to, device_id_type=MESH)

        mine = [pltpu.make_async_copy(in_refs[k], slot(k, me), local_sems.at[k]) for k in range(n)]
        for cp in mine:
            cp.start()
        first = []
        for k in range(n):
            first.append(copy(k, 0, me, sibling, src=in_refs[k]))
            first += [copy(k, 1 + j, me, (*chip, c), src=in_refs[k]) for j, chip in enumerate(chips)]
        for cp in first:
            cp.start()
        passed = []
        for j, chip in enumerate(chips):
            for k in range(n):
                copy(k, 1 + j, (*chip, c), me).wait_recv()
                cp = copy(k, 4 + j, (*chip, c), sibling)
                cp.start()
                passed.append(cp)
        for k in range(n):
            copy(k, 0, sibling, me).wait_recv()
            for j, chip in enumerate(chips):
                copy(k, 4 + j, (*chip, 1 - c), me).wait_recv()
        for cp in first + passed:
            cp.wait_send()
        for cp in mine:
            cp.wait()

    return pl.pallas_call(
        body, name=name, in_specs=[HBM_SPEC] * n, out_specs=[HBM_SPEC] * n,
        out_shape=[jax.ShapeDtypeStruct((N_DEV,) + s.shape, s.dtype) for s in shards],
        scratch_shapes=[pltpu.SemaphoreType.DMA((7 * n,)), pltpu.SemaphoreType.DMA((7 * n,)),
                        pltpu.SemaphoreType.DMA((n,))],
    )(*shards)


WHOLE, BLOCK, COLUMNS = "whole", "block", "columns"


def _slot(ref, index, mode):
    if mode == WHOLE:
        return ref
    if mode == BLOCK:
        return ref.at[index]
    width = ref.shape[1] // N_DEV
    return ref.at[:, pl.ds(pl.multiple_of(index * width, width), width)]


DIRECT = tuple(range(1, N_DEV))
CHIPS = (1, 4, 2, 6)
FORWARD = "forward"


def _copies(plan, x, y, c):
    def xor(r, flip_core=False):
        rx, ry, rc = (r >> 2) & 1, (r >> 1) & 1, (r & 1) ^ int(flip_core)
        return (1 - x if rx else x, 1 - y if ry else y, 1 - c if rc else c)

    me = _index(x, y, c)
    if plan == FORWARD:
        return [(xor(1), _index(*xor(r)), _index(*xor(r)), _index(*xor(r, True))) for r in (4, 2, 6)]
    return [(xor(r), _index(*xor(r)), me, _index(*xor(r))) for r in plan]


def _exchange_start(name, srcs, lands, src_modes, land_modes, deps=(), plan=DIRECT):
    n, n_src = len(lands), len(srcs)
    n_copies = 3 if plan == FORWARD else len(plan)

    def body(*refs):
        land_refs = refs[n_src:n_src + n]
        src_refs = refs[:n_src] if n_src else land_refs
        send_sems, recv_sems = refs[n_src + n + len(deps)], refs[n_src + n + len(deps) + 1]
        token = refs[-1]
        x, y, c = _place()
        for k in range(n):
            for j, (peer, src_index, dst_index, _) in enumerate(_copies(plan, x, y, c)):
                pltpu.make_async_remote_copy(
                    src_ref=_slot(src_refs[k], src_index, src_modes[k]),
                    dst_ref=_slot(land_refs[k], dst_index, land_modes[k]), send_sem=send_sems.at[n_copies * k + j],
                    recv_sem=recv_sems.at[n_copies * k + j], device_id=peer, device_id_type=MESH).start()
        token[...] = jnp.zeros_like(token)

    arrays = list(srcs) + list(lands)
    thru = [pltpu.HBM(a.shape, a.dtype) for a in arrays]
    out = pl.pallas_call(
        body, name=name,
        out_shape=(pltpu.SemaphoreType.DMA((n_copies * n,)), pltpu.SemaphoreType.DMA((n_copies * n,)), *thru,
                   jax.ShapeDtypeStruct((8, V7X_LANES), F32)),
        in_specs=[HBM_SPEC] * len(arrays) + [ANY_SPEC] * len(deps),
        out_specs=(SEM_SPEC, SEM_SPEC, *([HBM_SPEC] * len(arrays)), pl.BlockSpec(memory_space=pltpu.VMEM)),
        input_output_aliases={k: 2 + k for k in range(len(arrays))},
        compiler_params=pltpu.CompilerParams(has_side_effects=pltpu.SideEffectType.DATAFLOW_SIDE_EFFECTING),
    )(*[pltpu.with_memory_space_constraint(a, pltpu.HBM) for a in arrays], *deps)
    return out[0], out[1], list(out[2:2 + n_src]), list(out[2 + n_src:2 + n_src + n]), out[-1]


def _exchange_wait(name, send_sems, recv_sems, srcs, lands, src_modes, land_modes, after, plan=DIRECT):
    n, n_src = len(lands), len(srcs)
    n_copies = 3 if plan == FORWARD else len(plan)

    def body(*refs):
        land_refs = refs[n_src:n_src + n]
        src_refs = refs[:n_src] if n_src else land_refs
        send_sems, recv_sems = refs[n_src + n], refs[n_src + n + 1]
        x, y, c = _place()
        for k in range(n):
            for j, (peer, src_index, _, arrival_index) in enumerate(_copies(plan, x, y, c)):
                cp = pltpu.make_async_remote_copy(
                    src_ref=_slot(src_refs[k], src_index, src_modes[k]),
                    dst_ref=_slot(land_refs[k], arrival_index, land_modes[k]), send_sem=send_sems.at[n_copies * k + j],
                    recv_sem=recv_sems.at[n_copies * k + j], device_id=peer, device_id_type=MESH)
                cp.wait_send()
                cp.wait_recv()

    arrays = list(srcs) + list(lands)
    thru = [pltpu.HBM(a.shape, a.dtype) for a in arrays]
    out = pl.pallas_call(
        body, name=name, out_shape=tuple(thru),
        in_specs=[HBM_SPEC] * len(arrays) + [SEM_SPEC, SEM_SPEC, ANY_SPEC], out_specs=tuple([HBM_SPEC] * len(arrays)),
        input_output_aliases={k: k for k in range(len(arrays))},
        compiler_params=pltpu.CompilerParams(has_side_effects=pltpu.SideEffectType.DATAFLOW_SIDE_EFFECTING),
    )(*arrays, send_sems, recv_sems, after)
    return list(out[n_src:])


def _landing(shard, me, mode=BLOCK):
    if mode == COLUMNS:
        k, n = shard.shape
        return lax.dynamic_update_slice(lax.empty((k, N_DEV * n), shard.dtype), shard, (0, me * n))
    zone = lax.empty((N_DEV,) + shard.shape, shard.dtype)
    return lax.dynamic_update_slice(zone, shard[None], (me,) + (0,) * shard.ndim)


def _cols_whole(w):
    return jnp.transpose(w, (1, 0, 2)).reshape(w.shape[1], N_DEV * w.shape[2])


def _rows_whole(w):
    return w.reshape(N_DEV * w.shape[1], w.shape[2])


def _cols_parts(dw):
    k, n8 = dw.shape
    return jnp.transpose(dw.reshape(k, N_DEV, n8 // N_DEV), (1, 0, 2))


def _rows_parts(dw):
    r8, c = dw.shape
    return dw.reshape(N_DEV, r8 // N_DEV, c)


def _pack_rows(arrays):
    rows = []
    for a in arrays:
        flat = a.reshape(-1).astype(F32)
        flat = jnp.pad(flat, [(0, (-flat.shape[0]) % PACK_TILE)])
        rows.append(flat.reshape(-1, V7X_LANES))
    return jnp.concatenate(rows, axis=0)


def _unpack_rows(packed, shapes):
    out, row = [], 0
    for s in shapes:
        size = math.prod(s)
        n_rows = -(-size // PACK_TILE) * 8
        out.append(packed[row:row + n_rows].reshape(-1)[:size].reshape(s))
        row += n_rows
    return out


def _merge2d(a):
    return a.reshape(-1, a.shape[-1])


def _ffn_fwd(tag, x, g, w_up_t, w_down, deps=()):
    f = w_down.shape[0]
    gu, act, n = _mm(f"{tag}_up", x, w_up_t, tb=True, prologue=_rms_rows, prologue_pars=[g], deps=deps,
                     epilogue=lambda acc: _swiglu_rows(acc, f), outs=[(2 * f, BF16), (f, BF16)])
    out = _mm(f"{tag}_down", act, w_down, res=x, scale=0.5, out_dtype=F32)
    return out, (x, n, gu, act)


def _ffn_bwd(tag, dres, saved, g, w_up_t, w_down, deps=()):
    x, n, gu, act = saved
    dres32, dres16 = dres
    f = w_down.shape[0]
    dgu = _mm(f"{tag}_down_dx", dres16, w_down, tb=True, scale=0.5, deps=deps, epilogue=_swiglu_bwd_rows,
              row_ins=[(gu, 0, f), (gu, 1, f)], outs=[(2 * f, BF16)])[0]
    d_down = _mm(f"{tag}_down_dw", act, dres16, ta=True, scale=0.5)
    d_up_t = _mm(f"{tag}_up_dw", dgu, n, ta=True)
    dx, dg = _mm_rms_bwd(f"{tag}_up_dx", dgu, w_up_t, x, g, dres32, tb=False)
    return dx, dg, d_up_t, d_down


def _conv_mixer_fwd(tag, x, g, w_in, w_conv, w_out, n_seq, seq):
    cbv, h = _mm(f"{tag}_in", x, w_in, prologue=_rms_rows, prologue_pars=[g])
    z = _conv_fwd(f"{tag}_conv", cbv, w_conv, n_seq, seq)
    out = _mm(f"{tag}_out", z, w_out, res=x, out_dtype=F32)
    return out, (x, h, cbv, z)


def _conv_mixer_bwd(tag, dres, saved, g, w_in, w_conv, w_out, n_seq, seq):
    x, h, cbv, z = saved
    dres32, dres16 = dres
    dz = _mm(f"{tag}_out_dx", dres16, w_out, tb=True)
    d_out = _mm(f"{tag}_out_dw", z, dres16, ta=True)
    dc, db, dv, d_conv = _conv_bwd(f"{tag}_conv_bwd", dz, cbv, w_conv, n_seq, seq)
    dcbv = jnp.concatenate([dc, db, dv], axis=1)
    d_in = _mm(f"{tag}_in_dw", h, dcbv, ta=True)
    dx, dg = _mm_rms_bwd(f"{tag}_in_dx", dcbv, w_in, x, g, dres32)
    return dx, dg, d_in, d_conv, d_out


def _s5_mixer_fwd(tag, x, g, ssm, dskip, w_glu, n_seq, seq):
    a_re, a_im, log_dt, b_re, b_im, c_re, c_im = ssm
    groups, p, hh = b_re.shape
    gb = S5_CHANNELS // hh
    disc, disc_vjp = jax.vjp(_s5_discretize, a_re, a_im, log_dt, b_re, b_im)
    abar_re, abar_im, bbar_re, bbar_im = disc
    mats = (_block_diag_in(bbar_re, gb).astype(BF16), _block_diag_in(bbar_im, gb).astype(BF16),
            _block_diag_out(c_re, gb).astype(BF16), _block_diag_out(c_im, gb).astype(BF16),
            abar_re.reshape(1, groups * p), abar_im.reshape(1, groups * p), dskip)
    d = x.shape[1]
    h = _rms_fwd(f"{tag}_norm", x, g)
    s_re, s_im, ypre, z = _s5_fwd(f"{tag}_scan", h, *mats, n_seq, seq)
    vg = _mm(f"{tag}_glu", z, w_glu)
    out = _glu_res(f"{tag}_glu_act", vg, x, d)
    return out, (x, h, s_re, s_im, ypre, z, vg, mats, disc_vjp, (groups, p, hh, gb))


def _s5_mixer_bwd(tag, dres, saved, g, w_glu, n_seq, seq):
    x, h, s_re, s_im, ypre, z, vg, mats, disc_vjp, (groups, p, hh, gb) = saved
    d = x.shape[1]
    dres32, _ = dres
    dvg = _glu_bwd(f"{tag}_glu_act_bwd", dres32, vg, d)
    d_glu = _mm(f"{tag}_glu_dw", z, dvg, ta=True)
    dz = _mm(f"{tag}_glu_dx", dvg, w_glu, tb=True)
    dh, dbin_re, dbin_im, dcout_re, dcout_im, dabar_re, dabar_im, d_skip = _s5_bwd(
        f"{tag}_scan_bwd", dz, ypre, h, s_re, s_im, *mats, n_seq, seq)
    d_are, d_aim, d_logdt, d_bre, d_bim = disc_vjp((
        dabar_re.reshape(groups, p), dabar_im.reshape(groups, p),
        _block_diag_in_t(dbin_re, gb, p, hh), _block_diag_in_t(dbin_im, gb, p, hh)))
    d_cre = _block_diag_out_t(dcout_re, gb, p, hh)
    d_cim = _block_diag_out_t(dcout_im, gb, p, hh)
    dx, dg = _rms_bwd(f"{tag}_norm_bwd", x, g, dh, dres32)
    return dx, dg, (d_are, d_aim, d_logdt, d_bre, d_bim, d_cre, d_cim), d_skip, d_glu


def _xattn_fwd(tag, x, mem, g_q, g_mem, w_q, w_kv, w_o, n_seq, seq, mlen, heads):
    q, n = _mm(f"{tag}_q", x, w_q, prologue=_rms_rows, prologue_pars=[g_q])
    mem_n = _rms_fwd(f"{tag}_mem_norm", mem, g_mem)
    kv = _mm(f"{tag}_kv", mem_n, w_kv)
    o = _attn_fwd(f"{tag}_attn", q, kv, n_seq, seq, mlen, heads)
    out = _mm(f"{tag}_o", o, w_o, res=x, out_dtype=F32)
    return out, (x, n, q, mem_n, kv, o)


def _xattn_bwd(tag, dres, saved, mem, g_q, g_mem, w_q, w_kv, w_o, n_seq, seq, mlen, heads):
    x, n, q, mem_n, kv, o = saved
    dres32, dres16 = dres
    do = _mm(f"{tag}_o_dx", dres16, w_o, tb=True)
    d_o = _mm(f"{tag}_o_dw", o, dres16, ta=True)
    dq, dk, dv = _attn_bwd(f"{tag}_attn_bwd", q, kv, do, n_seq, seq, mlen, heads)
    dkv = jnp.concatenate([dk, dv], axis=1)
    d_q = _mm(f"{tag}_q_dw", n, dq, ta=True)
    d_kv = _mm(f"{tag}_kv_dw", mem_n, dkv, ta=True)
    dmem_n = _mm(f"{tag}_kv_dx", dkv, w_kv, tb=True)
    _, dg_mem = _rms_bwd(f"{tag}_mem_norm_bwd", mem, g_mem, dmem_n)
    dx, dg_q = _mm_rms_bwd(f"{tag}_q_dx", dq, w_q, x, g_q, dres32)
    return dx, dg_q, dg_mem, d_q, d_kv, d_o


WEIGHT_NAMES = ("norm_g", "final_g", "ffn1_up", "ffn1_down", "ffn2_up", "ffn2_down", "conv_w_in", "conv_w",
                "conv_w_out", "ssm_a_re", "ssm_a_im", "ssm_log_dt", "ssm_b_re", "ssm_b_im", "ssm_c_re", "ssm_c_im",
                "ssm_d", "ssm_w_glu", "xa_w_q", "xa_w_kv", "xa_w_o")
MATRICES = ("ffn1_up", "ffn1_down", "ffn2_up", "ffn2_down", "conv_w_in", "conv_w_out", "ssm_w_glu", "xa_w_q",
            "xa_w_kv", "xa_w_o")
COLUMN_SHARDED = ("conv_w_in", "ssm_w_glu", "xa_w_kv")
TRANSPOSED = ("ffn1_up", "ffn2_up")
SMALL_SHARDED = ("norm_g", "conv_w", "ssm_d")
REPLICATED = ("ssm_a_re", "ssm_a_im", "ssm_log_dt", "ssm_b_re", "ssm_b_im", "ssm_c_re", "ssm_c_im", "final_g")


def kernel(x, mem, norm_g, final_g, ffn1_up, ffn1_down, ffn2_up, ffn2_down, conv_w_in, conv_w, conv_w_out, ssm_a_re, ssm_a_im, ssm_log_dt, ssm_b_re, ssm_b_im, ssm_c_re, ssm_c_im, ssm_d, ssm_w_glu, xa_w_q, xa_w_kv, xa_w_o, loss_target, m_norm_g, m_final_g, m_ffn1_up, m_ffn1_down, m_ffn2_up, m_ffn2_down, m_conv_w_in, m_conv_w, m_conv_w_out, m_ssm_a_re, m_ssm_a_im, m_ssm_log_dt, m_ssm_b_re, m_ssm_b_im, m_ssm_c_re, m_ssm_c_im, m_ssm_d, m_ssm_w_glu, m_xa_w_q, m_xa_w_kv, m_xa_w_o, v_norm_g, v_final_g, v_ffn1_up, v_ffn1_down, v_ffn2_up, v_ffn2_down, v_conv_w_in, v_conv_w, v_conv_w_out, v_ssm_a_re, v_ssm_a_im, v_ssm_log_dt, v_ssm_b_re, v_ssm_b_im, v_ssm_c_re, v_ssm_c_im, v_ssm_d, v_ssm_w_glu, v_xa_w_q, v_xa_w_kv, v_xa_w_o):
    w = dict(norm_g=norm_g, final_g=final_g, ffn1_up=ffn1_up, ffn1_down=ffn1_down, ffn2_up=ffn2_up,
             ffn2_down=ffn2_down, conv_w_in=conv_w_in, conv_w=conv_w, conv_w_out=conv_w_out, ssm_a_re=ssm_a_re,
             ssm_a_im=ssm_a_im, ssm_log_dt=ssm_log_dt, ssm_b_re=ssm_b_re, ssm_b_im=ssm_b_im, ssm_c_re=ssm_c_re,
             ssm_c_im=ssm_c_im, ssm_d=ssm_d, ssm_w_glu=ssm_w_glu, xa_w_q=xa_w_q, xa_w_kv=xa_w_kv, xa_w_o=xa_w_o)
    mom = dict(norm_g=m_norm_g, final_g=m_final_g, ffn1_up=m_ffn1_up, ffn1_down=m_ffn1_down, ffn2_up=m_ffn2_up,
               ffn2_down=m_ffn2_down, conv_w_in=m_conv_w_in, conv_w=m_conv_w, conv_w_out=m_conv_w_out,
               ssm_a_re=m_ssm_a_re, ssm_a_im=m_ssm_a_im, ssm_log_dt=m_ssm_log_dt, ssm_b_re=m_ssm_b_re,
               ssm_b_im=m_ssm_b_im, ssm_c_re=m_ssm_c_re, ssm_c_im=m_ssm_c_im, ssm_d=m_ssm_d, ssm_w_glu=m_ssm_w_glu,
               xa_w_q=m_xa_w_q, xa_w_kv=m_xa_w_kv, xa_w_o=m_xa_w_o)
    var = dict(norm_g=v_norm_g, final_g=v_final_g, ffn1_up=v_ffn1_up, ffn1_down=v_ffn1_down, ffn2_up=v_ffn2_up,
               ffn2_down=v_ffn2_down, conv_w_in=v_conv_w_in, conv_w=v_conv_w, conv_w_out=v_conv_w_out,
               ssm_a_re=v_ssm_a_re, ssm_a_im=v_ssm_a_im, ssm_log_dt=v_ssm_log_dt, ssm_b_re=v_ssm_b_re,
               ssm_b_im=v_ssm_b_im, ssm_c_re=v_ssm_c_re, ssm_c_im=v_ssm_c_im, ssm_d=v_ssm_d, ssm_w_glu=v_ssm_w_glu,
               xa_w_q=v_xa_w_q, xa_w_kv=v_xa_w_kv, xa_w_o=v_xa_w_o)

    n_seq, seq, d = x.shape
    mlen = mem.shape[1]
    depth, n_norms = norm_g.shape[0], norm_g.shape[1]
    heads = 4
    tokens = n_seq * seq
    x2 = x.reshape(tokens, d)
    mem2 = mem.reshape(n_seq * mlen, d)
    tgt2 = loss_target.reshape(tokens, d)

    small_shapes = [w[k].shape for k in SMALL_SHARDED]
    small_rows = [_merge2d(w[k]) for k in SMALL_SHARDED]
    small_counts = [s.shape[0] for s in small_rows]
    small = jnp.concatenate(small_rows, axis=0)
    small = jnp.pad(small, [(0, (-small.shape[0]) % 8), (0, 0)])
    me = _index(*_place())

    def layer_weights(i):
        names = [(k, i) for k in ("ffn1_up", "ffn1_down", "ffn2_up", "ffn2_down", "xa_w_q", "xa_w_kv", "xa_w_o")]
        return names + ([("conv_w_in", i // 2), ("conv_w_out", i // 2)] if i % 2 == 0 else [("ssm_w_glu", i // 2)])

    shards = [[(w[k][idx].T if k in TRANSPOSED else w[k][idx]).astype(BF16) for k, idx in layer_weights(i)]
              for i in range(depth)]
    n_first = 2
    gathered = _all_gather("gather_layer0_ffn1", shards[0][:n_first] + [small])
    small_all = gathered[-1]
    in_flight = [None] * depth
    token = gathered[0]
    in_place = {k for k in COLUMN_SHARDED if w[k].shape[-1] % V7X_LANES == 0}

    def modes(i):
        start = n_first if i == 0 else 0
        return [COLUMNS if k in in_place else BLOCK for k, _ in layer_weights(i)][start:]

    for i in range(depth):
        mine = shards[i][n_first:] if i == 0 else shards[i]
        zones = [_landing(s, me, mode) for s, mode in zip(mine, modes(i))]
        *in_flight[i], token = _exchange_start(f"gather_start_l{i}", mine, zones, [WHOLE] * len(zones), modes(i),
                                               deps=[token], plan=CHIPS)
    passing = [None] * depth

    def pass_on(i, after):
        landed = _exchange_wait(f"gather_wait_l{i}", *in_flight[i], [WHOLE] * len(modes(i)), modes(i), after,
                                plan=CHIPS)
        *passing[i], forward_token = _exchange_start(f"forward_start_l{i}", [], landed, modes(i), modes(i),
                                                     plan=FORWARD)
        return forward_token

    def passed(i, after):
        return _exchange_wait(f"forward_wait_l{i}", *passing[i], modes(i), modes(i), after, plan=FORWARD)

    def small_whole(idx):
        start = sum(small_counts[:idx])
        part = small_all[:, start:start + small_counts[idx]]
        lead = small_shapes[idx][:-1]
        part = part.reshape((N_DEV,) + lead + (part.shape[-1],))
        part = jnp.moveaxis(part, 0, -2)
        return part.reshape(lead + (N_DEV * part.shape[-1],))

    norm_all = small_whole(0)
    conv_all = small_whole(1)
    dskip_all = small_whole(2)

    def whole(names, arrived):
        return {k: blk if k in in_place else _cols_whole(blk) if k in COLUMN_SHARDED else _rows_whole(blk)
                for (k, _), blk in zip(names, arrived)}

    saved = []
    cur = x2
    for i in range(depth):
        g = [norm_all[i, k].reshape(1, d) for k in range(n_norms)]
        j = i // 2
        if i == 0:
            lw = whole(layer_weights(0)[:n_first], gathered[:n_first])
        else:
            lw = whole(layer_weights(i), passed(i, cur))
        cur, s_ffn1 = _ffn_fwd(f"l{i}_ffn1", cur, g[0], lw["ffn1_up"], lw["ffn1_down"], [token] if i == 0 else ())
        if i == 0:
            lw.update(whole(layer_weights(0)[n_first:], passed(0, pass_on(0, cur))))
        if i % 2 == 0:
            lw["conv_w"] = conv_all[j]
            cur, s_mix = _conv_mixer_fwd(f"l{i}_conv", cur, g[1], lw["conv_w_in"], lw["conv_w"], lw["conv_w_out"],
                                         n_seq, seq)
        else:
            ssm = tuple(w[k][j] for k in ("ssm_a_re", "ssm_a_im", "ssm_log_dt", "ssm_b_re", "ssm_b_im",
                                          "ssm_c_re", "ssm_c_im"))
            cur, s_mix = _s5_mixer_fwd(f"l{i}_s5", cur, g[1], ssm, dskip_all[j].reshape(1, d), lw["ssm_w_glu"],
                                       n_seq, seq)
        cur, s_xa = _xattn_fwd(f"l{i}_xa", cur, mem2, g[2], g[3], lw["xa_w_q"], lw["xa_w_kv"], lw["xa_w_o"],
                               n_seq, seq, mlen, heads)
        deps = [pass_on(i + 1, cur)] if i + 1 < depth else ()
        cur, s_ffn2 = _ffn_fwd(f"l{i}_ffn2", cur, g[4], lw["ffn2_up"], lw["ffn2_down"], deps)
        saved.append((g, lw, s_ffn1, s_mix, s_xa, s_ffn2))

    dres, err2, d_final = _final_loss("loss_head", cur, final_g.reshape(1, d), tgt2)
    loss = lax.psum(0.5 * jnp.sum(err2) / d, ("x", "y", "c"))

    d_norm = [[None] * n_norms for _ in range(depth)]
    d_conv = [None] * conv_w.shape[0]
    d_skip = [None] * ssm_d.shape[0]
    d_ssm = [None] * ssm_a_re.shape[0]
    leaving = [None] * depth
    deps = ()

    def leave(name, keys, gm, extra=()):
        srcs, src_modes, zones = [], [], []
        for k in keys:
            if k in in_place:
                rows, n = gm[k].shape[0], gm[k].shape[1] // N_DEV
                srcs.append(gm[k])
                src_modes.append(COLUMNS)
                zones.append(_landing(lax.dynamic_slice(gm[k], (0, me * n), (rows, n)), me))
            else:
                srcs.append(_cols_parts(gm[k]) if k in COLUMN_SHARDED else _rows_parts(gm[k]))
                src_modes.append(BLOCK)
                zones.append(_landing(lax.dynamic_index_in_dim(srcs[-1], me, 0, keepdims=False), me))
        for p in extra:
            srcs.append(p)
            src_modes.append(BLOCK)
            zones.append(_landing(lax.dynamic_index_in_dim(p, me, 0, keepdims=False), me))
        land_modes = [BLOCK] * len(srcs)
        *handles, token = _exchange_start(name, srcs, zones, src_modes, land_modes)
        return (*handles, src_modes, land_modes), token

    def small_parts(full):
        lead = full.shape[:-1]
        t = full.reshape(lead + (N_DEV, full.shape[-1] // N_DEV))
        t = jnp.moveaxis(t, -2, 0)
        return t.reshape(N_DEV, -1, t.shape[-1])

    for i in reversed(range(depth)):
        g, lw, s_ffn1, s_mix, s_xa, s_ffn2 = saved[i]
        j = i // 2
        gm = {}
        dres, d_norm[i][4], gm["ffn2_up"], gm["ffn2_down"] = _ffn_bwd(
            f"l{i}_ffn2", dres, s_ffn2, g[4], lw["ffn2_up"], lw["ffn2_down"], deps)
        dres, d_norm[i][2], d_norm[i][3], gm["xa_w_q"], gm["xa_w_kv"], gm["xa_w_o"] = _xattn_bwd(
            f"l{i}_xa", dres, s_xa, mem2, g[2], g[3], lw["xa_w_q"], lw["xa_w_kv"], lw["xa_w_o"], n_seq, seq, mlen,
            heads)
        if i % 2 == 0:
            dres, d_norm[i][1], gm["conv_w_in"], d_conv[j], gm["conv_w_out"] = _conv_mixer_bwd(
                f"l{i}_conv", dres, s_mix, g[1], lw["conv_w_in"], lw["conv_w"], lw["conv_w_out"], n_seq, seq)
        else:
            dres, d_norm[i][1], d_ssm[j], d_skip[j], gm["ssm_w_glu"] = _s5_mixer_bwd(
                f"l{i}_s5", dres, s_mix, g[1], lw["ssm_w_glu"], n_seq, seq)
        upper, token = leave(f"grads_start_l{i}_upper", [k for k, _ in layer_weights(i)[2:]], gm)
        dres, d_norm[i][0], gm["ffn1_up"], gm["ffn1_down"] = _ffn_bwd(
            f"l{i}_ffn1", dres, s_ffn1, g[0], lw["ffn1_up"], lw["ffn1_down"], [token])
        extra = []
        if i == 0:
            d_norm_all = jnp.stack([jnp.concatenate(row, axis=0) for row in d_norm])
            small_g = jnp.concatenate(
                [small_parts(a) for a in (d_norm_all, jnp.stack(d_conv), jnp.concatenate(d_skip, axis=0))], axis=1)
            extra = [jnp.pad(small_g, [(0, 0), (0, (-small_g.shape[1]) % 8), (0, 0)])]
        lower, token = leave(f"grads_start_l{i}_lower", ["ffn1_up", "ffn1_down"], gm, extra)
        leaving[i] = (upper, lower)
        deps = [token]
        if i == min(1, depth - 1):
            rep_grads = [jnp.stack([d_ssm[j][k] for j in range(len(d_ssm))]) for k in range(7)]
            rep_packed = _pack_rows(rep_grads + [d_final.reshape(-1)])
            *rep_leaving, token = _exchange_start("replicated_grads_start", [rep_packed], [_landing(rep_packed, me)],
                                                  [WHOLE], [BLOCK])
            deps = deps + [token]
    grad_x = dres[0].reshape(n_seq, seq, d)
    received = {k: [None] * w[k].shape[0] for k in MATRICES}

    def arrive(i, after):
        upper = _exchange_wait(f"grads_wait_l{i}_upper", *leaving[i][0], after)
        lower = _exchange_wait(f"grads_wait_l{i}_lower", *leaving[i][1], after)
        for (k, idx), blk in zip(layer_weights(i), lower[:2] + upper):
            received[k][idx] = blk
        return lower[2:]

    for i in range(1, depth):
        arrive(i, dres[0])
    rep_all = _exchange_wait("replicated_grads_wait", *rep_leaving, [WHOLE], [BLOCK], dres[0])[0]
    rep_shapes = [w[k].shape for k in REPLICATED]
    flat = {k: (_merge2d(w[k]), _merge2d(mom[k]), _merge2d(var[k])) for k in MATRICES}
    late = {k: received[k][0] is None for k in MATRICES}
    early = {}
    for k in MATRICES:
        first = 1 if late[k] else 0
        if first < len(received[k]):
            early[k] = _adamw(f"adamw_{k}_upper", received[k][first:], *flat[k], first_layer=first,
                              transposed=k in TRANSPOSED)
    small_received, = arrive(0, list(early.values())[-1][0] if early else dres[0])
    grads, deltas, new_m, new_v = {}, {}, {}, {}
    for k in MATRICES:
        out = early.get(k)
        if late[k]:
            out = _adamw(f"adamw_{k}_l0", received[k][:1], *flat[k], earlier=out, transposed=k in TRANSPOSED)
        grads[k], deltas[k], new_m[k], new_v[k] = [o.reshape(w[k].shape) for o in out]

    def small_local(src):
        rows = jnp.concatenate([_merge2d(src[k]) for k in SMALL_SHARDED], axis=0)
        return jnp.pad(rows, [(0, (-rows.shape[0]) % 8), (0, 0)])

    out = _adamw("adamw_small", [small_received], small, small_local(mom), small_local(var))
    for res, o in zip((grads, deltas, new_m, new_v), out):
        start = 0
        for k, cnt, shape in zip(SMALL_SHARDED, small_counts, small_shapes):
            res[k] = o[start:start + cnt].reshape(shape)
            start += cnt

    out = _adamw("adamw_replicated", [rep_all], _pack_rows([w[k] for k in REPLICATED]),
                 _pack_rows([mom[k] for k in REPLICATED]), _pack_rows([var[k] for k in REPLICATED]))
    for res, o in zip((grads, deltas, new_m, new_v), out):
        for k, a in zip(REPLICATED, _unpack_rows(o, rep_shapes)):
            res[k] = a

    return (loss, grad_x, *[grads[k] for k in WEIGHT_NAMES], *[deltas[k] for k in WEIGHT_NAMES],
            *[new_m[k] for k in WEIGHT_NAMES], *[new_v[k] for k in WEIGHT_NAMES])
```

```python
import math

import jax
import jax.numpy as jnp
from jax import lax
from jax.experimental import pallas as pl
from jax.experimental.pallas import tpu as pltpu

F32 = jnp.float32
BF16 = jnp.bfloat16
MESH = pl.DeviceIdType.MESH
N_DEV = 8

NORM_EPS = 1e-6
EIG_CLIP = -1e-4
CONV_WIDTH = 3
ADAM_LR = 0.001
ADAM_B1 = 0.9
ADAM_B2 = 0.999
ADAM_EPS = 1e-08
ADAM_WD = 0.01
ADAM_STEP = 10
GELU_C = math.sqrt(2.0 / math.pi)
GELU_A = 0.044715

V7X_LANES = 128
V7X_VMEM_LIMIT = 56 * 1024 * 1024
S5_CHANNELS = 128
PACK_TILE = 8 * V7X_LANES

HBM_SPEC = pl.BlockSpec(memory_space=pltpu.HBM)
ANY_SPEC = pl.BlockSpec(memory_space=pl.ANY)
SEM_SPEC = pl.BlockSpec(memory_space=pltpu.SEMAPHORE)


def _params(n_grid):
    return pltpu.CompilerParams(dimension_semantics=("arbitrary",) * n_grid, vmem_limit_bytes=V7X_VMEM_LIMIT)


def _pick(n, pref, align):
    if n <= pref:
        return n
    t = (pref // align) * align
    while t >= align:
        if n % t == 0:
            return t
        t -= align
    raise ValueError(f"no tile for {n} (pref {pref}, align {align})")


MM_RHS_BLOCK_BYTES = 12 * 1024 * 1024
MM_LHS_BLOCK_BYTES = 6 * 1024 * 1024
MM_ACC_BYTES = 6 * 1024 * 1024
MM_ROWS = 512


MM_EPILOGUE_ROWS = 256


def _mm_tiles(m, k, n, a_item, b_item, ta, max_rows):
    tn = _pick(n, max(V7X_LANES, MM_RHS_BLOCK_BYTES // (k * b_item)), V7X_LANES)
    rows = min(max_rows, MM_ACC_BYTES // (4 * tn), MM_LHS_BLOCK_BYTES // (k * a_item))
    align = V7X_LANES if ta else 16
    tm = _pick(m, max(align, rows), align)
    return tm, tn


def _store_results(out_refs, n_row, results, first):
    if not isinstance(results, (tuple, list)):
        results = (results,)
    for o, v in zip(out_refs[:n_row], results[:n_row]):
        if isinstance(v, (tuple, list)):
            off = 0
            for piece in v:
                w = piece.shape[1]
                o[:, off:off + w] = piece.astype(o.dtype)
                off += w
        else:
            o[...] = v.astype(o.dtype)
    if len(out_refs) > n_row:
        @pl.when(first)
        def _():
            for o in out_refs[n_row:]:
                o[...] = jnp.zeros_like(o)

        for o, v in zip(out_refs[n_row:], results[n_row:]):
            o[...] += v


def _mm(name, a, b, *, ta=False, tb=False, out_dtype=BF16, res=None, scale=None, deps=(),
        epilogue=None, row_ins=(), par_ins=(), outs=(), acc_outs=(), prologue=None, prologue_pars=()):
    if ta:
        k, m = a.shape
    else:
        m, k = a.shape
    if tb:
        n, k2 = b.shape
    else:
        k2, n = b.shape
    assert k == k2, (name, a.shape, b.shape)
    max_rows = MM_ROWS if epilogue is None else MM_EPILOGUE_ROWS
    tm, tn = _mm_tiles(m, k, n, a.dtype.itemsize, b.dtype.itemsize, ta, max_rows)
    a_spec = pl.BlockSpec((k, tm), lambda j, i: (0, i)) if ta else pl.BlockSpec((tm, k), lambda j, i: (i, 0))
    b_spec = pl.BlockSpec((tn, k), lambda j, i: (j, 0)) if tb else pl.BlockSpec((k, tn), lambda j, i: (0, j))
    o_spec = pl.BlockSpec((tm, tn), lambda j, i: (i, j))
    dims = (((0 if ta else 1,), (1 if tb else 0,)), ((), ()))
    has_res = res is not None
    ins = [a, b] + ([res] if has_res else [])
    specs = [a_spec, b_spec] + ([o_spec] if has_res else [])
    n_mm = len(ins)
    if epilogue is None:
        out_specs, out_shape = [o_spec], [jax.ShapeDtypeStruct((m, n), out_dtype)]
    else:
        assert tn == n, (name, tn, n)
        for r in row_ins:
            arr, cb, cw = r if isinstance(r, tuple) else (r, 0, r.shape[1])
            assert arr.shape[0] == m, (name, arr.shape, m)
            ins.append(arr)
            specs.append(pl.BlockSpec((tm, cw), lambda j, i, cb=cb: (i, cb)))
        for p in par_ins:
            ins.append(p)
            specs.append(pl.BlockSpec(p.shape, lambda j, i: (0, 0)))
        out_specs = [pl.BlockSpec((tm, c), lambda j, i: (i, 0)) for c, _ in outs]
        out_specs += [pl.BlockSpec((r, c), lambda j, i: (0, 0)) for r, c in acc_outs]
        out_shape = [jax.ShapeDtypeStruct((m, c), dt) for c, dt in outs]
        out_shape += [jax.ShapeDtypeStruct((r, c), F32) for r, c in acc_outs]
    n_in = len(ins)
    if prologue is not None:
        assert tn == n and not ta, (name, tn, n, ta)
        for p in prologue_pars:
            ins.append(p)
            specs.append(pl.BlockSpec(p.shape, lambda j, i: (0, 0)))
        out_specs = out_specs + [a_spec]
        out_shape = out_shape + [jax.ShapeDtypeStruct((m, k), BF16)]
    n_pro = len(ins)
    ins += list(deps)
    specs += [ANY_SPEC] * len(deps)

    def body(*refs):
        a_ref, b_ref = refs[0], refs[1]
        out_refs = refs[n_pro + len(deps):]
        if prologue is None:
            lhs = a_ref[...].astype(BF16)
        else:
            lhs = prologue(a_ref[...], *[r[...] for r in refs[n_in:n_pro]]).astype(BF16)
            out_refs[-1][...] = lhs
            out_refs = out_refs[:-1]
        acc = lax.dot_general(lhs, b_ref[...].astype(BF16), dims, preferred_element_type=F32)
        if scale is not None:
            acc = acc * scale
        if has_res:
            acc = acc + refs[2][...].astype(F32)
        if epilogue is None:
            out_refs[0][...] = acc.astype(out_refs[0].dtype)
        else:
            extra = [r[...] for r in refs[n_mm:n_in]]
            _store_results(out_refs, len(outs), epilogue(acc, *extra), pl.program_id(1) == 0)

    out = pl.pallas_call(
        body, name=name, grid=(n // tn, m // tm), in_specs=specs, out_specs=out_specs, out_shape=out_shape,
        compiler_params=_params(2),
    )(*ins)
    return out[0] if epilogue is None and prologue is None else out


def _rowwise(name, fn, rows, row_ins, par_ins, row_outs, acc_outs=(), tm_pref=256):
    tm = _pick(rows, tm_pref, 16)
    in_specs, ins = [], []
    for r in row_ins:
        arr, cb, cw = r if isinstance(r, tuple) else (r, 0, r.shape[1])
        assert arr.shape[0] == rows, (name, arr.shape, rows)
        ins.append(arr)
        in_specs.append(pl.BlockSpec((tm, cw), lambda i, cb=cb: (i, cb)))
    for p in par_ins:
        ins.append(p)
        in_specs.append(pl.BlockSpec(p.shape, lambda i: (0, 0)))
    out_specs = [pl.BlockSpec((tm, c), lambda i: (i, 0)) for c, _ in row_outs]
    out_specs += [pl.BlockSpec((r, c), lambda i: (0, 0)) for r, c in acc_outs]
    out_shape = [jax.ShapeDtypeStruct((rows, c), dt) for c, dt in row_outs]
    out_shape += [jax.ShapeDtypeStruct((r, c), F32) for r, c in acc_outs]
    n_in, n_row = len(ins), len(row_outs)

    def body(*refs):
        vals = [r[...] for r in refs[:n_in]]
        _store_results(refs[n_in:], n_row, fn(*vals), pl.program_id(0) == 0)

    return pl.pallas_call(
        body, name=name, grid=(rows // tm,), in_specs=in_specs, out_specs=out_specs, out_shape=out_shape,
        compiler_params=_params(1),
    )(*ins)


def _inv_rms(x):
    return lax.rsqrt(jnp.mean(x * x, axis=-1, keepdims=True) + NORM_EPS)


def _rms_rows(x, g):
    return x * _inv_rms(x) * g


def _rms_fwd(name, x, g):
    return _rowwise(name, _rms_rows, x.shape[0], [x], [g], [(x.shape[1], BF16)], tm_pref=512)[0]


def _rms_bwd_rows(dn, x, dres, g):
    r = _inv_rms(x)
    xh = x * r
    dg = jnp.sum(dn * xh, axis=0, keepdims=True)
    dxh = dn * g
    dx = r * (dxh - xh * jnp.mean(dxh * xh, axis=-1, keepdims=True)) + dres
    return dx, dx, dg


def _rms_bwd(name, x, g, dn, dres=None):
    d = x.shape[1]
    if dres is None:
        def fn(x, dn, g):
            return (jnp.sum(dn.astype(F32) * (x * _inv_rms(x)), axis=0, keepdims=True),)

        return None, _rowwise(name, fn, x.shape[0], [x, dn], [g], [], [(1, d)])[0]

    def fn(x, dn, dres, g):
        return _rms_bwd_rows(dn.astype(F32), x, dres, g)

    out = _rowwise(name, fn, x.shape[0], [x, dn, dres], [g], [(d, F32), (d, BF16)], [(1, d)])
    return (out[0], out[1]), out[2]


def _mm_rms_bwd(name, dy, w, x, g, dres, tb=True, deps=()):
    d = x.shape[1]
    out = _mm(name, dy, w, tb=tb, deps=deps, epilogue=_rms_bwd_rows, row_ins=[x, dres], par_ins=[g],
              outs=[(d, F32), (d, BF16)], acc_outs=[(1, d)])
    return (out[0], out[1]), out[2]


def _sigmoid(x):
    return 0.5 + 0.5 * jnp.tanh(0.5 * x)


def _swiglu_rows(gu, f):
    gt = gu[:, :f]
    return gu, gt * _sigmoid(gt) * gu[:, f:]


def _swiglu_bwd_rows(dact, gt, up):
    gt, up = gt.astype(F32), up.astype(F32)
    sg = _sigmoid(gt)
    return ((dact * up * (sg * (1.0 + gt * (1.0 - sg))), dact * (gt * sg)),)


def _glu_bwd(name, dres, vg, d):
    def fn(dres, val, gate):
        val, gate = val.astype(F32), gate.astype(F32)
        sg = _sigmoid(gate)
        return ((dres * sg, dres * val * sg * (1.0 - sg)),)

    return _rowwise(name, fn, dres.shape[0], [dres, (vg, 0, d), (vg, 1, d)], [], [(2 * d, BF16)])[0]


def _final_loss(name, x, g, tgt):
    d = x.shape[1]

    def fn(x, tgt, g):
        r = _inv_rms(x)
        xh = x * r
        err = xh * g - tgt
        dy = err * (1.0 / d)
        dxh = dy * g
        dx = r * (dxh - xh * jnp.mean(dxh * xh, axis=-1, keepdims=True))
        return dx, dx, jnp.sum(err * err, axis=0, keepdims=True), jnp.sum(dy * xh, axis=0, keepdims=True)

    dx, dx16, err2, dg = _rowwise(name, fn, x.shape[0], [x, tgt], [g], [(d, F32), (d, BF16)], [(1, d), (1, d)])
    return (dx, dx16), err2, dg


def _shift_down(u, k):
    rows = lax.broadcasted_iota(jnp.int32, u.shape, 0)
    return jnp.where(rows >= k, pltpu.roll(u, k, 0), 0.0)


def _shift_up(u, k):
    n = u.shape[0]
    rows = lax.broadcasted_iota(jnp.int32, u.shape, 0)
    return jnp.where(rows < n - k, pltpu.roll(u, n - k, 0), 0.0)


def _conv_specs(seq, cw, n_cb, swap):
    def at(off):
        if swap:
            return pl.BlockSpec((seq, cw), lambda j, b: (b, off * n_cb + j))
        return pl.BlockSpec((seq, cw), lambda b, j: (b, off * n_cb + j))

    return at


def _conv_fwd(name, cbv, w, n_seq, seq):
    d = w.shape[1]
    cw = _pick(d, 256, V7X_LANES)
    n_cb = d // cw
    at = _conv_specs(seq, cw, n_cb, swap=False)

    def body(c_ref, b_ref, v_ref, w_ref, z_ref):
        u = c_ref[...].astype(F32) * v_ref[...].astype(F32)
        cv = w_ref[0:1, :] * _shift_down(u, 2) + w_ref[1:2, :] * _shift_down(u, 1) + w_ref[2:3, :] * u
        z_ref[...] = (b_ref[...].astype(F32) * cv).astype(z_ref.dtype)

    return pl.pallas_call(
        body, name=name, grid=(n_seq, n_cb),
        in_specs=[at(0), at(1), at(2), pl.BlockSpec((CONV_WIDTH, cw), lambda b, j: (0, j))],
        out_specs=at(0), out_shape=jax.ShapeDtypeStruct((n_seq * seq, d), BF16), compiler_params=_params(2),
    )(cbv, cbv, cbv, w)


def _conv_bwd(name, dz, cbv, w, n_seq, seq):
    d = w.shape[1]
    cw = _pick(d, 256, V7X_LANES)
    n_cb = d // cw
    at = _conv_specs(seq, cw, n_cb, swap=True)

    def body(dz_ref, c_ref, b_ref, v_ref, w_ref, dc_ref, db_ref, dv_ref, dw_ref):
        c, b, v = c_ref[...].astype(F32), b_ref[...].astype(F32), v_ref[...].astype(F32)
        dz = dz_ref[...].astype(F32)
        w0, w1, w2 = w_ref[0:1, :], w_ref[1:2, :], w_ref[2:3, :]
        u = c * v
        u1, u2 = _shift_down(u, 1), _shift_down(u, 2)
        cv = w0 * u2 + w1 * u1 + w2 * u
        db_ref[...] = (dz * cv).astype(db_ref.dtype)
        dcv = dz * b
        du = w2 * dcv + w1 * _shift_up(dcv, 1) + w0 * _shift_up(dcv, 2)
        dc_ref[...] = (du * v).astype(dc_ref.dtype)
        dv_ref[...] = (du * c).astype(dv_ref.dtype)

        @pl.when(pl.program_id(1) == 0)
        def _():
            dw_ref[...] = jnp.zeros_like(dw_ref)

        dw_ref[0:1, :] += jnp.sum(dcv * u2, axis=0, keepdims=True)
        dw_ref[1:2, :] += jnp.sum(dcv * u1, axis=0, keepdims=True)
        dw_ref[2:3, :] += jnp.sum(dcv * u, axis=0, keepdims=True)

    act = jax.ShapeDtypeStruct((n_seq * seq, d), BF16)
    return pl.pallas_call(
        body, name=name, grid=(n_cb, n_seq),
        in_specs=[at(0), at(0), at(1), at(2), pl.BlockSpec((CONV_WIDTH, cw), lambda j, b: (0, j))],
        out_specs=[at(0), at(0), at(0), pl.BlockSpec((CONV_WIDTH, cw), lambda j, b: (0, j))],
        out_shape=[act, act, act, jax.ShapeDtypeStruct((CONV_WIDTH, d), F32)], compiler_params=_params(2),
    )(dz, cbv, cbv, cbv, w)


def _s5_discretize(a_re, a_im, log_dt, b_re, b_im):
    lam_re = jnp.minimum(a_re, EIG_CLIP)
    lam_im = a_im
    dt = jnp.exp(log_dt)[:, None]
    mag = jnp.exp(lam_re * dt)
    abar_re = mag * jnp.cos(lam_im * dt)
    abar_im = mag * jnp.sin(lam_im * dt)
    den = lam_re * lam_re + lam_im * lam_im
    num_re = abar_re - 1.0
    num_im = abar_im
    coef_re = (num_re * lam_re + num_im * lam_im) / den
    coef_im = (num_im * lam_re - num_re * lam_im) / den
    bbar_re = coef_re[..., None] * b_re - coef_im[..., None] * b_im
    bbar_im = coef_re[..., None] * b_im + coef_im[..., None] * b_re
    return abar_re, abar_im, bbar_re, bbar_im


def _block_diag_in(bbar, gb):
    g, p, h = bbar.shape
    t = jnp.transpose(bbar.reshape(g // gb, gb, p, h), (0, 1, 3, 2))
    return jnp.einsum("cghp,gk->cghkp", t, jnp.eye(gb, dtype=bbar.dtype)).reshape(g // gb, gb * h, gb * p)


def _block_diag_in_t(blk, gb, p, h):
    nb = blk.shape[0]
    t = jnp.einsum("cghkp,gk->cghp", blk.reshape(nb, gb, h, gb, p), jnp.eye(gb, dtype=blk.dtype))
    return jnp.transpose(t, (0, 1, 3, 2)).reshape(nb * gb, p, h)


def _block_diag_out(c, gb):
    g, h, p = c.shape
    t = jnp.transpose(c.reshape(g // gb, gb, h, p), (0, 1, 3, 2))
    return jnp.einsum("cgph,gk->cgpkh", t, jnp.eye(gb, dtype=c.dtype)).reshape(g // gb, gb * p, gb * h)


def _block_diag_out_t(blk, gb, p, h):
    nb = blk.shape[0]
    t = jnp.einsum("cgpkh,gk->cgph", blk.reshape(nb, gb, p, gb, h), jnp.eye(gb, dtype=blk.dtype))
    return jnp.transpose(t, (0, 1, 3, 2)).reshape(nb * gb, h, p)


def _gelu(y):
    return 0.5 * y * (1.0 + jnp.tanh(GELU_C * (y + GELU_A * y * y * y)))


def _gelu_grad(y):
    th = jnp.tanh(GELU_C * (y + GELU_A * y * y * y))
    return 0.5 * (1.0 + th) + 0.5 * y * (1.0 - th * th) * GELU_C * (1.0 + 3.0 * GELU_A * y * y)


def _dot(a, b, ca, cb):
    return lax.dot_general(a.astype(BF16), b.astype(BF16), (((ca,), (cb,)), ((), ())), preferred_element_type=F32)


def _s5_specs(seq, ch, sb):
    act = pl.BlockSpec((seq, ch), lambda j, b: (b, j))
    state = pl.BlockSpec((seq, sb), lambda j, b: (b, j))
    w_in = pl.BlockSpec((None, ch, sb), lambda j, b: (j, 0, 0))
    w_out = pl.BlockSpec((None, sb, ch), lambda j, b: (j, 0, 0))
    lane_s = pl.BlockSpec((1, sb), lambda j, b: (0, j))
    lane_c = pl.BlockSpec((1, ch), lambda j, b: (0, j))
    return act, state, w_in, w_out, lane_s, lane_c


def _s5_fwd(name, h, bin_re, bin_im, cout_re, cout_im, abar_re, abar_im, dskip, n_seq, seq):
    t, d = h.shape
    nb, ch, sb = bin_re.shape
    act, state, w_in, w_out, lane_s, lane_c = _s5_specs(seq, ch, sb)

    def body(h_ref, bre_ref, bim_ref, cre_ref, cim_ref, ar_ref, ai_ref, d_ref, sre_ref, sim_ref, y_ref, z_ref):
        u = h_ref[...]
        sre_ref[...] = _dot(u, bre_ref[...], 1, 0)
        sim_ref[...] = _dot(u, bim_ref[...], 1, 0)
        ar, ai = ar_ref[...], ai_ref[...]

        def step(i, carry):
            sr, si = carry
            row = pl.ds(i, 1)
            nr = ar * sr - ai * si + sre_ref[row, :]
            ni = ar * si + ai * sr + sim_ref[row, :]
            sre_ref[row, :] = nr
            sim_ref[row, :] = ni
            return nr, ni

        zero = jnp.zeros((1, sb), F32)
        lax.fori_loop(0, seq, step, (zero, zero), unroll=8)
        y = _dot(sre_ref[...], cre_ref[...], 1, 0) - _dot(sim_ref[...], cim_ref[...], 1, 0)
        y = y + d_ref[...] * u.astype(F32)
        y_ref[...] = y
        z_ref[...] = _gelu(y).astype(z_ref.dtype)

    return pl.pallas_call(
        body, name=name, grid=(nb, n_seq),
        in_specs=[act, w_in, w_in, w_out, w_out, lane_s, lane_s, lane_c],
        out_specs=[state, state, act, act],
        out_shape=[jax.ShapeDtypeStruct((t, nb * sb), F32), jax.ShapeDtypeStruct((t, nb * sb), F32),
                   jax.ShapeDtypeStruct((t, d), F32), jax.ShapeDtypeStruct((t, d), BF16)],
        compiler_params=_params(2),
    )(h, bin_re, bin_im, cout_re, cout_im, abar_re, abar_im, dskip)


def _s5_bwd(name, dz, ypre, h, s_re, s_im, bin_re, bin_im, cout_re, cout_im, abar_re, abar_im, dskip, n_seq, seq):
    t, d = h.shape
    nb, ch, sb = bin_re.shape
    act, state, w_in, w_out, lane_s, lane_c = _s5_specs(seq, ch, sb)

    def body(dz_ref, y_ref, h_ref, sre_ref, sim_ref, bre_ref, bim_ref, cre_ref, cim_ref, ar_ref, ai_ref, d_ref,
             dh_ref, dbre_ref, dbim_ref, dcre_ref, dcim_ref, dar_ref, dai_ref, dd_ref, gre, gim):
        first = pl.program_id(1) == 0
        u = h_ref[...].astype(F32)
        dy = dz_ref[...].astype(F32) * _gelu_grad(y_ref[...])
        gre[...] = _dot(dy, cre_ref[...], 1, 1)
        gim[...] = -_dot(dy, cim_ref[...], 1, 1)
        ar, ai = ar_ref[...], ai_ref[...]

        def step(i, carry):
            gr, gi = carry
            row = pl.ds(seq - 1 - i, 1)
            nr = gre[row, :] + ar * gr + ai * gi
            ni = gim[row, :] - ai * gr + ar * gi
            gre[row, :] = nr
            gim[row, :] = ni
            return nr, ni

        zero = jnp.zeros((1, sb), F32)
        lax.fori_loop(0, seq, step, (zero, zero), unroll=8)

        g_re, g_im = gre[...], gim[...]
        s_re, s_im = sre_ref[...], sim_ref[...]
        p_re, p_im = _shift_down(s_re, 1), _shift_down(s_im, 1)
        dar = jnp.sum(g_re * p_re + g_im * p_im, axis=0, keepdims=True)
        dai = jnp.sum(g_im * p_re - g_re * p_im, axis=0, keepdims=True)
        dbre = _dot(u, g_re, 0, 0)
        dbim = _dot(u, g_im, 0, 0)
        dcre = _dot(s_re, dy, 0, 0)
        dcim = -_dot(s_im, dy, 0, 0)
        ddd = jnp.sum(dy * u, axis=0, keepdims=True)
        dh_ref[...] = _dot(g_re, bre_ref[...], 1, 1) + _dot(g_im, bim_ref[...], 1, 1) + d_ref[...] * dy

        @pl.when(first)
        def _():
            dar_ref[...] = dar
            dai_ref[...] = dai
            dbre_ref[...] = dbre
            dbim_ref[...] = dbim
            dcre_ref[...] = dcre
            dcim_ref[...] = dcim
            dd_ref[...] = ddd

        @pl.when(jnp.logical_not(first))
        def _():
            dar_ref[...] += dar
            dai_ref[...] += dai
            dbre_ref[...] += dbre
            dbim_ref[...] += dbim
            dcre_ref[...] += dcre
            dcim_ref[...] += dcim
            dd_ref[...] += ddd

    return pl.pallas_call(
        body, name=name, grid=(nb, n_seq),
        in_specs=[act, act, act, state, state, w_in, w_in, w_out, w_out, lane_s, lane_s, lane_c],
        out_specs=[act, w_in, w_in, w_out, w_out, lane_s, lane_s, lane_c],
        out_shape=[jax.ShapeDtypeStruct((t, d), F32),
                   jax.ShapeDtypeStruct((nb, ch, sb), F32), jax.ShapeDtypeStruct((nb, ch, sb), F32),
                   jax.ShapeDtypeStruct((nb, sb, ch), F32), jax.ShapeDtypeStruct((nb, sb, ch), F32),
                   jax.ShapeDtypeStruct((1, nb * sb), F32), jax.ShapeDtypeStruct((1, nb * sb), F32),
                   jax.ShapeDtypeStruct((1, d), F32)],
        scratch_shapes=[pltpu.VMEM((seq, sb), F32), pltpu.VMEM((seq, sb), F32)],
        compiler_params=_params(2),
    )(dz, ypre, h, s_re, s_im, bin_re, bin_im, cout_re, cout_im, abar_re, abar_im, dskip)


ATTN_QUERY_ROWS = 1024


def _softmax_rows(q, k, scale):
    s = _dot(q, k, 1, 1) * scale
    e = jnp.exp(s - jnp.max(s, axis=-1, keepdims=True))
    return e * (1.0 / jnp.sum(e, axis=-1, keepdims=True))


def _attn_fwd(name, q, kv, n_seq, seq, mlen, heads):
    t, d = q.shape
    hd = d // heads
    tq = _pick(seq, ATTN_QUERY_ROWS, 16)
    nq = seq // tq
    scale = hd ** -0.5
    q_spec = pl.BlockSpec((tq, d), lambda b, i: (b * nq + i, 0))

    def body(q_ref, k_ref, v_ref, o_ref):
        for h in range(heads):
            cols = slice(h * hd, (h + 1) * hd)
            p = _softmax_rows(q_ref[:, cols], k_ref[:, cols], scale)
            o_ref[:, cols] = _dot(p, v_ref[:, cols], 1, 0).astype(o_ref.dtype)

    return pl.pallas_call(
        body, name=name, grid=(n_seq, nq),
        in_specs=[q_spec, pl.BlockSpec((mlen, d), lambda b, i: (b, 0)), pl.BlockSpec((mlen, d), lambda b, i: (b, 1))],
        out_specs=q_spec, out_shape=jax.ShapeDtypeStruct((t, d), BF16), compiler_params=_params(2),
    )(q, kv, kv)


def _attn_bwd(name, q, kv, do, n_seq, seq, mlen, heads):
    t, d = q.shape
    hd = d // heads
    tq = _pick(seq, ATTN_QUERY_ROWS, 16)
    nq = seq // tq
    scale = hd ** -0.5
    q_spec = pl.BlockSpec((tq, d), lambda b, i: (b * nq + i, 0))
    k_spec = pl.BlockSpec((mlen, d), lambda b, i: (b, 0))

    def body(q_ref, k_ref, v_ref, do_ref, dq_ref, dk_ref, dv_ref):
        @pl.when(pl.program_id(1) == 0)
        def _():
            dk_ref[...] = jnp.zeros_like(dk_ref)
            dv_ref[...] = jnp.zeros_like(dv_ref)

        for h in range(heads):
            cols = slice(h * hd, (h + 1) * hd)
            q, k, v, do = q_ref[:, cols], k_ref[:, cols], v_ref[:, cols], do_ref[:, cols]
            p = _softmax_rows(q, k, scale)
            dp = _dot(do, v, 1, 1)
            ds = p * (dp - jnp.sum(dp * p, axis=-1, keepdims=True)) * scale
            dq_ref[:, cols] = _dot(ds, k, 1, 0).astype(dq_ref.dtype)
            dk_ref[:, cols] += _dot(ds, q, 0, 0)
            dv_ref[:, cols] += _dot(p, do, 0, 0)

    return pl.pallas_call(
        body, name=name, grid=(n_seq, nq),
        in_specs=[q_spec, k_spec, pl.BlockSpec((mlen, d), lambda b, i: (b, 1)), q_spec],
        out_specs=[q_spec, k_spec, k_spec],
        out_shape=[jax.ShapeDtypeStruct((t, d), BF16), jax.ShapeDtypeStruct((n_seq * mlen, d), F32),
                   jax.ShapeDtypeStruct((n_seq * mlen, d), F32)],
        compiler_params=_params(2),
    )(q, kv, kv, do)


ADAMW_BLOCK_ELEMS = 128 * 1024


def _adamw(name, parts, w, m, v, first_layer=0, earlier=None, transposed=False):
    n_layers = len(parts)
    r, c = parts[0].shape[1:][::-1] if transposed else parts[0].shape[1:]
    assert w.shape[0] % r == 0 and w.shape[1] == c and first_layer + n_layers <= w.shape[0] // r, (name, w.shape)
    tr = _pick(r, max(V7X_LANES, ADAMW_BLOCK_ELEMS // c // V7X_LANES * V7X_LANES), V7X_LANES if transposed else 8)
    nt = r // tr
    spec = pl.BlockSpec((tr, c), lambda l, i: ((first_layer + l) * nt + i, 0))
    c1 = 1.0 - ADAM_B1 ** ADAM_STEP
    c2 = 1.0 - ADAM_B2 ** ADAM_STEP

    def parts_spec(q):
        def at(l, i):
            return jnp.where(l == q, i, jnp.where(l > q, nt - 1, 0))

        if transposed:
            return pl.BlockSpec((N_DEV, c, tr), lambda l, i: (0, 0, at(l, i)))
        return pl.BlockSpec((N_DEV, tr, c), lambda l, i: (0, at(l, i), 0))

    earlier = list(earlier or ())

    def body(*refs):
        p_refs = refs[:n_layers]
        w_ref, m_ref, v_ref = refs[n_layers:n_layers + 3]
        g_ref, d_ref, nm_ref, nv_ref = refs[n_layers + 3 + len(earlier):]

        def update(p_ref):
            g = p_ref[0].astype(F32)
            for k in range(1, N_DEV):
                g = g + p_ref[k].astype(F32)
            if transposed:
                g = g.T
            nm = ADAM_B1 * m_ref[...] + (1.0 - ADAM_B1) * g
            nv = ADAM_B2 * v_ref[...] + (1.0 - ADAM_B2) * (g * g)
            g_ref[...] = g
            nm_ref[...] = nm
            nv_ref[...] = nv
            d_ref[...] = -ADAM_LR * ((nm / c1) / (jnp.sqrt(nv / c2) + ADAM_EPS) + ADAM_WD * w_ref[...])

        for q in range(n_layers):
            pl.when(pl.program_id(0) == q)(lambda q=q: update(p_refs[q]))

    out = jax.ShapeDtypeStruct(w.shape, F32)
    return pl.pallas_call(
        body, name=name, grid=(n_layers, nt),
        in_specs=[parts_spec(q) for q in range(n_layers)] + [spec] * 3 + [ANY_SPEC] * len(earlier),
        out_specs=[spec] * 4, out_shape=[out] * 4, compiler_params=_params(2),
        input_output_aliases={n_layers + 3 + q: q for q in range(len(earlier))},
    )(*parts, w, m, v, *earlier)


def _place():
    x, y, c = lax.axis_index("x"), lax.axis_index("y"), lax.axis_index("c")
    return x, y, c


def _index(px, py, pc):
    return 4 * px + 2 * py + pc


def _all_gather(name, shards):
    n = len(shards)

    def body(*refs):
        in_refs, out_refs = refs[:n], refs[n:2 * n]
        send_sems, recv_sems, local_sems = refs[2 * n:]
        x, y, c = _place()
        me, sibling = (x, y, c), (x, y, 1 - c)
        chips = [(1 - x, y), (x, 1 - y), (1 - x, 1 - y)]

        def slot(k, block):
            return out_refs[k].at[_index(*block)]

        def copy(k, j, block, to, src=None):
            return pltpu.make_async_remote_copy(
                src_ref=slot(k, block) if src is None else src, dst_ref=slot(k, block),
                send_sem=send_sems.at[7 * k + j], recv_sem=recv_sems.at[7 * k + j], device_id=to, device_id_type=MESH)

        mine = [pltpu.make_async_copy(in_refs[k], slot(k, me), local_sems.at[k]) for k in range(n)]
        for cp in mine:
            cp.start()
        first = []
        for k in range(n):
            first.append(copy(k, 0, me, sibling, src=in_refs[k]))
            first += [copy(k, 1 + j, me, (*chip, c), src=in_refs[k]) for j, chip in enumerate(chips)]
        for cp in first:
            cp.start()
        passed = []
        for j, chip in enumerate(chips):
            for k in range(n):
                copy(k, 1 + j, (*chip, c), me).wait_recv()
                cp = copy(k, 4 + j, (*chip, c), sibling)
                cp.start()
                passed.append(cp)
        for k in range(n):
            copy(k, 0, sibling, me).wait_recv()
            for j, chip in enumerate(chips):
                copy(k, 4 + j, (*chip, 1 - c), me).wait_recv()
        for cp in first + passed:
            cp.wait_send()
        for cp in mine:
            cp.wait()

    return pl.pallas_call(
        body, name=name, in_specs=[HBM_SPEC] * n, out_specs=[HBM_SPEC] * n,
        out_shape=[jax.ShapeDtypeStruct((N_DEV,) + s.shape, s.dtype) for s in shards],
        scratch_shapes=[pltpu.SemaphoreType.DMA((7 * n,)), pltpu.SemaphoreType.DMA((7 * n,)),
                        pltpu.SemaphoreType.DMA((n,))],
    )(*shards)


WHOLE, BLOCK, COLUMNS = "whole", "block", "columns"


def _slot(ref, index, mode):
    if mode == WHOLE:
        return ref
    if mode == BLOCK:
        return ref.at[index]
    width = ref.shape[1] // N_DEV
    return ref.at[:, pl.ds(pl.multiple_of(index * width, width), width)]


DIRECT = tuple(range(1, N_DEV))
CHIPS = (1, 4, 2, 6)
FORWARD = "forward"


def _copies(plan, x, y, c):
    def xor(r, flip_core=False):
        rx, ry, rc = (r >> 2) & 1, (r >> 1) & 1, (r & 1) ^ int(flip_core)
        return (1 - x if rx else x, 1 - y if ry else y, 1 - c if rc else c)

    me = _index(x, y, c)
    if plan == FORWARD:
        return [(xor(1), _index(*xor(r)), _index(*xor(r)), _index(*xor(r, True))) for r in (4, 2, 6)]
    return [(xor(r), _index(*xor(r)), me, _index(*xor(r))) for r in plan]


def _exchange_start(name, srcs, lands, src_modes, land_modes, deps=(), plan=DIRECT):
    n, n_src = len(lands), len(srcs)
    n_copies = 3 if plan == FORWARD else len(plan)

    def body(*refs):
        land_refs = refs[n_src:n_src + n]
        src_refs = refs[:n_src] if n_src else land_refs
        send_sems, recv_sems = refs[n_src + n + len(deps)], refs[n_src + n + len(deps) + 1]
        token = refs[-1]
        x, y, c = _place()
        for k in range(n):
            for j, (peer, src_index, dst_index, _) in enumerate(_copies(plan, x, y, c)):
                pltpu.make_async_remote_copy(
                    src_ref=_slot(src_refs[k], src_index, src_modes[k]),
                    dst_ref=_slot(land_refs[k], dst_index, land_modes[k]), send_sem=send_sems.at[n_copies * k + j],
                    recv_sem=recv_sems.at[n_copies * k + j], device_id=peer, device_id_type=MESH).start()
        token[...] = jnp.zeros_like(token)

    arrays = list(srcs) + list(lands)
    thru = [pltpu.HBM(a.shape, a.dtype) for a in arrays]
    out = pl.pallas_call(
        body, name=name,
        out_shape=(pltpu.SemaphoreType.DMA((n_copies * n,)), pltpu.SemaphoreType.DMA((n_copies * n,)), *thru,
                   jax.ShapeDtypeStruct((8, V7X_LANES), F32)),
        in_specs=[HBM_SPEC] * len(arrays) + [ANY_SPEC] * len(deps),
        out_specs=(SEM_SPEC, SEM_SPEC, *([HBM_SPEC] * len(arrays)), pl.BlockSpec(memory_space=pltpu.VMEM)),
        input_output_aliases={k: 2 + k for k in range(len(arrays))},
        compiler_params=pltpu.CompilerParams(has_side_effects=pltpu.SideEffectType.DATAFLOW_SIDE_EFFECTING),
    )(*[pltpu.with_memory_space_constraint(a, pltpu.HBM) for a in arrays], *deps)
    return out[0], out[1], list(out[2:2 + n_src]), list(out[2 + n_src:2 + n_src + n]), out[-1]


def _exchange_wait(name, send_sems, recv_sems, srcs, lands, src_modes, land_modes, after, plan=DIRECT):
    n, n_src = len(lands), len(srcs)
    n_copies = 3 if plan == FORWARD else len(plan)

    def body(*refs):
        land_refs = refs[n_src:n_src + n]
        src_refs = refs[:n_src] if n_src else land_refs
        send_sems, recv_sems = refs[n_src + n], refs[n_src + n + 1]
        x, y, c = _place()
        for k in range(n):
            for j, (peer, src_index, _, arrival_index) in enumerate(_copies(plan, x, y, c)):
                cp = pltpu.make_async_remote_copy(
                    src_ref=_slot(src_refs[k], src_index, src_modes[k]),
                    dst_ref=_slot(land_refs[k], arrival_index, land_modes[k]), send_sem=send_sems.at[n_copies * k + j],
                    recv_sem=recv_sems.at[n_copies * k + j], device_id=peer, device_id_type=MESH)
                cp.wait_send()
                cp.wait_recv()

    arrays = list(srcs) + list(lands)
    thru = [pltpu.HBM(a.shape, a.dtype) for a in arrays]
    out = pl.pallas_call(
        body, name=name, out_shape=tuple(thru),
        in_specs=[HBM_SPEC] * len(arrays) + [SEM_SPEC, SEM_SPEC, ANY_SPEC], out_specs=tuple([HBM_SPEC] * len(arrays)),
        input_output_aliases={k: k for k in range(len(arrays))},
        compiler_params=pltpu.CompilerParams(has_side_effects=pltpu.SideEffectType.DATAFLOW_SIDE_EFFECTING),
    )(*arrays, send_sems, recv_sems, after)
    return list(out[n_src:])


def _landing(shard, me, mode=BLOCK):
    if mode == COLUMNS:
        k, n = shard.shape
        return lax.dynamic_update_slice(lax.empty((k, N_DEV * n), shard.dtype), shard, (0, me * n))
    zone = lax.empty((N_DEV,) + shard.shape, shard.dtype)
    return lax.dynamic_update_slice(zone, shard[None], (me,) + (0,) * shard.ndim)


def _cols_whole(w):
    return jnp.transpose(w, (1, 0, 2)).reshape(w.shape[1], N_DEV * w.shape[2])


def _rows_whole(w):
    return w.reshape(N_DEV * w.shape[1], w.shape[2])


def _cols_parts(dw):
    k, n8 = dw.shape
    return jnp.transpose(dw.reshape(k, N_DEV, n8 // N_DEV), (1, 0, 2))


def _rows_parts(dw):
    r8, c = dw.shape
    return dw.reshape(N_DEV, r8 // N_DEV, c)


def _pack_rows(arrays):
    rows = []
    for a in arrays:
        flat = a.reshape(-1).astype(F32)
        flat = jnp.pad(flat, [(0, (-flat.shape[0]) % PACK_TILE)])
        rows.append(flat.reshape(-1, V7X_LANES))
    return jnp.concatenate(rows, axis=0)


def _unpack_rows(packed, shapes):
    out, row = [], 0
    for s in shapes:
        size = math.prod(s)
        n_rows = -(-size // PACK_TILE) * 8
        out.append(packed[row:row + n_rows].reshape(-1)[:size].reshape(s))
        row += n_rows
    return out


def _merge2d(a):
    return a.reshape(-1, a.shape[-1])


def _ffn_fwd(tag, x, g, w_up_t, w_down, deps=()):
    f = w_down.shape[0]
    gu, act, n = _mm(f"{tag}_up", x, w_up_t, tb=True, prologue=_rms_rows, prologue_pars=[g], deps=deps,
                     epilogue=lambda acc: _swiglu_rows(acc, f), outs=[(2 * f, BF16), (f, BF16)])
    out = _mm(f"{tag}_down", act, w_down, res=x, scale=0.5, out_dtype=F32)
    return out, (x, n, gu, act)


def _ffn_bwd(tag, dres, saved, g, w_up_t, w_down, deps=(), after_dw=None):
    x, n, gu, act = saved
    dres32, dres16 = dres
    f = w_down.shape[0]
    dgu = _mm(f"{tag}_down_dx", dres16, w_down, tb=True, scale=0.5, deps=deps, epilogue=_swiglu_bwd_rows,
              row_ins=[(gu, 0, f), (gu, 1, f)], outs=[(2 * f, BF16)])[0]
    d_down = _mm(f"{tag}_down_dw", act, dres16, ta=True, scale=0.5)
    d_up_t = _mm(f"{tag}_up_dw", dgu, n, ta=True)
    dx, dg = _mm_rms_bwd(f"{tag}_up_dx", dgu, w_up_t, x, g, dres32, tb=False,
                         deps=after_dw(d_up_t, d_down) if after_dw else ())
    return dx, dg, d_up_t, d_down


def _conv_mixer_fwd(tag, x, g, w_in, w_conv, w_out, n_seq, seq):
    cbv, h = _mm(f"{tag}_in", x, w_in, prologue=_rms_rows, prologue_pars=[g])
    z = _conv_fwd(f"{tag}_conv", cbv, w_conv, n_seq, seq)
    out = _mm(f"{tag}_out", z, w_out, res=x, out_dtype=F32)
    return out, (x, h, cbv, z)


def _conv_mixer_bwd(tag, dres, saved, g, w_in, w_conv, w_out, n_seq, seq):
    x, h, cbv, z = saved
    dres32, dres16 = dres
    dz = _mm(f"{tag}_out_dx", dres16, w_out, tb=True)
    d_out = _mm(f"{tag}_out_dw", z, dres16, ta=True)
    dc, db, dv, d_conv = _conv_bwd(f"{tag}_conv_bwd", dz, cbv, w_conv, n_seq, seq)
    dcbv = jnp.concatenate([dc, db, dv], axis=1)
    d_in = _mm(f"{tag}_in_dw", h, dcbv, ta=True)
    dx, dg = _mm_rms_bwd(f"{tag}_in_dx", dcbv, w_in, x, g, dres32)
    return dx, dg, d_in, d_conv, d_out


def _s5_mixer_fwd(tag, x, g, ssm, dskip, w_glu, n_seq, seq):
    a_re, a_im, log_dt, b_re, b_im, c_re, c_im = ssm
    groups, p, hh = b_re.shape
    gb = S5_CHANNELS // hh
    disc, disc_vjp = jax.vjp(_s5_discretize, a_re, a_im, log_dt, b_re, b_im)
    abar_re, abar_im, bbar_re, bbar_im = disc
    mats = (_block_diag_in(bbar_re, gb).astype(BF16), _block_diag_in(bbar_im, gb).astype(BF16),
            _block_diag_out(c_re, gb).astype(BF16), _block_diag_out(c_im, gb).astype(BF16),
            abar_re.reshape(1, groups * p), abar_im.reshape(1, groups * p), dskip)
    d = x.shape[1]
    h = _rms_fwd(f"{tag}_norm", x, g)
    s_re, s_im, ypre, z = _s5_fwd(f"{tag}_scan", h, *mats, n_seq, seq)
    vg, out = _mm(f"{tag}_glu", z, w_glu, epilogue=lambda vg, x: (vg, x + vg[:, :d] * _sigmoid(vg[:, d:])),
                  row_ins=[x], outs=[(2 * d, BF16), (d, F32)])
    return out, (x, h, s_re, s_im, ypre, z, vg, mats, disc_vjp, (groups, p, hh, gb))


def _s5_mixer_bwd(tag, dres, saved, g, w_glu, n_seq, seq):
    x, h, s_re, s_im, ypre, z, vg, mats, disc_vjp, (groups, p, hh, gb) = saved
    d = x.shape[1]
    dres32, _ = dres
    dvg = _glu_bwd(f"{tag}_glu_act_bwd", dres32, vg, d)
    d_glu = _mm(f"{tag}_glu_dw", z, dvg, ta=True)
    dz = _mm(f"{tag}_glu_dx", dvg, w_glu, tb=True)
    dh, dbin_re, dbin_im, dcout_re, dcout_im, dabar_re, dabar_im, d_skip = _s5_bwd(
        f"{tag}_scan_bwd", dz, ypre, h, s_re, s_im, *mats, n_seq, seq)
    d_are, d_aim, d_logdt, d_bre, d_bim = disc_vjp((
        dabar_re.reshape(groups, p), dabar_im.reshape(groups, p),
        _block_diag_in_t(dbin_re, gb, p, hh), _block_diag_in_t(dbin_im, gb, p, hh)))
    d_cre = _block_diag_out_t(dcout_re, gb, p, hh)
    d_cim = _block_diag_out_t(dcout_im, gb, p, hh)
    dx, dg = _rms_bwd(f"{tag}_norm_bwd", x, g, dh, dres32)
    return dx, dg, (d_are, d_aim, d_logdt, d_bre, d_bim, d_cre, d_cim), d_skip, d_glu


def _xattn_fwd(tag, x, mem, g_q, g_mem, w_q, w_kv, w_o, n_seq, seq, mlen, heads):
    q, n = _mm(f"{tag}_q", x, w_q, prologue=_rms_rows, prologue_pars=[g_q])
    mem_n = _rms_fwd(f"{tag}_mem_norm", mem, g_mem)
    kv = _mm(f"{tag}_kv", mem_n, w_kv)
    o = _attn_fwd(f"{tag}_attn", q, kv, n_seq, seq, mlen, heads)
    out = _mm(f"{tag}_o", o, w_o, res=x, out_dtype=F32)
    return out, (x, n, q, mem_n, kv, o)


def _xattn_bwd(tag, dres, saved, mem, g_q, g_mem, w_q, w_kv, w_o, n_seq, seq, mlen, heads):
    x, n, q, mem_n, kv, o = saved
    dres32, dres16 = dres
    do = _mm(f"{tag}_o_dx", dres16, w_o, tb=True)
    d_o = _mm(f"{tag}_o_dw", o, dres16, ta=True)
    dq, dk, dv = _attn_bwd(f"{tag}_attn_bwd", q, kv, do, n_seq, seq, mlen, heads)
    dkv = jnp.concatenate([dk, dv], axis=1)
    d_q = _mm(f"{tag}_q_dw", n, dq, ta=True)
    d_kv = _mm(f"{tag}_kv_dw", mem_n, dkv, ta=True)
    dmem_n = _mm(f"{tag}_kv_dx", dkv, w_kv, tb=True)
    _, dg_mem = _rms_bwd(f"{tag}_mem_norm_bwd", mem, g_mem, dmem_n)
    dx, dg_q = _mm_rms_bwd(f"{tag}_q_dx", dq, w_q, x, g_q, dres32)
    return dx, dg_q, dg_mem, d_q, d_kv, d_o


WEIGHT_NAMES = ("norm_g", "final_g", "ffn1_up", "ffn1_down", "ffn2_up", "ffn2_down", "conv_w_in", "conv_w",
                "conv_w_out", "ssm_a_re", "ssm_a_im", "ssm_log_dt", "ssm_b_re", "ssm_b_im", "ssm_c_re", "ssm_c_im",
                "ssm_d", "ssm_w_glu", "xa_w_q", "xa_w_kv", "xa_w_o")
MATRICES = ("ffn1_up", "ffn1_down", "ffn2_up", "ffn2_down", "conv_w_in", "conv_w_out", "ssm_w_glu", "xa_w_q",
            "xa_w_kv", "xa_w_o")
COLUMN_SHARDED = ("conv_w_in", "ssm_w_glu", "xa_w_kv")
TRANSPOSED = ("ffn1_up", "ffn2_up")
SMALL_SHARDED = ("norm_g", "conv_w", "ssm_d")
REPLICATED = ("ssm_a_re", "ssm_a_im", "ssm_log_dt", "ssm_b_re", "ssm_b_im", "ssm_c_re", "ssm_c_im", "final_g")


def kernel(x, mem, norm_g, final_g, ffn1_up, ffn1_down, ffn2_up, ffn2_down, conv_w_in, conv_w, conv_w_out, ssm_a_re, ssm_a_im, ssm_log_dt, ssm_b_re, ssm_b_im, ssm_c_re, ssm_c_im, ssm_d, ssm_w_glu, xa_w_q, xa_w_kv, xa_w_o, loss_target, m_norm_g, m_final_g, m_ffn1_up, m_ffn1_down, m_ffn2_up, m_ffn2_down, m_conv_w_in, m_conv_w, m_conv_w_out, m_ssm_a_re, m_ssm_a_im, m_ssm_log_dt, m_ssm_b_re, m_ssm_b_im, m_ssm_c_re, m_ssm_c_im, m_ssm_d, m_ssm_w_glu, m_xa_w_q, m_xa_w_kv, m_xa_w_o, v_norm_g, v_final_g, v_ffn1_up, v_ffn1_down, v_ffn2_up, v_ffn2_down, v_conv_w_in, v_conv_w, v_conv_w_out, v_ssm_a_re, v_ssm_a_im, v_ssm_log_dt, v_ssm_b_re, v_ssm_b_im, v_ssm_c_re, v_ssm_c_im, v_ssm_d, v_ssm_w_glu, v_xa_w_q, v_xa_w_kv, v_xa_w_o):
    w = dict(norm_g=norm_g, final_g=final_g, ffn1_up=ffn1_up, ffn1_down=ffn1_down, ffn2_up=ffn2_up,
             ffn2_down=ffn2_down, conv_w_in=conv_w_in, conv_w=conv_w, conv_w_out=conv_w_out, ssm_a_re=ssm_a_re,
             ssm_a_im=ssm_a_im, ssm_log_dt=ssm_log_dt, ssm_b_re=ssm_b_re, ssm_b_im=ssm_b_im, ssm_c_re=ssm_c_re,
             ssm_c_im=ssm_c_im, ssm_d=ssm_d, ssm_w_glu=ssm_w_glu, xa_w_q=xa_w_q, xa_w_kv=xa_w_kv, xa_w_o=xa_w_o)
    mom = dict(norm_g=m_norm_g, final_g=m_final_g, ffn1_up=m_ffn1_up, ffn1_down=m_ffn1_down, ffn2_up=m_ffn2_up,
               ffn2_down=m_ffn2_down, conv_w_in=m_conv_w_in, conv_w=m_conv_w, conv_w_out=m_conv_w_out,
               ssm_a_re=m_ssm_a_re, ssm_a_im=m_ssm_a_im, ssm_log_dt=m_ssm_log_dt, ssm_b_re=m_ssm_b_re,
               ssm_b_im=m_ssm_b_im, ssm_c_re=m_ssm_c_re, ssm_c_im=m_ssm_c_im, ssm_d=m_ssm_d, ssm_w_glu=m_ssm_w_glu,
               xa_w_q=m_xa_w_q, xa_w_kv=m_xa_w_kv, xa_w_o=m_xa_w_o)
    var = dict(norm_g=v_norm_g, final_g=v_final_g, ffn1_up=v_ffn1_up, ffn1_down=v_ffn1_down, ffn2_up=v_ffn2_up,
               ffn2_down=v_ffn2_down, conv_w_in=v_conv_w_in, conv_w=v_conv_w, conv_w_out=v_conv_w_out,
               ssm_a_re=v_ssm_a_re, ssm_a_im=v_ssm_a_im, ssm_log_dt=v_ssm_log_dt, ssm_b_re=v_ssm_b_re,
               ssm_b_im=v_ssm_b_im, ssm_c_re=v_ssm_c_re, ssm_c_im=v_ssm_c_im, ssm_d=v_ssm_d, ssm_w_glu=v_ssm_w_glu,
               xa_w_q=v_xa_w_q, xa_w_kv=v_xa_w_kv, xa_w_o=v_xa_w_o)

    n_seq, seq, d = x.shape
    mlen = mem.shape[1]
    depth, n_norms = norm_g.shape[0], norm_g.shape[1]
    heads = 4
    tokens = n_seq * seq
    x2 = x.reshape(tokens, d)
    mem2 = mem.reshape(n_seq * mlen, d)
    tgt2 = loss_target.reshape(tokens, d)

    small_shapes = [w[k].shape for k in SMALL_SHARDED]
    small_rows = [_merge2d(w[k]) for k in SMALL_SHARDED]
    small_counts = [s.shape[0] for s in small_rows]
    small = jnp.concatenate(small_rows, axis=0)
    small = jnp.pad(small, [(0, (-small.shape[0]) % 8), (0, 0)])
    me = _index(*_place())

    def layer_weights(i):
        names = [(k, i) for k in ("ffn1_up", "ffn1_down", "ffn2_up", "ffn2_down", "xa_w_q", "xa_w_kv", "xa_w_o")]
        return names + ([("conv_w_in", i // 2), ("conv_w_out", i // 2)] if i % 2 == 0 else [("ssm_w_glu", i // 2)])

    shards = [[(w[k][idx].T if k in TRANSPOSED else w[k][idx]).astype(BF16) for k, idx in layer_weights(i)]
              for i in range(depth)]
    n_first = 2
    gathered = _all_gather("gather_layer0_ffn1", shards[0][:n_first] + [small])
    small_all = gathered[-1]
    in_flight = [None] * depth
    token = gathered[0]
    in_place = {k for k in COLUMN_SHARDED if w[k].shape[-1] % V7X_LANES == 0}

    def modes(i):
        start = n_first if i == 0 else 0
        return [COLUMNS if k in in_place else BLOCK for k, _ in layer_weights(i)][start:]

    for i in range(depth):
        mine = shards[i][n_first:] if i == 0 else shards[i]
        zones = [_landing(s, me, mode) for s, mode in zip(mine, modes(i))]
        *in_flight[i], token = _exchange_start(f"gather_start_l{i}", mine, zones, [WHOLE] * len(zones), modes(i),
                                               deps=[token], plan=CHIPS)
    passing = [None] * depth

    def pass_on(i, after):
        landed = _exchange_wait(f"gather_wait_l{i}", *in_flight[i], [WHOLE] * len(modes(i)), modes(i), after,
                                plan=CHIPS)
        *passing[i], forward_token = _exchange_start(f"forward_start_l{i}", [], landed, modes(i), modes(i),
                                                     plan=FORWARD)
        return forward_token

    def passed(i, after):
        return _exchange_wait(f"forward_wait_l{i}", *passing[i], modes(i), modes(i), after, plan=FORWARD)

    def small_whole(idx):
        start = sum(small_counts[:idx])
        part = small_all[:, start:start + small_counts[idx]]
        lead = small_shapes[idx][:-1]
        part = part.reshape((N_DEV,) + lead + (part.shape[-1],))
        part = jnp.moveaxis(part, 0, -2)
        return part.reshape(lead + (N_DEV * part.shape[-1],))

    norm_all = small_whole(0)
    conv_all = small_whole(1)
    dskip_all = small_whole(2)

    def whole(names, arrived):
        return {k: blk if k in in_place else _cols_whole(blk) if k in COLUMN_SHARDED else _rows_whole(blk)
                for (k, _), blk in zip(names, arrived)}

    saved = []
    cur = x2
    for i in range(depth):
        g = [norm_all[i, k].reshape(1, d) for k in range(n_norms)]
        j = i // 2
        if i == 0:
            lw = whole(layer_weights(0)[:n_first], gathered[:n_first])
        else:
            lw = whole(layer_weights(i), passed(i, cur))
        cur, s_ffn1 = _ffn_fwd(f"l{i}_ffn1", cur, g[0], lw["ffn1_up"], lw["ffn1_down"], [token] if i == 0 else ())
        if i == 0:
            lw.update(whole(layer_weights(0)[n_first:], passed(0, pass_on(0, cur))))
        if i % 2 == 0:
            lw["conv_w"] = conv_all[j]
            cur, s_mix = _conv_mixer_fwd(f"l{i}_conv", cur, g[1], lw["conv_w_in"], lw["conv_w"], lw["conv_w_out"],
                                         n_seq, seq)
        else:
            ssm = tuple(w[k][j] for k in ("ssm_a_re", "ssm_a_im", "ssm_log_dt", "ssm_b_re", "ssm_b_im",
                                          "ssm_c_re", "ssm_c_im"))
            cur, s_mix = _s5_mixer_fwd(f"l{i}_s5", cur, g[1], ssm, dskip_all[j].reshape(1, d), lw["ssm_w_glu"],
                                       n_seq, seq)
        cur, s_xa = _xattn_fwd(f"l{i}_xa", cur, mem2, g[2], g[3], lw["xa_w_q"], lw["xa_w_kv"], lw["xa_w_o"],
                               n_seq, seq, mlen, heads)
        deps = [pass_on(i + 1, cur)] if i + 1 < depth else ()
        cur, s_ffn2 = _ffn_fwd(f"l{i}_ffn2", cur, g[4], lw["ffn2_up"], lw["ffn2_down"], deps)
        saved.append((g, lw, s_ffn1, s_mix, s_xa, s_ffn2))

    dres, err2, d_final = _final_loss("loss_head", cur, final_g.reshape(1, d), tgt2)
    loss = lax.psum(0.5 * jnp.sum(err2) / d, ("x", "y", "c"))

    d_norm = [[None] * n_norms for _ in range(depth)]
    d_conv = [None] * conv_w.shape[0]
    d_skip = [None] * ssm_d.shape[0]
    d_ssm = [None] * ssm_a_re.shape[0]
    leaving = [None] * depth
    deps = ()

    def leave(name, keys, gm, extra=()):
        srcs, src_modes, zones = [], [], []
        for k in keys:
            if k in in_place:
                rows, n = gm[k].shape[0], gm[k].shape[1] // N_DEV
                srcs.append(gm[k])
                src_modes.append(COLUMNS)
                zones.append(_landing(lax.dynamic_slice(gm[k], (0, me * n), (rows, n)), me))
            else:
                srcs.append(_cols_parts(gm[k]) if k in COLUMN_SHARDED else _rows_parts(gm[k]))
                src_modes.append(BLOCK)
                zones.append(_landing(lax.dynamic_index_in_dim(srcs[-1], me, 0, keepdims=False), me))
        for p in extra:
            srcs.append(p)
            src_modes.append(BLOCK)
            zones.append(_landing(lax.dynamic_index_in_dim(p, me, 0, keepdims=False), me))
        land_modes = [BLOCK] * len(srcs)
        *handles, token = _exchange_start(name, srcs, zones, src_modes, land_modes)
        return (*handles, src_modes, land_modes), token

    def small_parts(full):
        lead = full.shape[:-1]
        t = full.reshape(lead + (N_DEV, full.shape[-1] // N_DEV))
        t = jnp.moveaxis(t, -2, 0)
        return t.reshape(N_DEV, -1, t.shape[-1])

    for i in reversed(range(depth)):
        g, lw, s_ffn1, s_mix, s_xa, s_ffn2 = saved[i]
        j = i // 2
        gm = {}
        dres, d_norm[i][4], gm["ffn2_up"], gm["ffn2_down"] = _ffn_bwd(
            f"l{i}_ffn2", dres, s_ffn2, g[4], lw["ffn2_up"], lw["ffn2_down"], deps)
        dres, d_norm[i][2], d_norm[i][3], gm["xa_w_q"], gm["xa_w_kv"], gm["xa_w_o"] = _xattn_bwd(
            f"l{i}_xa", dres, s_xa, mem2, g[2], g[3], lw["xa_w_q"], lw["xa_w_kv"], lw["xa_w_o"], n_seq, seq, mlen,
            heads)
        if i % 2 == 0:
            dres, d_norm[i][1], gm["conv_w_in"], d_conv[j], gm["conv_w_out"] = _conv_mixer_bwd(
                f"l{i}_conv", dres, s_mix, g[1], lw["conv_w_in"], lw["conv_w"], lw["conv_w_out"], n_seq, seq)
        else:
            dres, d_norm[i][1], d_ssm[j], d_skip[j], gm["ssm_w_glu"] = _s5_mixer_bwd(
                f"l{i}_s5", dres, s_mix, g[1], lw["ssm_w_glu"], n_seq, seq)
        upper, token = leave(f"grads_start_l{i}_upper", [k for k, _ in layer_weights(i)[2:]], gm)
        lower = []

        def send_lower(d_up_t, d_down, i=i, lower=lower):
            handles, token = leave(f"grads_start_l{i}_lower", ["ffn1_up", "ffn1_down"],
                                   {"ffn1_up": d_up_t, "ffn1_down": d_down})
            lower.append(handles)
            return [token]

        dres, d_norm[i][0], _, _ = _ffn_bwd(
            f"l{i}_ffn1", dres, s_ffn1, g[0], lw["ffn1_up"], lw["ffn1_down"], [token], after_dw=send_lower)
        leaving[i] = (upper, lower[0])
        deps = ()
        if i == min(1, depth - 1):
            rep_grads = [jnp.stack([d_ssm[j][k] for j in range(len(d_ssm))]) for k in range(7)]
            rep_packed = _pack_rows(rep_grads + [d_final.reshape(-1)])
            *rep_leaving, token = _exchange_start("replicated_grads_start", [rep_packed], [_landing(rep_packed, me)],
                                                  [WHOLE], [BLOCK])
            deps = [token]
    grad_x = dres[0].reshape(n_seq, seq, d)
    d_norm_all = jnp.stack([jnp.concatenate(row, axis=0) for row in d_norm])
    small_g = jnp.concatenate(
        [small_parts(a) for a in (d_norm_all, jnp.stack(d_conv), jnp.concatenate(d_skip, axis=0))], axis=1)
    small_g = jnp.pad(small_g, [(0, 0), (0, (-small_g.shape[1]) % 8), (0, 0)])
    small_leaving, _ = leave("small_grads_start", [], {}, [small_g])
    received = {k: [None] * w[k].shape[0] for k in MATRICES}

    def arrive(i, after):
        upper = _exchange_wait(f"grads_wait_l{i}_upper", *leaving[i][0], after)
        lower = _exchange_wait(f"grads_wait_l{i}_lower", *leaving[i][1], after)
        for (k, idx), blk in zip(layer_weights(i), lower + upper):
            received[k][idx] = blk

    for i in range(1, depth):
        arrive(i, dres[0])
    rep_all = _exchange_wait("replicated_grads_wait", *rep_leaving, [WHOLE], [BLOCK], dres[0])[0]
    rep_shapes = [w[k].shape for k in REPLICATED]
    flat = {k: (_merge2d(w[k]), _merge2d(mom[k]), _merge2d(var[k])) for k in MATRICES}
    late = {k: received[k][0] is None for k in MATRICES}
    early = {}
    for k in MATRICES:
        first = 1 if late[k] else 0
        if first < len(received[k]):
            early[k] = _adamw(f"adamw_{k}_upper", received[k][first:], *flat[k], first_layer=first,
                              transposed=k in TRANSPOSED)
    after = list(early.values())[-1][0] if early else dres[0]
    arrive(0, after)
    small_received, = _exchange_wait("small_grads_wait", *small_leaving, after)
    grads, deltas, new_m, new_v = {}, {}, {}, {}
    for k in MATRICES:
        out = early.get(k)
        if late[k]:
            out = _adamw(f"adamw_{k}_l0", received[k][:1], *flat[k], earlier=out, transposed=k in TRANSPOSED)
        grads[k], deltas[k], new_m[k], new_v[k] = [o.reshape(w[k].shape) for o in out]

    def small_local(src):
        rows = jnp.concatenate([_merge2d(src[k]) for k in SMALL_SHARDED], axis=0)
        return jnp.pad(rows, [(0, (-rows.shape[0]) % 8), (0, 0)])

    out = _adamw("adamw_small", [small_received], small, small_local(mom), small_local(var))
    for res, o in zip((grads, deltas, new_m, new_v), out):
        start = 0
        for k, cnt, shape in zip(SMALL_SHARDED, small_counts, small_shapes):
            res[k] = o[start:start + cnt].reshape(shape)
            start += cnt

    out = _adamw("adamw_replicated", [rep_all], _pack_rows([w[k] for k in REPLICATED]),
                 _pack_rows([mom[k] for k in REPLICATED]), _pack_rows([var[k] for k in REPLICATED]))
    for res, o in zip((grads, deltas, new_m, new_v), out):
        for k, a in zip(REPLICATED, _unpack_rows(o, rep_shapes)):
            res[k] = a

    return (loss, grad_x, *[grads[k] for k in WEIGHT_NAMES], *[deltas[k] for k in WEIGHT_NAMES],
            *[new_m[k] for k in WEIGHT_NAMES], *[new_v[k] for k in WEIGHT_NAMES])
```

```python
import math

import jax
import jax.numpy as jnp
from jax import lax
from jax.experimental import pallas as pl
from jax.experimental.pallas import tpu as pltpu

F32 = jnp.float32
BF16 = jnp.bfloat16
MESH = pl.DeviceIdType.MESH
N_DEV = 8

NORM_EPS = 1e-6
EIG_CLIP = -1e-4
CONV_WIDTH = 3
ADAM_LR = 0.001
ADAM_B1 = 0.9
ADAM_B2 = 0.999
ADAM_EPS = 1e-08
ADAM_WD = 0.01
ADAM_STEP = 10
GELU_C = math.sqrt(2.0 / math.pi)
GELU_A = 0.044715

V7X_LANES = 128
V7X_VMEM_LIMIT = 56 * 1024 * 1024
S5_CHANNELS = 128
PACK_TILE = 8 * V7X_LANES

HBM_SPEC = pl.BlockSpec(memory_space=pltpu.HBM)
ANY_SPEC = pl.BlockSpec(memory_space=pl.ANY)
SEM_SPEC = pl.BlockSpec(memory_space=pltpu.SEMAPHORE)


def _params(n_grid):
    return pltpu.CompilerParams(dimension_semantics=("arbitrary",) * n_grid, vmem_limit_bytes=V7X_VMEM_LIMIT)


def _pick(n, pref, align):
    if n <= pref:
        return n
    t = (pref // align) * align
    while t >= align:
        if n % t == 0:
            return t
        t -= align
    raise ValueError(f"no tile for {n} (pref {pref}, align {align})")


MM_RHS_BLOCK_BYTES = 12 * 1024 * 1024
MM_LHS_BLOCK_BYTES = 6 * 1024 * 1024
MM_ACC_BYTES = 6 * 1024 * 1024
MM_ROWS = 512


MM_EPILOGUE_ROWS = 256


def _mm_tiles(m, k, n, a_item, b_item, ta, max_rows):
    tn = _pick(n, max(V7X_LANES, MM_RHS_BLOCK_BYTES // (k * b_item)), V7X_LANES)
    rows = min(max_rows, MM_ACC_BYTES // (4 * tn), MM_LHS_BLOCK_BYTES // (k * a_item))
    align = V7X_LANES if ta else 16
    tm = _pick(m, max(align, rows), align)
    return tm, tn


def _store_results(out_refs, n_row, results, first):
    if not isinstance(results, (tuple, list)):
        results = (results,)
    for o, v in zip(out_refs[:n_row], results[:n_row]):
        if isinstance(v, (tuple, list)):
            off = 0
            for piece in v:
                w = piece.shape[1]
                o[:, off:off + w] = piece.astype(o.dtype)
                off += w
        else:
            o[...] = v.astype(o.dtype)
    if len(out_refs) > n_row:
        @pl.when(first)
        def _():
            for o in out_refs[n_row:]:
                o[...] = jnp.zeros_like(o)

        for o, v in zip(out_refs[n_row:], results[n_row:]):
            o[...] += v


def _mm(name, a, b, *, ta=False, tb=False, out_dtype=BF16, res=None, scale=None, deps=(),
        epilogue=None, row_ins=(), par_ins=(), outs=(), acc_outs=(), prologue=None, prologue_pars=()):
    if ta:
        k, m = a.shape
    else:
        m, k = a.shape
    if tb:
        n, k2 = b.shape
    else:
        k2, n = b.shape
    assert k == k2, (name, a.shape, b.shape)
    max_rows = MM_ROWS if epilogue is None else MM_EPILOGUE_ROWS
    tm, tn = _mm_tiles(m, k, n, a.dtype.itemsize, b.dtype.itemsize, ta, max_rows)
    a_spec = pl.BlockSpec((k, tm), lambda j, i: (0, i)) if ta else pl.BlockSpec((tm, k), lambda j, i: (i, 0))
    b_spec = pl.BlockSpec((tn, k), lambda j, i: (j, 0)) if tb else pl.BlockSpec((k, tn), lambda j, i: (0, j))
    o_spec = pl.BlockSpec((tm, tn), lambda j, i: (i, j))
    dims = (((0 if ta else 1,), (1 if tb else 0,)), ((), ()))
    has_res = res is not None
    ins = [a, b] + ([res] if has_res else [])
    specs = [a_spec, b_spec] + ([o_spec] if has_res else [])
    n_mm = len(ins)
    if epilogue is None:
        out_specs, out_shape = [o_spec], [jax.ShapeDtypeStruct((m, n), out_dtype)]
    else:
        assert tn == n, (name, tn, n)
        for r in row_ins:
            arr, cb, cw = r if isinstance(r, tuple) else (r, 0, r.shape[1])
            assert arr.shape[0] == m, (name, arr.shape, m)
            ins.append(arr)
            specs.append(pl.BlockSpec((tm, cw), lambda j, i, cb=cb: (i, cb)))
        for p in par_ins:
            ins.append(p)
            specs.append(pl.BlockSpec(p.shape, lambda j, i: (0, 0)))
        out_specs = [pl.BlockSpec((tm, c), lambda j, i: (i, 0)) for c, _ in outs]
        out_specs += [pl.BlockSpec((r, c), lambda j, i: (0, 0)) for r, c in acc_outs]
        out_shape = [jax.ShapeDtypeStruct((m, c), dt) for c, dt in outs]
        out_shape += [jax.ShapeDtypeStruct((r, c), F32) for r, c in acc_outs]
    n_in = len(ins)
    if prologue is not None:
        assert tn == n and not ta, (name, tn, n, ta)
        for p in prologue_pars:
            ins.append(p)
            specs.append(pl.BlockSpec(p.shape, lambda j, i: (0, 0)))
        out_specs = out_specs + [a_spec]
        out_shape = out_shape + [jax.ShapeDtypeStruct((m, k), BF16)]
    n_pro = len(ins)
    ins += list(deps)
    specs += [ANY_SPEC] * len(deps)

    def body(*refs):
        a_ref, b_ref = refs[0], refs[1]
        out_refs = refs[n_pro + len(deps):]
        if prologue is None:
            lhs = a_ref[...].astype(BF16)
        else:
            lhs = prologue(a_ref[...], *[r[...] for r in refs[n_in:n_pro]]).astype(BF16)
            out_refs[-1][...] = lhs
            out_refs = out_refs[:-1]
        acc = lax.dot_general(lhs, b_ref[...].astype(BF16), dims, preferred_element_type=F32)
        if scale is not None:
            acc = acc * scale
        if has_res:
            acc = acc + refs[2][...].astype(F32)
        if epilogue is None:
            out_refs[0][...] = acc.astype(out_refs[0].dtype)
        else:
            extra = [r[...] for r in refs[n_mm:n_in]]
            _store_results(out_refs, len(outs), epilogue(acc, *extra), pl.program_id(1) == 0)

    out = pl.pallas_call(
        body, name=name, grid=(n // tn, m // tm), in_specs=specs, out_specs=out_specs, out_shape=out_shape,
        compiler_params=_params(2),
    )(*ins)
    return out[0] if epilogue is None and prologue is None else out


def _rowwise(name, fn, rows, row_ins, par_ins, row_outs, acc_outs=(), tm_pref=256):
    tm = _pick(rows, tm_pref, 16)
    in_specs, ins = [], []
    for r in row_ins:
        arr, cb, cw = r if isinstance(r, tuple) else (r, 0, r.shape[1])
        assert arr.shape[0] == rows, (name, arr.shape, rows)
        ins.append(arr)
        in_specs.append(pl.BlockSpec((tm, cw), lambda i, cb=cb: (i, cb)))
    for p in par_ins:
        ins.append(p)
        in_specs.append(pl.BlockSpec(p.shape, lambda i: (0, 0)))
    out_specs = [pl.BlockSpec((tm, c), lambda i: (i, 0)) for c, _ in row_outs]
    out_specs += [pl.BlockSpec((r, c), lambda i: (0, 0)) for r, c in acc_outs]
    out_shape = [jax.ShapeDtypeStruct((rows, c), dt) for c, dt in row_outs]
    out_shape += [jax.ShapeDtypeStruct((r, c), F32) for r, c in acc_outs]
    n_in, n_row = len(ins), len(row_outs)

    def body(*refs):
        vals = [r[...] for r in refs[:n_in]]
        _store_results(refs[n_in:], n_row, fn(*vals), pl.program_id(0) == 0)

    return pl.pallas_call(
        body, name=name, grid=(rows // tm,), in_specs=in_specs, out_specs=out_specs, out_shape=out_shape,
        compiler_params=_params(1),
    )(*ins)


def _inv_rms(x):
    return lax.rsqrt(jnp.mean(x * x, axis=-1, keepdims=True) + NORM_EPS)


def _rms_rows(x, g):
    return x * _inv_rms(x) * g


def _rms_fwd(name, x, g):
    return _rowwise(name, _rms_rows, x.shape[0], [x], [g], [(x.shape[1], BF16)], tm_pref=512)[0]


def _rms_bwd_rows(dn, x, dres, g):
    r = _inv_rms(x)
    xh = x * r
    dg = jnp.sum(dn * xh, axis=0, keepdims=True)
    dxh = dn * g
    dx = r * (dxh - xh * jnp.mean(dxh * xh, axis=-1, keepdims=True)) + dres
    return dx, dx, dg


def _rms_bwd(name, x, g, dn, dres=None):
    d = x.shape[1]
    if dres is None:
        def fn(x, dn, g):
            return (jnp.sum(dn.astype(F32) * (x * _inv_rms(x)), axis=0, keepdims=True),)

        return None, _rowwise(name, fn, x.shape[0], [x, dn], [g], [], [(1, d)])[0]

    def fn(x, dn, dres, g):
        return _rms_bwd_rows(dn.astype(F32), x, dres, g)

    out = _rowwise(name, fn, x.shape[0], [x, dn, dres], [g], [(d, F32), (d, BF16)], [(1, d)])
    return (out[0], out[1]), out[2]


def _mm_rms_bwd(name, dy, w, x, g, dres, tb=True, deps=()):
    d = x.shape[1]
    out = _mm(name, dy, w, tb=tb, deps=deps, epilogue=_rms_bwd_rows, row_ins=[x, dres], par_ins=[g],
              outs=[(d, F32), (d, BF16)], acc_outs=[(1, d)])
    return (out[0], out[1]), out[2]


def _sigmoid(x):
    return 0.5 + 0.5 * jnp.tanh(0.5 * x)


def _swiglu_rows(gu, f):
    gt = gu[:, :f]
    return gu, gt * _sigmoid(gt) * gu[:, f:]


def _swiglu_bwd_rows(dact, gt, up):
    gt, up = gt.astype(F32), up.astype(F32)
    sg = _sigmoid(gt)
    return ((dact * up * (sg * (1.0 + gt * (1.0 - sg))), dact * (gt * sg)),)


def _glu_bwd(name, dres, vg, d):
    def fn(dres, val, gate):
        val, gate = val.astype(F32), gate.astype(F32)
        sg = _sigmoid(gate)
        return ((dres * sg, dres * val * sg * (1.0 - sg)),)

    return _rowwise(name, fn, dres.shape[0], [dres, (vg, 0, d), (vg, 1, d)], [], [(2 * d, BF16)])[0]


def _final_loss(name, x, g, tgt):
    d = x.shape[1]

    def fn(x, tgt, g):
        r = _inv_rms(x)
        xh = x * r
        err = xh * g - tgt
        dy = err * (1.0 / d)
        dxh = dy * g
        dx = r * (dxh - xh * jnp.mean(dxh * xh, axis=-1, keepdims=True))
        return dx, dx, jnp.sum(err * err, axis=0, keepdims=True), jnp.sum(dy * xh, axis=0, keepdims=True)

    dx, dx16, err2, dg = _rowwise(name, fn, x.shape[0], [x, tgt], [g], [(d, F32), (d, BF16)], [(1, d), (1, d)])
    return (dx, dx16), err2, dg


def _shift_down(u, k):
    rows = lax.broadcasted_iota(jnp.int32, u.shape, 0)
    return jnp.where(rows >= k, pltpu.roll(u, k, 0), 0.0)


def _shift_up(u, k):
    n = u.shape[0]
    rows = lax.broadcasted_iota(jnp.int32, u.shape, 0)
    return jnp.where(rows < n - k, pltpu.roll(u, n - k, 0), 0.0)


def _conv_specs(seq, cw, n_cb, swap):
    def at(off):
        if swap:
            return pl.BlockSpec((seq, cw), lambda j, b: (b, off * n_cb + j))
        return pl.BlockSpec((seq, cw), lambda b, j: (b, off * n_cb + j))

    return at


def _conv_fwd(name, cbv, w, n_seq, seq):
    d = w.shape[1]
    cw = _pick(d, 256, V7X_LANES)
    n_cb = d // cw
    at = _conv_specs(seq, cw, n_cb, swap=False)

    def body(c_ref, b_ref, v_ref, w_ref, z_ref):
        u = c_ref[...].astype(F32) * v_ref[...].astype(F32)
        cv = w_ref[0:1, :] * _shift_down(u, 2) + w_ref[1:2, :] * _shift_down(u, 1) + w_ref[2:3, :] * u
        z_ref[...] = (b_ref[...].astype(F32) * cv).astype(z_ref.dtype)

    return pl.pallas_call(
        body, name=name, grid=(n_seq, n_cb),
        in_specs=[at(0), at(1), at(2), pl.BlockSpec((CONV_WIDTH, cw), lambda b, j: (0, j))],
        out_specs=at(0), out_shape=jax.ShapeDtypeStruct((n_seq * seq, d), BF16), compiler_params=_params(2),
    )(cbv, cbv, cbv, w)


def _conv_bwd(name, dz, cbv, w, n_seq, seq):
    d = w.shape[1]
    cw = _pick(d, 256, V7X_LANES)
    n_cb = d // cw
    at = _conv_specs(seq, cw, n_cb, swap=True)

    def body(dz_ref, c_ref, b_ref, v_ref, w_ref, dc_ref, db_ref, dv_ref, dw_ref):
        c, b, v = c_ref[...].astype(F32), b_ref[...].astype(F32), v_ref[...].astype(F32)
        dz = dz_ref[...].astype(F32)
        w0, w1, w2 = w_ref[0:1, :], w_ref[1:2, :], w_ref[2:3, :]
        u = c * v
        u1, u2 = _shift_down(u, 1), _shift_down(u, 2)
        cv = w0 * u2 + w1 * u1 + w2 * u
        db_ref[...] = (dz * cv).astype(db_ref.dtype)
        dcv = dz * b
        du = w2 * dcv + w1 * _shift_up(dcv, 1) + w0 * _shift_up(dcv, 2)
        dc_ref[...] = (du * v).astype(dc_ref.dtype)
        dv_ref[...] = (du * c).astype(dv_ref.dtype)

        @pl.when(pl.program_id(1) == 0)
        def _():
            dw_ref[...] = jnp.zeros_like(dw_ref)

        dw_ref[0:1, :] += jnp.sum(dcv * u2, axis=0, keepdims=True)
        dw_ref[1:2, :] += jnp.sum(dcv * u1, axis=0, keepdims=True)
        dw_ref[2:3, :] += jnp.sum(dcv * u, axis=0, keepdims=True)

    act = jax.ShapeDtypeStruct((n_seq * seq, d), BF16)
    return pl.pallas_call(
        body, name=name, grid=(n_cb, n_seq),
        in_specs=[at(0), at(0), at(1), at(2), pl.BlockSpec((CONV_WIDTH, cw), lambda j, b: (0, j))],
        out_specs=[at(0), at(0), at(0), pl.BlockSpec((CONV_WIDTH, cw), lambda j, b: (0, j))],
        out_shape=[act, act, act, jax.ShapeDtypeStruct((CONV_WIDTH, d), F32)], compiler_params=_params(2),
    )(dz, cbv, cbv, cbv, w)


def _s5_discretize(a_re, a_im, log_dt, b_re, b_im):
    lam_re = jnp.minimum(a_re, EIG_CLIP)
    lam_im = a_im
    dt = jnp.exp(log_dt)[:, None]
    mag = jnp.exp(lam_re * dt)
    abar_re = mag * jnp.cos(lam_im * dt)
    abar_im = mag * jnp.sin(lam_im * dt)
    den = lam_re * lam_re + lam_im * lam_im
    num_re = abar_re - 1.0
    num_im = abar_im
    coef_re = (num_re * lam_re + num_im * lam_im) / den
    coef_im = (num_im * lam_re - num_re * lam_im) / den
    bbar_re = coef_re[..., None] * b_re - coef_im[..., None] * b_im
    bbar_im = coef_re[..., None] * b_im + coef_im[..., None] * b_re
    return abar_re, abar_im, bbar_re, bbar_im


def _block_diag_in(bbar, gb):
    g, p, h = bbar.shape
    t = jnp.transpose(bbar.reshape(g // gb, gb, p, h), (0, 1, 3, 2))
    return jnp.einsum("cghp,gk->cghkp", t, jnp.eye(gb, dtype=bbar.dtype)).reshape(g // gb, gb * h, gb * p)


def _block_diag_in_t(blk, gb, p, h):
    nb = blk.shape[0]
    t = jnp.einsum("cghkp,gk->cghp", blk.reshape(nb, gb, h, gb, p), jnp.eye(gb, dtype=blk.dtype))
    return jnp.transpose(t, (0, 1, 3, 2)).reshape(nb * gb, p, h)


def _block_diag_out(c, gb):
    g, h, p = c.shape
    t = jnp.transpose(c.reshape(g // gb, gb, h, p), (0, 1, 3, 2))
    return jnp.einsum("cgph,gk->cgpkh", t, jnp.eye(gb, dtype=c.dtype)).reshape(g // gb, gb * p, gb * h)


def _block_diag_out_t(blk, gb, p, h):
    nb = blk.shape[0]
    t = jnp.einsum("cgpkh,gk->cgph", blk.reshape(nb, gb, p, gb, h), jnp.eye(gb, dtype=blk.dtype))
    return jnp.transpose(t, (0, 1, 3, 2)).reshape(nb * gb, h, p)


def _gelu(y):
    return 0.5 * y * (1.0 + jnp.tanh(GELU_C * (y + GELU_A * y * y * y)))


def _gelu_grad(y):
    th = jnp.tanh(GELU_C * (y + GELU_A * y * y * y))
    return 0.5 * (1.0 + th) + 0.5 * y * (1.0 - th * th) * GELU_C * (1.0 + 3.0 * GELU_A * y * y)


def _dot(a, b, ca, cb):
    return lax.dot_general(a.astype(BF16), b.astype(BF16), (((ca,), (cb,)), ((), ())), preferred_element_type=F32)


def _s5_specs(seq, ch, sb):
    act = pl.BlockSpec((seq, ch), lambda j, b: (b, j))
    state = pl.BlockSpec((seq, sb), lambda j, b: (b, j))
    w_in = pl.BlockSpec((None, ch, sb), lambda j, b: (j, 0, 0))
    w_out = pl.BlockSpec((None, sb, ch), lambda j, b: (j, 0, 0))
    lane_s = pl.BlockSpec((1, sb), lambda j, b: (0, j))
    lane_c = pl.BlockSpec((1, ch), lambda j, b: (0, j))
    return act, state, w_in, w_out, lane_s, lane_c


def _s5_fwd(name, h, bin_re, bin_im, cout_re, cout_im, abar_re, abar_im, dskip, n_seq, seq):
    t, d = h.shape
    nb, ch, sb = bin_re.shape
    act, state, w_in, w_out, lane_s, lane_c = _s5_specs(seq, ch, sb)

    def body(h_ref, bre_ref, bim_ref, cre_ref, cim_ref, ar_ref, ai_ref, d_ref, sre_ref, sim_ref, y_ref, z_ref):
        u = h_ref[...]
        sre_ref[...] = _dot(u, bre_ref[...], 1, 0)
        sim_ref[...] = _dot(u, bim_ref[...], 1, 0)
        ar, ai = ar_ref[...], ai_ref[...]

        def step(i, carry):
            sr, si = carry
            row = pl.ds(i, 1)
            nr = ar * sr - ai * si + sre_ref[row, :]
            ni = ar * si + ai * sr + sim_ref[row, :]
            sre_ref[row, :] = nr
            sim_ref[row, :] = ni
            return nr, ni

        zero = jnp.zeros((1, sb), F32)
        lax.fori_loop(0, seq, step, (zero, zero), unroll=8)
        y = _dot(sre_ref[...], cre_ref[...], 1, 0) - _dot(sim_ref[...], cim_ref[...], 1, 0)
        y = y + d_ref[...] * u.astype(F32)
        y_ref[...] = y
        z_ref[...] = _gelu(y).astype(z_ref.dtype)

    return pl.pallas_call(
        body, name=name, grid=(nb, n_seq),
        in_specs=[act, w_in, w_in, w_out, w_out, lane_s, lane_s, lane_c],
        out_specs=[state, state, act, act],
        out_shape=[jax.ShapeDtypeStruct((t, nb * sb), F32), jax.ShapeDtypeStruct((t, nb * sb), F32),
                   jax.ShapeDtypeStruct((t, d), F32), jax.ShapeDtypeStruct((t, d), BF16)],
        compiler_params=_params(2),
    )(h, bin_re, bin_im, cout_re, cout_im, abar_re, abar_im, dskip)


def _s5_bwd(name, dz, ypre, h, s_re, s_im, bin_re, bin_im, cout_re, cout_im, abar_re, abar_im, dskip, n_seq, seq):
    t, d = h.shape
    nb, ch, sb = bin_re.shape
    act, state, w_in, w_out, lane_s, lane_c = _s5_specs(seq, ch, sb)

    def body(dz_ref, y_ref, h_ref, sre_ref, sim_ref, bre_ref, bim_ref, cre_ref, cim_ref, ar_ref, ai_ref, d_ref,
             dh_ref, dbre_ref, dbim_ref, dcre_ref, dcim_ref, dar_ref, dai_ref, dd_ref, gre, gim):
        first = pl.program_id(1) == 0
        u = h_ref[...].astype(F32)
        dy = dz_ref[...].astype(F32) * _gelu_grad(y_ref[...])
        gre[...] = _dot(dy, cre_ref[...], 1, 1)
        gim[...] = -_dot(dy, cim_ref[...], 1, 1)
        ar, ai = ar_ref[...], ai_ref[...]

        def step(i, carry):
            gr, gi = carry
            row = pl.ds(seq - 1 - i, 1)
            nr = gre[row, :] + ar * gr + ai * gi
            ni = gim[row, :] - ai * gr + ar * gi
            gre[row, :] = nr
            gim[row, :] = ni
            return nr, ni

        zero = jnp.zeros((1, sb), F32)
        lax.fori_loop(0, seq, step, (zero, zero), unroll=8)

        g_re, g_im = gre[...], gim[...]
        s_re, s_im = sre_ref[...], sim_ref[...]
        p_re, p_im = _shift_down(s_re, 1), _shift_down(s_im, 1)
        dar = jnp.sum(g_re * p_re + g_im * p_im, axis=0, keepdims=True)
        dai = jnp.sum(g_im * p_re - g_re * p_im, axis=0, keepdims=True)
        dbre = _dot(u, g_re, 0, 0)
        dbim = _dot(u, g_im, 0, 0)
        dcre = _dot(s_re, dy, 0, 0)
        dcim = -_dot(s_im, dy, 0, 0)
        ddd = jnp.sum(dy * u, axis=0, keepdims=True)
        dh_ref[...] = _dot(g_re, bre_ref[...], 1, 1) + _dot(g_im, bim_ref[...], 1, 1) + d_ref[...] * dy

        @pl.when(first)
        def _():
            dar_ref[...] = dar
            dai_ref[...] = dai
            dbre_ref[...] = dbre
            dbim_ref[...] = dbim
            dcre_ref[...] = dcre
            dcim_ref[...] = dcim
            dd_ref[...] = ddd

        @pl.when(jnp.logical_not(first))
        def _():
            dar_ref[...] += dar
            dai_ref[...] += dai
            dbre_ref[...] += dbre
            dbim_ref[...] += dbim
            dcre_ref[...] += dcre
            dcim_ref[...] += dcim
            dd_ref[...] += ddd

    return pl.pallas_call(
        body, name=name, grid=(nb, n_seq),
        in_specs=[act, act, act, state, state, w_in, w_in, w_out, w_out, lane_s, lane_s, lane_c],
        out_specs=[act, w_in, w_in, w_out, w_out, lane_s, lane_s, lane_c],
        out_shape=[jax.ShapeDtypeStruct((t, d), F32),
                   jax.ShapeDtypeStruct((nb, ch, sb), F32), jax.ShapeDtypeStruct((nb, ch, sb), F32),
                   jax.ShapeDtypeStruct((nb, sb, ch), F32), jax.ShapeDtypeStruct((nb, sb, ch), F32),
                   jax.ShapeDtypeStruct((1, nb * sb), F32), jax.ShapeDtypeStruct((1, nb * sb), F32),
                   jax.ShapeDtypeStruct((1, d), F32)],
        scratch_shapes=[pltpu.VMEM((seq, sb), F32), pltpu.VMEM((seq, sb), F32)],
        compiler_params=_params(2),
    )(dz, ypre, h, s_re, s_im, bin_re, bin_im, cout_re, cout_im, abar_re, abar_im, dskip)


ATTN_QUERY_ROWS = 1024


def _softmax_rows(q, k, scale):
    s = _dot(q, k, 1, 1) * scale
    e = jnp.exp(s - jnp.max(s, axis=-1, keepdims=True))
    return e * (1.0 / jnp.sum(e, axis=-1, keepdims=True))


def _attn_fwd(name, q, kv, n_seq, seq, mlen, heads):
    t, d = q.shape
    hd = d // heads
    tq = _pick(seq, ATTN_QUERY_ROWS, 16)
    nq = seq // tq
    scale = hd ** -0.5
    q_spec = pl.BlockSpec((tq, d), lambda b, i: (b * nq + i, 0))

    def body(q_ref, k_ref, v_ref, o_ref):
        for h in range(heads):
            cols = slice(h * hd, (h + 1) * hd)
            p = _softmax_rows(q_ref[:, cols], k_ref[:, cols], scale)
            o_ref[:, cols] = _dot(p, v_ref[:, cols], 1, 0).astype(o_ref.dtype)

    return pl.pallas_call(
        body, name=name, grid=(n_seq, nq),
        in_specs=[q_spec, pl.BlockSpec((mlen, d), lambda b, i: (b, 0)), pl.BlockSpec((mlen, d), lambda b, i: (b, 1))],
        out_specs=q_spec, out_shape=jax.ShapeDtypeStruct((t, d), BF16), compiler_params=_params(2),
    )(q, kv, kv)


def _attn_bwd(name, q, kv, do, n_seq, seq, mlen, heads):
    t, d = q.shape
    hd = d // heads
    tq = _pick(seq, ATTN_QUERY_ROWS, 16)
    nq = seq // tq
    scale = hd ** -0.5
    q_spec = pl.BlockSpec((tq, d), lambda b, i: (b * nq + i, 0))
    k_spec = pl.BlockSpec((mlen, d), lambda b, i: (b, 0))

    def body(q_ref, k_ref, v_ref, do_ref, dq_ref, dk_ref, dv_ref):
        @pl.when(pl.program_id(1) == 0)
        def _():
            dk_ref[...] = jnp.zeros_like(dk_ref)
            dv_ref[...] = jnp.zeros_like(dv_ref)

        for h in range(heads):
            cols = slice(h * hd, (h + 1) * hd)
            q, k, v, do = q_ref[:, cols], k_ref[:, cols], v_ref[:, cols], do_ref[:, cols]
            p = _softmax_rows(q, k, scale)
            dp = _dot(do, v, 1, 1)
            ds = p * (dp - jnp.sum(dp * p, axis=-1, keepdims=True)) * scale
            dq_ref[:, cols] = _dot(ds, k, 1, 0).astype(dq_ref.dtype)
            dk_ref[:, cols] += _dot(ds, q, 0, 0)
            dv_ref[:, cols] += _dot(p, do, 0, 0)

    return pl.pallas_call(
        body, name=name, grid=(n_seq, nq),
        in_specs=[q_spec, k_spec, pl.BlockSpec((mlen, d), lambda b, i: (b, 1)), q_spec],
        out_specs=[q_spec, k_spec, k_spec],
        out_shape=[jax.ShapeDtypeStruct((t, d), BF16), jax.ShapeDtypeStruct((n_seq * mlen, d), F32),
                   jax.ShapeDtypeStruct((n_seq * mlen, d), F32)],
        compiler_params=_params(2),
    )(q, kv, kv, do)


ADAMW_BLOCK_ELEMS = 128 * 1024


def _adamw(name, parts, w, m, v, first_layer=0, earlier=None, transposed=False):
    n_layers = len(parts)
    r, c = parts[0].shape[1:][::-1] if transposed else parts[0].shape[1:]
    assert w.shape[0] % r == 0 and w.shape[1] == c and first_layer + n_layers <= w.shape[0] // r, (name, w.shape)
    tr = _pick(r, max(V7X_LANES, ADAMW_BLOCK_ELEMS // c // V7X_LANES * V7X_LANES), V7X_LANES if transposed else 8)
    nt = r // tr
    spec = pl.BlockSpec((tr, c), lambda l, i: ((first_layer + l) * nt + i, 0))
    c1 = 1.0 - ADAM_B1 ** ADAM_STEP
    c2 = 1.0 - ADAM_B2 ** ADAM_STEP

    def parts_spec(q):
        def at(l, i):
            return jnp.where(l == q, i, jnp.where(l > q, nt - 1, 0))

        if transposed:
            return pl.BlockSpec((N_DEV, c, tr), lambda l, i: (0, 0, at(l, i)))
        return pl.BlockSpec((N_DEV, tr, c), lambda l, i: (0, at(l, i), 0))

    earlier = list(earlier or ())

    def body(*refs):
        p_refs = refs[:n_layers]
        w_ref, m_ref, v_ref = refs[n_layers:n_layers + 3]
        g_ref, d_ref, nm_ref, nv_ref = refs[n_layers + 3 + len(earlier):]

        def update(p_ref):
            g = p_ref[0].astype(F32)
            for k in range(1, N_DEV):
                g = g + p_ref[k].astype(F32)
            if transposed:
                g = g.T
            nm = ADAM_B1 * m_ref[...] + (1.0 - ADAM_B1) * g
            nv = ADAM_B2 * v_ref[...] + (1.0 - ADAM_B2) * (g * g)
            g_ref[...] = g
            nm_ref[...] = nm
            nv_ref[...] = nv
            d_ref[...] = -ADAM_LR * ((nm / c1) / (jnp.sqrt(nv / c2) + ADAM_EPS) + ADAM_WD * w_ref[...])

        for q in range(n_layers):
            pl.when(pl.program_id(0) == q)(lambda q=q: update(p_refs[q]))

    out = jax.ShapeDtypeStruct(w.shape, F32)
    return pl.pallas_call(
        body, name=name, grid=(n_layers, nt),
        in_specs=[parts_spec(q) for q in range(n_layers)] + [spec] * 3 + [ANY_SPEC] * len(earlier),
        out_specs=[spec] * 4, out_shape=[out] * 4, compiler_params=_params(2),
        input_output_aliases={n_layers + 3 + q: q for q in range(len(earlier))},
    )(*parts, w, m, v, *earlier)


def _place():
    x, y, c = lax.axis_index("x"), lax.axis_index("y"), lax.axis_index("c")
    return x, y, c


def _index(px, py, pc):
    return 4 * px + 2 * py + pc


def _all_gather(name, shards):
    n = len(shards)

    def body(*refs):
        in_refs, out_refs = refs[:n], refs[n:2 * n]
        send_sems, recv_sems, local_sems = refs[2 * n:]
        x, y, c = _place()
        me, sibling = (x, y, c), (x, y, 1 - c)
        chips = [(1 - x, y), (x, 1 - y), (1 - x, 1 - y)]

        def slot(k, block):
            return out_refs[k].at[_index(*block)]

        def copy(k, j, block, to, src=None):
            return pltpu.make_async_remote_copy(
                src_ref=slot(k, block) if src is None else src, dst_ref=slot(k, block),
                send_sem=send_sems.at[7 * k + j], recv_sem=recv_sems.at[7 * k + j], device_id=to, device_id_type=MESH)

        mine = [pltpu.make_async_copy(in_refs[k], slot(k, me), local_sems.at[k]) for k in range(n)]
        for cp in mine:
            cp.start()
        first = []
        for k in range(n):
            first.append(copy(k, 0, me, sibling, src=in_refs[k]))
            first += [copy(k, 1 + j, me, (*chip, c), src=in_refs[k]) for j, chip in enumerate(chips)]
        for cp in first:
            cp.start()
        passed = []
        for j, chip in enumerate(chips):
            for k in range(n):
                copy(k, 1 + j, (*chip, c), me).wait_recv()
                cp = copy(k, 4 + j, (*chip, c), sibling)
                cp.start()
                passed.append(cp)
        for k in range(n):
            copy(k, 0, sibling, me).wait_recv()
            for j, chip in enumerate(chips):
                copy(k, 4 + j, (*chip, 1 - c), me).wait_recv()
        for cp in first + passed:
            cp.wait_send()
        for cp in mine:
            cp.wait()

    return pl.pallas_call(
        body, name=name, in_specs=[HBM_SPEC] * n, out_specs=[HBM_SPEC] * n,
        out_shape=[jax.ShapeDtypeStruct((N_DEV,) + s.shape, s.dtype) for s in shards],
        scratch_shapes=[pltpu.SemaphoreType.DMA((7 * n,)), pltpu.SemaphoreType.DMA((7 * n,)),
                        pltpu.SemaphoreType.DMA((n,))],
    )(*shards)


WHOLE, BLOCK, COLUMNS = "whole", "block", "columns"


def _slot(ref, index, mode):
    if mode == WHOLE:
        return ref
    if mode == BLOCK:
        return ref.at[index]
    width = ref.shape[1] // N_DEV
    return ref.at[:, pl.ds(pl.multiple_of(index * width, width), width)]


DIRECT = tuple(range(1, N_DEV))
CHIPS = (1, 4, 2, 6)
FORWARD = "forward"


def _copies(plan, x, y, c):
    def xor(r, flip_core=False):
        rx, ry, rc = (r >> 2) & 1, (r >> 1) & 1, (r & 1) ^ int(flip_core)
        return (1 - x if rx else x, 1 - y if ry else y, 1 - c if rc else c)

    me = _index(x, y, c)
    if plan == FORWARD:
        return [(xor(1), _index(*xor(r)), _index(*xor(r)), _index(*xor(r, True))) for r in (4, 2, 6)]
    return [(xor(r), _index(*xor(r)), me, _index(*xor(r))) for r in plan]


def _exchange_start(name, srcs, lands, src_modes, land_modes, deps=(), plan=DIRECT):
    n, n_src = len(lands), len(srcs)
    n_copies = 3 if plan == FORWARD else len(plan)

    def body(*refs):
        land_refs = refs[n_src:n_src + n]
        src_refs = refs[:n_src] if n_src else land_refs
        send_sems, recv_sems = refs[n_src + n + len(deps)], refs[n_src + n + len(deps) + 1]
        token = refs[-1]
        x, y, c = _place()
        for k in range(n):
            for j, (peer, src_index, dst_index, _) in enumerate(_copies(plan, x, y, c)):
                pltpu.make_async_remote_copy(
                    src_ref=_slot(src_refs[k], src_index, src_modes[k]),
                    dst_ref=_slot(land_refs[k], dst_index, land_modes[k]), send_sem=send_sems.at[n_copies * k + j],
                    recv_sem=recv_sems.at[n_copies * k + j], device_id=peer, device_id_type=MESH).start()
        token[...] = jnp.zeros_like(token)

    arrays = list(srcs) + list(lands)
    thru = [pltpu.HBM(a.shape, a.dtype) for a in arrays]
    out = pl.pallas_call(
        body, name=name,
        out_shape=(pltpu.SemaphoreType.DMA((n_copies * n,)), pltpu.SemaphoreType.DMA((n_copies * n,)), *thru,
                   jax.ShapeDtypeStruct((8, V7X_LANES), F32)),
        in_specs=[HBM_SPEC] * len(arrays) + [ANY_SPEC] * len(deps),
        out_specs=(SEM_SPEC, SEM_SPEC, *([HBM_SPEC] * len(arrays)), pl.BlockSpec(memory_space=pltpu.VMEM)),
        input_output_aliases={k: 2 + k for k in range(len(arrays))},
        compiler_params=pltpu.CompilerParams(has_side_effects=pltpu.SideEffectType.DATAFLOW_SIDE_EFFECTING),
    )(*[pltpu.with_memory_space_constraint(a, pltpu.HBM) for a in arrays], *deps)
    return out[0], out[1], list(out[2:2 + n_src]), list(out[2 + n_src:2 + n_src + n]), out[-1]


def _exchange_wait(name, send_sems, recv_sems, srcs, lands, src_modes, land_modes, after, plan=DIRECT):
    n, n_src = len(lands), len(srcs)
    n_copies = 3 if plan == FORWARD else len(plan)

    def body(*refs):
        land_refs = refs[n_src:n_src + n]
        src_refs = refs[:n_src] if n_src else land_refs
        send_sems, recv_sems = refs[n_src + n], refs[n_src + n + 1]
        x, y, c = _place()
        for k in range(n):
            for j, (peer, src_index, _, arrival_index) in enumerate(_copies(plan, x, y, c)):
                cp = pltpu.make_async_remote_copy(
                    src_ref=_slot(src_refs[k], src_index, src_modes[k]),
                    dst_ref=_slot(land_refs[k], arrival_index, land_modes[k]), send_sem=send_sems.at[n_copies * k + j],
                    recv_sem=recv_sems.at[n_copies * k + j], device_id=peer, device_id_type=MESH)
                cp.wait_send()
                cp.wait_recv()

    arrays = list(srcs) + list(lands)
    thru = [pltpu.HBM(a.shape, a.dtype) for a in arrays]
    out = pl.pallas_call(
        body, name=name, out_shape=tuple(thru),
        in_specs=[HBM_SPEC] * len(arrays) + [SEM_SPEC, SEM_SPEC, ANY_SPEC], out_specs=tuple([HBM_SPEC] * len(arrays)),
        input_output_aliases={k: k for k in range(len(arrays))},
        compiler_params=pltpu.CompilerParams(has_side_effects=pltpu.SideEffectType.DATAFLOW_SIDE_EFFECTING),
    )(*arrays, send_sems, recv_sems, after)
    return list(out[n_src:])


def _landing(shard, me, mode=BLOCK):
    if mode == COLUMNS:
        k, n = shard.shape
        return lax.dynamic_update_slice(lax.empty((k, N_DEV * n), shard.dtype), shard, (0, me * n))
    zone = lax.empty((N_DEV,) + shard.shape, shard.dtype)
    return lax.dynamic_update_slice(zone, shard[None], (me,) + (0,) * shard.ndim)


def _cols_whole(w):
    return jnp.transpose(w, (1, 0, 2)).reshape(w.shape[1], N_DEV * w.shape[2])


def _rows_whole(w):
    return w.reshape(N_DEV * w.shape[1], w.shape[2])


def _cols_parts(dw):
    k, n8 = dw.shape
    return jnp.transpose(dw.reshape(k, N_DEV, n8 // N_DEV), (1, 0, 2))


def _rows_parts(dw):
    r8, c = dw.shape
    return dw.reshape(N_DEV, r8 // N_DEV, c)


def _pack_rows(arrays):
    rows = []
    for a in arrays:
        flat = a.reshape(-1).astype(F32)
        flat = jnp.pad(flat, [(0, (-flat.shape[0]) % PACK_TILE)])
        rows.append(flat.reshape(-1, V7X_LANES))
    return jnp.concatenate(rows, axis=0)


def _unpack_rows(packed, shapes):
    out, row = [], 0
    for s in shapes:
        size = math.prod(s)
        n_rows = -(-size // PACK_TILE) * 8
        out.append(packed[row:row + n_rows].reshape(-1)[:size].reshape(s))
        row += n_rows
    return out


def _merge2d(a):
    return a.reshape(-1, a.shape[-1])


def _ffn_fwd(tag, x, g, w_up_t, w_down, deps=()):
    f = w_down.shape[0]
    gu, act, n = _mm(f"{tag}_up", x, w_up_t, tb=True, prologue=_rms_rows, prologue_pars=[g], deps=deps,
                     epilogue=lambda acc: _swiglu_rows(acc, f), outs=[(2 * f, BF16), (f, BF16)])
    out = _mm(f"{tag}_down", act, w_down, res=x, scale=0.5, out_dtype=F32)
    return out, (x, n, gu, act)


def _ffn_bwd(tag, dres, saved, g, w_up_t, w_down, deps=(), after_dw=None):
    x, n, gu, act = saved
    dres32, dres16 = dres
    f = w_down.shape[0]
    dgu = _mm(f"{tag}_down_dx", dres16, w_down, tb=True, scale=0.5, deps=deps, epilogue=_swiglu_bwd_rows,
              row_ins=[(gu, 0, f), (gu, 1, f)], outs=[(2 * f, BF16)])[0]
    d_down = _mm(f"{tag}_down_dw", act, dres16, ta=True, scale=0.5)
    d_up_t = _mm(f"{tag}_up_dw", dgu, n, ta=True)
    dx, dg = _mm_rms_bwd(f"{tag}_up_dx", dgu, w_up_t, x, g, dres32, tb=False,
                         deps=after_dw(d_up_t, d_down) if after_dw else ())
    return dx, dg, d_up_t, d_down


def _conv_mixer_fwd(tag, x, g, w_in, w_conv, w_out, n_seq, seq):
    cbv, h = _mm(f"{tag}_in", x, w_in, prologue=_rms_rows, prologue_pars=[g])
    z = _conv_fwd(f"{tag}_conv", cbv, w_conv, n_seq, seq)
    out = _mm(f"{tag}_out", z, w_out, res=x, out_dtype=F32)
    return out, (x, h, cbv, z)


def _conv_mixer_bwd(tag, dres, saved, g, w_in, w_conv, w_out, n_seq, seq):
    x, h, cbv, z = saved
    dres32, dres16 = dres
    dz = _mm(f"{tag}_out_dx", dres16, w_out, tb=True)
    d_out = _mm(f"{tag}_out_dw", z, dres16, ta=True)
    dc, db, dv, d_conv = _conv_bwd(f"{tag}_conv_bwd", dz, cbv, w_conv, n_seq, seq)
    dcbv = jnp.concatenate([dc, db, dv], axis=1)
    d_in = _mm(f"{tag}_in_dw", h, dcbv, ta=True)
    dx, dg = _mm_rms_bwd(f"{tag}_in_dx", dcbv, w_in, x, g, dres32)
    return dx, dg, d_in, d_conv, d_out


def _s5_mixer_fwd(tag, x, g, ssm, dskip, w_glu, n_seq, seq):
    a_re, a_im, log_dt, b_re, b_im, c_re, c_im = ssm
    groups, p, hh = b_re.shape
    gb = S5_CHANNELS // hh
    disc, disc_vjp = jax.vjp(_s5_discretize, a_re, a_im, log_dt, b_re, b_im)
    abar_re, abar_im, bbar_re, bbar_im = disc
    mats = (_block_diag_in(bbar_re, gb).astype(BF16), _block_diag_in(bbar_im, gb).astype(BF16),
            _block_diag_out(c_re, gb).astype(BF16), _block_diag_out(c_im, gb).astype(BF16),
            abar_re.reshape(1, groups * p), abar_im.reshape(1, groups * p), dskip)
    d = x.shape[1]
    h = _rms_fwd(f"{tag}_norm", x, g)
    s_re, s_im, ypre, z = _s5_fwd(f"{tag}_scan", h, *mats, n_seq, seq)
    vg, out = _mm(f"{tag}_glu", z, w_glu, epilogue=lambda vg, x: (vg, x + vg[:, :d] * _sigmoid(vg[:, d:])),
                  row_ins=[x], outs=[(2 * d, BF16), (d, F32)])
    return out, (x, h, s_re, s_im, ypre, z, vg, mats, disc_vjp, (groups, p, hh, gb))


def _s5_mixer_bwd(tag, dres, saved, g, w_glu, n_seq, seq):
    x, h, s_re, s_im, ypre, z, vg, mats, disc_vjp, (groups, p, hh, gb) = saved
    d = x.shape[1]
    dres32, _ = dres
    dvg = _glu_bwd(f"{tag}_glu_act_bwd", dres32, vg, d)
    d_glu = _mm(f"{tag}_glu_dw", z, dvg, ta=True)
    dz = _mm(f"{tag}_glu_dx", dvg, w_glu, tb=True)
    dh, dbin_re, dbin_im, dcout_re, dcout_im, dabar_re, dabar_im, d_skip = _s5_bwd(
        f"{tag}_scan_bwd", dz, ypre, h, s_re, s_im, *mats, n_seq, seq)
    d_are, d_aim, d_logdt, d_bre, d_bim = disc_vjp((
        dabar_re.reshape(groups, p), dabar_im.reshape(groups, p),
        _block_diag_in_t(dbin_re, gb, p, hh), _block_diag_in_t(dbin_im, gb, p, hh)))
    d_cre = _block_diag_out_t(dcout_re, gb, p, hh)
    d_cim = _block_diag_out_t(dcout_im, gb, p, hh)
    dx, dg = _rms_bwd(f"{tag}_norm_bwd", x, g, dh, dres32)
    return dx, dg, (d_are, d_aim, d_logdt, d_bre, d_bim, d_cre, d_cim), d_skip, d_glu


def _xattn_fwd(tag, x, mem, g_q, g_mem, w_q, w_kv, w_o, n_seq, seq, mlen, heads):
    q, n = _mm(f"{tag}_q", x, w_q, prologue=_rms_rows, prologue_pars=[g_q])
    mem_n = _rms_fwd(f"{tag}_mem_norm", mem, g_mem)
    kv = _mm(f"{tag}_kv", mem_n, w_kv)
    o = _attn_fwd(f"{tag}_attn", q, kv, n_seq, seq, mlen, heads)
    out = _mm(f"{tag}_o", o, w_o, res=x, out_dtype=F32)
    return out, (x, n, q, mem_n, kv, o)


def _xattn_bwd(tag, dres, saved, mem, g_q, g_mem, w_q, w_kv, w_o, n_seq, seq, mlen, heads):
    x, n, q, mem_n, kv, o = saved
    dres32, dres16 = dres
    do = _mm(f"{tag}_o_dx", dres16, w_o, tb=True)
    d_o = _mm(f"{tag}_o_dw", o, dres16, ta=True)
    dq, dk, dv = _attn_bwd(f"{tag}_attn_bwd", q, kv, do, n_seq, seq, mlen, heads)
    dkv = jnp.concatenate([dk, dv], axis=1)
    d_q = _mm(f"{tag}_q_dw", n, dq, ta=True)
    d_kv = _mm(f"{tag}_kv_dw", mem_n, dkv, ta=True)
    dmem_n = _mm(f"{tag}_kv_dx", dkv, w_kv, tb=True)
    _, dg_mem = _rms_bwd(f"{tag}_mem_norm_bwd", mem, g_mem, dmem_n)
    dx, dg_q = _mm_rms_bwd(f"{tag}_q_dx", dq, w_q, x, g_q, dres32)
    return dx, dg_q, dg_mem, d_q, d_kv, d_o


WEIGHT_NAMES = ("norm_g", "final_g", "ffn1_up", "ffn1_down", "ffn2_up", "ffn2_down", "conv_w_in", "conv_w",
                "conv_w_out", "ssm_a_re", "ssm_a_im", "ssm_log_dt", "ssm_b_re", "ssm_b_im", "ssm_c_re", "ssm_c_im",
                "ssm_d", "ssm_w_glu", "xa_w_q", "xa_w_kv", "xa_w_o")
MATRICES = ("ffn1_up", "ffn1_down", "ffn2_up", "ffn2_down", "conv_w_in", "conv_w_out", "ssm_w_glu", "xa_w_q",
            "xa_w_kv", "xa_w_o")
COLUMN_SHARDED = ("conv_w_in", "ssm_w_glu", "xa_w_kv")
TRANSPOSED = ("ffn1_up", "ffn2_up")
SMALL_SHARDED = ("norm_g", "conv_w", "ssm_d")
REPLICATED = ("ssm_a_re", "ssm_a_im", "ssm_log_dt", "ssm_b_re", "ssm_b_im", "ssm_c_re", "ssm_c_im", "final_g")


def kernel(x, mem, norm_g, final_g, ffn1_up, ffn1_down, ffn2_up, ffn2_down, conv_w_in, conv_w, conv_w_out, ssm_a_re, ssm_a_im, ssm_log_dt, ssm_b_re, ssm_b_im, ssm_c_re, ssm_c_im, ssm_d, ssm_w_glu, xa_w_q, xa_w_kv, xa_w_o, loss_target, m_norm_g, m_final_g, m_ffn1_up, m_ffn1_down, m_ffn2_up, m_ffn2_down, m_conv_w_in, m_conv_w, m_conv_w_out, m_ssm_a_re, m_ssm_a_im, m_ssm_log_dt, m_ssm_b_re, m_ssm_b_im, m_ssm_c_re, m_ssm_c_im, m_ssm_d, m_ssm_w_glu, m_xa_w_q, m_xa_w_kv, m_xa_w_o, v_norm_g, v_final_g, v_ffn1_up, v_ffn1_down, v_ffn2_up, v_ffn2_down, v_conv_w_in, v_conv_w, v_conv_w_out, v_ssm_a_re, v_ssm_a_im, v_ssm_log_dt, v_ssm_b_re, v_ssm_b_im, v_ssm_c_re, v_ssm_c_im, v_ssm_d, v_ssm_w_glu, v_xa_w_q, v_xa_w_kv, v_xa_w_o):
    w = dict(norm_g=norm_g, final_g=final_g, ffn1_up=ffn1_up, ffn1_down=ffn1_down, ffn2_up=ffn2_up,
             ffn2_down=ffn2_down, conv_w_in=conv_w_in, conv_w=conv_w, conv_w_out=conv_w_out, ssm_a_re=ssm_a_re,
             ssm_a_im=ssm_a_im, ssm_log_dt=ssm_log_dt, ssm_b_re=ssm_b_re, ssm_b_im=ssm_b_im, ssm_c_re=ssm_c_re,
             ssm_c_im=ssm_c_im, ssm_d=ssm_d, ssm_w_glu=ssm_w_glu, xa_w_q=xa_w_q, xa_w_kv=xa_w_kv, xa_w_o=xa_w_o)
    mom = dict(norm_g=m_norm_g, final_g=m_final_g, ffn1_up=m_ffn1_up, ffn1_down=m_ffn1_down, ffn2_up=m_ffn2_up,
               ffn2_down=m_ffn2_down, conv_w_in=m_conv_w_in, conv_w=m_conv_w, conv_w_out=m_conv_w_out,
               ssm_a_re=m_ssm_a_re, ssm_a_im=m_ssm_a_im, ssm_log_dt=m_ssm_log_dt, ssm_b_re=m_ssm_b_re,
               ssm_b_im=m_ssm_b_im, ssm_c_re=m_ssm_c_re, ssm_c_im=m_ssm_c_im, ssm_d=m_ssm_d, ssm_w_glu=m_ssm_w_glu,
               xa_w_q=m_xa_w_q, xa_w_kv=m_xa_w_kv, xa_w_o=m_xa_w_o)
    var = dict(norm_g=v_norm_g, final_g=v_final_g, ffn1_up=v_ffn1_up, ffn1_down=v_ffn1_down, ffn2_up=v_ffn2_up,
               ffn2_down=v_ffn2_down, conv_w_in=v_conv_w_in, conv_w=v_conv_w, conv_w_out=v_conv_w_out,
               ssm_a_re=v_ssm_a_re, ssm_a_im=v_ssm_a_im, ssm_log_dt=v_ssm_log_dt, ssm_b_re=v_ssm_b_re,
               ssm_b_im=v_ssm_b_im, ssm_c_re=v_ssm_c_re, ssm_c_im=v_ssm_c_im, ssm_d=v_ssm_d, ssm_w_glu=v_ssm_w_glu,
               xa_w_q=v_xa_w_q, xa_w_kv=v_xa_w_kv, xa_w_o=v_xa_w_o)

    n_seq, seq, d = x.shape
    mlen = mem.shape[1]
    depth, n_norms = norm_g.shape[0], norm_g.shape[1]
    heads = 4
    tokens = n_seq * seq
    x2 = x.reshape(tokens, d)
    mem2 = mem.reshape(n_seq * mlen, d)
    tgt2 = loss_target.reshape(tokens, d)

    small_shapes = [w[k].shape for k in SMALL_SHARDED]
    small_rows = [_merge2d(w[k]) for k in SMALL_SHARDED]
    small_counts = [s.shape[0] for s in small_rows]
    small = jnp.concatenate(small_rows, axis=0)
    small = jnp.pad(small, [(0, (-small.shape[0]) % 8), (0, 0)])
    me = _index(*_place())

    def layer_weights(i):
        names = [(k, i) for k in ("ffn1_up", "ffn1_down", "ffn2_up", "ffn2_down", "xa_w_q", "xa_w_kv", "xa_w_o")]
        return names + ([("conv_w_in", i // 2), ("conv_w_out", i // 2)] if i % 2 == 0 else [("ssm_w_glu", i // 2)])

    shards = [[(w[k][idx].T if k in TRANSPOSED else w[k][idx]).astype(BF16) for k, idx in layer_weights(i)]
              for i in range(depth)]
    n_first = 2
    gathered = _all_gather("gather_layer0_ffn1", shards[0][:n_first] + [small])
    small_all = gathered[-1]
    in_flight = [None] * depth
    token = gathered[0]
    in_place = {k for k in COLUMN_SHARDED if w[k].shape[-1] % V7X_LANES == 0}

    def modes(i):
        start = n_first if i == 0 else 0
        return [COLUMNS if k in in_place else BLOCK for k, _ in layer_weights(i)][start:]

    for i in range(depth):
        mine = shards[i][n_first:] if i == 0 else shards[i]
        zones = [_landing(s, me, mode) for s, mode in zip(mine, modes(i))]
        *in_flight[i], token = _exchange_start(f"gather_start_l{i}", mine, zones, [WHOLE] * len(zones), modes(i),
                                               deps=[token], plan=CHIPS)
    passing = [None] * depth

    def pass_on(i, after):
        landed = _exchange_wait(f"gather_wait_l{i}", *in_flight[i], [WHOLE] * len(modes(i)), modes(i), after,
                                plan=CHIPS)
        *passing[i], forward_token = _exchange_start(f"forward_start_l{i}", [], landed, modes(i), modes(i),
                                                     plan=FORWARD)
        return forward_token

    def passed(i, after):
        return _exchange_wait(f"forward_wait_l{i}", *passing[i], modes(i), modes(i), after, plan=FORWARD)

    def small_whole(idx):
        start = sum(small_counts[:idx])
        part = small_all[:, start:start + small_counts[idx]]
        lead = small_shapes[idx][:-1]
        part = part.reshape((N_DEV,) + lead + (part.shape[-1],))
        part = jnp.moveaxis(part, 0, -2)
        return part.reshape(lead + (N_DEV * part.shape[-1],))

    norm_all = small_whole(0)
    conv_all = small_whole(1)
    dskip_all = small_whole(2)

    def whole(names, arrived):
        return {k: blk if k in in_place else _cols_whole(blk) if k in COLUMN_SHARDED else _rows_whole(blk)
                for (k, _), blk in zip(names, arrived)}

    saved = []
    cur = x2
    for i in range(depth):
        g = [norm_all[i, k].reshape(1, d) for k in range(n_norms)]
        j = i // 2
        if i == 0:
            lw = whole(layer_weights(0)[:n_first], gathered[:n_first])
        else:
            lw = whole(layer_weights(i), passed(i, cur))
        cur, s_ffn1 = _ffn_fwd(f"l{i}_ffn1", cur, g[0], lw["ffn1_up"], lw["ffn1_down"], [token] if i == 0 else ())
        if i == 0:
            lw.update(whole(layer_weights(0)[n_first:], passed(0, pass_on(0, cur))))
        if i % 2 == 0:
            lw["conv_w"] = conv_all[j]
            cur, s_mix = _conv_mixer_fwd(f"l{i}_conv", cur, g[1], lw["conv_w_in"], lw["conv_w"], lw["conv_w_out"],
                                         n_seq, seq)
        else:
            ssm = tuple(w[k][j] for k in ("ssm_a_re", "ssm_a_im", "ssm_log_dt", "ssm_b_re", "ssm_b_im",
                                          "ssm_c_re", "ssm_c_im"))
            cur, s_mix = _s5_mixer_fwd(f"l{i}_s5", cur, g[1], ssm, dskip_all[j].reshape(1, d), lw["ssm_w_glu"],
                                       n_seq, seq)
        cur, s_xa = _xattn_fwd(f"l{i}_xa", cur, mem2, g[2], g[3], lw["xa_w_q"], lw["xa_w_kv"], lw["xa_w_o"],
                               n_seq, seq, mlen, heads)
        early_pass = 0 < i < depth - 1
        deps = [pass_on(i + 1, cur)] if early_pass else ()
        cur, s_ffn2 = _ffn_fwd(f"l{i}_ffn2", cur, g[4], lw["ffn2_up"], lw["ffn2_down"], deps)
        if i == 0 and depth > 1:
            pass_on(1, cur)
        saved.append((g, lw, s_ffn1, s_mix, s_xa, s_ffn2))

    dres, err2, d_final = _final_loss("loss_head", cur, final_g.reshape(1, d), tgt2)
    loss = lax.psum(0.5 * jnp.sum(err2) / d, ("x", "y", "c"))

    d_norm = [[None] * n_norms for _ in range(depth)]
    d_conv = [None] * conv_w.shape[0]
    d_skip = [None] * ssm_d.shape[0]
    d_ssm = [None] * ssm_a_re.shape[0]
    leaving = [None] * depth
    deps = ()

    def leave(name, keys, gm, extra=()):
        srcs, src_modes, zones = [], [], []
        for k in keys:
            if k in in_place:
                rows, n = gm[k].shape[0], gm[k].shape[1] // N_DEV
                srcs.append(gm[k])
                src_modes.append(COLUMNS)
                zones.append(_landing(lax.dynamic_slice(gm[k], (0, me * n), (rows, n)), me))
            else:
                srcs.append(_cols_parts(gm[k]) if k in COLUMN_SHARDED else _rows_parts(gm[k]))
                src_modes.append(BLOCK)
                zones.append(_landing(lax.dynamic_index_in_dim(srcs[-1], me, 0, keepdims=False), me))
        for p in extra:
            srcs.append(p)
            src_modes.append(BLOCK)
            zones.append(_landing(lax.dynamic_index_in_dim(p, me, 0, keepdims=False), me))
        land_modes = [BLOCK] * len(srcs)
        *handles, token = _exchange_start(name, srcs, zones, src_modes, land_modes)
        return (*handles, src_modes, land_modes), token

    def small_parts(full):
        lead = full.shape[:-1]
        t = full.reshape(lead + (N_DEV, full.shape[-1] // N_DEV))
        t = jnp.moveaxis(t, -2, 0)
        return t.reshape(N_DEV, -1, t.shape[-1])

    for i in reversed(range(depth)):
        g, lw, s_ffn1, s_mix, s_xa, s_ffn2 = saved[i]
        j = i // 2
        gm = {}
        dres, d_norm[i][4], gm["ffn2_up"], gm["ffn2_down"] = _ffn_bwd(
            f"l{i}_ffn2", dres, s_ffn2, g[4], lw["ffn2_up"], lw["ffn2_down"], deps)
        dres, d_norm[i][2], d_norm[i][3], gm["xa_w_q"], gm["xa_w_kv"], gm["xa_w_o"] = _xattn_bwd(
            f"l{i}_xa", dres, s_xa, mem2, g[2], g[3], lw["xa_w_q"], lw["xa_w_kv"], lw["xa_w_o"], n_seq, seq, mlen,
            heads)
        if i % 2 == 0:
            dres, d_norm[i][1], gm["conv_w_in"], d_conv[j], gm["conv_w_out"] = _conv_mixer_bwd(
                f"l{i}_conv", dres, s_mix, g[1], lw["conv_w_in"], lw["conv_w"], lw["conv_w_out"], n_seq, seq)
        else:
            dres, d_norm[i][1], d_ssm[j], d_skip[j], gm["ssm_w_glu"] = _s5_mixer_bwd(
                f"l{i}_s5", dres, s_mix, g[1], lw["ssm_w_glu"], n_seq, seq)
        upper, token = leave(f"grads_start_l{i}_upper", [k for k, _ in layer_weights(i)[2:]], gm)
        lower = []

        def send_lower(d_up_t, d_down, i=i, lower=lower):
            handles, token = leave(f"grads_start_l{i}_lower", ["ffn1_up", "ffn1_down"],
                                   {"ffn1_up": d_up_t, "ffn1_down": d_down})
            lower.append(handles)
            return [token]

        dres, d_norm[i][0], _, _ = _ffn_bwd(
            f"l{i}_ffn1", dres, s_ffn1, g[0], lw["ffn1_up"], lw["ffn1_down"], [token], after_dw=send_lower)
        leaving[i] = (upper, lower[0])
        deps = ()
        if i == min(1, depth - 1):
            rep_grads = [jnp.stack([d_ssm[j][k] for j in range(len(d_ssm))]) for k in range(7)]
            rep_packed = _pack_rows(rep_grads + [d_final.reshape(-1)])
            *rep_leaving, token = _exchange_start("replicated_grads_start", [rep_packed], [_landing(rep_packed, me)],
                                                  [WHOLE], [BLOCK])
            deps = [token]
    grad_x = dres[0].reshape(n_seq, seq, d)
    d_norm_all = jnp.stack([jnp.concatenate(row, axis=0) for row in d_norm])
    small_g = jnp.concatenate(
        [small_parts(a) for a in (d_norm_all, jnp.stack(d_conv), jnp.concatenate(d_skip, axis=0))], axis=1)
    small_g = jnp.pad(small_g, [(0, 0), (0, (-small_g.shape[1]) % 8), (0, 0)])
    small_leaving, _ = leave("small_grads_start", [], {}, [small_g])
    received = {k: [None] * w[k].shape[0] for k in MATRICES}

    def arrive(i, part, after):
        names = layer_weights(i)[2:] if part == 0 else layer_weights(i)[:2]
        blks = _exchange_wait(f"grads_wait_l{i}_{'upper' if part == 0 else 'lower'}", *leaving[i][part], after)
        for (k, idx), blk in zip(names, blks):
            received[k][idx] = blk
        return [k for k, _ in names]

    for i in range(1, depth):
        arrive(i, 0, dres[0])
        arrive(i, 1, dres[0])
    rep_all = _exchange_wait("replicated_grads_wait", *rep_leaving, [WHOLE], [BLOCK], dres[0])[0]
    rep_shapes = [w[k].shape for k in REPLICATED]
    flat = {k: (_merge2d(w[k]), _merge2d(mom[k]), _merge2d(var[k])) for k in MATRICES}
    late = {k: received[k][0] is None for k in MATRICES}
    early = {}
    for k in MATRICES:
        first = 1 if late[k] else 0
        if first < len(received[k]):
            early[k] = _adamw(f"adamw_{k}_upper", received[k][first:], *flat[k], first_layer=first,
                              transposed=k in TRANSPOSED)
    after = list(early.values())[-1][0] if early else dres[0]
    done = dict(early)
    for part in (0, 1):
        for k in arrive(0, part, after):
            done[k] = _adamw(f"adamw_{k}_l0", received[k][:1], *flat[k], earlier=early.get(k),
                             transposed=k in TRANSPOSED)
            after = done[k][0]
    small_received, = _exchange_wait("small_grads_wait", *small_leaving, after)
    grads, deltas, new_m, new_v = {}, {}, {}, {}
    for k in MATRICES:
        grads[k], deltas[k], new_m[k], new_v[k] = [o.reshape(w[k].shape) for o in done[k]]

    def small_local(src):
        rows = jnp.concatenate([_merge2d(src[k]) for k in SMALL_SHARDED], axis=0)
        return jnp.pad(rows, [(0, (-rows.shape[0]) % 8), (0, 0)])

    out = _adamw("adamw_small", [small_received], small, small_local(mom), small_local(var))
    for res, o in zip((grads, deltas, new_m, new_v), out):
        start = 0
        for k, cnt, shape in zip(SMALL_SHARDED, small_counts, small_shapes):
            res[k] = o[start:start + cnt].reshape(shape)
            start += cnt

    out = _adamw("adamw_replicated", [rep_all], _pack_rows([w[k] for k in REPLICATED]),
                 _pack_rows([mom[k] for k in REPLICATED]), _pack_rows([var[k] for k in REPLICATED]))
    for res, o in zip((grads, deltas, new_m, new_v), out):
        for k, a in zip(REPLICATED, _unpack_rows(o, rep_shapes)):
            res[k] = a

    return (loss, grad_x, *[grads[k] for k in WEIGHT_NAMES], *[deltas[k] for k in WEIGHT_NAMES],
            *[new_m[k] for k in WEIGHT_NAMES], *[new_v[k] for k in WEIGHT_NAMES])
```

```python
import math

import jax
import jax.numpy as jnp
from jax import lax
from jax.experimental import pallas as pl
from jax.experimental.pallas import tpu as pltpu

F32 = jnp.float32
BF16 = jnp.bfloat16
MESH = pl.DeviceIdType.MESH
N_DEV = 8

NORM_EPS = 1e-6
EIG_CLIP = -1e-4
CONV_WIDTH = 3
ADAM_LR = 0.001
ADAM_B1 = 0.9
ADAM_B2 = 0.999
ADAM_EPS = 1e-08
ADAM_WD = 0.01
ADAM_STEP = 10
GELU_C = math.sqrt(2.0 / math.pi)
GELU_A = 0.044715

V7X_LANES = 128
V7X_VMEM_LIMIT = 56 * 1024 * 1024
S5_CHANNELS = 128
PACK_TILE = 8 * V7X_LANES

HBM_SPEC = pl.BlockSpec(memory_space=pltpu.HBM)
ANY_SPEC = pl.BlockSpec(memory_space=pl.ANY)
SEM_SPEC = pl.BlockSpec(memory_space=pltpu.SEMAPHORE)


def _params(n_grid):
    return pltpu.CompilerParams(dimension_semantics=("arbitrary",) * n_grid, vmem_limit_bytes=V7X_VMEM_LIMIT)


def _pick(n, pref, align):
    if n <= pref:
        return n
    t = (pref // align) * align
    while t >= align:
        if n % t == 0:
            return t
        t -= align
    raise ValueError(f"no tile for {n} (pref {pref}, align {align})")


MM_RHS_BLOCK_BYTES = 12 * 1024 * 1024
MM_LHS_BLOCK_BYTES = 6 * 1024 * 1024
MM_ACC_BYTES = 6 * 1024 * 1024
MM_ROWS = 512


MM_EPILOGUE_ACC_BYTES = 3 * 1024 * 1024
MM_EPILOGUE_MIN_ROWS = 256


def _mm_tiles(m, k, n, a_item, b_item, ta, max_rows):
    tn = _pick(n, max(V7X_LANES, MM_RHS_BLOCK_BYTES // (k * b_item)), V7X_LANES)
    rows = min(max_rows, MM_ACC_BYTES // (4 * tn), MM_LHS_BLOCK_BYTES // (k * a_item))
    align = V7X_LANES if ta else 16
    tm = _pick(m, max(align, rows), align)
    return tm, tn


def _store_results(out_refs, n_row, results, first):
    if not isinstance(results, (tuple, list)):
        results = (results,)
    for o, v in zip(out_refs[:n_row], results[:n_row]):
        if isinstance(v, (tuple, list)):
            off = 0
            for piece in v:
                w = piece.shape[1]
                o[:, off:off + w] = piece.astype(o.dtype)
                off += w
        else:
            o[...] = v.astype(o.dtype)
    if len(out_refs) > n_row:
        @pl.when(first)
        def _():
            for o in out_refs[n_row:]:
                o[...] = jnp.zeros_like(o)

        for o, v in zip(out_refs[n_row:], results[n_row:]):
            o[...] += v


def _mm(name, a, b, *, ta=False, tb=False, out_dtype=BF16, res=None, scale=None, deps=(),
        epilogue=None, row_ins=(), par_ins=(), outs=(), acc_outs=(), prologue=None, prologue_pars=()):
    if ta:
        k, m = a.shape
    else:
        m, k = a.shape
    if tb:
        n, k2 = b.shape
    else:
        k2, n = b.shape
    assert k == k2, (name, a.shape, b.shape)
    max_rows = MM_ROWS if epilogue is None else max(MM_EPILOGUE_MIN_ROWS, MM_EPILOGUE_ACC_BYTES // (4 * n))
    tm, tn = _mm_tiles(m, k, n, a.dtype.itemsize, b.dtype.itemsize, ta, max_rows)
    a_spec = pl.BlockSpec((k, tm), lambda j, i: (0, i)) if ta else pl.BlockSpec((tm, k), lambda j, i: (i, 0))
    b_spec = pl.BlockSpec((tn, k), lambda j, i: (j, 0)) if tb else pl.BlockSpec((k, tn), lambda j, i: (0, j))
    o_spec = pl.BlockSpec((tm, tn), lambda j, i: (i, j))
    dims = (((0 if ta else 1,), (1 if tb else 0,)), ((), ()))
    has_res = res is not None
    ins = [a, b] + ([res] if has_res else [])
    specs = [a_spec, b_spec] + ([o_spec] if has_res else [])
    n_mm = len(ins)
    if epilogue is None:
        out_specs, out_shape = [o_spec], [jax.ShapeDtypeStruct((m, n), out_dtype)]
    else:
        assert tn == n, (name, tn, n)
        for r in row_ins:
            arr, cb, cw = r if isinstance(r, tuple) else (r, 0, r.shape[1])
            assert arr.shape[0] == m, (name, arr.shape, m)
            ins.append(arr)
            specs.append(pl.BlockSpec((tm, cw), lambda j, i, cb=cb: (i, cb)))
        for p in par_ins:
            ins.append(p)
            specs.append(pl.BlockSpec(p.shape, lambda j, i: (0, 0)))
        out_specs = [pl.BlockSpec((tm, c), lambda j, i: (i, 0)) for c, _ in outs]
        out_specs += [pl.BlockSpec((r, c), lambda j, i: (0, 0)) for r, c in acc_outs]
        out_shape = [jax.ShapeDtypeStruct((m, c), dt) for c, dt in outs]
        out_shape += [jax.ShapeDtypeStruct((r, c), F32) for r, c in acc_outs]
    n_in = len(ins)
    if prologue is not None:
        assert tn == n and not ta, (name, tn, n, ta)
        for p in prologue_pars:
            ins.append(p)
            specs.append(pl.BlockSpec(p.shape, lambda j, i: (0, 0)))
        out_specs = out_specs + [a_spec]
        out_shape = out_shape + [jax.ShapeDtypeStruct((m, k), BF16)]
    n_pro = len(ins)
    ins += list(deps)
    specs += [ANY_SPEC] * len(deps)

    def body(*refs):
        a_ref, b_ref = refs[0], refs[1]
        out_refs = refs[n_pro + len(deps):]
        if prologue is None:
            lhs = a_ref[...].astype(BF16)
        else:
            lhs = prologue(a_ref[...], *[r[...] for r in refs[n_in:n_pro]]).astype(BF16)
            out_refs[-1][...] = lhs
            out_refs = out_refs[:-1]
        acc = lax.dot_general(lhs, b_ref[...].astype(BF16), dims, preferred_element_type=F32)
        if scale is not None:
            acc = acc * scale
        if has_res:
            acc = acc + refs[2][...].astype(F32)
        if epilogue is None:
            out_refs[0][...] = acc.astype(out_refs[0].dtype)
        else:
            extra = [r[...] for r in refs[n_mm:n_in]]
            _store_results(out_refs, len(outs), epilogue(acc, *extra), pl.program_id(1) == 0)

    out = pl.pallas_call(
        body, name=name, grid=(n // tn, m // tm), in_specs=specs, out_specs=out_specs, out_shape=out_shape,
        compiler_params=_params(2),
    )(*ins)
    return out[0] if epilogue is None and prologue is None else out


def _rowwise(name, fn, rows, row_ins, par_ins, row_outs, acc_outs=(), tm_pref=256):
    tm = _pick(rows, tm_pref, 16)
    in_specs, ins = [], []
    for r in row_ins:
        arr, cb, cw = r if isinstance(r, tuple) else (r, 0, r.shape[1])
        assert arr.shape[0] == rows, (name, arr.shape, rows)
        ins.append(arr)
        in_specs.append(pl.BlockSpec((tm, cw), lambda i, cb=cb: (i, cb)))
    for p in par_ins:
        ins.append(p)
        in_specs.append(pl.BlockSpec(p.shape, lambda i: (0, 0)))
    out_specs = [pl.BlockSpec((tm, c), lambda i: (i, 0)) for c, _ in row_outs]
    out_specs += [pl.BlockSpec((r, c), lambda i: (0, 0)) for r, c in acc_outs]
    out_shape = [jax.ShapeDtypeStruct((rows, c), dt) for c, dt in row_outs]
    out_shape += [jax.ShapeDtypeStruct((r, c), F32) for r, c in acc_outs]
    n_in, n_row = len(ins), len(row_outs)

    def body(*refs):
        vals = [r[...] for r in refs[:n_in]]
        _store_results(refs[n_in:], n_row, fn(*vals), pl.program_id(0) == 0)

    return pl.pallas_call(
        body, name=name, grid=(rows // tm,), in_specs=in_specs, out_specs=out_specs, out_shape=out_shape,
        compiler_params=_params(1),
    )(*ins)


def _inv_rms(x):
    return lax.rsqrt(jnp.mean(x * x, axis=-1, keepdims=True) + NORM_EPS)


def _rms_rows(x, g):
    return x * _inv_rms(x) * g


def _rms_fwd(name, x, g):
    return _rowwise(name, _rms_rows, x.shape[0], [x], [g], [(x.shape[1], BF16)], tm_pref=512)[0]


def _rms_bwd_rows(dn, x, dres, g):
    r = _inv_rms(x)
    xh = x * r
    dg = jnp.sum(dn * xh, axis=0, keepdims=True)
    dxh = dn * g
    dx = r * (dxh - xh * jnp.mean(dxh * xh, axis=-1, keepdims=True)) + dres
    return dx, dx, dg


def _rms_bwd(name, x, g, dn, dres=None):
    d = x.shape[1]
    if dres is None:
        def fn(x, dn, g):
            return (jnp.sum(dn.astype(F32) * (x * _inv_rms(x)), axis=0, keepdims=True),)

        return None, _rowwise(name, fn, x.shape[0], [x, dn], [g], [], [(1, d)])[0]

    def fn(x, dn, dres, g):
        return _rms_bwd_rows(dn.astype(F32), x, dres, g)

    out = _rowwise(name, fn, x.shape[0], [x, dn, dres], [g], [(d, F32), (d, BF16)], [(1, d)])
    return (out[0], out[1]), out[2]


def _mm_rms_bwd(name, dy, w, x, g, dres, tb=True, deps=()):
    d = x.shape[1]
    out = _mm(name, dy, w, tb=tb, deps=deps, epilogue=_rms_bwd_rows, row_ins=[x, dres], par_ins=[g],
              outs=[(d, F32), (d, BF16)], acc_outs=[(1, d)])
    return (out[0], out[1]), out[2]


def _sigmoid(x):
    return 0.5 + 0.5 * jnp.tanh(0.5 * x)


def _swiglu_rows(gu, f):
    gt = gu[:, :f]
    return gu, gt * _sigmoid(gt) * gu[:, f:]


def _swiglu_bwd_rows(dact, gt, up):
    gt, up = gt.astype(F32), up.astype(F32)
    sg = _sigmoid(gt)
    return ((dact * up * (sg * (1.0 + gt * (1.0 - sg))), dact * (gt * sg)),)


def _glu_bwd(name, dres, vg, d):
    def fn(dres, val, gate):
        val, gate = val.astype(F32), gate.astype(F32)
        sg = _sigmoid(gate)
        return ((dres * sg, dres * val * sg * (1.0 - sg)),)

    return _rowwise(name, fn, dres.shape[0], [dres, (vg, 0, d), (vg, 1, d)], [], [(2 * d, BF16)])[0]


def _final_loss(name, x, g, tgt):
    d = x.shape[1]

    def fn(x, tgt, g):
        r = _inv_rms(x)
        xh = x * r
        err = xh * g - tgt
        dy = err * (1.0 / d)
        dxh = dy * g
        dx = r * (dxh - xh * jnp.mean(dxh * xh, axis=-1, keepdims=True))
        return dx, dx, jnp.sum(err * err, axis=0, keepdims=True), jnp.sum(dy * xh, axis=0, keepdims=True)

    dx, dx16, err2, dg = _rowwise(name, fn, x.shape[0], [x, tgt], [g], [(d, F32), (d, BF16)], [(1, d), (1, d)])
    return (dx, dx16), err2, dg


def _shift_down(u, k):
    rows = lax.broadcasted_iota(jnp.int32, u.shape, 0)
    return jnp.where(rows >= k, pltpu.roll(u, k, 0), 0.0)


def _shift_up(u, k):
    n = u.shape[0]
    rows = lax.broadcasted_iota(jnp.int32, u.shape, 0)
    return jnp.where(rows < n - k, pltpu.roll(u, n - k, 0), 0.0)


def _conv_specs(seq, cw, n_cb, swap):
    def at(off):
        if swap:
            return pl.BlockSpec((seq, cw), lambda j, b: (b, off * n_cb + j))
        return pl.BlockSpec((seq, cw), lambda b, j: (b, off * n_cb + j))

    return at


def _conv_fwd(name, cbv, w, n_seq, seq):
    d = w.shape[1]
    cw = _pick(d, 256, V7X_LANES)
    n_cb = d // cw
    at = _conv_specs(seq, cw, n_cb, swap=False)

    def body(c_ref, b_ref, v_ref, w_ref, z_ref):
        u = c_ref[...].astype(F32) * v_ref[...].astype(F32)
        cv = w_ref[0:1, :] * _shift_down(u, 2) + w_ref[1:2, :] * _shift_down(u, 1) + w_ref[2:3, :] * u
        z_ref[...] = (b_ref[...].astype(F32) * cv).astype(z_ref.dtype)

    return pl.pallas_call(
        body, name=name, grid=(n_seq, n_cb),
        in_specs=[at(0), at(1), at(2), pl.BlockSpec((CONV_WIDTH, cw), lambda b, j: (0, j))],
        out_specs=at(0), out_shape=jax.ShapeDtypeStruct((n_seq * seq, d), BF16), compiler_params=_params(2),
    )(cbv, cbv, cbv, w)


def _conv_bwd(name, dz, cbv, w, n_seq, seq):
    d = w.shape[1]
    cw = _pick(d, 256, V7X_LANES)
    n_cb = d // cw
    at = _conv_specs(seq, cw, n_cb, swap=True)

    def body(dz_ref, c_ref, b_ref, v_ref, w_ref, dc_ref, db_ref, dv_ref, dw_ref):
        c, b, v = c_ref[...].astype(F32), b_ref[...].astype(F32), v_ref[...].astype(F32)
        dz = dz_ref[...].astype(F32)
        w0, w1, w2 = w_ref[0:1, :], w_ref[1:2, :], w_ref[2:3, :]
        u = c * v
        u1, u2 = _shift_down(u, 1), _shift_down(u, 2)
        cv = w0 * u2 + w1 * u1 + w2 * u
        db_ref[...] = (dz * cv).astype(db_ref.dtype)
        dcv = dz * b
        du = w2 * dcv + w1 * _shift_up(dcv, 1) + w0 * _shift_up(dcv, 2)
        dc_ref[...] = (du * v).astype(dc_ref.dtype)
        dv_ref[...] = (du * c).astype(dv_ref.dtype)

        @pl.when(pl.program_id(1) == 0)
        def _():
            dw_ref[...] = jnp.zeros_like(dw_ref)

        dw_ref[0:1, :] += jnp.sum(dcv * u2, axis=0, keepdims=True)
        dw_ref[1:2, :] += jnp.sum(dcv * u1, axis=0, keepdims=True)
        dw_ref[2:3, :] += jnp.sum(dcv * u, axis=0, keepdims=True)

    act = jax.ShapeDtypeStruct((n_seq * seq, d), BF16)
    return pl.pallas_call(
        body, name=name, grid=(n_cb, n_seq),
        in_specs=[at(0), at(0), at(1), at(2), pl.BlockSpec((CONV_WIDTH, cw), lambda j, b: (0, j))],
        out_specs=[at(0), at(0), at(0), pl.BlockSpec((CONV_WIDTH, cw), lambda j, b: (0, j))],
        out_shape=[act, act, act, jax.ShapeDtypeStruct((CONV_WIDTH, d), F32)], compiler_params=_params(2),
    )(dz, cbv, cbv, cbv, w)


def _s5_discretize(a_re, a_im, log_dt, b_re, b_im):
    lam_re = jnp.minimum(a_re, EIG_CLIP)
    lam_im = a_im
    dt = jnp.exp(log_dt)[:, None]
    mag = jnp.exp(lam_re * dt)
    abar_re = mag * jnp.cos(lam_im * dt)
    abar_im = mag * jnp.sin(lam_im * dt)
    den = lam_re * lam_re + lam_im * lam_im
    num_re = abar_re - 1.0
    num_im = abar_im
    coef_re = (num_re * lam_re + num_im * lam_im) / den
    coef_im = (num_im * lam_re - num_re * lam_im) / den
    bbar_re = coef_re[..., None] * b_re - coef_im[..., None] * b_im
    bbar_im = coef_re[..., None] * b_im + coef_im[..., None] * b_re
    return abar_re, abar_im, bbar_re, bbar_im


def _block_diag_in(bbar, gb):
    g, p, h = bbar.shape
    t = jnp.transpose(bbar.reshape(g // gb, gb, p, h), (0, 1, 3, 2))
    return jnp.einsum("cghp,gk->cghkp", t, jnp.eye(gb, dtype=bbar.dtype)).reshape(g // gb, gb * h, gb * p)


def _block_diag_in_t(blk, gb, p, h):
    nb = blk.shape[0]
    t = jnp.einsum("cghkp,gk->cghp", blk.reshape(nb, gb, h, gb, p), jnp.eye(gb, dtype=blk.dtype))
    return jnp.transpose(t, (0, 1, 3, 2)).reshape(nb * gb, p, h)


def _block_diag_out(c, gb):
    g, h, p = c.shape
    t = jnp.transpose(c.reshape(g // gb, gb, h, p), (0, 1, 3, 2))
    return jnp.einsum("cgph,gk->cgpkh", t, jnp.eye(gb, dtype=c.dtype)).reshape(g // gb, gb * p, gb * h)


def _block_diag_out_t(blk, gb, p, h):
    nb = blk.shape[0]
    t = jnp.einsum("cgpkh,gk->cgph", blk.reshape(nb, gb, p, gb, h), jnp.eye(gb, dtype=blk.dtype))
    return jnp.transpose(t, (0, 1, 3, 2)).reshape(nb * gb, h, p)


def _gelu(y):
    return 0.5 * y * (1.0 + jnp.tanh(GELU_C * (y + GELU_A * y * y * y)))


def _gelu_grad(y):
    th = jnp.tanh(GELU_C * (y + GELU_A * y * y * y))
    return 0.5 * (1.0 + th) + 0.5 * y * (1.0 - th * th) * GELU_C * (1.0 + 3.0 * GELU_A * y * y)


def _dot(a, b, ca, cb):
    return lax.dot_general(a.astype(BF16), b.astype(BF16), (((ca,), (cb,)), ((), ())), preferred_element_type=F32)


def _s5_specs(seq, ch, sb):
    act = pl.BlockSpec((seq, ch), lambda j, b: (b, j))
    state = pl.BlockSpec((seq, sb), lambda j, b: (b, j))
    w_in = pl.BlockSpec((None, ch, sb), lambda j, b: (j, 0, 0))
    w_out = pl.BlockSpec((None, sb, ch), lambda j, b: (j, 0, 0))
    lane_s = pl.BlockSpec((1, sb), lambda j, b: (0, j))
    lane_c = pl.BlockSpec((1, ch), lambda j, b: (0, j))
    return act, state, w_in, w_out, lane_s, lane_c


def _s5_fwd(name, h, bin_re, bin_im, cout_re, cout_im, abar_re, abar_im, dskip, n_seq, seq):
    t, d = h.shape
    nb, ch, sb = bin_re.shape
    act, state, w_in, w_out, lane_s, lane_c = _s5_specs(seq, ch, sb)

    def body(h_ref, bre_ref, bim_ref, cre_ref, cim_ref, ar_ref, ai_ref, d_ref, sre_ref, sim_ref, y_ref, z_ref):
        u = h_ref[...]
        sre_ref[...] = _dot(u, bre_ref[...], 1, 0)
        sim_ref[...] = _dot(u, bim_ref[...], 1, 0)
        ar, ai = ar_ref[...], ai_ref[...]

        def step(i, carry):
            sr, si = carry
            row = pl.ds(i, 1)
            nr = ar * sr - ai * si + sre_ref[row, :]
            ni = ar * si + ai * sr + sim_ref[row, :]
            sre_ref[row, :] = nr
            sim_ref[row, :] = ni
            return nr, ni

        zero = jnp.zeros((1, sb), F32)
        lax.fori_loop(0, seq, step, (zero, zero), unroll=8)
        y = _dot(sre_ref[...], cre_ref[...], 1, 0) - _dot(sim_ref[...], cim_ref[...], 1, 0)
        y = y + d_ref[...] * u.astype(F32)
        y_ref[...] = y
        z_ref[...] = _gelu(y).astype(z_ref.dtype)

    return pl.pallas_call(
        body, name=name, grid=(nb, n_seq),
        in_specs=[act, w_in, w_in, w_out, w_out, lane_s, lane_s, lane_c],
        out_specs=[state, state, act, act],
        out_shape=[jax.ShapeDtypeStruct((t, nb * sb), F32), jax.ShapeDtypeStruct((t, nb * sb), F32),
                   jax.ShapeDtypeStruct((t, d), F32), jax.ShapeDtypeStruct((t, d), BF16)],
        compiler_params=_params(2),
    )(h, bin_re, bin_im, cout_re, cout_im, abar_re, abar_im, dskip)


def _s5_bwd(name, dz, ypre, h, s_re, s_im, bin_re, bin_im, cout_re, cout_im, abar_re, abar_im, dskip, n_seq, seq):
    t, d = h.shape
    nb, ch, sb = bin_re.shape
    act, state, w_in, w_out, lane_s, lane_c = _s5_specs(seq, ch, sb)

    def body(dz_ref, y_ref, h_ref, sre_ref, sim_ref, bre_ref, bim_ref, cre_ref, cim_ref, ar_ref, ai_ref, d_ref,
             dh_ref, dbre_ref, dbim_ref, dcre_ref, dcim_ref, dar_ref, dai_ref, dd_ref, gre, gim):
        first = pl.program_id(1) == 0
        u = h_ref[...].astype(F32)
        dy = dz_ref[...].astype(F32) * _gelu_grad(y_ref[...])
        gre[...] = _dot(dy, cre_ref[...], 1, 1)
        gim[...] = -_dot(dy, cim_ref[...], 1, 1)
        ar, ai = ar_ref[...], ai_ref[...]

        def step(i, carry):
            gr, gi = carry
            row = pl.ds(seq - 1 - i, 1)
            nr = gre[row, :] + ar * gr + ai * gi
            ni = gim[row, :] - ai * gr + ar * gi
            gre[row, :] = nr
            gim[row, :] = ni
            return nr, ni

        zero = jnp.zeros((1, sb), F32)
        lax.fori_loop(0, seq, step, (zero, zero), unroll=8)

        g_re, g_im = gre[...], gim[...]
        s_re, s_im = sre_ref[...], sim_ref[...]
        p_re, p_im = _shift_down(s_re, 1), _shift_down(s_im, 1)
        dar = jnp.sum(g_re * p_re + g_im * p_im, axis=0, keepdims=True)
        dai = jnp.sum(g_im * p_re - g_re * p_im, axis=0, keepdims=True)
        dbre = _dot(u, g_re, 0, 0)
        dbim = _dot(u, g_im, 0, 0)
        dcre = _dot(s_re, dy, 0, 0)
        dcim = -_dot(s_im, dy, 0, 0)
        ddd = jnp.sum(dy * u, axis=0, keepdims=True)
        dh_ref[...] = _dot(g_re, bre_ref[...], 1, 1) + _dot(g_im, bim_ref[...], 1, 1) + d_ref[...] * dy

        @pl.when(first)
        def _():
            dar_ref[...] = dar
            dai_ref[...] = dai
            dbre_ref[...] = dbre
            dbim_ref[...] = dbim
            dcre_ref[...] = dcre
            dcim_ref[...] = dcim
            dd_ref[...] = ddd

        @pl.when(jnp.logical_not(first))
        def _():
            dar_ref[...] += dar
            dai_ref[...] += dai
            dbre_ref[...] += dbre
            dbim_ref[...] += dbim
            dcre_ref[...] += dcre
            dcim_ref[...] += dcim
            dd_ref[...] += ddd

    return pl.pallas_call(
        body, name=name, grid=(nb, n_seq),
        in_specs=[act, act, act, state, state, w_in, w_in, w_out, w_out, lane_s, lane_s, lane_c],
        out_specs=[act, w_in, w_in, w_out, w_out, lane_s, lane_s, lane_c],
        out_shape=[jax.ShapeDtypeStruct((t, d), F32),
                   jax.ShapeDtypeStruct((nb, ch, sb), F32), jax.ShapeDtypeStruct((nb, ch, sb), F32),
                   jax.ShapeDtypeStruct((nb, sb, ch), F32), jax.ShapeDtypeStruct((nb, sb, ch), F32),
                   jax.ShapeDtypeStruct((1, nb * sb), F32), jax.ShapeDtypeStruct((1, nb * sb), F32),
                   jax.ShapeDtypeStruct((1, d), F32)],
        scratch_shapes=[pltpu.VMEM((seq, sb), F32), pltpu.VMEM((seq, sb), F32)],
        compiler_params=_params(2),
    )(dz, ypre, h, s_re, s_im, bin_re, bin_im, cout_re, cout_im, abar_re, abar_im, dskip)


ATTN_QUERY_ROWS = 1024


def _softmax_rows(q, k, scale):
    s = _dot(q, k, 1, 1) * scale
    e = jnp.exp(s - jnp.max(s, axis=-1, keepdims=True))
    return e * (1.0 / jnp.sum(e, axis=-1, keepdims=True))


def _attn_fwd(name, q, kv, n_seq, seq, mlen, heads):
    t, d = q.shape
    hd = d // heads
    tq = _pick(seq, ATTN_QUERY_ROWS, 16)
    nq = seq // tq
    scale = hd ** -0.5
    q_spec = pl.BlockSpec((tq, d), lambda b, i: (b * nq + i, 0))

    def body(q_ref, k_ref, v_ref, o_ref):
        for h in range(heads):
            cols = slice(h * hd, (h + 1) * hd)
            p = _softmax_rows(q_ref[:, cols], k_ref[:, cols], scale)
            o_ref[:, cols] = _dot(p, v_ref[:, cols], 1, 0).astype(o_ref.dtype)

    return pl.pallas_call(
        body, name=name, grid=(n_seq, nq),
        in_specs=[q_spec, pl.BlockSpec((mlen, d), lambda b, i: (b, 0)), pl.BlockSpec((mlen, d), lambda b, i: (b, 1))],
        out_specs=q_spec, out_shape=jax.ShapeDtypeStruct((t, d), BF16), compiler_params=_params(2),
    )(q, kv, kv)


def _attn_bwd(name, q, kv, do, n_seq, seq, mlen, heads):
    t, d = q.shape
    hd = d // heads
    tq = _pick(seq, ATTN_QUERY_ROWS, 16)
    nq = seq // tq
    scale = hd ** -0.5
    q_spec = pl.BlockSpec((tq, d), lambda b, i: (b * nq + i, 0))
    k_spec = pl.BlockSpec((mlen, d), lambda b, i: (b, 0))

    def body(q_ref, k_ref, v_ref, do_ref, dq_ref, dk_ref, dv_ref):
        @pl.when(pl.program_id(1) == 0)
        def _():
            dk_ref[...] = jnp.zeros_like(dk_ref)
            dv_ref[...] = jnp.zeros_like(dv_ref)

        for h in range(heads):
            cols = slice(h * hd, (h + 1) * hd)
            q, k, v, do = q_ref[:, cols], k_ref[:, cols], v_ref[:, cols], do_ref[:, cols]
            p = _softmax_rows(q, k, scale)
            dp = _dot(do, v, 1, 1)
            ds = p * (dp - jnp.sum(dp * p, axis=-1, keepdims=True)) * scale
            dq_ref[:, cols] = _dot(ds, k, 1, 0).astype(dq_ref.dtype)
            dk_ref[:, cols] += _dot(ds, q, 0, 0)
            dv_ref[:, cols] += _dot(p, do, 0, 0)

    return pl.pallas_call(
        body, name=name, grid=(n_seq, nq),
        in_specs=[q_spec, k_spec, pl.BlockSpec((mlen, d), lambda b, i: (b, 1)), q_spec],
        out_specs=[q_spec, k_spec, k_spec],
        out_shape=[jax.ShapeDtypeStruct((t, d), BF16), jax.ShapeDtypeStruct((n_seq * mlen, d), F32),
                   jax.ShapeDtypeStruct((n_seq * mlen, d), F32)],
        compiler_params=_params(2),
    )(q, kv, kv, do)


ADAMW_BLOCK_ELEMS = 128 * 1024


def _adamw(name, parts, w, m, v, first_layer=0, earlier=None, transposed=False):
    n_layers = len(parts)
    r, c = parts[0].shape[1:][::-1] if transposed else parts[0].shape[1:]
    assert w.shape[0] % r == 0 and w.shape[1] == c and first_layer + n_layers <= w.shape[0] // r, (name, w.shape)
    tr = _pick(r, max(V7X_LANES, ADAMW_BLOCK_ELEMS // c // V7X_LANES * V7X_LANES), V7X_LANES if transposed else 8)
    nt = r // tr
    spec = pl.BlockSpec((tr, c), lambda l, i: ((first_layer + l) * nt + i, 0))
    c1 = 1.0 - ADAM_B1 ** ADAM_STEP
    c2 = 1.0 - ADAM_B2 ** ADAM_STEP

    def parts_spec(q):
        def at(l, i):
            return jnp.where(l == q, i, jnp.where(l > q, nt - 1, 0))

        if transposed:
            return pl.BlockSpec((N_DEV, c, tr), lambda l, i: (0, 0, at(l, i)))
        return pl.BlockSpec((N_DEV, tr, c), lambda l, i: (0, at(l, i), 0))

    earlier = list(earlier or ())

    def body(*refs):
        p_refs = refs[:n_layers]
        w_ref, m_ref, v_ref = refs[n_layers:n_layers + 3]
        g_ref, d_ref, nm_ref, nv_ref = refs[n_layers + 3 + len(earlier):]

        def update(p_ref):
            g = p_ref[0].astype(F32)
            for k in range(1, N_DEV):
                g = g + p_ref[k].astype(F32)
            if transposed:
                g = g.T
            nm = ADAM_B1 * m_ref[...] + (1.0 - ADAM_B1) * g
            nv = ADAM_B2 * v_ref[...] + (1.0 - ADAM_B2) * (g * g)
            g_ref[...] = g
            nm_ref[...] = nm
            nv_ref[...] = nv
            d_ref[...] = -ADAM_LR * ((nm / c1) / (jnp.sqrt(nv / c2) + ADAM_EPS) + ADAM_WD * w_ref[...])

        for q in range(n_layers):
            pl.when(pl.program_id(0) == q)(lambda q=q: update(p_refs[q]))

    out = jax.ShapeDtypeStruct(w.shape, F32)
    return pl.pallas_call(
        body, name=name, grid=(n_layers, nt),
        in_specs=[parts_spec(q) for q in range(n_layers)] + [spec] * 3 + [ANY_SPEC] * len(earlier),
        out_specs=[spec] * 4, out_shape=[out] * 4, compiler_params=_params(2),
        input_output_aliases={n_layers + 3 + q: q for q in range(len(earlier))},
    )(*parts, w, m, v, *earlier)


def _place():
    x, y, c = lax.axis_index("x"), lax.axis_index("y"), lax.axis_index("c")
    return x, y, c


def _index(px, py, pc):
    return 4 * px + 2 * py + pc


def _all_gather(name, shards):
    n = len(shards)

    def body(*refs):
        in_refs, out_refs = refs[:n], refs[n:2 * n]
        send_sems, recv_sems, local_sems = refs[2 * n:]
        x, y, c = _place()
        me, sibling = (x, y, c), (x, y, 1 - c)
        chips = [(1 - x, y), (x, 1 - y), (1 - x, 1 - y)]

        def slot(k, block):
            return out_refs[k].at[_index(*block)]

        def copy(k, j, block, to, src=None):
            return pltpu.make_async_remote_copy(
                src_ref=slot(k, block) if src is None else src, dst_ref=slot(k, block),
                send_sem=send_sems.at[7 * k + j], recv_sem=recv_sems.at[7 * k + j], device_id=to, device_id_type=MESH)

        mine = [pltpu.make_async_copy(in_refs[k], slot(k, me), local_sems.at[k]) for k in range(n)]
        for cp in mine:
            cp.start()
        first = []
        for k in range(n):
            first.append(copy(k, 0, me, sibling, src=in_refs[k]))
            first += [copy(k, 1 + j, me, (*chip, c), src=in_refs[k]) for j, chip in enumerate(chips)]
        for cp in first:
            cp.start()
        passed = []
        for j, chip in enumerate(chips):
            for k in range(n):
                copy(k, 1 + j, (*chip, c), me).wait_recv()
                cp = copy(k, 4 + j, (*chip, c), sibling)
                cp.start()
                passed.append(cp)
        for k in range(n):
            copy(k, 0, sibling, me).wait_recv()
            for j, chip in enumerate(chips):
                copy(k, 4 + j, (*chip, 1 - c), me).wait_recv()
        for cp in first + passed:
            cp.wait_send()
        for cp in mine:
            cp.wait()

    return pl.pallas_call(
        body, name=name, in_specs=[HBM_SPEC] * n, out_specs=[HBM_SPEC] * n,
        out_shape=[jax.ShapeDtypeStruct((N_DEV,) + s.shape, s.dtype) for s in shards],
        scratch_shapes=[pltpu.SemaphoreType.DMA((7 * n,)), pltpu.SemaphoreType.DMA((7 * n,)),
                        pltpu.SemaphoreType.DMA((n,))],
    )(*shards)


WHOLE, BLOCK, COLUMNS = "whole", "block", "columns"


def _slot(ref, index, mode):
    if mode == WHOLE:
        return ref
    if mode == BLOCK:
        return ref.at[index]
    width = ref.shape[1] // N_DEV
    return ref.at[:, pl.ds(pl.multiple_of(index * width, width), width)]


DIRECT = tuple(range(1, N_DEV))
CHIPS = (1, 4, 2, 6)
FORWARD = "forward"


def _copies(plan, x, y, c):
    def xor(r, flip_core=False):
        rx, ry, rc = (r >> 2) & 1, (r >> 1) & 1, (r & 1) ^ int(flip_core)
        return (1 - x if rx else x, 1 - y if ry else y, 1 - c if rc else c)

    me = _index(x, y, c)
    if plan == FORWARD:
        return [(xor(1), _index(*xor(r)), _index(*xor(r)), _index(*xor(r, True))) for r in (4, 2, 6)]
    return [(xor(r), _index(*xor(r)), me, _index(*xor(r))) for r in plan]


def _exchange_start(name, srcs, lands, src_modes, land_modes, deps=(), plan=DIRECT):
    n, n_src = len(lands), len(srcs)
    n_copies = 3 if plan == FORWARD else len(plan)

    def body(*refs):
        land_refs = refs[n_src:n_src + n]
        src_refs = refs[:n_src] if n_src else land_refs
        send_sems, recv_sems = refs[n_src + n + len(deps)], refs[n_src + n + len(deps) + 1]
        token = refs[-1]
        x, y, c = _place()
        for k in range(n):
            for j, (peer, src_index, dst_index, _) in enumerate(_copies(plan, x, y, c)):
                pltpu.make_async_remote_copy(
                    src_ref=_slot(src_refs[k], src_index, src_modes[k]),
                    dst_ref=_slot(land_refs[k], dst_index, land_modes[k]), send_sem=send_sems.at[n_copies * k + j],
                    recv_sem=recv_sems.at[n_copies * k + j], device_id=peer, device_id_type=MESH).start()
        token[...] = jnp.zeros_like(token)

    arrays = list(srcs) + list(lands)
    thru = [pltpu.HBM(a.shape, a.dtype) for a in arrays]
    out = pl.pallas_call(
        body, name=name,
        out_shape=(pltpu.SemaphoreType.DMA((n_copies * n,)), pltpu.SemaphoreType.DMA((n_copies * n,)), *thru,
                   jax.ShapeDtypeStruct((8, V7X_LANES), F32)),
        in_specs=[HBM_SPEC] * len(arrays) + [ANY_SPEC] * len(deps),
        out_specs=(SEM_SPEC, SEM_SPEC, *([HBM_SPEC] * len(arrays)), pl.BlockSpec(memory_space=pltpu.VMEM)),
        input_output_aliases={k: 2 + k for k in range(len(arrays))},
        compiler_params=pltpu.CompilerParams(has_side_effects=pltpu.SideEffectType.DATAFLOW_SIDE_EFFECTING),
    )(*[pltpu.with_memory_space_constraint(a, pltpu.HBM) for a in arrays], *deps)
    return out[0], out[1], list(out[2:2 + n_src]), list(out[2 + n_src:2 + n_src + n]), out[-1]


def _exchange_wait(name, send_sems, recv_sems, srcs, lands, src_modes, land_modes, after, plan=DIRECT):
    n, n_src = len(lands), len(srcs)
    n_copies = 3 if plan == FORWARD else len(plan)

    def body(*refs):
        land_refs = refs[n_src:n_src + n]
        src_refs = refs[:n_src] if n_src else land_refs
        send_sems, recv_sems = refs[n_src + n], refs[n_src + n + 1]
        x, y, c = _place()
        for k in range(n):
            for j, (peer, src_index, _, arrival_index) in enumerate(_copies(plan, x, y, c)):
                cp = pltpu.make_async_remote_copy(
                    src_ref=_slot(src_refs[k], src_index, src_modes[k]),
                    dst_ref=_slot(land_refs[k], arrival_index, land_modes[k]), send_sem=send_sems.at[n_copies * k + j],
                    recv_sem=recv_sems.at[n_copies * k + j], device_id=peer, device_id_type=MESH)
                cp.wait_send()
                cp.wait_recv()

    arrays = list(srcs) + list(lands)
    thru = [pltpu.HBM(a.shape, a.dtype) for a in arrays]
    out = pl.pallas_call(
        body, name=name, out_shape=tuple(thru),
        in_specs=[HBM_SPEC] * len(arrays) + [SEM_SPEC, SEM_SPEC, ANY_SPEC], out_specs=tuple([HBM_SPEC] * len(arrays)),
        input_output_aliases={k: k for k in range(len(arrays))},
        compiler_params=pltpu.CompilerParams(has_side_effects=pltpu.SideEffectType.DATAFLOW_SIDE_EFFECTING),
    )(*arrays, send_sems, recv_sems, after)
    return list(out[n_src:])


def _landing(shard, me, mode=BLOCK):
    if mode == COLUMNS:
        k, n = shard.shape
        return lax.dynamic_update_slice(lax.empty((k, N_DEV * n), shard.dtype), shard, (0, me * n))
    zone = lax.empty((N_DEV,) + shard.shape, shard.dtype)
    return lax.dynamic_update_slice(zone, shard[None], (me,) + (0,) * shard.ndim)


def _cols_whole(w):
    return jnp.transpose(w, (1, 0, 2)).reshape(w.shape[1], N_DEV * w.shape[2])


def _rows_whole(w):
    return w.reshape(N_DEV * w.shape[1], w.shape[2])


def _cols_parts(dw):
    k, n8 = dw.shape
    return jnp.transpose(dw.reshape(k, N_DEV, n8 // N_DEV), (1, 0, 2))


def _rows_parts(dw):
    r8, c = dw.shape
    return dw.reshape(N_DEV, r8 // N_DEV, c)


def _pack_rows(arrays):
    rows = []
    for a in arrays:
        flat = a.reshape(-1).astype(F32)
        flat = jnp.pad(flat, [(0, (-flat.shape[0]) % PACK_TILE)])
        rows.append(flat.reshape(-1, V7X_LANES))
    return jnp.concatenate(rows, axis=0)


def _unpack_rows(packed, shapes):
    out, row = [], 0
    for s in shapes:
        size = math.prod(s)
        n_rows = -(-size // PACK_TILE) * 8
        out.append(packed[row:row + n_rows].reshape(-1)[:size].reshape(s))
        row += n_rows
    return out


def _merge2d(a):
    return a.reshape(-1, a.shape[-1])


def _ffn_fwd(tag, x, g, w_up_t, w_down, deps=()):
    f = w_down.shape[0]
    gu, act, n = _mm(f"{tag}_up", x, w_up_t, tb=True, prologue=_rms_rows, prologue_pars=[g], deps=deps,
                     epilogue=lambda acc: _swiglu_rows(acc, f), outs=[(2 * f, BF16), (f, BF16)])
    out = _mm(f"{tag}_down", act, w_down, res=x, scale=0.5, out_dtype=F32)
    return out, (x, n, gu, act)


def _ffn_bwd(tag, dres, saved, g, w_up_t, w_down, deps=(), after_dw=None):
    x, n, gu, act = saved
    dres32, dres16 = dres
    f = w_down.shape[0]
    dgu = _mm(f"{tag}_down_dx", dres16, w_down, tb=True, scale=0.5, deps=deps, epilogue=_swiglu_bwd_rows,
              row_ins=[(gu, 0, f), (gu, 1, f)], outs=[(2 * f, BF16)])[0]
    d_down = _mm(f"{tag}_down_dw", act, dres16, ta=True, scale=0.5)
    d_up_t = _mm(f"{tag}_up_dw", dgu, n, ta=True)
    dx, dg = _mm_rms_bwd(f"{tag}_up_dx", dgu, w_up_t, x, g, dres32, tb=False,
                         deps=after_dw(d_up_t, d_down) if after_dw else ())
    return dx, dg, d_up_t, d_down


def _conv_mixer_fwd(tag, x, g, w_in, w_conv, w_out, n_seq, seq):
    cbv, h = _mm(f"{tag}_in", x, w_in, prologue=_rms_rows, prologue_pars=[g])
    z = _conv_fwd(f"{tag}_conv", cbv, w_conv, n_seq, seq)
    out = _mm(f"{tag}_out", z, w_out, res=x, out_dtype=F32)
    return out, (x, h, cbv, z)


def _conv_mixer_bwd(tag, dres, saved, g, w_in, w_conv, w_out, n_seq, seq):
    x, h, cbv, z = saved
    dres32, dres16 = dres
    dz = _mm(f"{tag}_out_dx", dres16, w_out, tb=True)
    d_out = _mm(f"{tag}_out_dw", z, dres16, ta=True)
    dc, db, dv, d_conv = _conv_bwd(f"{tag}_conv_bwd", dz, cbv, w_conv, n_seq, seq)
    dcbv = jnp.concatenate([dc, db, dv], axis=1)
    d_in = _mm(f"{tag}_in_dw", h, dcbv, ta=True)
    dx, dg = _mm_rms_bwd(f"{tag}_in_dx", dcbv, w_in, x, g, dres32)
    return dx, dg, d_in, d_conv, d_out


def _s5_mixer_fwd(tag, x, g, ssm, dskip, w_glu, n_seq, seq):
    a_re, a_im, log_dt, b_re, b_im, c_re, c_im = ssm
    groups, p, hh = b_re.shape
    gb = S5_CHANNELS // hh
    disc, disc_vjp = jax.vjp(_s5_discretize, a_re, a_im, log_dt, b_re, b_im)
    abar_re, abar_im, bbar_re, bbar_im = disc
    mats = (_block_diag_in(bbar_re, gb).astype(BF16), _block_diag_in(bbar_im, gb).astype(BF16),
            _block_diag_out(c_re, gb).astype(BF16), _block_diag_out(c_im, gb).astype(BF16),
            abar_re.reshape(1, groups * p), abar_im.reshape(1, groups * p), dskip)
    d = x.shape[1]
    h = _rms_fwd(f"{tag}_norm", x, g)
    s_re, s_im, ypre, z = _s5_fwd(f"{tag}_scan", h, *mats, n_seq, seq)
    vg, out = _mm(f"{tag}_glu", z, w_glu, epilogue=lambda vg, x: (vg, x + vg[:, :d] * _sigmoid(vg[:, d:])),
                  row_ins=[x], outs=[(2 * d, BF16), (d, F32)])
    return out, (x, h, s_re, s_im, ypre, z, vg, mats, disc_vjp, (groups, p, hh, gb))


def _s5_mixer_bwd(tag, dres, saved, g, w_glu, n_seq, seq):
    x, h, s_re, s_im, ypre, z, vg, mats, disc_vjp, (groups, p, hh, gb) = saved
    d = x.shape[1]
    dres32, _ = dres
    dvg = _glu_bwd(f"{tag}_glu_act_bwd", dres32, vg, d)
    d_glu = _mm(f"{tag}_glu_dw", z, dvg, ta=True)
    dz = _mm(f"{tag}_glu_dx", dvg, w_glu, tb=True)
    dh, dbin_re, dbin_im, dcout_re, dcout_im, dabar_re, dabar_im, d_skip = _s5_bwd(
        f"{tag}_scan_bwd", dz, ypre, h, s_re, s_im, *mats, n_seq, seq)
    d_are, d_aim, d_logdt, d_bre, d_bim = disc_vjp((
        dabar_re.reshape(groups, p), dabar_im.reshape(groups, p),
        _block_diag_in_t(dbin_re, gb, p, hh), _block_diag_in_t(dbin_im, gb, p, hh)))
    d_cre = _block_diag_out_t(dcout_re, gb, p, hh)
    d_cim = _block_diag_out_t(dcout_im, gb, p, hh)
    dx, dg = _rms_bwd(f"{tag}_norm_bwd", x, g, dh, dres32)
    return dx, dg, (d_are, d_aim, d_logdt, d_bre, d_bim, d_cre, d_cim), d_skip, d_glu


def _xattn_fwd(tag, x, mem, g_q, g_mem, w_q, w_kv, w_o, n_seq, seq, mlen, heads):
    q, n = _mm(f"{tag}_q", x, w_q, prologue=_rms_rows, prologue_pars=[g_q])
    mem_n = _rms_fwd(f"{tag}_mem_norm", mem, g_mem)
    kv = _mm(f"{tag}_kv", mem_n, w_kv)
    o = _attn_fwd(f"{tag}_attn", q, kv, n_seq, seq, mlen, heads)
    out = _mm(f"{tag}_o", o, w_o, res=x, out_dtype=F32)
    return out, (x, n, q, mem_n, kv, o)


def _xattn_bwd(tag, dres, saved, mem, g_q, g_mem, w_q, w_kv, w_o, n_seq, seq, mlen, heads):
    x, n, q, mem_n, kv, o = saved
    dres32, dres16 = dres
    do = _mm(f"{tag}_o_dx", dres16, w_o, tb=True)
    d_o = _mm(f"{tag}_o_dw", o, dres16, ta=True)
    dq, dk, dv = _attn_bwd(f"{tag}_attn_bwd", q, kv, do, n_seq, seq, mlen, heads)
    dkv = jnp.concatenate([dk, dv], axis=1)
    d_q = _mm(f"{tag}_q_dw", n, dq, ta=True)
    d_kv = _mm(f"{tag}_kv_dw", mem_n, dkv, ta=True)
    dmem_n = _mm(f"{tag}_kv_dx", dkv, w_kv, tb=True)
    _, dg_mem = _rms_bwd(f"{tag}_mem_norm_bwd", mem, g_mem, dmem_n)
    dx, dg_q = _mm_rms_bwd(f"{tag}_q_dx", dq, w_q, x, g_q, dres32)
    return dx, dg_q, dg_mem, d_q, d_kv, d_o


WEIGHT_NAMES = ("norm_g", "final_g", "ffn1_up", "ffn1_down", "ffn2_up", "ffn2_down", "conv_w_in", "conv_w",
                "conv_w_out", "ssm_a_re", "ssm_a_im", "ssm_log_dt", "ssm_b_re", "ssm_b_im", "ssm_c_re", "ssm_c_im",
                "ssm_d", "ssm_w_glu", "xa_w_q", "xa_w_kv", "xa_w_o")
MATRICES = ("ffn1_up", "ffn1_down", "ffn2_up", "ffn2_down", "conv_w_in", "conv_w_out", "ssm_w_glu", "xa_w_q",
            "xa_w_kv", "xa_w_o")
COLUMN_SHARDED = ("conv_w_in", "ssm_w_glu", "xa_w_kv")
TRANSPOSED = ("ffn1_up", "ffn2_up")
SMALL_SHARDED = ("norm_g", "conv_w", "ssm_d")
REPLICATED = ("ssm_a_re", "ssm_a_im", "ssm_log_dt", "ssm_b_re", "ssm_b_im", "ssm_c_re", "ssm_c_im", "final_g")


def kernel(x, mem, norm_g, final_g, ffn1_up, ffn1_down, ffn2_up, ffn2_down, conv_w_in, conv_w, conv_w_out, ssm_a_re, ssm_a_im, ssm_log_dt, ssm_b_re, ssm_b_im, ssm_c_re, ssm_c_im, ssm_d, ssm_w_glu, xa_w_q, xa_w_kv, xa_w_o, loss_target, m_norm_g, m_final_g, m_ffn1_up, m_ffn1_down, m_ffn2_up, m_ffn2_down, m_conv_w_in, m_conv_w, m_conv_w_out, m_ssm_a_re, m_ssm_a_im, m_ssm_log_dt, m_ssm_b_re, m_ssm_b_im, m_ssm_c_re, m_ssm_c_im, m_ssm_d, m_ssm_w_glu, m_xa_w_q, m_xa_w_kv, m_xa_w_o, v_norm_g, v_final_g, v_ffn1_up, v_ffn1_down, v_ffn2_up, v_ffn2_down, v_conv_w_in, v_conv_w, v_conv_w_out, v_ssm_a_re, v_ssm_a_im, v_ssm_log_dt, v_ssm_b_re, v_ssm_b_im, v_ssm_c_re, v_ssm_c_im, v_ssm_d, v_ssm_w_glu, v_xa_w_q, v_xa_w_kv, v_xa_w_o):
    w = dict(norm_g=norm_g, final_g=final_g, ffn1_up=ffn1_up, ffn1_down=ffn1_down, ffn2_up=ffn2_up,
             ffn2_down=ffn2_down, conv_w_in=conv_w_in, conv_w=conv_w, conv_w_out=conv_w_out, ssm_a_re=ssm_a_re,
             ssm_a_im=ssm_a_im, ssm_log_dt=ssm_log_dt, ssm_b_re=ssm_b_re, ssm_b_im=ssm_b_im, ssm_c_re=ssm_c_re,
             ssm_c_im=ssm_c_im, ssm_d=ssm_d, ssm_w_glu=ssm_w_glu, xa_w_q=xa_w_q, xa_w_kv=xa_w_kv, xa_w_o=xa_w_o)
    mom = dict(norm_g=m_norm_g, final_g=m_final_g, ffn1_up=m_ffn1_up, ffn1_down=m_ffn1_down, ffn2_up=m_ffn2_up,
               ffn2_down=m_ffn2_down, conv_w_in=m_conv_w_in, conv_w=m_conv_w, conv_w_out=m_conv_w_out,
               ssm_a_re=m_ssm_a_re, ssm_a_im=m_ssm_a_im, ssm_log_dt=m_ssm_log_dt, ssm_b_re=m_ssm_b_re,
               ssm_b_im=m_ssm_b_im, ssm_c_re=m_ssm_c_re, ssm_c_im=m_ssm_c_im, ssm_d=m_ssm_d, ssm_w_glu=m_ssm_w_glu,
               xa_w_q=m_xa_w_q, xa_w_kv=m_xa_w_kv, xa_w_o=m_xa_w_o)
    var = dict(norm_g=v_norm_g, final_g=v_final_g, ffn1_up=v_ffn1_up, ffn1_down=v_ffn1_down, ffn2_up=v_ffn2_up,
               ffn2_down=v_ffn2_down, conv_w_in=v_conv_w_in, conv_w=v_conv_w, conv_w_out=v_conv_w_out,
               ssm_a_re=v_ssm_a_re, ssm_a_im=v_ssm_a_im, ssm_log_dt=v_ssm_log_dt, ssm_b_re=v_ssm_b_re,
               ssm_b_im=v_ssm_b_im, ssm_c_re=v_ssm_c_re, ssm_c_im=v_ssm_c_im, ssm_d=v_ssm_d, ssm_w_glu=v_ssm_w_glu,
               xa_w_q=v_xa_w_q, xa_w_kv=v_xa_w_kv, xa_w_o=v_xa_w_o)

    n_seq, seq, d = x.shape
    mlen = mem.shape[1]
    depth, n_norms = norm_g.shape[0], norm_g.shape[1]
    heads = 4
    tokens = n_seq * seq
    x2 = x.reshape(tokens, d)
    mem2 = mem.reshape(n_seq * mlen, d)
    tgt2 = loss_target.reshape(tokens, d)

    small_shapes = [w[k].shape for k in SMALL_SHARDED]
    small_rows = [_merge2d(w[k]) for k in SMALL_SHARDED]
    small_counts = [s.shape[0] for s in small_rows]
    small = jnp.concatenate(small_rows, axis=0)
    small = jnp.pad(small, [(0, (-small.shape[0]) % 8), (0, 0)])
    me = _index(*_place())

    def layer_weights(i):
        names = [(k, i) for k in ("ffn1_up", "ffn1_down", "ffn2_up", "ffn2_down", "xa_w_q", "xa_w_kv", "xa_w_o")]
        return names + ([("conv_w_in", i // 2), ("conv_w_out", i // 2)] if i % 2 == 0 else [("ssm_w_glu", i // 2)])

    shards = [[(w[k][idx].T if k in TRANSPOSED else w[k][idx]).astype(BF16) for k, idx in layer_weights(i)]
              for i in range(depth)]
    n_first = 2
    gathered = _all_gather("gather_layer0_ffn1", shards[0][:n_first] + [small])
    small_all = gathered[-1]
    in_flight = [None] * depth
    token = gathered[0]
    in_place = {k for k in COLUMN_SHARDED if w[k].shape[-1] % V7X_LANES == 0}

    def modes(i):
        start = n_first if i == 0 else 0
        return [COLUMNS if k in in_place else BLOCK for k, _ in layer_weights(i)][start:]

    for i in range(depth):
        mine = shards[i][n_first:] if i == 0 else shards[i]
        zones = [_landing(s, me, mode) for s, mode in zip(mine, modes(i))]
        *in_flight[i], token = _exchange_start(f"gather_start_l{i}", mine, zones, [WHOLE] * len(zones), modes(i),
                                               deps=[token], plan=CHIPS)
    passing = [None] * depth

    def pass_on(i, after):
        landed = _exchange_wait(f"gather_wait_l{i}", *in_flight[i], [WHOLE] * len(modes(i)), modes(i), after,
                                plan=CHIPS)
        *passing[i], forward_token = _exchange_start(f"forward_start_l{i}", [], landed, modes(i), modes(i),
                                                     plan=FORWARD)
        return forward_token

    def passed(i, after):
        return _exchange_wait(f"forward_wait_l{i}", *passing[i], modes(i), modes(i), after, plan=FORWARD)

    def small_whole(idx):
        start = sum(small_counts[:idx])
        part = small_all[:, start:start + small_counts[idx]]
        lead = small_shapes[idx][:-1]
        part = part.reshape((N_DEV,) + lead + (part.shape[-1],))
        part = jnp.moveaxis(part, 0, -2)
        return part.reshape(lead + (N_DEV * part.shape[-1],))

    norm_all = small_whole(0)
    conv_all = small_whole(1)
    dskip_all = small_whole(2)

    def whole(names, arrived):
        return {k: blk if k in in_place else _cols_whole(blk) if k in COLUMN_SHARDED else _rows_whole(blk)
                for (k, _), blk in zip(names, arrived)}

    saved = []
    cur = x2
    for i in range(depth):
        g = [norm_all[i, k].reshape(1, d) for k in range(n_norms)]
        j = i // 2
        if i == 0:
            lw = whole(layer_weights(0)[:n_first], gathered[:n_first])
        else:
            lw = whole(layer_weights(i), passed(i, cur))
        cur, s_ffn1 = _ffn_fwd(f"l{i}_ffn1", cur, g[0], lw["ffn1_up"], lw["ffn1_down"], [token] if i == 0 else ())
        if i == 0:
            lw.update(whole(layer_weights(0)[n_first:], passed(0, pass_on(0, cur))))
        if i % 2 == 0:
            lw["conv_w"] = conv_all[j]
            cur, s_mix = _conv_mixer_fwd(f"l{i}_conv", cur, g[1], lw["conv_w_in"], lw["conv_w"], lw["conv_w_out"],
                                         n_seq, seq)
        else:
            ssm = tuple(w[k][j] for k in ("ssm_a_re", "ssm_a_im", "ssm_log_dt", "ssm_b_re", "ssm_b_im",
                                          "ssm_c_re", "ssm_c_im"))
            cur, s_mix = _s5_mixer_fwd(f"l{i}_s5", cur, g[1], ssm, dskip_all[j].reshape(1, d), lw["ssm_w_glu"],
                                       n_seq, seq)
        cur, s_xa = _xattn_fwd(f"l{i}_xa", cur, mem2, g[2], g[3], lw["xa_w_q"], lw["xa_w_kv"], lw["xa_w_o"],
                               n_seq, seq, mlen, heads)
        early_pass = 0 < i < depth - 1
        deps = [pass_on(i + 1, cur)] if early_pass else ()
        cur, s_ffn2 = _ffn_fwd(f"l{i}_ffn2", cur, g[4], lw["ffn2_up"], lw["ffn2_down"], deps)
        if i == 0 and depth > 1:
            pass_on(1, cur)
        saved.append((g, lw, s_ffn1, s_mix, s_xa, s_ffn2))

    dres, err2, d_final = _final_loss("loss_head", cur, final_g.reshape(1, d), tgt2)
    loss = lax.psum(0.5 * jnp.sum(err2) / d, ("x", "y", "c"))

    d_norm = [[None] * n_norms for _ in range(depth)]
    d_conv = [None] * conv_w.shape[0]
    d_skip = [None] * ssm_d.shape[0]
    d_ssm = [None] * ssm_a_re.shape[0]
    leaving = [None] * depth
    deps = ()

    def leave(name, keys, gm, extra=()):
        srcs, src_modes, zones = [], [], []
        for k in keys:
            if k in in_place:
                rows, n = gm[k].shape[0], gm[k].shape[1] // N_DEV
                srcs.append(gm[k])
                src_modes.append(COLUMNS)
                zones.append(_landing(lax.dynamic_slice(gm[k], (0, me * n), (rows, n)), me))
            else:
                srcs.append(_cols_parts(gm[k]) if k in COLUMN_SHARDED else _rows_parts(gm[k]))
                src_modes.append(BLOCK)
                zones.append(_landing(lax.dynamic_index_in_dim(srcs[-1], me, 0, keepdims=False), me))
        for p in extra:
            srcs.append(p)
            src_modes.append(BLOCK)
            zones.append(_landing(lax.dynamic_index_in_dim(p, me, 0, keepdims=False), me))
        land_modes = [BLOCK] * len(srcs)
        *handles, token = _exchange_start(name, srcs, zones, src_modes, land_modes)
        return (*handles, src_modes, land_modes), token

    def small_parts(full):
        lead = full.shape[:-1]
        t = full.reshape(lead + (N_DEV, full.shape[-1] // N_DEV))
        t = jnp.moveaxis(t, -2, 0)
        return t.reshape(N_DEV, -1, t.shape[-1])

    for i in reversed(range(depth)):
        g, lw, s_ffn1, s_mix, s_xa, s_ffn2 = saved[i]
        j = i // 2
        gm = {}
        dres, d_norm[i][4], gm["ffn2_up"], gm["ffn2_down"] = _ffn_bwd(
            f"l{i}_ffn2", dres, s_ffn2, g[4], lw["ffn2_up"], lw["ffn2_down"], deps)
        dres, d_norm[i][2], d_norm[i][3], gm["xa_w_q"], gm["xa_w_kv"], gm["xa_w_o"] = _xattn_bwd(
            f"l{i}_xa", dres, s_xa, mem2, g[2], g[3], lw["xa_w_q"], lw["xa_w_kv"], lw["xa_w_o"], n_seq, seq, mlen,
            heads)
        if i % 2 == 0:
            dres, d_norm[i][1], gm["conv_w_in"], d_conv[j], gm["conv_w_out"] = _conv_mixer_bwd(
                f"l{i}_conv", dres, s_mix, g[1], lw["conv_w_in"], lw["conv_w"], lw["conv_w_out"], n_seq, seq)
        else:
            dres, d_norm[i][1], d_ssm[j], d_skip[j], gm["ssm_w_glu"] = _s5_mixer_bwd(
                f"l{i}_s5", dres, s_mix, g[1], lw["ssm_w_glu"], n_seq, seq)
        upper, token = leave(f"grads_start_l{i}_upper", [k for k, _ in layer_weights(i)[2:]], gm)
        lower = []

        def send_lower(d_up_t, d_down, i=i, lower=lower):
            handles, token = leave(f"grads_start_l{i}_lower", ["ffn1_up", "ffn1_down"],
                                   {"ffn1_up": d_up_t, "ffn1_down": d_down})
            lower.append(handles)
            return [token]

        dres, d_norm[i][0], _, _ = _ffn_bwd(
            f"l{i}_ffn1", dres, s_ffn1, g[0], lw["ffn1_up"], lw["ffn1_down"], [token], after_dw=send_lower)
        leaving[i] = (upper, lower[0])
        deps = ()
        if i == min(1, depth - 1):
            rep_grads = [jnp.stack([d_ssm[j][k] for j in range(len(d_ssm))]) for k in range(7)]
            rep_packed = _pack_rows(rep_grads + [d_final.reshape(-1)])
            *rep_leaving, token = _exchange_start("replicated_grads_start", [rep_packed], [_landing(rep_packed, me)],
                                                  [WHOLE], [BLOCK])
            deps = [token]
    grad_x = dres[0].reshape(n_seq, seq, d)
    d_norm_all = jnp.stack([jnp.concatenate(row, axis=0) for row in d_norm])
    small_g = jnp.concatenate(
        [small_parts(a) for a in (d_norm_all, jnp.stack(d_conv), jnp.concatenate(d_skip, axis=0))], axis=1)
    small_g = jnp.pad(small_g, [(0, 0), (0, (-small_g.shape[1]) % 8), (0, 0)])
    small_leaving, _ = leave("small_grads_start", [], {}, [small_g])
    received = {k: [None] * w[k].shape[0] for k in MATRICES}

    def arrive(i, part, after):
        names = layer_weights(i)[2:] if part == 0 else layer_weights(i)[:2]
        blks = _exchange_wait(f"grads_wait_l{i}_{'upper' if part == 0 else 'lower'}", *leaving[i][part], after)
        for (k, idx), blk in zip(names, blks):
            received[k][idx] = blk
        return [k for k, _ in names]

    for i in range(1, depth):
        arrive(i, 0, dres[0])
        arrive(i, 1, dres[0])
    rep_all = _exchange_wait("replicated_grads_wait", *rep_leaving, [WHOLE], [BLOCK], dres[0])[0]
    rep_shapes = [w[k].shape for k in REPLICATED]
    flat = {k: (_merge2d(w[k]), _merge2d(mom[k]), _merge2d(var[k])) for k in MATRICES}
    late = {k: received[k][0] is None for k in MATRICES}
    early = {}
    for k in MATRICES:
        first = 1 if late[k] else 0
        if first < len(received[k]):
            early[k] = _adamw(f"adamw_{k}_upper", received[k][first:], *flat[k], first_layer=first,
                              transposed=k in TRANSPOSED)
    after = list(early.values())[-1][0] if early else dres[0]
    done = dict(early)
    for part in (0, 1):
        for k in arrive(0, part, after):
            done[k] = _adamw(f"adamw_{k}_l0", received[k][:1], *flat[k], earlier=early.get(k),
                             transposed=k in TRANSPOSED)
            after = done[k][0]
    small_received, = _exchange_wait("small_grads_wait", *small_leaving, after)
    grads, deltas, new_m, new_v = {}, {}, {}, {}
    for k in MATRICES:
        grads[k], deltas[k], new_m[k], new_v[k] = [o.reshape(w[k].shape) for o in done[k]]

    def small_local(src):
        rows = jnp.concatenate([_merge2d(src[k]) for k in SMALL_SHARDED], axis=0)
        return jnp.pad(rows, [(0, (-rows.shape[0]) % 8), (0, 0)])

    out = _adamw("adamw_small", [small_received], small, small_local(mom), small_local(var))
    for res, o in zip((grads, deltas, new_m, new_v), out):
        start = 0
        for k, cnt, shape in zip(SMALL_SHARDED, small_counts, small_shapes):
            res[k] = o[start:start + cnt].reshape(shape)
            start += cnt

    out = _adamw("adamw_replicated", [rep_all], _pack_rows([w[k] for k in REPLICATED]),
                 _pack_rows([mom[k] for k in REPLICATED]), _pack_rows([var[k] for k in REPLICATED]))
    for res, o in zip((grads, deltas, new_m, new_v), out):
        for k, a in zip(REPLICATED, _unpack_rows(o, rep_shapes)):
            res[k] = a

    return (loss, grad_x, *[grads[k] for k in WEIGHT_NAMES], *[deltas[k] for k in WEIGHT_NAMES],
            *[new_m[k] for k in WEIGHT_NAMES], *[new_v[k] for k in WEIGHT_NAMES])
```

```python
import math

import jax
import jax.numpy as jnp
from jax import lax
from jax.experimental import pallas as pl
from jax.experimental.pallas import tpu as pltpu

F32 = jnp.float32
BF16 = jnp.bfloat16
MESH = pl.DeviceIdType.MESH
N_DEV = 8

NORM_EPS = 1e-6
EIG_CLIP = -1e-4
CONV_WIDTH = 3
ADAM_LR = 0.001
ADAM_B1 = 0.9
ADAM_B2 = 0.999
ADAM_EPS = 1e-08
ADAM_WD = 0.01
ADAM_STEP = 10
GELU_C = math.sqrt(2.0 / math.pi)
GELU_A = 0.044715

V7X_LANES = 128
V7X_VMEM_LIMIT = 56 * 1024 * 1024
S5_CHANNELS = 128
PACK_TILE = 8 * V7X_LANES

HBM_SPEC = pl.BlockSpec(memory_space=pltpu.HBM)
ANY_SPEC = pl.BlockSpec(memory_space=pl.ANY)
SEM_SPEC = pl.BlockSpec(memory_space=pltpu.SEMAPHORE)


def _params(n_grid):
    return pltpu.CompilerParams(dimension_semantics=("arbitrary",) * n_grid, vmem_limit_bytes=V7X_VMEM_LIMIT)


def _pick(n, pref, align):
    if n <= pref:
        return n
    t = (pref // align) * align
    while t >= align:
        if n % t == 0:
            return t
        t -= align
    raise ValueError(f"no tile for {n} (pref {pref}, align {align})")


MM_RHS_BLOCK_BYTES = 12 * 1024 * 1024
MM_LHS_BLOCK_BYTES = 6 * 1024 * 1024
MM_ACC_BYTES = 6 * 1024 * 1024
MM_ROWS = 512


MM_EPILOGUE_ACC_BYTES = 3 * 1024 * 1024
MM_EPILOGUE_MIN_ROWS = 256


def _mm_tiles(m, k, n, a_item, b_item, ta, max_rows):
    tn = _pick(n, max(V7X_LANES, MM_RHS_BLOCK_BYTES // (k * b_item)), V7X_LANES)
    rows = min(max_rows, MM_ACC_BYTES // (4 * tn), MM_LHS_BLOCK_BYTES // (k * a_item))
    align = V7X_LANES if ta else 16
    tm = _pick(m, max(align, rows), align)
    return tm, tn


def _store_results(out_refs, n_row, results, first):
    if not isinstance(results, (tuple, list)):
        results = (results,)
    for o, v in zip(out_refs[:n_row], results[:n_row]):
        if isinstance(v, (tuple, list)):
            off = 0
            for piece in v:
                w = piece.shape[1]
                o[:, off:off + w] = piece.astype(o.dtype)
                off += w
        else:
            o[...] = v.astype(o.dtype)
    if len(out_refs) > n_row:
        @pl.when(first)
        def _():
            for o in out_refs[n_row:]:
                o[...] = jnp.zeros_like(o)

        for o, v in zip(out_refs[n_row:], results[n_row:]):
            o[...] += v


def _mm(name, a, b, *, ta=False, tb=False, out_dtype=BF16, res=None, scale=None, deps=(),
        epilogue=None, row_ins=(), par_ins=(), outs=(), acc_outs=(), prologue=None, prologue_pars=()):
    if ta:
        k, m = a.shape
    else:
        m, k = a.shape
    if tb:
        n, k2 = b.shape
    else:
        k2, n = b.shape
    assert k == k2, (name, a.shape, b.shape)
    max_rows = MM_ROWS if epilogue is None else max(MM_EPILOGUE_MIN_ROWS, MM_EPILOGUE_ACC_BYTES // (4 * n))
    tm, tn = _mm_tiles(m, k, n, a.dtype.itemsize, b.dtype.itemsize, ta, max_rows)
    a_spec = pl.BlockSpec((k, tm), lambda j, i: (0, i)) if ta else pl.BlockSpec((tm, k), lambda j, i: (i, 0))
    b_spec = pl.BlockSpec((tn, k), lambda j, i: (j, 0)) if tb else pl.BlockSpec((k, tn), lambda j, i: (0, j))
    o_spec = pl.BlockSpec((tm, tn), lambda j, i: (i, j))
    dims = (((0 if ta else 1,), (1 if tb else 0,)), ((), ()))
    has_res = res is not None
    ins = [a, b] + ([res] if has_res else [])
    specs = [a_spec, b_spec] + ([o_spec] if has_res else [])
    n_mm = len(ins)
    if epilogue is None:
        out_specs, out_shape = [o_spec], [jax.ShapeDtypeStruct((m, n), out_dtype)]
    else:
        assert tn == n, (name, tn, n)
        for r in row_ins:
            arr, cb, cw = r if isinstance(r, tuple) else (r, 0, r.shape[1])
            assert arr.shape[0] == m, (name, arr.shape, m)
            ins.append(arr)
            specs.append(pl.BlockSpec((tm, cw), lambda j, i, cb=cb: (i, cb)))
        for p in par_ins:
            ins.append(p)
            specs.append(pl.BlockSpec(p.shape, lambda j, i: (0, 0)))
        out_specs = [pl.BlockSpec((tm, c), lambda j, i: (i, 0)) for c, _ in outs]
        out_specs += [pl.BlockSpec((r, c), lambda j, i: (0, 0)) for r, c in acc_outs]
        out_shape = [jax.ShapeDtypeStruct((m, c), dt) for c, dt in outs]
        out_shape += [jax.ShapeDtypeStruct((r, c), F32) for r, c in acc_outs]
    n_in = len(ins)
    if prologue is not None:
        assert tn == n and not ta, (name, tn, n, ta)
        for p in prologue_pars:
            ins.append(p)
            specs.append(pl.BlockSpec(p.shape, lambda j, i: (0, 0)))
        out_specs = out_specs + [a_spec]
        out_shape = out_shape + [jax.ShapeDtypeStruct((m, k), BF16)]
    n_pro = len(ins)
    ins += list(deps)
    specs += [ANY_SPEC] * len(deps)

    def body(*refs):
        a_ref, b_ref = refs[0], refs[1]
        out_refs = refs[n_pro + len(deps):]
        if prologue is None:
            lhs = a_ref[...].astype(BF16)
        else:
            lhs = prologue(a_ref[...], *[r[...] for r in refs[n_in:n_pro]]).astype(BF16)
            out_refs[-1][...] = lhs
            out_refs = out_refs[:-1]
        acc = lax.dot_general(lhs, b_ref[...].astype(BF16), dims, preferred_element_type=F32)
        if scale is not None:
            acc = acc * scale
        if has_res:
            acc = acc + refs[2][...].astype(F32)
        if epilogue is None:
            out_refs[0][...] = acc.astype(out_refs[0].dtype)
        else:
            extra = [r[...] for r in refs[n_mm:n_in]]
            _store_results(out_refs, len(outs), epilogue(acc, *extra), pl.program_id(1) == 0)

    out = pl.pallas_call(
        body, name=name, grid=(n // tn, m // tm), in_specs=specs, out_specs=out_specs, out_shape=out_shape,
        compiler_params=_params(2),
    )(*ins)
    return out[0] if epilogue is None and prologue is None else out


def _rowwise(name, fn, rows, row_ins, par_ins, row_outs, acc_outs=(), tm_pref=256):
    tm = _pick(rows, tm_pref, 16)
    in_specs, ins = [], []
    for r in row_ins:
        arr, cb, cw = r if isinstance(r, tuple) else (r, 0, r.shape[1])
        assert arr.shape[0] == rows, (name, arr.shape, rows)
        ins.append(arr)
        in_specs.append(pl.BlockSpec((tm, cw), lambda i, cb=cb: (i, cb)))
    for p in par_ins:
        ins.append(p)
        in_specs.append(pl.BlockSpec(p.shape, lambda i: (0, 0)))
    out_specs = [pl.BlockSpec((tm, c), lambda i: (i, 0)) for c, _ in row_outs]
    out_specs += [pl.BlockSpec((r, c), lambda i: (0, 0)) for r, c in acc_outs]
    out_shape = [jax.ShapeDtypeStruct((rows, c), dt) for c, dt in row_outs]
    out_shape += [jax.ShapeDtypeStruct((r, c), F32) for r, c in acc_outs]
    n_in, n_row = len(ins), len(row_outs)

    def body(*refs):
        vals = [r[...] for r in refs[:n_in]]
        _store_results(refs[n_in:], n_row, fn(*vals), pl.program_id(0) == 0)

    return pl.pallas_call(
        body, name=name, grid=(rows // tm,), in_specs=in_specs, out_specs=out_specs, out_shape=out_shape,
        compiler_params=_params(1),
    )(*ins)


def _inv_rms(x):
    return lax.rsqrt(jnp.mean(x * x, axis=-1, keepdims=True) + NORM_EPS)


def _rms_rows(x, g):
    return x * _inv_rms(x) * g


def _rms_fwd(name, x, g):
    return _rowwise(name, _rms_rows, x.shape[0], [x], [g], [(x.shape[1], BF16)], tm_pref=512)[0]


def _rms_bwd_rows(dn, x, dres, g):
    r = _inv_rms(x)
    xh = x * r
    dg = jnp.sum(dn * xh, axis=0, keepdims=True)
    dxh = dn * g
    dx = r * (dxh - xh * jnp.mean(dxh * xh, axis=-1, keepdims=True)) + dres
    return dx, dx, dg


def _rms_bwd(name, x, g, dn, dres=None):
    d = x.shape[1]
    if dres is None:
        def fn(x, dn, g):
            return (jnp.sum(dn.astype(F32) * (x * _inv_rms(x)), axis=0, keepdims=True),)

        return None, _rowwise(name, fn, x.shape[0], [x, dn], [g], [], [(1, d)])[0]

    def fn(x, dn, dres, g):
        return _rms_bwd_rows(dn.astype(F32), x, dres, g)

    out = _rowwise(name, fn, x.shape[0], [x, dn, dres], [g], [(d, F32), (d, BF16)], [(1, d)])
    return (out[0], out[1]), out[2]


def _mm_rms_bwd(name, dy, w, x, g, dres, tb=True, deps=()):
    d = x.shape[1]
    out = _mm(name, dy, w, tb=tb, deps=deps, epilogue=_rms_bwd_rows, row_ins=[x, dres], par_ins=[g],
              outs=[(d, F32), (d, BF16)], acc_outs=[(1, d)])
    return (out[0], out[1]), out[2]


def _sigmoid(x):
    return 0.5 + 0.5 * jnp.tanh(0.5 * x)


def _swiglu_rows(gu, f):
    gt = gu[:, :f]
    return gu, gt * _sigmoid(gt) * gu[:, f:]


def _swiglu_bwd_rows(dact, gt, up):
    gt, up = gt.astype(F32), up.astype(F32)
    sg = _sigmoid(gt)
    return ((dact * up * (sg * (1.0 + gt * (1.0 - sg))), dact * (gt * sg)),)


def _glu_bwd(name, dres, vg, d):
    def fn(dres, val, gate):
        val, gate = val.astype(F32), gate.astype(F32)
        sg = _sigmoid(gate)
        return ((dres * sg, dres * val * sg * (1.0 - sg)),)

    return _rowwise(name, fn, dres.shape[0], [dres, (vg, 0, d), (vg, 1, d)], [], [(2 * d, BF16)])[0]


def _final_loss(name, x, g, tgt):
    d = x.shape[1]

    def fn(x, tgt, g):
        r = _inv_rms(x)
        xh = x * r
        err = xh * g - tgt
        dy = err * (1.0 / d)
        dxh = dy * g
        dx = r * (dxh - xh * jnp.mean(dxh * xh, axis=-1, keepdims=True))
        return dx, dx, jnp.sum(err * err, axis=0, keepdims=True), jnp.sum(dy * xh, axis=0, keepdims=True)

    dx, dx16, err2, dg = _rowwise(name, fn, x.shape[0], [x, tgt], [g], [(d, F32), (d, BF16)], [(1, d), (1, d)])
    return (dx, dx16), err2, dg


def _shift_down(u, k):
    rows = lax.broadcasted_iota(jnp.int32, u.shape, 0)
    return jnp.where(rows >= k, pltpu.roll(u, k, 0), 0.0)


def _shift_up(u, k):
    n = u.shape[0]
    rows = lax.broadcasted_iota(jnp.int32, u.shape, 0)
    return jnp.where(rows < n - k, pltpu.roll(u, n - k, 0), 0.0)


def _conv_specs(seq, cw, n_cb, swap):
    def at(off):
        if swap:
            return pl.BlockSpec((seq, cw), lambda j, b: (b, off * n_cb + j))
        return pl.BlockSpec((seq, cw), lambda b, j: (b, off * n_cb + j))

    return at


def _conv_fwd(name, cbv, w, n_seq, seq):
    d = w.shape[1]
    cw = _pick(d, 256, V7X_LANES)
    n_cb = d // cw
    at = _conv_specs(seq, cw, n_cb, swap=False)

    def body(c_ref, b_ref, v_ref, w_ref, z_ref):
        u = c_ref[...].astype(F32) * v_ref[...].astype(F32)
        cv = w_ref[0:1, :] * _shift_down(u, 2) + w_ref[1:2, :] * _shift_down(u, 1) + w_ref[2:3, :] * u
        z_ref[...] = (b_ref[...].astype(F32) * cv).astype(z_ref.dtype)

    return pl.pallas_call(
        body, name=name, grid=(n_seq, n_cb),
        in_specs=[at(0), at(1), at(2), pl.BlockSpec((CONV_WIDTH, cw), lambda b, j: (0, j))],
        out_specs=at(0), out_shape=jax.ShapeDtypeStruct((n_seq * seq, d), BF16), compiler_params=_params(2),
    )(cbv, cbv, cbv, w)


def _conv_bwd(name, dz, cbv, w, n_seq, seq):
    d = w.shape[1]
    cw = _pick(d, 256, V7X_LANES)
    n_cb = d // cw
    at = _conv_specs(seq, cw, n_cb, swap=True)

    def body(dz_ref, c_ref, b_ref, v_ref, w_ref, dc_ref, db_ref, dv_ref, dw_ref):
        c, b, v = c_ref[...].astype(F32), b_ref[...].astype(F32), v_ref[...].astype(F32)
        dz = dz_ref[...].astype(F32)
        w0, w1, w2 = w_ref[0:1, :], w_ref[1:2, :], w_ref[2:3, :]
        u = c * v
        u1, u2 = _shift_down(u, 1), _shift_down(u, 2)
        cv = w0 * u2 + w1 * u1 + w2 * u
        db_ref[...] = (dz * cv).astype(db_ref.dtype)
        dcv = dz * b
        du = w2 * dcv + w1 * _shift_up(dcv, 1) + w0 * _shift_up(dcv, 2)
        dc_ref[...] = (du * v).astype(dc_ref.dtype)
        dv_ref[...] = (du * c).astype(dv_ref.dtype)

        @pl.when(pl.program_id(1) == 0)
        def _():
            dw_ref[...] = jnp.zeros_like(dw_ref)

        dw_ref[0:1, :] += jnp.sum(dcv * u2, axis=0, keepdims=True)
        dw_ref[1:2, :] += jnp.sum(dcv * u1, axis=0, keepdims=True)
        dw_ref[2:3, :] += jnp.sum(dcv * u, axis=0, keepdims=True)

    act = jax.ShapeDtypeStruct((n_seq * seq, d), BF16)
    return pl.pallas_call(
        body, name=name, grid=(n_cb, n_seq),
        in_specs=[at(0), at(0), at(1), at(2), pl.BlockSpec((CONV_WIDTH, cw), lambda j, b: (0, j))],
        out_specs=[at(0), at(0), at(0), pl.BlockSpec((CONV_WIDTH, cw), lambda j, b: (0, j))],
        out_shape=[act, act, act, jax.ShapeDtypeStruct((CONV_WIDTH, d), F32)], compiler_params=_params(2),
    )(dz, cbv, cbv, cbv, w)


def _s5_discretize(a_re, a_im, log_dt, b_re, b_im):
    lam_re = jnp.minimum(a_re, EIG_CLIP)
    lam_im = a_im
    dt = jnp.exp(log_dt)[:, None]
    mag = jnp.exp(lam_re * dt)
    abar_re = mag * jnp.cos(lam_im * dt)
    abar_im = mag * jnp.sin(lam_im * dt)
    den = lam_re * lam_re + lam_im * lam_im
    num_re = abar_re - 1.0
    num_im = abar_im
    coef_re = (num_re * lam_re + num_im * lam_im) / den
    coef_im = (num_im * lam_re - num_re * lam_im) / den
    bbar_re = coef_re[..., None] * b_re - coef_im[..., None] * b_im
    bbar_im = coef_re[..., None] * b_im + coef_im[..., None] * b_re
    return abar_re, abar_im, bbar_re, bbar_im


def _block_diag_in(bbar, gb):
    g, p, h = bbar.shape
    t = jnp.transpose(bbar.reshape(g // gb, gb, p, h), (0, 1, 3, 2))
    return jnp.einsum("cghp,gk->cghkp", t, jnp.eye(gb, dtype=bbar.dtype)).reshape(g // gb, gb * h, gb * p)


def _block_diag_in_t(blk, gb, p, h):
    nb = blk.shape[0]
    t = jnp.einsum("cghkp,gk->cghp", blk.reshape(nb, gb, h, gb, p), jnp.eye(gb, dtype=blk.dtype))
    return jnp.transpose(t, (0, 1, 3, 2)).reshape(nb * gb, p, h)


def _block_diag_out(c, gb):
    g, h, p = c.shape
    t = jnp.transpose(c.reshape(g // gb, gb, h, p), (0, 1, 3, 2))
    return jnp.einsum("cgph,gk->cgpkh", t, jnp.eye(gb, dtype=c.dtype)).reshape(g // gb, gb * p, gb * h)


def _block_diag_out_t(blk, gb, p, h):
    nb = blk.shape[0]
    t = jnp.einsum("cgpkh,gk->cgph", blk.reshape(nb, gb, p, gb, h), jnp.eye(gb, dtype=blk.dtype))
    return jnp.transpose(t, (0, 1, 3, 2)).reshape(nb * gb, h, p)


def _gelu(y):
    return 0.5 * y * (1.0 + jnp.tanh(GELU_C * (y + GELU_A * y * y * y)))


def _gelu_grad(y):
    th = jnp.tanh(GELU_C * (y + GELU_A * y * y * y))
    return 0.5 * (1.0 + th) + 0.5 * y * (1.0 - th * th) * GELU_C * (1.0 + 3.0 * GELU_A * y * y)


def _dot(a, b, ca, cb):
    return lax.dot_general(a.astype(BF16), b.astype(BF16), (((ca,), (cb,)), ((), ())), preferred_element_type=F32)


def _s5_specs(seq, ch, sb):
    act = pl.BlockSpec((seq, ch), lambda j, b: (b, j))
    state = pl.BlockSpec((seq, sb), lambda j, b: (b, j))
    w_in = pl.BlockSpec((None, ch, sb), lambda j, b: (j, 0, 0))
    w_out = pl.BlockSpec((None, sb, ch), lambda j, b: (j, 0, 0))
    lane_s = pl.BlockSpec((1, sb), lambda j, b: (0, j))
    lane_c = pl.BlockSpec((1, ch), lambda j, b: (0, j))
    return act, state, w_in, w_out, lane_s, lane_c


def _s5_fwd(name, h, bin_re, bin_im, cout_re, cout_im, abar_re, abar_im, dskip, n_seq, seq):
    t, d = h.shape
    nb, ch, sb = bin_re.shape
    act, state, w_in, w_out, lane_s, lane_c = _s5_specs(seq, ch, sb)

    def body(h_ref, bre_ref, bim_ref, cre_ref, cim_ref, ar_ref, ai_ref, d_ref, sre_ref, sim_ref, y_ref, z_ref):
        u = h_ref[...]
        bu = _dot(u, jnp.concatenate([bre_ref[...], bim_ref[...]], axis=1), 1, 0)
        sre_ref[...] = bu[:, :sb]
        sim_ref[...] = bu[:, sb:]
        ar, ai = ar_ref[...], ai_ref[...]

        def step(i, carry):
            sr, si = carry
            row = pl.ds(i, 1)
            nr = ar * sr - ai * si + sre_ref[row, :]
            ni = ar * si + ai * sr + sim_ref[row, :]
            sre_ref[row, :] = nr
            sim_ref[row, :] = ni
            return nr, ni

        zero = jnp.zeros((1, sb), F32)
        lax.fori_loop(0, seq, step, (zero, zero), unroll=8)
        y = _dot(jnp.concatenate([sre_ref[...], sim_ref[...]], axis=1),
                 jnp.concatenate([cre_ref[...], -cim_ref[...]], axis=0), 1, 0)
        y = y + d_ref[...] * u.astype(F32)
        y_ref[...] = y
        z_ref[...] = _gelu(y).astype(z_ref.dtype)

    return pl.pallas_call(
        body, name=name, grid=(nb, n_seq),
        in_specs=[act, w_in, w_in, w_out, w_out, lane_s, lane_s, lane_c],
        out_specs=[state, state, act, act],
        out_shape=[jax.ShapeDtypeStruct((t, nb * sb), F32), jax.ShapeDtypeStruct((t, nb * sb), F32),
                   jax.ShapeDtypeStruct((t, d), F32), jax.ShapeDtypeStruct((t, d), BF16)],
        compiler_params=_params(2),
    )(h, bin_re, bin_im, cout_re, cout_im, abar_re, abar_im, dskip)


def _s5_bwd(name, dz, ypre, h, s_re, s_im, bin_re, bin_im, cout_re, cout_im, abar_re, abar_im, dskip, n_seq, seq):
    t, d = h.shape
    nb, ch, sb = bin_re.shape
    act, state, w_in, w_out, lane_s, lane_c = _s5_specs(seq, ch, sb)

    def body(dz_ref, y_ref, h_ref, sre_ref, sim_ref, bre_ref, bim_ref, cre_ref, cim_ref, ar_ref, ai_ref, d_ref,
             dh_ref, dbre_ref, dbim_ref, dcre_ref, dcim_ref, dar_ref, dai_ref, dd_ref, gre, gim):
        first = pl.program_id(1) == 0
        u = h_ref[...].astype(F32)
        dy = dz_ref[...].astype(F32) * _gelu_grad(y_ref[...])
        b_cat = jnp.concatenate([bre_ref[...], bim_ref[...]], axis=1)
        gdir = _dot(dy, jnp.concatenate([cre_ref[...], -cim_ref[...]], axis=0), 1, 1)
        gre[...] = gdir[:, :sb]
        gim[...] = gdir[:, sb:]
        ar, ai = ar_ref[...], ai_ref[...]

        def step(i, carry):
            gr, gi = carry
            row = pl.ds(seq - 1 - i, 1)
            nr = gre[row, :] + ar * gr + ai * gi
            ni = gim[row, :] - ai * gr + ar * gi
            gre[row, :] = nr
            gim[row, :] = ni
            return nr, ni

        zero = jnp.zeros((1, sb), F32)
        lax.fori_loop(0, seq, step, (zero, zero), unroll=8)

        g_re, g_im = gre[...], gim[...]
        s_re, s_im = sre_ref[...], sim_ref[...]
        p_re, p_im = _shift_down(s_re, 1), _shift_down(s_im, 1)
        dar = jnp.sum(g_re * p_re + g_im * p_im, axis=0, keepdims=True)
        dai = jnp.sum(g_im * p_re - g_re * p_im, axis=0, keepdims=True)
        g_cat = jnp.concatenate([g_re, g_im], axis=1).astype(BF16)
        db = _dot(u, g_cat, 0, 0)
        dbre, dbim = db[:, :sb], db[:, sb:]
        dc = _dot(jnp.concatenate([s_re, s_im], axis=1), dy, 0, 0)
        dcre, dcim = dc[:sb], -dc[sb:]
        ddd = jnp.sum(dy * u, axis=0, keepdims=True)
        dh_ref[...] = _dot(g_cat, b_cat, 1, 1) + d_ref[...] * dy

        @pl.when(first)
        def _():
            dar_ref[...] = dar
            dai_ref[...] = dai
            dbre_ref[...] = dbre
            dbim_ref[...] = dbim
            dcre_ref[...] = dcre
            dcim_ref[...] = dcim
            dd_ref[...] = ddd

        @pl.when(jnp.logical_not(first))
        def _():
            dar_ref[...] += dar
            dai_ref[...] += dai
            dbre_ref[...] += dbre
            dbim_ref[...] += dbim
            dcre_ref[...] += dcre
            dcim_ref[...] += dcim
            dd_ref[...] += ddd

    return pl.pallas_call(
        body, name=name, grid=(nb, n_seq),
        in_specs=[act, act, act, state, state, w_in, w_in, w_out, w_out, lane_s, lane_s, lane_c],
        out_specs=[act, w_in, w_in, w_out, w_out, lane_s, lane_s, lane_c],
        out_shape=[jax.ShapeDtypeStruct((t, d), F32),
                   jax.ShapeDtypeStruct((nb, ch, sb), F32), jax.ShapeDtypeStruct((nb, ch, sb), F32),
                   jax.ShapeDtypeStruct((nb, sb, ch), F32), jax.ShapeDtypeStruct((nb, sb, ch), F32),
                   jax.ShapeDtypeStruct((1, nb * sb), F32), jax.ShapeDtypeStruct((1, nb * sb), F32),
                   jax.ShapeDtypeStruct((1, d), F32)],
        scratch_shapes=[pltpu.VMEM((seq, sb), F32), pltpu.VMEM((seq, sb), F32)],
        compiler_params=_params(2),
    )(dz, ypre, h, s_re, s_im, bin_re, bin_im, cout_re, cout_im, abar_re, abar_im, dskip)


ATTN_QUERY_ROWS = 1024


def _softmax_rows(q, k, scale):
    s = _dot(q, k, 1, 1) * scale
    e = jnp.exp(s - jnp.max(s, axis=-1, keepdims=True))
    return e * (1.0 / jnp.sum(e, axis=-1, keepdims=True))


def _attn_fwd(name, q, kv, n_seq, seq, mlen, heads):
    t, d = q.shape
    hd = d // heads
    tq = _pick(seq, ATTN_QUERY_ROWS, 16)
    nq = seq // tq
    scale = hd ** -0.5
    q_spec = pl.BlockSpec((tq, d), lambda b, i: (b * nq + i, 0))

    def body(q_ref, k_ref, v_ref, o_ref):
        for h in range(heads):
            cols = slice(h * hd, (h + 1) * hd)
            p = _softmax_rows(q_ref[:, cols], k_ref[:, cols], scale)
            o_ref[:, cols] = _dot(p, v_ref[:, cols], 1, 0).astype(o_ref.dtype)

    return pl.pallas_call(
        body, name=name, grid=(n_seq, nq),
        in_specs=[q_spec, pl.BlockSpec((mlen, d), lambda b, i: (b, 0)), pl.BlockSpec((mlen, d), lambda b, i: (b, 1))],
        out_specs=q_spec, out_shape=jax.ShapeDtypeStruct((t, d), BF16), compiler_params=_params(2),
    )(q, kv, kv)


def _attn_bwd(name, q, kv, do, n_seq, seq, mlen, heads):
    t, d = q.shape
    hd = d // heads
    tq = _pick(seq, ATTN_QUERY_ROWS, 16)
    nq = seq // tq
    scale = hd ** -0.5
    q_spec = pl.BlockSpec((tq, d), lambda b, i: (b * nq + i, 0))
    k_spec = pl.BlockSpec((mlen, d), lambda b, i: (b, 0))

    def body(q_ref, k_ref, v_ref, do_ref, dq_ref, dk_ref, dv_ref):
        @pl.when(pl.program_id(1) == 0)
        def _():
            dk_ref[...] = jnp.zeros_like(dk_ref)
            dv_ref[...] = jnp.zeros_like(dv_ref)

        for h in range(heads):
            cols = slice(h * hd, (h + 1) * hd)
            q, k, v, do = q_ref[:, cols], k_ref[:, cols], v_ref[:, cols], do_ref[:, cols]
            p = _softmax_rows(q, k, scale)
            dp = _dot(do, v, 1, 1)
            ds = p * (dp - jnp.sum(dp * p, axis=-1, keepdims=True)) * scale
            dq_ref[:, cols] = _dot(ds, k, 1, 0).astype(dq_ref.dtype)
            dk_ref[:, cols] += _dot(ds, q, 0, 0)
            dv_ref[:, cols] += _dot(p, do, 0, 0)

    return pl.pallas_call(
        body, name=name, grid=(n_seq, nq),
        in_specs=[q_spec, k_spec, pl.BlockSpec((mlen, d), lambda b, i: (b, 1)), q_spec],
        out_specs=[q_spec, k_spec, k_spec],
        out_shape=[jax.ShapeDtypeStruct((t, d), BF16), jax.ShapeDtypeStruct((n_seq * mlen, d), F32),
                   jax.ShapeDtypeStruct((n_seq * mlen, d), F32)],
        compiler_params=_params(2),
    )(q, kv, kv, do)


ADAMW_BLOCK_ELEMS = 128 * 1024


def _adamw(name, parts, w, m, v, first_layer=0, earlier=None, transposed=False):
    n_layers = len(parts)
    r, c = parts[0].shape[1:][::-1] if transposed else parts[0].shape[1:]
    assert w.shape[0] % r == 0 and w.shape[1] == c and first_layer + n_layers <= w.shape[0] // r, (name, w.shape)
    tr = _pick(r, max(V7X_LANES, ADAMW_BLOCK_ELEMS // c // V7X_LANES * V7X_LANES), V7X_LANES if transposed else 8)
    nt = r // tr
    spec = pl.BlockSpec((tr, c), lambda l, i: ((first_layer + l) * nt + i, 0))
    c1 = 1.0 - ADAM_B1 ** ADAM_STEP
    c2 = 1.0 - ADAM_B2 ** ADAM_STEP

    def parts_spec(q):
        def at(l, i):
            return jnp.where(l == q, i, jnp.where(l > q, nt - 1, 0))

        if transposed:
            return pl.BlockSpec((N_DEV, c, tr), lambda l, i: (0, 0, at(l, i)))
        return pl.BlockSpec((N_DEV, tr, c), lambda l, i: (0, at(l, i), 0))

    earlier = list(earlier or ())

    def body(*refs):
        p_refs = refs[:n_layers]
        w_ref, m_ref, v_ref = refs[n_layers:n_layers + 3]
        g_ref, d_ref, nm_ref, nv_ref = refs[n_layers + 3 + len(earlier):]

        def update(p_ref):
            g = p_ref[0].astype(F32)
            for k in range(1, N_DEV):
                g = g + p_ref[k].astype(F32)
            if transposed:
                g = g.T
            nm = ADAM_B1 * m_ref[...] + (1.0 - ADAM_B1) * g
            nv = ADAM_B2 * v_ref[...] + (1.0 - ADAM_B2) * (g * g)
            g_ref[...] = g
            nm_ref[...] = nm
            nv_ref[...] = nv
            d_ref[...] = -ADAM_LR * ((nm / c1) / (jnp.sqrt(nv / c2) + ADAM_EPS) + ADAM_WD * w_ref[...])

        for q in range(n_layers):
            pl.when(pl.program_id(0) == q)(lambda q=q: update(p_refs[q]))

    out = jax.ShapeDtypeStruct(w.shape, F32)
    return pl.pallas_call(
        body, name=name, grid=(n_layers, nt),
        in_specs=[parts_spec(q) for q in range(n_layers)] + [spec] * 3 + [ANY_SPEC] * len(earlier),
        out_specs=[spec] * 4, out_shape=[out] * 4, compiler_params=_params(2),
        input_output_aliases={n_layers + 3 + q: q for q in range(len(earlier))},
    )(*parts, w, m, v, *earlier)


def _place():
    x, y, c = lax.axis_index("x"), lax.axis_index("y"), lax.axis_index("c")
    return x, y, c


def _index(px, py, pc):
    return 4 * px + 2 * py + pc


def _all_gather(name, shards):
    n = len(shards)

    def body(*refs):
        in_refs, out_refs = refs[:n], refs[n:2 * n]
        send_sems, recv_sems, local_sems = refs[2 * n:]
        x, y, c = _place()
        me, sibling = (x, y, c), (x, y, 1 - c)
        chips = [(1 - x, y), (x, 1 - y), (1 - x, 1 - y)]

        def slot(k, block):
            return out_refs[k].at[_index(*block)]

        def copy(k, j, block, to, src=None):
            return pltpu.make_async_remote_copy(
                src_ref=slot(k, block) if src is None else src, dst_ref=slot(k, block),
                send_sem=send_sems.at[7 * k + j], recv_sem=recv_sems.at[7 * k + j], device_id=to, device_id_type=MESH)

        mine = [pltpu.make_async_copy(in_refs[k], slot(k, me), local_sems.at[k]) for k in range(n)]
        for cp in mine:
            cp.start()
        first = []
        for k in range(n):
            first.append(copy(k, 0, me, sibling, src=in_refs[k]))
            first += [copy(k, 1 + j, me, (*chip, c), src=in_refs[k]) for j, chip in enumerate(chips)]
        for cp in first:
            cp.start()
        passed = []
        for j, chip in enumerate(chips):
            for k in range(n):
                copy(k, 1 + j, (*chip, c), me).wait_recv()
                cp = copy(k, 4 + j, (*chip, c), sibling)
                cp.start()
                passed.append(cp)
        for k in range(n):
            copy(k, 0, sibling, me).wait_recv()
            for j, chip in enumerate(chips):
                copy(k, 4 + j, (*chip, 1 - c), me).wait_recv()
        for cp in first + passed:
            cp.wait_send()
        for cp in mine:
            cp.wait()

    return pl.pallas_call(
        body, name=name, in_specs=[HBM_SPEC] * n, out_specs=[HBM_SPEC] * n,
        out_shape=[jax.ShapeDtypeStruct((N_DEV,) + s.shape, s.dtype) for s in shards],
        scratch_shapes=[pltpu.SemaphoreType.DMA((7 * n,)), pltpu.SemaphoreType.DMA((7 * n,)),
                        pltpu.SemaphoreType.DMA((n,))],
    )(*shards)


WHOLE, BLOCK, COLUMNS = "whole", "block", "columns"


def _slot(ref, index, mode):
    if mode == WHOLE:
        return ref
    if mode == BLOCK:
        return ref.at[index]
    width = ref.shape[1] // N_DEV
    return ref.at[:, pl.ds(pl.multiple_of(index * width, width), width)]


DIRECT = tuple(range(1, N_DEV))
CHIPS = (1, 4, 2, 6)
FORWARD = "forward"


def _copies(plan, x, y, c):
    def xor(r, flip_core=False):
        rx, ry, rc = (r >> 2) & 1, (r >> 1) & 1, (r & 1) ^ int(flip_core)
        return (1 - x if rx else x, 1 - y if ry else y, 1 - c if rc else c)

    me = _index(x, y, c)
    if plan == FORWARD:
        return [(xor(1), _index(*xor(r)), _index(*xor(r)), _index(*xor(r, True))) for r in (4, 2, 6)]
    return [(xor(r), _index(*xor(r)), me, _index(*xor(r))) for r in plan]


def _exchange_start(name, srcs, lands, src_modes, land_modes, deps=(), plan=DIRECT):
    n, n_src = len(lands), len(srcs)
    n_copies = 3 if plan == FORWARD else len(plan)

    def body(*refs):
        land_refs = refs[n_src:n_src + n]
        src_refs = refs[:n_src] if n_src else land_refs
        send_sems, recv_sems = refs[n_src + n + len(deps)], refs[n_src + n + len(deps) + 1]
        token = refs[-1]
        x, y, c = _place()
        for k in range(n):
            for j, (peer, src_index, dst_index, _) in enumerate(_copies(plan, x, y, c)):
                pltpu.make_async_remote_copy(
                    src_ref=_slot(src_refs[k], src_index, src_modes[k]),
                    dst_ref=_slot(land_refs[k], dst_index, land_modes[k]), send_sem=send_sems.at[n_copies * k + j],
                    recv_sem=recv_sems.at[n_copies * k + j], device_id=peer, device_id_type=MESH).start()
        token[...] = jnp.zeros_like(token)

    arrays = list(srcs) + list(lands)
    thru = [pltpu.HBM(a.shape, a.dtype) for a in arrays]
    out = pl.pallas_call(
        body, name=name,
        out_shape=(pltpu.SemaphoreType.DMA((n_copies * n,)), pltpu.SemaphoreType.DMA((n_copies * n,)), *thru,
                   jax.ShapeDtypeStruct((8, V7X_LANES), F32)),
        in_specs=[HBM_SPEC] * len(arrays) + [ANY_SPEC] * len(deps),
        out_specs=(SEM_SPEC, SEM_SPEC, *([HBM_SPEC] * len(arrays)), pl.BlockSpec(memory_space=pltpu.VMEM)),
        input_output_aliases={k: 2 + k for k in range(len(arrays))},
        compiler_params=pltpu.CompilerParams(has_side_effects=pltpu.SideEffectType.DATAFLOW_SIDE_EFFECTING),
    )(*[pltpu.with_memory_space_constraint(a, pltpu.HBM) for a in arrays], *deps)
    return out[0], out[1], list(out[2:2 + n_src]), list(out[2 + n_src:2 + n_src + n]), out[-1]


def _exchange_wait(name, send_sems, recv_sems, srcs, lands, src_modes, land_modes, after, plan=DIRECT):
    n, n_src = len(lands), len(srcs)
    n_copies = 3 if plan == FORWARD else len(plan)

    def body(*refs):
        land_refs = refs[n_src:n_src + n]
        src_refs = refs[:n_src] if n_src else land_refs
        send_sems, recv_sems = refs[n_src + n], refs[n_src + n + 1]
        x, y, c = _place()
        for k in range(n):
            for j, (peer, src_index, _, arrival_index) in enumerate(_copies(plan, x, y, c)):
                cp = pltpu.make_async_remote_copy(
                    src_ref=_slot(src_refs[k], src_index, src_modes[k]),
                    dst_ref=_slot(land_refs[k], arrival_index, land_modes[k]), send_sem=send_sems.at[n_copies * k + j],
                    recv_sem=recv_sems.at[n_copies * k + j], device_id=peer, device_id_type=MESH)
                cp.wait_send()
                cp.wait_recv()

    arrays = list(srcs) + list(lands)
    thru = [pltpu.HBM(a.shape, a.dtype) for a in arrays]
    out = pl.pallas_call(
        body, name=name, out_shape=tuple(thru),
        in_specs=[HBM_SPEC] * len(arrays) + [SEM_SPEC, SEM_SPEC, ANY_SPEC], out_specs=tuple([HBM_SPEC] * len(arrays)),
        input_output_aliases={k: k for k in range(len(arrays))},
        compiler_params=pltpu.CompilerParams(has_side_effects=pltpu.SideEffectType.DATAFLOW_SIDE_EFFECTING),
    )(*arrays, send_sems, recv_sems, after)
    return list(out[n_src:])


def _landing(shard, me, mode=BLOCK):
    if mode == COLUMNS:
        k, n = shard.shape
        return lax.dynamic_update_slice(lax.empty((k, N_DEV * n), shard.dtype), shard, (0, me * n))
    zone = lax.empty((N_DEV,) + shard.shape, shard.dtype)
    return lax.dynamic_update_slice(zone, shard[None], (me,) + (0,) * shard.ndim)


def _cols_whole(w):
    return jnp.transpose(w, (1, 0, 2)).reshape(w.shape[1], N_DEV * w.shape[2])


def _rows_whole(w):
    return w.reshape(N_DEV * w.shape[1], w.shape[2])


def _cols_parts(dw):
    k, n8 = dw.shape
    return jnp.transpose(dw.reshape(k, N_DEV, n8 // N_DEV), (1, 0, 2))


def _rows_parts(dw):
    r8, c = dw.shape
    return dw.reshape(N_DEV, r8 // N_DEV, c)


def _pack_rows(arrays):
    rows = []
    for a in arrays:
        flat = a.reshape(-1).astype(F32)
        flat = jnp.pad(flat, [(0, (-flat.shape[0]) % PACK_TILE)])
        rows.append(flat.reshape(-1, V7X_LANES))
    return jnp.concatenate(rows, axis=0)


def _unpack_rows(packed, shapes):
    out, row = [], 0
    for s in shapes:
        size = math.prod(s)
        n_rows = -(-size // PACK_TILE) * 8
        out.append(packed[row:row + n_rows].reshape(-1)[:size].reshape(s))
        row += n_rows
    return out


def _merge2d(a):
    return a.reshape(-1, a.shape[-1])


def _ffn_fwd(tag, x, g, w_up_t, w_down, deps=()):
    f = w_down.shape[0]
    gu, act, n = _mm(f"{tag}_up", x, w_up_t, tb=True, prologue=_rms_rows, prologue_pars=[g], deps=deps,
                     epilogue=lambda acc: _swiglu_rows(acc, f), outs=[(2 * f, BF16), (f, BF16)])
    out = _mm(f"{tag}_down", act, w_down, res=x, scale=0.5, out_dtype=F32)
    return out, (x, n, gu, act)


def _ffn_bwd(tag, dres, saved, g, w_up_t, w_down, deps=(), after_dw=None):
    x, n, gu, act = saved
    dres32, dres16 = dres
    f = w_down.shape[0]
    dgu = _mm(f"{tag}_down_dx", dres16, w_down, tb=True, scale=0.5, deps=deps, epilogue=_swiglu_bwd_rows,
              row_ins=[(gu, 0, f), (gu, 1, f)], outs=[(2 * f, BF16)])[0]
    d_down = _mm(f"{tag}_down_dw", act, dres16, ta=True, scale=0.5)
    d_up_t = _mm(f"{tag}_up_dw", dgu, n, ta=True)
    dx, dg = _mm_rms_bwd(f"{tag}_up_dx", dgu, w_up_t, x, g, dres32, tb=False,
                         deps=after_dw(d_up_t, d_down) if after_dw else ())
    return dx, dg, d_up_t, d_down


def _conv_mixer_fwd(tag, x, g, w_in, w_conv, w_out, n_seq, seq):
    cbv, h = _mm(f"{tag}_in", x, w_in, prologue=_rms_rows, prologue_pars=[g])
    z = _conv_fwd(f"{tag}_conv", cbv, w_conv, n_seq, seq)
    out = _mm(f"{tag}_out", z, w_out, res=x, out_dtype=F32)
    return out, (x, h, cbv, z)


def _conv_mixer_bwd(tag, dres, saved, g, w_in, w_conv, w_out, n_seq, seq):
    x, h, cbv, z = saved
    dres32, dres16 = dres
    dz = _mm(f"{tag}_out_dx", dres16, w_out, tb=True)
    d_out = _mm(f"{tag}_out_dw", z, dres16, ta=True)
    dc, db, dv, d_conv = _conv_bwd(f"{tag}_conv_bwd", dz, cbv, w_conv, n_seq, seq)
    dcbv = jnp.concatenate([dc, db, dv], axis=1)
    d_in = _mm(f"{tag}_in_dw", h, dcbv, ta=True)
    dx, dg = _mm_rms_bwd(f"{tag}_in_dx", dcbv, w_in, x, g, dres32)
    return dx, dg, d_in, d_conv, d_out


def _s5_mixer_fwd(tag, x, g, ssm, dskip, w_glu, n_seq, seq):
    a_re, a_im, log_dt, b_re, b_im, c_re, c_im = ssm
    groups, p, hh = b_re.shape
    gb = S5_CHANNELS // hh
    disc, disc_vjp = jax.vjp(_s5_discretize, a_re, a_im, log_dt, b_re, b_im)
    abar_re, abar_im, bbar_re, bbar_im = disc
    mats = (_block_diag_in(bbar_re, gb).astype(BF16), _block_diag_in(bbar_im, gb).astype(BF16),
            _block_diag_out(c_re, gb).astype(BF16), _block_diag_out(c_im, gb).astype(BF16),
            abar_re.reshape(1, groups * p), abar_im.reshape(1, groups * p), dskip)
    d = x.shape[1]
    h = _rms_fwd(f"{tag}_norm", x, g)
    s_re, s_im, ypre, z = _s5_fwd(f"{tag}_scan", h, *mats, n_seq, seq)
    vg, out = _mm(f"{tag}_glu", z, w_glu, epilogue=lambda vg, x: (vg, x + vg[:, :d] * _sigmoid(vg[:, d:])),
                  row_ins=[x], outs=[(2 * d, BF16), (d, F32)])
    return out, (x, h, s_re, s_im, ypre, z, vg, mats, disc_vjp, (groups, p, hh, gb))


def _s5_mixer_bwd(tag, dres, saved, g, w_glu, n_seq, seq):
    x, h, s_re, s_im, ypre, z, vg, mats, disc_vjp, (groups, p, hh, gb) = saved
    d = x.shape[1]
    dres32, _ = dres
    dvg = _glu_bwd(f"{tag}_glu_act_bwd", dres32, vg, d)
    d_glu = _mm(f"{tag}_glu_dw", z, dvg, ta=True)
    dz = _mm(f"{tag}_glu_dx", dvg, w_glu, tb=True)
    dh, dbin_re, dbin_im, dcout_re, dcout_im, dabar_re, dabar_im, d_skip = _s5_bwd(
        f"{tag}_scan_bwd", dz, ypre, h, s_re, s_im, *mats, n_seq, seq)
    d_are, d_aim, d_logdt, d_bre, d_bim = disc_vjp((
        dabar_re.reshape(groups, p), dabar_im.reshape(groups, p),
        _block_diag_in_t(dbin_re, gb, p, hh), _block_diag_in_t(dbin_im, gb, p, hh)))
    d_cre = _block_diag_out_t(dcout_re, gb, p, hh)
    d_cim = _block_diag_out_t(dcout_im, gb, p, hh)
    dx, dg = _rms_bwd(f"{tag}_norm_bwd", x, g, dh, dres32)
    return dx, dg, (d_are, d_aim, d_logdt, d_bre, d_bim, d_cre, d_cim), d_skip, d_glu


def _xattn_fwd(tag, x, mem, g_q, g_mem, w_q, w_kv, w_o, n_seq, seq, mlen, heads):
    q, n = _mm(f"{tag}_q", x, w_q, prologue=_rms_rows, prologue_pars=[g_q])
    mem_n = _rms_fwd(f"{tag}_mem_norm", mem, g_mem)
    kv = _mm(f"{tag}_kv", mem_n, w_kv)
    o = _attn_fwd(f"{tag}_attn", q, kv, n_seq, seq, mlen, heads)
    out = _mm(f"{tag}_o", o, w_o, res=x, out_dtype=F32)
    return out, (x, n, q, mem_n, kv, o)


def _xattn_bwd(tag, dres, saved, mem, g_q, g_mem, w_q, w_kv, w_o, n_seq, seq, mlen, heads):
    x, n, q, mem_n, kv, o = saved
    dres32, dres16 = dres
    do = _mm(f"{tag}_o_dx", dres16, w_o, tb=True)
    d_o = _mm(f"{tag}_o_dw", o, dres16, ta=True)
    dq, dk, dv = _attn_bwd(f"{tag}_attn_bwd", q, kv, do, n_seq, seq, mlen, heads)
    dkv = jnp.concatenate([dk, dv], axis=1)
    d_q = _mm(f"{tag}_q_dw", n, dq, ta=True)
    d_kv = _mm(f"{tag}_kv_dw", mem_n, dkv, ta=True)
    dmem_n = _mm(f"{tag}_kv_dx", dkv, w_kv, tb=True)
    _, dg_mem = _rms_bwd(f"{tag}_mem_norm_bwd", mem, g_mem, dmem_n)
    dx, dg_q = _mm_rms_bwd(f"{tag}_q_dx", dq, w_q, x, g_q, dres32)
    return dx, dg_q, dg_mem, d_q, d_kv, d_o


WEIGHT_NAMES = ("norm_g", "final_g", "ffn1_up", "ffn1_down", "ffn2_up", "ffn2_down", "conv_w_in", "conv_w",
                "conv_w_out", "ssm_a_re", "ssm_a_im", "ssm_log_dt", "ssm_b_re", "ssm_b_im", "ssm_c_re", "ssm_c_im",
                "ssm_d", "ssm_w_glu", "xa_w_q", "xa_w_kv", "xa_w_o")
MATRICES = ("ffn1_up", "ffn1_down", "ffn2_up", "ffn2_down", "conv_w_in", "conv_w_out", "ssm_w_glu", "xa_w_q",
            "xa_w_kv", "xa_w_o")
COLUMN_SHARDED = ("conv_w_in", "ssm_w_glu", "xa_w_kv")
TRANSPOSED = ("ffn1_up", "ffn2_up")
SMALL_SHARDED = ("norm_g", "conv_w", "ssm_d")
REPLICATED = ("ssm_a_re", "ssm_a_im", "ssm_log_dt", "ssm_b_re", "ssm_b_im", "ssm_c_re", "ssm_c_im", "final_g")


def kernel(x, mem, norm_g, final_g, ffn1_up, ffn1_down, ffn2_up, ffn2_down, conv_w_in, conv_w, conv_w_out, ssm_a_re, ssm_a_im, ssm_log_dt, ssm_b_re, ssm_b_im, ssm_c_re, ssm_c_im, ssm_d, ssm_w_glu, xa_w_q, xa_w_kv, xa_w_o, loss_target, m_norm_g, m_final_g, m_ffn1_up, m_ffn1_down, m_ffn2_up, m_ffn2_down, m_conv_w_in, m_conv_w, m_conv_w_out, m_ssm_a_re, m_ssm_a_im, m_ssm_log_dt, m_ssm_b_re, m_ssm_b_im, m_ssm_c_re, m_ssm_c_im, m_ssm_d, m_ssm_w_glu, m_xa_w_q, m_xa_w_kv, m_xa_w_o, v_norm_g, v_final_g, v_ffn1_up, v_ffn1_down, v_ffn2_up, v_ffn2_down, v_conv_w_in, v_conv_w, v_conv_w_out, v_ssm_a_re, v_ssm_a_im, v_ssm_log_dt, v_ssm_b_re, v_ssm_b_im, v_ssm_c_re, v_ssm_c_im, v_ssm_d, v_ssm_w_glu, v_xa_w_q, v_xa_w_kv, v_xa_w_o):
    w = dict(norm_g=norm_g, final_g=final_g, ffn1_up=ffn1_up, ffn1_down=ffn1_down, ffn2_up=ffn2_up,
             ffn2_down=ffn2_down, conv_w_in=conv_w_in, conv_w=conv_w, conv_w_out=conv_w_out, ssm_a_re=ssm_a_re,
             ssm_a_im=ssm_a_im, ssm_log_dt=ssm_log_dt, ssm_b_re=ssm_b_re, ssm_b_im=ssm_b_im, ssm_c_re=ssm_c_re,
             ssm_c_im=ssm_c_im, ssm_d=ssm_d, ssm_w_glu=ssm_w_glu, xa_w_q=xa_w_q, xa_w_kv=xa_w_kv, xa_w_o=xa_w_o)
    mom = dict(norm_g=m_norm_g, final_g=m_final_g, ffn1_up=m_ffn1_up, ffn1_down=m_ffn1_down, ffn2_up=m_ffn2_up,
               ffn2_down=m_ffn2_down, conv_w_in=m_conv_w_in, conv_w=m_conv_w, conv_w_out=m_conv_w_out,
               ssm_a_re=m_ssm_a_re, ssm_a_im=m_ssm_a_im, ssm_log_dt=m_ssm_log_dt, ssm_b_re=m_ssm_b_re,
               ssm_b_im=m_ssm_b_im, ssm_c_re=m_ssm_c_re, ssm_c_im=m_ssm_c_im, ssm_d=m_ssm_d, ssm_w_glu=m_ssm_w_glu,
               xa_w_q=m_xa_w_q, xa_w_kv=m_xa_w_kv, xa_w_o=m_xa_w_o)
    var = dict(norm_g=v_norm_g, final_g=v_final_g, ffn1_up=v_ffn1_up, ffn1_down=v_ffn1_down, ffn2_up=v_ffn2_up,
               ffn2_down=v_ffn2_down, conv_w_in=v_conv_w_in, conv_w=v_conv_w, conv_w_out=v_conv_w_out,
               ssm_a_re=v_ssm_a_re, ssm_a_im=v_ssm_a_im, ssm_log_dt=v_ssm_log_dt, ssm_b_re=v_ssm_b_re,
               ssm_b_im=v_ssm_b_im, ssm_c_re=v_ssm_c_re, ssm_c_im=v_ssm_c_im, ssm_d=v_ssm_d, ssm_w_glu=v_ssm_w_glu,
               xa_w_q=v_xa_w_q, xa_w_kv=v_xa_w_kv, xa_w_o=v_xa_w_o)

    n_seq, seq, d = x.shape
    mlen = mem.shape[1]
    depth, n_norms = norm_g.shape[0], norm_g.shape[1]
    heads = 4
    tokens = n_seq * seq
    x2 = x.reshape(tokens, d)
    mem2 = mem.reshape(n_seq * mlen, d)
    tgt2 = loss_target.reshape(tokens, d)

    small_shapes = [w[k].shape for k in SMALL_SHARDED]
    small_rows = [_merge2d(w[k]) for k in SMALL_SHARDED]
    small_counts = [s.shape[0] for s in small_rows]
    small = jnp.concatenate(small_rows, axis=0)
    small = jnp.pad(small, [(0, (-small.shape[0]) % 8), (0, 0)])
    me = _index(*_place())

    def layer_weights(i):
        names = [(k, i) for k in ("ffn1_up", "ffn1_down", "ffn2_up", "ffn2_down", "xa_w_q", "xa_w_kv", "xa_w_o")]
        return names + ([("conv_w_in", i // 2), ("conv_w_out", i // 2)] if i % 2 == 0 else [("ssm_w_glu", i // 2)])

    shards = [[(w[k][idx].T if k in TRANSPOSED else w[k][idx]).astype(BF16) for k, idx in layer_weights(i)]
              for i in range(depth)]
    n_first = 2
    gathered = _all_gather("gather_layer0_ffn1", shards[0][:n_first] + [small])
    small_all = gathered[-1]
    in_flight = [None] * depth
    token = gathered[0]
    in_place = {k for k in COLUMN_SHARDED if w[k].shape[-1] % V7X_LANES == 0}

    def modes(i):
        start = n_first if i == 0 else 0
        return [COLUMNS if k in in_place else BLOCK for k, _ in layer_weights(i)][start:]

    for i in range(depth):
        mine = shards[i][n_first:] if i == 0 else shards[i]
        zones = [_landing(s, me, mode) for s, mode in zip(mine, modes(i))]
        *in_flight[i], token = _exchange_start(f"gather_start_l{i}", mine, zones, [WHOLE] * len(zones), modes(i),
                                               deps=[token], plan=CHIPS)
    passing = [None] * depth

    def pass_on(i, after):
        landed = _exchange_wait(f"gather_wait_l{i}", *in_flight[i], [WHOLE] * len(modes(i)), modes(i), after,
                                plan=CHIPS)
        *passing[i], forward_token = _exchange_start(f"forward_start_l{i}", [], landed, modes(i), modes(i),
                                                     plan=FORWARD)
        return forward_token

    def passed(i, after):
        return _exchange_wait(f"forward_wait_l{i}", *passing[i], modes(i), modes(i), after, plan=FORWARD)

    def small_whole(idx):
        start = sum(small_counts[:idx])
        part = small_all[:, start:start + small_counts[idx]]
        lead = small_shapes[idx][:-1]
        part = part.reshape((N_DEV,) + lead + (part.shape[-1],))
        part = jnp.moveaxis(part, 0, -2)
        return part.reshape(lead + (N_DEV * part.shape[-1],))

    norm_all = small_whole(0)
    conv_all = small_whole(1)
    dskip_all = small_whole(2)

    def whole(names, arrived):
        return {k: blk if k in in_place else _cols_whole(blk) if k in COLUMN_SHARDED else _rows_whole(blk)
                for (k, _), blk in zip(names, arrived)}

    saved = []
    cur = x2
    for i in range(depth):
        g = [norm_all[i, k].reshape(1, d) for k in range(n_norms)]
        j = i // 2
        if i == 0:
            lw = whole(layer_weights(0)[:n_first], gathered[:n_first])
        else:
            lw = whole(layer_weights(i), passed(i, cur))
        cur, s_ffn1 = _ffn_fwd(f"l{i}_ffn1", cur, g[0], lw["ffn1_up"], lw["ffn1_down"], [token] if i == 0 else ())
        if i == 0:
            lw.update(whole(layer_weights(0)[n_first:], passed(0, pass_on(0, cur))))
        if i % 2 == 0:
            lw["conv_w"] = conv_all[j]
            cur, s_mix = _conv_mixer_fwd(f"l{i}_conv", cur, g[1], lw["conv_w_in"], lw["conv_w"], lw["conv_w_out"],
                                         n_seq, seq)
        else:
            ssm = tuple(w[k][j] for k in ("ssm_a_re", "ssm_a_im", "ssm_log_dt", "ssm_b_re", "ssm_b_im",
                                          "ssm_c_re", "ssm_c_im"))
            cur, s_mix = _s5_mixer_fwd(f"l{i}_s5", cur, g[1], ssm, dskip_all[j].reshape(1, d), lw["ssm_w_glu"],
                                       n_seq, seq)
        cur, s_xa = _xattn_fwd(f"l{i}_xa", cur, mem2, g[2], g[3], lw["xa_w_q"], lw["xa_w_kv"], lw["xa_w_o"],
                               n_seq, seq, mlen, heads)
        early_pass = 0 < i < depth - 1
        deps = [pass_on(i + 1, cur)] if early_pass else ()
        cur, s_ffn2 = _ffn_fwd(f"l{i}_ffn2", cur, g[4], lw["ffn2_up"], lw["ffn2_down"], deps)
        if i == 0 and depth > 1:
            pass_on(1, cur)
        saved.append((g, lw, s_ffn1, s_mix, s_xa, s_ffn2))

    dres, err2, d_final = _final_loss("loss_head", cur, final_g.reshape(1, d), tgt2)
    loss = lax.psum(0.5 * jnp.sum(err2) / d, ("x", "y", "c"))

    d_norm = [[None] * n_norms for _ in range(depth)]
    d_conv = [None] * conv_w.shape[0]
    d_skip = [None] * ssm_d.shape[0]
    d_ssm = [None] * ssm_a_re.shape[0]
    leaving = [None] * depth
    deps = ()

    def leave(name, keys, gm, extra=()):
        srcs, src_modes, zones = [], [], []
        for k in keys:
            if k in in_place:
                rows, n = gm[k].shape[0], gm[k].shape[1] // N_DEV
                srcs.append(gm[k])
                src_modes.append(COLUMNS)
                zones.append(_landing(lax.dynamic_slice(gm[k], (0, me * n), (rows, n)), me))
            else:
                srcs.append(_cols_parts(gm[k]) if k in COLUMN_SHARDED else _rows_parts(gm[k]))
                src_modes.append(BLOCK)
                zones.append(_landing(lax.dynamic_index_in_dim(srcs[-1], me, 0, keepdims=False), me))
        for p in extra:
            srcs.append(p)
            src_modes.append(BLOCK)
            zones.append(_landing(lax.dynamic_index_in_dim(p, me, 0, keepdims=False), me))
        land_modes = [BLOCK] * len(srcs)
        *handles, token = _exchange_start(name, srcs, zones, src_modes, land_modes)
        return (*handles, src_modes, land_modes), token

    def small_parts(full):
        lead = full.shape[:-1]
        t = full.reshape(lead + (N_DEV, full.shape[-1] // N_DEV))
        t = jnp.moveaxis(t, -2, 0)
        return t.reshape(N_DEV, -1, t.shape[-1])

    for i in reversed(range(depth)):
        g, lw, s_ffn1, s_mix, s_xa, s_ffn2 = saved[i]
        j = i // 2
        gm = {}
        dres, d_norm[i][4], gm["ffn2_up"], gm["ffn2_down"] = _ffn_bwd(
            f"l{i}_ffn2", dres, s_ffn2, g[4], lw["ffn2_up"], lw["ffn2_down"], deps)
        dres, d_norm[i][2], d_norm[i][3], gm["xa_w_q"], gm["xa_w_kv"], gm["xa_w_o"] = _xattn_bwd(
            f"l{i}_xa", dres, s_xa, mem2, g[2], g[3], lw["xa_w_q"], lw["xa_w_kv"], lw["xa_w_o"], n_seq, seq, mlen,
            heads)
        if i % 2 == 0:
            dres, d_norm[i][1], gm["conv_w_in"], d_conv[j], gm["conv_w_out"] = _conv_mixer_bwd(
                f"l{i}_conv", dres, s_mix, g[1], lw["conv_w_in"], lw["conv_w"], lw["conv_w_out"], n_seq, seq)
        else:
            dres, d_norm[i][1], d_ssm[j], d_skip[j], gm["ssm_w_glu"] = _s5_mixer_bwd(
                f"l{i}_s5", dres, s_mix, g[1], lw["ssm_w_glu"], n_seq, seq)
        upper, token = leave(f"grads_start_l{i}_upper", [k for k, _ in layer_weights(i)[2:]], gm)
        lower = []

        def send_lower(d_up_t, d_down, i=i, lower=lower):
            handles, token = leave(f"grads_start_l{i}_lower", ["ffn1_up", "ffn1_down"],
                                   {"ffn1_up": d_up_t, "ffn1_down": d_down})
            lower.append(handles)
            return [token]

        dres, d_norm[i][0], _, _ = _ffn_bwd(
            f"l{i}_ffn1", dres, s_ffn1, g[0], lw["ffn1_up"], lw["ffn1_down"], [token], after_dw=send_lower)
        leaving[i] = (upper, lower[0])
        deps = ()
        if i == min(1, depth - 1):
            rep_grads = [jnp.stack([d_ssm[j][k] for j in range(len(d_ssm))]) for k in range(7)]
            rep_packed = _pack_rows(rep_grads + [d_final.reshape(-1)])
            *rep_leaving, token = _exchange_start("replicated_grads_start", [rep_packed], [_landing(rep_packed, me)],
                                                  [WHOLE], [BLOCK])
            deps = [token]
    grad_x = dres[0].reshape(n_seq, seq, d)
    d_norm_all = jnp.stack([jnp.concatenate(row, axis=0) for row in d_norm])
    small_g = jnp.concatenate(
        [small_parts(a) for a in (d_norm_all, jnp.stack(d_conv), jnp.concatenate(d_skip, axis=0))], axis=1)
    small_g = jnp.pad(small_g, [(0, 0), (0, (-small_g.shape[1]) % 8), (0, 0)])
    small_leaving, _ = leave("small_grads_start", [], {}, [small_g])
    received = {k: [None] * w[k].shape[0] for k in MATRICES}

    def arrive(i, part, after):
        names = layer_weights(i)[2:] if part == 0 else layer_weights(i)[:2]
        blks = _exchange_wait(f"grads_wait_l{i}_{'upper' if part == 0 else 'lower'}", *leaving[i][part], after)
        for (k, idx), blk in zip(names, blks):
            received[k][idx] = blk
        return [k for k, _ in names]

    for i in range(1, depth):
        arrive(i, 0, dres[0])
        arrive(i, 1, dres[0])
    rep_all = _exchange_wait("replicated_grads_wait", *rep_leaving, [WHOLE], [BLOCK], dres[0])[0]
    rep_shapes = [w[k].shape for k in REPLICATED]
    flat = {k: (_merge2d(w[k]), _merge2d(mom[k]), _merge2d(var[k])) for k in MATRICES}
    late = {k: received[k][0] is None for k in MATRICES}
    early = {}
    for k in MATRICES:
        first = 1 if late[k] else 0
        if first < len(received[k]):
            early[k] = _adamw(f"adamw_{k}_upper", received[k][first:], *flat[k], first_layer=first,
                              transposed=k in TRANSPOSED)
    after = list(early.values())[-1][0] if early else dres[0]
    done = dict(early)
    for part in (0, 1):
        for k in arrive(0, part, after):
            done[k] = _adamw(f"adamw_{k}_l0", received[k][:1], *flat[k], earlier=early.get(k),
                             transposed=k in TRANSPOSED)
            after = done[k][0]
    small_received, = _exchange_wait("small_grads_wait", *small_leaving, after)
    grads, deltas, new_m, new_v = {}, {}, {}, {}
    for k in MATRICES:
        grads[k], deltas[k], new_m[k], new_v[k] = [o.reshape(w[k].shape) for o in done[k]]

    def small_local(src):
        rows = jnp.concatenate([_merge2d(src[k]) for k in SMALL_SHARDED], axis=0)
        return jnp.pad(rows, [(0, (-rows.shape[0]) % 8), (0, 0)])

    out = _adamw("adamw_small", [small_received], small, small_local(mom), small_local(var))
    for res, o in zip((grads, deltas, new_m, new_v), out):
        start = 0
        for k, cnt, shape in zip(SMALL_SHARDED, small_counts, small_shapes):
            res[k] = o[start:start + cnt].reshape(shape)
            start += cnt

    out = _adamw("adamw_replicated", [rep_all], _pack_rows([w[k] for k in REPLICATED]),
                 _pack_rows([mom[k] for k in REPLICATED]), _pack_rows([var[k] for k in REPLICATED]))
    for res, o in zip((grads, deltas, new_m, new_v), out):
        for k, a in zip(REPLICATED, _unpack_rows(o, rep_shapes)):
            res[k] = a

    return (loss, grad_x, *[grads[k] for k in WEIGHT_NAMES], *[deltas[k] for k in WEIGHT_NAMES],
            *[new_m[k] for k in WEIGHT_NAMES], *[new_v[k] for k in WEIGHT_NAMES])
```

```python
import math

import jax
import jax.numpy as jnp
from jax import lax
from jax.experimental import pallas as pl
from jax.experimental.pallas import tpu as pltpu

F32 = jnp.float32
BF16 = jnp.bfloat16
MESH = pl.DeviceIdType.MESH
N_DEV = 8

NORM_EPS = 1e-6
EIG_CLIP = -1e-4
CONV_WIDTH = 3
ADAM_LR = 0.001
ADAM_B1 = 0.9
ADAM_B2 = 0.999
ADAM_EPS = 1e-08
ADAM_WD = 0.01
ADAM_STEP = 10
GELU_C = math.sqrt(2.0 / math.pi)
GELU_A = 0.044715

V7X_LANES = 128
V7X_VMEM_LIMIT = 56 * 1024 * 1024
S5_CHANNELS = 128
PACK_TILE = 8 * V7X_LANES

HBM_SPEC = pl.BlockSpec(memory_space=pltpu.HBM)
ANY_SPEC = pl.BlockSpec(memory_space=pl.ANY)
SEM_SPEC = pl.BlockSpec(memory_space=pltpu.SEMAPHORE)


def _params(n_grid):
    return pltpu.CompilerParams(dimension_semantics=("arbitrary",) * n_grid, vmem_limit_bytes=V7X_VMEM_LIMIT)


def _pick(n, pref, align):
    if n <= pref:
        return n
    t = (pref // align) * align
    while t >= align:
        if n % t == 0:
            return t
        t -= align
    raise ValueError(f"no tile for {n} (pref {pref}, align {align})")


MM_RHS_BLOCK_BYTES = 12 * 1024 * 1024
MM_LHS_BLOCK_BYTES = 6 * 1024 * 1024
MM_ACC_BYTES = 6 * 1024 * 1024
MM_ROWS = 512


MM_EPILOGUE_ACC_BYTES = 3 * 1024 * 1024
MM_EPILOGUE_MIN_ROWS = 256


def _mm_tiles(m, k, n, a_item, b_item, ta, max_rows):
    tn = _pick(n, max(V7X_LANES, MM_RHS_BLOCK_BYTES // (k * b_item)), V7X_LANES)
    rows = min(max_rows, MM_ACC_BYTES // (4 * tn), MM_LHS_BLOCK_BYTES // (k * a_item))
    align = V7X_LANES if ta else 16
    tm = _pick(m, max(align, rows), align)
    return tm, tn


def _store_results(out_refs, n_row, results, first):
    if not isinstance(results, (tuple, list)):
        results = (results,)
    for o, v in zip(out_refs[:n_row], results[:n_row]):
        if isinstance(v, (tuple, list)):
            off = 0
            for piece in v:
                w = piece.shape[1]
                o[:, off:off + w] = piece.astype(o.dtype)
                off += w
        else:
            o[...] = v.astype(o.dtype)
    if len(out_refs) > n_row:
        @pl.when(first)
        def _():
            for o in out_refs[n_row:]:
                o[...] = jnp.zeros_like(o)

        for o, v in zip(out_refs[n_row:], results[n_row:]):
            o[...] += v


def _mm(name, a, b, *, ta=False, tb=False, out_dtype=BF16, res=None, scale=None, deps=(),
        epilogue=None, row_ins=(), par_ins=(), outs=(), acc_outs=(), prologue=None, prologue_pars=()):
    if ta:
        k, m = a.shape
    else:
        m, k = a.shape
    if tb:
        n, k2 = b.shape
    else:
        k2, n = b.shape
    assert k == k2, (name, a.shape, b.shape)
    max_rows = MM_ROWS if epilogue is None else max(MM_EPILOGUE_MIN_ROWS, MM_EPILOGUE_ACC_BYTES // (4 * n))
    tm, tn = _mm_tiles(m, k, n, a.dtype.itemsize, b.dtype.itemsize, ta, max_rows)
    a_spec = pl.BlockSpec((k, tm), lambda j, i: (0, i)) if ta else pl.BlockSpec((tm, k), lambda j, i: (i, 0))
    b_spec = pl.BlockSpec((tn, k), lambda j, i: (j, 0)) if tb else pl.BlockSpec((k, tn), lambda j, i: (0, j))
    o_spec = pl.BlockSpec((tm, tn), lambda j, i: (i, j))
    dims = (((0 if ta else 1,), (1 if tb else 0,)), ((), ()))
    has_res = res is not None
    ins = [a, b] + ([res] if has_res else [])
    specs = [a_spec, b_spec] + ([o_spec] if has_res else [])
    n_mm = len(ins)
    if epilogue is None:
        out_specs, out_shape = [o_spec], [jax.ShapeDtypeStruct((m, n), out_dtype)]
    else:
        assert tn == n, (name, tn, n)
        for r in row_ins:
            arr, cb, cw = r if isinstance(r, tuple) else (r, 0, r.shape[1])
            assert arr.shape[0] == m, (name, arr.shape, m)
            ins.append(arr)
            specs.append(pl.BlockSpec((tm, cw), lambda j, i, cb=cb: (i, cb)))
        for p in par_ins:
            ins.append(p)
            specs.append(pl.BlockSpec(p.shape, lambda j, i: (0, 0)))
        out_specs = [pl.BlockSpec((tm, c), lambda j, i: (i, 0)) for c, _ in outs]
        out_specs += [pl.BlockSpec((r, c), lambda j, i: (0, 0)) for r, c in acc_outs]
        out_shape = [jax.ShapeDtypeStruct((m, c), dt) for c, dt in outs]
        out_shape += [jax.ShapeDtypeStruct((r, c), F32) for r, c in acc_outs]
    n_in = len(ins)
    if prologue is not None:
        assert tn == n and not ta, (name, tn, n, ta)
        for p in prologue_pars:
            ins.append(p)
            specs.append(pl.BlockSpec(p.shape, lambda j, i: (0, 0)))
        out_specs = out_specs + [a_spec]
        out_shape = out_shape + [jax.ShapeDtypeStruct((m, k), BF16)]
    n_pro = len(ins)
    ins += list(deps)
    specs += [ANY_SPEC] * len(deps)

    def body(*refs):
        a_ref, b_ref = refs[0], refs[1]
        out_refs = refs[n_pro + len(deps):]
        if prologue is None:
            lhs = a_ref[...].astype(BF16)
        else:
            lhs = prologue(a_ref[...], *[r[...] for r in refs[n_in:n_pro]]).astype(BF16)
            out_refs[-1][...] = lhs
            out_refs = out_refs[:-1]
        acc = lax.dot_general(lhs, b_ref[...].astype(BF16), dims, preferred_element_type=F32)
        if scale is not None:
            acc = acc * scale
        if has_res:
            acc = acc + refs[2][...].astype(F32)
        if epilogue is None:
            out_refs[0][...] = acc.astype(out_refs[0].dtype)
        else:
            extra = [r[...] for r in refs[n_mm:n_in]]
            _store_results(out_refs, len(outs), epilogue(acc, *extra), pl.program_id(1) == 0)

    out = pl.pallas_call(
        body, name=name, grid=(n // tn, m // tm), in_specs=specs, out_specs=out_specs, out_shape=out_shape,
        compiler_params=_params(2),
    )(*ins)
    return out[0] if epilogue is None and prologue is None else out


def _rowwise(name, fn, rows, row_ins, par_ins, row_outs, acc_outs=(), tm_pref=256):
    tm = _pick(rows, tm_pref, 16)
    in_specs, ins = [], []
    for r in row_ins:
        arr, cb, cw = r if isinstance(r, tuple) else (r, 0, r.shape[1])
        assert arr.shape[0] == rows, (name, arr.shape, rows)
        ins.append(arr)
        in_specs.append(pl.BlockSpec((tm, cw), lambda i, cb=cb: (i, cb)))
    for p in par_ins:
        ins.append(p)
        in_specs.append(pl.BlockSpec(p.shape, lambda i: (0, 0)))
    out_specs = [pl.BlockSpec((tm, c), lambda i: (i, 0)) for c, _ in row_outs]
    out_specs += [pl.BlockSpec((r, c), lambda i: (0, 0)) for r, c in acc_outs]
    out_shape = [jax.ShapeDtypeStruct((rows, c), dt) for c, dt in row_outs]
    out_shape += [jax.ShapeDtypeStruct((r, c), F32) for r, c in acc_outs]
    n_in, n_row = len(ins), len(row_outs)

    def body(*refs):
        vals = [r[...] for r in refs[:n_in]]
        _store_results(refs[n_in:], n_row, fn(*vals), pl.program_id(0) == 0)

    return pl.pallas_call(
        body, name=name, grid=(rows // tm,), in_specs=in_specs, out_specs=out_specs, out_shape=out_shape,
        compiler_params=_params(1),
    )(*ins)


def _inv_rms(x):
    return lax.rsqrt(jnp.mean(x * x, axis=-1, keepdims=True) + NORM_EPS)


def _rms_rows(x, g):
    return x * _inv_rms(x) * g


def _rms_fwd(name, x, g):
    return _rowwise(name, _rms_rows, x.shape[0], [x], [g], [(x.shape[1], BF16)], tm_pref=512)[0]


def _rms_bwd_rows(dn, x, dres, g):
    r = _inv_rms(x)
    xh = x * r
    dg = jnp.sum(dn * xh, axis=0, keepdims=True)
    dxh = dn * g
    dx = r * (dxh - xh * jnp.mean(dxh * xh, axis=-1, keepdims=True)) + dres
    return dx, dx, dg


def _rms_bwd(name, x, g, dn, dres=None):
    d = x.shape[1]
    if dres is None:
        def fn(x, dn, g):
            return (jnp.sum(dn.astype(F32) * (x * _inv_rms(x)), axis=0, keepdims=True),)

        return None, _rowwise(name, fn, x.shape[0], [x, dn], [g], [], [(1, d)])[0]

    def fn(x, dn, dres, g):
        return _rms_bwd_rows(dn.astype(F32), x, dres, g)

    out = _rowwise(name, fn, x.shape[0], [x, dn, dres], [g], [(d, F32), (d, BF16)], [(1, d)])
    return (out[0], out[1]), out[2]


def _mm_rms_bwd(name, dy, w, x, g, dres, tb=True, deps=()):
    d = x.shape[1]
    out = _mm(name, dy, w, tb=tb, deps=deps, epilogue=_rms_bwd_rows, row_ins=[x, dres], par_ins=[g],
              outs=[(d, F32), (d, BF16)], acc_outs=[(1, d)])
    return (out[0], out[1]), out[2]


def _sigmoid(x):
    return 0.5 + 0.5 * jnp.tanh(0.5 * x)


def _swiglu_rows(gu, f):
    gt = gu[:, :f]
    return gu, gt * _sigmoid(gt) * gu[:, f:]


def _swiglu_bwd_rows(dact, gt, up):
    gt, up = gt.astype(F32), up.astype(F32)
    sg = _sigmoid(gt)
    return ((dact * up * (sg * (1.0 + gt * (1.0 - sg))), dact * (gt * sg)),)


def _glu_bwd(name, dres, vg, d):
    def fn(dres, val, gate):
        val, gate = val.astype(F32), gate.astype(F32)
        sg = _sigmoid(gate)
        return ((dres * sg, dres * val * sg * (1.0 - sg)),)

    return _rowwise(name, fn, dres.shape[0], [dres, (vg, 0, d), (vg, 1, d)], [], [(2 * d, BF16)])[0]


def _final_loss(name, x, g, tgt):
    d = x.shape[1]

    def fn(x, tgt, g):
        r = _inv_rms(x)
        xh = x * r
        err = xh * g - tgt
        dy = err * (1.0 / d)
        dxh = dy * g
        dx = r * (dxh - xh * jnp.mean(dxh * xh, axis=-1, keepdims=True))
        return dx, dx, jnp.sum(err * err, axis=0, keepdims=True), jnp.sum(dy * xh, axis=0, keepdims=True)

    dx, dx16, err2, dg = _rowwise(name, fn, x.shape[0], [x, tgt], [g], [(d, F32), (d, BF16)], [(1, d), (1, d)])
    return (dx, dx16), err2, dg


def _shift_down(u, k):
    rows = lax.broadcasted_iota(jnp.int32, u.shape, 0)
    return jnp.where(rows >= k, pltpu.roll(u, k, 0), 0.0)


def _shift_up(u, k):
    n = u.shape[0]
    rows = lax.broadcasted_iota(jnp.int32, u.shape, 0)
    return jnp.where(rows < n - k, pltpu.roll(u, n - k, 0), 0.0)


def _conv_specs(seq, cw, n_cb, swap):
    def at(off):
        if swap:
            return pl.BlockSpec((seq, cw), lambda j, b: (b, off * n_cb + j))
        return pl.BlockSpec((seq, cw), lambda b, j: (b, off * n_cb + j))

    return at


def _conv_fwd(name, cbv, w, n_seq, seq):
    d = w.shape[1]
    cw = _pick(d, 256, V7X_LANES)
    n_cb = d // cw
    at = _conv_specs(seq, cw, n_cb, swap=False)

    def body(c_ref, b_ref, v_ref, w_ref, z_ref):
        u = c_ref[...].astype(F32) * v_ref[...].astype(F32)
        cv = w_ref[0:1, :] * _shift_down(u, 2) + w_ref[1:2, :] * _shift_down(u, 1) + w_ref[2:3, :] * u
        z_ref[...] = (b_ref[...].astype(F32) * cv).astype(z_ref.dtype)

    return pl.pallas_call(
        body, name=name, grid=(n_seq, n_cb),
        in_specs=[at(0), at(1), at(2), pl.BlockSpec((CONV_WIDTH, cw), lambda b, j: (0, j))],
        out_specs=at(0), out_shape=jax.ShapeDtypeStruct((n_seq * seq, d), BF16), compiler_params=_params(2),
    )(cbv, cbv, cbv, w)


def _conv_bwd(name, dz, cbv, w, n_seq, seq):
    d = w.shape[1]
    cw = _pick(d, 256, V7X_LANES)
    n_cb = d // cw
    at = _conv_specs(seq, cw, n_cb, swap=True)

    def body(dz_ref, c_ref, b_ref, v_ref, w_ref, dc_ref, db_ref, dv_ref, dw_ref):
        c, b, v = c_ref[...].astype(F32), b_ref[...].astype(F32), v_ref[...].astype(F32)
        dz = dz_ref[...].astype(F32)
        w0, w1, w2 = w_ref[0:1, :], w_ref[1:2, :], w_ref[2:3, :]
        u = c * v
        u1, u2 = _shift_down(u, 1), _shift_down(u, 2)
        cv = w0 * u2 + w1 * u1 + w2 * u
        db_ref[...] = (dz * cv).astype(db_ref.dtype)
        dcv = dz * b
        du = w2 * dcv + w1 * _shift_up(dcv, 1) + w0 * _shift_up(dcv, 2)
        dc_ref[...] = (du * v).astype(dc_ref.dtype)
        dv_ref[...] = (du * c).astype(dv_ref.dtype)

        @pl.when(pl.program_id(1) == 0)
        def _():
            dw_ref[...] = jnp.zeros_like(dw_ref)

        dw_ref[0:1, :] += jnp.sum(dcv * u2, axis=0, keepdims=True)
        dw_ref[1:2, :] += jnp.sum(dcv * u1, axis=0, keepdims=True)
        dw_ref[2:3, :] += jnp.sum(dcv * u, axis=0, keepdims=True)

    act = jax.ShapeDtypeStruct((n_seq * seq, d), BF16)
    return pl.pallas_call(
        body, name=name, grid=(n_cb, n_seq),
        in_specs=[at(0), at(0), at(1), at(2), pl.BlockSpec((CONV_WIDTH, cw), lambda j, b: (0, j))],
        out_specs=[at(0), at(0), at(0), pl.BlockSpec((CONV_WIDTH, cw), lambda j, b: (0, j))],
        out_shape=[act, act, act, jax.ShapeDtypeStruct((CONV_WIDTH, d), F32)], compiler_params=_params(2),
    )(dz, cbv, cbv, cbv, w)


def _s5_discretize(a_re, a_im, log_dt, b_re, b_im):
    lam_re = jnp.minimum(a_re, EIG_CLIP)
    lam_im = a_im
    dt = jnp.exp(log_dt)[:, None]
    mag = jnp.exp(lam_re * dt)
    abar_re = mag * jnp.cos(lam_im * dt)
    abar_im = mag * jnp.sin(lam_im * dt)
    den = lam_re * lam_re + lam_im * lam_im
    num_re = abar_re - 1.0
    num_im = abar_im
    coef_re = (num_re * lam_re + num_im * lam_im) / den
    coef_im = (num_im * lam_re - num_re * lam_im) / den
    bbar_re = coef_re[..., None] * b_re - coef_im[..., None] * b_im
    bbar_im = coef_re[..., None] * b_im + coef_im[..., None] * b_re
    return abar_re, abar_im, bbar_re, bbar_im


def _block_diag_in(bbar, gb):
    g, p, h = bbar.shape
    t = jnp.transpose(bbar.reshape(g // gb, gb, p, h), (0, 1, 3, 2))
    return jnp.einsum("cghp,gk->cghkp", t, jnp.eye(gb, dtype=bbar.dtype)).reshape(g // gb, gb * h, gb * p)


def _block_diag_in_t(blk, gb, p, h):
    nb = blk.shape[0]
    t = jnp.einsum("cghkp,gk->cghp", blk.reshape(nb, gb, h, gb, p), jnp.eye(gb, dtype=blk.dtype))
    return jnp.transpose(t, (0, 1, 3, 2)).reshape(nb * gb, p, h)


def _block_diag_out(c, gb):
    g, h, p = c.shape
    t = jnp.transpose(c.reshape(g // gb, gb, h, p), (0, 1, 3, 2))
    return jnp.einsum("cgph,gk->cgpkh", t, jnp.eye(gb, dtype=c.dtype)).reshape(g // gb, gb * p, gb * h)


def _block_diag_out_t(blk, gb, p, h):
    nb = blk.shape[0]
    t = jnp.einsum("cgpkh,gk->cgph", blk.reshape(nb, gb, p, gb, h), jnp.eye(gb, dtype=blk.dtype))
    return jnp.transpose(t, (0, 1, 3, 2)).reshape(nb * gb, h, p)


def _gelu(y):
    return 0.5 * y * (1.0 + jnp.tanh(GELU_C * (y + GELU_A * y * y * y)))


def _gelu_grad(y):
    th = jnp.tanh(GELU_C * (y + GELU_A * y * y * y))
    return 0.5 * (1.0 + th) + 0.5 * y * (1.0 - th * th) * GELU_C * (1.0 + 3.0 * GELU_A * y * y)


def _dot(a, b, ca, cb):
    return lax.dot_general(a.astype(BF16), b.astype(BF16), (((ca,), (cb,)), ((), ())), preferred_element_type=F32)


def _s5_specs(seq, ch, sb):
    act = pl.BlockSpec((seq, ch), lambda j, b: (b, j))
    state = pl.BlockSpec((seq, sb), lambda j, b: (b, j))
    w_in = pl.BlockSpec((None, ch, sb), lambda j, b: (j, 0, 0))
    w_out = pl.BlockSpec((None, sb, ch), lambda j, b: (j, 0, 0))
    lane_s = pl.BlockSpec((1, sb), lambda j, b: (0, j))
    lane_c = pl.BlockSpec((1, ch), lambda j, b: (0, j))
    return act, state, w_in, w_out, lane_s, lane_c


def _s5_fwd(name, h, bin_re, bin_im, cout_re, cout_im, abar_re, abar_im, dskip, n_seq, seq):
    t, d = h.shape
    nb, ch, sb = bin_re.shape
    act, state, w_in, w_out, lane_s, lane_c = _s5_specs(seq, ch, sb)

    def body(h_ref, bre_ref, bim_ref, cre_ref, cim_ref, ar_ref, ai_ref, d_ref, sre_ref, sim_ref, y_ref, z_ref):
        u = h_ref[...]
        sre_ref[...] = _dot(u, bre_ref[...], 1, 0)
        sim_ref[...] = _dot(u, bim_ref[...], 1, 0)
        ar, ai = ar_ref[...], ai_ref[...]

        def step(i, carry):
            sr, si = carry
            row = pl.ds(i, 1)
            nr = ar * sr - ai * si + sre_ref[row, :]
            ni = ar * si + ai * sr + sim_ref[row, :]
            sre_ref[row, :] = nr
            sim_ref[row, :] = ni
            return nr, ni

        zero = jnp.zeros((1, sb), F32)
        lax.fori_loop(0, seq, step, (zero, zero), unroll=8)
        y = _dot(sre_ref[...], cre_ref[...], 1, 0) - _dot(sim_ref[...], cim_ref[...], 1, 0)
        y = y + d_ref[...] * u.astype(F32)
        y_ref[...] = y
        z_ref[...] = _gelu(y).astype(z_ref.dtype)

    return pl.pallas_call(
        body, name=name, grid=(nb, n_seq),
        in_specs=[act, w_in, w_in, w_out, w_out, lane_s, lane_s, lane_c],
        out_specs=[state, state, act, act],
        out_shape=[jax.ShapeDtypeStruct((t, nb * sb), F32), jax.ShapeDtypeStruct((t, nb * sb), F32),
                   jax.ShapeDtypeStruct((t, d), F32), jax.ShapeDtypeStruct((t, d), BF16)],
        compiler_params=_params(2),
    )(h, bin_re, bin_im, cout_re, cout_im, abar_re, abar_im, dskip)


def _s5_bwd(name, dz, ypre, h, s_re, s_im, bin_re, bin_im, cout_re, cout_im, abar_re, abar_im, dskip, n_seq, seq):
    t, d = h.shape
    nb, ch, sb = bin_re.shape
    act, state, w_in, w_out, lane_s, lane_c = _s5_specs(seq, ch, sb)

    def body(dz_ref, y_ref, h_ref, sre_ref, sim_ref, bre_ref, bim_ref, cre_ref, cim_ref, ar_ref, ai_ref, d_ref,
             dh_ref, dbre_ref, dbim_ref, dcre_ref, dcim_ref, dar_ref, dai_ref, dd_ref, gre, gim):
        first = pl.program_id(1) == 0
        u = h_ref[...].astype(F32)
        dy = dz_ref[...].astype(F32) * _gelu_grad(y_ref[...])
        gre[...] = _dot(dy, cre_ref[...], 1, 1)
        gim[...] = -_dot(dy, cim_ref[...], 1, 1)
        ar, ai = ar_ref[...], ai_ref[...]

        def step(i, carry):
            gr, gi = carry
            row = pl.ds(seq - 1 - i, 1)
            nr = gre[row, :] + ar * gr + ai * gi
            ni = gim[row, :] - ai * gr + ar * gi
            gre[row, :] = nr
            gim[row, :] = ni
            return nr, ni

        zero = jnp.zeros((1, sb), F32)
        lax.fori_loop(0, seq, step, (zero, zero), unroll=8)

        g_re, g_im = gre[...], gim[...]
        s_re, s_im = sre_ref[...], sim_ref[...]
        p_re, p_im = _shift_down(s_re, 1), _shift_down(s_im, 1)
        dar = jnp.sum(g_re * p_re + g_im * p_im, axis=0, keepdims=True)
        dai = jnp.sum(g_im * p_re - g_re * p_im, axis=0, keepdims=True)
        dbre = _dot(u, g_re, 0, 0)
        dbim = _dot(u, g_im, 0, 0)
        dcre = _dot(s_re, dy, 0, 0)
        dcim = -_dot(s_im, dy, 0, 0)
        ddd = jnp.sum(dy * u, axis=0, keepdims=True)
        dh_ref[...] = _dot(g_re, bre_ref[...], 1, 1) + _dot(g_im, bim_ref[...], 1, 1) + d_ref[...] * dy

        @pl.when(first)
        def _():
            dar_ref[...] = dar
            dai_ref[...] = dai
            dbre_ref[...] = dbre
            dbim_ref[...] = dbim
            dcre_ref[...] = dcre
            dcim_ref[...] = dcim
            dd_ref[...] = ddd

        @pl.when(jnp.logical_not(first))
        def _():
            dar_ref[...] += dar
            dai_ref[...] += dai
            dbre_ref[...] += dbre
            dbim_ref[...] += dbim
            dcre_ref[...] += dcre
            dcim_ref[...] += dcim
            dd_ref[...] += ddd

    return pl.pallas_call(
        body, name=name, grid=(nb, n_seq),
        in_specs=[act, act, act, state, state, w_in, w_in, w_out, w_out, lane_s, lane_s, lane_c],
        out_specs=[act, w_in, w_in, w_out, w_out, lane_s, lane_s, lane_c],
        out_shape=[jax.ShapeDtypeStruct((t, d), F32),
                   jax.ShapeDtypeStruct((nb, ch, sb), F32), jax.ShapeDtypeStruct((nb, ch, sb), F32),
                   jax.ShapeDtypeStruct((nb, sb, ch), F32), jax.ShapeDtypeStruct((nb, sb, ch), F32),
                   jax.ShapeDtypeStruct((1, nb * sb), F32), jax.ShapeDtypeStruct((1, nb * sb), F32),
                   jax.ShapeDtypeStruct((1, d), F32)],
        scratch_shapes=[pltpu.VMEM((seq, sb), F32), pltpu.VMEM((seq, sb), F32)],
        compiler_params=_params(2),
    )(dz, ypre, h, s_re, s_im, bin_re, bin_im, cout_re, cout_im, abar_re, abar_im, dskip)


ATTN_QUERY_ROWS = 2048


def _softmax_rows(q, k, scale):
    s = _dot(q, k, 1, 1) * scale
    e = jnp.exp(s - jnp.max(s, axis=-1, keepdims=True))
    return e * (1.0 / jnp.sum(e, axis=-1, keepdims=True))


def _attn_fwd(name, q, kv, n_seq, seq, mlen, heads):
    t, d = q.shape
    hd = d // heads
    tq = _pick(seq, ATTN_QUERY_ROWS, 16)
    nq = seq // tq
    scale = hd ** -0.5
    q_spec = pl.BlockSpec((tq, d), lambda b, i: (b * nq + i, 0))

    def body(q_ref, k_ref, v_ref, o_ref):
        for h in range(heads):
            cols = slice(h * hd, (h + 1) * hd)
            p = _softmax_rows(q_ref[:, cols], k_ref[:, cols], scale)
            o_ref[:, cols] = _dot(p, v_ref[:, cols], 1, 0).astype(o_ref.dtype)

    return pl.pallas_call(
        body, name=name, grid=(n_seq, nq),
        in_specs=[q_spec, pl.BlockSpec((mlen, d), lambda b, i: (b, 0)), pl.BlockSpec((mlen, d), lambda b, i: (b, 1))],
        out_specs=q_spec, out_shape=jax.ShapeDtypeStruct((t, d), BF16), compiler_params=_params(2),
    )(q, kv, kv)


def _attn_bwd(name, q, kv, do, n_seq, seq, mlen, heads):
    t, d = q.shape
    hd = d // heads
    tq = _pick(seq, ATTN_QUERY_ROWS, 16)
    nq = seq // tq
    scale = hd ** -0.5
    q_spec = pl.BlockSpec((tq, d), lambda b, i: (b * nq + i, 0))
    k_spec = pl.BlockSpec((mlen, d), lambda b, i: (b, 0))

    def body(q_ref, k_ref, v_ref, do_ref, dq_ref, dk_ref, dv_ref):
        @pl.when(pl.program_id(1) == 0)
        def _():
            dk_ref[...] = jnp.zeros_like(dk_ref)
            dv_ref[...] = jnp.zeros_like(dv_ref)

        for h in range(heads):
            cols = slice(h * hd, (h + 1) * hd)
            q, k, v, do = q_ref[:, cols], k_ref[:, cols], v_ref[:, cols], do_ref[:, cols]
            p = _softmax_rows(q, k, scale)
            dp = _dot(do, v, 1, 1)
            ds = p * (dp - jnp.sum(dp * p, axis=-1, keepdims=True)) * scale
            dq_ref[:, cols] = _dot(ds, k, 1, 0).astype(dq_ref.dtype)
            dk_ref[:, cols] += _dot(ds, q, 0, 0)
            dv_ref[:, cols] += _dot(p, do, 0, 0)

    return pl.pallas_call(
        body, name=name, grid=(n_seq, nq),
        in_specs=[q_spec, k_spec, pl.BlockSpec((mlen, d), lambda b, i: (b, 1)), q_spec],
        out_specs=[q_spec, k_spec, k_spec],
        out_shape=[jax.ShapeDtypeStruct((t, d), BF16), jax.ShapeDtypeStruct((n_seq * mlen, d), F32),
                   jax.ShapeDtypeStruct((n_seq * mlen, d), F32)],
        compiler_params=_params(2),
    )(q, kv, kv, do)


ADAMW_BLOCK_ELEMS = 128 * 1024


def _adamw(name, parts, w, m, v, first_layer=0, earlier=None, transposed=False):
    n_layers = len(parts)
    r, c = parts[0].shape[1:][::-1] if transposed else parts[0].shape[1:]
    assert w.shape[0] % r == 0 and w.shape[1] == c and first_layer + n_layers <= w.shape[0] // r, (name, w.shape)
    tr = _pick(r, max(V7X_LANES, ADAMW_BLOCK_ELEMS // c // V7X_LANES * V7X_LANES), V7X_LANES if transposed else 8)
    nt = r // tr
    spec = pl.BlockSpec((tr, c), lambda l, i: ((first_layer + l) * nt + i, 0))
    c1 = 1.0 - ADAM_B1 ** ADAM_STEP
    c2 = 1.0 - ADAM_B2 ** ADAM_STEP

    def parts_spec(q):
        def at(l, i):
            return jnp.where(l == q, i, jnp.where(l > q, nt - 1, 0))

        if transposed:
            return pl.BlockSpec((N_DEV, c, tr), lambda l, i: (0, 0, at(l, i)))
        return pl.BlockSpec((N_DEV, tr, c), lambda l, i: (0, at(l, i), 0))

    earlier = list(earlier or ())

    def body(*refs):
        p_refs = refs[:n_layers]
        w_ref, m_ref, v_ref = refs[n_layers:n_layers + 3]
        g_ref, d_ref, nm_ref, nv_ref = refs[n_layers + 3 + len(earlier):]

        def update(p_ref):
            g = p_ref[0].astype(F32)
            for k in range(1, N_DEV):
                g = g + p_ref[k].astype(F32)
            if transposed:
                g = g.T
            nm = ADAM_B1 * m_ref[...] + (1.0 - ADAM_B1) * g
            nv = ADAM_B2 * v_ref[...] + (1.0 - ADAM_B2) * (g * g)
            g_ref[...] = g
            nm_ref[...] = nm
            nv_ref[...] = nv
            d_ref[...] = -ADAM_LR * ((nm / c1) / (jnp.sqrt(nv / c2) + ADAM_EPS) + ADAM_WD * w_ref[...])

        for q in range(n_layers):
            pl.when(pl.program_id(0) == q)(lambda q=q: update(p_refs[q]))

    out = jax.ShapeDtypeStruct(w.shape, F32)
    return pl.pallas_call(
        body, name=name, grid=(n_layers, nt),
        in_specs=[parts_spec(q) for q in range(n_layers)] + [spec] * 3 + [ANY_SPEC] * len(earlier),
        out_specs=[spec] * 4, out_shape=[out] * 4, compiler_params=_params(2),
        input_output_aliases={n_layers + 3 + q: q for q in range(len(earlier))},
    )(*parts, w, m, v, *earlier)


def _place():
    x, y, c = lax.axis_index("x"), lax.axis_index("y"), lax.axis_index("c")
    return x, y, c


def _index(px, py, pc):
    return 4 * px + 2 * py + pc


def _all_gather(name, shards):
    n = len(shards)

    def body(*refs):
        in_refs, out_refs = refs[:n], refs[n:2 * n]
        send_sems, recv_sems, local_sems = refs[2 * n:]
        x, y, c = _place()
        me, sibling = (x, y, c), (x, y, 1 - c)
        chips = [(1 - x, y), (x, 1 - y), (1 - x, 1 - y)]

        def slot(k, block):
            return out_refs[k].at[_index(*block)]

        def copy(k, j, block, to, src=None):
            return pltpu.make_async_remote_copy(
                src_ref=slot(k, block) if src is None else src, dst_ref=slot(k, block),
                send_sem=send_sems.at[7 * k + j], recv_sem=recv_sems.at[7 * k + j], device_id=to, device_id_type=MESH)

        mine = [pltpu.make_async_copy(in_refs[k], slot(k, me), local_sems.at[k]) for k in range(n)]
        for cp in mine:
            cp.start()
        first = []
        for k in range(n):
            first.append(copy(k, 0, me, sibling, src=in_refs[k]))
            first += [copy(k, 1 + j, me, (*chip, c), src=in_refs[k]) for j, chip in enumerate(chips)]
        for cp in first:
            cp.start()
        passed = []
        for j, chip in enumerate(chips):
            for k in range(n):
                copy(k, 1 + j, (*chip, c), me).wait_recv()
                cp = copy(k, 4 + j, (*chip, c), sibling)
                cp.start()
                passed.append(cp)
        for k in range(n):
            copy(k, 0, sibling, me).wait_recv()
            for j, chip in enumerate(chips):
                copy(k, 4 + j, (*chip, 1 - c), me).wait_recv()
        for cp in first + passed:
            cp.wait_send()
        for cp in mine:
            cp.wait()

    return pl.pallas_call(
        body, name=name, in_specs=[HBM_SPEC] * n, out_specs=[HBM_SPEC] * n,
        out_shape=[jax.ShapeDtypeStruct((N_DEV,) + s.shape, s.dtype) for s in shards],
        scratch_shapes=[pltpu.SemaphoreType.DMA((7 * n,)), pltpu.SemaphoreType.DMA((7 * n,)),
                        pltpu.SemaphoreType.DMA((n,))],
    )(*shards)


WHOLE, BLOCK, COLUMNS = "whole", "block", "columns"


def _slot(ref, index, mode):
    if mode == WHOLE:
        return ref
    if mode == BLOCK:
        return ref.at[index]
    width = ref.shape[1] // N_DEV
    return ref.at[:, pl.ds(pl.multiple_of(index * width, width), width)]


DIRECT = tuple(range(1, N_DEV))
CHIPS = (1, 4, 2, 6)
FORWARD = "forward"


def _copies(plan, x, y, c):
    def xor(r, flip_core=False):
        rx, ry, rc = (r >> 2) & 1, (r >> 1) & 1, (r & 1) ^ int(flip_core)
        return (1 - x if rx else x, 1 - y if ry else y, 1 - c if rc else c)

    me = _index(x, y, c)
    if plan == FORWARD:
        return [(xor(1), _index(*xor(r)), _index(*xor(r)), _index(*xor(r, True))) for r in (4, 2, 6)]
    return [(xor(r), _index(*xor(r)), me, _index(*xor(r))) for r in plan]


def _exchange_start(name, srcs, lands, src_modes, land_modes, deps=(), plan=DIRECT):
    n, n_src = len(lands), len(srcs)
    n_copies = 3 if plan == FORWARD else len(plan)

    def body(*refs):
        land_refs = refs[n_src:n_src + n]
        src_refs = refs[:n_src] if n_src else land_refs
        send_sems, recv_sems = refs[n_src + n + len(deps)], refs[n_src + n + len(deps) + 1]
        token = refs[-1]
        x, y, c = _place()
        for k in range(n):
            for j, (peer, src_index, dst_index, _) in enumerate(_copies(plan, x, y, c)):
                pltpu.make_async_remote_copy(
                    src_ref=_slot(src_refs[k], src_index, src_modes[k]),
                    dst_ref=_slot(land_refs[k], dst_index, land_modes[k]), send_sem=send_sems.at[n_copies * k + j],
                    recv_sem=recv_sems.at[n_copies * k + j], device_id=peer, device_id_type=MESH).start()
        token[...] = jnp.zeros_like(token)

    arrays = list(srcs) + list(lands)
    thru = [pltpu.HBM(a.shape, a.dtype) for a in arrays]
    out = pl.pallas_call(
        body, name=name,
        out_shape=(pltpu.SemaphoreType.DMA((n_copies * n,)), pltpu.SemaphoreType.DMA((n_copies * n,)), *thru,
                   jax.ShapeDtypeStruct((8, V7X_LANES), F32)),
        in_specs=[HBM_SPEC] * len(arrays) + [ANY_SPEC] * len(deps),
        out_specs=(SEM_SPEC, SEM_SPEC, *([HBM_SPEC] * len(arrays)), pl.BlockSpec(memory_space=pltpu.VMEM)),
        input_output_aliases={k: 2 + k for k in range(len(arrays))},
        compiler_params=pltpu.CompilerParams(has_side_effects=pltpu.SideEffectType.DATAFLOW_SIDE_EFFECTING),
    )(*[pltpu.with_memory_space_constraint(a, pltpu.HBM) for a in arrays], *deps)
    return out[0], out[1], list(out[2:2 + n_src]), list(out[2 + n_src:2 + n_src + n]), out[-1]


def _exchange_wait(name, send_sems, recv_sems, srcs, lands, src_modes, land_modes, after, plan=DIRECT):
    n, n_src = len(lands), len(srcs)
    n_copies = 3 if plan == FORWARD else len(plan)

    def body(*refs):
        land_refs = refs[n_src:n_src + n]
        src_refs = refs[:n_src] if n_src else land_refs
        send_sems, recv_sems = refs[n_src + n], refs[n_src + n + 1]
        x, y, c = _place()
        for k in range(n):
            for j, (peer, src_index, _, arrival_index) in enumerate(_copies(plan, x, y, c)):
                cp = pltpu.make_async_remote_copy(
                    src_ref=_slot(src_refs[k], src_index, src_modes[k]),
                    dst_ref=_slot(land_refs[k], arrival_index, land_modes[k]), send_sem=send_sems.at[n_copies * k + j],
                    recv_sem=recv_sems.at[n_copies * k + j], device_id=peer, device_id_type=MESH)
                cp.wait_send()
                cp.wait_recv()

    arrays = list(srcs) + list(lands)
    thru = [pltpu.HBM(a.shape, a.dtype) for a in arrays]
    out = pl.pallas_call(
        body, name=name, out_shape=tuple(thru),
        in_specs=[HBM_SPEC] * len(arrays) + [SEM_SPEC, SEM_SPEC, ANY_SPEC], out_specs=tuple([HBM_SPEC] * len(arrays)),
        input_output_aliases={k: k for k in range(len(arrays))},
        compiler_params=pltpu.CompilerParams(has_side_effects=pltpu.SideEffectType.DATAFLOW_SIDE_EFFECTING),
    )(*arrays, send_sems, recv_sems, after)
    return list(out[n_src:])


def _landing(shard, me, mode=BLOCK):
    if mode == COLUMNS:
        k, n = shard.shape
        return lax.dynamic_update_slice(lax.empty((k, N_DEV * n), shard.dtype), shard, (0, me * n))
    zone = lax.empty((N_DEV,) + shard.shape, shard.dtype)
    return lax.dynamic_update_slice(zone, shard[None], (me,) + (0,) * shard.ndim)


def _cols_whole(w):
    return jnp.transpose(w, (1, 0, 2)).reshape(w.shape[1], N_DEV * w.shape[2])


def _rows_whole(w):
    return w.reshape(N_DEV * w.shape[1], w.shape[2])


def _cols_parts(dw):
    k, n8 = dw.shape
    return jnp.transpose(dw.reshape(k, N_DEV, n8 // N_DEV), (1, 0, 2))


def _rows_parts(dw):
    r8, c = dw.shape
    return dw.reshape(N_DEV, r8 // N_DEV, c)


def _pack_rows(arrays):
    rows = []
    for a in arrays:
        flat = a.reshape(-1).astype(F32)
        flat = jnp.pad(flat, [(0, (-flat.shape[0]) % PACK_TILE)])
        rows.append(flat.reshape(-1, V7X_LANES))
    return jnp.concatenate(rows, axis=0)


def _unpack_rows(packed, shapes):
    out, row = [], 0
    for s in shapes:
        size = math.prod(s)
        n_rows = -(-size // PACK_TILE) * 8
        out.append(packed[row:row + n_rows].reshape(-1)[:size].reshape(s))
        row += n_rows
    return out


def _merge2d(a):
    return a.reshape(-1, a.shape[-1])


def _ffn_fwd(tag, x, g, w_up_t, w_down, deps=()):
    f = w_down.shape[0]
    gu, act, n = _mm(f"{tag}_up", x, w_up_t, tb=True, prologue=_rms_rows, prologue_pars=[g], deps=deps,
                     epilogue=lambda acc: _swiglu_rows(acc, f), outs=[(2 * f, BF16), (f, BF16)])
    out = _mm(f"{tag}_down", act, w_down, res=x, scale=0.5, out_dtype=F32)
    return out, (x, n, gu, act)


def _ffn_bwd(tag, dres, saved, g, w_up_t, w_down, deps=(), after_dw=None):
    x, n, gu, act = saved
    dres32, dres16 = dres
    f = w_down.shape[0]
    dgu = _mm(f"{tag}_down_dx", dres16, w_down, tb=True, scale=0.5, deps=deps, epilogue=_swiglu_bwd_rows,
              row_ins=[(gu, 0, f), (gu, 1, f)], outs=[(2 * f, BF16)])[0]
    d_down = _mm(f"{tag}_down_dw", act, dres16, ta=True, scale=0.5)
    d_up_t = _mm(f"{tag}_up_dw", dgu, n, ta=True)
    dx, dg = _mm_rms_bwd(f"{tag}_up_dx", dgu, w_up_t, x, g, dres32, tb=False,
                         deps=after_dw(d_up_t, d_down) if after_dw else ())
    return dx, dg, d_up_t, d_down


def _conv_mixer_fwd(tag, x, g, w_in, w_conv, w_out, n_seq, seq):
    cbv, h = _mm(f"{tag}_in", x, w_in, prologue=_rms_rows, prologue_pars=[g])
    z = _conv_fwd(f"{tag}_conv", cbv, w_conv, n_seq, seq)
    out = _mm(f"{tag}_out", z, w_out, res=x, out_dtype=F32)
    return out, (x, h, cbv, z)


def _conv_mixer_bwd(tag, dres, saved, g, w_in, w_conv, w_out, n_seq, seq):
    x, h, cbv, z = saved
    dres32, dres16 = dres
    dz = _mm(f"{tag}_out_dx", dres16, w_out, tb=True)
    d_out = _mm(f"{tag}_out_dw", z, dres16, ta=True)
    dc, db, dv, d_conv = _conv_bwd(f"{tag}_conv_bwd", dz, cbv, w_conv, n_seq, seq)
    dcbv = jnp.concatenate([dc, db, dv], axis=1)
    d_in = _mm(f"{tag}_in_dw", h, dcbv, ta=True)
    dx, dg = _mm_rms_bwd(f"{tag}_in_dx", dcbv, w_in, x, g, dres32)
    return dx, dg, d_in, d_conv, d_out


def _s5_mixer_fwd(tag, x, g, ssm, dskip, w_glu, n_seq, seq):
    a_re, a_im, log_dt, b_re, b_im, c_re, c_im = ssm
    groups, p, hh = b_re.shape
    gb = S5_CHANNELS // hh
    disc, disc_vjp = jax.vjp(_s5_discretize, a_re, a_im, log_dt, b_re, b_im)
    abar_re, abar_im, bbar_re, bbar_im = disc
    mats = (_block_diag_in(bbar_re, gb).astype(BF16), _block_diag_in(bbar_im, gb).astype(BF16),
            _block_diag_out(c_re, gb).astype(BF16), _block_diag_out(c_im, gb).astype(BF16),
            abar_re.reshape(1, groups * p), abar_im.reshape(1, groups * p), dskip)
    d = x.shape[1]
    h = _rms_fwd(f"{tag}_norm", x, g)
    s_re, s_im, ypre, z = _s5_fwd(f"{tag}_scan", h, *mats, n_seq, seq)
    vg, out = _mm(f"{tag}_glu", z, w_glu, epilogue=lambda vg, x: (vg, x + vg[:, :d] * _sigmoid(vg[:, d:])),
                  row_ins=[x], outs=[(2 * d, BF16), (d, F32)])
    return out, (x, h, s_re, s_im, ypre, z, vg, mats, disc_vjp, (groups, p, hh, gb))


def _s5_mixer_bwd(tag, dres, saved, g, w_glu, n_seq, seq):
    x, h, s_re, s_im, ypre, z, vg, mats, disc_vjp, (groups, p, hh, gb) = saved
    d = x.shape[1]
    dres32, _ = dres
    dvg = _glu_bwd(f"{tag}_glu_act_bwd", dres32, vg, d)
    d_glu = _mm(f"{tag}_glu_dw", z, dvg, ta=True)
    dz = _mm(f"{tag}_glu_dx", dvg, w_glu, tb=True)
    dh, dbin_re, dbin_im, dcout_re, dcout_im, dabar_re, dabar_im, d_skip = _s5_bwd(
        f"{tag}_scan_bwd", dz, ypre, h, s_re, s_im, *mats, n_seq, seq)
    d_are, d_aim, d_logdt, d_bre, d_bim = disc_vjp((
        dabar_re.reshape(groups, p), dabar_im.reshape(groups, p),
        _block_diag_in_t(dbin_re, gb, p, hh), _block_diag_in_t(dbin_im, gb, p, hh)))
    d_cre = _block_diag_out_t(dcout_re, gb, p, hh)
    d_cim = _block_diag_out_t(dcout_im, gb, p, hh)
    dx, dg = _rms_bwd(f"{tag}_norm_bwd", x, g, dh, dres32)
    return dx, dg, (d_are, d_aim, d_logdt, d_bre, d_bim, d_cre, d_cim), d_skip, d_glu


def _xattn_fwd(tag, x, mem, g_q, g_mem, w_q, w_kv, w_o, n_seq, seq, mlen, heads):
    q, n = _mm(f"{tag}_q", x, w_q, prologue=_rms_rows, prologue_pars=[g_q])
    mem_n = _rms_fwd(f"{tag}_mem_norm", mem, g_mem)
    kv = _mm(f"{tag}_kv", mem_n, w_kv)
    o = _attn_fwd(f"{tag}_attn", q, kv, n_seq, seq, mlen, heads)
    out = _mm(f"{tag}_o", o, w_o, res=x, out_dtype=F32)
    return out, (x, n, q, mem_n, kv, o)


def _xattn_bwd(tag, dres, saved, mem, g_q, g_mem, w_q, w_kv, w_o, n_seq, seq, mlen, heads):
    x, n, q, mem_n, kv, o = saved
    dres32, dres16 = dres
    do = _mm(f"{tag}_o_dx", dres16, w_o, tb=True)
    d_o = _mm(f"{tag}_o_dw", o, dres16, ta=True)
    dq, dk, dv = _attn_bwd(f"{tag}_attn_bwd", q, kv, do, n_seq, seq, mlen, heads)
    dkv = jnp.concatenate([dk, dv], axis=1)
    d_q = _mm(f"{tag}_q_dw", n, dq, ta=True)
    d_kv = _mm(f"{tag}_kv_dw", mem_n, dkv, ta=True)
    dmem_n = _mm(f"{tag}_kv_dx", dkv, w_kv, tb=True)
    _, dg_mem = _rms_bwd(f"{tag}_mem_norm_bwd", mem, g_mem, dmem_n)
    dx, dg_q = _mm_rms_bwd(f"{tag}_q_dx", dq, w_q, x, g_q, dres32)
    return dx, dg_q, dg_mem, d_q, d_kv, d_o


WEIGHT_NAMES = ("norm_g", "final_g", "ffn1_up", "ffn1_down", "ffn2_up", "ffn2_down", "conv_w_in", "conv_w",
                "conv_w_out", "ssm_a_re", "ssm_a_im", "ssm_log_dt", "ssm_b_re", "ssm_b_im", "ssm_c_re", "ssm_c_im",
                "ssm_d", "ssm_w_glu", "xa_w_q", "xa_w_kv", "xa_w_o")
MATRICES = ("ffn1_up", "ffn1_down", "ffn2_up", "ffn2_down", "conv_w_in", "conv_w_out", "ssm_w_glu", "xa_w_q",
            "xa_w_kv", "xa_w_o")
COLUMN_SHARDED = ("conv_w_in", "ssm_w_glu", "xa_w_kv")
TRANSPOSED = ("ffn1_up", "ffn2_up")
SMALL_SHARDED = ("norm_g", "conv_w", "ssm_d")
REPLICATED = ("ssm_a_re", "ssm_a_im", "ssm_log_dt", "ssm_b_re", "ssm_b_im", "ssm_c_re", "ssm_c_im", "final_g")


def kernel(x, mem, norm_g, final_g, ffn1_up, ffn1_down, ffn2_up, ffn2_down, conv_w_in, conv_w, conv_w_out, ssm_a_re, ssm_a_im, ssm_log_dt, ssm_b_re, ssm_b_im, ssm_c_re, ssm_c_im, ssm_d, ssm_w_glu, xa_w_q, xa_w_kv, xa_w_o, loss_target, m_norm_g, m_final_g, m_ffn1_up, m_ffn1_down, m_ffn2_up, m_ffn2_down, m_conv_w_in, m_conv_w, m_conv_w_out, m_ssm_a_re, m_ssm_a_im, m_ssm_log_dt, m_ssm_b_re, m_ssm_b_im, m_ssm_c_re, m_ssm_c_im, m_ssm_d, m_ssm_w_glu, m_xa_w_q, m_xa_w_kv, m_xa_w_o, v_norm_g, v_final_g, v_ffn1_up, v_ffn1_down, v_ffn2_up, v_ffn2_down, v_conv_w_in, v_conv_w, v_conv_w_out, v_ssm_a_re, v_ssm_a_im, v_ssm_log_dt, v_ssm_b_re, v_ssm_b_im, v_ssm_c_re, v_ssm_c_im, v_ssm_d, v_ssm_w_glu, v_xa_w_q, v_xa_w_kv, v_xa_w_o):
    w = dict(norm_g=norm_g, final_g=final_g, ffn1_up=ffn1_up, ffn1_down=ffn1_down, ffn2_up=ffn2_up,
             ffn2_down=ffn2_down, conv_w_in=conv_w_in, conv_w=conv_w, conv_w_out=conv_w_out, ssm_a_re=ssm_a_re,
             ssm_a_im=ssm_a_im, ssm_log_dt=ssm_log_dt, ssm_b_re=ssm_b_re, ssm_b_im=ssm_b_im, ssm_c_re=ssm_c_re,
             ssm_c_im=ssm_c_im, ssm_d=ssm_d, ssm_w_glu=ssm_w_glu, xa_w_q=xa_w_q, xa_w_kv=xa_w_kv, xa_w_o=xa_w_o)
    mom = dict(norm_g=m_norm_g, final_g=m_final_g, ffn1_up=m_ffn1_up, ffn1_down=m_ffn1_down, ffn2_up=m_ffn2_up,
               ffn2_down=m_ffn2_down, conv_w_in=m_conv_w_in, conv_w=m_conv_w, conv_w_out=m_conv_w_out,
               ssm_a_re=m_ssm_a_re, ssm_a_im=m_ssm_a_im, ssm_log_dt=m_ssm_log_dt, ssm_b_re=m_ssm_b_re,
               ssm_b_im=m_ssm_b_im, ssm_c_re=m_ssm_c_re, ssm_c_im=m_ssm_c_im, ssm_d=m_ssm_d, ssm_w_glu=m_ssm_w_glu,
               xa_w_q=m_xa_w_q, xa_w_kv=m_xa_w_kv, xa_w_o=m_xa_w_o)
    var = dict(norm_g=v_norm_g, final_g=v_final_g, ffn1_up=v_ffn1_up, ffn1_down=v_ffn1_down, ffn2_up=v_ffn2_up,
               ffn2_down=v_ffn2_down, conv_w_in=v_conv_w_in, conv_w=v_conv_w, conv_w_out=v_conv_w_out,
               ssm_a_re=v_ssm_a_re, ssm_a_im=v_ssm_a_im, ssm_log_dt=v_ssm_log_dt, ssm_b_re=v_ssm_b_re,
               ssm_b_im=v_ssm_b_im, ssm_c_re=v_ssm_c_re, ssm_c_im=v_ssm_c_im, ssm_d=v_ssm_d, ssm_w_glu=v_ssm_w_glu,
               xa_w_q=v_xa_w_q, xa_w_kv=v_xa_w_kv, xa_w_o=v_xa_w_o)

    n_seq, seq, d = x.shape
    mlen = mem.shape[1]
    depth, n_norms = norm_g.shape[0], norm_g.shape[1]
    heads = 4
    tokens = n_seq * seq
    x2 = x.reshape(tokens, d)
    mem2 = mem.reshape(n_seq * mlen, d)
    tgt2 = loss_target.reshape(tokens, d)

    small_shapes = [w[k].shape for k in SMALL_SHARDED]
    small_rows = [_merge2d(w[k]) for k in SMALL_SHARDED]
    small_counts = [s.shape[0] for s in small_rows]
    small = jnp.concatenate(small_rows, axis=0)
    small = jnp.pad(small, [(0, (-small.shape[0]) % 8), (0, 0)])
    me = _index(*_place())

    def layer_weights(i):
        names = [(k, i) for k in ("ffn1_up", "ffn1_down", "ffn2_up", "ffn2_down", "xa_w_q", "xa_w_kv", "xa_w_o")]
        return names + ([("conv_w_in", i // 2), ("conv_w_out", i // 2)] if i % 2 == 0 else [("ssm_w_glu", i // 2)])

    shards = [[(w[k][idx].T if k in TRANSPOSED else w[k][idx]).astype(BF16) for k, idx in layer_weights(i)]
              for i in range(depth)]
    n_first = 2
    gathered = _all_gather("gather_layer0_ffn1", shards[0][:n_first] + [small])
    small_all = gathered[-1]
    in_flight = [None] * depth
    token = gathered[0]
    in_place = {k for k in COLUMN_SHARDED if w[k].shape[-1] % V7X_LANES == 0}

    def modes(i):
        start = n_first if i == 0 else 0
        return [COLUMNS if k in in_place else BLOCK for k, _ in layer_weights(i)][start:]

    for i in range(depth):
        mine = shards[i][n_first:] if i == 0 else shards[i]
        zones = [_landing(s, me, mode) for s, mode in zip(mine, modes(i))]
        *in_flight[i], token = _exchange_start(f"gather_start_l{i}", mine, zones, [WHOLE] * len(zones), modes(i),
                                               deps=[token], plan=CHIPS)
    passing = [None] * depth

    def pass_on(i, after):
        landed = _exchange_wait(f"gather_wait_l{i}", *in_flight[i], [WHOLE] * len(modes(i)), modes(i), after,
                                plan=CHIPS)
        *passing[i], forward_token = _exchange_start(f"forward_start_l{i}", [], landed, modes(i), modes(i),
                                                     plan=FORWARD)
        return forward_token

    def passed(i, after):
        return _exchange_wait(f"forward_wait_l{i}", *passing[i], modes(i), modes(i), after, plan=FORWARD)

    def small_whole(idx):
        start = sum(small_counts[:idx])
        part = small_all[:, start:start + small_counts[idx]]
        lead = small_shapes[idx][:-1]
        part = part.reshape((N_DEV,) + lead + (part.shape[-1],))
        part = jnp.moveaxis(part, 0, -2)
        return part.reshape(lead + (N_DEV * part.shape[-1],))

    norm_all = small_whole(0)
    conv_all = small_whole(1)
    dskip_all = small_whole(2)

    def whole(names, arrived):
        return {k: blk if k in in_place else _cols_whole(blk) if k in COLUMN_SHARDED else _rows_whole(blk)
                for (k, _), blk in zip(names, arrived)}

    saved = []
    cur = x2
    for i in range(depth):
        g = [norm_all[i, k].reshape(1, d) for k in range(n_norms)]
        j = i // 2
        if i == 0:
            lw = whole(layer_weights(0)[:n_first], gathered[:n_first])
        else:
            lw = whole(layer_weights(i), passed(i, cur))
        cur, s_ffn1 = _ffn_fwd(f"l{i}_ffn1", cur, g[0], lw["ffn1_up"], lw["ffn1_down"], [token] if i == 0 else ())
        if i == 0:
            lw.update(whole(layer_weights(0)[n_first:], passed(0, pass_on(0, cur))))
        if i % 2 == 0:
            lw["conv_w"] = conv_all[j]
            cur, s_mix = _conv_mixer_fwd(f"l{i}_conv", cur, g[1], lw["conv_w_in"], lw["conv_w"], lw["conv_w_out"],
                                         n_seq, seq)
        else:
            ssm = tuple(w[k][j] for k in ("ssm_a_re", "ssm_a_im", "ssm_log_dt", "ssm_b_re", "ssm_b_im",
                                          "ssm_c_re", "ssm_c_im"))
            cur, s_mix = _s5_mixer_fwd(f"l{i}_s5", cur, g[1], ssm, dskip_all[j].reshape(1, d), lw["ssm_w_glu"],
                                       n_seq, seq)
        cur, s_xa = _xattn_fwd(f"l{i}_xa", cur, mem2, g[2], g[3], lw["xa_w_q"], lw["xa_w_kv"], lw["xa_w_o"],
                               n_seq, seq, mlen, heads)
        early_pass = 0 < i < depth - 1
        deps = [pass_on(i + 1, cur)] if early_pass else ()
        cur, s_ffn2 = _ffn_fwd(f"l{i}_ffn2", cur, g[4], lw["ffn2_up"], lw["ffn2_down"], deps)
        if i == 0 and depth > 1:
            pass_on(1, cur)
        saved.append((g, lw, s_ffn1, s_mix, s_xa, s_ffn2))

    dres, err2, d_final = _final_loss("loss_head", cur, final_g.reshape(1, d), tgt2)
    loss = lax.psum(0.5 * jnp.sum(err2) / d, ("x", "y", "c"))

    d_norm = [[None] * n_norms for _ in range(depth)]
    d_conv = [None] * conv_w.shape[0]
    d_skip = [None] * ssm_d.shape[0]
    d_ssm = [None] * ssm_a_re.shape[0]
    leaving = [None] * depth
    deps = ()

    def leave(name, keys, gm, extra=()):
        srcs, src_modes, zones = [], [], []
        for k in keys:
            if k in in_place:
                rows, n = gm[k].shape[0], gm[k].shape[1] // N_DEV
                srcs.append(gm[k])
                src_modes.append(COLUMNS)
                zones.append(_landing(lax.dynamic_slice(gm[k], (0, me * n), (rows, n)), me))
            else:
                srcs.append(_cols_parts(gm[k]) if k in COLUMN_SHARDED else _rows_parts(gm[k]))
                src_modes.append(BLOCK)
                zones.append(_landing(lax.dynamic_index_in_dim(srcs[-1], me, 0, keepdims=False), me))
        for p in extra:
            srcs.append(p)
            src_modes.append(BLOCK)
            zones.append(_landing(lax.dynamic_index_in_dim(p, me, 0, keepdims=False), me))
        land_modes = [BLOCK] * len(srcs)
        *handles, token = _exchange_start(name, srcs, zones, src_modes, land_modes)
        return (*handles, src_modes, land_modes), token

    def small_parts(full):
        lead = full.shape[:-1]
        t = full.reshape(lead + (N_DEV, full.shape[-1] // N_DEV))
        t = jnp.moveaxis(t, -2, 0)
        return t.reshape(N_DEV, -1, t.shape[-1])

    for i in reversed(range(depth)):
        g, lw, s_ffn1, s_mix, s_xa, s_ffn2 = saved[i]
        j = i // 2
        gm = {}
        dres, d_norm[i][4], gm["ffn2_up"], gm["ffn2_down"] = _ffn_bwd(
            f"l{i}_ffn2", dres, s_ffn2, g[4], lw["ffn2_up"], lw["ffn2_down"], deps)
        dres, d_norm[i][2], d_norm[i][3], gm["xa_w_q"], gm["xa_w_kv"], gm["xa_w_o"] = _xattn_bwd(
            f"l{i}_xa", dres, s_xa, mem2, g[2], g[3], lw["xa_w_q"], lw["xa_w_kv"], lw["xa_w_o"], n_seq, seq, mlen,
            heads)
        if i % 2 == 0:
            dres, d_norm[i][1], gm["conv_w_in"], d_conv[j], gm["conv_w_out"] = _conv_mixer_bwd(
                f"l{i}_conv", dres, s_mix, g[1], lw["conv_w_in"], lw["conv_w"], lw["conv_w_out"], n_seq, seq)
        else:
            dres, d_norm[i][1], d_ssm[j], d_skip[j], gm["ssm_w_glu"] = _s5_mixer_bwd(
                f"l{i}_s5", dres, s_mix, g[1], lw["ssm_w_glu"], n_seq, seq)
        upper, token = leave(f"grads_start_l{i}_upper", [k for k, _ in layer_weights(i)[2:]], gm)
        lower = []

        def send_lower(d_up_t, d_down, i=i, lower=lower):
            handles, token = leave(f"grads_start_l{i}_lower", ["ffn1_up", "ffn1_down"],
                                   {"ffn1_up": d_up_t, "ffn1_down": d_down})
            lower.append(handles)
            return [token]

        dres, d_norm[i][0], _, _ = _ffn_bwd(
            f"l{i}_ffn1", dres, s_ffn1, g[0], lw["ffn1_up"], lw["ffn1_down"], [token], after_dw=send_lower)
        leaving[i] = (upper, lower[0])
        deps = ()
        if i == min(1, depth - 1):
            rep_grads = [jnp.stack([d_ssm[j][k] for j in range(len(d_ssm))]) for k in range(7)]
            rep_packed = _pack_rows(rep_grads + [d_final.reshape(-1)])
            *rep_leaving, token = _exchange_start("replicated_grads_start", [rep_packed], [_landing(rep_packed, me)],
                                                  [WHOLE], [BLOCK])
            deps = [token]
    grad_x = dres[0].reshape(n_seq, seq, d)
    d_norm_all = jnp.stack([jnp.concatenate(row, axis=0) for row in d_norm])
    small_g = jnp.concatenate(
        [small_parts(a) for a in (d_norm_all, jnp.stack(d_conv), jnp.concatenate(d_skip, axis=0))], axis=1)
    small_g = jnp.pad(small_g, [(0, 0), (0, (-small_g.shape[1]) % 8), (0, 0)])
    small_leaving, _ = leave("small_grads_start", [], {}, [small_g])
    received = {k: [None] * w[k].shape[0] for k in MATRICES}

    def arrive(i, part, after):
        names = layer_weights(i)[2:] if part == 0 else layer_weights(i)[:2]
        blks = _exchange_wait(f"grads_wait_l{i}_{'upper' if part == 0 else 'lower'}", *leaving[i][part], after)
        for (k, idx), blk in zip(names, blks):
            received[k][idx] = blk
        return [k for k, _ in names]

    for i in range(1, depth):
        arrive(i, 0, dres[0])
        arrive(i, 1, dres[0])
    rep_all = _exchange_wait("replicated_grads_wait", *rep_leaving, [WHOLE], [BLOCK], dres[0])[0]
    rep_shapes = [w[k].shape for k in REPLICATED]
    flat = {k: (_merge2d(w[k]), _merge2d(mom[k]), _merge2d(var[k])) for k in MATRICES}
    late = {k: received[k][0] is None for k in MATRICES}
    early = {}
    for k in MATRICES:
        first = 1 if late[k] else 0
        if first < len(received[k]):
            early[k] = _adamw(f"adamw_{k}_upper", received[k][first:], *flat[k], first_layer=first,
                              transposed=k in TRANSPOSED)
    after = list(early.values())[-1][0] if early else dres[0]
    done = dict(early)
    for part in (0, 1):
        for k in arrive(0, part, after):
            done[k] = _adamw(f"adamw_{k}_l0", received[k][:1], *flat[k], earlier=early.get(k),
                             transposed=k in TRANSPOSED)
            after = done[k][0]
    small_received, = _exchange_wait("small_grads_wait", *small_leaving, after)
    grads, deltas, new_m, new_v = {}, {}, {}, {}
    for k in MATRICES:
        grads[k], deltas[k], new_m[k], new_v[k] = [o.reshape(w[k].shape) for o in done[k]]

    def small_local(src):
        rows = jnp.concatenate([_merge2d(src[k]) for k in SMALL_SHARDED], axis=0)
        return jnp.pad(rows, [(0, (-rows.shape[0]) % 8), (0, 0)])

    out = _adamw("adamw_small", [small_received], small, small_local(mom), small_local(var))
    for res, o in zip((grads, deltas, new_m, new_v), out):
        start = 0
        for k, cnt, shape in zip(SMALL_SHARDED, small_counts, small_shapes):
            res[k] = o[start:start + cnt].reshape(shape)
            start += cnt

    out = _adamw("adamw_replicated", [rep_all], _pack_rows([w[k] for k in REPLICATED]),
                 _pack_rows([mom[k] for k in REPLICATED]), _pack_rows([var[k] for k in REPLICATED]))
    for res, o in zip((grads, deltas, new_m, new_v), out):
        for k, a in zip(REPLICATED, _unpack_rows(o, rep_shapes)):
            res[k] = a

    return (loss, grad_x, *[grads[k] for k in WEIGHT_NAMES], *[deltas[k] for k in WEIGHT_NAMES],
            *[new_m[k] for k in WEIGHT_NAMES], *[new_v[k] for k in WEIGHT_NAMES])
```

```python
import math

import jax
import jax.numpy as jnp
from jax import lax
from jax.experimental import pallas as pl
from jax.experimental.pallas import tpu as pltpu

F32 = jnp.float32
BF16 = jnp.bfloat16
MESH = pl.DeviceIdType.MESH
N_DEV = 8

NORM_EPS = 1e-6
EIG_CLIP = -1e-4
CONV_WIDTH = 3
ADAM_LR = 0.001
ADAM_B1 = 0.9
ADAM_B2 = 0.999
ADAM_EPS = 1e-08
ADAM_WD = 0.01
ADAM_STEP = 10
GELU_C = math.sqrt(2.0 / math.pi)
GELU_A = 0.044715

V7X_LANES = 128
V7X_VMEM_LIMIT = 56 * 1024 * 1024
S5_CHANNELS = 128
PACK_TILE = 8 * V7X_LANES

HBM_SPEC = pl.BlockSpec(memory_space=pltpu.HBM)
ANY_SPEC = pl.BlockSpec(memory_space=pl.ANY)
SEM_SPEC = pl.BlockSpec(memory_space=pltpu.SEMAPHORE)


def _params(n_grid):
    return pltpu.CompilerParams(dimension_semantics=("arbitrary",) * n_grid, vmem_limit_bytes=V7X_VMEM_LIMIT)


def _pick(n, pref, align):
    if n <= pref:
        return n
    t = (pref // align) * align
    while t >= align:
        if n % t == 0:
            return t
        t -= align
    raise ValueError(f"no tile for {n} (pref {pref}, align {align})")


MM_RHS_BLOCK_BYTES = 12 * 1024 * 1024
MM_LHS_BLOCK_BYTES = 6 * 1024 * 1024
MM_ACC_BYTES = 6 * 1024 * 1024
MM_ROWS = 512


MM_EPILOGUE_ACC_BYTES = 3 * 1024 * 1024
MM_EPILOGUE_MIN_ROWS = 256


def _mm_tiles(m, k, n, a_item, b_item, ta, max_rows):
    tn = _pick(n, max(V7X_LANES, MM_RHS_BLOCK_BYTES // (k * b_item)), V7X_LANES)
    rows = min(max_rows, MM_ACC_BYTES // (4 * tn), MM_LHS_BLOCK_BYTES // (k * a_item))
    align = V7X_LANES if ta else 16
    tm = _pick(m, max(align, rows), align)
    return tm, tn


def _store_results(out_refs, n_row, results, first):
    if not isinstance(results, (tuple, list)):
        results = (results,)
    for o, v in zip(out_refs[:n_row], results[:n_row]):
        if isinstance(v, (tuple, list)):
            off = 0
            for piece in v:
                w = piece.shape[1]
                o[:, off:off + w] = piece.astype(o.dtype)
                off += w
        else:
            o[...] = v.astype(o.dtype)
    if len(out_refs) > n_row:
        @pl.when(first)
        def _():
            for o in out_refs[n_row:]:
                o[...] = jnp.zeros_like(o)

        for o, v in zip(out_refs[n_row:], results[n_row:]):
            o[...] += v


def _mm(name, a, b, *, ta=False, tb=False, out_dtype=BF16, res=None, scale=None, deps=(),
        epilogue=None, row_ins=(), par_ins=(), outs=(), acc_outs=(), prologue=None, prologue_pars=()):
    if ta:
        k, m = a.shape
    else:
        m, k = a.shape
    if tb:
        n, k2 = b.shape
    else:
        k2, n = b.shape
    assert k == k2, (name, a.shape, b.shape)
    max_rows = MM_ROWS if epilogue is None else max(MM_EPILOGUE_MIN_ROWS, MM_EPILOGUE_ACC_BYTES // (4 * n))
    tm, tn = _mm_tiles(m, k, n, a.dtype.itemsize, b.dtype.itemsize, ta, max_rows)
    a_spec = pl.BlockSpec((k, tm), lambda j, i: (0, i)) if ta else pl.BlockSpec((tm, k), lambda j, i: (i, 0))
    b_mode = {"pipeline_mode": pl.Buffered(1)} if tn == n else {}
    b_spec = (pl.BlockSpec((tn, k), lambda j, i: (j, 0), **b_mode) if tb
              else pl.BlockSpec((k, tn), lambda j, i: (0, j), **b_mode))
    o_spec = pl.BlockSpec((tm, tn), lambda j, i: (i, j))
    dims = (((0 if ta else 1,), (1 if tb else 0,)), ((), ()))
    has_res = res is not None
    ins = [a, b] + ([res] if has_res else [])
    specs = [a_spec, b_spec] + ([o_spec] if has_res else [])
    n_mm = len(ins)
    if epilogue is None:
        out_specs, out_shape = [o_spec], [jax.ShapeDtypeStruct((m, n), out_dtype)]
    else:
        assert tn == n, (name, tn, n)
        for r in row_ins:
            arr, cb, cw = r if isinstance(r, tuple) else (r, 0, r.shape[1])
            assert arr.shape[0] == m, (name, arr.shape, m)
            ins.append(arr)
            specs.append(pl.BlockSpec((tm, cw), lambda j, i, cb=cb: (i, cb)))
        for p in par_ins:
            ins.append(p)
            specs.append(pl.BlockSpec(p.shape, lambda j, i: (0, 0)))
        out_specs = [pl.BlockSpec((tm, c), lambda j, i: (i, 0)) for c, _ in outs]
        out_specs += [pl.BlockSpec((r, c), lambda j, i: (0, 0)) for r, c in acc_outs]
        out_shape = [jax.ShapeDtypeStruct((m, c), dt) for c, dt in outs]
        out_shape += [jax.ShapeDtypeStruct((r, c), F32) for r, c in acc_outs]
    n_in = len(ins)
    if prologue is not None:
        assert tn == n and not ta, (name, tn, n, ta)
        for p in prologue_pars:
            ins.append(p)
            specs.append(pl.BlockSpec(p.shape, lambda j, i: (0, 0)))
        out_specs = out_specs + [a_spec]
        out_shape = out_shape + [jax.ShapeDtypeStruct((m, k), BF16)]
    n_pro = len(ins)
    ins += list(deps)
    specs += [ANY_SPEC] * len(deps)

    def body(*refs):
        a_ref, b_ref = refs[0], refs[1]
        out_refs = refs[n_pro + len(deps):]
        if prologue is None:
            lhs = a_ref[...].astype(BF16)
        else:
            lhs = prologue(a_ref[...], *[r[...] for r in refs[n_in:n_pro]]).astype(BF16)
            out_refs[-1][...] = lhs
            out_refs = out_refs[:-1]
        acc = lax.dot_general(lhs, b_ref[...].astype(BF16), dims, preferred_element_type=F32)
        if scale is not None:
            acc = acc * scale
        if has_res:
            acc = acc + refs[2][...].astype(F32)
        if epilogue is None:
            out_refs[0][...] = acc.astype(out_refs[0].dtype)
        else:
            extra = [r[...] for r in refs[n_mm:n_in]]
            _store_results(out_refs, len(outs), epilogue(acc, *extra), pl.program_id(1) == 0)

    out = pl.pallas_call(
        body, name=name, grid=(n // tn, m // tm), in_specs=specs, out_specs=out_specs, out_shape=out_shape,
        compiler_params=_params(2),
    )(*ins)
    return out[0] if epilogue is None and prologue is None else out


def _rowwise(name, fn, rows, row_ins, par_ins, row_outs, acc_outs=(), tm_pref=256):
    tm = _pick(rows, tm_pref, 16)
    in_specs, ins = [], []
    for r in row_ins:
        arr, cb, cw = r if isinstance(r, tuple) else (r, 0, r.shape[1])
        assert arr.shape[0] == rows, (name, arr.shape, rows)
        ins.append(arr)
        in_specs.append(pl.BlockSpec((tm, cw), lambda i, cb=cb: (i, cb)))
    for p in par_ins:
        ins.append(p)
        in_specs.append(pl.BlockSpec(p.shape, lambda i: (0, 0)))
    out_specs = [pl.BlockSpec((tm, c), lambda i: (i, 0)) for c, _ in row_outs]
    out_specs += [pl.BlockSpec((r, c), lambda i: (0, 0)) for r, c in acc_outs]
    out_shape = [jax.ShapeDtypeStruct((rows, c), dt) for c, dt in row_outs]
    out_shape += [jax.ShapeDtypeStruct((r, c), F32) for r, c in acc_outs]
    n_in, n_row = len(ins), len(row_outs)

    def body(*refs):
        vals = [r[...] for r in refs[:n_in]]
        _store_results(refs[n_in:], n_row, fn(*vals), pl.program_id(0) == 0)

    return pl.pallas_call(
        body, name=name, grid=(rows // tm,), in_specs=in_specs, out_specs=out_specs, out_shape=out_shape,
        compiler_params=_params(1),
    )(*ins)


def _inv_rms(x):
    return lax.rsqrt(jnp.mean(x * x, axis=-1, keepdims=True) + NORM_EPS)


def _rms_rows(x, g):
    return x * _inv_rms(x) * g


def _rms_fwd(name, x, g):
    return _rowwise(name, _rms_rows, x.shape[0], [x], [g], [(x.shape[1], BF16)], tm_pref=512)[0]


def _rms_bwd_rows(dn, x, dres, g):
    r = _inv_rms(x)
    xh = x * r
    dg = jnp.sum(dn * xh, axis=0, keepdims=True)
    dxh = dn * g
    dx = r * (dxh - xh * jnp.mean(dxh * xh, axis=-1, keepdims=True)) + dres
    return dx, dx, dg


def _rms_bwd(name, x, g, dn, dres=None):
    d = x.shape[1]
    if dres is None:
        def fn(x, dn, g):
            return (jnp.sum(dn.astype(F32) * (x * _inv_rms(x)), axis=0, keepdims=True),)

        return None, _rowwise(name, fn, x.shape[0], [x, dn], [g], [], [(1, d)])[0]

    def fn(x, dn, dres, g):
        return _rms_bwd_rows(dn.astype(F32), x, dres, g)

    out = _rowwise(name, fn, x.shape[0], [x, dn, dres], [g], [(d, F32), (d, BF16)], [(1, d)])
    return (out[0], out[1]), out[2]


def _mm_rms_bwd(name, dy, w, x, g, dres, tb=True, deps=()):
    d = x.shape[1]
    out = _mm(name, dy, w, tb=tb, deps=deps, epilogue=_rms_bwd_rows, row_ins=[x, dres], par_ins=[g],
              outs=[(d, F32), (d, BF16)], acc_outs=[(1, d)])
    return (out[0], out[1]), out[2]


def _sigmoid(x):
    return 0.5 + 0.5 * jnp.tanh(0.5 * x)


def _swiglu_rows(gu, f):
    gt = gu[:, :f]
    return gu, gt * _sigmoid(gt) * gu[:, f:]


def _swiglu_bwd_rows(dact, gt, up):
    gt, up = gt.astype(F32), up.astype(F32)
    sg = _sigmoid(gt)
    return ((dact * up * (sg * (1.0 + gt * (1.0 - sg))), dact * (gt * sg)),)


def _glu_bwd(name, dres, vg, d):
    def fn(dres, val, gate):
        val, gate = val.astype(F32), gate.astype(F32)
        sg = _sigmoid(gate)
        return ((dres * sg, dres * val * sg * (1.0 - sg)),)

    return _rowwise(name, fn, dres.shape[0], [dres, (vg, 0, d), (vg, 1, d)], [], [(2 * d, BF16)])[0]


def _final_loss(name, x, g, tgt):
    d = x.shape[1]

    def fn(x, tgt, g):
        r = _inv_rms(x)
        xh = x * r
        err = xh * g - tgt
        dy = err * (1.0 / d)
        dxh = dy * g
        dx = r * (dxh - xh * jnp.mean(dxh * xh, axis=-1, keepdims=True))
        return dx, dx, jnp.sum(err * err, axis=0, keepdims=True), jnp.sum(dy * xh, axis=0, keepdims=True)

    dx, dx16, err2, dg = _rowwise(name, fn, x.shape[0], [x, tgt], [g], [(d, F32), (d, BF16)], [(1, d), (1, d)])
    return (dx, dx16), err2, dg


def _shift_down(u, k):
    rows = lax.broadcasted_iota(jnp.int32, u.shape, 0)
    return jnp.where(rows >= k, pltpu.roll(u, k, 0), 0.0)


def _shift_up(u, k):
    n = u.shape[0]
    rows = lax.broadcasted_iota(jnp.int32, u.shape, 0)
    return jnp.where(rows < n - k, pltpu.roll(u, n - k, 0), 0.0)


def _conv_specs(seq, cw, n_cb, swap):
    def at(off):
        if swap:
            return pl.BlockSpec((seq, cw), lambda j, b: (b, off * n_cb + j))
        return pl.BlockSpec((seq, cw), lambda b, j: (b, off * n_cb + j))

    return at


def _conv_fwd(name, cbv, w, n_seq, seq):
    d = w.shape[1]
    cw = _pick(d, 256, V7X_LANES)
    n_cb = d // cw
    at = _conv_specs(seq, cw, n_cb, swap=False)

    def body(c_ref, b_ref, v_ref, w_ref, z_ref):
        u = c_ref[...].astype(F32) * v_ref[...].astype(F32)
        cv = w_ref[0:1, :] * _shift_down(u, 2) + w_ref[1:2, :] * _shift_down(u, 1) + w_ref[2:3, :] * u
        z_ref[...] = (b_ref[...].astype(F32) * cv).astype(z_ref.dtype)

    return pl.pallas_call(
        body, name=name, grid=(n_seq, n_cb),
        in_specs=[at(0), at(1), at(2), pl.BlockSpec((CONV_WIDTH, cw), lambda b, j: (0, j))],
        out_specs=at(0), out_shape=jax.ShapeDtypeStruct((n_seq * seq, d), BF16), compiler_params=_params(2),
    )(cbv, cbv, cbv, w)


def _conv_bwd(name, dz, cbv, w, n_seq, seq):
    d = w.shape[1]
    cw = _pick(d, 256, V7X_LANES)
    n_cb = d // cw
    at = _conv_specs(seq, cw, n_cb, swap=True)

    def body(dz_ref, c_ref, b_ref, v_ref, w_ref, dc_ref, db_ref, dv_ref, dw_ref):
        c, b, v = c_ref[...].astype(F32), b_ref[...].astype(F32), v_ref[...].astype(F32)
        dz = dz_ref[...].astype(F32)
        w0, w1, w2 = w_ref[0:1, :], w_ref[1:2, :], w_ref[2:3, :]
        u = c * v
        u1, u2 = _shift_down(u, 1), _shift_down(u, 2)
        cv = w0 * u2 + w1 * u1 + w2 * u
        db_ref[...] = (dz * cv).astype(db_ref.dtype)
        dcv = dz * b
        du = w2 * dcv + w1 * _shift_up(dcv, 1) + w0 * _shift_up(dcv, 2)
        dc_ref[...] = (du * v).astype(dc_ref.dtype)
        dv_ref[...] = (du * c).astype(dv_ref.dtype)

        @pl.when(pl.program_id(1) == 0)
        def _():
            dw_ref[...] = jnp.zeros_like(dw_ref)

        dw_ref[0:1, :] += jnp.sum(dcv * u2, axis=0, keepdims=True)
        dw_ref[1:2, :] += jnp.sum(dcv * u1, axis=0, keepdims=True)
        dw_ref[2:3, :] += jnp.sum(dcv * u, axis=0, keepdims=True)

    act = jax.ShapeDtypeStruct((n_seq * seq, d), BF16)
    return pl.pallas_call(
        body, name=name, grid=(n_cb, n_seq),
        in_specs=[at(0), at(0), at(1), at(2), pl.BlockSpec((CONV_WIDTH, cw), lambda j, b: (0, j))],
        out_specs=[at(0), at(0), at(0), pl.BlockSpec((CONV_WIDTH, cw), lambda j, b: (0, j))],
        out_shape=[act, act, act, jax.ShapeDtypeStruct((CONV_WIDTH, d), F32)], compiler_params=_params(2),
    )(dz, cbv, cbv, cbv, w)


def _s5_discretize(a_re, a_im, log_dt, b_re, b_im):
    lam_re = jnp.minimum(a_re, EIG_CLIP)
    lam_im = a_im
    dt = jnp.exp(log_dt)[:, None]
    mag = jnp.exp(lam_re * dt)
    abar_re = mag * jnp.cos(lam_im * dt)
    abar_im = mag * jnp.sin(lam_im * dt)
    den = lam_re * lam_re + lam_im * lam_im
    num_re = abar_re - 1.0
    num_im = abar_im
    coef_re = (num_re * lam_re + num_im * lam_im) / den
    coef_im = (num_im * lam_re - num_re * lam_im) / den
    bbar_re = coef_re[..., None] * b_re - coef_im[..., None] * b_im
    bbar_im = coef_re[..., None] * b_im + coef_im[..., None] * b_re
    return abar_re, abar_im, bbar_re, bbar_im


def _block_diag_in(bbar, gb):
    g, p, h = bbar.shape
    t = jnp.transpose(bbar.reshape(g // gb, gb, p, h), (0, 1, 3, 2))
    return jnp.einsum("cghp,gk->cghkp", t, jnp.eye(gb, dtype=bbar.dtype)).reshape(g // gb, gb * h, gb * p)


def _block_diag_in_t(blk, gb, p, h):
    nb = blk.shape[0]
    t = jnp.einsum("cghkp,gk->cghp", blk.reshape(nb, gb, h, gb, p), jnp.eye(gb, dtype=blk.dtype))
    return jnp.transpose(t, (0, 1, 3, 2)).reshape(nb * gb, p, h)


def _block_diag_out(c, gb):
    g, h, p = c.shape
    t = jnp.transpose(c.reshape(g // gb, gb, h, p), (0, 1, 3, 2))
    return jnp.einsum("cgph,gk->cgpkh", t, jnp.eye(gb, dtype=c.dtype)).reshape(g // gb, gb * p, gb * h)


def _block_diag_out_t(blk, gb, p, h):
    nb = blk.shape[0]
    t = jnp.einsum("cgpkh,gk->cgph", blk.reshape(nb, gb, p, gb, h), jnp.eye(gb, dtype=blk.dtype))
    return jnp.transpose(t, (0, 1, 3, 2)).reshape(nb * gb, h, p)


def _gelu(y):
    return 0.5 * y * (1.0 + jnp.tanh(GELU_C * (y + GELU_A * y * y * y)))


def _gelu_grad(y):
    th = jnp.tanh(GELU_C * (y + GELU_A * y * y * y))
    return 0.5 * (1.0 + th) + 0.5 * y * (1.0 - th * th) * GELU_C * (1.0 + 3.0 * GELU_A * y * y)


def _dot(a, b, ca, cb):
    return lax.dot_general(a.astype(BF16), b.astype(BF16), (((ca,), (cb,)), ((), ())), preferred_element_type=F32)


def _s5_specs(seq, ch, sb):
    act = pl.BlockSpec((seq, ch), lambda j, b: (b, j))
    state = pl.BlockSpec((seq, sb), lambda j, b: (b, j))
    w_in = pl.BlockSpec((None, ch, sb), lambda j, b: (j, 0, 0))
    w_out = pl.BlockSpec((None, sb, ch), lambda j, b: (j, 0, 0))
    lane_s = pl.BlockSpec((1, sb), lambda j, b: (0, j))
    lane_c = pl.BlockSpec((1, ch), lambda j, b: (0, j))
    return act, state, w_in, w_out, lane_s, lane_c


def _s5_fwd(name, h, bin_re, bin_im, cout_re, cout_im, abar_re, abar_im, dskip, n_seq, seq):
    t, d = h.shape
    nb, ch, sb = bin_re.shape
    act, state, w_in, w_out, lane_s, lane_c = _s5_specs(seq, ch, sb)

    def body(h_ref, bre_ref, bim_ref, cre_ref, cim_ref, ar_ref, ai_ref, d_ref, sre_ref, sim_ref, y_ref, z_ref):
        u = h_ref[...]
        sre_ref[...] = _dot(u, bre_ref[...], 1, 0)
        sim_ref[...] = _dot(u, bim_ref[...], 1, 0)
        ar, ai = ar_ref[...], ai_ref[...]

        def step(i, carry):
            sr, si = carry
            row = pl.ds(i, 1)
            nr = ar * sr - ai * si + sre_ref[row, :]
            ni = ar * si + ai * sr + sim_ref[row, :]
            sre_ref[row, :] = nr
            sim_ref[row, :] = ni
            return nr, ni

        zero = jnp.zeros((1, sb), F32)
        lax.fori_loop(0, seq, step, (zero, zero), unroll=8)
        y = _dot(sre_ref[...], cre_ref[...], 1, 0) - _dot(sim_ref[...], cim_ref[...], 1, 0)
        y = y + d_ref[...] * u.astype(F32)
        y_ref[...] = y
        z_ref[...] = _gelu(y).astype(z_ref.dtype)

    return pl.pallas_call(
        body, name=name, grid=(nb, n_seq),
        in_specs=[act, w_in, w_in, w_out, w_out, lane_s, lane_s, lane_c],
        out_specs=[state, state, act, act],
        out_shape=[jax.ShapeDtypeStruct((t, nb * sb), F32), jax.ShapeDtypeStruct((t, nb * sb), F32),
                   jax.ShapeDtypeStruct((t, d), F32), jax.ShapeDtypeStruct((t, d), BF16)],
        compiler_params=_params(2),
    )(h, bin_re, bin_im, cout_re, cout_im, abar_re, abar_im, dskip)


def _s5_bwd(name, dz, ypre, h, s_re, s_im, bin_re, bin_im, cout_re, cout_im, abar_re, abar_im, dskip, n_seq, seq):
    t, d = h.shape
    nb, ch, sb = bin_re.shape
    act, state, w_in, w_out, lane_s, lane_c = _s5_specs(seq, ch, sb)

    def body(dz_ref, y_ref, h_ref, sre_ref, sim_ref, bre_ref, bim_ref, cre_ref, cim_ref, ar_ref, ai_ref, d_ref,
             dh_ref, dbre_ref, dbim_ref, dcre_ref, dcim_ref, dar_ref, dai_ref, dd_ref, gre, gim):
        first = pl.program_id(1) == 0
        u = h_ref[...].astype(F32)
        dy = dz_ref[...].astype(F32) * _gelu_grad(y_ref[...])
        gre[...] = _dot(dy, cre_ref[...], 1, 1)
        gim[...] = -_dot(dy, cim_ref[...], 1, 1)
        ar, ai = ar_ref[...], ai_ref[...]

        def step(i, carry):
            gr, gi = carry
            row = pl.ds(seq - 1 - i, 1)
            nr = gre[row, :] + ar * gr + ai * gi
            ni = gim[row, :] - ai * gr + ar * gi
            gre[row, :] = nr
            gim[row, :] = ni
            return nr, ni

        zero = jnp.zeros((1, sb), F32)
        lax.fori_loop(0, seq, step, (zero, zero), unroll=8)

        g_re, g_im = gre[...], gim[...]
        s_re, s_im = sre_ref[...], sim_ref[...]
        p_re, p_im = _shift_down(s_re, 1), _shift_down(s_im, 1)
        dar = jnp.sum(g_re * p_re + g_im * p_im, axis=0, keepdims=True)
        dai = jnp.sum(g_im * p_re - g_re * p_im, axis=0, keepdims=True)
        dbre = _dot(u, g_re, 0, 0)
        dbim = _dot(u, g_im, 0, 0)
        dcre = _dot(s_re, dy, 0, 0)
        dcim = -_dot(s_im, dy, 0, 0)
        ddd = jnp.sum(dy * u, axis=0, keepdims=True)
        dh_ref[...] = _dot(g_re, bre_ref[...], 1, 1) + _dot(g_im, bim_ref[...], 1, 1) + d_ref[...] * dy

        @pl.when(first)
        def _():
            dar_ref[...] = dar
            dai_ref[...] = dai
            dbre_ref[...] = dbre
            dbim_ref[...] = dbim
            dcre_ref[...] = dcre
            dcim_ref[...] = dcim
            dd_ref[...] = ddd

        @pl.when(jnp.logical_not(first))
        def _():
            dar_ref[...] += dar
            dai_ref[...] += dai
            dbre_ref[...] += dbre
            dbim_ref[...] += dbim
            dcre_ref[...] += dcre
            dcim_ref[...] += dcim
            dd_ref[...] += ddd

    return pl.pallas_call(
        body, name=name, grid=(nb, n_seq),
        in_specs=[act, act, act, state, state, w_in, w_in, w_out, w_out, lane_s, lane_s, lane_c],
        out_specs=[act, w_in, w_in, w_out, w_out, lane_s, lane_s, lane_c],
        out_shape=[jax.ShapeDtypeStruct((t, d), F32),
                   jax.ShapeDtypeStruct((nb, ch, sb), F32), jax.ShapeDtypeStruct((nb, ch, sb), F32),
                   jax.ShapeDtypeStruct((nb, sb, ch), F32), jax.ShapeDtypeStruct((nb, sb, ch), F32),
                   jax.ShapeDtypeStruct((1, nb * sb), F32), jax.ShapeDtypeStruct((1, nb * sb), F32),
                   jax.ShapeDtypeStruct((1, d), F32)],
        scratch_shapes=[pltpu.VMEM((seq, sb), F32), pltpu.VMEM((seq, sb), F32)],
        compiler_params=_params(2),
    )(dz, ypre, h, s_re, s_im, bin_re, bin_im, cout_re, cout_im, abar_re, abar_im, dskip)


ATTN_QUERY_ROWS = 1024


def _softmax_rows(q, k, scale):
    s = _dot(q, k, 1, 1) * scale
    e = jnp.exp(s - jnp.max(s, axis=-1, keepdims=True))
    return e * (1.0 / jnp.sum(e, axis=-1, keepdims=True))


def _attn_fwd(name, q, kv, n_seq, seq, mlen, heads):
    t, d = q.shape
    hd = d // heads
    tq = _pick(seq, ATTN_QUERY_ROWS, 16)
    nq = seq // tq
    scale = hd ** -0.5
    q_spec = pl.BlockSpec((tq, d), lambda b, i: (b * nq + i, 0))

    def body(q_ref, k_ref, v_ref, o_ref):
        for h in range(heads):
            cols = slice(h * hd, (h + 1) * hd)
            p = _softmax_rows(q_ref[:, cols], k_ref[:, cols], scale)
            o_ref[:, cols] = _dot(p, v_ref[:, cols], 1, 0).astype(o_ref.dtype)

    return pl.pallas_call(
        body, name=name, grid=(n_seq, nq),
        in_specs=[q_spec, pl.BlockSpec((mlen, d), lambda b, i: (b, 0)), pl.BlockSpec((mlen, d), lambda b, i: (b, 1))],
        out_specs=q_spec, out_shape=jax.ShapeDtypeStruct((t, d), BF16), compiler_params=_params(2),
    )(q, kv, kv)


def _attn_bwd(name, q, kv, do, n_seq, seq, mlen, heads):
    t, d = q.shape
    hd = d // heads
    tq = _pick(seq, ATTN_QUERY_ROWS, 16)
    nq = seq // tq
    scale = hd ** -0.5
    q_spec = pl.BlockSpec((tq, d), lambda b, i: (b * nq + i, 0))
    k_spec = pl.BlockSpec((mlen, d), lambda b, i: (b, 0))

    def body(q_ref, k_ref, v_ref, do_ref, dq_ref, dk_ref, dv_ref):
        @pl.when(pl.program_id(1) == 0)
        def _():
            dk_ref[...] = jnp.zeros_like(dk_ref)
            dv_ref[...] = jnp.zeros_like(dv_ref)

        for h in range(heads):
            cols = slice(h * hd, (h + 1) * hd)
            q, k, v, do = q_ref[:, cols], k_ref[:, cols], v_ref[:, cols], do_ref[:, cols]
            p = _softmax_rows(q, k, scale)
            dp = _dot(do, v, 1, 1)
            ds = p * (dp - jnp.sum(dp * p, axis=-1, keepdims=True)) * scale
            dq_ref[:, cols] = _dot(ds, k, 1, 0).astype(dq_ref.dtype)
            dk_ref[:, cols] += _dot(ds, q, 0, 0)
            dv_ref[:, cols] += _dot(p, do, 0, 0)

    return pl.pallas_call(
        body, name=name, grid=(n_seq, nq),
        in_specs=[q_spec, k_spec, pl.BlockSpec((mlen, d), lambda b, i: (b, 1)), q_spec],
        out_specs=[q_spec, k_spec, k_spec],
        out_shape=[jax.ShapeDtypeStruct((t, d), BF16), jax.ShapeDtypeStruct((n_seq * mlen, d), F32),
                   jax.ShapeDtypeStruct((n_seq * mlen, d), F32)],
        compiler_params=_params(2),
    )(q, kv, kv, do)


ADAMW_BLOCK_ELEMS = 128 * 1024


def _adamw(name, parts, w, m, v, first_layer=0, earlier=None, transposed=False):
    n_layers = len(parts)
    r, c = parts[0].shape[1:][::-1] if transposed else parts[0].shape[1:]
    assert w.shape[0] % r == 0 and w.shape[1] == c and first_layer + n_layers <= w.shape[0] // r, (name, w.shape)
    tr = _pick(r, max(V7X_LANES, ADAMW_BLOCK_ELEMS // c // V7X_LANES * V7X_LANES), V7X_LANES if transposed else 8)
    nt = r // tr
    spec = pl.BlockSpec((tr, c), lambda l, i: ((first_layer + l) * nt + i, 0))
    c1 = 1.0 - ADAM_B1 ** ADAM_STEP
    c2 = 1.0 - ADAM_B2 ** ADAM_STEP

    def parts_spec(q):
        def at(l, i):
            return jnp.where(l == q, i, jnp.where(l > q, nt - 1, 0))

        if transposed:
            return pl.BlockSpec((N_DEV, c, tr), lambda l, i: (0, 0, at(l, i)))
        return pl.BlockSpec((N_DEV, tr, c), lambda l, i: (0, at(l, i), 0))

    earlier = list(earlier or ())

    def body(*refs):
        p_refs = refs[:n_layers]
        w_ref, m_ref, v_ref = refs[n_layers:n_layers + 3]
        g_ref, d_ref, nm_ref, nv_ref = refs[n_layers + 3 + len(earlier):]

        def update(p_ref):
            g = p_ref[0].astype(F32)
            for k in range(1, N_DEV):
                g = g + p_ref[k].astype(F32)
            if transposed:
                g = g.T
            nm = ADAM_B1 * m_ref[...] + (1.0 - ADAM_B1) * g
            nv = ADAM_B2 * v_ref[...] + (1.0 - ADAM_B2) * (g * g)
            g_ref[...] = g
            nm_ref[...] = nm
            nv_ref[...] = nv
            d_ref[...] = -ADAM_LR * ((nm / c1) / (jnp.sqrt(nv / c2) + ADAM_EPS) + ADAM_WD * w_ref[...])

        for q in range(n_layers):
            pl.when(pl.program_id(0) == q)(lambda q=q: update(p_refs[q]))

    out = jax.ShapeDtypeStruct(w.shape, F32)
    return pl.pallas_call(
        body, name=name, grid=(n_layers, nt),
        in_specs=[parts_spec(q) for q in range(n_layers)] + [spec] * 3 + [ANY_SPEC] * len(earlier),
        out_specs=[spec] * 4, out_shape=[out] * 4, compiler_params=_params(2),
        input_output_aliases={n_layers + 3 + q: q for q in range(len(earlier))},
    )(*parts, w, m, v, *earlier)


def _place():
    x, y, c = lax.axis_index("x"), lax.axis_index("y"), lax.axis_index("c")
    return x, y, c


def _index(px, py, pc):
    return 4 * px + 2 * py + pc


def _all_gather(name, shards):
    n = len(shards)

    def body(*refs):
        in_refs, out_refs = refs[:n], refs[n:2 * n]
        send_sems, recv_sems, local_sems = refs[2 * n:]
        x, y, c = _place()
        me, sibling = (x, y, c), (x, y, 1 - c)
        chips = [(1 - x, y), (x, 1 - y), (1 - x, 1 - y)]

        def slot(k, block):
            return out_refs[k].at[_index(*block)]

        def copy(k, j, block, to, src=None):
            return pltpu.make_async_remote_copy(
                src_ref=slot(k, block) if src is None else src, dst_ref=slot(k, block),
                send_sem=send_sems.at[7 * k + j], recv_sem=recv_sems.at[7 * k + j], device_id=to, device_id_type=MESH)

        mine = [pltpu.make_async_copy(in_refs[k], slot(k, me), local_sems.at[k]) for k in range(n)]
        for cp in mine:
            cp.start()
        first = []
        for k in range(n):
            first.append(copy(k, 0, me, sibling, src=in_refs[k]))
            first += [copy(k, 1 + j, me, (*chip, c), src=in_refs[k]) for j, chip in enumerate(chips)]
        for cp in first:
            cp.start()
        passed = []
        for j, chip in enumerate(chips):
            for k in range(n):
                copy(k, 1 + j, (*chip, c), me).wait_recv()
                cp = copy(k, 4 + j, (*chip, c), sibling)
                cp.start()
                passed.append(cp)
        for k in range(n):
            copy(k, 0, sibling, me).wait_recv()
            for j, chip in enumerate(chips):
                copy(k, 4 + j, (*chip, 1 - c), me).wait_recv()
        for cp in first + passed:
            cp.wait_send()
        for cp in mine:
            cp.wait()

    return pl.pallas_call(
        body, name=name, in_specs=[HBM_SPEC] * n, out_specs=[HBM_SPEC] * n,
        out_shape=[jax.ShapeDtypeStruct((N_DEV,) + s.shape, s.dtype) for s in shards],
        scratch_shapes=[pltpu.SemaphoreType.DMA((7 * n,)), pltpu.SemaphoreType.DMA((7 * n,)),
                        pltpu.SemaphoreType.DMA((n,))],
    )(*shards)


WHOLE, BLOCK, COLUMNS = "whole", "block", "columns"


def _slot(ref, index, mode):
    if mode == WHOLE:
        return ref
    if mode == BLOCK:
        return ref.at[index]
    width = ref.shape[1] // N_DEV
    return ref.at[:, pl.ds(pl.multiple_of(index * width, width), width)]


DIRECT = tuple(range(1, N_DEV))
CHIPS = (1, 4, 2, 6)
FORWARD = "forward"


def _copies(plan, x, y, c):
    def xor(r, flip_core=False):
        rx, ry, rc = (r >> 2) & 1, (r >> 1) & 1, (r & 1) ^ int(flip_core)
        return (1 - x if rx else x, 1 - y if ry else y, 1 - c if rc else c)

    me = _index(x, y, c)
    if plan == FORWARD:
        return [(xor(1), _index(*xor(r)), _index(*xor(r)), _index(*xor(r, True))) for r in (4, 2, 6)]
    return [(xor(r), _index(*xor(r)), me, _index(*xor(r))) for r in plan]


def _exchange_start(name, srcs, lands, src_modes, land_modes, deps=(), plan=DIRECT):
    n, n_src = len(lands), len(srcs)
    n_copies = 3 if plan == FORWARD else len(plan)

    def body(*refs):
        land_refs = refs[n_src:n_src + n]
        src_refs = refs[:n_src] if n_src else land_refs
        send_sems, recv_sems = refs[n_src + n + len(deps)], refs[n_src + n + len(deps) + 1]
        token = refs[-1]
        x, y, c = _place()
        for k in range(n):
            for j, (peer, src_index, dst_index, _) in enumerate(_copies(plan, x, y, c)):
                pltpu.make_async_remote_copy(
                    src_ref=_slot(src_refs[k], src_index, src_modes[k]),
                    dst_ref=_slot(land_refs[k], dst_index, land_modes[k]), send_sem=send_sems.at[n_copies * k + j],
                    recv_sem=recv_sems.at[n_copies * k + j], device_id=peer, device_id_type=MESH).start()
        token[...] = jnp.zeros_like(token)

    arrays = list(srcs) + list(lands)
    thru = [pltpu.HBM(a.shape, a.dtype) for a in arrays]
    out = pl.pallas_call(
        body, name=name,
        out_shape=(pltpu.SemaphoreType.DMA((n_copies * n,)), pltpu.SemaphoreType.DMA((n_copies * n,)), *thru,
                   jax.ShapeDtypeStruct((8, V7X_LANES), F32)),
        in_specs=[HBM_SPEC] * len(arrays) + [ANY_SPEC] * len(deps),
        out_specs=(SEM_SPEC, SEM_SPEC, *([HBM_SPEC] * len(arrays)), pl.BlockSpec(memory_space=pltpu.VMEM)),
        input_output_aliases={k: 2 + k for k in range(len(arrays))},
        compiler_params=pltpu.CompilerParams(has_side_effects=pltpu.SideEffectType.DATAFLOW_SIDE_EFFECTING),
    )(*[pltpu.with_memory_space_constraint(a, pltpu.HBM) for a in arrays], *deps)
    return out[0], out[1], list(out[2:2 + n_src]), list(out[2 + n_src:2 + n_src + n]), out[-1]


def _exchange_wait(name, send_sems, recv_sems, srcs, lands, src_modes, land_modes, after, plan=DIRECT):
    n, n_src = len(lands), len(srcs)
    n_copies = 3 if plan == FORWARD else len(plan)

    def body(*refs):
        land_refs = refs[n_src:n_src + n]
        src_refs = refs[:n_src] if n_src else land_refs
        send_sems, recv_sems = refs[n_src + n], refs[n_src + n + 1]
        x, y, c = _place()
        for k in range(n):
            for j, (peer, src_index, _, arrival_index) in enumerate(_copies(plan, x, y, c)):
                cp = pltpu.make_async_remote_copy(
                    src_ref=_slot(src_refs[k], src_index, src_modes[k]),
                    dst_ref=_slot(land_refs[k], arrival_index, land_modes[k]), send_sem=send_sems.at[n_copies * k + j],
                    recv_sem=recv_sems.at[n_copies * k + j], device_id=peer, device_id_type=MESH)
                cp.wait_send()
                cp.wait_recv()

    arrays = list(srcs) + list(lands)
    thru = [pltpu.HBM(a.shape, a.dtype) for a in arrays]
    out = pl.pallas_call(
        body, name=name, out_shape=tuple(thru),
        in_specs=[HBM_SPEC] * len(arrays) + [SEM_SPEC, SEM_SPEC, ANY_SPEC], out_specs=tuple([HBM_SPEC] * len(arrays)),
        input_output_aliases={k: k for k in range(len(arrays))},
        compiler_params=pltpu.CompilerParams(has_side_effects=pltpu.SideEffectType.DATAFLOW_SIDE_EFFECTING),
    )(*arrays, send_sems, recv_sems, after)
    return list(out[n_src:])


def _landing(shard, me, mode=BLOCK):
    if mode == COLUMNS:
        k, n = shard.shape
        return lax.dynamic_update_slice(lax.empty((k, N_DEV * n), shard.dtype), shard, (0, me * n))
    zone = lax.empty((N_DEV,) + shard.shape, shard.dtype)
    return lax.dynamic_update_slice(zone, shard[None], (me,) + (0,) * shard.ndim)


def _cols_whole(w):
    return jnp.transpose(w, (1, 0, 2)).reshape(w.shape[1], N_DEV * w.shape[2])


def _rows_whole(w):
    return w.reshape(N_DEV * w.shape[1], w.shape[2])


def _cols_parts(dw):
    k, n8 = dw.shape
    return jnp.transpose(dw.reshape(k, N_DEV, n8 // N_DEV), (1, 0, 2))


def _rows_parts(dw):
    r8, c = dw.shape
    return dw.reshape(N_DEV, r8 // N_DEV, c)


def _pack_rows(arrays):
    rows = []
    for a in arrays:
        flat = a.reshape(-1).astype(F32)
        flat = jnp.pad(flat, [(0, (-flat.shape[0]) % PACK_TILE)])
        rows.append(flat.reshape(-1, V7X_LANES))
    return jnp.concatenate(rows, axis=0)


def _unpack_rows(packed, shapes):
    out, row = [], 0
    for s in shapes:
        size = math.prod(s)
        n_rows = -(-size // PACK_TILE) * 8
        out.append(packed[row:row + n_rows].reshape(-1)[:size].reshape(s))
        row += n_rows
    return out


def _merge2d(a):
    return a.reshape(-1, a.shape[-1])


def _ffn_fwd(tag, x, g, w_up_t, w_down, deps=()):
    f = w_down.shape[0]
    gu, act, n = _mm(f"{tag}_up", x, w_up_t, tb=True, prologue=_rms_rows, prologue_pars=[g], deps=deps,
                     epilogue=lambda acc: _swiglu_rows(acc, f), outs=[(2 * f, BF16), (f, BF16)])
    out = _mm(f"{tag}_down", act, w_down, res=x, scale=0.5, out_dtype=F32)
    return out, (x, n, gu, act)


def _ffn_bwd(tag, dres, saved, g, w_up_t, w_down, deps=(), after_dw=None):
    x, n, gu, act = saved
    dres32, dres16 = dres
    f = w_down.shape[0]
    dgu = _mm(f"{tag}_down_dx", dres16, w_down, tb=True, scale=0.5, deps=deps, epilogue=_swiglu_bwd_rows,
              row_ins=[(gu, 0, f), (gu, 1, f)], outs=[(2 * f, BF16)])[0]
    d_down = _mm(f"{tag}_down_dw", act, dres16, ta=True, scale=0.5)
    d_up_t = _mm(f"{tag}_up_dw", dgu, n, ta=True)
    dx, dg = _mm_rms_bwd(f"{tag}_up_dx", dgu, w_up_t, x, g, dres32, tb=False,
                         deps=after_dw(d_up_t, d_down) if after_dw else ())
    return dx, dg, d_up_t, d_down


def _conv_mixer_fwd(tag, x, g, w_in, w_conv, w_out, n_seq, seq):
    cbv, h = _mm(f"{tag}_in", x, w_in, prologue=_rms_rows, prologue_pars=[g])
    z = _conv_fwd(f"{tag}_conv", cbv, w_conv, n_seq, seq)
    out = _mm(f"{tag}_out", z, w_out, res=x, out_dtype=F32)
    return out, (x, h, cbv, z)


def _conv_mixer_bwd(tag, dres, saved, g, w_in, w_conv, w_out, n_seq, seq):
    x, h, cbv, z = saved
    dres32, dres16 = dres
    dz = _mm(f"{tag}_out_dx", dres16, w_out, tb=True)
    d_out = _mm(f"{tag}_out_dw", z, dres16, ta=True)
    dc, db, dv, d_conv = _conv_bwd(f"{tag}_conv_bwd", dz, cbv, w_conv, n_seq, seq)
    dcbv = jnp.concatenate([dc, db, dv], axis=1)
    d_in = _mm(f"{tag}_in_dw", h, dcbv, ta=True)
    dx, dg = _mm_rms_bwd(f"{tag}_in_dx", dcbv, w_in, x, g, dres32)
    return dx, dg, d_in, d_conv, d_out


def _s5_mixer_fwd(tag, x, g, ssm, dskip, w_glu, n_seq, seq):
    a_re, a_im, log_dt, b_re, b_im, c_re, c_im = ssm
    groups, p, hh = b_re.shape
    gb = S5_CHANNELS // hh
    disc, disc_vjp = jax.vjp(_s5_discretize, a_re, a_im, log_dt, b_re, b_im)
    abar_re, abar_im, bbar_re, bbar_im = disc
    mats = (_block_diag_in(bbar_re, gb).astype(BF16), _block_diag_in(bbar_im, gb).astype(BF16),
            _block_diag_out(c_re, gb).astype(BF16), _block_diag_out(c_im, gb).astype(BF16),
            abar_re.reshape(1, groups * p), abar_im.reshape(1, groups * p), dskip)
    d = x.shape[1]
    h = _rms_fwd(f"{tag}_norm", x, g)
    s_re, s_im, ypre, z = _s5_fwd(f"{tag}_scan", h, *mats, n_seq, seq)
    vg, out = _mm(f"{tag}_glu", z, w_glu, epilogue=lambda vg, x: (vg, x + vg[:, :d] * _sigmoid(vg[:, d:])),
                  row_ins=[x], outs=[(2 * d, BF16), (d, F32)])
    return out, (x, h, s_re, s_im, ypre, z, vg, mats, disc_vjp, (groups, p, hh, gb))


def _s5_mixer_bwd(tag, dres, saved, g, w_glu, n_seq, seq):
    x, h, s_re, s_im, ypre, z, vg, mats, disc_vjp, (groups, p, hh, gb) = saved
    d = x.shape[1]
    dres32, _ = dres
    dvg = _glu_bwd(f"{tag}_glu_act_bwd", dres32, vg, d)
    d_glu = _mm(f"{tag}_glu_dw", z, dvg, ta=True)
    dz = _mm(f"{tag}_glu_dx", dvg, w_glu, tb=True)
    dh, dbin_re, dbin_im, dcout_re, dcout_im, dabar_re, dabar_im, d_skip = _s5_bwd(
        f"{tag}_scan_bwd", dz, ypre, h, s_re, s_im, *mats, n_seq, seq)
    d_are, d_aim, d_logdt, d_bre, d_bim = disc_vjp((
        dabar_re.reshape(groups, p), dabar_im.reshape(groups, p),
        _block_diag_in_t(dbin_re, gb, p, hh), _block_diag_in_t(dbin_im, gb, p, hh)))
    d_cre = _block_diag_out_t(dcout_re, gb, p, hh)
    d_cim = _block_diag_out_t(dcout_im, gb, p, hh)
    dx, dg = _rms_bwd(f"{tag}_norm_bwd", x, g, dh, dres32)
    return dx, dg, (d_are, d_aim, d_logdt, d_bre, d_bim, d_cre, d_cim), d_skip, d_glu


def _xattn_fwd(tag, x, mem, g_q, g_mem, w_q, w_kv, w_o, n_seq, seq, mlen, heads):
    q, n = _mm(f"{tag}_q", x, w_q, prologue=_rms_rows, prologue_pars=[g_q])
    mem_n = _rms_fwd(f"{tag}_mem_norm", mem, g_mem)
    kv = _mm(f"{tag}_kv", mem_n, w_kv)
    o = _attn_fwd(f"{tag}_attn", q, kv, n_seq, seq, mlen, heads)
    out = _mm(f"{tag}_o", o, w_o, res=x, out_dtype=F32)
    return out, (x, n, q, mem_n, kv, o)


def _xattn_bwd(tag, dres, saved, mem, g_q, g_mem, w_q, w_kv, w_o, n_seq, seq, mlen, heads):
    x, n, q, mem_n, kv, o = saved
    dres32, dres16 = dres
    do = _mm(f"{tag}_o_dx", dres16, w_o, tb=True)
    d_o = _mm(f"{tag}_o_dw", o, dres16, ta=True)
    dq, dk, dv = _attn_bwd(f"{tag}_attn_bwd", q, kv, do, n_seq, seq, mlen, heads)
    dkv = jnp.concatenate([dk, dv], axis=1)
    d_q = _mm(f"{tag}_q_dw", n, dq, ta=True)
    d_kv = _mm(f"{tag}_kv_dw", mem_n, dkv, ta=True)
    dmem_n = _mm(f"{tag}_kv_dx", dkv, w_kv, tb=True)
    _, dg_mem = _rms_bwd(f"{tag}_mem_norm_bwd", mem, g_mem, dmem_n)
    dx, dg_q = _mm_rms_bwd(f"{tag}_q_dx", dq, w_q, x, g_q, dres32)
    return dx, dg_q, dg_mem, d_q, d_kv, d_o


WEIGHT_NAMES = ("norm_g", "final_g", "ffn1_up", "ffn1_down", "ffn2_up", "ffn2_down", "conv_w_in", "conv_w",
                "conv_w_out", "ssm_a_re", "ssm_a_im", "ssm_log_dt", "ssm_b_re", "ssm_b_im", "ssm_c_re", "ssm_c_im",
                "ssm_d", "ssm_w_glu", "xa_w_q", "xa_w_kv", "xa_w_o")
MATRICES = ("ffn1_up", "ffn1_down", "ffn2_up", "ffn2_down", "conv_w_in", "conv_w_out", "ssm_w_glu", "xa_w_q",
            "xa_w_kv", "xa_w_o")
COLUMN_SHARDED = ("conv_w_in", "ssm_w_glu", "xa_w_kv")
TRANSPOSED = ("ffn1_up", "ffn2_up")
SMALL_SHARDED = ("norm_g", "conv_w", "ssm_d")
REPLICATED = ("ssm_a_re", "ssm_a_im", "ssm_log_dt", "ssm_b_re", "ssm_b_im", "ssm_c_re", "ssm_c_im", "final_g")


def kernel(x, mem, norm_g, final_g, ffn1_up, ffn1_down, ffn2_up, ffn2_down, conv_w_in, conv_w, conv_w_out, ssm_a_re, ssm_a_im, ssm_log_dt, ssm_b_re, ssm_b_im, ssm_c_re, ssm_c_im, ssm_d, ssm_w_glu, xa_w_q, xa_w_kv, xa_w_o, loss_target, m_norm_g, m_final_g, m_ffn1_up, m_ffn1_down, m_ffn2_up, m_ffn2_down, m_conv_w_in, m_conv_w, m_conv_w_out, m_ssm_a_re, m_ssm_a_im, m_ssm_log_dt, m_ssm_b_re, m_ssm_b_im, m_ssm_c_re, m_ssm_c_im, m_ssm_d, m_ssm_w_glu, m_xa_w_q, m_xa_w_kv, m_xa_w_o, v_norm_g, v_final_g, v_ffn1_up, v_ffn1_down, v_ffn2_up, v_ffn2_down, v_conv_w_in, v_conv_w, v_conv_w_out, v_ssm_a_re, v_ssm_a_im, v_ssm_log_dt, v_ssm_b_re, v_ssm_b_im, v_ssm_c_re, v_ssm_c_im, v_ssm_d, v_ssm_w_glu, v_xa_w_q, v_xa_w_kv, v_xa_w_o):
    w = dict(norm_g=norm_g, final_g=final_g, ffn1_up=ffn1_up, ffn1_down=ffn1_down, ffn2_up=ffn2_up,
             ffn2_down=ffn2_down, conv_w_in=conv_w_in, conv_w=conv_w, conv_w_out=conv_w_out, ssm_a_re=ssm_a_re,
             ssm_a_im=ssm_a_im, ssm_log_dt=ssm_log_dt, ssm_b_re=ssm_b_re, ssm_b_im=ssm_b_im, ssm_c_re=ssm_c_re,
             ssm_c_im=ssm_c_im, ssm_d=ssm_d, ssm_w_glu=ssm_w_glu, xa_w_q=xa_w_q, xa_w_kv=xa_w_kv, xa_w_o=xa_w_o)
    mom = dict(norm_g=m_norm_g, final_g=m_final_g, ffn1_up=m_ffn1_up, ffn1_down=m_ffn1_down, ffn2_up=m_ffn2_up,
               ffn2_down=m_ffn2_down, conv_w_in=m_conv_w_in, conv_w=m_conv_w, conv_w_out=m_conv_w_out,
               ssm_a_re=m_ssm_a_re, ssm_a_im=m_ssm_a_im, ssm_log_dt=m_ssm_log_dt, ssm_b_re=m_ssm_b_re,
               ssm_b_im=m_ssm_b_im, ssm_c_re=m_ssm_c_re, ssm_c_im=m_ssm_c_im, ssm_d=m_ssm_d, ssm_w_glu=m_ssm_w_glu,
               xa_w_q=m_xa_w_q, xa_w_kv=m_xa_w_kv, xa_w_o=m_xa_w_o)
    var = dict(norm_g=v_norm_g, final_g=v_final_g, ffn1_up=v_ffn1_up, ffn1_down=v_ffn1_down, ffn2_up=v_ffn2_up,
               ffn2_down=v_ffn2_down, conv_w_in=v_conv_w_in, conv_w=v_conv_w, conv_w_out=v_conv_w_out,
               ssm_a_re=v_ssm_a_re, ssm_a_im=v_ssm_a_im, ssm_log_dt=v_ssm_log_dt, ssm_b_re=v_ssm_b_re,
               ssm_b_im=v_ssm_b_im, ssm_c_re=v_ssm_c_re, ssm_c_im=v_ssm_c_im, ssm_d=v_ssm_d, ssm_w_glu=v_ssm_w_glu,
               xa_w_q=v_xa_w_q, xa_w_kv=v_xa_w_kv, xa_w_o=v_xa_w_o)

    n_seq, seq, d = x.shape
    mlen = mem.shape[1]
    depth, n_norms = norm_g.shape[0], norm_g.shape[1]
    heads = 4
    tokens = n_seq * seq
    x2 = x.reshape(tokens, d)
    mem2 = mem.reshape(n_seq * mlen, d)
    tgt2 = loss_target.reshape(tokens, d)

    small_shapes = [w[k].shape for k in SMALL_SHARDED]
    small_rows = [_merge2d(w[k]) for k in SMALL_SHARDED]
    small_counts = [s.shape[0] for s in small_rows]
    small = jnp.concatenate(small_rows, axis=0)
    small = jnp.pad(small, [(0, (-small.shape[0]) % 8), (0, 0)])
    me = _index(*_place())

    def layer_weights(i):
        names = [(k, i) for k in ("ffn1_up", "ffn1_down", "ffn2_up", "ffn2_down", "xa_w_q", "xa_w_kv", "xa_w_o")]
        return names + ([("conv_w_in", i // 2), ("conv_w_out", i // 2)] if i % 2 == 0 else [("ssm_w_glu", i // 2)])

    shards = [[(w[k][idx].T if k in TRANSPOSED else w[k][idx]).astype(BF16) for k, idx in layer_weights(i)]
              for i in range(depth)]
    n_first = 2
    gathered = _all_gather("gather_layer0_ffn1", shards[0][:n_first] + [small])
    small_all = gathered[-1]
    in_flight = [None] * depth
    token = gathered[0]
    in_place = {k for k in COLUMN_SHARDED if w[k].shape[-1] % V7X_LANES == 0}

    def modes(i):
        start = n_first if i == 0 else 0
        return [COLUMNS if k in in_place else BLOCK for k, _ in layer_weights(i)][start:]

    for i in range(depth):
        mine = shards[i][n_first:] if i == 0 else shards[i]
        zones = [_landing(s, me, mode) for s, mode in zip(mine, modes(i))]
        *in_flight[i], token = _exchange_start(f"gather_start_l{i}", mine, zones, [WHOLE] * len(zones), modes(i),
                                               deps=[token], plan=CHIPS)
    passing = [None] * depth

    def pass_on(i, after):
        landed = _exchange_wait(f"gather_wait_l{i}", *in_flight[i], [WHOLE] * len(modes(i)), modes(i), after,
                                plan=CHIPS)
        *passing[i], forward_token = _exchange_start(f"forward_start_l{i}", [], landed, modes(i), modes(i),
                                                     plan=FORWARD)
        return forward_token

    def passed(i, after):
        return _exchange_wait(f"forward_wait_l{i}", *passing[i], modes(i), modes(i), after, plan=FORWARD)

    def small_whole(idx):
        start = sum(small_counts[:idx])
        part = small_all[:, start:start + small_counts[idx]]
        lead = small_shapes[idx][:-1]
        part = part.reshape((N_DEV,) + lead + (part.shape[-1],))
        part = jnp.moveaxis(part, 0, -2)
        return part.reshape(lead + (N_DEV * part.shape[-1],))

    norm_all = small_whole(0)
    conv_all = small_whole(1)
    dskip_all = small_whole(2)

    def whole(names, arrived):
        return {k: blk if k in in_place else _cols_whole(blk) if k in COLUMN_SHARDED else _rows_whole(blk)
                for (k, _), blk in zip(names, arrived)}

    saved = []
    cur = x2
    for i in range(depth):
        g = [norm_all[i, k].reshape(1, d) for k in range(n_norms)]
        j = i // 2
        if i == 0:
            lw = whole(layer_weights(0)[:n_first], gathered[:n_first])
        else:
            lw = whole(layer_weights(i), passed(i, cur))
        cur, s_ffn1 = _ffn_fwd(f"l{i}_ffn1", cur, g[0], lw["ffn1_up"], lw["ffn1_down"], [token] if i == 0 else ())
        if i == 0:
            lw.update(whole(layer_weights(0)[n_first:], passed(0, pass_on(0, cur))))
        if i % 2 == 0:
            lw["conv_w"] = conv_all[j]
            cur, s_mix = _conv_mixer_fwd(f"l{i}_conv", cur, g[1], lw["conv_w_in"], lw["conv_w"], lw["conv_w_out"],
                                         n_seq, seq)
        else:
            ssm = tuple(w[k][j] for k in ("ssm_a_re", "ssm_a_im", "ssm_log_dt", "ssm_b_re", "ssm_b_im",
                                          "ssm_c_re", "ssm_c_im"))
            cur, s_mix = _s5_mixer_fwd(f"l{i}_s5", cur, g[1], ssm, dskip_all[j].reshape(1, d), lw["ssm_w_glu"],
                                       n_seq, seq)
        cur, s_xa = _xattn_fwd(f"l{i}_xa", cur, mem2, g[2], g[3], lw["xa_w_q"], lw["xa_w_kv"], lw["xa_w_o"],
                               n_seq, seq, mlen, heads)
        early_pass = 0 < i < depth - 1
        deps = [pass_on(i + 1, cur)] if early_pass else ()
        cur, s_ffn2 = _ffn_fwd(f"l{i}_ffn2", cur, g[4], lw["ffn2_up"], lw["ffn2_down"], deps)
        if i == 0 and depth > 1:
            pass_on(1, cur)
        saved.append((g, lw, s_ffn1, s_mix, s_xa, s_ffn2))

    dres, err2, d_final = _final_loss("loss_head", cur, final_g.reshape(1, d), tgt2)
    loss = lax.psum(0.5 * jnp.sum(err2) / d, ("x", "y", "c"))

    d_norm = [[None] * n_norms for _ in range(depth)]
    d_conv = [None] * conv_w.shape[0]
    d_skip = [None] * ssm_d.shape[0]
    d_ssm = [None] * ssm_a_re.shape[0]
    leaving = [None] * depth
    deps = ()

    def leave(name, keys, gm, extra=()):
        srcs, src_modes, zones = [], [], []
        for k in keys:
            if k in in_place:
                rows, n = gm[k].shape[0], gm[k].shape[1] // N_DEV
                srcs.append(gm[k])
                src_modes.append(COLUMNS)
                zones.append(_landing(lax.dynamic_slice(gm[k], (0, me * n), (rows, n)), me))
            else:
                srcs.append(_cols_parts(gm[k]) if k in COLUMN_SHARDED else _rows_parts(gm[k]))
                src_modes.append(BLOCK)
                zones.append(_landing(lax.dynamic_index_in_dim(srcs[-1], me, 0, keepdims=False), me))
        for p in extra:
            srcs.append(p)
            src_modes.append(BLOCK)
            zones.append(_landing(lax.dynamic_index_in_dim(p, me, 0, keepdims=False), me))
        land_modes = [BLOCK] * len(srcs)
        *handles, token = _exchange_start(name, srcs, zones, src_modes, land_modes)
        return (*handles, src_modes, land_modes), token

    def small_parts(full):
        lead = full.shape[:-1]
        t = full.reshape(lead + (N_DEV, full.shape[-1] // N_DEV))
        t = jnp.moveaxis(t, -2, 0)
        return t.reshape(N_DEV, -1, t.shape[-1])

    for i in reversed(range(depth)):
        g, lw, s_ffn1, s_mix, s_xa, s_ffn2 = saved[i]
        j = i // 2
        gm = {}
        dres, d_norm[i][4], gm["ffn2_up"], gm["ffn2_down"] = _ffn_bwd(
            f"l{i}_ffn2", dres, s_ffn2, g[4], lw["ffn2_up"], lw["ffn2_down"], deps)
        dres, d_norm[i][2], d_norm[i][3], gm["xa_w_q"], gm["xa_w_kv"], gm["xa_w_o"] = _xattn_bwd(
            f"l{i}_xa", dres, s_xa, mem2, g[2], g[3], lw["xa_w_q"], lw["xa_w_kv"], lw["xa_w_o"], n_seq, seq, mlen,
            heads)
        if i % 2 == 0:
            dres, d_norm[i][1], gm["conv_w_in"], d_conv[j], gm["conv_w_out"] = _conv_mixer_bwd(
                f"l{i}_conv", dres, s_mix, g[1], lw["conv_w_in"], lw["conv_w"], lw["conv_w_out"], n_seq, seq)
        else:
            dres, d_norm[i][1], d_ssm[j], d_skip[j], gm["ssm_w_glu"] = _s5_mixer_bwd(
                f"l{i}_s5", dres, s_mix, g[1], lw["ssm_w_glu"], n_seq, seq)
        upper, token = leave(f"grads_start_l{i}_upper", [k for k, _ in layer_weights(i)[2:]], gm)
        lower = []

        def send_lower(d_up_t, d_down, i=i, lower=lower):
            handles, token = leave(f"grads_start_l{i}_lower", ["ffn1_up", "ffn1_down"],
                                   {"ffn1_up": d_up_t, "ffn1_down": d_down})
            lower.append(handles)
            return [token]

        dres, d_norm[i][0], _, _ = _ffn_bwd(
            f"l{i}_ffn1", dres, s_ffn1, g[0], lw["ffn1_up"], lw["ffn1_down"], [token], after_dw=send_lower)
        leaving[i] = (upper, lower[0])
        deps = ()
        if i == min(1, depth - 1):
            rep_grads = [jnp.stack([d_ssm[j][k] for j in range(len(d_ssm))]) for k in range(7)]
            rep_packed = _pack_rows(rep_grads + [d_final.reshape(-1)])
            *rep_leaving, token = _exchange_start("replicated_grads_start", [rep_packed], [_landing(rep_packed, me)],
                                                  [WHOLE], [BLOCK])
            deps = [token]
    grad_x = dres[0].reshape(n_seq, seq, d)
    d_norm_all = jnp.stack([jnp.concatenate(row, axis=0) for row in d_norm])
    small_g = jnp.concatenate(
        [small_parts(a) for a in (d_norm_all, jnp.stack(d_conv), jnp.concatenate(d_skip, axis=0))], axis=1)
    small_g = jnp.pad(small_g, [(0, 0), (0, (-small_g.shape[1]) % 8), (0, 0)])
    small_leaving, _ = leave("small_grads_start", [], {}, [small_g])
    received = {k: [None] * w[k].shape[0] for k in MATRICES}

    def arrive(i, part, after):
        names = layer_weights(i)[2:] if part == 0 else layer_weights(i)[:2]
        blks = _exchange_wait(f"grads_wait_l{i}_{'upper' if part == 0 else 'lower'}", *leaving[i][part], after)
        for (k, idx), blk in zip(names, blks):
            received[k][idx] = blk
        return [k for k, _ in names]

    for i in range(1, depth):
        arrive(i, 0, dres[0])
        arrive(i, 1, dres[0])
    rep_all = _exchange_wait("replicated_grads_wait", *rep_leaving, [WHOLE], [BLOCK], dres[0])[0]
    rep_shapes = [w[k].shape for k in REPLICATED]
    flat = {k: (_merge2d(w[k]), _merge2d(mom[k]), _merge2d(var[k])) for k in MATRICES}
    late = {k: received[k][0] is None for k in MATRICES}
    early = {}
    for k in MATRICES:
        first = 1 if late[k] else 0
        if first < len(received[k]):
            early[k] = _adamw(f"adamw_{k}_upper", received[k][first:], *flat[k], first_layer=first,
                              transposed=k in TRANSPOSED)
    after = list(early.values())[-1][0] if early else dres[0]
    done = dict(early)
    for part in (0, 1):
        for k in arrive(0, part, after):
            done[k] = _adamw(f"adamw_{k}_l0", received[k][:1], *flat[k], earlier=early.get(k),
                             transposed=k in TRANSPOSED)
            after = done[k][0]
    small_received, = _exchange_wait("small_grads_wait", *small_leaving, after)
    grads, deltas, new_m, new_v = {}, {}, {}, {}
    for k in MATRICES:
        grads[k], deltas[k], new_m[k], new_v[k] = [o.reshape(w[k].shape) for o in done[k]]

    def small_local(src):
        rows = jnp.concatenate([_merge2d(src[k]) for k in SMALL_SHARDED], axis=0)
        return jnp.pad(rows, [(0, (-rows.shape[0]) % 8), (0, 0)])

    out = _adamw("adamw_small", [small_received], small, small_local(mom), small_local(var))
    for res, o in zip((grads, deltas, new_m, new_v), out):
        start = 0
        for k, cnt, shape in zip(SMALL_SHARDED, small_counts, small_shapes):
            res[k] = o[start:start + cnt].reshape(shape)
            start += cnt

    out = _adamw("adamw_replicated", [rep_all], _pack_rows([w[k] for k in REPLICATED]),
                 _pack_rows([mom[k] for k in REPLICATED]), _pack_rows([var[k] for k in REPLICATED]))
    for res, o in zip((grads, deltas, new_m, new_v), out):
        for k, a in zip(REPLICATED, _unpack_rows(o, rep_shapes)):
            res[k] = a

    return (loss, grad_x, *[grads[k] for k in WEIGHT_NAMES], *[deltas[k] for k in WEIGHT_NAMES],
            *[new_m[k] for k in WEIGHT_NAMES], *[new_v[k] for k in WEIGHT_NAMES])
```
